```python
import jax, jax.numpy as jnp
from jax import lax
import numpy as np

D_MODEL = 1024
BATCH = 8
SEQ = 8192
DEPTH = 4

N_MIXERS = 2
EPS = 1e-6

CHUNK = 128
A_WIDTH = 2 * D_MODEL
A_GROUPS = 8
A_GROUP_DIM = A_WIDTH // A_GROUPS

B_WINDOWS = (2, 4, 8, 16)
B_GROUPS = len(B_WINDOWS)
B_WIDTH = D_MODEL
B_GROUP_DIM = B_WIDTH // B_GROUPS

D_FF = ((8 * D_MODEL + 3 * 256 - 1) // (3 * 256)) * 256

N_A_LAYERS = (DEPTH + 1) // 2
N_B_LAYERS = DEPTH // 2

kernel_name = 'hybrid_gmlp_pool_swiglu_trunk'


def rmsnorm(x, g):
    xf = x.astype(jnp.float32)
    y = xf * lax.rsqrt(jnp.mean(xf * xf, axis=-1, keepdims=True) + EPS)
    return (y * g.astype(jnp.float32)).astype(x.dtype)


def layernorm(x, g, b):
    xf = x.astype(jnp.float32)
    mu = jnp.mean(xf, axis=-1, keepdims=True)
    xc = xf - mu
    var = jnp.mean(xc * xc, axis=-1, keepdims=True)
    y = xc * lax.rsqrt(var + EPS) * g.astype(jnp.float32) + b.astype(jnp.float32)
    return y.astype(x.dtype)


def mixer_a(h, w_in, ln_g, ln_b, w_s, b_s, w_out):
    bsz, s, _ = h.shape
    z = jax.nn.gelu(h @ w_in, approximate=False)
    u, v = jnp.split(z, 2, axis=-1)
    v = layernorm(v, ln_g, ln_b)
    n_chunks = s // CHUNK
    v = v.reshape(bsz, n_chunks, CHUNK, A_GROUPS, A_GROUP_DIM)
    u = u.reshape(bsz, n_chunks, CHUNK, A_GROUPS, A_GROUP_DIM)
    causal = jnp.tril(jnp.ones((CHUNK, CHUNK), dtype=bool))
    w = jnp.where(causal[None], w_s, jnp.zeros_like(w_s))
    sv = jnp.einsum('gts,bnsgd->bntgd', w, v)
    sv = sv + jnp.transpose(b_s)[None, None, :, :, None]
    gated = (u * sv).reshape(bsz, s, A_WIDTH)
    return gated @ w_out


def mixer_b(h, w_in, w_grp, scale, w_out):
    bsz, s, _ = h.shape
    p = h @ w_in
    pf = p.astype(jnp.float32)
    cs = jnp.cumsum(pf, axis=1)
    cs0 = jnp.concatenate([jnp.zeros((bsz, 1, B_WIDTH), jnp.float32), cs], axis=1)
    t = jnp.arange(s)
    pooled = []
    for g, win in enumerate(B_WINDOWS):
        lo, hi = g * B_GROUP_DIM, (g + 1) * B_GROUP_DIM
        c = cs0[..., lo:hi]
        c_pad = jnp.concatenate([jnp.zeros((bsz, win - 1, B_GROUP_DIM), jnp.float32), c], axis=1)
        total = c[:, 1:] - c_pad[:, :s]
        count = jnp.minimum(t + 1, win).astype(jnp.float32)
        pooled.append(total / count[None, :, None] - pf[..., lo:hi])
    pooled = jnp.stack(pooled, axis=2)
    mixed = jnp.einsum('bsgd,gde->bsge', pooled, w_grp.astype(jnp.float32))
    mixed = mixed.reshape(bsz, s, B_WIDTH) * scale.astype(jnp.float32)
    return mixed.astype(h.dtype) @ w_out


def swiglu(h, w_gate, w_up, w_down):
    return (jax.nn.silu(h @ w_gate) * (h @ w_up)) @ w_down


def _fwd_setup_inputs(seed: int = 0) -> dict:
    key = jax.random.key(seed)
    ks = jax.random.split(key, 20)
    f32 = jnp.float32
    d = D_MODEL
    x = jax.random.normal(ks[0], (BATCH, SEQ, d), f32)
    a_w_in = jax.random.normal(ks[1], (N_A_LAYERS, d, 2 * A_WIDTH), f32) * d ** -0.5
    a_ln_g = 1.0 + 0.02 * jax.random.normal(ks[2], (N_A_LAYERS, A_WIDTH), f32)
    a_ln_b = 0.02 * jax.random.normal(ks[3], (N_A_LAYERS, A_WIDTH), f32)
    a_w_s = jnp.tril(jax.random.normal(ks[4], (N_A_LAYERS, A_GROUPS, CHUNK, CHUNK), f32) * CHUNK ** -0.5)
    a_b_s = 1.0 + 0.1 * jax.random.normal(ks[5], (N_A_LAYERS, A_GROUPS, CHUNK), f32)
    a_w_out = jax.random.normal(ks[6], (N_A_LAYERS, A_WIDTH, d), f32) * A_WIDTH ** -0.5
    b_w_in = jax.random.normal(ks[7], (N_B_LAYERS, d, B_WIDTH), f32) * d ** -0.5
    b_w_grp = jax.random.normal(ks[8], (N_B_LAYERS, B_GROUPS, B_GROUP_DIM, B_GROUP_DIM), f32) * B_GROUP_DIM ** -0.5
    b_scale = 1.0 + 0.1 * jax.random.normal(ks[9], (N_B_LAYERS, B_WIDTH), f32)
    b_w_out = jax.random.normal(ks[10], (N_B_LAYERS, B_WIDTH, d), f32) * B_WIDTH ** -0.5
    mix_pre_g = 1.0 + 0.02 * jax.random.normal(ks[11], (DEPTH, d), f32)
    mix_post_g = 1.0 + 0.02 * jax.random.normal(ks[12], (DEPTH, d), f32)
    ffn_pre_g = 1.0 + 0.02 * jax.random.normal(ks[13], (DEPTH, d), f32)
    ffn_post_g = 1.0 + 0.02 * jax.random.normal(ks[14], (DEPTH, d), f32)
    ffn_w_gate = jax.random.normal(ks[15], (DEPTH, d, D_FF), f32) * d ** -0.5
    ffn_w_up = jax.random.normal(ks[16], (DEPTH, d, D_FF), f32) * d ** -0.5
    ffn_w_down = jax.random.normal(ks[17], (DEPTH, D_FF, d), f32) * D_FF ** -0.5
    return {'x': x, 'a_w_in': a_w_in, 'a_ln_g': a_ln_g, 'a_ln_b': a_ln_b,
            'a_w_s': a_w_s, 'a_b_s': a_b_s, 'a_w_out': a_w_out,
            'b_w_in': b_w_in, 'b_w_grp': b_w_grp, 'b_scale': b_scale, 'b_w_out': b_w_out,
            'mix_pre_g': mix_pre_g, 'mix_post_g': mix_post_g,
            'ffn_pre_g': ffn_pre_g, 'ffn_post_g': ffn_post_g,
            'ffn_w_gate': ffn_w_gate, 'ffn_w_up': ffn_w_up, 'ffn_w_down': ffn_w_down}


def _fwd_reference(x, a_w_in, a_ln_g, a_ln_b, a_w_s, a_b_s, a_w_out,
              b_w_in, b_w_grp, b_scale, b_w_out,
              mix_pre_g, mix_post_g, ffn_pre_g, ffn_post_g,
              ffn_w_gate, ffn_w_up, ffn_w_down):
    for i in range(DEPTH):
        j = i // N_MIXERS
        h = rmsnorm(x, mix_pre_g[i])
        if i % N_MIXERS == 0:
            m = mixer_a(h, a_w_in[j], a_ln_g[j], a_ln_b[j], a_w_s[j], a_b_s[j], a_w_out[j])
        else:
            m = mixer_b(h, b_w_in[j], b_w_grp[j], b_scale[j], b_w_out[j])
        x = x + rmsnorm(m, mix_post_g[i])
        h = rmsnorm(x, ffn_pre_g[i])
        f = swiglu(h, ffn_w_gate[i], ffn_w_up[i], ffn_w_down[i])
        x = x + rmsnorm(f, ffn_post_g[i])
    return x


import jax as _jax
import jax.numpy as _jnp

TWIN_FORMAT = 'train_step'
FWD_PARAMS = ['x', 'a_w_in', 'a_ln_g', 'a_ln_b', 'a_w_s', 'a_b_s', 'a_w_out', 'b_w_in', 'b_w_grp', 'b_scale', 'b_w_out', 'mix_pre_g', 'mix_post_g', 'ffn_pre_g', 'ffn_post_g', 'ffn_w_gate', 'ffn_w_up', 'ffn_w_down']
TWIN_WEIGHTS = ['a_w_in', 'a_ln_g', 'a_ln_b', 'a_w_s', 'a_b_s', 'a_w_out', 'b_w_in', 'b_w_grp', 'b_scale', 'b_w_out', 'mix_pre_g', 'mix_post_g', 'ffn_pre_g', 'ffn_post_g', 'ffn_w_gate', 'ffn_w_up', 'ffn_w_down']
TWIN_DIFF_INPUT = 'x'
TWIN_INPUTS = ['x', 'a_w_in', 'a_ln_g', 'a_ln_b', 'a_w_s', 'a_b_s', 'a_w_out', 'b_w_in', 'b_w_grp', 'b_scale', 'b_w_out', 'mix_pre_g', 'mix_post_g', 'ffn_pre_g', 'ffn_post_g', 'ffn_w_gate', 'ffn_w_up', 'ffn_w_down', 'loss_target', 'm_a_w_in', 'm_a_ln_g', 'm_a_ln_b', 'm_a_w_s', 'm_a_b_s', 'm_a_w_out', 'm_b_w_in', 'm_b_w_grp', 'm_b_scale', 'm_b_w_out', 'm_mix_pre_g', 'm_mix_post_g', 'm_ffn_pre_g', 'm_ffn_post_g', 'm_ffn_w_gate', 'm_ffn_w_up', 'm_ffn_w_down', 'v_a_w_in', 'v_a_ln_g', 'v_a_ln_b', 'v_a_w_s', 'v_a_b_s', 'v_a_w_out', 'v_b_w_in', 'v_b_w_grp', 'v_b_scale', 'v_b_w_out', 'v_mix_pre_g', 'v_mix_post_g', 'v_ffn_pre_g', 'v_ffn_post_g', 'v_ffn_w_gate', 'v_ffn_w_up', 'v_ffn_w_down']
TWIN_OUTPUTS = ['loss', 'grad_x', 'grad_a_w_in', 'grad_a_ln_g', 'grad_a_ln_b', 'grad_a_w_s', 'grad_a_b_s', 'grad_a_w_out', 'grad_b_w_in', 'grad_b_w_grp', 'grad_b_scale', 'grad_b_w_out', 'grad_mix_pre_g', 'grad_mix_post_g', 'grad_ffn_pre_g', 'grad_ffn_post_g', 'grad_ffn_w_gate', 'grad_ffn_w_up', 'grad_ffn_w_down', 'delta_a_w_in', 'delta_a_ln_g', 'delta_a_ln_b', 'delta_a_w_s', 'delta_a_b_s', 'delta_a_w_out', 'delta_b_w_in', 'delta_b_w_grp', 'delta_b_scale', 'delta_b_w_out', 'delta_mix_pre_g', 'delta_mix_post_g', 'delta_ffn_pre_g', 'delta_ffn_post_g', 'delta_ffn_w_gate', 'delta_ffn_w_up', 'delta_ffn_w_down', 'new_m_a_w_in', 'new_m_a_ln_g', 'new_m_a_ln_b', 'new_m_a_w_s', 'new_m_a_b_s', 'new_m_a_w_out', 'new_m_b_w_in', 'new_m_b_w_grp', 'new_m_b_scale', 'new_m_b_w_out', 'new_m_mix_pre_g', 'new_m_mix_post_g', 'new_m_ffn_pre_g', 'new_m_ffn_post_g', 'new_m_ffn_w_gate', 'new_m_ffn_w_up', 'new_m_ffn_w_down', 'new_v_a_w_in', 'new_v_a_ln_g', 'new_v_a_ln_b', 'new_v_a_w_s', 'new_v_a_b_s', 'new_v_a_w_out', 'new_v_b_w_in', 'new_v_b_w_grp', 'new_v_b_scale', 'new_v_b_w_out', 'new_v_mix_pre_g', 'new_v_mix_post_g', 'new_v_ffn_pre_g', 'new_v_ffn_post_g', 'new_v_ffn_w_gate', 'new_v_ffn_w_up', 'new_v_ffn_w_down']
TWIN_LEAF_KINDS = {'loss': 'loss', 'grad_x': 'grad_x', 'grad_a_w_in': 'grad_w', 'grad_a_ln_g': 'grad_w', 'grad_a_ln_b': 'grad_w', 'grad_a_w_s': 'grad_w', 'grad_a_b_s': 'grad_w', 'grad_a_w_out': 'grad_w', 'grad_b_w_in': 'grad_w', 'grad_b_w_grp': 'grad_w', 'grad_b_scale': 'grad_w', 'grad_b_w_out': 'grad_w', 'grad_mix_pre_g': 'grad_w', 'grad_mix_post_g': 'grad_w', 'grad_ffn_pre_g': 'grad_w', 'grad_ffn_post_g': 'grad_w', 'grad_ffn_w_gate': 'grad_w', 'grad_ffn_w_up': 'grad_w', 'grad_ffn_w_down': 'grad_w', 'delta_a_w_in': 'delta_w', 'delta_a_ln_g': 'delta_w', 'delta_a_ln_b': 'delta_w', 'delta_a_w_s': 'delta_w', 'delta_a_b_s': 'delta_w', 'delta_a_w_out': 'delta_w', 'delta_b_w_in': 'delta_w', 'delta_b_w_grp': 'delta_w', 'delta_b_scale': 'delta_w', 'delta_b_w_out': 'delta_w', 'delta_mix_pre_g': 'delta_w', 'delta_mix_post_g': 'delta_w', 'delta_ffn_pre_g': 'delta_w', 'delta_ffn_post_g': 'delta_w', 'delta_ffn_w_gate': 'delta_w', 'delta_ffn_w_up': 'delta_w', 'delta_ffn_w_down': 'delta_w', 'new_m_a_w_in': 'new_m', 'new_m_a_ln_g': 'new_m', 'new_m_a_ln_b': 'new_m', 'new_m_a_w_s': 'new_m', 'new_m_a_b_s': 'new_m', 'new_m_a_w_out': 'new_m', 'new_m_b_w_in': 'new_m', 'new_m_b_w_grp': 'new_m', 'new_m_b_scale': 'new_m', 'new_m_b_w_out': 'new_m', 'new_m_mix_pre_g': 'new_m', 'new_m_mix_post_g': 'new_m', 'new_m_ffn_pre_g': 'new_m', 'new_m_ffn_post_g': 'new_m', 'new_m_ffn_w_gate': 'new_m', 'new_m_ffn_w_up': 'new_m', 'new_m_ffn_w_down': 'new_m', 'new_v_a_w_in': 'new_v', 'new_v_a_ln_g': 'new_v', 'new_v_a_ln_b': 'new_v', 'new_v_a_w_s': 'new_v', 'new_v_a_b_s': 'new_v', 'new_v_a_w_out': 'new_v', 'new_v_b_w_in': 'new_v', 'new_v_b_w_grp': 'new_v', 'new_v_b_scale': 'new_v', 'new_v_b_w_out': 'new_v', 'new_v_mix_pre_g': 'new_v', 'new_v_mix_post_g': 'new_v', 'new_v_ffn_pre_g': 'new_v', 'new_v_ffn_post_g': 'new_v', 'new_v_ffn_w_gate': 'new_v', 'new_v_ffn_w_up': 'new_v', 'new_v_ffn_w_down': 'new_v'}


def _forward(args):
    return _fwd_reference(*[args[k] for k in FWD_PARAMS])


def _output_shape():
    def fwd():
        inp = _fwd_setup_inputs(0)
        return _fwd_reference(*[inp[k] for k in FWD_PARAMS])
    out = _jax.eval_shape(fwd)
    return out.shape, out.dtype

N_MICROBATCH = 1
ADAM_LR = 0.001
ADAM_B1 = 0.9
ADAM_B2 = 0.999
ADAM_EPS = 1e-08
ADAM_WD = 0.01
ADAM_STEP = 10
PER_EXAMPLE_BATCH_AXIS = {'x': 0, 'loss_target': 0}
SHARED_INPUTS = []
_WEIGHT_DTYPES = {'a_w_in': _jnp.float32, 'a_ln_g': _jnp.float32, 'a_ln_b': _jnp.float32, 'a_w_s': _jnp.float32, 'a_b_s': _jnp.float32, 'a_w_out': _jnp.float32, 'b_w_in': _jnp.float32, 'b_w_grp': _jnp.float32, 'b_scale': _jnp.float32, 'b_w_out': _jnp.float32, 'mix_pre_g': _jnp.float32, 'mix_post_g': _jnp.float32, 'ffn_pre_g': _jnp.float32, 'ffn_post_g': _jnp.float32, 'ffn_w_gate': _jnp.float32, 'ffn_w_up': _jnp.float32, 'ffn_w_down': _jnp.float32}
MOMENT_SCALE = {'a_w_in': 1.677000e+00, 'a_ln_g': 7.399564e-01, 'a_ln_b': 7.878988e-01, 'a_w_s': 1.030985e+00, 'a_b_s': 1.564064e+00, 'a_w_out': 9.823950e+00, 'b_w_in': 4.260535e+00, 'b_w_grp': 4.408197e+00, 'b_scale': 4.693024e+00, 'b_w_out': 4.550808e+00, 'mix_pre_g': 3.552997e+00, 'mix_post_g': 6.451676e+01, 'ffn_pre_g': 3.710792e+00, 'ffn_post_g': 6.376990e+01, 'ffn_w_gate': 1.178576e+00, 'ffn_w_up': 1.906724e+00, 'ffn_w_down': 3.272167e+00}


def _to_microbatches(a, axis):
    t = _jnp.moveaxis(a, axis, 0)
    t = t.reshape((N_MICROBATCH, t.shape[0] // N_MICROBATCH) + t.shape[1:])
    return _jnp.moveaxis(t, 1, axis + 1)


def setup_inputs(seed: int = 0) -> dict:
    inp = _fwd_setup_inputs(seed)
    key = _jax.random.fold_in(_jax.random.key(seed), 7919)
    shape, _ = _output_shape()
    out = dict(inp)
    out["loss_target"] = _jax.random.normal(_jax.random.fold_in(key, 0), shape, _jnp.float32)
    for i, name in enumerate(TWIN_WEIGHTS):
        w = inp[name].astype(_jnp.float32)
        if MOMENT_SCALE is None:
            s = _jnp.sqrt(_jnp.mean(_jnp.square(w)) + 1e-30)
        else:
            s = MOMENT_SCALE[name]
        km, kv = _jax.random.split(_jax.random.fold_in(key, i + 1))
        out[name] = w
        out["m_" + name] = s * _jax.random.normal(km, w.shape, _jnp.float32)
        out["v_" + name] = (s * s) * _jax.random.uniform(kv, w.shape, _jnp.float32, 0.5, 1.5)
    if N_MICROBATCH > 1:
        for name, axis in PER_EXAMPLE_BATCH_AXIS.items():
            out[name] = _to_microbatches(out[name], axis)
    return {'x': out['x'], 'a_w_in': out['a_w_in'], 'a_ln_g': out['a_ln_g'], 'a_ln_b': out['a_ln_b'], 'a_w_s': out['a_w_s'], 'a_b_s': out['a_b_s'], 'a_w_out': out['a_w_out'], 'b_w_in': out['b_w_in'], 'b_w_grp': out['b_w_grp'], 'b_scale': out['b_scale'], 'b_w_out': out['b_w_out'], 'mix_pre_g': out['mix_pre_g'], 'mix_post_g': out['mix_post_g'], 'ffn_pre_g': out['ffn_pre_g'], 'ffn_post_g': out['ffn_post_g'], 'ffn_w_gate': out['ffn_w_gate'], 'ffn_w_up': out['ffn_w_up'], 'ffn_w_down': out['ffn_w_down'], 'loss_target': out['loss_target'], 'm_a_w_in': out['m_a_w_in'], 'm_a_ln_g': out['m_a_ln_g'], 'm_a_ln_b': out['m_a_ln_b'], 'm_a_w_s': out['m_a_w_s'], 'm_a_b_s': out['m_a_b_s'], 'm_a_w_out': out['m_a_w_out'], 'm_b_w_in': out['m_b_w_in'], 'm_b_w_grp': out['m_b_w_grp'], 'm_b_scale': out['m_b_scale'], 'm_b_w_out': out['m_b_w_out'], 'm_mix_pre_g': out['m_mix_pre_g'], 'm_mix_post_g': out['m_mix_post_g'], 'm_ffn_pre_g': out['m_ffn_pre_g'], 'm_ffn_post_g': out['m_ffn_post_g'], 'm_ffn_w_gate': out['m_ffn_w_gate'], 'm_ffn_w_up': out['m_ffn_w_up'], 'm_ffn_w_down': out['m_ffn_w_down'], 'v_a_w_in': out['v_a_w_in'], 'v_a_ln_g': out['v_a_ln_g'], 'v_a_ln_b': out['v_a_ln_b'], 'v_a_w_s': out['v_a_w_s'], 'v_a_b_s': out['v_a_b_s'], 'v_a_w_out': out['v_a_w_out'], 'v_b_w_in': out['v_b_w_in'], 'v_b_w_grp': out['v_b_w_grp'], 'v_b_scale': out['v_b_scale'], 'v_b_w_out': out['v_b_w_out'], 'v_mix_pre_g': out['v_mix_pre_g'], 'v_mix_post_g': out['v_mix_post_g'], 'v_ffn_pre_g': out['v_ffn_pre_g'], 'v_ffn_post_g': out['v_ffn_post_g'], 'v_ffn_w_gate': out['v_ffn_w_gate'], 'v_ffn_w_up': out['v_ffn_w_up'], 'v_ffn_w_down': out['v_ffn_w_down']}


def _loss(weights, diff, rest, loss_target):
    with _jax.named_scope("forward"):
        args = {**rest, TWIN_DIFF_INPUT: diff, **{k: w.astype(_WEIGHT_DTYPES[k]) for k, w in weights.items()}}
        y = _forward(args)
    with _jax.named_scope("loss_head"):
        err = _jnp.square(y.astype(_jnp.float32) - loss_target)
        return 0.5 * _jnp.sum(_jnp.mean(err, axis=-1)) if err.ndim else 0.5 * err


def _adamw(w, g, m, v):
    m = ADAM_B1 * m + (1.0 - ADAM_B1) * g
    v = ADAM_B2 * v + (1.0 - ADAM_B2) * _jnp.square(g)
    m_hat = m / (1.0 - ADAM_B1 ** ADAM_STEP)
    v_hat = v / (1.0 - ADAM_B2 ** ADAM_STEP)
    delta = -ADAM_LR * (m_hat / (_jnp.sqrt(v_hat) + ADAM_EPS) + ADAM_WD * w)
    return delta, m, v


def reference(x, a_w_in, a_ln_g, a_ln_b, a_w_s, a_b_s, a_w_out, b_w_in, b_w_grp, b_scale, b_w_out, mix_pre_g, mix_post_g, ffn_pre_g, ffn_post_g, ffn_w_gate, ffn_w_up, ffn_w_down, loss_target, m_a_w_in, m_a_ln_g, m_a_ln_b, m_a_w_s, m_a_b_s, m_a_w_out, m_b_w_in, m_b_w_grp, m_b_scale, m_b_w_out, m_mix_pre_g, m_mix_post_g, m_ffn_pre_g, m_ffn_post_g, m_ffn_w_gate, m_ffn_w_up, m_ffn_w_down, v_a_w_in, v_a_ln_g, v_a_ln_b, v_a_w_s, v_a_b_s, v_a_w_out, v_b_w_in, v_b_w_grp, v_b_scale, v_b_w_out, v_mix_pre_g, v_mix_post_g, v_ffn_pre_g, v_ffn_post_g, v_ffn_w_gate, v_ffn_w_up, v_ffn_w_down):
    given = dict(x=x, a_w_in=a_w_in, a_ln_g=a_ln_g, a_ln_b=a_ln_b, a_w_s=a_w_s, a_b_s=a_b_s, a_w_out=a_w_out, b_w_in=b_w_in, b_w_grp=b_w_grp, b_scale=b_scale, b_w_out=b_w_out, mix_pre_g=mix_pre_g, mix_post_g=mix_post_g, ffn_pre_g=ffn_pre_g, ffn_post_g=ffn_post_g, ffn_w_gate=ffn_w_gate, ffn_w_up=ffn_w_up, ffn_w_down=ffn_w_down, loss_target=loss_target, m_a_w_in=m_a_w_in, m_a_ln_g=m_a_ln_g, m_a_ln_b=m_a_ln_b, m_a_w_s=m_a_w_s, m_a_b_s=m_a_b_s, m_a_w_out=m_a_w_out, m_b_w_in=m_b_w_in, m_b_w_grp=m_b_w_grp, m_b_scale=m_b_scale, m_b_w_out=m_b_w_out, m_mix_pre_g=m_mix_pre_g, m_mix_post_g=m_mix_post_g, m_ffn_pre_g=m_ffn_pre_g, m_ffn_post_g=m_ffn_post_g, m_ffn_w_gate=m_ffn_w_gate, m_ffn_w_up=m_ffn_w_up, m_ffn_w_down=m_ffn_w_down, v_a_w_in=v_a_w_in, v_a_ln_g=v_a_ln_g, v_a_ln_b=v_a_ln_b, v_a_w_s=v_a_w_s, v_a_b_s=v_a_b_s, v_a_w_out=v_a_w_out, v_b_w_in=v_b_w_in, v_b_w_grp=v_b_w_grp, v_b_scale=v_b_scale, v_b_w_out=v_b_w_out, v_mix_pre_g=v_mix_pre_g, v_mix_post_g=v_mix_post_g, v_ffn_pre_g=v_ffn_pre_g, v_ffn_post_g=v_ffn_post_g, v_ffn_w_gate=v_ffn_w_gate, v_ffn_w_up=v_ffn_w_up, v_ffn_w_down=v_ffn_w_down)
    weights = {n: given[n] for n in TWIN_WEIGHTS}
    shared = {n: given[n] for n in SHARED_INPUTS}
    per_example = {n: given[n] for n in ['x']}
    grad_fn = _jax.value_and_grad(_loss, argnums=(0, 1))

    def one_microbatch(ex, loss_target):
        ex = dict(ex)
        diff = ex.pop(TWIN_DIFF_INPUT)
        return grad_fn(weights, diff, {**shared, **ex}, loss_target)

    if N_MICROBATCH == 1:
        loss, (grad_w, grad_x) = one_microbatch(per_example, given["loss_target"])
    else:
        def body(carry, xs):
            loss_sum, grad_sum = carry
            l_k, (gw_k, gx_k) = one_microbatch(xs[0], xs[1])
            with _jax.named_scope("update"):
                return (loss_sum + l_k, _jax.tree.map(_jnp.add, grad_sum, gw_k)), gx_k

        init = (_jnp.zeros((), _jnp.float32), _jax.tree.map(_jnp.zeros_like, weights))
        (loss, grad_w), grad_x = _jax.lax.scan(body, init, (per_example, given["loss_target"]))
    with _jax.named_scope("update"):
        delta_w, new_m, new_v = {}, {}, {}
        for n in TWIN_WEIGHTS:
            delta_w[n], new_m[n], new_v[n] = _adamw(weights[n], grad_w[n], given["m_" + n], given["v_" + n])
    return (loss, grad_x, *[grad_w[n] for n in TWIN_WEIGHTS], *[delta_w[n] for n in TWIN_WEIGHTS],
            *[new_m[n] for n in TWIN_WEIGHTS], *[new_v[n] for n in TWIN_WEIGHTS])
```

```python
import functools
import math

import jax
import jax.numpy as jnp
from jax import lax
from jax.experimental import pallas as pl
from jax.experimental.pallas import tpu as pltpu

F32 = jnp.float32
BF16 = jnp.bfloat16
MESH = pl.DeviceIdType.MESH
ANY = pl.BlockSpec(memory_space=pl.ANY)

N_DEV = 8
EPS = 1e-6
CHUNK = 128
A_GROUPS = 8
A_GROUP_DIM = 256
B_WINDOWS = (2, 4, 8, 16)
B_GROUP_DIM = 256
HALO = 16
DEPTH = 4

ADAM_LR = 0.001
ADAM_B1 = 0.9
ADAM_B2 = 0.999
ADAM_EPS = 1e-08
ADAM_WD = 0.01
ADAM_STEP = 10

VMEM_LIMIT_BYTES = 60 * 1024 * 1024

ERF_P = 0.3275911
ERF_A = (0.254829592, -0.284496736, 1.421413741, -1.453152027, 1.061405429)
INV_SQRT2 = 1.0 / math.sqrt(2.0)
INV_SQRT_2PI = 1.0 / math.sqrt(2.0 * math.pi)


def _call(body, **kw):
    return pl.pallas_call(body, **kw)


def _params(*semantics):
    return pltpu.CompilerParams(dimension_semantics=semantics or None, vmem_limit_bytes=VMEM_LIMIT_BYTES)


def _resident(shape, index):
    return pl.BlockSpec(shape, lambda *_: index, pipeline_mode=pl.Buffered(1))


def _rows(tm, width):
    return pl.BlockSpec((tm, width), lambda i: (i, 0))


def _nn(a, b):
    return jnp.dot(a, b, preferred_element_type=F32)


def _nt(a, b):
    return lax.dot_general(a, b, (((1,), (1,)), ((), ())), preferred_element_type=F32)


def _tn(a, b):
    return lax.dot_general(a, b, (((0,), (0,)), ((), ())), preferred_element_type=F32)


def _rms_fwd(x, g):
    r = lax.rsqrt(jnp.mean(x * x, axis=-1, keepdims=True) + EPS)
    return x * r * g


def _rms_bwd(x, g, dy):
    r = lax.rsqrt(jnp.mean(x * x, axis=-1, keepdims=True) + EPS)
    xh = x * r
    dg = jnp.sum(dy * xh, axis=0, keepdims=True)
    dxh = dy * g
    dx = r * (dxh - xh * jnp.mean(dxh * xh, axis=-1, keepdims=True))
    return dx, dg


def _gelu(z):
    a = jnp.abs(z) * INV_SQRT2
    t = 1.0 / (1.0 + ERF_P * a)
    e = jnp.exp(-a * a)
    poly = t * (ERF_A[0] + t * (ERF_A[1] + t * (ERF_A[2] + t * (ERF_A[3] + t * ERF_A[4]))))
    half = 0.5 * poly * e
    phi = jnp.where(z >= 0, 1.0 - half, half)
    return z * phi, phi + z * e * INV_SQRT_2PI


def _layernorm_stats(v):
    mu = jnp.mean(v, axis=-1, keepdims=True)
    xc = v - mu
    rs = lax.rsqrt(jnp.mean(xc * xc, axis=-1, keepdims=True) + EPS)
    return xc * rs, rs


def _tril_mask():
    r = lax.broadcasted_iota(jnp.int32, (CHUNK, CHUNK), 0)
    c = lax.broadcasted_iota(jnp.int32, (CHUNK, CHUNK), 1)
    return r >= c


def _two_d(ref):
    k, r, d = ref.shape
    return ref[...].reshape(k * r, d)


class _Layout:
    def __init__(self, d):
        self.ffn_rows = 384
        self.gate = [self.ffn_rows * l for l in range(DEPTH)]
        self.up = [self.ffn_rows * (DEPTH + l) for l in range(DEPTH)]
        self.down = [self.ffn_rows * (2 * DEPTH + l) for l in range(DEPTH)]
        base = self.ffn_rows * 3 * DEPTH
        self.a_in_rows, self.a_out_rows, self.b_rows = 4 * d // N_DEV, 2 * d // N_DEV, d // N_DEV
        self.a_in = [base + self.a_in_rows * j for j in range(2)]
        base += 2 * self.a_in_rows
        self.a_out = [base + self.a_out_rows * j for j in range(2)]
        base += 2 * self.a_out_rows
        self.b_in = [base + self.b_rows * j for j in range(2)]
        base += 2 * self.b_rows
        self.b_out = [base + self.b_rows * j for j in range(2)]
        self.rows = base + 2 * self.b_rows


def _wspec(rows, off, d):
    assert off % rows == 0
    return _resident((N_DEV, rows, d), (0, off // rows, 0))


def _a_fwd(x, gpre, wg, lay, j, lng, lnb, ws, bst, gpost, tm, name):
    t, d = x.shape
    aw = 2 * d
    nch = tm // CHUNK

    def body(x_ref, gpre_ref, win_ref, lng_ref, lnb_ref, ws_ref, bst_ref, wout_ref, gpost_ref,
             x1_ref, h1_ref, z_ref, gated_ref, m_ref):
        xv = x_ref[...]
        h1 = _rms_fwd(xv, gpre_ref[...]).astype(BF16)
        h1_ref[...] = h1
        z = _nt(h1, _two_d(win_ref))
        z_ref[...] = z.astype(BF16)
        u, _ = _gelu(z[:, :aw])
        v, _ = _gelu(z[:, aw:])
        vh, _ = _layernorm_stats(v)
        vn = (vh * lng_ref[...] + lnb_ref[...]).astype(BF16)
        mask = _tril_mask()
        for g in range(A_GROUPS):
            wm = jnp.where(mask, ws_ref[g], 0.0).astype(BF16)
            cols = slice(g * A_GROUP_DIM, (g + 1) * A_GROUP_DIM)
            for c in range(nch):
                rows = slice(c * CHUNK, (c + 1) * CHUNK)
                sv = _nn(wm, vn[rows, cols]) + bst_ref[:, g:g + 1]
                gated_ref[rows, cols] = (u[rows, cols] * sv).astype(BF16)
        m = _nn(gated_ref[...], _two_d(wout_ref))
        m_ref[...] = m
        x1_ref[...] = xv + _rms_fwd(m, gpost_ref[...])

    vec = lambda w: _resident((1, w), (0, 0))
    return _call(
        body, name=name, grid=(t // tm,),
        in_specs=[_rows(tm, d), vec(d), _wspec(lay.a_in_rows, lay.a_in[j], d), vec(aw), vec(aw),
                  _resident((A_GROUPS, CHUNK, CHUNK), (0, 0, 0)), _resident((CHUNK, A_GROUPS), (0, 0)),
                  _wspec(lay.a_out_rows, lay.a_out[j], d), vec(d)],
        out_specs=[_rows(tm, d), _rows(tm, d), _rows(tm, 2 * aw), _rows(tm, aw), _rows(tm, d)],
        out_shape=[jax.ShapeDtypeStruct((t, d), F32), jax.ShapeDtypeStruct((t, d), BF16),
                   jax.ShapeDtypeStruct((t, 2 * aw), BF16), jax.ShapeDtypeStruct((t, aw), BF16),
                   jax.ShapeDtypeStruct((t, d), F32)],
        compiler_params=_params("parallel"),
    )(x, gpre, wg, lng, lnb, ws, bst, wg, gpost)


def _a_bwd(dx1, m, x, z, gpre, wg, lay, j, lng, lnb, ws, bst, gpost, tm, name):
    t, d = x.shape
    aw = 2 * d
    nch = tm // CHUNK

    def body(dx1_ref, m_ref, x_ref, z_ref, gpre_ref, win_ref, lng_ref, lnb_ref, ws_ref, bst_ref, wout_ref, gpost_ref,
             dx_ref, dm_ref, dz_ref, dgpost_ref, dgpre_ref, dlng_ref, dlnb_ref, dws_ref, dbt_ref, dvn_ref):
        @pl.when(pl.program_id(0) == 0)
        def _():
            for r in (dgpost_ref, dgpre_ref, dlng_ref, dlnb_ref, dws_ref, dbt_ref):
                r[...] = jnp.zeros_like(r)

        dy = dx1_ref[...]
        dm, dgpost = _rms_bwd(m_ref[...], gpost_ref[...], dy)
        dgpost_ref[...] += dgpost
        dm_bf = dm.astype(BF16)
        dm_ref[...] = dm_bf
        dgated = _nt(dm_bf, _two_d(wout_ref))

        zf = z_ref[...].astype(F32)
        u, du_dz = _gelu(zf[:, :aw])
        v, dv_dz = _gelu(zf[:, aw:])
        vh, rs = _layernorm_stats(v)
        lng_v = lng_ref[...]
        vn = (vh * lng_v + lnb_ref[...]).astype(BF16)
        mask = _tril_mask()
        lane = lax.broadcasted_iota(jnp.int32, (CHUNK, CHUNK), 1)
        for g in range(A_GROUPS):
            wm = jnp.where(mask, ws_ref[g], 0.0).astype(BF16)
            cols = slice(g * A_GROUP_DIM, (g + 1) * A_GROUP_DIM)
            dws_g = jnp.zeros((CHUNK, CHUNK), F32)
            db_g = jnp.zeros((CHUNK, 1), F32)
            for c in range(nch):
                rows = slice(c * CHUNK, (c + 1) * CHUNK)
                vn_cg = vn[rows, cols]
                sv = _nn(wm, vn_cg) + bst_ref[:, g:g + 1]
                dg_cg = dgated[rows, cols]
                dsv = dg_cg * u[rows, cols]
                dsv_bf = dsv.astype(BF16)
                db_g = db_g + jnp.sum(dsv, axis=1, keepdims=True)
                dws_g = dws_g + _nt(dsv_bf, vn_cg)
                dvn_ref[rows, cols] = _tn(wm, dsv_bf)
                dz_ref[rows, cols] = (dg_cg * sv * du_dz[rows, cols]).astype(BF16)
            dws_ref[g] += jnp.where(mask, dws_g, 0.0)
            dbt_ref[...] += jnp.where(lane == g, db_g, 0.0)
        dvn = dvn_ref[...]
        dlng_ref[...] += jnp.sum(dvn * vh, axis=0, keepdims=True)
        dlnb_ref[...] += jnp.sum(dvn, axis=0, keepdims=True)
        dvh = dvn * lng_v
        dv = rs * (dvh - jnp.mean(dvh, axis=-1, keepdims=True) - vh * jnp.mean(dvh * vh, axis=-1, keepdims=True))
        dz_ref[:, aw:] = (dv * dv_dz).astype(BF16)
        dh1 = _nn(dz_ref[...], _two_d(win_ref))
        dxp, dgpre = _rms_bwd(x_ref[...], gpre_ref[...], dh1)
        dgpre_ref[...] += dgpre
        dx_ref[...] = dy + dxp

    vec = lambda w: _resident((1, w), (0, 0))
    acc = lambda shape: pl.BlockSpec(shape, lambda i: (0,) * len(shape))
    return _call(
        body, name=name, grid=(t // tm,),
        in_specs=[_rows(tm, d), _rows(tm, d), _rows(tm, d), _rows(tm, 2 * aw), vec(d),
                  _wspec(lay.a_in_rows, lay.a_in[j], d), vec(aw), vec(aw),
                  _resident((A_GROUPS, CHUNK, CHUNK), (0, 0, 0)), _resident((CHUNK, A_GROUPS), (0, 0)),
                  _wspec(lay.a_out_rows, lay.a_out[j], d), vec(d)],
        out_specs=[_rows(tm, d), _rows(tm, d), _rows(tm, 2 * aw), acc((1, d)), acc((1, d)), acc((1, aw)), acc((1, aw)),
                   acc((A_GROUPS, CHUNK, CHUNK)), acc((CHUNK, CHUNK))],
        out_shape=[jax.ShapeDtypeStruct((t, d), F32), jax.ShapeDtypeStruct((t, d), BF16),
                   jax.ShapeDtypeStruct((t, 2 * aw), BF16), jax.ShapeDtypeStruct((1, d), F32),
                   jax.ShapeDtypeStruct((1, d), F32), jax.ShapeDtypeStruct((1, aw), F32),
                   jax.ShapeDtypeStruct((1, aw), F32), jax.ShapeDtypeStruct((A_GROUPS, CHUNK, CHUNK), F32),
                   jax.ShapeDtypeStruct((CHUNK, CHUNK), F32)],
        scratch_shapes=[pltpu.VMEM((tm, aw), F32)],
        compiler_params=_params("arbitrary"),
    )(dx1, m, x, z, gpre, wg, lng, lnb, ws, bst, wg, gpost)


def _window_counts(first_row, n, win):
    tpos = first_row + lax.broadcasted_iota(jnp.int32, (n, 1), 0)
    return jnp.clip(tpos + 1, 1, win).astype(F32)


def _b_fwd(x, gpre, wg, lay, j, wgrp, scale, gpost, tm, name):
    t, d = x.shape
    n = tm + HALO
    ngrp = len(B_WINDOWS)

    def body(x_ref, xprev_ref, gpre_ref, win_ref, wgrp_ref, scale_ref, wout_ref, gpost_ref,
             x1_ref, h1_ref, pooled_ref, mixed_ref, m_ref):
        i = pl.program_id(0)
        xv = x_ref[...]
        keep = jnp.where(i > 0, 1.0, 0.0)
        xe = jnp.concatenate([xprev_ref[...] * keep, xv], axis=0)
        h1e = _rms_fwd(xe, gpre_ref[...]).astype(BF16)
        h1_ref[...] = h1e[HALO:]
        p = _nn(h1e, _two_d(win_ref))
        acc = p
        shift = 1
        for g, win in enumerate(B_WINDOWS):
            lo = g * B_GROUP_DIM
            if g > 0:
                acc = acc[:, B_GROUP_DIM:]
            while shift < win:
                acc = acc + pltpu.roll(acc, shift, 0)
                shift *= 2
            cnt = _window_counts(i * tm - HALO, n, win)
            pooled = acc[:, :B_GROUP_DIM] / cnt - p[:, lo:lo + B_GROUP_DIM]
            pooled_ref[:, lo:lo + B_GROUP_DIM] = pooled[HALO:].astype(BF16)
        for g in range(ngrp):
            cols = slice(g * B_GROUP_DIM, (g + 1) * B_GROUP_DIM)
            raw = _nn(pooled_ref[:, cols], wgrp_ref[g])
            mixed_ref[:, cols] = (raw * scale_ref[:, cols]).astype(BF16)
        m = _nn(mixed_ref[...], _two_d(wout_ref))
        m_ref[...] = m
        x1_ref[...] = xv + _rms_fwd(m, gpost_ref[...])

    vec = lambda w: _resident((1, w), (0, 0))
    per = tm // HALO
    return _call(
        body, name=name, grid=(t // tm,),
        in_specs=[_rows(tm, d), pl.BlockSpec((HALO, d), lambda i: (jnp.maximum(i * per - 1, 0), 0)), vec(d),
                  _wspec(lay.b_rows, lay.b_in[j], d), _resident((ngrp, B_GROUP_DIM, B_GROUP_DIM), (0, 0, 0)), vec(d),
                  _wspec(lay.b_rows, lay.b_out[j], d), vec(d)],
        out_specs=[_rows(tm, d)] * 5,
        out_shape=[jax.ShapeDtypeStruct((t, d), F32), jax.ShapeDtypeStruct((t, d), BF16),
                   jax.ShapeDtypeStruct((t, d), BF16), jax.ShapeDtypeStruct((t, d), BF16),
                   jax.ShapeDtypeStruct((t, d), F32)],
        compiler_params=_params("parallel"),
    )(x, x, gpre, wg, wgrp, scale, wg, gpost)


def _b_bwd(dx1, m, x, pooled, gpre, wg, lay, j, wgrp, scale, gpost, tm, name):
    t, d = x.shape
    n = tm + HALO
    ngrp = len(B_WINDOWS)
    steps = t // tm

    def body(dx1_ref, dx1n_ref, m_ref, mn_ref, x_ref, pooled_ref, pooledn_ref, gpre_ref, win_ref, wgrp_ref, scale_ref,
             wout_ref, gpost_ref, dx_ref, dm_ref, draw_ref, dp_ref, dgpost_ref, dgpre_ref, dscale_ref, dpool_ref):
        i = pl.program_id(0)

        @pl.when(i == 0)
        def _():
            for r in (dgpost_ref, dgpre_ref, dscale_ref):
                r[...] = jnp.zeros_like(r)

        keep = jnp.where(i < steps - 1, 1.0, 0.0)
        dy = dx1_ref[...]
        dye = jnp.concatenate([dy, dx1n_ref[...] * keep], axis=0)
        me = jnp.concatenate([m_ref[...], mn_ref[...]], axis=0)
        gpost_v = gpost_ref[...]
        r = lax.rsqrt(jnp.mean(me * me, axis=-1, keepdims=True) + EPS)
        mh = me * r
        dgpost_ref[...] += jnp.sum((dye * mh)[:tm], axis=0, keepdims=True)
        dmh = dye * gpost_v
        dme = (r * (dmh - mh * jnp.mean(dmh * mh, axis=-1, keepdims=True))).astype(BF16)
        dm_ref[...] = dme[:tm]
        dmixed = _nt(dme, _two_d(wout_ref))
        pooled_e = jnp.concatenate([pooled_ref[...], pooledn_ref[...]], axis=0)
        scale_v = scale_ref[...]
        for g, win in enumerate(B_WINDOWS):
            cols = slice(g * B_GROUP_DIM, (g + 1) * B_GROUP_DIM)
            raw = _nn(pooled_e[:, cols], wgrp_ref[g])
            dscale_ref[:, cols] += jnp.sum((dmixed[:, cols] * raw)[:tm], axis=0, keepdims=True)
            draw = (dmixed[:, cols] * scale_v[:, cols]).astype(BF16)
            draw_ref[:, cols] = draw[:tm]
            dpool = _nt(draw, wgrp_ref[g])
            acc = dpool / _window_counts(i * tm, n, win)
            shift = 1
            while shift < win:
                acc = acc + pltpu.roll(acc, n - shift, 0)
                shift *= 2
            dpool_ref[:, cols] = (acc - dpool)[:tm]
        dp = dpool_ref[...].astype(BF16)
        dp_ref[...] = dp
        dh1 = _nt(dp, _two_d(win_ref))
        dxp, dgpre = _rms_bwd(x_ref[...], gpre_ref[...], dh1)
        dgpre_ref[...] += dgpre
        dx_ref[...] = dy + dxp

    vec = lambda w: _resident((1, w), (0, 0))
    acc = lambda shape: pl.BlockSpec(shape, lambda i: (0,) * len(shape))
    per = tm // HALO
    nxt = lambda i: (jnp.minimum((i + 1) * per, t // HALO - 1), 0)
    return _call(
        body, name=name, grid=(steps,),
        in_specs=[_rows(tm, d), pl.BlockSpec((HALO, d), nxt), _rows(tm, d), pl.BlockSpec((HALO, d), nxt), _rows(tm, d),
                  _rows(tm, d), pl.BlockSpec((HALO, d), nxt), vec(d), _wspec(lay.b_rows, lay.b_in[j], d),
                  _resident((ngrp, B_GROUP_DIM, B_GROUP_DIM), (0, 0, 0)), vec(d), _wspec(lay.b_rows, lay.b_out[j], d),
                  vec(d)],
        out_specs=[_rows(tm, d)] * 4 + [acc((1, d))] * 3,
        out_shape=[jax.ShapeDtypeStruct((t, d), F32), jax.ShapeDtypeStruct((t, d), BF16),
                   jax.ShapeDtypeStruct((t, d), BF16), jax.ShapeDtypeStruct((t, d), BF16)]
                  + [jax.ShapeDtypeStruct((1, d), F32)] * 3,
        scratch_shapes=[pltpu.VMEM((tm, d), F32)],
        compiler_params=_params("arbitrary"),
    )(dx1, dx1, m, m, x, pooled, pooled, gpre, wg, wgrp, scale, wg, gpost)


def _f_fwd(x1, gpre, wg, lay, l, gpost, tm, name):
    t, d = x1.shape
    hid = N_DEV * lay.ffn_rows

    def body(x_ref, gpre_ref, wgate_ref, wup_ref, wdown_ref, gpost_ref, x2_ref, h2_ref, a_ref, b_ref, s_ref, f_ref):
        xv = x_ref[...]
        h2 = _rms_fwd(xv, gpre_ref[...]).astype(BF16)
        h2_ref[...] = h2
        a = _nt(h2, _two_d(wgate_ref))
        b = _nt(h2, _two_d(wup_ref))
        a_ref[...] = a.astype(BF16)
        b_ref[...] = b.astype(BF16)
        s = (a * (1.0 / (1.0 + jnp.exp(-a))) * b).astype(BF16)
        s_ref[...] = s
        f = _nn(s, _two_d(wdown_ref))
        f_ref[...] = f
        x2_ref[...] = xv + _rms_fwd(f, gpost_ref[...])

    vec = lambda w: _resident((1, w), (0, 0))
    return _call(
        body, name=name, grid=(t // tm,),
        in_specs=[_rows(tm, d), vec(d), _wspec(lay.ffn_rows, lay.gate[l], d), _wspec(lay.ffn_rows, lay.up[l], d),
                  _wspec(lay.ffn_rows, lay.down[l], d), vec(d)],
        out_specs=[_rows(tm, d), _rows(tm, d), _rows(tm, hid), _rows(tm, hid), _rows(tm, hid), _rows(tm, d)],
        out_shape=[jax.ShapeDtypeStruct((t, d), F32), jax.ShapeDtypeStruct((t, d), BF16),
                   jax.ShapeDtypeStruct((t, hid), BF16), jax.ShapeDtypeStruct((t, hid), BF16),
                   jax.ShapeDtypeStruct((t, hid), BF16), jax.ShapeDtypeStruct((t, d), F32)],
        compiler_params=_params("parallel"),
    )(x1, gpre, wg, wg, wg, gpost)


def _f_bwd(dx2, f, x1, a, b, gpre, wg, lay, l, gpost, tm, name):
    t, d = x1.shape
    hid = N_DEV * lay.ffn_rows

    def body(dx2_ref, f_ref, x_ref, a_ref, b_ref, gpre_ref, wgate_ref, wup_ref, wdown_ref, gpost_ref,
             dx1_ref, df_ref, da_ref, db_ref, dgpost_ref, dgpre_ref):
        @pl.when(pl.program_id(0) == 0)
        def _():
            dgpost_ref[...] = jnp.zeros_like(dgpost_ref)
            dgpre_ref[...] = jnp.zeros_like(dgpre_ref)

        dy = dx2_ref[...]
        df, dgpost = _rms_bwd(f_ref[...], gpost_ref[...], dy)
        dgpost_ref[...] += dgpost
        df_bf = df.astype(BF16)
        df_ref[...] = df_bf
        ds = _nt(df_bf, _two_d(wdown_ref))
        av = a_ref[...].astype(F32)
        bv = b_ref[...].astype(F32)
        sig = 1.0 / (1.0 + jnp.exp(-av))
        da = (ds * bv * (sig * (1.0 + av * (1.0 - sig)))).astype(BF16)
        db = (ds * (av * sig)).astype(BF16)
        da_ref[...] = da
        db_ref[...] = db
        dh2 = _nn(da, _two_d(wgate_ref)) + _nn(db, _two_d(wup_ref))
        dxp, dgpre = _rms_bwd(x_ref[...], gpre_ref[...], dh2)
        dgpre_ref[...] += dgpre
        dx1_ref[...] = dy + dxp

    vec = lambda w: _resident((1, w), (0, 0))
    acc = pl.BlockSpec((1, d), lambda i: (0, 0))
    return _call(
        body, name=name, grid=(t // tm,),
        in_specs=[_rows(tm, d), _rows(tm, d), _rows(tm, d), _rows(tm, hid), _rows(tm, hid), vec(d),
                  _wspec(lay.ffn_rows, lay.gate[l], d), _wspec(lay.ffn_rows, lay.up[l], d),
                  _wspec(lay.ffn_rows, lay.down[l], d), vec(d)],
        out_specs=[_rows(tm, d), _rows(tm, d), _rows(tm, hid), _rows(tm, hid), acc, acc],
        out_shape=[jax.ShapeDtypeStruct((t, d), F32), jax.ShapeDtypeStruct((t, d), BF16),
                   jax.ShapeDtypeStruct((t, hid), BF16), jax.ShapeDtypeStruct((t, hid), BF16),
                   jax.ShapeDtypeStruct((1, d), F32), jax.ShapeDtypeStruct((1, d), F32)],
        compiler_params=_params("arbitrary"),
    )(dx2, f, x1, a, b, gpre, wg, wg, wg, gpost)


def _loss_head(y, target, tm, name):
    t, d = y.shape

    def body(y_ref, t_ref, dy_ref, loss_ref):
        @pl.when(pl.program_id(0) == 0)
        def _():
            loss_ref[...] = jnp.zeros_like(loss_ref)

        diff = y_ref[...] - t_ref[...]
        dy_ref[...] = diff * (1.0 / d)
        sq = jnp.sum(jnp.sum(diff * diff, axis=0, keepdims=True), axis=1, keepdims=True)
        loss_ref[...] += sq * (0.5 / d)

    return _call(
        body, name=name, grid=(t // tm,),
        in_specs=[_rows(tm, d), _rows(tm, d)],
        out_specs=[_rows(tm, d), pl.BlockSpec((8, 128), lambda i: (0, 0))],
        out_shape=[jax.ShapeDtypeStruct((t, d), F32), jax.ShapeDtypeStruct((8, 128), F32)],
        compiler_params=_params("arbitrary"),
    )(y, target)


def _grad_into(gbuf, lhs, rhs, off, rows, name):
    t, m = lhs.shape
    d = rhs.shape[1]
    assert m == N_DEV * rows and off % rows == 0
    per_tile = {384: 4, 512: 2, 256: 4, 128: 8}[rows]
    tm = per_tile * rows
    tk = 1024 if t % 1024 == 0 else 256
    ksteps = t // tk

    def body(l_ref, r_ref, g_ref, o_ref, acc_ref):
        k = pl.program_id(1)

        @pl.when(k == 0)
        def _():
            acc_ref[...] = jnp.zeros_like(acc_ref)

        acc_ref[...] += _tn(l_ref[...], r_ref[...])

        @pl.when(k == ksteps - 1)
        def _():
            o_ref[...] = acc_ref[...].reshape(per_tile, rows, d).astype(BF16)

    return _call(
        body, name=name, grid=(N_DEV // per_tile, ksteps),
        in_specs=[pl.BlockSpec((tk, tm), lambda i, k: (k, i)), pl.BlockSpec((tk, d), lambda i, k: (k, 0)), ANY],
        out_specs=pl.BlockSpec((per_tile, rows, d), lambda i, k: (i, off // rows, 0)),
        out_shape=jax.ShapeDtypeStruct(gbuf.shape, gbuf.dtype),
        scratch_shapes=[pltpu.VMEM((tm, d), F32)],
        input_output_aliases={2: 0},
        compiler_params=_params("parallel", "arbitrary"),
    )(lhs, rhs, gbuf)


def _grad_grouped(pooled, draw, name):
    t, d = pooled.shape
    ngrp = len(B_WINDOWS)
    tk = 1024 if t % 1024 == 0 else 256

    def body(p_ref, q_ref, o_ref):
        @pl.when(pl.program_id(0) == 0)
        def _():
            o_ref[...] = jnp.zeros_like(o_ref)

        for g in range(ngrp):
            cols = slice(g * B_GROUP_DIM, (g + 1) * B_GROUP_DIM)
            o_ref[g] += _tn(p_ref[:, cols], q_ref[:, cols])

    return _call(
        body, name=name, grid=(t // tk,),
        in_specs=[_rows(tk, d), _rows(tk, d)],
        out_specs=pl.BlockSpec((ngrp, B_GROUP_DIM, B_GROUP_DIM), lambda i: (0, 0, 0)),
        out_shape=jax.ShapeDtypeStruct((ngrp, B_GROUP_DIM, B_GROUP_DIM), F32),
        compiler_params=_params("arbitrary"),
    )(pooled, draw)


def _place():
    x, y, c = lax.axis_index("x"), lax.axis_index("y"), lax.axis_index("c")
    chips = [(1 - x, y), (x, 1 - y), (1 - x, 1 - y)]
    return x, y, c, chips


def _all_gather(arrs, name):
    n = len(arrs)

    def body(*refs):
        ins, outs = refs[:n], refs[n:2 * n]
        send_sems, recv_sems, local_sems = refs[2 * n:]
        x, y, c, chips = _place()
        me, sibling = (x, y, c), (x, y, 1 - c)

        def slot(a, px, py, pc):
            return outs[a].at[4 * px + 2 * py + pc]

        def copy(a, k, block, to, src=None):
            return pltpu.make_async_remote_copy(
                src_ref=slot(a, *block) if src is None else src, dst_ref=slot(a, *block),
                send_sem=send_sems.at[a, k], recv_sem=recv_sems.at[a, k], device_id=to, device_id_type=MESH)

        mine = [pltpu.make_async_copy(ins[a], slot(a, *me), local_sems.at[a]) for a in range(n)]
        for cp in mine:
            cp.start()
        first = []
        for a in range(n):
            first.append(copy(a, 0, me, sibling, src=ins[a]))
            first += [copy(a, 1 + j, me, (*chip, c), src=ins[a]) for j, chip in enumerate(chips)]
        for cp in first:
            cp.start()
        passed = [[copy(a, 4 + j, (*chip, c), sibling) for j, chip in enumerate(chips)] for a in range(n)]
        for j, chip in enumerate(chips):
            for a in range(n):
                copy(a, 1 + j, (*chip, c), me).wait_recv()
                passed[a][j].start()
        for a in range(n):
            copy(a, 0, sibling, me).wait_recv()
            for j, chip in enumerate(chips):
                copy(a, 4 + j, (*chip, 1 - c), me).wait_recv()
        for cp in first + [p for ps in passed for p in ps]:
            cp.wait_send()
        for cp in mine:
            cp.wait()

    return _call(
        body, name=name,
        in_specs=[ANY] * n, out_specs=[ANY] * n,
        out_shape=[jax.ShapeDtypeStruct((N_DEV,) + a.shape, a.dtype) for a in arrs],
        scratch_shapes=[pltpu.SemaphoreType.DMA((n, 7)), pltpu.SemaphoreType.DMA((n, 7)),
                        pltpu.SemaphoreType.DMA((n,))],
    )(*arrs)


def _to_sibling(arrs, name):
    n = len(arrs)
    nchip = N_DEV // 2

    def body(*refs):
        ins, outs = refs[:n], refs[n:2 * n]
        send_sems, recv_sems = refs[2 * n:]
        x, y, c, _ = _place()
        copies = []
        for a in range(n):
            for q in range(nchip):
                copies.append(pltpu.make_async_remote_copy(
                    src_ref=ins[a].at[q, 1 - c], dst_ref=outs[a].at[q], send_sem=send_sems.at[a, q],
                    recv_sem=recv_sems.at[a, q], device_id=(x, y, 1 - c), device_id_type=MESH))
        for cp in copies:
            cp.start()
        for cp in copies:
            cp.wait()

    return _call(
        body, name=name,
        in_specs=[ANY] * n, out_specs=[ANY] * n,
        out_shape=[jax.ShapeDtypeStruct((nchip,) + a.shape[2:], a.dtype) for a in arrs],
        scratch_shapes=[pltpu.SemaphoreType.DMA((n, nchip)), pltpu.SemaphoreType.DMA((n, nchip))],
    )(*arrs)


def _to_chips(arrs, small, name):
    n = len(arrs)

    def body(*refs):
        ins, small_ref = refs[:n], refs[n]
        outs, small_out = refs[n + 1:2 * n + 1], refs[2 * n + 1]
        send_sems, recv_sems, small_send, small_recv, local_sem = refs[2 * n + 2:]
        x, y, c, chips = _place()
        copies = []
        for a in range(n):
            for j, (cx, cy) in enumerate(chips):
                copies.append(pltpu.make_async_remote_copy(
                    src_ref=ins[a].at[2 * cx + cy], dst_ref=outs[a].at[j], send_sem=send_sems.at[a, j],
                    recv_sem=recv_sems.at[a, j], device_id=(cx, cy, c), device_id_type=MESH))
        my_slot = small_out.at[4 * x + 2 * y + c]
        flip = lambda v, f: 1 - v if f else v
        for r in range(1, N_DEV):
            peer = (flip(x, r & 4), flip(y, r & 2), flip(c, r & 1))
            copies.append(pltpu.make_async_remote_copy(
                src_ref=small_ref, dst_ref=my_slot, send_sem=small_send.at[r - 1], recv_sem=small_recv.at[r - 1],
                device_id=peer, device_id_type=MESH))
        mine = pltpu.make_async_copy(small_ref, my_slot, local_sem.at[0])
        mine.start()
        for cp in copies:
            cp.start()
        for cp in copies:
            cp.wait()
        mine.wait()

    return _call(
        body, name=name,
        in_specs=[ANY] * (n + 1), out_specs=[ANY] * (n + 1),
        out_shape=[jax.ShapeDtypeStruct((3,) + a.shape[1:], a.dtype) for a in arrs]
                  + [jax.ShapeDtypeStruct((N_DEV,) + small.shape, small.dtype)],
        scratch_shapes=[pltpu.SemaphoreType.DMA((n, 3)), pltpu.SemaphoreType.DMA((n, 3)),
                        pltpu.SemaphoreType.DMA((N_DEV - 1,)), pltpu.SemaphoreType.DMA((N_DEV - 1,)),
                        pltpu.SemaphoreType.DMA((1,))],
    )(*arrs, small)


def _row_tile(rows):
    if rows <= 512:
        return rows
    for tr in (512, 384, 256, 128, 64, 32, 16, 8):
        if rows % tr == 0:
            return tr
    return rows


def _sum_with_sibling(own, got, core, out_dtype, name):
    nchip, _, rows, w = own.shape
    tr = _row_tile(rows)

    def body(core_ref, a_ref, b_ref, o_ref):
        o_ref[...] = (a_ref[...].astype(F32) + b_ref[...].astype(F32)).astype(out_dtype)

    return _call(
        body, name=name,
        grid_spec=pltpu.PrefetchScalarGridSpec(
            num_scalar_prefetch=1, grid=(nchip, rows // tr),
            in_specs=[pl.BlockSpec((None, None, tr, w), lambda q, i, core_ref: (q, core_ref[0], i, 0)),
                      pl.BlockSpec((None, tr, w), lambda q, i, core_ref: (q, i, 0))],
            out_specs=pl.BlockSpec((None, tr, w), lambda q, i, core_ref: (q, i, 0))),
        out_shape=jax.ShapeDtypeStruct((nchip, rows, w), out_dtype),
        compiler_params=_params("parallel", "parallel"),
    )(core, own, got)


def _sum_chips(own, got, chip, name):
    _, rows, w = own.shape
    tr = _row_tile(rows)

    def body(chip_ref, a_ref, b_ref, o_ref):
        s = a_ref[...].astype(F32)
        for j in range(3):
            s = s + b_ref[j].astype(F32)
        o_ref[...] = s

    return _call(
        body, name=name,
        grid_spec=pltpu.PrefetchScalarGridSpec(
            num_scalar_prefetch=1, grid=(rows // tr,),
            in_specs=[pl.BlockSpec((None, tr, w), lambda i, chip_ref: (chip_ref[0], i, 0)),
                      pl.BlockSpec((3, tr, w), lambda i, chip_ref: (0, i, 0))],
            out_specs=pl.BlockSpec((tr, w), lambda i, chip_ref: (i, 0))),
        out_shape=jax.ShapeDtypeStruct((rows, w), F32),
        compiler_params=_params("parallel"),
    )(chip, own, got)


def _sum_devices(stacked, name):
    k, rows, w = stacked.shape
    tr = _row_tile(rows)

    def body(a_ref, o_ref):
        s = a_ref[0]
        for j in range(1, k):
            s = s + a_ref[j]
        o_ref[...] = s

    return _call(
        body, name=name, grid=(rows // tr,),
        in_specs=[pl.BlockSpec((k, tr, w), lambda i: (0, i, 0))],
        out_specs=pl.BlockSpec((tr, w), lambda i: (i, 0)),
        out_shape=jax.ShapeDtypeStruct((rows, w), F32),
        compiler_params=_params("parallel"),
    )(stacked)


def _adamw(w, g, m, v, name):
    rows, cols = w.shape
    tr = _row_tile(rows)

    def body(w_ref, g_ref, m_ref, v_ref, d_ref, nm_ref, nv_ref):
        gv = g_ref[...]
        nm = ADAM_B1 * m_ref[...] + (1.0 - ADAM_B1) * gv
        nv = ADAM_B2 * v_ref[...] + (1.0 - ADAM_B2) * (gv * gv)
        m_hat = nm / (1.0 - ADAM_B1 ** ADAM_STEP)
        v_hat = nv / (1.0 - ADAM_B2 ** ADAM_STEP)
        d_ref[...] = -ADAM_LR * (m_hat / (jnp.sqrt(v_hat) + ADAM_EPS) + ADAM_WD * w_ref[...])
        nm_ref[...] = nm
        nv_ref[...] = nv

    spec = pl.BlockSpec((tr, cols), lambda i: (i, 0))
    return _call(
        body, name=name, grid=(rows // tr,),
        in_specs=[spec] * 4, out_specs=[spec] * 3,
        out_shape=[jax.ShapeDtypeStruct((rows, cols), F32)] * 3,
        compiler_params=_params("parallel"),
    )(w, g, m, v)


SMALL = ("a_ln_g", "a_ln_b", "a_w_s", "a_b_s", "mix_pre_g", "mix_post_g", "ffn_pre_g", "ffn_post_g")


def _pack_small(parts, d):
    flat = jnp.concatenate([parts[k].reshape(-1, d) for k in SMALL], axis=0)
    return jnp.pad(flat, ((0, -flat.shape[0] % 8), (0, 0)))


def _unpack_small(flat, like):
    out, r = {}, 0
    for k in SMALL:
        n = like[k].size // flat.shape[1]
        out[k] = flat[r:r + n].reshape(like[k].shape)
        r += n
    return out


def kernel(x, a_w_in, a_ln_g, a_ln_b, a_w_s, a_b_s, a_w_out, b_w_in, b_w_grp, b_scale, b_w_out, mix_pre_g, mix_post_g, ffn_pre_g, ffn_post_g, ffn_w_gate, ffn_w_up, ffn_w_down, loss_target, m_a_w_in, m_a_ln_g, m_a_ln_b, m_a_w_s, m_a_b_s, m_a_w_out, m_b_w_in, m_b_w_grp, m_b_scale, m_b_w_out, m_mix_pre_g, m_mix_post_g, m_ffn_pre_g, m_ffn_post_g, m_ffn_w_gate, m_ffn_w_up, m_ffn_w_down, v_a_w_in, v_a_ln_g, v_a_ln_b, v_a_w_s, v_a_b_s, v_a_w_out, v_b_w_in, v_b_w_grp, v_b_scale, v_b_w_out, v_mix_pre_g, v_mix_post_g, v_ffn_pre_g, v_ffn_post_g, v_ffn_w_gate, v_ffn_w_up, v_ffn_w_down):
    args = dict(locals())
    names = ("a_w_in", "a_ln_g", "a_ln_b", "a_w_s", "a_b_s", "a_w_out", "b_w_in", "b_w_grp", "b_scale", "b_w_out",
             "mix_pre_g", "mix_post_g", "ffn_pre_g", "ffn_post_g", "ffn_w_gate", "ffn_w_up", "ffn_w_down")
    w = {k: args[k] for k in names}
    mom = {k: args["m_" + k] for k in names}
    var = {k: args["v_" + k] for k in names}

    t, d = x.shape[1], x.shape[2]
    lay = _Layout(d)
    ffn_local = ffn_w_gate.shape[2]
    ffn_pad = lay.ffn_rows - ffn_local
    xc, yc, cc = lax.axis_index("x"), lax.axis_index("y"), lax.axis_index("c")
    core = jnp.reshape(cc, (1,)).astype(jnp.int32)
    chip = jnp.reshape(2 * xc + yc, (1,)).astype(jnp.int32)

    def ffn_t(wl):
        return jnp.pad(wl.T, ((0, ffn_pad), (0, 0)))

    pieces = ([ffn_t(ffn_w_gate[l]) for l in range(DEPTH)] + [ffn_t(ffn_w_up[l]) for l in range(DEPTH)]
              + [jnp.pad(ffn_w_down[l], ((0, ffn_pad), (0, 0))) for l in range(DEPTH)]
              + [a_w_in[j].T for j in range(2)] + [a_w_out[j] for j in range(2)]
              + [b_w_in[j] for j in range(2)] + [b_w_out[j] for j in range(2)])
    packed = jnp.concatenate(pieces, axis=0).astype(BF16)
    assert packed.shape == (lay.rows, d)
    ngrp = len(B_WINDOWS)
    grp_local = b_w_grp.shape[2]
    sdev = b_scale.shape[1]
    side_rows = 2 * ngrp * grp_local
    side = jnp.concatenate(
        [b_w_grp.reshape(side_rows, B_GROUP_DIM),
         jnp.pad(b_scale, ((0, 6), (0, B_GROUP_DIM - sdev)))], axis=0)

    wg, side_g = _all_gather([packed, side], "gather_weights")
    wgrp_full = (side_g[:, :side_rows].reshape(N_DEV, 2, ngrp, grp_local, B_GROUP_DIM)
                 .transpose(1, 2, 0, 3, 4).reshape(2, ngrp, B_GROUP_DIM, B_GROUP_DIM).astype(BF16))
    scale_full = side_g[:, side_rows:side_rows + 2, :sdev].transpose(1, 0, 2).reshape(2, 1, N_DEV * sdev)
    row = lambda a: a.reshape(1, -1)
    bst = jnp.transpose(a_b_s, (0, 2, 1))

    tm = 256 if t % 256 == 0 else CHUNK
    tm_abwd = CHUNK

    saved = []
    h = x[0]
    for i in range(DEPTH):
        j = i // 2
        if i % 2 == 0:
            x1, h1, z, gated, m = _a_fwd(h, row(mix_pre_g[i]), wg, lay, j, row(a_ln_g[j]), row(a_ln_b[j]), a_w_s[j],
                                         bst[j], row(mix_post_g[i]), tm, f"a_fwd_{j}")
            mix = dict(h1=h1, z=z, gated=gated, m=m)
        else:
            x1, h1, pooled, mixed, m = _b_fwd(h, row(mix_pre_g[i]), wg, lay, j, wgrp_full[j], scale_full[j],
                                              row(mix_post_g[i]), tm, f"b_fwd_{j}")
            mix = dict(h1=h1, pooled=pooled, mixed=mixed, m=m)
        x2, h2, a, b, s, f = _f_fwd(x1, row(ffn_pre_g[i]), wg, lay, i, row(ffn_post_g[i]), tm, f"f_fwd_{i}")
        saved.append(dict(x=h, x1=x1, mix=mix, h2=h2, a=a, b=b, s=s, f=f))
        h = x2

    dy, loss_acc = _loss_head(h, loss_target[0], tm, "loss_head")
    loss = lax.psum(loss_acc[0, 0], ("x", "y", "c"))

    gbuf = jnp.zeros((N_DEV, lay.rows, d), BF16)
    small_g = {k: [None] * w[k].shape[0] for k in SMALL}
    dgrp, dscale = [None, None], [None, None]
    for i in reversed(range(DEPTH)):
        sv = saved[i]
        j = i // 2
        dx1, df, da, db, dgpost, dgpre = _f_bwd(dy, sv["f"], sv["x1"], sv["a"], sv["b"], row(ffn_pre_g[i]), wg, lay, i,
                                                 row(ffn_post_g[i]), tm, f"f_bwd_{i}")
        small_g["ffn_post_g"][i], small_g["ffn_pre_g"][i] = dgpost[0], dgpre[0]
        gbuf = _grad_into(gbuf, da, sv["h2"], lay.gate[i], lay.ffn_rows, f"g_gate_{i}")
        gbuf = _grad_into(gbuf, db, sv["h2"], lay.up[i], lay.ffn_rows, f"g_up_{i}")
        gbuf = _grad_into(gbuf, sv["s"], df, lay.down[i], lay.ffn_rows, f"g_down_{i}")
        mix = sv["mix"]
        if i % 2 == 0:
            dx, dm, dz, dgpost, dgpre, dlng, dlnb, dws, dbt = _a_bwd(
                dx1, mix["m"], sv["x"], mix["z"], row(mix_pre_g[i]), wg, lay, j, row(a_ln_g[j]), row(a_ln_b[j]),
                a_w_s[j], bst[j], row(mix_post_g[i]), tm_abwd, f"a_bwd_{j}")
            small_g["a_ln_g"][j], small_g["a_ln_b"][j] = dlng[0], dlnb[0]
            small_g["a_w_s"][j], small_g["a_b_s"][j] = dws, dbt[:, :A_GROUPS].T
            gbuf = _grad_into(gbuf, dz, mix["h1"], lay.a_in[j], lay.a_in_rows, f"g_a_in_{j}")
            gbuf = _grad_into(gbuf, mix["gated"], dm, lay.a_out[j], lay.a_out_rows, f"g_a_out_{j}")
        else:
            dx, dm, draw, dp, dgpost, dgpre, dsc = _b_bwd(
                dx1, mix["m"], sv["x"], mix["pooled"], row(mix_pre_g[i]), wg, lay, j, wgrp_full[j], scale_full[j],
                row(mix_post_g[i]), tm, f"b_bwd_{j}")
            dscale[j] = dsc[0]
            dgrp[j] = _grad_grouped(mix["pooled"], draw, f"g_b_grp_{j}")
            gbuf = _grad_into(gbuf, mix["h1"], dp, lay.b_in[j], lay.b_rows, f"g_b_in_{j}")
            gbuf = _grad_into(gbuf, mix["mixed"], dm, lay.b_out[j], lay.b_rows, f"g_b_out_{j}")
        small_g["mix_post_g"][i], small_g["mix_pre_g"][i] = dgpost[0], dgpre[0]
        dy = dx
    grad_x = dy[None]

    side_grad = jnp.concatenate(
        [jnp.stack(dgrp).reshape(2, ngrp, N_DEV, grp_local, B_GROUP_DIM).transpose(2, 0, 1, 3, 4)
         .reshape(N_DEV, side_rows, B_GROUP_DIM),
         jnp.pad(jnp.stack(dscale).reshape(2, N_DEV, sdev).transpose(1, 0, 2),
                 ((0, 0), (0, 6), (0, B_GROUP_DIM - sdev)))], axis=1)
    small_part = _pack_small({k: jnp.stack(small_g[k]) for k in SMALL}, d)

    nchip = N_DEV // 2
    gbuf4 = gbuf.reshape(nchip, 2, lay.rows, d)
    side4 = side_grad.reshape(nchip, 2, side_rows + 8, B_GROUP_DIM)
    got_g, got_s = _to_sibling([gbuf4, side4], "reduce_to_sibling")
    chip_g = _sum_with_sibling(gbuf4, got_g, core, BF16, "sum_sibling_big")
    chip_s = _sum_with_sibling(side4, got_s, core, F32, "sum_sibling_side")
    far_g, far_s, small_all = _to_chips([chip_g, chip_s], small_part, "reduce_to_chips")
    g_packed = _sum_chips(chip_g, far_g, chip, "sum_chips_big")
    g_side = _sum_chips(chip_s, far_s, chip, "sum_chips_side")
    g_small = _unpack_small(_sum_devices(small_all, "sum_small"), w)

    def rows_of(off, n):
        return g_packed[off:off + n]

    grads = dict(g_small)
    grads["ffn_w_gate"] = jnp.stack([rows_of(lay.gate[l], ffn_local).T for l in range(DEPTH)])
    grads["ffn_w_up"] = jnp.stack([rows_of(lay.up[l], ffn_local).T for l in range(DEPTH)])
    grads["ffn_w_down"] = jnp.stack([rows_of(lay.down[l], ffn_local) for l in range(DEPTH)])
    grads["a_w_in"] = jnp.stack([rows_of(lay.a_in[j], lay.a_in_rows).T for j in range(2)])
    grads["a_w_out"] = jnp.stack([rows_of(lay.a_out[j], lay.a_out_rows) for j in range(2)])
    grads["b_w_in"] = jnp.stack([rows_of(lay.b_in[j], lay.b_rows) for j in range(2)])
    grads["b_w_out"] = jnp.stack([rows_of(lay.b_out[j], lay.b_rows) for j in range(2)])
    grads["b_w_grp"] = g_side[:side_rows].reshape(b_w_grp.shape)
    grads["b_scale"] = g_side[side_rows:side_rows + 2, :sdev]

    delta, new_m, new_v = {}, {}, {}
    big = [k for k in names if k not in SMALL]
    for k in big:
        shape = w[k].shape
        two = lambda a: a.reshape(-1, shape[-1])
        dl, nm, nv = _adamw(two(w[k]), two(grads[k]), two(mom[k]), two(var[k]), f"adamw_{k}")
        delta[k], new_m[k], new_v[k] = dl.reshape(shape), nm.reshape(shape), nv.reshape(shape)
    dl, nm, nv = _adamw(_pack_small(w, d), _pack_small(g_small, d), _pack_small(mom, d), _pack_small(var, d),
                        "adamw_small")
    delta.update(_unpack_small(dl, w))
    new_m.update(_unpack_small(nm, w))
    new_v.update(_unpack_small(nv, w))

    return (loss, grad_x, *[grads[k] for k in names], *[delta[k] for k in names], *[new_m[k] for k in names],
            *[new_v[k] for k in names])
```

```python
import functools
import math

import jax
import jax.numpy as jnp
from jax import lax
from jax.experimental import pallas as pl
from jax.experimental.pallas import tpu as pltpu

F32 = jnp.float32
BF16 = jnp.bfloat16
MESH = pl.DeviceIdType.MESH
ANY = pl.BlockSpec(memory_space=pl.ANY)

N_DEV = 8
EPS = 1e-6
CHUNK = 128
A_GROUPS = 8
A_GROUP_DIM = 256
B_WINDOWS = (2, 4, 8, 16)
B_GROUP_DIM = 256
HALO = 16
DEPTH = 4

ADAM_LR = 0.001
ADAM_B1 = 0.9
ADAM_B2 = 0.999
ADAM_EPS = 1e-08
ADAM_WD = 0.01
ADAM_STEP = 10

VMEM_LIMIT_BYTES = 60 * 1024 * 1024

ERF_P = 0.3275911
ERF_A = (0.254829592, -0.284496736, 1.421413741, -1.453152027, 1.061405429)
INV_SQRT2 = 1.0 / math.sqrt(2.0)
INV_SQRT_2PI = 1.0 / math.sqrt(2.0 * math.pi)


def _call(body, **kw):
    return pl.pallas_call(body, **kw)


def _params(*semantics):
    return pltpu.CompilerParams(dimension_semantics=semantics or None, vmem_limit_bytes=VMEM_LIMIT_BYTES)


def _resident(shape, index):
    return pl.BlockSpec(shape, lambda *_: index, pipeline_mode=pl.Buffered(1))


def _rows(tm, width):
    return pl.BlockSpec((tm, width), lambda i: (i, 0))


def _nn(a, b):
    return jnp.dot(a, b, preferred_element_type=F32)


def _nt(a, b):
    return lax.dot_general(a, b, (((1,), (1,)), ((), ())), preferred_element_type=F32)


def _tn(a, b):
    return lax.dot_general(a, b, (((0,), (0,)), ((), ())), preferred_element_type=F32)


def _rms_fwd(x, g):
    r = lax.rsqrt(jnp.mean(x * x, axis=-1, keepdims=True) + EPS)
    return x * r * g


def _rms_bwd(x, g, dy):
    r = lax.rsqrt(jnp.mean(x * x, axis=-1, keepdims=True) + EPS)
    xh = x * r
    dg = jnp.sum(dy * xh, axis=0, keepdims=True)
    dxh = dy * g
    dx = r * (dxh - xh * jnp.mean(dxh * xh, axis=-1, keepdims=True))
    return dx, dg


def _gelu(z):
    a = jnp.abs(z) * INV_SQRT2
    t = 1.0 / (1.0 + ERF_P * a)
    e = jnp.exp(-a * a)
    poly = t * (ERF_A[0] + t * (ERF_A[1] + t * (ERF_A[2] + t * (ERF_A[3] + t * ERF_A[4]))))
    half = 0.5 * poly * e
    phi = jnp.where(z >= 0, 1.0 - half, half)
    return z * phi, phi + z * e * INV_SQRT_2PI


def _layernorm_stats(v):
    mu = jnp.mean(v, axis=-1, keepdims=True)
    xc = v - mu
    rs = lax.rsqrt(jnp.mean(xc * xc, axis=-1, keepdims=True) + EPS)
    return xc * rs, rs


def _tril_mask():
    r = lax.broadcasted_iota(jnp.int32, (CHUNK, CHUNK), 0)
    c = lax.broadcasted_iota(jnp.int32, (CHUNK, CHUNK), 1)
    return r >= c


def _two_d(ref):
    k, r, d = ref.shape
    return ref[...].reshape(k * r, d)


class _Layout:
    def __init__(self, d):
        self.ffn_rows = 384
        self.gate, self.up, self.down = [0] * DEPTH, [self.ffn_rows] * DEPTH, [2 * self.ffn_rows] * DEPTH
        self.f_total = 3 * self.ffn_rows
        self.a_in_rows, self.a_out_rows, self.b_rows = 4 * d // N_DEV, 2 * d // N_DEV, d // N_DEV
        self.a_in, self.a_out = [0, 0], [self.a_in_rows] * 2
        self.a_total = self.a_in_rows + self.a_out_rows
        self.b_in, self.b_out = [0, 0], [self.b_rows] * 2
        self.b_total = 2 * self.b_rows


def _wspec(rows, off, d):
    assert off % rows == 0
    return _resident((N_DEV, rows, d), (0, off // rows, 0))


def _a_fwd(x, gpre, wg, lay, j, lng, lnb, ws, bst, gpost, tm, name):
    t, d = x.shape
    aw = 2 * d
    nch = tm // CHUNK

    def body(x_ref, gpre_ref, win_ref, lng_ref, lnb_ref, ws_ref, bst_ref, wout_ref, gpost_ref,
             x1_ref, h1_ref, z_ref, gated_ref, m_ref):
        xv = x_ref[...]
        h1 = _rms_fwd(xv, gpre_ref[...]).astype(BF16)
        h1_ref[...] = h1
        z = _nt(h1, _two_d(win_ref))
        z_ref[...] = z.astype(BF16)
        u, _ = _gelu(z[:, :aw])
        v, _ = _gelu(z[:, aw:])
        vh, _ = _layernorm_stats(v)
        vn = (vh * lng_ref[...] + lnb_ref[...]).astype(BF16)
        mask = _tril_mask()
        for g in range(A_GROUPS):
            wm = jnp.where(mask, ws_ref[g], 0.0).astype(BF16)
            cols = slice(g * A_GROUP_DIM, (g + 1) * A_GROUP_DIM)
            for c in range(nch):
                rows = slice(c * CHUNK, (c + 1) * CHUNK)
                sv = _nn(wm, vn[rows, cols]) + bst_ref[:, g:g + 1]
                gated_ref[rows, cols] = (u[rows, cols] * sv).astype(BF16)
        m = _nn(gated_ref[...], _two_d(wout_ref))
        m_ref[...] = m
        x1_ref[...] = xv + _rms_fwd(m, gpost_ref[...])

    vec = lambda w: _resident((1, w), (0, 0))
    return _call(
        body, name=name, grid=(t // tm,),
        in_specs=[_rows(tm, d), vec(d), _wspec(lay.a_in_rows, lay.a_in[j], d), vec(aw), vec(aw),
                  _resident((A_GROUPS, CHUNK, CHUNK), (0, 0, 0)), _resident((CHUNK, A_GROUPS), (0, 0)),
                  _wspec(lay.a_out_rows, lay.a_out[j], d), vec(d)],
        out_specs=[_rows(tm, d), _rows(tm, d), _rows(tm, 2 * aw), _rows(tm, aw), _rows(tm, d)],
        out_shape=[jax.ShapeDtypeStruct((t, d), F32), jax.ShapeDtypeStruct((t, d), BF16),
                   jax.ShapeDtypeStruct((t, 2 * aw), BF16), jax.ShapeDtypeStruct((t, aw), BF16),
                   jax.ShapeDtypeStruct((t, d), F32)],
        compiler_params=_params("parallel"),
    )(x, gpre, wg, lng, lnb, ws, bst, wg, gpost)


def _a_bwd(dx1, m, x, z, gpre, wg, lay, j, lng, lnb, ws, bst, gpost, tm, name):
    t, d = x.shape
    aw = 2 * d
    nch = tm // CHUNK

    def body(dx1_ref, m_ref, x_ref, z_ref, gpre_ref, win_ref, lng_ref, lnb_ref, ws_ref, bst_ref, wout_ref, gpost_ref,
             dx_ref, dm_ref, dz_ref, dgpost_ref, dgpre_ref, dlng_ref, dlnb_ref, dws_ref, dbt_ref, dvn_ref):
        @pl.when(pl.program_id(0) == 0)
        def _():
            for r in (dgpost_ref, dgpre_ref, dlng_ref, dlnb_ref, dws_ref, dbt_ref):
                r[...] = jnp.zeros_like(r)

        dy = dx1_ref[...]
        dm, dgpost = _rms_bwd(m_ref[...], gpost_ref[...], dy)
        dgpost_ref[...] += dgpost
        dm_bf = dm.astype(BF16)
        dm_ref[...] = dm_bf
        dgated = _nt(dm_bf, _two_d(wout_ref))

        zf = z_ref[...].astype(F32)
        u, du_dz = _gelu(zf[:, :aw])
        v, dv_dz = _gelu(zf[:, aw:])
        vh, rs = _layernorm_stats(v)
        lng_v = lng_ref[...]
        vn = (vh * lng_v + lnb_ref[...]).astype(BF16)
        mask = _tril_mask()
        lane = lax.broadcasted_iota(jnp.int32, (CHUNK, CHUNK), 1)
        for g in range(A_GROUPS):
            wm = jnp.where(mask, ws_ref[g], 0.0).astype(BF16)
            cols = slice(g * A_GROUP_DIM, (g + 1) * A_GROUP_DIM)
            dws_g = jnp.zeros((CHUNK, CHUNK), F32)
            db_g = jnp.zeros((CHUNK, 1), F32)
            for c in range(nch):
                rows = slice(c * CHUNK, (c + 1) * CHUNK)
                vn_cg = vn[rows, cols]
                sv = _nn(wm, vn_cg) + bst_ref[:, g:g + 1]
                dg_cg = dgated[rows, cols]
                dsv = dg_cg * u[rows, cols]
                dsv_bf = dsv.astype(BF16)
                db_g = db_g + jnp.sum(dsv, axis=1, keepdims=True)
                dws_g = dws_g + _nt(dsv_bf, vn_cg)
                dvn_ref[rows, cols] = _tn(wm, dsv_bf)
                dz_ref[rows, cols] = (dg_cg * sv * du_dz[rows, cols]).astype(BF16)
            dws_ref[g] += jnp.where(mask, dws_g, 0.0)
            dbt_ref[...] += jnp.where(lane == g, db_g, 0.0)
        dvn = dvn_ref[...]
        dlng_ref[...] += jnp.sum(dvn * vh, axis=0, keepdims=True)
        dlnb_ref[...] += jnp.sum(dvn, axis=0, keepdims=True)
        dvh = dvn * lng_v
        dv = rs * (dvh - jnp.mean(dvh, axis=-1, keepdims=True) - vh * jnp.mean(dvh * vh, axis=-1, keepdims=True))
        dz_ref[:, aw:] = (dv * dv_dz).astype(BF16)
        dh1 = _nn(dz_ref[...], _two_d(win_ref))
        dxp, dgpre = _rms_bwd(x_ref[...], gpre_ref[...], dh1)
        dgpre_ref[...] += dgpre
        dx_ref[...] = dy + dxp

    vec = lambda w: _resident((1, w), (0, 0))
    acc = lambda shape: pl.BlockSpec(shape, lambda i: (0,) * len(shape))
    return _call(
        body, name=name, grid=(t // tm,),
        in_specs=[_rows(tm, d), _rows(tm, d), _rows(tm, d), _rows(tm, 2 * aw), vec(d),
                  _wspec(lay.a_in_rows, lay.a_in[j], d), vec(aw), vec(aw),
                  _resident((A_GROUPS, CHUNK, CHUNK), (0, 0, 0)), _resident((CHUNK, A_GROUPS), (0, 0)),
                  _wspec(lay.a_out_rows, lay.a_out[j], d), vec(d)],
        out_specs=[_rows(tm, d), _rows(tm, d), _rows(tm, 2 * aw), acc((1, d)), acc((1, d)), acc((1, aw)), acc((1, aw)),
                   acc((A_GROUPS, CHUNK, CHUNK)), acc((CHUNK, CHUNK))],
        out_shape=[jax.ShapeDtypeStruct((t, d), F32), jax.ShapeDtypeStruct((t, d), BF16),
                   jax.ShapeDtypeStruct((t, 2 * aw), BF16), jax.ShapeDtypeStruct((1, d), F32),
                   jax.ShapeDtypeStruct((1, d), F32), jax.ShapeDtypeStruct((1, aw), F32),
                   jax.ShapeDtypeStruct((1, aw), F32), jax.ShapeDtypeStruct((A_GROUPS, CHUNK, CHUNK), F32),
                   jax.ShapeDtypeStruct((CHUNK, CHUNK), F32)],
        scratch_shapes=[pltpu.VMEM((tm, aw), F32)],
        compiler_params=_params("arbitrary"),
    )(dx1, m, x, z, gpre, wg, lng, lnb, ws, bst, wg, gpost)


def _window_counts(first_row, n, win):
    tpos = first_row + lax.broadcasted_iota(jnp.int32, (n, 1), 0)
    return jnp.clip(tpos + 1, 1, win).astype(F32)


def _b_fwd(x, gpre, wg, lay, j, wgrp, scale, gpost, tm, name):
    t, d = x.shape
    n = tm + HALO
    ngrp = len(B_WINDOWS)

    def body(x_ref, xprev_ref, gpre_ref, win_ref, wgrp_ref, scale_ref, wout_ref, gpost_ref,
             x1_ref, h1_ref, pooled_ref, mixed_ref, m_ref):
        i = pl.program_id(0)
        xv = x_ref[...]
        keep = jnp.where(i > 0, 1.0, 0.0)
        xe = jnp.concatenate([xprev_ref[...] * keep, xv], axis=0)
        h1e = _rms_fwd(xe, gpre_ref[...]).astype(BF16)
        h1_ref[...] = h1e[HALO:]
        p = _nn(h1e, _two_d(win_ref))
        acc = p
        shift = 1
        for g, win in enumerate(B_WINDOWS):
            lo = g * B_GROUP_DIM
            if g > 0:
                acc = acc[:, B_GROUP_DIM:]
            while shift < win:
                acc = acc + pltpu.roll(acc, shift, 0)
                shift *= 2
            cnt = _window_counts(i * tm - HALO, n, win)
            pooled = acc[:, :B_GROUP_DIM] / cnt - p[:, lo:lo + B_GROUP_DIM]
            pooled_ref[:, lo:lo + B_GROUP_DIM] = pooled[HALO:].astype(BF16)
        for g in range(ngrp):
            cols = slice(g * B_GROUP_DIM, (g + 1) * B_GROUP_DIM)
            raw = _nn(pooled_ref[:, cols], wgrp_ref[g])
            mixed_ref[:, cols] = (raw * scale_ref[:, cols]).astype(BF16)
        m = _nn(mixed_ref[...], _two_d(wout_ref))
        m_ref[...] = m
        x1_ref[...] = xv + _rms_fwd(m, gpost_ref[...])

    vec = lambda w: _resident((1, w), (0, 0))
    per = tm // HALO
    return _call(
        body, name=name, grid=(t // tm,),
        in_specs=[_rows(tm, d), pl.BlockSpec((HALO, d), lambda i: (jnp.maximum(i * per - 1, 0), 0)), vec(d),
                  _wspec(lay.b_rows, lay.b_in[j], d), _resident((ngrp, B_GROUP_DIM, B_GROUP_DIM), (0, 0, 0)), vec(d),
                  _wspec(lay.b_rows, lay.b_out[j], d), vec(d)],
        out_specs=[_rows(tm, d)] * 5,
        out_shape=[jax.ShapeDtypeStruct((t, d), F32), jax.ShapeDtypeStruct((t, d), BF16),
                   jax.ShapeDtypeStruct((t, d), BF16), jax.ShapeDtypeStruct((t, d), BF16),
                   jax.ShapeDtypeStruct((t, d), F32)],
        compiler_params=_params("parallel"),
    )(x, x, gpre, wg, wgrp, scale, wg, gpost)


def _b_bwd(dx1, m, x, pooled, gpre, wg, lay, j, wgrp, scale, gpost, tm, name):
    t, d = x.shape
    n = tm + HALO
    ngrp = len(B_WINDOWS)
    steps = t // tm

    def body(dx1_ref, dx1n_ref, m_ref, mn_ref, x_ref, pooled_ref, pooledn_ref, gpre_ref, win_ref, wgrp_ref, scale_ref,
             wout_ref, gpost_ref, dx_ref, dm_ref, draw_ref, dp_ref, dgpost_ref, dgpre_ref, dscale_ref, dpool_ref):
        i = pl.program_id(0)

        @pl.when(i == 0)
        def _():
            for r in (dgpost_ref, dgpre_ref, dscale_ref):
                r[...] = jnp.zeros_like(r)

        keep = jnp.where(i < steps - 1, 1.0, 0.0)
        dy = dx1_ref[...]
        dye = jnp.concatenate([dy, dx1n_ref[...] * keep], axis=0)
        me = jnp.concatenate([m_ref[...], mn_ref[...]], axis=0)
        gpost_v = gpost_ref[...]
        r = lax.rsqrt(jnp.mean(me * me, axis=-1, keepdims=True) + EPS)
        mh = me * r
        dgpost_ref[...] += jnp.sum((dye * mh)[:tm], axis=0, keepdims=True)
        dmh = dye * gpost_v
        dme = (r * (dmh - mh * jnp.mean(dmh * mh, axis=-1, keepdims=True))).astype(BF16)
        dm_ref[...] = dme[:tm]
        dmixed = _nt(dme, _two_d(wout_ref))
        pooled_e = jnp.concatenate([pooled_ref[...], pooledn_ref[...]], axis=0)
        scale_v = scale_ref[...]
        for g, win in enumerate(B_WINDOWS):
            cols = slice(g * B_GROUP_DIM, (g + 1) * B_GROUP_DIM)
            raw = _nn(pooled_e[:, cols], wgrp_ref[g])
            dscale_ref[:, cols] += jnp.sum((dmixed[:, cols] * raw)[:tm], axis=0, keepdims=True)
            draw = (dmixed[:, cols] * scale_v[:, cols]).astype(BF16)
            draw_ref[:, cols] = draw[:tm]
            dpool = _nt(draw, wgrp_ref[g])
            acc = dpool / _window_counts(i * tm, n, win)
            shift = 1
            while shift < win:
                acc = acc + pltpu.roll(acc, n - shift, 0)
                shift *= 2
            dpool_ref[:, cols] = (acc - dpool)[:tm]
        dp = dpool_ref[...].astype(BF16)
        dp_ref[...] = dp
        dh1 = _nt(dp, _two_d(win_ref))
        dxp, dgpre = _rms_bwd(x_ref[...], gpre_ref[...], dh1)
        dgpre_ref[...] += dgpre
        dx_ref[...] = dy + dxp

    vec = lambda w: _resident((1, w), (0, 0))
    acc = lambda shape: pl.BlockSpec(shape, lambda i: (0,) * len(shape))
    per = tm // HALO
    nxt = lambda i: (jnp.minimum((i + 1) * per, t // HALO - 1), 0)
    return _call(
        body, name=name, grid=(steps,),
        in_specs=[_rows(tm, d), pl.BlockSpec((HALO, d), nxt), _rows(tm, d), pl.BlockSpec((HALO, d), nxt), _rows(tm, d),
                  _rows(tm, d), pl.BlockSpec((HALO, d), nxt), vec(d), _wspec(lay.b_rows, lay.b_in[j], d),
                  _resident((ngrp, B_GROUP_DIM, B_GROUP_DIM), (0, 0, 0)), vec(d), _wspec(lay.b_rows, lay.b_out[j], d),
                  vec(d)],
        out_specs=[_rows(tm, d)] * 4 + [acc((1, d))] * 3,
        out_shape=[jax.ShapeDtypeStruct((t, d), F32), jax.ShapeDtypeStruct((t, d), BF16),
                   jax.ShapeDtypeStruct((t, d), BF16), jax.ShapeDtypeStruct((t, d), BF16)]
                  + [jax.ShapeDtypeStruct((1, d), F32)] * 3,
        scratch_shapes=[pltpu.VMEM((tm, d), F32)],
        compiler_params=_params("arbitrary"),
    )(dx1, dx1, m, m, x, pooled, pooled, gpre, wg, wgrp, scale, wg, gpost)


def _f_fwd(x1, gpre, wg, lay, l, gpost, tm, name):
    t, d = x1.shape
    hid = N_DEV * lay.ffn_rows

    def body(x_ref, gpre_ref, wgate_ref, wup_ref, wdown_ref, gpost_ref, x2_ref, h2_ref, a_ref, b_ref, s_ref, f_ref):
        xv = x_ref[...]
        h2 = _rms_fwd(xv, gpre_ref[...]).astype(BF16)
        h2_ref[...] = h2
        a = _nt(h2, _two_d(wgate_ref))
        b = _nt(h2, _two_d(wup_ref))
        a_ref[...] = a.astype(BF16)
        b_ref[...] = b.astype(BF16)
        s = (a * (1.0 / (1.0 + jnp.exp(-a))) * b).astype(BF16)
        s_ref[...] = s
        f = _nn(s, _two_d(wdown_ref))
        f_ref[...] = f
        x2_ref[...] = xv + _rms_fwd(f, gpost_ref[...])

    vec = lambda w: _resident((1, w), (0, 0))
    return _call(
        body, name=name, grid=(t // tm,),
        in_specs=[_rows(tm, d), vec(d), _wspec(lay.ffn_rows, lay.gate[l], d), _wspec(lay.ffn_rows, lay.up[l], d),
                  _wspec(lay.ffn_rows, lay.down[l], d), vec(d)],
        out_specs=[_rows(tm, d), _rows(tm, d), _rows(tm, hid), _rows(tm, hid), _rows(tm, hid), _rows(tm, d)],
        out_shape=[jax.ShapeDtypeStruct((t, d), F32), jax.ShapeDtypeStruct((t, d), BF16),
                   jax.ShapeDtypeStruct((t, hid), BF16), jax.ShapeDtypeStruct((t, hid), BF16),
                   jax.ShapeDtypeStruct((t, hid), BF16), jax.ShapeDtypeStruct((t, d), F32)],
        compiler_params=_params("parallel"),
    )(x1, gpre, wg, wg, wg, gpost)


def _f_bwd(dx2, f, x1, a, b, gpre, wg, lay, l, gpost, tm, name):
    t, d = x1.shape
    hid = N_DEV * lay.ffn_rows

    def body(dx2_ref, f_ref, x_ref, a_ref, b_ref, gpre_ref, wgate_ref, wup_ref, wdown_ref, gpost_ref,
             dx1_ref, df_ref, da_ref, db_ref, dgpost_ref, dgpre_ref):
        @pl.when(pl.program_id(0) == 0)
        def _():
            dgpost_ref[...] = jnp.zeros_like(dgpost_ref)
            dgpre_ref[...] = jnp.zeros_like(dgpre_ref)

        dy = dx2_ref[...]
        df, dgpost = _rms_bwd(f_ref[...], gpost_ref[...], dy)
        dgpost_ref[...] += dgpost
        df_bf = df.astype(BF16)
        df_ref[...] = df_bf
        ds = _nt(df_bf, _two_d(wdown_ref))
        av = a_ref[...].astype(F32)
        bv = b_ref[...].astype(F32)
        sig = 1.0 / (1.0 + jnp.exp(-av))
        da = (ds * bv * (sig * (1.0 + av * (1.0 - sig)))).astype(BF16)
        db = (ds * (av * sig)).astype(BF16)
        da_ref[...] = da
        db_ref[...] = db
        dh2 = _nn(da, _two_d(wgate_ref)) + _nn(db, _two_d(wup_ref))
        dxp, dgpre = _rms_bwd(x_ref[...], gpre_ref[...], dh2)
        dgpre_ref[...] += dgpre
        dx1_ref[...] = dy + dxp

    vec = lambda w: _resident((1, w), (0, 0))
    acc = pl.BlockSpec((1, d), lambda i: (0, 0))
    return _call(
        body, name=name, grid=(t // tm,),
        in_specs=[_rows(tm, d), _rows(tm, d), _rows(tm, d), _rows(tm, hid), _rows(tm, hid), vec(d),
                  _wspec(lay.ffn_rows, lay.gate[l], d), _wspec(lay.ffn_rows, lay.up[l], d),
                  _wspec(lay.ffn_rows, lay.down[l], d), vec(d)],
        out_specs=[_rows(tm, d), _rows(tm, d), _rows(tm, hid), _rows(tm, hid), acc, acc],
        out_shape=[jax.ShapeDtypeStruct((t, d), F32), jax.ShapeDtypeStruct((t, d), BF16),
                   jax.ShapeDtypeStruct((t, hid), BF16), jax.ShapeDtypeStruct((t, hid), BF16),
                   jax.ShapeDtypeStruct((1, d), F32), jax.ShapeDtypeStruct((1, d), F32)],
        compiler_params=_params("arbitrary"),
    )(dx2, f, x1, a, b, gpre, wg, wg, wg, gpost)


def _loss_head(y, target, tm, name):
    t, d = y.shape

    def body(y_ref, t_ref, dy_ref, loss_ref):
        @pl.when(pl.program_id(0) == 0)
        def _():
            loss_ref[...] = jnp.zeros_like(loss_ref)

        diff = y_ref[...] - t_ref[...]
        dy_ref[...] = diff * (1.0 / d)
        sq = jnp.sum(jnp.sum(diff * diff, axis=0, keepdims=True), axis=1, keepdims=True)
        loss_ref[...] += sq * (0.5 / d)

    return _call(
        body, name=name, grid=(t // tm,),
        in_specs=[_rows(tm, d), _rows(tm, d)],
        out_specs=[_rows(tm, d), pl.BlockSpec((8, 128), lambda i: (0, 0))],
        out_shape=[jax.ShapeDtypeStruct((t, d), F32), jax.ShapeDtypeStruct((8, 128), F32)],
        compiler_params=_params("arbitrary"),
    )(y, target)


def _grad_into(gbuf, lhs, rhs, off, rows, name):
    t, m = lhs.shape
    d = rhs.shape[1]
    assert m == N_DEV * rows and off % rows == 0
    per_tile = {384: 4, 512: 2, 256: 4, 128: 8}[rows]
    tm = per_tile * rows
    tk = 1024 if t % 1024 == 0 else 256
    ksteps = t // tk
    fresh = isinstance(gbuf, int)
    shape = (N_DEV, gbuf, d) if fresh else gbuf.shape

    def body(l_ref, r_ref, *rest):
        o_ref, acc_ref = rest[-2:]
        k = pl.program_id(1)

        @pl.when(k == 0)
        def _():
            acc_ref[...] = jnp.zeros_like(acc_ref)

        acc_ref[...] += _tn(l_ref[...], r_ref[...])

        @pl.when(k == ksteps - 1)
        def _():
            o_ref[...] = acc_ref[...].reshape(per_tile, rows, d).astype(BF16)

    return _call(
        body, name=name, grid=(N_DEV // per_tile, ksteps),
        in_specs=[pl.BlockSpec((tk, tm), lambda i, k: (k, i)), pl.BlockSpec((tk, d), lambda i, k: (k, 0))]
                 + ([] if fresh else [ANY]),
        out_specs=pl.BlockSpec((per_tile, rows, d), lambda i, k: (i, off // rows, 0)),
        out_shape=jax.ShapeDtypeStruct(shape, BF16),
        scratch_shapes=[pltpu.VMEM((tm, d), F32)],
        input_output_aliases={} if fresh else {2: 0},
        compiler_params=_params("parallel", "arbitrary"),
    )(*((lhs, rhs) if fresh else (lhs, rhs, gbuf)))


def _grad_grouped(pooled, draw, name):
    t, d = pooled.shape
    ngrp = len(B_WINDOWS)
    tk = 1024 if t % 1024 == 0 else 256

    def body(p_ref, q_ref, o_ref):
        @pl.when(pl.program_id(0) == 0)
        def _():
            o_ref[...] = jnp.zeros_like(o_ref)

        for g in range(ngrp):
            cols = slice(g * B_GROUP_DIM, (g + 1) * B_GROUP_DIM)
            o_ref[g] += _tn(p_ref[:, cols], q_ref[:, cols])

    return _call(
        body, name=name, grid=(t // tk,),
        in_specs=[_rows(tk, d), _rows(tk, d)],
        out_specs=pl.BlockSpec((ngrp, B_GROUP_DIM, B_GROUP_DIM), lambda i: (0, 0, 0)),
        out_shape=jax.ShapeDtypeStruct((ngrp, B_GROUP_DIM, B_GROUP_DIM), F32),
        compiler_params=_params("arbitrary"),
    )(pooled, draw)


def _place():
    x, y, c = lax.axis_index("x"), lax.axis_index("y"), lax.axis_index("c")
    chips = [(1 - x, y), (x, 1 - y), (1 - x, 1 - y)]
    return x, y, c, chips


def _all_gather(arrs, name):
    n = len(arrs)

    def body(*refs):
        ins, outs = refs[:n], refs[n:2 * n]
        send_sems, recv_sems, local_sems = refs[2 * n:]
        x, y, c, chips = _place()
        me, sibling = (x, y, c), (x, y, 1 - c)

        def slot(a, px, py, pc):
            return outs[a].at[4 * px + 2 * py + pc]

        def copy(a, k, block, to, src=None):
            return pltpu.make_async_remote_copy(
                src_ref=slot(a, *block) if src is None else src, dst_ref=slot(a, *block),
                send_sem=send_sems.at[a, k], recv_sem=recv_sems.at[a, k], device_id=to, device_id_type=MESH)

        mine = [pltpu.make_async_copy(ins[a], slot(a, *me), local_sems.at[a]) for a in range(n)]
        for cp in mine:
            cp.start()
        first = []
        for a in range(n):
            first.append(copy(a, 0, me, sibling, src=ins[a]))
            first += [copy(a, 1 + j, me, (*chip, c), src=ins[a]) for j, chip in enumerate(chips)]
        for cp in first:
            cp.start()
        passed = [[copy(a, 4 + j, (*chip, c), sibling) for j, chip in enumerate(chips)] for a in range(n)]
        for j, chip in enumerate(chips):
            for a in range(n):
                copy(a, 1 + j, (*chip, c), me).wait_recv()
                passed[a][j].start()
        for a in range(n):
            copy(a, 0, sibling, me).wait_recv()
            for j, chip in enumerate(chips):
                copy(a, 4 + j, (*chip, 1 - c), me).wait_recv()
        for cp in first + [p for ps in passed for p in ps]:
            cp.wait_send()
        for cp in mine:
            cp.wait()

    return _call(
        body, name=name,
        in_specs=[ANY] * n, out_specs=[ANY] * n,
        out_shape=[jax.ShapeDtypeStruct((N_DEV,) + a.shape, a.dtype) for a in arrs],
        scratch_shapes=[pltpu.SemaphoreType.DMA((n, 7)), pltpu.SemaphoreType.DMA((n, 7)),
                        pltpu.SemaphoreType.DMA((n,))],
    )(*arrs)


def _peers():
    x, y, c = lax.axis_index("x"), lax.axis_index("y"), lax.axis_index("c")
    flip = lambda v, f: 1 - v if f else v
    peers = []
    for r in range(1, N_DEV):
        px, py, pc = flip(x, r & 4), flip(y, r & 2), flip(c, r & 1)
        peers.append(((px, py, pc), 4 * px + 2 * py + pc))
    return 4 * x + 2 * y + c, peers


HBM = pl.BlockSpec(memory_space=pltpu.HBM)
SEM = pl.BlockSpec(memory_space=pltpu.SEMAPHORE)
EFFECT = pltpu.SideEffectType.DATAFLOW_SIDE_EFFECTING


def _peer_copies(scatter, srcs, lands, send_sems, recv_sems):
    me, peers = _peers()
    copies = []
    for a in range(len(srcs)):
        for r, (peer, pidx) in enumerate(peers):
            src = srcs[a].at[pidx] if scatter else srcs[a]
            mine = lands[a].at[r] if scatter else lands[a].at[pidx]
            theirs = lands[a].at[r] if scatter else lands[a].at[me]
            send = pltpu.make_async_remote_copy(src_ref=src, dst_ref=theirs, send_sem=send_sems[a].at[r],
                                                recv_sem=recv_sems[a].at[r], device_id=peer, device_id_type=MESH)
            recv = pltpu.make_async_remote_copy(src_ref=src, dst_ref=mine, send_sem=send_sems[a].at[r],
                                                recv_sem=recv_sems[a].at[r], device_id=peer, device_id_type=MESH)
            copies.append((send, recv))
    return copies


def _exchange_start(scatter, srcs, lands, after, name):
    n = len(srcs)

    def body(*refs):
        src_refs, land_refs = refs[:n], refs[n:2 * n]
        outs = refs[2 * n + 1:]
        send_sems, recv_sems, token = outs[:n], outs[n:2 * n], outs[-1]
        for send, _ in _peer_copies(scatter, src_refs, land_refs, send_sems, recv_sems):
            send.start()
        token[...] = jnp.zeros_like(token)

    hbm = lambda a: pltpu.with_memory_space_constraint(a, pltpu.HBM)
    res = _call(
        body, name=name,
        in_specs=[HBM] * (2 * n) + [ANY],
        out_specs=[SEM] * (2 * n) + [HBM] * (2 * n) + [pl.BlockSpec(memory_space=pltpu.VMEM)],
        out_shape=[pltpu.SemaphoreType.DMA((N_DEV - 1,))] * (2 * n)
                  + [pltpu.HBM(a.shape, a.dtype) for a in list(srcs) + list(lands)]
                  + [jax.ShapeDtypeStruct((8, 128), F32)],
        input_output_aliases={i: 2 * n + i for i in range(2 * n)},
        compiler_params=pltpu.CompilerParams(has_side_effects=EFFECT),
    )(*[hbm(a) for a in srcs], *[hbm(a) for a in lands], after)
    return res[:n], res[n:2 * n], res[2 * n:3 * n], res[3 * n:4 * n], res[-1]


def _exchange_wait(scatter, send_sems, recv_sems, srcs, lands, after, name):
    n = len(srcs)

    def body(*refs):
        src_refs, land_refs = refs[:n], refs[n:2 * n]
        send_refs, recv_refs = refs[2 * n:3 * n], refs[3 * n:4 * n]
        for send, recv in _peer_copies(scatter, src_refs, land_refs, send_refs, recv_refs):
            send.wait_send()
            recv.wait_recv()

    res = _call(
        body, name=name,
        in_specs=[HBM] * (2 * n) + [SEM] * (2 * n) + [ANY],
        out_specs=[HBM] * (2 * n),
        out_shape=[pltpu.HBM(a.shape, a.dtype) for a in list(srcs) + list(lands)],
        input_output_aliases={i: i for i in range(2 * n)},
        compiler_params=pltpu.CompilerParams(has_side_effects=EFFECT),
    )(*srcs, *lands, *send_sems, *recv_sems, after)
    return res[:n], res[n:]


def _exchange_small(parts, small, name):
    def body(parts_ref, small_ref, got_ref, all_ref, send_sems, recv_sems, local_sem):
        me, peers = _peers()
        copies = []
        for r, (peer, pidx) in enumerate(peers):
            copies.append(pltpu.make_async_remote_copy(
                src_ref=parts_ref.at[pidx], dst_ref=got_ref.at[r], send_sem=send_sems.at[0, r],
                recv_sem=recv_sems.at[0, r], device_id=peer, device_id_type=MESH))
            copies.append(pltpu.make_async_remote_copy(
                src_ref=small_ref, dst_ref=all_ref.at[me], send_sem=send_sems.at[1, r],
                recv_sem=recv_sems.at[1, r], device_id=peer, device_id_type=MESH))
        mine = pltpu.make_async_copy(small_ref, all_ref.at[me], local_sem.at[0])
        mine.start()
        for cp in copies:
            cp.start()
        for cp in copies:
            cp.wait()
        mine.wait()

    return _call(
        body, name=name,
        in_specs=[ANY] * 2, out_specs=[ANY] * 2,
        out_shape=[jax.ShapeDtypeStruct((N_DEV - 1,) + parts.shape[1:], parts.dtype),
                   jax.ShapeDtypeStruct((N_DEV,) + small.shape, small.dtype)],
        scratch_shapes=[pltpu.SemaphoreType.DMA((2, N_DEV - 1)), pltpu.SemaphoreType.DMA((2, N_DEV - 1)),
                        pltpu.SemaphoreType.DMA((1,))],
    )(parts, small)


def _row_tile(rows):
    if rows <= 512:
        return rows
    for tr in (512, 384, 256, 128, 64, 32, 16, 8):
        if rows % tr == 0:
            return tr
    return rows


def _sum_parts(own, got, me, name):
    _, rows, w = own.shape
    tr = _row_tile(rows)

    def body(me_ref, a_ref, b_ref, o_ref):
        s = a_ref[...].astype(F32)
        for j in range(N_DEV - 1):
            s = s + b_ref[j].astype(F32)
        o_ref[...] = s

    return _call(
        body, name=name,
        grid_spec=pltpu.PrefetchScalarGridSpec(
            num_scalar_prefetch=1, grid=(rows // tr,),
            in_specs=[pl.BlockSpec((None, tr, w), lambda i, me_ref: (me_ref[0], i, 0)),
                      pl.BlockSpec((N_DEV - 1, tr, w), lambda i, me_ref: (0, i, 0))],
            out_specs=pl.BlockSpec((tr, w), lambda i, me_ref: (i, 0))),
        out_shape=jax.ShapeDtypeStruct((rows, w), F32),
        compiler_params=_params("parallel"),
    )(me, own, got)


def _sum_devices(stacked, name):
    k, rows, w = stacked.shape
    tr = _row_tile(rows)

    def body(a_ref, o_ref):
        s = a_ref[0]
        for j in range(1, k):
            s = s + a_ref[j]
        o_ref[...] = s

    return _call(
        body, name=name, grid=(rows // tr,),
        in_specs=[pl.BlockSpec((k, tr, w), lambda i: (0, i, 0))],
        out_specs=pl.BlockSpec((tr, w), lambda i: (i, 0)),
        out_shape=jax.ShapeDtypeStruct((rows, w), F32),
        compiler_params=_params("parallel"),
    )(stacked)


def _adamw(w, g, m, v, name):
    rows, cols = w.shape
    tr = _row_tile(rows)

    def body(w_ref, g_ref, m_ref, v_ref, d_ref, nm_ref, nv_ref):
        gv = g_ref[...]
        nm = ADAM_B1 * m_ref[...] + (1.0 - ADAM_B1) * gv
        nv = ADAM_B2 * v_ref[...] + (1.0 - ADAM_B2) * (gv * gv)
        m_hat = nm / (1.0 - ADAM_B1 ** ADAM_STEP)
        v_hat = nv / (1.0 - ADAM_B2 ** ADAM_STEP)
        d_ref[...] = -ADAM_LR * (m_hat / (jnp.sqrt(v_hat) + ADAM_EPS) + ADAM_WD * w_ref[...])
        nm_ref[...] = nm
        nv_ref[...] = nv

    spec = pl.BlockSpec((tr, cols), lambda i: (i, 0))
    return _call(
        body, name=name, grid=(rows // tr,),
        in_specs=[spec] * 4, out_specs=[spec] * 3,
        out_shape=[jax.ShapeDtypeStruct((rows, cols), F32)] * 3,
        compiler_params=_params("parallel"),
    )(w, g, m, v)


SMALL = ("a_ln_g", "a_ln_b", "a_w_s", "a_b_s", "mix_pre_g", "mix_post_g", "ffn_pre_g", "ffn_post_g")


def _pack_small(parts, d):
    flat = jnp.concatenate([parts[k].reshape(-1, d) for k in SMALL], axis=0)
    return jnp.pad(flat, ((0, -flat.shape[0] % 8), (0, 0)))


def _unpack_small(flat, like):
    out, r = {}, 0
    for k in SMALL:
        n = like[k].size // flat.shape[1]
        out[k] = flat[r:r + n].reshape(like[k].shape)
        r += n
    return out


def kernel(x, a_w_in, a_ln_g, a_ln_b, a_w_s, a_b_s, a_w_out, b_w_in, b_w_grp, b_scale, b_w_out, mix_pre_g, mix_post_g, ffn_pre_g, ffn_post_g, ffn_w_gate, ffn_w_up, ffn_w_down, loss_target, m_a_w_in, m_a_ln_g, m_a_ln_b, m_a_w_s, m_a_b_s, m_a_w_out, m_b_w_in, m_b_w_grp, m_b_scale, m_b_w_out, m_mix_pre_g, m_mix_post_g, m_ffn_pre_g, m_ffn_post_g, m_ffn_w_gate, m_ffn_w_up, m_ffn_w_down, v_a_w_in, v_a_ln_g, v_a_ln_b, v_a_w_s, v_a_b_s, v_a_w_out, v_b_w_in, v_b_w_grp, v_b_scale, v_b_w_out, v_mix_pre_g, v_mix_post_g, v_ffn_pre_g, v_ffn_post_g, v_ffn_w_gate, v_ffn_w_up, v_ffn_w_down):
    args = dict(locals())
    names = ("a_w_in", "a_ln_g", "a_ln_b", "a_w_s", "a_b_s", "a_w_out", "b_w_in", "b_w_grp", "b_scale", "b_w_out",
             "mix_pre_g", "mix_post_g", "ffn_pre_g", "ffn_post_g", "ffn_w_gate", "ffn_w_up", "ffn_w_down")
    w = {k: args[k] for k in names}
    mom = {k: args["m_" + k] for k in names}
    var = {k: args["v_" + k] for k in names}

    t, d = x.shape[1], x.shape[2]
    lay = _Layout(d)
    ffn_local = ffn_w_gate.shape[2]
    ffn_pad = lay.ffn_rows - ffn_local
    me = 4 * lax.axis_index("x") + 2 * lax.axis_index("y") + lax.axis_index("c")
    me1 = jnp.reshape(me, (1,)).astype(jnp.int32)

    def ffn_t(wl):
        return jnp.pad(wl.T, ((0, ffn_pad), (0, 0)))

    packed = []
    for i in range(DEPTH):
        j = i // 2
        mixer = [a_w_in[j].T, a_w_out[j]] if i % 2 == 0 else [b_w_in[j], b_w_out[j]]
        ffn = [ffn_t(ffn_w_gate[i]), ffn_t(ffn_w_up[i]), jnp.pad(ffn_w_down[i], ((0, ffn_pad), (0, 0)))]
        packed += [jnp.concatenate(mixer, axis=0).astype(BF16), jnp.concatenate(ffn, axis=0).astype(BF16)]
    nsub = len(packed)
    ngrp = len(B_WINDOWS)
    grp_local = b_w_grp.shape[2]
    sdev = b_scale.shape[1]
    side_rows = 2 * ngrp * grp_local
    side = jnp.concatenate(
        [b_w_grp.reshape(side_rows, B_GROUP_DIM),
         jnp.pad(b_scale, ((0, 6), (0, B_GROUP_DIM - sdev)))], axis=0)

    def landing(block):
        zone = jnp.zeros((N_DEV,) + block.shape, block.dtype)
        return lax.dynamic_update_slice(zone, block[None], (me,) + (0,) * block.ndim)

    wg = [None] * nsub
    (wg[0],) = _all_gather([packed[0]], "gather_first")
    later = [side] + packed[1:]
    send_sems, recv_sems, later, zones, token = _exchange_start(
        False, later, [landing(b) for b in later], wg[0], "gather_start")

    def gathered(k, after):
        _, (zone,) = _exchange_wait(False, [send_sems[k]], [recv_sems[k]], [later[k]], [zones[k]], after,
                                    f"gather_wait_{k}")
        return zone

    row = lambda a: a.reshape(1, -1)
    bst = jnp.transpose(a_b_s, (0, 2, 1))

    tm = 256 if t % 256 == 0 else CHUNK
    tm_abwd = CHUNK

    saved = []
    h = x[0]
    wgrp_full = scale_full = None
    for i in range(DEPTH):
        j = i // 2
        gpre = row(mix_pre_g[i]) + token[:1, :1] if i == 0 else row(mix_pre_g[i])
        if i > 0:
            wg[2 * i] = gathered(2 * i, h)
        if i % 2 == 0:
            x1, h1, z, gated, m = _a_fwd(h, gpre, wg[2 * i], lay, j, row(a_ln_g[j]), row(a_ln_b[j]), a_w_s[j],
                                         bst[j], row(mix_post_g[i]), tm, f"a_fwd_{j}")
            mix = dict(h1=h1, z=z, gated=gated, m=m)
        else:
            if wgrp_full is None:
                side_g = gathered(0, h)
                wgrp_full = (side_g[:, :side_rows].reshape(N_DEV, 2, ngrp, grp_local, B_GROUP_DIM)
                             .transpose(1, 2, 0, 3, 4).reshape(2, ngrp, B_GROUP_DIM, B_GROUP_DIM).astype(BF16))
                scale_full = (side_g[:, side_rows:side_rows + 2, :sdev].transpose(1, 0, 2)
                              .reshape(2, 1, N_DEV * sdev))
            x1, h1, pooled, mixed, m = _b_fwd(h, gpre, wg[2 * i], lay, j, wgrp_full[j], scale_full[j],
                                              row(mix_post_g[i]), tm, f"b_fwd_{j}")
            mix = dict(h1=h1, pooled=pooled, mixed=mixed, m=m)
        wg[2 * i + 1] = gathered(2 * i + 1, x1)
        x2, h2, a, b, s, f = _f_fwd(x1, row(ffn_pre_g[i]), wg[2 * i + 1], lay, i, row(ffn_post_g[i]), tm,
                                    f"f_fwd_{i}")
        saved.append(dict(x=h, x1=x1, mix=mix, h2=h2, a=a, b=b, s=s, f=f))
        h = x2

    dy, loss_acc = _loss_head(h, loss_target[0], tm, "loss_head")
    loss = lax.psum(loss_acc[0, 0], ("x", "y", "c"))

    small_g = {k: [None] * w[k].shape[0] for k in SMALL}
    dgrp, dscale = [None, None], [None, None]
    pending = [None] * nsub
    token = jnp.zeros((8, 128), F32)

    def scatter(k, gbuf):
        got = pltpu.with_memory_space_constraint(lax.empty((N_DEV - 1,) + gbuf.shape[1:], BF16), pltpu.HBM)
        ss, rs, src, zone, tok = _exchange_start(True, [gbuf], [got], token, f"scatter_start_{k}")
        pending[k] = (ss, rs, src, zone)
        return tok

    for i in reversed(range(DEPTH)):
        sv = saved[i]
        j = i // 2
        wf, wm = wg[2 * i + 1], wg[2 * i]
        dx1, df, da, db, dgpost, dgpre = _f_bwd(dy, sv["f"], sv["x1"], sv["a"], sv["b"], row(ffn_pre_g[i]), wf, lay, i,
                                                 row(ffn_post_g[i]) + token[:1, :1], tm, f"f_bwd_{i}")
        small_g["ffn_post_g"][i], small_g["ffn_pre_g"][i] = dgpost[0], dgpre[0]
        gbuf = _grad_into(lay.f_total, da, sv["h2"], lay.gate[i], lay.ffn_rows, f"g_gate_{i}")
        gbuf = _grad_into(gbuf, db, sv["h2"], lay.up[i], lay.ffn_rows, f"g_up_{i}")
        gbuf = _grad_into(gbuf, sv["s"], df, lay.down[i], lay.ffn_rows, f"g_down_{i}")
        token = scatter(2 * i + 1, gbuf)
        mix = sv["mix"]
        gpost = row(mix_post_g[i]) + token[:1, :1]
        if i % 2 == 0:
            dx, dm, dz, dgpost, dgpre, dlng, dlnb, dws, dbt = _a_bwd(
                dx1, mix["m"], sv["x"], mix["z"], row(mix_pre_g[i]), wm, lay, j, row(a_ln_g[j]), row(a_ln_b[j]),
                a_w_s[j], bst[j], gpost, tm_abwd, f"a_bwd_{j}")
            small_g["a_ln_g"][j], small_g["a_ln_b"][j] = dlng[0], dlnb[0]
            small_g["a_w_s"][j], small_g["a_b_s"][j] = dws, dbt[:, :A_GROUPS].T
            gbuf = _grad_into(lay.a_total, dz, mix["h1"], lay.a_in[j], lay.a_in_rows, f"g_a_in_{j}")
            gbuf = _grad_into(gbuf, mix["gated"], dm, lay.a_out[j], lay.a_out_rows, f"g_a_out_{j}")
        else:
            dx, dm, draw, dp, dgpost, dgpre, dsc = _b_bwd(
                dx1, mix["m"], sv["x"], mix["pooled"], row(mix_pre_g[i]), wm, lay, j, wgrp_full[j], scale_full[j],
                gpost, tm, f"b_bwd_{j}")
            dscale[j] = dsc[0]
            dgrp[j] = _grad_grouped(mix["pooled"], draw, f"g_b_grp_{j}")
            gbuf = _grad_into(lay.b_total, mix["h1"], dp, lay.b_in[j], lay.b_rows, f"g_b_in_{j}")
            gbuf = _grad_into(gbuf, mix["mixed"], dm, lay.b_out[j], lay.b_rows, f"g_b_out_{j}")
        token = scatter(2 * i, gbuf)
        small_g["mix_post_g"][i], small_g["mix_pre_g"][i] = dgpost[0], dgpre[0]
        dy = dx
    grad_x = dy[None]

    side_grad = jnp.concatenate(
        [jnp.stack(dgrp).reshape(2, ngrp, N_DEV, grp_local, B_GROUP_DIM).transpose(2, 0, 1, 3, 4)
         .reshape(N_DEV, side_rows, B_GROUP_DIM),
         jnp.pad(jnp.stack(dscale).reshape(2, N_DEV, sdev).transpose(1, 0, 2),
                 ((0, 0), (0, 6), (0, B_GROUP_DIM - sdev)))], axis=1)
    small_part = _pack_small({k: jnp.stack(small_g[k]) for k in SMALL}, d)

    far_s, small_all = _exchange_small(side_grad, small_part, "exchange_small")
    g_side = _sum_parts(side_grad, far_s, me1, "sum_side")
    g_small = _unpack_small(_sum_devices(small_all, "sum_small"), w)
    g_sub = []
    for k in range(nsub):
        ss, rs, src, zone = pending[k]
        (own,), (got,) = _exchange_wait(True, ss, rs, src, zone, dy, f"scatter_wait_{k}")
        g_sub.append(_sum_parts(own, got, me1, f"sum_grads_{k}"))

    def rows_of(k, off, n):
        return g_sub[k][off:off + n]

    grads = dict(g_small)
    grads["ffn_w_gate"] = jnp.stack([rows_of(2 * l + 1, lay.gate[l], ffn_local).T for l in range(DEPTH)])
    grads["ffn_w_up"] = jnp.stack([rows_of(2 * l + 1, lay.up[l], ffn_local).T for l in range(DEPTH)])
    grads["ffn_w_down"] = jnp.stack([rows_of(2 * l + 1, lay.down[l], ffn_local) for l in range(DEPTH)])
    grads["a_w_in"] = jnp.stack([rows_of(4 * j, lay.a_in[j], lay.a_in_rows).T for j in range(2)])
    grads["a_w_out"] = jnp.stack([rows_of(4 * j, lay.a_out[j], lay.a_out_rows) for j in range(2)])
    grads["b_w_in"] = jnp.stack([rows_of(4 * j + 2, lay.b_in[j], lay.b_rows) for j in range(2)])
    grads["b_w_out"] = jnp.stack([rows_of(4 * j + 2, lay.b_out[j], lay.b_rows) for j in range(2)])
    grads["b_w_grp"] = g_side[:side_rows].reshape(b_w_grp.shape)
    grads["b_scale"] = g_side[side_rows:side_rows + 2, :sdev]

    delta, new_m, new_v = {}, {}, {}
    big = [k for k in names if k not in SMALL]
    for k in big:
        shape = w[k].shape
        two = lambda a: a.reshape(-1, shape[-1])
        dl, nm, nv = _adamw(two(w[k]), two(grads[k]), two(mom[k]), two(var[k]), f"adamw_{k}")
        delta[k], new_m[k], new_v[k] = dl.reshape(shape), nm.reshape(shape), nv.reshape(shape)
    dl, nm, nv = _adamw(_pack_small(w, d), _pack_small(g_small, d), _pack_small(mom, d), _pack_small(var, d),
                        "adamw_small")
    delta.update(_unpack_small(dl, w))
    new_m.update(_unpack_small(nm, w))
    new_v.update(_unpack_small(nv, w))

    return (loss, grad_x, *[grads[k] for k in names], *[delta[k] for k in names], *[new_m[k] for k in names],
            *[new_v[k] for k in names])
```

```python
import math

import jax
import jax.numpy as jnp
from jax import lax
from jax.experimental import pallas as pl
from jax.experimental.pallas import tpu as pltpu

F32 = jnp.float32
BF16 = jnp.bfloat16
MESH = pl.DeviceIdType.MESH
ANY = pl.BlockSpec(memory_space=pl.ANY)

N_DEV = 8
EPS = 1e-6
CHUNK = 128
A_GROUPS = 8
A_GROUP_DIM = 256
B_WINDOWS = (2, 4, 8, 16)
B_GROUP_DIM = 256
HALO = 16
DEPTH = 4

ADAM_LR = 0.001
ADAM_B1 = 0.9
ADAM_B2 = 0.999
ADAM_EPS = 1e-08
ADAM_WD = 0.01
ADAM_STEP = 10

VMEM_LIMIT_BYTES = 60 * 1024 * 1024

ERF_P = 0.3275911
ERF_A = (0.254829592, -0.284496736, 1.421413741, -1.453152027, 1.061405429)
INV_SQRT2 = 1.0 / math.sqrt(2.0)
INV_SQRT_2PI = 1.0 / math.sqrt(2.0 * math.pi)


def _call(body, **kw):
    return pl.pallas_call(body, **kw)


def _params(*semantics):
    return pltpu.CompilerParams(dimension_semantics=semantics or None, vmem_limit_bytes=VMEM_LIMIT_BYTES)


def _resident(shape, index):
    return pl.BlockSpec(shape, lambda *_: index, pipeline_mode=pl.Buffered(1))


def _rows(tm, width):
    return pl.BlockSpec((tm, width), lambda i: (i, 0))


def _nn(a, b):
    return jnp.dot(a, b, preferred_element_type=F32)


def _nt(a, b):
    return lax.dot_general(a, b, (((1,), (1,)), ((), ())), preferred_element_type=F32)


def _tn(a, b):
    return lax.dot_general(a, b, (((0,), (0,)), ((), ())), preferred_element_type=F32)


def _rms_fwd(x, g):
    r = lax.rsqrt(jnp.mean(x * x, axis=-1, keepdims=True) + EPS)
    return x * r * g


def _rms_bwd(x, g, dy):
    r = lax.rsqrt(jnp.mean(x * x, axis=-1, keepdims=True) + EPS)
    xh = x * r
    dg = jnp.sum(dy * xh, axis=0, keepdims=True)
    dxh = dy * g
    dx = r * (dxh - xh * jnp.mean(dxh * xh, axis=-1, keepdims=True))
    return dx, dg


def _gelu(z):
    a = jnp.abs(z) * INV_SQRT2
    t = 1.0 / (1.0 + ERF_P * a)
    e = jnp.exp(-a * a)
    poly = t * (ERF_A[0] + t * (ERF_A[1] + t * (ERF_A[2] + t * (ERF_A[3] + t * ERF_A[4]))))
    half = 0.5 * poly * e
    phi = jnp.where(z >= 0, 1.0 - half, half)
    return z * phi, phi + z * e * INV_SQRT_2PI


def _layernorm_stats(v):
    mu = jnp.mean(v, axis=-1, keepdims=True)
    xc = v - mu
    rs = lax.rsqrt(jnp.mean(xc * xc, axis=-1, keepdims=True) + EPS)
    return xc * rs, rs


def _tril_mask():
    r = lax.broadcasted_iota(jnp.int32, (CHUNK, CHUNK), 0)
    c = lax.broadcasted_iota(jnp.int32, (CHUNK, CHUNK), 1)
    return r >= c


def _two_d(ref):
    k, r, d = ref.shape
    return ref[...].reshape(k * r, d)


class _Layout:
    def __init__(self, d):
        self.ffn_rows = 384
        self.gate, self.up, self.down = [0] * DEPTH, [self.ffn_rows] * DEPTH, [2 * self.ffn_rows] * DEPTH
        self.f_total = 3 * self.ffn_rows
        self.a_in_rows, self.a_out_rows, self.b_rows = 4 * d // N_DEV, 2 * d // N_DEV, d // N_DEV
        self.a_in, self.a_out = [0, 0], [self.a_in_rows] * 2
        self.a_total = self.a_in_rows + self.a_out_rows
        self.b_in, self.b_out = [0, 0], [self.b_rows] * 2
        self.b_total = 2 * self.b_rows


def _wspec(rows, off, d):
    assert off % rows == 0
    return _resident((N_DEV, rows, d), (0, off // rows, 0))


def _a_fwd(x, gpre, wg, lay, j, lng, lnb, ws, bst, gpost, tm, name):
    t, d = x.shape
    aw = 2 * d
    nch = tm // CHUNK

    def body(x_ref, gpre_ref, win_ref, lng_ref, lnb_ref, ws_ref, bst_ref, wout_ref, gpost_ref,
             x1_ref, h1_ref, gp_ref, u_ref, vh_ref, rs_ref, gated_ref, m_ref):
        xv = x_ref[...]
        h1 = _rms_fwd(xv, gpre_ref[...]).astype(BF16)
        h1_ref[...] = h1
        z = _nt(h1, _two_d(win_ref))
        u, du_dz = _gelu(z[:, :aw])
        v, dv_dz = _gelu(z[:, aw:])
        gp_ref[:, :aw] = du_dz.astype(BF16)
        gp_ref[:, aw:] = dv_dz.astype(BF16)
        u_ref[...] = u.astype(BF16)
        vh, rs = _layernorm_stats(v)
        vh_ref[...] = vh.astype(BF16)
        rs_ref[...] = jnp.broadcast_to(rs, rs_ref.shape)
        vn = (vh * lng_ref[...] + lnb_ref[...]).astype(BF16)
        mask = _tril_mask()
        for g in range(A_GROUPS):
            wm = jnp.where(mask, ws_ref[g], 0.0).astype(BF16)
            cols = slice(g * A_GROUP_DIM, (g + 1) * A_GROUP_DIM)
            for c in range(nch):
                rows = slice(c * CHUNK, (c + 1) * CHUNK)
                sv = _nn(wm, vn[rows, cols]) + bst_ref[:, g:g + 1]
                gated_ref[rows, cols] = (u[rows, cols] * sv).astype(BF16)
        m = _nn(gated_ref[...], _two_d(wout_ref))
        m_ref[...] = m
        x1_ref[...] = xv + _rms_fwd(m, gpost_ref[...])

    vec = lambda w: _resident((1, w), (0, 0))
    return _call(
        body, name=name, grid=(t // tm,),
        in_specs=[_rows(tm, d), vec(d), _wspec(lay.a_in_rows, lay.a_in[j], d), vec(aw), vec(aw),
                  _resident((A_GROUPS, CHUNK, CHUNK), (0, 0, 0)), _resident((CHUNK, A_GROUPS), (0, 0)),
                  _wspec(lay.a_out_rows, lay.a_out[j], d), vec(d)],
        out_specs=[_rows(tm, d), _rows(tm, d), _rows(tm, 2 * aw), _rows(tm, aw), _rows(tm, aw), _rows(tm, 128),
                   _rows(tm, aw), _rows(tm, d)],
        out_shape=[jax.ShapeDtypeStruct((t, d), F32), jax.ShapeDtypeStruct((t, d), BF16),
                   jax.ShapeDtypeStruct((t, 2 * aw), BF16), jax.ShapeDtypeStruct((t, aw), BF16),
                   jax.ShapeDtypeStruct((t, aw), BF16), jax.ShapeDtypeStruct((t, 128), F32),
                   jax.ShapeDtypeStruct((t, aw), BF16), jax.ShapeDtypeStruct((t, d), F32)],
        compiler_params=_params("parallel"),
    )(x, gpre, wg, lng, lnb, ws, bst, wg, gpost)


def _a_bwd(dx1, m, x, gp, u, vh, rs, gpre, wg, lay, j, lng, lnb, ws, bst, gpost, tm, name):
    t, d = x.shape
    aw = 2 * d
    nch = tm // CHUNK

    def body(dx1_ref, m_ref, x_ref, gp_ref, u_ref, vh_ref, rs_ref, gpre_ref, win_ref, lng_ref, lnb_ref, ws_ref, bst_ref,
             wout_ref, gpost_ref,
             dx_ref, dm_ref, dz_ref, dgpost_ref, dgpre_ref, dlng_ref, dlnb_ref, dws_ref, dbt_ref, dvn_ref):
        @pl.when(pl.program_id(0) == 0)
        def _():
            for r in (dgpost_ref, dgpre_ref, dlng_ref, dlnb_ref, dws_ref, dbt_ref):
                r[...] = jnp.zeros_like(r)

        dy = dx1_ref[...]
        dm, dgpost = _rms_bwd(m_ref[...], gpost_ref[...], dy)
        dgpost_ref[...] += dgpost
        dm_bf = dm.astype(BF16)
        dm_ref[...] = dm_bf
        dgated = _nt(dm_bf, _two_d(wout_ref))

        vh = vh_ref[...].astype(F32)
        rs = rs_ref[:, :1]
        lng_v = lng_ref[...]
        vn = (vh * lng_v + lnb_ref[...]).astype(BF16)
        mask = _tril_mask()
        lane = lax.broadcasted_iota(jnp.int32, (CHUNK, CHUNK), 1)
        for g in range(A_GROUPS):
            wm = jnp.where(mask, ws_ref[g], 0.0).astype(BF16)
            cols = slice(g * A_GROUP_DIM, (g + 1) * A_GROUP_DIM)
            dws_g = jnp.zeros((CHUNK, CHUNK), F32)
            db_g = jnp.zeros((CHUNK, 1), F32)
            for c in range(nch):
                rows = slice(c * CHUNK, (c + 1) * CHUNK)
                vn_cg = vn[rows, cols]
                sv = _nn(wm, vn_cg) + bst_ref[:, g:g + 1]
                dg_cg = dgated[rows, cols]
                dsv = dg_cg * u_ref[rows, cols].astype(F32)
                dsv_bf = dsv.astype(BF16)
                db_g = db_g + jnp.sum(dsv, axis=1, keepdims=True)
                dws_g = dws_g + _nt(dsv_bf, vn_cg)
                dvn_ref[rows, cols] = _tn(wm, dsv_bf)
                dz_ref[rows, cols] = (dg_cg * sv * gp_ref[rows, cols].astype(F32)).astype(BF16)
            dws_ref[g] += jnp.where(mask, dws_g, 0.0)
            dbt_ref[...] += jnp.where(lane == g, db_g, 0.0)
        dvn = dvn_ref[...]
        dlng_ref[...] += jnp.sum(dvn * vh, axis=0, keepdims=True)
        dlnb_ref[...] += jnp.sum(dvn, axis=0, keepdims=True)
        dvh = dvn * lng_v
        dv = rs * (dvh - jnp.mean(dvh, axis=-1, keepdims=True) - vh * jnp.mean(dvh * vh, axis=-1, keepdims=True))
        dz_ref[:, aw:] = (dv * gp_ref[:, aw:].astype(F32)).astype(BF16)
        dh1 = _nn(dz_ref[...], _two_d(win_ref))
        dxp, dgpre = _rms_bwd(x_ref[...], gpre_ref[...], dh1)
        dgpre_ref[...] += dgpre
        dx_ref[...] = dy + dxp

    vec = lambda w: _resident((1, w), (0, 0))
    acc = lambda shape: pl.BlockSpec(shape, lambda i: (0,) * len(shape))
    return _call(
        body, name=name, grid=(t // tm,),
        in_specs=[_rows(tm, d), _rows(tm, d), _rows(tm, d), _rows(tm, 2 * aw), _rows(tm, aw), _rows(tm, aw),
                  _rows(tm, 128), vec(d), _wspec(lay.a_in_rows, lay.a_in[j], d), vec(aw), vec(aw),
                  _resident((A_GROUPS, CHUNK, CHUNK), (0, 0, 0)), _resident((CHUNK, A_GROUPS), (0, 0)),
                  _wspec(lay.a_out_rows, lay.a_out[j], d), vec(d)],
        out_specs=[_rows(tm, d), _rows(tm, d), _rows(tm, 2 * aw), acc((1, d)), acc((1, d)), acc((1, aw)), acc((1, aw)),
                   acc((A_GROUPS, CHUNK, CHUNK)), acc((CHUNK, CHUNK))],
        out_shape=[jax.ShapeDtypeStruct((t, d), F32), jax.ShapeDtypeStruct((t, d), BF16),
                   jax.ShapeDtypeStruct((t, 2 * aw), BF16), jax.ShapeDtypeStruct((1, d), F32),
                   jax.ShapeDtypeStruct((1, d), F32), jax.ShapeDtypeStruct((1, aw), F32),
                   jax.ShapeDtypeStruct((1, aw), F32), jax.ShapeDtypeStruct((A_GROUPS, CHUNK, CHUNK), F32),
                   jax.ShapeDtypeStruct((CHUNK, CHUNK), F32)],
        scratch_shapes=[pltpu.VMEM((tm, aw), F32)],
        compiler_params=_params("arbitrary"),
    )(dx1, m, x, gp, u, vh, rs, gpre, wg, lng, lnb, ws, bst, wg, gpost)


def _window_counts(first_row, n, win):
    tpos = first_row + lax.broadcasted_iota(jnp.int32, (n, 1), 0)
    return jnp.clip(tpos + 1, 1, win).astype(F32)


def _b_fwd(x, gpre, wg, lay, j, wgrp, scale, gpost, tm, name):
    t, d = x.shape
    n = tm + HALO
    ngrp = len(B_WINDOWS)

    def body(x_ref, xprev_ref, gpre_ref, win_ref, wgrp_ref, scale_ref, wout_ref, gpost_ref,
             x1_ref, h1_ref, pooled_ref, mixed_ref, m_ref):
        i = pl.program_id(0)
        xv = x_ref[...]
        keep = jnp.where(i > 0, 1.0, 0.0)
        xe = jnp.concatenate([xprev_ref[...] * keep, xv], axis=0)
        h1e = _rms_fwd(xe, gpre_ref[...]).astype(BF16)
        h1_ref[...] = h1e[HALO:]
        p = _nn(h1e, _two_d(win_ref))
        acc = p
        shift = 1
        for g, win in enumerate(B_WINDOWS):
            lo = g * B_GROUP_DIM
            if g > 0:
                acc = acc[:, B_GROUP_DIM:]
            while shift < win:
                acc = acc + pltpu.roll(acc, shift, 0)
                shift *= 2
            cnt = _window_counts(i * tm - HALO, n, win)
            pooled = acc[:, :B_GROUP_DIM] / cnt - p[:, lo:lo + B_GROUP_DIM]
            pooled_ref[:, lo:lo + B_GROUP_DIM] = pooled[HALO:].astype(BF16)
        for g in range(ngrp):
            cols = slice(g * B_GROUP_DIM, (g + 1) * B_GROUP_DIM)
            raw = _nn(pooled_ref[:, cols], wgrp_ref[g])
            mixed_ref[:, cols] = (raw * scale_ref[:, cols]).astype(BF16)
        m = _nn(mixed_ref[...], _two_d(wout_ref))
        m_ref[...] = m
        x1_ref[...] = xv + _rms_fwd(m, gpost_ref[...])

    vec = lambda w: _resident((1, w), (0, 0))
    per = tm // HALO
    return _call(
        body, name=name, grid=(t // tm,),
        in_specs=[_rows(tm, d), pl.BlockSpec((HALO, d), lambda i: (jnp.maximum(i * per - 1, 0), 0)), vec(d),
                  _wspec(lay.b_rows, lay.b_in[j], d), _resident((ngrp, B_GROUP_DIM, B_GROUP_DIM), (0, 0, 0)), vec(d),
                  _wspec(lay.b_rows, lay.b_out[j], d), vec(d)],
        out_specs=[_rows(tm, d)] * 5,
        out_shape=[jax.ShapeDtypeStruct((t, d), F32), jax.ShapeDtypeStruct((t, d), BF16),
                   jax.ShapeDtypeStruct((t, d), BF16), jax.ShapeDtypeStruct((t, d), BF16),
                   jax.ShapeDtypeStruct((t, d), F32)],
        compiler_params=_params("parallel"),
    )(x, x, gpre, wg, wgrp, scale, wg, gpost)


def _b_bwd(dx1, m, x, pooled, gpre, wg, lay, j, wgrp, scale, gpost, tm, name):
    t, d = x.shape
    n = tm + HALO
    ngrp = len(B_WINDOWS)
    steps = t // tm

    def body(dx1_ref, dx1n_ref, m_ref, mn_ref, x_ref, pooled_ref, pooledn_ref, gpre_ref, win_ref, wgrp_ref, scale_ref,
             wout_ref, gpost_ref, dx_ref, dm_ref, draw_ref, dp_ref, dgpost_ref, dgpre_ref, dscale_ref, dpool_ref):
        i = pl.program_id(0)

        @pl.when(i == 0)
        def _():
            for r in (dgpost_ref, dgpre_ref, dscale_ref):
                r[...] = jnp.zeros_like(r)

        keep = jnp.where(i < steps - 1, 1.0, 0.0)
        dy = dx1_ref[...]
        dye = jnp.concatenate([dy, dx1n_ref[...] * keep], axis=0)
        me = jnp.concatenate([m_ref[...], mn_ref[...]], axis=0)
        gpost_v = gpost_ref[...]
        r = lax.rsqrt(jnp.mean(me * me, axis=-1, keepdims=True) + EPS)
        mh = me * r
        dgpost_ref[...] += jnp.sum((dye * mh)[:tm], axis=0, keepdims=True)
        dmh = dye * gpost_v
        dme = (r * (dmh - mh * jnp.mean(dmh * mh, axis=-1, keepdims=True))).astype(BF16)
        dm_ref[...] = dme[:tm]
        dmixed = _nt(dme, _two_d(wout_ref))
        pooled_e = jnp.concatenate([pooled_ref[...], pooledn_ref[...]], axis=0)
        scale_v = scale_ref[...]
        for g, win in enumerate(B_WINDOWS):
            cols = slice(g * B_GROUP_DIM, (g + 1) * B_GROUP_DIM)
            raw = _nn(pooled_e[:, cols], wgrp_ref[g])
            dscale_ref[:, cols] += jnp.sum((dmixed[:, cols] * raw)[:tm], axis=0, keepdims=True)
            draw = (dmixed[:, cols] * scale_v[:, cols]).astype(BF16)
            draw_ref[:, cols] = draw[:tm]
            dpool = _nt(draw, wgrp_ref[g])
            acc = dpool / _window_counts(i * tm, n, win)
            shift = 1
            while shift < win:
                acc = acc + pltpu.roll(acc, n - shift, 0)
                shift *= 2
            dpool_ref[:, cols] = (acc - dpool)[:tm]
        dp = dpool_ref[...].astype(BF16)
        dp_ref[...] = dp
        dh1 = _nt(dp, _two_d(win_ref))
        dxp, dgpre = _rms_bwd(x_ref[...], gpre_ref[...], dh1)
        dgpre_ref[...] += dgpre
        dx_ref[...] = dy + dxp

    vec = lambda w: _resident((1, w), (0, 0))
    acc = lambda shape: pl.BlockSpec(shape, lambda i: (0,) * len(shape))
    per = tm // HALO
    nxt = lambda i: (jnp.minimum((i + 1) * per, t // HALO - 1), 0)
    return _call(
        body, name=name, grid=(steps,),
        in_specs=[_rows(tm, d), pl.BlockSpec((HALO, d), nxt), _rows(tm, d), pl.BlockSpec((HALO, d), nxt), _rows(tm, d),
                  _rows(tm, d), pl.BlockSpec((HALO, d), nxt), vec(d), _wspec(lay.b_rows, lay.b_in[j], d),
                  _resident((ngrp, B_GROUP_DIM, B_GROUP_DIM), (0, 0, 0)), vec(d), _wspec(lay.b_rows, lay.b_out[j], d),
                  vec(d)],
        out_specs=[_rows(tm, d)] * 4 + [acc((1, d))] * 3,
        out_shape=[jax.ShapeDtypeStruct((t, d), F32), jax.ShapeDtypeStruct((t, d), BF16),
                   jax.ShapeDtypeStruct((t, d), BF16), jax.ShapeDtypeStruct((t, d), BF16)]
                  + [jax.ShapeDtypeStruct((1, d), F32)] * 3,
        scratch_shapes=[pltpu.VMEM((tm, d), F32)],
        compiler_params=_params("arbitrary"),
    )(dx1, dx1, m, m, x, pooled, pooled, gpre, wg, wgrp, scale, wg, gpost)


def _f_fwd(x1, gpre, wg, lay, l, gpost, tm, name):
    t, d = x1.shape
    hid = N_DEV * lay.ffn_rows

    def body(x_ref, gpre_ref, wgate_ref, wup_ref, wdown_ref, gpost_ref, x2_ref, h2_ref, a_ref, b_ref, s_ref, f_ref):
        xv = x_ref[...]
        h2 = _rms_fwd(xv, gpre_ref[...]).astype(BF16)
        h2_ref[...] = h2
        a = _nt(h2, _two_d(wgate_ref))
        b = _nt(h2, _two_d(wup_ref))
        a_ref[...] = a.astype(BF16)
        b_ref[...] = b.astype(BF16)
        s = (a * (1.0 / (1.0 + jnp.exp(-a))) * b).astype(BF16)
        s_ref[...] = s
        f = _nn(s, _two_d(wdown_ref))
        f_ref[...] = f
        x2_ref[...] = xv + _rms_fwd(f, gpost_ref[...])

    vec = lambda w: _resident((1, w), (0, 0))
    return _call(
        body, name=name, grid=(t // tm,),
        in_specs=[_rows(tm, d), vec(d), _wspec(lay.ffn_rows, lay.gate[l], d), _wspec(lay.ffn_rows, lay.up[l], d),
                  _wspec(lay.ffn_rows, lay.down[l], d), vec(d)],
        out_specs=[_rows(tm, d), _rows(tm, d), _rows(tm, hid), _rows(tm, hid), _rows(tm, hid), _rows(tm, d)],
        out_shape=[jax.ShapeDtypeStruct((t, d), F32), jax.ShapeDtypeStruct((t, d), BF16),
                   jax.ShapeDtypeStruct((t, hid), BF16), jax.ShapeDtypeStruct((t, hid), BF16),
                   jax.ShapeDtypeStruct((t, hid), BF16), jax.ShapeDtypeStruct((t, d), F32)],
        compiler_params=_params("parallel"),
    )(x1, gpre, wg, wg, wg, gpost)


def _f_bwd(dx2, f, x1, a, b, gpre, wg, lay, l, gpost, tm, name):
    t, d = x1.shape
    hid = N_DEV * lay.ffn_rows

    def body(dx2_ref, f_ref, x_ref, a_ref, b_ref, gpre_ref, wgate_ref, wup_ref, wdown_ref, gpost_ref,
             dx1_ref, df_ref, da_ref, db_ref, dgpost_ref, dgpre_ref):
        @pl.when(pl.program_id(0) == 0)
        def _():
            dgpost_ref[...] = jnp.zeros_like(dgpost_ref)
            dgpre_ref[...] = jnp.zeros_like(dgpre_ref)

        dy = dx2_ref[...]
        df, dgpost = _rms_bwd(f_ref[...], gpost_ref[...], dy)
        dgpost_ref[...] += dgpost
        df_bf = df.astype(BF16)
        df_ref[...] = df_bf
        ds = _nt(df_bf, _two_d(wdown_ref))
        av = a_ref[...].astype(F32)
        bv = b_ref[...].astype(F32)
        sig = 1.0 / (1.0 + jnp.exp(-av))
        da = (ds * bv * (sig * (1.0 + av * (1.0 - sig)))).astype(BF16)
        db = (ds * (av * sig)).astype(BF16)
        da_ref[...] = da
        db_ref[...] = db
        dh2 = _nn(da, _two_d(wgate_ref)) + _nn(db, _two_d(wup_ref))
        dxp, dgpre = _rms_bwd(x_ref[...], gpre_ref[...], dh2)
        dgpre_ref[...] += dgpre
        dx1_ref[...] = dy + dxp

    vec = lambda w: _resident((1, w), (0, 0))
    acc = pl.BlockSpec((1, d), lambda i: (0, 0))
    return _call(
        body, name=name, grid=(t // tm,),
        in_specs=[_rows(tm, d), _rows(tm, d), _rows(tm, d), _rows(tm, hid), _rows(tm, hid), vec(d),
                  _wspec(lay.ffn_rows, lay.gate[l], d), _wspec(lay.ffn_rows, lay.up[l], d),
                  _wspec(lay.ffn_rows, lay.down[l], d), vec(d)],
        out_specs=[_rows(tm, d), _rows(tm, d), _rows(tm, hid), _rows(tm, hid), acc, acc],
        out_shape=[jax.ShapeDtypeStruct((t, d), F32), jax.ShapeDtypeStruct((t, d), BF16),
                   jax.ShapeDtypeStruct((t, hid), BF16), jax.ShapeDtypeStruct((t, hid), BF16),
                   jax.ShapeDtypeStruct((1, d), F32), jax.ShapeDtypeStruct((1, d), F32)],
        compiler_params=_params("arbitrary"),
    )(dx2, f, x1, a, b, gpre, wg, wg, wg, gpost)


def _loss_head(y, target, tm, name):
    t, d = y.shape

    def body(y_ref, t_ref, dy_ref, loss_ref):
        @pl.when(pl.program_id(0) == 0)
        def _():
            loss_ref[...] = jnp.zeros_like(loss_ref)

        diff = y_ref[...] - t_ref[...]
        dy_ref[...] = diff * (1.0 / d)
        sq = jnp.sum(jnp.sum(diff * diff, axis=0, keepdims=True), axis=1, keepdims=True)
        loss_ref[...] += sq * (0.5 / d)

    return _call(
        body, name=name, grid=(t // tm,),
        in_specs=[_rows(tm, d), _rows(tm, d)],
        out_specs=[_rows(tm, d), pl.BlockSpec((8, 128), lambda i: (0, 0))],
        out_shape=[jax.ShapeDtypeStruct((t, d), F32), jax.ShapeDtypeStruct((8, 128), F32)],
        compiler_params=_params("arbitrary"),
    )(y, target)


def _grad_into(gbuf, lhs, rhs, off, rows, name, after=None):
    t, m = lhs.shape
    d = rhs.shape[1]
    assert m == N_DEV * rows and off % rows == 0
    per_tile = {384: 4, 512: 2, 256: 4, 128: 8}[rows]
    tm = per_tile * rows
    tk = 2048 if t % 2048 == 0 else 256
    ksteps = t // tk
    fresh = isinstance(gbuf, int)
    shape = (N_DEV, gbuf, d) if fresh else gbuf.shape
    extra = ([] if fresh else [gbuf]) + ([] if after is None else [after])

    def body(l_ref, r_ref, *rest):
        o_ref, acc_ref = rest[-2:]
        k = pl.program_id(1)

        @pl.when(k == 0)
        def _():
            acc_ref[...] = jnp.zeros_like(acc_ref)

        acc_ref[...] += _tn(l_ref[...], r_ref[...])

        @pl.when(k == ksteps - 1)
        def _():
            o_ref[...] = acc_ref[...].reshape(per_tile, rows, d).astype(BF16)

    return _call(
        body, name=name, grid=(N_DEV // per_tile, ksteps),
        in_specs=[pl.BlockSpec((tk, tm), lambda i, k: (k, i)), pl.BlockSpec((tk, d), lambda i, k: (k, 0))]
                 + [ANY] * len(extra),
        out_specs=pl.BlockSpec((per_tile, rows, d), lambda i, k: (i, off // rows, 0)),
        out_shape=jax.ShapeDtypeStruct(shape, BF16),
        scratch_shapes=[pltpu.VMEM((tm, d), F32)],
        input_output_aliases={} if fresh else {2: 0},
        compiler_params=_params("parallel", "arbitrary"),
    )(lhs, rhs, *extra)


def _grad_grouped(pooled, draw, name):
    t, d = pooled.shape
    ngrp = len(B_WINDOWS)
    tk = 1024 if t % 1024 == 0 else 256

    def body(p_ref, q_ref, o_ref):
        @pl.when(pl.program_id(0) == 0)
        def _():
            o_ref[...] = jnp.zeros_like(o_ref)

        for g in range(ngrp):
            cols = slice(g * B_GROUP_DIM, (g + 1) * B_GROUP_DIM)
            o_ref[g] += _tn(p_ref[:, cols], q_ref[:, cols])

    return _call(
        body, name=name, grid=(t // tk,),
        in_specs=[_rows(tk, d), _rows(tk, d)],
        out_specs=pl.BlockSpec((ngrp, B_GROUP_DIM, B_GROUP_DIM), lambda i: (0, 0, 0)),
        out_shape=jax.ShapeDtypeStruct((ngrp, B_GROUP_DIM, B_GROUP_DIM), F32),
        compiler_params=_params("arbitrary"),
    )(pooled, draw)


def _peers():
    x, y, c = lax.axis_index("x"), lax.axis_index("y"), lax.axis_index("c")
    flip = lambda v, f: 1 - v if f else v
    peers = []
    for r in range(1, N_DEV):
        px, py, pc = flip(x, r & 4), flip(y, r & 2), flip(c, r & 1)
        peers.append(((px, py, pc), 4 * px + 2 * py + pc))
    return 4 * x + 2 * y + c, peers


HBM = pl.BlockSpec(memory_space=pltpu.HBM)
SEM = pl.BlockSpec(memory_space=pltpu.SEMAPHORE)
EFFECT = pltpu.SideEffectType.DATAFLOW_SIDE_EFFECTING


def _peer_copies(scatter, srcs, lands, send_sems, recv_sems):
    me, peers = _peers()
    copies = []
    for a in range(len(srcs)):
        for r, (peer, pidx) in enumerate(peers):
            src = srcs[a].at[pidx] if scatter else srcs[a]
            mine = lands[a].at[r] if scatter else lands[a].at[pidx]
            theirs = lands[a].at[r] if scatter else lands[a].at[me]
            send = pltpu.make_async_remote_copy(src_ref=src, dst_ref=theirs, send_sem=send_sems[a].at[r],
                                                recv_sem=recv_sems[a].at[r], device_id=peer, device_id_type=MESH)
            recv = pltpu.make_async_remote_copy(src_ref=src, dst_ref=mine, send_sem=send_sems[a].at[r],
                                                recv_sem=recv_sems[a].at[r], device_id=peer, device_id_type=MESH)
            copies.append((send, recv))
    return copies


def _exchange_start(scatter, srcs, lands, after, name):
    n = len(srcs)

    def body(*refs):
        src_refs, land_refs = refs[:n], refs[n:2 * n]
        outs = refs[2 * n + 1:]
        send_sems, recv_sems, token = outs[:n], outs[n:2 * n], outs[-1]
        for send, _ in _peer_copies(scatter, src_refs, land_refs, send_sems, recv_sems):
            send.start()
        token[...] = jnp.zeros_like(token)

    hbm = lambda a: pltpu.with_memory_space_constraint(a, pltpu.HBM)
    res = _call(
        body, name=name,
        in_specs=[HBM] * (2 * n) + [ANY],
        out_specs=[SEM] * (2 * n) + [HBM] * (2 * n) + [pl.BlockSpec(memory_space=pltpu.VMEM)],
        out_shape=[pltpu.SemaphoreType.DMA((N_DEV - 1,))] * (2 * n)
                  + [pltpu.HBM(a.shape, a.dtype) for a in list(srcs) + list(lands)]
                  + [jax.ShapeDtypeStruct((8, 128), F32)],
        input_output_aliases={i: 2 * n + i for i in range(2 * n)},
        compiler_params=pltpu.CompilerParams(has_side_effects=EFFECT),
    )(*[hbm(a) for a in srcs], *[hbm(a) for a in lands], after)
    return res[:n], res[n:2 * n], res[2 * n:3 * n], res[3 * n:4 * n], res[-1]


def _exchange_wait(scatter, send_sems, recv_sems, srcs, lands, after, name):
    n = len(srcs)

    def body(*refs):
        src_refs, land_refs = refs[:n], refs[n:2 * n]
        send_refs, recv_refs = refs[2 * n:3 * n], refs[3 * n:4 * n]
        for send, recv in _peer_copies(scatter, src_refs, land_refs, send_refs, recv_refs):
            send.wait_send()
            recv.wait_recv()

    res = _call(
        body, name=name,
        in_specs=[HBM] * (2 * n) + [SEM] * (2 * n) + [ANY],
        out_specs=[HBM] * (2 * n),
        out_shape=[pltpu.HBM(a.shape, a.dtype) for a in list(srcs) + list(lands)],
        input_output_aliases={i: i for i in range(2 * n)},
        compiler_params=pltpu.CompilerParams(has_side_effects=EFFECT),
    )(*srcs, *lands, *send_sems, *recv_sems, after)
    return res[:n], res[n:]


def _row_tile(rows):
    if rows <= 512:
        return rows
    for tr in (512, 384, 256, 128, 64, 32, 16, 8):
        if rows % tr == 0:
            return tr
    return rows


def _sum_parts(own, got, me, name):
    _, rows, w = own.shape
    tr = _row_tile(rows)

    def body(me_ref, a_ref, b_ref, o_ref):
        s = a_ref[...].astype(F32)
        for j in range(N_DEV - 1):
            s = s + b_ref[j].astype(F32)
        o_ref[...] = s

    return _call(
        body, name=name,
        grid_spec=pltpu.PrefetchScalarGridSpec(
            num_scalar_prefetch=1, grid=(rows // tr,),
            in_specs=[pl.BlockSpec((None, tr, w), lambda i, me_ref: (me_ref[0], i, 0)),
                      pl.BlockSpec((N_DEV - 1, tr, w), lambda i, me_ref: (0, i, 0))],
            out_specs=pl.BlockSpec((tr, w), lambda i, me_ref: (i, 0))),
        out_shape=jax.ShapeDtypeStruct((rows, w), F32),
        compiler_params=_params("parallel"),
    )(me, own, got)


def _sum_devices(stacked, name):
    k, rows, w = stacked.shape
    tr = _row_tile(rows)

    def body(a_ref, o_ref):
        s = a_ref[0]
        for j in range(1, k):
            s = s + a_ref[j]
        o_ref[...] = s

    return _call(
        body, name=name, grid=(rows // tr,),
        in_specs=[pl.BlockSpec((k, tr, w), lambda i: (0, i, 0))],
        out_specs=pl.BlockSpec((tr, w), lambda i: (i, 0)),
        out_shape=jax.ShapeDtypeStruct((rows, w), F32),
        compiler_params=_params("parallel"),
    )(stacked)


def _adamw(w, g, m, v, name):
    rows, cols = w.shape
    tr = _row_tile(rows)

    def body(w_ref, g_ref, m_ref, v_ref, d_ref, nm_ref, nv_ref):
        gv = g_ref[...]
        nm = ADAM_B1 * m_ref[...] + (1.0 - ADAM_B1) * gv
        nv = ADAM_B2 * v_ref[...] + (1.0 - ADAM_B2) * (gv * gv)
        m_hat = nm / (1.0 - ADAM_B1 ** ADAM_STEP)
        v_hat = nv / (1.0 - ADAM_B2 ** ADAM_STEP)
        d_ref[...] = -ADAM_LR * (m_hat / (jnp.sqrt(v_hat) + ADAM_EPS) + ADAM_WD * w_ref[...])
        nm_ref[...] = nm
        nv_ref[...] = nv

    spec = pl.BlockSpec((tr, cols), lambda i: (i, 0))
    return _call(
        body, name=name, grid=(rows // tr,),
        in_specs=[spec] * 4, out_specs=[spec] * 3,
        out_shape=[jax.ShapeDtypeStruct((rows, cols), F32)] * 3,
        compiler_params=_params("parallel"),
    )(w, g, m, v)


SMALL = ("a_ln_g", "a_ln_b", "a_w_s", "a_b_s", "mix_pre_g", "mix_post_g", "ffn_pre_g", "ffn_post_g")


def _pack_small(parts, d):
    flat = jnp.concatenate([parts[k].reshape(-1, d) for k in SMALL], axis=0)
    return jnp.pad(flat, ((0, -flat.shape[0] % 8), (0, 0)))


def _unpack_small(flat, like):
    out, r = {}, 0
    for k in SMALL:
        n = like[k].size // flat.shape[1]
        out[k] = flat[r:r + n].reshape(like[k].shape)
        r += n
    return out


def kernel(x, a_w_in, a_ln_g, a_ln_b, a_w_s, a_b_s, a_w_out, b_w_in, b_w_grp, b_scale, b_w_out, mix_pre_g, mix_post_g, ffn_pre_g, ffn_post_g, ffn_w_gate, ffn_w_up, ffn_w_down, loss_target, m_a_w_in, m_a_ln_g, m_a_ln_b, m_a_w_s, m_a_b_s, m_a_w_out, m_b_w_in, m_b_w_grp, m_b_scale, m_b_w_out, m_mix_pre_g, m_mix_post_g, m_ffn_pre_g, m_ffn_post_g, m_ffn_w_gate, m_ffn_w_up, m_ffn_w_down, v_a_w_in, v_a_ln_g, v_a_ln_b, v_a_w_s, v_a_b_s, v_a_w_out, v_b_w_in, v_b_w_grp, v_b_scale, v_b_w_out, v_mix_pre_g, v_mix_post_g, v_ffn_pre_g, v_ffn_post_g, v_ffn_w_gate, v_ffn_w_up, v_ffn_w_down):
    args = dict(locals())
    names = ("a_w_in", "a_ln_g", "a_ln_b", "a_w_s", "a_b_s", "a_w_out", "b_w_in", "b_w_grp", "b_scale", "b_w_out",
             "mix_pre_g", "mix_post_g", "ffn_pre_g", "ffn_post_g", "ffn_w_gate", "ffn_w_up", "ffn_w_down")
    w = {k: args[k] for k in names}
    mom = {k: args["m_" + k] for k in names}
    var = {k: args["v_" + k] for k in names}

    t, d = x.shape[1], x.shape[2]
    lay = _Layout(d)
    ffn_local = ffn_w_gate.shape[2]
    ffn_pad = lay.ffn_rows - ffn_local
    me = 4 * lax.axis_index("x") + 2 * lax.axis_index("y") + lax.axis_index("c")
    me1 = jnp.reshape(me, (1,)).astype(jnp.int32)

    def ffn_t(wl):
        return jnp.pad(wl.T, ((0, ffn_pad), (0, 0)))

    packed = []
    for i in range(DEPTH):
        j = i // 2
        mixer = [a_w_in[j].T, a_w_out[j]] if i % 2 == 0 else [b_w_in[j], b_w_out[j]]
        ffn = [ffn_t(ffn_w_gate[i]), ffn_t(ffn_w_up[i]), jnp.pad(ffn_w_down[i], ((0, ffn_pad), (0, 0)))]
        packed += [jnp.concatenate(mixer, axis=0).astype(BF16), jnp.concatenate(ffn, axis=0).astype(BF16)]
    nsub = len(packed)
    ngrp = len(B_WINDOWS)
    grp_local = b_w_grp.shape[2]
    sdev = b_scale.shape[1]
    side_rows = 2 * ngrp * grp_local
    side = jnp.concatenate(
        [b_w_grp.reshape(side_rows, B_GROUP_DIM),
         jnp.pad(b_scale, ((0, 6), (0, B_GROUP_DIM - sdev)))], axis=0)

    def landing(block):
        zone = jnp.zeros((N_DEV,) + block.shape, block.dtype)
        return lax.dynamic_update_slice(zone, block[None], (me,) + (0,) * block.ndim)

    wg = [None] * nsub
    first = _exchange_start(False, [packed[0]], [landing(packed[0])], jnp.zeros((8, 128), F32), "gather_first_start")
    later = [side] + packed[1:]
    send_sems, recv_sems, later, zones, token = _exchange_start(
        False, later, [landing(b) for b in later], first[4], "gather_start")
    _, (wg[0],) = _exchange_wait(False, *first[:4], token, "gather_first_wait")

    def gathered(k, after):
        _, (zone,) = _exchange_wait(False, [send_sems[k]], [recv_sems[k]], [later[k]], [zones[k]], after,
                                    f"gather_wait_{k}")
        return zone

    row = lambda a: a.reshape(1, -1)
    bst = jnp.transpose(a_b_s, (0, 2, 1))

    tm = 256 if t % 256 == 0 else CHUNK
    tm_abwd = tm

    saved = []
    h = x[0]
    wgrp_full = scale_full = None
    for i in range(DEPTH):
        j = i // 2
        gpre = row(mix_pre_g[i])
        if i > 0:
            wg[2 * i] = gathered(2 * i, h)
        if i % 2 == 0:
            x1, h1, gp, u, vh, rs, gated, m = _a_fwd(h, gpre, wg[2 * i], lay, j, row(a_ln_g[j]), row(a_ln_b[j]),
                                                     a_w_s[j], bst[j], row(mix_post_g[i]), tm, f"a_fwd_{j}")
            mix = dict(h1=h1, gp=gp, u=u, vh=vh, rs=rs, gated=gated, m=m)
        else:
            if wgrp_full is None:
                side_g = gathered(0, h)
                wgrp_full = (side_g[:, :side_rows].reshape(N_DEV, 2, ngrp, grp_local, B_GROUP_DIM)
                             .transpose(1, 2, 0, 3, 4).reshape(2, ngrp, B_GROUP_DIM, B_GROUP_DIM).astype(BF16))
                scale_full = (side_g[:, side_rows:side_rows + 2, :sdev].transpose(1, 0, 2)
                              .reshape(2, 1, N_DEV * sdev))
            x1, h1, pooled, mixed, m = _b_fwd(h, gpre, wg[2 * i], lay, j, wgrp_full[j], scale_full[j],
                                              row(mix_post_g[i]), tm, f"b_fwd_{j}")
            mix = dict(h1=h1, pooled=pooled, mixed=mixed, m=m)
        wg[2 * i + 1] = gathered(2 * i + 1, x1)
        x2, h2, a, b, s, f = _f_fwd(x1, row(ffn_pre_g[i]), wg[2 * i + 1], lay, i, row(ffn_post_g[i]), tm,
                                    f"f_fwd_{i}")
        saved.append(dict(x=h, x1=x1, mix=mix, h2=h2, a=a, b=b, s=s, f=f))
        h = x2

    dy, loss_acc = _loss_head(h, loss_target[0], tm, "loss_head")
    loss = lax.psum(loss_acc[0, 0], ("x", "y", "c"))

    small_g = {k: [None] * w[k].shape[0] for k in SMALL}
    dgrp, dscale = [None, None], [None, None]
    pending = [None] * nsub
    token = jnp.zeros((8, 128), F32)

    def scatter(k, gbuf):
        got = pltpu.with_memory_space_constraint(lax.empty((N_DEV - 1,) + gbuf.shape[1:], gbuf.dtype), pltpu.HBM)
        ss, rs, src, zone, tok = _exchange_start(True, [gbuf], [got], token, f"scatter_start_{k}")
        pending[k] = (ss, rs, src, zone)
        return tok

    def small_exchanges():
        side_grad = jnp.concatenate(
            [jnp.stack(dgrp).reshape(2, ngrp, N_DEV, grp_local, B_GROUP_DIM).transpose(2, 0, 1, 3, 4)
             .reshape(N_DEV, side_rows, B_GROUP_DIM),
             jnp.pad(jnp.stack(dscale).reshape(2, N_DEV, sdev).transpose(1, 0, 2),
                     ((0, 0), (0, 6), (0, B_GROUP_DIM - sdev)))], axis=1)
        small_part = _pack_small({k: jnp.stack(small_g[k]) for k in SMALL}, d)
        got = pltpu.with_memory_space_constraint(lax.empty((N_DEV - 1,) + side_grad.shape[1:], F32), pltpu.HBM)
        side_x = _exchange_start(True, [side_grad], [got], token, "side_scatter_start")
        small_x = _exchange_start(False, [small_part], [landing(small_part)], side_x[4], "small_gather_start")
        return side_x[:4], small_x[:4], small_x[4]

    for i in reversed(range(DEPTH)):
        sv = saved[i]
        j = i // 2
        wf, wm = wg[2 * i + 1], wg[2 * i]
        dx1, df, da, db, dgpost, dgpre = _f_bwd(dy, sv["f"], sv["x1"], sv["a"], sv["b"], row(ffn_pre_g[i]), wf, lay, i,
                                                 row(ffn_post_g[i]) + token[:1, :1], tm, f"f_bwd_{i}")
        small_g["ffn_post_g"][i], small_g["ffn_pre_g"][i] = dgpost[0], dgpre[0]
        gbuf = _grad_into(lay.f_total, da, sv["h2"], lay.gate[i], lay.ffn_rows, f"g_gate_{i}")
        gbuf = _grad_into(gbuf, db, sv["h2"], lay.up[i], lay.ffn_rows, f"g_up_{i}")
        gbuf = _grad_into(gbuf, sv["s"], df, lay.down[i], lay.ffn_rows, f"g_down_{i}")
        token = scatter(2 * i + 1, gbuf)
        mix = sv["mix"]
        gpost = row(mix_post_g[i]) + token[:1, :1]
        if i % 2 == 0:
            dx, dm, dz, dgpost, dgpre, dlng, dlnb, dws, dbt = _a_bwd(
                dx1, mix["m"], sv["x"], mix["gp"], mix["u"], mix["vh"], mix["rs"], row(mix_pre_g[i]), wm, lay, j,
                row(a_ln_g[j]), row(a_ln_b[j]), a_w_s[j], bst[j], gpost, tm_abwd, f"a_bwd_{j}")
            small_g["a_ln_g"][j], small_g["a_ln_b"][j] = dlng[0], dlnb[0]
            small_g["a_w_s"][j], small_g["a_b_s"][j] = dws, dbt[:, :A_GROUPS].T
            small_g["mix_post_g"][i], small_g["mix_pre_g"][i] = dgpost[0], dgpre[0]
            order = None
            if i == 0:
                side_x, small_x, order = small_exchanges()
            gbuf = _grad_into(lay.a_total, dz, mix["h1"], lay.a_in[j], lay.a_in_rows, f"g_a_in_{j}", after=order)
            gbuf = _grad_into(gbuf, mix["gated"], dm, lay.a_out[j], lay.a_out_rows, f"g_a_out_{j}")
        else:
            dx, dm, draw, dp, dgpost, dgpre, dsc = _b_bwd(
                dx1, mix["m"], sv["x"], mix["pooled"], row(mix_pre_g[i]), wm, lay, j, wgrp_full[j], scale_full[j],
                gpost, tm, f"b_bwd_{j}")
            dscale[j] = dsc[0]
            dgrp[j] = _grad_grouped(mix["pooled"], draw, f"g_b_grp_{j}")
            gbuf = _grad_into(lay.b_total, mix["h1"], dp, lay.b_in[j], lay.b_rows, f"g_b_in_{j}")
            gbuf = _grad_into(gbuf, mix["mixed"], dm, lay.b_out[j], lay.b_rows, f"g_b_out_{j}")
            small_g["mix_post_g"][i], small_g["mix_pre_g"][i] = dgpost[0], dgpre[0]
        token = scatter(2 * i, gbuf)
        dy = dx
    grad_x = dy[None]

    g_sub = [None] * nsub

    def arrived(k, after):
        ss, rs, src, zone = pending[k]
        (own,), (got,) = _exchange_wait(True, ss, rs, src, zone, after, f"scatter_wait_{k}")
        g_sub[k] = _sum_parts(own, got, me1, f"sum_grads_{k}")

    def rows_of(k, off, n):
        return g_sub[k][off:off + n]

    grads, delta, new_m, new_v = {}, {}, {}, {}

    def update(k):
        shape = w[k].shape
        two = lambda a: a.reshape(-1, shape[-1])
        dl, nm, nv = _adamw(two(w[k]), two(grads[k]), two(mom[k]), two(var[k]), f"adamw_{k}")
        delta[k], new_m[k], new_v[k] = dl.reshape(shape), nm.reshape(shape), nv.reshape(shape)

    for k in range(1, nsub):
        arrived(k, dy)
    grads["ffn_w_gate"] = jnp.stack([rows_of(2 * l + 1, lay.gate[l], ffn_local).T for l in range(DEPTH)])
    grads["ffn_w_up"] = jnp.stack([rows_of(2 * l + 1, lay.up[l], ffn_local).T for l in range(DEPTH)])
    grads["ffn_w_down"] = jnp.stack([rows_of(2 * l + 1, lay.down[l], ffn_local) for l in range(DEPTH)])
    grads["b_w_in"] = jnp.stack([rows_of(4 * j + 2, lay.b_in[j], lay.b_rows) for j in range(2)])
    grads["b_w_out"] = jnp.stack([rows_of(4 * j + 2, lay.b_out[j], lay.b_rows) for j in range(2)])
    for k in ("ffn_w_gate", "ffn_w_up", "ffn_w_down", "b_w_in", "b_w_out"):
        update(k)

    (side_own,), (side_got,) = _exchange_wait(True, *side_x, delta["b_w_out"], "side_scatter_wait")
    g_side = _sum_parts(side_own, side_got, me1, "sum_side")
    grads["b_w_grp"] = g_side[:side_rows].reshape(b_w_grp.shape)
    grads["b_scale"] = g_side[side_rows:side_rows + 2, :sdev]
    update("b_w_grp")
    update("b_scale")
    _, (small_all,) = _exchange_wait(False, *small_x, delta["b_scale"], "small_gather_wait")
    g_small = _unpack_small(_sum_devices(small_all, "sum_small"), w)
    grads.update(g_small)
    dl, nm, nv = _adamw(_pack_small(w, d), _pack_small(g_small, d), _pack_small(mom, d), _pack_small(var, d),
                        "adamw_small")
    delta.update(_unpack_small(dl, w))
    new_m.update(_unpack_small(nm, w))
    new_v.update(_unpack_small(nv, w))

    arrived(0, dl)
    grads["a_w_in"] = jnp.stack([rows_of(4 * j, lay.a_in[j], lay.a_in_rows).T for j in range(2)])
    grads["a_w_out"] = jnp.stack([rows_of(4 * j, lay.a_out[j], lay.a_out_rows) for j in range(2)])
    update("a_w_in")
    update("a_w_out")

    return (loss, grad_x, *[grads[k] for k in names], *[delta[k] for k in names], *[new_m[k] for k in names],
            *[new_v[k] for k in names])
```

```python
import math

import jax
import jax.numpy as jnp
from jax import lax
from jax.experimental import pallas as pl
from jax.experimental.pallas import tpu as pltpu

F32 = jnp.float32
BF16 = jnp.bfloat16
MESH = pl.DeviceIdType.MESH
ANY = pl.BlockSpec(memory_space=pl.ANY)

N_DEV = 8
EPS = 1e-6
CHUNK = 128
A_GROUPS = 8
A_GROUP_DIM = 256
B_WINDOWS = (2, 4, 8, 16)
B_GROUP_DIM = 256
HALO = 16
DEPTH = 4

ADAM_LR = 0.001
ADAM_B1 = 0.9
ADAM_B2 = 0.999
ADAM_EPS = 1e-08
ADAM_WD = 0.01
ADAM_STEP = 10

VMEM_LIMIT_BYTES = 60 * 1024 * 1024

ERF_P = 0.3275911
ERF_A = (0.254829592, -0.284496736, 1.421413741, -1.453152027, 1.061405429)
INV_SQRT2 = 1.0 / math.sqrt(2.0)
INV_SQRT_2PI = 1.0 / math.sqrt(2.0 * math.pi)


def _call(body, **kw):
    return pl.pallas_call(body, **kw)


def _params(*semantics):
    return pltpu.CompilerParams(dimension_semantics=semantics or None, vmem_limit_bytes=VMEM_LIMIT_BYTES)


def _resident(shape, index):
    return pl.BlockSpec(shape, lambda *_: index, pipeline_mode=pl.Buffered(1))


def _rows(tm, width):
    return pl.BlockSpec((tm, width), lambda i: (i, 0))


def _nn(a, b):
    return jnp.dot(a, b, preferred_element_type=F32)


def _nt(a, b):
    return lax.dot_general(a, b, (((1,), (1,)), ((), ())), preferred_element_type=F32)


def _tn(a, b):
    return lax.dot_general(a, b, (((0,), (0,)), ((), ())), preferred_element_type=F32)


def _rms_fwd(x, g):
    r = lax.rsqrt(jnp.mean(x * x, axis=-1, keepdims=True) + EPS)
    return x * r * g


def _rms_bwd(x, g, dy):
    r = lax.rsqrt(jnp.mean(x * x, axis=-1, keepdims=True) + EPS)
    xh = x * r
    dg = jnp.sum(dy * xh, axis=0, keepdims=True)
    dxh = dy * g
    dx = r * (dxh - xh * jnp.mean(dxh * xh, axis=-1, keepdims=True))
    return dx, dg


def _gelu(z):
    a = jnp.abs(z) * INV_SQRT2
    t = 1.0 / (1.0 + ERF_P * a)
    e = jnp.exp(-a * a)
    poly = t * (ERF_A[0] + t * (ERF_A[1] + t * (ERF_A[2] + t * (ERF_A[3] + t * ERF_A[4]))))
    half = 0.5 * poly * e
    phi = jnp.where(z >= 0, 1.0 - half, half)
    return z * phi, phi + z * e * INV_SQRT_2PI


def _layernorm_stats(v):
    mu = jnp.mean(v, axis=-1, keepdims=True)
    xc = v - mu
    rs = lax.rsqrt(jnp.mean(xc * xc, axis=-1, keepdims=True) + EPS)
    return xc * rs, rs


def _tril_mask():
    r = lax.broadcasted_iota(jnp.int32, (CHUNK, CHUNK), 0)
    c = lax.broadcasted_iota(jnp.int32, (CHUNK, CHUNK), 1)
    return r >= c


def _two_d(ref):
    k, r, d = ref.shape
    return ref[...].reshape(k * r, d)


class _Layout:
    def __init__(self, d):
        self.ffn_rows = 384
        self.gate, self.up, self.down = [0] * DEPTH, [self.ffn_rows] * DEPTH, [2 * self.ffn_rows] * DEPTH
        self.f_total = 3 * self.ffn_rows
        self.a_in_rows, self.a_out_rows, self.b_rows = 4 * d // N_DEV, 2 * d // N_DEV, d // N_DEV
        self.a_in, self.a_out = [0, 0], [self.a_in_rows] * 2
        self.a_total = self.a_in_rows + self.a_out_rows
        self.b_in, self.b_out = [0, 0], [self.b_rows] * 2
        self.b_total = 2 * self.b_rows


def _wspec(rows, off, d):
    assert off % rows == 0
    return _resident((N_DEV, rows, d), (0, off // rows, 0))


def _a_fwd(x, gpre, wg, lay, j, lng, lnb, ws, bst, gpost, tm, name):
    t, d = x.shape
    aw = 2 * d
    nch = tm // CHUNK

    def body(x_ref, gpre_ref, win_ref, lng_ref, lnb_ref, ws_ref, bst_ref, wout_ref, gpost_ref,
             x1_ref, h1_ref, gp_ref, u_ref, vh_ref, rs_ref, gated_ref, m_ref):
        xv = x_ref[...]
        h1 = _rms_fwd(xv, gpre_ref[...]).astype(BF16)
        h1_ref[...] = h1
        z = _nt(h1, _two_d(win_ref))
        u, du_dz = _gelu(z[:, :aw])
        v, dv_dz = _gelu(z[:, aw:])
        gp_ref[:, :aw] = du_dz.astype(BF16)
        gp_ref[:, aw:] = dv_dz.astype(BF16)
        u_ref[...] = u.astype(BF16)
        vh, rs = _layernorm_stats(v)
        vh_ref[...] = vh.astype(BF16)
        rs_ref[...] = jnp.broadcast_to(rs, rs_ref.shape)
        vn = (vh * lng_ref[...] + lnb_ref[...]).astype(BF16)
        mask = _tril_mask()
        for g in range(A_GROUPS):
            wm = jnp.where(mask, ws_ref[g], 0.0).astype(BF16)
            cols = slice(g * A_GROUP_DIM, (g + 1) * A_GROUP_DIM)
            for c in range(nch):
                rows = slice(c * CHUNK, (c + 1) * CHUNK)
                sv = _nn(wm, vn[rows, cols]) + bst_ref[:, g:g + 1]
                gated_ref[rows, cols] = (u[rows, cols] * sv).astype(BF16)
        m = _nn(gated_ref[...], _two_d(wout_ref))
        m_ref[...] = m
        x1_ref[...] = xv + _rms_fwd(m, gpost_ref[...])

    vec = lambda w: _resident((1, w), (0, 0))
    return _call(
        body, name=name, grid=(t // tm,),
        in_specs=[_rows(tm, d), vec(d), _wspec(lay.a_in_rows, lay.a_in[j], d), vec(aw), vec(aw),
                  _resident((A_GROUPS, CHUNK, CHUNK), (0, 0, 0)), _resident((CHUNK, A_GROUPS), (0, 0)),
                  _wspec(lay.a_out_rows, lay.a_out[j], d), vec(d)],
        out_specs=[_rows(tm, d), _rows(tm, d), _rows(tm, 2 * aw), _rows(tm, aw), _rows(tm, aw), _rows(tm, 128),
                   _rows(tm, aw), _rows(tm, d)],
        out_shape=[jax.ShapeDtypeStruct((t, d), F32), jax.ShapeDtypeStruct((t, d), BF16),
                   jax.ShapeDtypeStruct((t, 2 * aw), BF16), jax.ShapeDtypeStruct((t, aw), BF16),
                   jax.ShapeDtypeStruct((t, aw), BF16), jax.ShapeDtypeStruct((t, 128), F32),
                   jax.ShapeDtypeStruct((t, aw), BF16), jax.ShapeDtypeStruct((t, d), F32)],
        compiler_params=_params("parallel"),
    )(x, gpre, wg, lng, lnb, ws, bst, wg, gpost)


def _a_bwd(dx1, m, x, gp, u, vh, rs, gpre, wg, lay, j, lng, lnb, ws, bst, gpost, tm, name):
    t, d = x.shape
    aw = 2 * d
    nch = tm // CHUNK

    def body(dx1_ref, m_ref, x_ref, gp_ref, u_ref, vh_ref, rs_ref, gpre_ref, win_ref, lng_ref, lnb_ref, ws_ref, bst_ref,
             wout_ref, gpost_ref,
             dx_ref, dm_ref, dz_ref, dgpost_ref, dgpre_ref, dlng_ref, dlnb_ref, dws_ref, dbt_ref, dvn_ref):
        @pl.when(pl.program_id(0) == 0)
        def _():
            for r in (dgpost_ref, dgpre_ref, dlng_ref, dlnb_ref, dws_ref, dbt_ref):
                r[...] = jnp.zeros_like(r)

        dy = dx1_ref[...]
        dm, dgpost = _rms_bwd(m_ref[...], gpost_ref[...], dy)
        dgpost_ref[...] += dgpost
        dm_bf = dm.astype(BF16)
        dm_ref[...] = dm_bf
        dgated = _nt(dm_bf, _two_d(wout_ref))

        vh = vh_ref[...].astype(F32)
        rs = rs_ref[:, :1]
        lng_v = lng_ref[...]
        vn = (vh * lng_v + lnb_ref[...]).astype(BF16)
        mask = _tril_mask()
        lane = lax.broadcasted_iota(jnp.int32, (CHUNK, CHUNK), 1)
        for g in range(A_GROUPS):
            wm = jnp.where(mask, ws_ref[g], 0.0).astype(BF16)
            cols = slice(g * A_GROUP_DIM, (g + 1) * A_GROUP_DIM)
            dws_g = jnp.zeros((CHUNK, CHUNK), F32)
            db_g = jnp.zeros((CHUNK, 1), F32)
            for c in range(nch):
                rows = slice(c * CHUNK, (c + 1) * CHUNK)
                vn_cg = vn[rows, cols]
                sv = _nn(wm, vn_cg) + bst_ref[:, g:g + 1]
                dg_cg = dgated[rows, cols]
                dsv = dg_cg * u_ref[rows, cols].astype(F32)
                dsv_bf = dsv.astype(BF16)
                db_g = db_g + jnp.sum(dsv, axis=1, keepdims=True)
                dws_g = dws_g + _nt(dsv_bf, vn_cg)
                dvn_ref[rows, cols] = _tn(wm, dsv_bf)
                dz_ref[rows, cols] = (dg_cg * sv * gp_ref[rows, cols].astype(F32)).astype(BF16)
            dws_ref[g] += jnp.where(mask, dws_g, 0.0)
            dbt_ref[...] += jnp.where(lane == g, db_g, 0.0)
        dvn = dvn_ref[...]
        dlng_ref[...] += jnp.sum(dvn * vh, axis=0, keepdims=True)
        dlnb_ref[...] += jnp.sum(dvn, axis=0, keepdims=True)
        dvh = dvn * lng_v
        dv = rs * (dvh - jnp.mean(dvh, axis=-1, keepdims=True) - vh * jnp.mean(dvh * vh, axis=-1, keepdims=True))
        dz_ref[:, aw:] = (dv * gp_ref[:, aw:].astype(F32)).astype(BF16)
        dh1 = _nn(dz_ref[...], _two_d(win_ref))
        dxp, dgpre = _rms_bwd(x_ref[...], gpre_ref[...], dh1)
        dgpre_ref[...] += dgpre
        dx_ref[...] = dy + dxp

    vec = lambda w: _resident((1, w), (0, 0))
    acc = lambda shape: pl.BlockSpec(shape, lambda i: (0,) * len(shape))
    return _call(
        body, name=name, grid=(t // tm,),
        in_specs=[_rows(tm, d), _rows(tm, d), _rows(tm, d), _rows(tm, 2 * aw), _rows(tm, aw), _rows(tm, aw),
                  _rows(tm, 128), vec(d), _wspec(lay.a_in_rows, lay.a_in[j], d), vec(aw), vec(aw),
                  _resident((A_GROUPS, CHUNK, CHUNK), (0, 0, 0)), _resident((CHUNK, A_GROUPS), (0, 0)),
                  _wspec(lay.a_out_rows, lay.a_out[j], d), vec(d)],
        out_specs=[_rows(tm, d), _rows(tm, d), _rows(tm, 2 * aw), acc((1, d)), acc((1, d)), acc((1, aw)), acc((1, aw)),
                   acc((A_GROUPS, CHUNK, CHUNK)), acc((CHUNK, CHUNK))],
        out_shape=[jax.ShapeDtypeStruct((t, d), F32), jax.ShapeDtypeStruct((t, d), BF16),
                   jax.ShapeDtypeStruct((t, 2 * aw), BF16), jax.ShapeDtypeStruct((1, d), F32),
                   jax.ShapeDtypeStruct((1, d), F32), jax.ShapeDtypeStruct((1, aw), F32),
                   jax.ShapeDtypeStruct((1, aw), F32), jax.ShapeDtypeStruct((A_GROUPS, CHUNK, CHUNK), F32),
                   jax.ShapeDtypeStruct((CHUNK, CHUNK), F32)],
        scratch_shapes=[pltpu.VMEM((tm, aw), F32)],
        compiler_params=_params("arbitrary"),
    )(dx1, m, x, gp, u, vh, rs, gpre, wg, lng, lnb, ws, bst, wg, gpost)


def _window_counts(first_row, n, win):
    tpos = first_row + lax.broadcasted_iota(jnp.int32, (n, 1), 0)
    return jnp.clip(tpos + 1, 1, win).astype(F32)


def _b_fwd(x, gpre, wg, lay, j, wgrp, scale, gpost, tm, name):
    t, d = x.shape
    n = tm + HALO
    ngrp = len(B_WINDOWS)

    def body(x_ref, xprev_ref, gpre_ref, win_ref, wgrp_ref, scale_ref, wout_ref, gpost_ref,
             x1_ref, h1_ref, pooled_ref, mixed_ref, m_ref):
        i = pl.program_id(0)
        xv = x_ref[...]
        keep = jnp.where(i > 0, 1.0, 0.0)
        xe = jnp.concatenate([xprev_ref[...] * keep, xv], axis=0)
        h1e = _rms_fwd(xe, gpre_ref[...]).astype(BF16)
        h1_ref[...] = h1e[HALO:]
        p = _nn(h1e, _two_d(win_ref))
        acc = p
        shift = 1
        for g, win in enumerate(B_WINDOWS):
            lo = g * B_GROUP_DIM
            if g > 0:
                acc = acc[:, B_GROUP_DIM:]
            while shift < win:
                acc = acc + pltpu.roll(acc, shift, 0)
                shift *= 2
            cnt = _window_counts(i * tm - HALO, n, win)
            pooled = acc[:, :B_GROUP_DIM] / cnt - p[:, lo:lo + B_GROUP_DIM]
            pooled_ref[:, lo:lo + B_GROUP_DIM] = pooled[HALO:].astype(BF16)
        for g in range(ngrp):
            cols = slice(g * B_GROUP_DIM, (g + 1) * B_GROUP_DIM)
            raw = _nn(pooled_ref[:, cols], wgrp_ref[g])
            mixed_ref[:, cols] = (raw * scale_ref[:, cols]).astype(BF16)
        m = _nn(mixed_ref[...], _two_d(wout_ref))
        m_ref[...] = m
        x1_ref[...] = xv + _rms_fwd(m, gpost_ref[...])

    vec = lambda w: _resident((1, w), (0, 0))
    per = tm // HALO
    return _call(
        body, name=name, grid=(t // tm,),
        in_specs=[_rows(tm, d), pl.BlockSpec((HALO, d), lambda i: (jnp.maximum(i * per - 1, 0), 0)), vec(d),
                  _wspec(lay.b_rows, lay.b_in[j], d), _resident((ngrp, B_GROUP_DIM, B_GROUP_DIM), (0, 0, 0)), vec(d),
                  _wspec(lay.b_rows, lay.b_out[j], d), vec(d)],
        out_specs=[_rows(tm, d)] * 5,
        out_shape=[jax.ShapeDtypeStruct((t, d), F32), jax.ShapeDtypeStruct((t, d), BF16),
                   jax.ShapeDtypeStruct((t, d), BF16), jax.ShapeDtypeStruct((t, d), BF16),
                   jax.ShapeDtypeStruct((t, d), F32)],
        compiler_params=_params("parallel"),
    )(x, x, gpre, wg, wgrp, scale, wg, gpost)


def _b_bwd(dx1, m, x, pooled, gpre, wg, lay, j, wgrp, scale, gpost, tm, name):
    t, d = x.shape
    n = tm + HALO
    ngrp = len(B_WINDOWS)
    steps = t // tm

    def body(dx1_ref, dx1n_ref, m_ref, mn_ref, x_ref, pooled_ref, pooledn_ref, gpre_ref, win_ref, wgrp_ref, scale_ref,
             wout_ref, gpost_ref, dx_ref, dm_ref, draw_ref, dp_ref, dgpost_ref, dgpre_ref, dscale_ref, dpool_ref):
        i = pl.program_id(0)

        @pl.when(i == 0)
        def _():
            for r in (dgpost_ref, dgpre_ref, dscale_ref):
                r[...] = jnp.zeros_like(r)

        keep = jnp.where(i < steps - 1, 1.0, 0.0)
        dy = dx1_ref[...]
        dye = jnp.concatenate([dy, dx1n_ref[...] * keep], axis=0)
        me = jnp.concatenate([m_ref[...], mn_ref[...]], axis=0)
        gpost_v = gpost_ref[...]
        r = lax.rsqrt(jnp.mean(me * me, axis=-1, keepdims=True) + EPS)
        mh = me * r
        dgpost_ref[...] += jnp.sum((dye * mh)[:tm], axis=0, keepdims=True)
        dmh = dye * gpost_v
        dme = (r * (dmh - mh * jnp.mean(dmh * mh, axis=-1, keepdims=True))).astype(BF16)
        dm_ref[...] = dme[:tm]
        dmixed = _nt(dme, _two_d(wout_ref))
        pooled_e = jnp.concatenate([pooled_ref[...], pooledn_ref[...]], axis=0)
        scale_v = scale_ref[...]
        for g, win in enumerate(B_WINDOWS):
            cols = slice(g * B_GROUP_DIM, (g + 1) * B_GROUP_DIM)
            raw = _nn(pooled_e[:, cols], wgrp_ref[g])
            dscale_ref[:, cols] += jnp.sum((dmixed[:, cols] * raw)[:tm], axis=0, keepdims=True)
            draw = (dmixed[:, cols] * scale_v[:, cols]).astype(BF16)
            draw_ref[:, cols] = draw[:tm]
            dpool = _nt(draw, wgrp_ref[g])
            acc = dpool / _window_counts(i * tm, n, win)
            shift = 1
            while shift < win:
                acc = acc + pltpu.roll(acc, n - shift, 0)
                shift *= 2
            dpool_ref[:, cols] = (acc - dpool)[:tm]
        dp = dpool_ref[...].astype(BF16)
        dp_ref[...] = dp
        dh1 = _nt(dp, _two_d(win_ref))
        dxp, dgpre = _rms_bwd(x_ref[...], gpre_ref[...], dh1)
        dgpre_ref[...] += dgpre
        dx_ref[...] = dy + dxp

    vec = lambda w: _resident((1, w), (0, 0))
    acc = lambda shape: pl.BlockSpec(shape, lambda i: (0,) * len(shape))
    per = tm // HALO
    nxt = lambda i: (jnp.minimum((i + 1) * per, t // HALO - 1), 0)
    return _call(
        body, name=name, grid=(steps,),
        in_specs=[_rows(tm, d), pl.BlockSpec((HALO, d), nxt), _rows(tm, d), pl.BlockSpec((HALO, d), nxt), _rows(tm, d),
                  _rows(tm, d), pl.BlockSpec((HALO, d), nxt), vec(d), _wspec(lay.b_rows, lay.b_in[j], d),
                  _resident((ngrp, B_GROUP_DIM, B_GROUP_DIM), (0, 0, 0)), vec(d), _wspec(lay.b_rows, lay.b_out[j], d),
                  vec(d)],
        out_specs=[_rows(tm, d)] * 4 + [acc((1, d))] * 3,
        out_shape=[jax.ShapeDtypeStruct((t, d), F32), jax.ShapeDtypeStruct((t, d), BF16),
                   jax.ShapeDtypeStruct((t, d), BF16), jax.ShapeDtypeStruct((t, d), BF16)]
                  + [jax.ShapeDtypeStruct((1, d), F32)] * 3,
        scratch_shapes=[pltpu.VMEM((tm, d), F32)],
        compiler_params=_params("arbitrary"),
    )(dx1, dx1, m, m, x, pooled, pooled, gpre, wg, wgrp, scale, wg, gpost)


def _f_fwd(x1, gpre, wg, lay, l, gpost, tm, name):
    t, d = x1.shape
    hid = N_DEV * lay.ffn_rows

    def body(x_ref, gpre_ref, wgate_ref, wup_ref, wdown_ref, gpost_ref, x2_ref, h2_ref, a_ref, b_ref, s_ref, f_ref):
        xv = x_ref[...]
        h2 = _rms_fwd(xv, gpre_ref[...]).astype(BF16)
        h2_ref[...] = h2
        a = _nt(h2, _two_d(wgate_ref))
        b = _nt(h2, _two_d(wup_ref))
        a_ref[...] = a.astype(BF16)
        b_ref[...] = b.astype(BF16)
        s = (a * (1.0 / (1.0 + jnp.exp(-a))) * b).astype(BF16)
        s_ref[...] = s
        f = _nn(s, _two_d(wdown_ref))
        f_ref[...] = f
        x2_ref[...] = xv + _rms_fwd(f, gpost_ref[...])

    vec = lambda w: _resident((1, w), (0, 0))
    return _call(
        body, name=name, grid=(t // tm,),
        in_specs=[_rows(tm, d), vec(d), _wspec(lay.ffn_rows, lay.gate[l], d), _wspec(lay.ffn_rows, lay.up[l], d),
                  _wspec(lay.ffn_rows, lay.down[l], d), vec(d)],
        out_specs=[_rows(tm, d), _rows(tm, d), _rows(tm, hid), _rows(tm, hid), _rows(tm, hid), _rows(tm, d)],
        out_shape=[jax.ShapeDtypeStruct((t, d), F32), jax.ShapeDtypeStruct((t, d), BF16),
                   jax.ShapeDtypeStruct((t, hid), BF16), jax.ShapeDtypeStruct((t, hid), BF16),
                   jax.ShapeDtypeStruct((t, hid), BF16), jax.ShapeDtypeStruct((t, d), F32)],
        compiler_params=_params("parallel"),
    )(x1, gpre, wg, wg, wg, gpost)


def _f_bwd(dx2, f, x1, a, b, gpre, wg, lay, l, gpost, tm, name):
    t, d = x1.shape
    hid = N_DEV * lay.ffn_rows

    def body(dx2_ref, f_ref, x_ref, a_ref, b_ref, gpre_ref, wgate_ref, wup_ref, wdown_ref, gpost_ref,
             dx1_ref, df_ref, da_ref, db_ref, dgpost_ref, dgpre_ref):
        @pl.when(pl.program_id(0) == 0)
        def _():
            dgpost_ref[...] = jnp.zeros_like(dgpost_ref)
            dgpre_ref[...] = jnp.zeros_like(dgpre_ref)

        dy = dx2_ref[...]
        df, dgpost = _rms_bwd(f_ref[...], gpost_ref[...], dy)
        dgpost_ref[...] += dgpost
        df_bf = df.astype(BF16)
        df_ref[...] = df_bf
        ds = _nt(df_bf, _two_d(wdown_ref))
        av = a_ref[...].astype(F32)
        bv = b_ref[...].astype(F32)
        sig = 1.0 / (1.0 + jnp.exp(-av))
        da = (ds * bv * (sig * (1.0 + av * (1.0 - sig)))).astype(BF16)
        db = (ds * (av * sig)).astype(BF16)
        da_ref[...] = da
        db_ref[...] = db
        dh2 = _nn(da, _two_d(wgate_ref)) + _nn(db, _two_d(wup_ref))
        dxp, dgpre = _rms_bwd(x_ref[...], gpre_ref[...], dh2)
        dgpre_ref[...] += dgpre
        dx1_ref[...] = dy + dxp

    vec = lambda w: _resident((1, w), (0, 0))
    acc = pl.BlockSpec((1, d), lambda i: (0, 0))
    return _call(
        body, name=name, grid=(t // tm,),
        in_specs=[_rows(tm, d), _rows(tm, d), _rows(tm, d), _rows(tm, hid), _rows(tm, hid), vec(d),
                  _wspec(lay.ffn_rows, lay.gate[l], d), _wspec(lay.ffn_rows, lay.up[l], d),
                  _wspec(lay.ffn_rows, lay.down[l], d), vec(d)],
        out_specs=[_rows(tm, d), _rows(tm, d), _rows(tm, hid), _rows(tm, hid), acc, acc],
        out_shape=[jax.ShapeDtypeStruct((t, d), F32), jax.ShapeDtypeStruct((t, d), BF16),
                   jax.ShapeDtypeStruct((t, hid), BF16), jax.ShapeDtypeStruct((t, hid), BF16),
                   jax.ShapeDtypeStruct((1, d), F32), jax.ShapeDtypeStruct((1, d), F32)],
        compiler_params=_params("arbitrary"),
    )(dx2, f, x1, a, b, gpre, wg, wg, wg, gpost)


def _loss_head(y, target, tm, name):
    t, d = y.shape

    def body(y_ref, t_ref, dy_ref, loss_ref):
        @pl.when(pl.program_id(0) == 0)
        def _():
            loss_ref[...] = jnp.zeros_like(loss_ref)

        diff = y_ref[...] - t_ref[...]
        dy_ref[...] = diff * (1.0 / d)
        sq = jnp.sum(jnp.sum(diff * diff, axis=0, keepdims=True), axis=1, keepdims=True)
        loss_ref[...] += sq * (0.5 / d)

    return _call(
        body, name=name, grid=(t // tm,),
        in_specs=[_rows(tm, d), _rows(tm, d)],
        out_specs=[_rows(tm, d), pl.BlockSpec((8, 128), lambda i: (0, 0))],
        out_shape=[jax.ShapeDtypeStruct((t, d), F32), jax.ShapeDtypeStruct((8, 128), F32)],
        compiler_params=_params("arbitrary"),
    )(y, target)


def _grad_into(gbuf, lhs, rhs, off, rows, name, after=None):
    t, m = lhs.shape
    d = rhs.shape[1]
    assert m == N_DEV * rows and off % rows == 0
    per_tile = {384: 4, 512: 2, 256: 4, 128: 8}[rows]
    tm = per_tile * rows
    tk = 2048 if t % 2048 == 0 else 256
    ksteps = t // tk
    fresh = isinstance(gbuf, int)
    shape = (N_DEV, gbuf, d) if fresh else gbuf.shape
    extra = ([] if fresh else [gbuf]) + ([] if after is None else [after])

    def body(l_ref, r_ref, *rest):
        o_ref, acc_ref = rest[-2:]
        k = pl.program_id(1)

        @pl.when(k == 0)
        def _():
            acc_ref[...] = jnp.zeros_like(acc_ref)

        acc_ref[...] += _tn(l_ref[...], r_ref[...])

        @pl.when(k == ksteps - 1)
        def _():
            o_ref[...] = acc_ref[...].reshape(per_tile, rows, d).astype(BF16)

    return _call(
        body, name=name, grid=(N_DEV // per_tile, ksteps),
        in_specs=[pl.BlockSpec((tk, tm), lambda i, k: (k, i)), pl.BlockSpec((tk, d), lambda i, k: (k, 0))]
                 + [ANY] * len(extra),
        out_specs=pl.BlockSpec((per_tile, rows, d), lambda i, k: (i, off // rows, 0)),
        out_shape=jax.ShapeDtypeStruct(shape, BF16),
        scratch_shapes=[pltpu.VMEM((tm, d), F32)],
        input_output_aliases={} if fresh else {2: 0},
        compiler_params=_params("parallel", "arbitrary"),
    )(lhs, rhs, *extra)


def _grad_grouped(pooled, draw, name):
    t, d = pooled.shape
    ngrp = len(B_WINDOWS)
    tk = 1024 if t % 1024 == 0 else 256

    def body(p_ref, q_ref, o_ref):
        @pl.when(pl.program_id(0) == 0)
        def _():
            o_ref[...] = jnp.zeros_like(o_ref)

        for g in range(ngrp):
            cols = slice(g * B_GROUP_DIM, (g + 1) * B_GROUP_DIM)
            o_ref[g] += _tn(p_ref[:, cols], q_ref[:, cols])

    return _call(
        body, name=name, grid=(t // tk,),
        in_specs=[_rows(tk, d), _rows(tk, d)],
        out_specs=pl.BlockSpec((ngrp, B_GROUP_DIM, B_GROUP_DIM), lambda i: (0, 0, 0)),
        out_shape=jax.ShapeDtypeStruct((ngrp, B_GROUP_DIM, B_GROUP_DIM), F32),
        compiler_params=_params("arbitrary"),
    )(pooled, draw)


def _peers():
    x, y, c = lax.axis_index("x"), lax.axis_index("y"), lax.axis_index("c")
    flip = lambda v, f: 1 - v if f else v
    peers = []
    for r in range(1, N_DEV):
        px, py, pc = flip(x, r & 4), flip(y, r & 2), flip(c, r & 1)
        peers.append(((px, py, pc), 4 * px + 2 * py + pc))
    return 4 * x + 2 * y + c, peers


HBM = pl.BlockSpec(memory_space=pltpu.HBM)
SEM = pl.BlockSpec(memory_space=pltpu.SEMAPHORE)
EFFECT = pltpu.SideEffectType.DATAFLOW_SIDE_EFFECTING


def _peer_copies(scatter, srcs, lands, send_sems, recv_sems):
    me, peers = _peers()
    copies = []
    for a in range(len(srcs)):
        for r, (peer, pidx) in enumerate(peers):
            src = srcs[a].at[pidx] if scatter else srcs[a]
            mine = lands[a].at[r] if scatter else lands[a].at[pidx]
            theirs = lands[a].at[r] if scatter else lands[a].at[me]
            send = pltpu.make_async_remote_copy(src_ref=src, dst_ref=theirs, send_sem=send_sems[a].at[r],
                                                recv_sem=recv_sems[a].at[r], device_id=peer, device_id_type=MESH)
            recv = pltpu.make_async_remote_copy(src_ref=src, dst_ref=mine, send_sem=send_sems[a].at[r],
                                                recv_sem=recv_sems[a].at[r], device_id=peer, device_id_type=MESH)
            copies.append((send, recv))
    return copies


def _exchange_start(scatter, srcs, lands, after, name):
    n = len(srcs)

    def body(*refs):
        src_refs, land_refs = refs[:n], refs[n:2 * n]
        outs = refs[2 * n + 1:]
        send_sems, recv_sems, token = outs[:n], outs[n:2 * n], outs[-1]
        for send, _ in _peer_copies(scatter, src_refs, land_refs, send_sems, recv_sems):
            send.start()
        token[...] = jnp.zeros_like(token)

    hbm = lambda a: pltpu.with_memory_space_constraint(a, pltpu.HBM)
    res = _call(
        body, name=name,
        in_specs=[HBM] * (2 * n) + [ANY],
        out_specs=[SEM] * (2 * n) + [HBM] * (2 * n) + [pl.BlockSpec(memory_space=pltpu.VMEM)],
        out_shape=[pltpu.SemaphoreType.DMA((N_DEV - 1,))] * (2 * n)
                  + [pltpu.HBM(a.shape, a.dtype) for a in list(srcs) + list(lands)]
                  + [jax.ShapeDtypeStruct((8, 128), F32)],
        input_output_aliases={i: 2 * n + i for i in range(2 * n)},
        compiler_params=pltpu.CompilerParams(has_side_effects=EFFECT),
    )(*[hbm(a) for a in srcs], *[hbm(a) for a in lands], after)
    return res[:n], res[n:2 * n], res[2 * n:3 * n], res[3 * n:4 * n], res[-1]


def _exchange_wait(scatter, send_sems, recv_sems, srcs, lands, after, name):
    n = len(srcs)
    after = list(after) if isinstance(after, (list, tuple)) else [after]

    def body(*refs):
        src_refs, land_refs = refs[:n], refs[n:2 * n]
        send_refs, recv_refs = refs[2 * n:3 * n], refs[3 * n:4 * n]
        for send, recv in _peer_copies(scatter, src_refs, land_refs, send_refs, recv_refs):
            send.wait_send()
            recv.wait_recv()

    res = _call(
        body, name=name,
        in_specs=[HBM] * (2 * n) + [SEM] * (2 * n) + [ANY] * len(after),
        out_specs=[HBM] * (2 * n),
        out_shape=[pltpu.HBM(a.shape, a.dtype) for a in list(srcs) + list(lands)],
        input_output_aliases={i: i for i in range(2 * n)},
        compiler_params=pltpu.CompilerParams(has_side_effects=EFFECT),
    )(*srcs, *lands, *send_sems, *recv_sems, *after)
    return res[:n], res[n:]


def _row_tile(rows):
    if rows <= 512:
        return rows
    for tr in (512, 384, 256, 128, 64, 32, 16, 8):
        if rows % tr == 0:
            return tr
    return rows


def _sum_parts(own, got, me, name):
    _, rows, w = own.shape
    tr = _row_tile(rows)

    def body(me_ref, a_ref, b_ref, o_ref):
        s = a_ref[...].astype(F32)
        for j in range(N_DEV - 1):
            s = s + b_ref[j].astype(F32)
        o_ref[...] = s

    return _call(
        body, name=name,
        grid_spec=pltpu.PrefetchScalarGridSpec(
            num_scalar_prefetch=1, grid=(rows // tr,),
            in_specs=[pl.BlockSpec((None, tr, w), lambda i, me_ref: (me_ref[0], i, 0)),
                      pl.BlockSpec((N_DEV - 1, tr, w), lambda i, me_ref: (0, i, 0))],
            out_specs=pl.BlockSpec((tr, w), lambda i, me_ref: (i, 0))),
        out_shape=jax.ShapeDtypeStruct((rows, w), F32),
        compiler_params=_params("parallel"),
    )(me, own, got)


def _sum_devices(stacked, name):
    k, rows, w = stacked.shape
    tr = _row_tile(rows)

    def body(a_ref, o_ref):
        s = a_ref[0]
        for j in range(1, k):
            s = s + a_ref[j]
        o_ref[...] = s

    return _call(
        body, name=name, grid=(rows // tr,),
        in_specs=[pl.BlockSpec((k, tr, w), lambda i: (0, i, 0))],
        out_specs=pl.BlockSpec((tr, w), lambda i: (i, 0)),
        out_shape=jax.ShapeDtypeStruct((rows, w), F32),
        compiler_params=_params("parallel"),
    )(stacked)


def _adamw(w, g, m, v, name):
    rows, cols = w.shape
    tr = _row_tile(rows)

    def body(w_ref, g_ref, m_ref, v_ref, d_ref, nm_ref, nv_ref):
        gv = g_ref[...]
        nm = ADAM_B1 * m_ref[...] + (1.0 - ADAM_B1) * gv
        nv = ADAM_B2 * v_ref[...] + (1.0 - ADAM_B2) * (gv * gv)
        m_hat = nm / (1.0 - ADAM_B1 ** ADAM_STEP)
        v_hat = nv / (1.0 - ADAM_B2 ** ADAM_STEP)
        d_ref[...] = -ADAM_LR * (m_hat / (jnp.sqrt(v_hat) + ADAM_EPS) + ADAM_WD * w_ref[...])
        nm_ref[...] = nm
        nv_ref[...] = nv

    spec = pl.BlockSpec((tr, cols), lambda i: (i, 0))
    return _call(
        body, name=name, grid=(rows // tr,),
        in_specs=[spec] * 4, out_specs=[spec] * 3,
        out_shape=[jax.ShapeDtypeStruct((rows, cols), F32)] * 3,
        compiler_params=_params("parallel"),
    )(w, g, m, v)


SMALL = ("a_ln_g", "a_ln_b", "a_w_s", "a_b_s", "mix_pre_g", "mix_post_g", "ffn_pre_g", "ffn_post_g")


def _pack_small(parts, d):
    flat = jnp.concatenate([parts[k].reshape(-1, d) for k in SMALL], axis=0)
    return jnp.pad(flat, ((0, -flat.shape[0] % 8), (0, 0)))


def _unpack_small(flat, like):
    out, r = {}, 0
    for k in SMALL:
        n = like[k].size // flat.shape[1]
        out[k] = flat[r:r + n].reshape(like[k].shape)
        r += n
    return out


def kernel(x, a_w_in, a_ln_g, a_ln_b, a_w_s, a_b_s, a_w_out, b_w_in, b_w_grp, b_scale, b_w_out, mix_pre_g, mix_post_g, ffn_pre_g, ffn_post_g, ffn_w_gate, ffn_w_up, ffn_w_down, loss_target, m_a_w_in, m_a_ln_g, m_a_ln_b, m_a_w_s, m_a_b_s, m_a_w_out, m_b_w_in, m_b_w_grp, m_b_scale, m_b_w_out, m_mix_pre_g, m_mix_post_g, m_ffn_pre_g, m_ffn_post_g, m_ffn_w_gate, m_ffn_w_up, m_ffn_w_down, v_a_w_in, v_a_ln_g, v_a_ln_b, v_a_w_s, v_a_b_s, v_a_w_out, v_b_w_in, v_b_w_grp, v_b_scale, v_b_w_out, v_mix_pre_g, v_mix_post_g, v_ffn_pre_g, v_ffn_post_g, v_ffn_w_gate, v_ffn_w_up, v_ffn_w_down):
    args = dict(locals())
    names = ("a_w_in", "a_ln_g", "a_ln_b", "a_w_s", "a_b_s", "a_w_out", "b_w_in", "b_w_grp", "b_scale", "b_w_out",
             "mix_pre_g", "mix_post_g", "ffn_pre_g", "ffn_post_g", "ffn_w_gate", "ffn_w_up", "ffn_w_down")
    w = {k: args[k] for k in names}
    mom = {k: args["m_" + k] for k in names}
    var = {k: args["v_" + k] for k in names}

    t, d = x.shape[1], x.shape[2]
    lay = _Layout(d)
    ffn_local = ffn_w_gate.shape[2]
    ffn_pad = lay.ffn_rows - ffn_local
    me = 4 * lax.axis_index("x") + 2 * lax.axis_index("y") + lax.axis_index("c")
    me1 = jnp.reshape(me, (1,)).astype(jnp.int32)

    def ffn_t(wl):
        return jnp.pad(wl.T, ((0, ffn_pad), (0, 0)))

    def landing(block):
        zone = lax.empty((N_DEV,) + block.shape, block.dtype)
        return lax.dynamic_update_slice(zone, block[None], (me,) + (0,) * block.ndim)

    def pack(i, mixer, zero):
        j = i // 2
        if not mixer:
            parts = [ffn_t(ffn_w_gate[i]), ffn_t(ffn_w_up[i]), jnp.pad(ffn_w_down[i], ((0, ffn_pad), (0, 0)))]
        elif i % 2 == 0:
            parts = [a_w_in[j].T, a_w_out[j]]
        else:
            parts = [b_w_in[j], b_w_out[j]]
        return (jnp.concatenate(parts, axis=0) + zero).astype(BF16)

    nsub = 2 * DEPTH
    wg = [None] * nsub
    packed0 = pack(0, True, 0.0)
    first = _exchange_start(False, [packed0], [landing(packed0)], jnp.zeros((8, 128), F32), "gather_first_start")
    zero = first[4][0, 0]
    packed = [packed0] + [pack(k // 2, k % 2 == 0, zero) for k in range(1, nsub)]
    ngrp = len(B_WINDOWS)
    grp_local = b_w_grp.shape[2]
    sdev = b_scale.shape[1]
    side_rows = 2 * ngrp * grp_local
    side = jnp.concatenate(
        [b_w_grp.reshape(side_rows, B_GROUP_DIM),
         jnp.pad(b_scale, ((0, 6), (0, B_GROUP_DIM - sdev)))], axis=0) + zero
    later = [side] + packed[1:]
    send_sems, recv_sems, later, zones, token = _exchange_start(
        False, later, [landing(b) for b in later], first[4], "gather_start")
    _, (wg[0],) = _exchange_wait(False, *first[:4], token, "gather_first_wait")

    def gathered(k, after):
        _, (zone,) = _exchange_wait(False, [send_sems[k]], [recv_sems[k]], [later[k]], [zones[k]], after,
                                    f"gather_wait_{k}")
        return zone

    row = lambda a: a.reshape(1, -1)
    bst = jnp.transpose(a_b_s, (0, 2, 1))

    tm = 256 if t % 256 == 0 else CHUNK
    tm_abwd = tm

    saved = []
    h = x[0]
    wgrp_full = scale_full = None
    for i in range(DEPTH):
        j = i // 2
        gpre = row(mix_pre_g[i])
        if i > 0:
            wg[2 * i] = gathered(2 * i, h)
        if i % 2 == 0:
            x1, h1, gp, u, vh, rs, gated, m = _a_fwd(h, gpre, wg[2 * i], lay, j, row(a_ln_g[j]), row(a_ln_b[j]),
                                                     a_w_s[j], bst[j], row(mix_post_g[i]), tm, f"a_fwd_{j}")
            mix = dict(h1=h1, gp=gp, u=u, vh=vh, rs=rs, gated=gated, m=m)
        else:
            if wgrp_full is None:
                side_g = gathered(0, h)
                wgrp_full = (side_g[:, :side_rows].reshape(N_DEV, 2, ngrp, grp_local, B_GROUP_DIM)
                             .transpose(1, 2, 0, 3, 4).reshape(2, ngrp, B_GROUP_DIM, B_GROUP_DIM).astype(BF16))
                scale_full = (side_g[:, side_rows:side_rows + 2, :sdev].transpose(1, 0, 2)
                              .reshape(2, 1, N_DEV * sdev))
            x1, h1, pooled, mixed, m = _b_fwd(h, gpre, wg[2 * i], lay, j, wgrp_full[j], scale_full[j],
                                              row(mix_post_g[i]), tm, f"b_fwd_{j}")
            mix = dict(h1=h1, pooled=pooled, mixed=mixed, m=m)
        wg[2 * i + 1] = gathered(2 * i + 1, x1)
        x2, h2, a, b, s, f = _f_fwd(x1, row(ffn_pre_g[i]), wg[2 * i + 1], lay, i, row(ffn_post_g[i]), tm,
                                    f"f_fwd_{i}")
        saved.append(dict(x=h, x1=x1, mix=mix, h2=h2, a=a, b=b, s=s, f=f))
        h = x2

    dy, loss_acc = _loss_head(h, loss_target[0], tm, "loss_head")
    loss = lax.psum(loss_acc[0, 0], ("x", "y", "c"))

    small_g = {k: [None] * w[k].shape[0] for k in SMALL}
    dgrp, dscale = [None, None], [None, None]
    pending = [None] * nsub
    token = jnp.zeros((8, 128), F32)

    def scatter(k, gbuf):
        got = pltpu.with_memory_space_constraint(lax.empty((N_DEV - 1,) + gbuf.shape[1:], gbuf.dtype), pltpu.HBM)
        ss, rs, src, zone, tok = _exchange_start(True, [gbuf], [got], token, f"scatter_start_{k}")
        pending[k] = (ss, rs, src, zone)
        return tok

    def small_exchanges():
        side_grad = jnp.concatenate(
            [jnp.stack(dgrp).reshape(2, ngrp, N_DEV, grp_local, B_GROUP_DIM).transpose(2, 0, 1, 3, 4)
             .reshape(N_DEV, side_rows, B_GROUP_DIM),
             jnp.pad(jnp.stack(dscale).reshape(2, N_DEV, sdev).transpose(1, 0, 2),
                     ((0, 0), (0, 6), (0, B_GROUP_DIM - sdev)))], axis=1)
        small_part = _pack_small({k: jnp.stack(small_g[k]) for k in SMALL}, d)
        got = pltpu.with_memory_space_constraint(lax.empty((N_DEV - 1,) + side_grad.shape[1:], F32), pltpu.HBM)
        side_x = _exchange_start(True, [side_grad], [got], token, "side_scatter_start")
        small_x = _exchange_start(False, [small_part], [landing(small_part)], side_x[4], "small_gather_start")
        return side_x[:4], small_x[:4], small_x[4]

    for i in reversed(range(DEPTH)):
        sv = saved[i]
        j = i // 2
        wf, wm = wg[2 * i + 1], wg[2 * i]
        dx1, df, da, db, dgpost, dgpre = _f_bwd(dy, sv["f"], sv["x1"], sv["a"], sv["b"], row(ffn_pre_g[i]), wf, lay, i,
                                                 row(ffn_post_g[i]) + token[:1, :1], tm, f"f_bwd_{i}")
        small_g["ffn_post_g"][i], small_g["ffn_pre_g"][i] = dgpost[0], dgpre[0]
        gbuf = _grad_into(lay.f_total, da, sv["h2"], lay.gate[i], lay.ffn_rows, f"g_gate_{i}")
        gbuf = _grad_into(gbuf, db, sv["h2"], lay.up[i], lay.ffn_rows, f"g_up_{i}")
        gbuf = _grad_into(gbuf, sv["s"], df, lay.down[i], lay.ffn_rows, f"g_down_{i}")
        token = scatter(2 * i + 1, gbuf)
        mix = sv["mix"]
        gpost = row(mix_post_g[i]) + token[:1, :1]
        if i % 2 == 0:
            dx, dm, dz, dgpost, dgpre, dlng, dlnb, dws, dbt = _a_bwd(
                dx1, mix["m"], sv["x"], mix["gp"], mix["u"], mix["vh"], mix["rs"], row(mix_pre_g[i]), wm, lay, j,
                row(a_ln_g[j]), row(a_ln_b[j]), a_w_s[j], bst[j], gpost, tm_abwd, f"a_bwd_{j}")
            small_g["a_ln_g"][j], small_g["a_ln_b"][j] = dlng[0], dlnb[0]
            small_g["a_w_s"][j], small_g["a_b_s"][j] = dws, dbt[:, :A_GROUPS].T
            small_g["mix_post_g"][i], small_g["mix_pre_g"][i] = dgpost[0], dgpre[0]
            order = None
            if i == 0:
                side_x, small_x, order = small_exchanges()
            gbuf = _grad_into(lay.a_total, dz, mix["h1"], lay.a_in[j], lay.a_in_rows, f"g_a_in_{j}", after=order)
            gbuf = _grad_into(gbuf, mix["gated"], dm, lay.a_out[j], lay.a_out_rows, f"g_a_out_{j}")
        else:
            dx, dm, draw, dp, dgpost, dgpre, dsc = _b_bwd(
                dx1, mix["m"], sv["x"], mix["pooled"], row(mix_pre_g[i]), wm, lay, j, wgrp_full[j], scale_full[j],
                gpost, tm, f"b_bwd_{j}")
            dscale[j] = dsc[0]
            dgrp[j] = _grad_grouped(mix["pooled"], draw, f"g_b_grp_{j}")
            gbuf = _grad_into(lay.b_total, mix["h1"], dp, lay.b_in[j], lay.b_rows, f"g_b_in_{j}")
            gbuf = _grad_into(gbuf, mix["mixed"], dm, lay.b_out[j], lay.b_rows, f"g_b_out_{j}")
            small_g["mix_post_g"][i], small_g["mix_pre_g"][i] = dgpost[0], dgpre[0]
        token = scatter(2 * i, gbuf)
        dy = dx
    grad_x = dy[None]

    g_sub = [None] * nsub

    def arrived(k, after):
        ss, rs, src, zone = pending[k]
        (own,), (got,) = _exchange_wait(True, ss, rs, src, zone, after, f"scatter_wait_{k}")
        g_sub[k] = _sum_parts(own, got, me1, f"sum_grads_{k}")

    def rows_of(k, off, n):
        return g_sub[k][off:off + n]

    grads, delta, new_m, new_v = {}, {}, {}, {}

    def update(k):
        shape = w[k].shape
        two = lambda a: a.reshape(-1, shape[-1])
        dl, nm, nv = _adamw(two(w[k]), two(grads[k]), two(mom[k]), two(var[k]), f"adamw_{k}")
        delta[k], new_m[k], new_v[k] = dl.reshape(shape), nm.reshape(shape), nv.reshape(shape)

    for k in range(1, nsub):
        arrived(k, token)
    grads["ffn_w_gate"] = jnp.stack([rows_of(2 * l + 1, lay.gate[l], ffn_local).T for l in range(DEPTH)])
    grads["ffn_w_up"] = jnp.stack([rows_of(2 * l + 1, lay.up[l], ffn_local).T for l in range(DEPTH)])
    grads["ffn_w_down"] = jnp.stack([rows_of(2 * l + 1, lay.down[l], ffn_local) for l in range(DEPTH)])
    grads["b_w_in"] = jnp.stack([rows_of(4 * j + 2, lay.b_in[j], lay.b_rows) for j in range(2)])
    grads["b_w_out"] = jnp.stack([rows_of(4 * j + 2, lay.b_out[j], lay.b_rows) for j in range(2)])
    early = ("ffn_w_gate", "ffn_w_up", "ffn_w_down", "b_w_in", "b_w_out")
    for k in early:
        update(k)

    (side_own,), (side_got,) = _exchange_wait(True, *side_x, [delta[k] for k in early], "side_scatter_wait")
    g_side = _sum_parts(side_own, side_got, me1, "sum_side")
    grads["b_w_grp"] = g_side[:side_rows].reshape(b_w_grp.shape)
    grads["b_scale"] = g_side[side_rows:side_rows + 2, :sdev]
    update("b_w_grp")
    update("b_scale")
    _, (small_all,) = _exchange_wait(False, *small_x, [delta["b_w_grp"], delta["b_scale"]], "small_gather_wait")
    g_small = _unpack_small(_sum_devices(small_all, "sum_small"), w)
    grads.update(g_small)
    dl, nm, nv = _adamw(_pack_small(w, d), _pack_small(g_small, d), _pack_small(mom, d), _pack_small(var, d),
                        "adamw_small")
    delta.update(_unpack_small(dl, w))
    new_m.update(_unpack_small(nm, w))
    new_v.update(_unpack_small(nv, w))

    arrived(0, dl)
    grads["a_w_in"] = jnp.stack([rows_of(4 * j, lay.a_in[j], lay.a_in_rows).T for j in range(2)])
    grads["a_w_out"] = jnp.stack([rows_of(4 * j, lay.a_out[j], lay.a_out_rows) for j in range(2)])
    update("a_w_in")
    update("a_w_out")

    return (loss, grad_x, *[grads[k] for k in names], *[delta[k] for k in names], *[new_m[k] for k in names],
            *[new_v[k] for k in names])
```

```python
import math

import jax
import jax.numpy as jnp
from jax import lax
from jax.experimental import pallas as pl
from jax.experimental.pallas import tpu as pltpu

F32 = jnp.float32
BF16 = jnp.bfloat16
MESH = pl.DeviceIdType.MESH
ANY = pl.BlockSpec(memory_space=pl.ANY)

N_DEV = 8
EPS = 1e-6
CHUNK = 128
A_GROUPS = 8
A_GROUP_DIM = 256
B_WINDOWS = (2, 4, 8, 16)
B_GROUP_DIM = 256
HALO = 16
DEPTH = 4

ADAM_LR = 0.001
ADAM_B1 = 0.9
ADAM_B2 = 0.999
ADAM_EPS = 1e-08
ADAM_WD = 0.01
ADAM_STEP = 10

VMEM_LIMIT_BYTES = 60 * 1024 * 1024

ERF_P = 0.3275911
ERF_A = (0.254829592, -0.284496736, 1.421413741, -1.453152027, 1.061405429)
INV_SQRT2 = 1.0 / math.sqrt(2.0)
LOG2_E = 1.0 / math.log(2.0)
INV_SQRT_2PI = 1.0 / math.sqrt(2.0 * math.pi)


def _call(body, **kw):
    return pl.pallas_call(body, **kw)


def _params(*semantics):
    return pltpu.CompilerParams(dimension_semantics=semantics or None, vmem_limit_bytes=VMEM_LIMIT_BYTES)


def _resident(shape, index):
    return pl.BlockSpec(shape, lambda *_: index, pipeline_mode=pl.Buffered(1))


def _rows(tm, width):
    return pl.BlockSpec((tm, width), lambda i: (i, 0))


def _nn(a, b):
    return jnp.dot(a, b, preferred_element_type=F32)


def _nt(a, b):
    return lax.dot_general(a, b, (((1,), (1,)), ((), ())), preferred_element_type=F32)


def _tn(a, b):
    return lax.dot_general(a, b, (((0,), (0,)), ((), ())), preferred_element_type=F32)


def _rms_fwd(x, g):
    r = lax.rsqrt(jnp.mean(x * x, axis=-1, keepdims=True) + EPS)
    return x * r * g


def _rms_bwd(x, g, dy):
    r = lax.rsqrt(jnp.mean(x * x, axis=-1, keepdims=True) + EPS)
    xh = x * r
    dg = jnp.sum(dy * xh, axis=0, keepdims=True)
    dxh = dy * g
    dx = r * (dxh - xh * jnp.mean(dxh * xh, axis=-1, keepdims=True))
    return dx, dg


def _gelu(z):
    t = 1.0 / (1.0 + (ERF_P * INV_SQRT2) * jnp.abs(z))
    e = jnp.exp2(z * z * (-0.5 * LOG2_E))
    h = [0.5 * a for a in ERF_A]
    half = t * (h[0] + t * (h[1] + t * (h[2] + t * (h[3] + t * h[4])))) * e
    phi = jnp.where(z >= 0, 1.0 - half, half)
    return z * phi, phi + z * e * INV_SQRT_2PI


def _layernorm_stats(v):
    mu = jnp.mean(v, axis=-1, keepdims=True)
    xc = v - mu
    rs = lax.rsqrt(jnp.mean(xc * xc, axis=-1, keepdims=True) + EPS)
    return xc * rs, rs


def _tril_mask():
    r = lax.broadcasted_iota(jnp.int32, (CHUNK, CHUNK), 0)
    c = lax.broadcasted_iota(jnp.int32, (CHUNK, CHUNK), 1)
    return r >= c


def _two_d(ref):
    k, r, d = ref.shape
    return ref[...].reshape(k * r, d)


class _Layout:
    def __init__(self, d, ffn_rows):
        self.ffn_rows = ffn_rows
        self.gate, self.up, self.down = [0] * DEPTH, [self.ffn_rows] * DEPTH, [2 * self.ffn_rows] * DEPTH
        self.f_total = 3 * self.ffn_rows
        self.a_in_rows, self.a_out_rows, self.b_rows = 4 * d // N_DEV, 2 * d // N_DEV, d // N_DEV
        self.a_in, self.a_out = [0, 0], [self.a_in_rows] * 2
        self.a_total = self.a_in_rows + self.a_out_rows
        self.b_in, self.b_out = [0, 0], [self.b_rows] * 2
        self.b_total = 2 * self.b_rows


def _wspec(rows, off, d):
    assert off % rows == 0
    return _resident((N_DEV, rows, d), (0, off // rows, 0))


def _a_fwd(x, gpre, wg, lay, j, lng, lnb, ws, bst, gpost, tm, name):
    t, d = x.shape
    aw = 2 * d
    nch = tm // CHUNK

    def body(x_ref, gpre_ref, win_ref, lng_ref, lnb_ref, ws_ref, bst_ref, wout_ref, gpost_ref,
             x1_ref, h1_ref, gp_ref, u_ref, vh_ref, rs_ref, gated_ref, m_ref):
        xv = x_ref[...]
        h1 = _rms_fwd(xv, gpre_ref[...]).astype(BF16)
        h1_ref[...] = h1
        z = _nt(h1, _two_d(win_ref))
        u, du_dz = _gelu(z[:, :aw])
        v, dv_dz = _gelu(z[:, aw:])
        gp_ref[:, :aw] = du_dz.astype(BF16)
        gp_ref[:, aw:] = dv_dz.astype(BF16)
        u_ref[...] = u.astype(BF16)
        vh, rs = _layernorm_stats(v)
        vh_ref[...] = vh.astype(BF16)
        rs_ref[...] = jnp.broadcast_to(rs, rs_ref.shape)
        vn = (vh * lng_ref[...] + lnb_ref[...]).astype(BF16)
        mask = _tril_mask()
        for g in range(A_GROUPS):
            wm = jnp.where(mask, ws_ref[g], 0.0).astype(BF16)
            cols = slice(g * A_GROUP_DIM, (g + 1) * A_GROUP_DIM)
            for c in range(nch):
                rows = slice(c * CHUNK, (c + 1) * CHUNK)
                sv = _nn(wm, vn[rows, cols]) + bst_ref[:, g:g + 1]
                gated_ref[rows, cols] = (u[rows, cols] * sv).astype(BF16)
        m = _nn(gated_ref[...], _two_d(wout_ref))
        m_ref[...] = m
        x1_ref[...] = xv + _rms_fwd(m, gpost_ref[...])

    vec = lambda w: _resident((1, w), (0, 0))
    return _call(
        body, name=name, grid=(t // tm,),
        in_specs=[_rows(tm, d), vec(d), _wspec(lay.a_in_rows, lay.a_in[j], d), vec(aw), vec(aw),
                  _resident((A_GROUPS, CHUNK, CHUNK), (0, 0, 0)), _resident((CHUNK, A_GROUPS), (0, 0)),
                  _wspec(lay.a_out_rows, lay.a_out[j], d), vec(d)],
        out_specs=[_rows(tm, d), _rows(tm, d), _rows(tm, 2 * aw), _rows(tm, aw), _rows(tm, aw), _rows(tm, 128),
                   _rows(tm, aw), _rows(tm, d)],
        out_shape=[jax.ShapeDtypeStruct((t, d), F32), jax.ShapeDtypeStruct((t, d), BF16),
                   jax.ShapeDtypeStruct((t, 2 * aw), BF16), jax.ShapeDtypeStruct((t, aw), BF16),
                   jax.ShapeDtypeStruct((t, aw), BF16), jax.ShapeDtypeStruct((t, 128), F32),
                   jax.ShapeDtypeStruct((t, aw), BF16), jax.ShapeDtypeStruct((t, d), F32)],
        compiler_params=_params("parallel"),
    )(x, gpre, wg, lng, lnb, ws, bst, wg, gpost)


def _a_bwd(dx1, m, x, gp, u, vh, rs, gpre, wg, lay, j, lng, lnb, ws, bst, gpost, after, tm, name):
    t, d = x.shape
    aw = 2 * d
    nch = tm // CHUNK

    def body(dx1_ref, m_ref, x_ref, gp_ref, u_ref, vh_ref, rs_ref, gpre_ref, win_ref, lng_ref, lnb_ref, ws_ref, bst_ref,
             wout_ref, gpost_ref, after_ref,
             dx_ref, dm_ref, dz_ref, dgpost_ref, dgpre_ref, dlng_ref, dlnb_ref, dws_ref, dbt_ref, dvn_ref):
        @pl.when(pl.program_id(0) == 0)
        def _():
            for r in (dgpost_ref, dgpre_ref, dlng_ref, dlnb_ref, dws_ref, dbt_ref):
                r[...] = jnp.zeros_like(r)

        dy = dx1_ref[...]
        dm, dgpost = _rms_bwd(m_ref[...], gpost_ref[...], dy)
        dgpost_ref[...] += dgpost
        dm_bf = dm.astype(BF16)
        dm_ref[...] = dm_bf
        dgated = _nt(dm_bf, _two_d(wout_ref))

        vh = vh_ref[...].astype(F32)
        rs = rs_ref[:, :1]
        lng_v = lng_ref[...]
        vn = (vh * lng_v + lnb_ref[...]).astype(BF16)
        mask = _tril_mask()
        lane = lax.broadcasted_iota(jnp.int32, (CHUNK, CHUNK), 1)
        for g in range(A_GROUPS):
            wm = jnp.where(mask, ws_ref[g], 0.0).astype(BF16)
            cols = slice(g * A_GROUP_DIM, (g + 1) * A_GROUP_DIM)
            dws_g = jnp.zeros((CHUNK, CHUNK), F32)
            db_g = jnp.zeros((CHUNK, 1), F32)
            for c in range(nch):
                rows = slice(c * CHUNK, (c + 1) * CHUNK)
                vn_cg = vn[rows, cols]
                sv = _nn(wm, vn_cg) + bst_ref[:, g:g + 1]
                dg_cg = dgated[rows, cols]
                dsv = dg_cg * u_ref[rows, cols].astype(F32)
                dsv_bf = dsv.astype(BF16)
                db_g = db_g + jnp.sum(dsv, axis=1, keepdims=True)
                dws_g = dws_g + _nt(dsv_bf, vn_cg)
                dvn_ref[rows, cols] = _tn(wm, dsv_bf)
                dz_ref[rows, cols] = (dg_cg * sv * gp_ref[rows, cols].astype(F32)).astype(BF16)
            dws_ref[g] += jnp.where(mask, dws_g, 0.0)
            dbt_ref[...] += jnp.where(lane == g, db_g, 0.0)
        dvn = dvn_ref[...]
        dlng_ref[...] += jnp.sum(dvn * vh, axis=0, keepdims=True)
        dlnb_ref[...] += jnp.sum(dvn, axis=0, keepdims=True)
        dvh = dvn * lng_v
        dv = rs * (dvh - jnp.mean(dvh, axis=-1, keepdims=True) - vh * jnp.mean(dvh * vh, axis=-1, keepdims=True))
        dz_ref[:, aw:] = (dv * gp_ref[:, aw:].astype(F32)).astype(BF16)
        dh1 = _nn(dz_ref[...], _two_d(win_ref))
        dxp, dgpre = _rms_bwd(x_ref[...], gpre_ref[...], dh1)
        dgpre_ref[...] += dgpre
        dx_ref[...] = dy + dxp

    vec = lambda w: _resident((1, w), (0, 0))
    acc = lambda shape: pl.BlockSpec(shape, lambda i: (0,) * len(shape))
    return _call(
        body, name=name, grid=(t // tm,),
        in_specs=[_rows(tm, d), _rows(tm, d), _rows(tm, d), _rows(tm, 2 * aw), _rows(tm, aw), _rows(tm, aw),
                  _rows(tm, 128), vec(d), _wspec(lay.a_in_rows, lay.a_in[j], d), vec(aw), vec(aw),
                  _resident((A_GROUPS, CHUNK, CHUNK), (0, 0, 0)), _resident((CHUNK, A_GROUPS), (0, 0)),
                  _wspec(lay.a_out_rows, lay.a_out[j], d), vec(d), ANY],
        out_specs=[_rows(tm, d), _rows(tm, d), _rows(tm, 2 * aw), acc((1, d)), acc((1, d)), acc((1, aw)), acc((1, aw)),
                   acc((A_GROUPS, CHUNK, CHUNK)), acc((CHUNK, CHUNK))],
        out_shape=[jax.ShapeDtypeStruct((t, d), F32), jax.ShapeDtypeStruct((t, d), BF16),
                   jax.ShapeDtypeStruct((t, 2 * aw), BF16), jax.ShapeDtypeStruct((1, d), F32),
                   jax.ShapeDtypeStruct((1, d), F32), jax.ShapeDtypeStruct((1, aw), F32),
                   jax.ShapeDtypeStruct((1, aw), F32), jax.ShapeDtypeStruct((A_GROUPS, CHUNK, CHUNK), F32),
                   jax.ShapeDtypeStruct((CHUNK, CHUNK), F32)],
        scratch_shapes=[pltpu.VMEM((tm, aw), F32)],
        compiler_params=_params("arbitrary"),
    )(dx1, m, x, gp, u, vh, rs, gpre, wg, lng, lnb, ws, bst, wg, gpost, after)


def _window_counts(first_row, n, win):
    tpos = first_row + lax.broadcasted_iota(jnp.int32, (n, 1), 0)
    return jnp.clip(tpos + 1, 1, win).astype(F32)


def _b_fwd(x, gpre, wg, lay, j, wgrp, scale, gpost, tm, name):
    t, d = x.shape
    n = tm + HALO
    ngrp = len(B_WINDOWS)

    def body(x_ref, xprev_ref, gpre_ref, win_ref, wgrp_ref, scale_ref, wout_ref, gpost_ref,
             x1_ref, h1_ref, pooled_ref, mixed_ref, m_ref):
        i = pl.program_id(0)
        xv = x_ref[...]
        keep = jnp.where(i > 0, 1.0, 0.0)
        xe = jnp.concatenate([xprev_ref[...] * keep, xv], axis=0)
        h1e = _rms_fwd(xe, gpre_ref[...]).astype(BF16)
        h1_ref[...] = h1e[HALO:]
        p = _nn(h1e, _two_d(win_ref))
        acc = p
        shift = 1
        for g, win in enumerate(B_WINDOWS):
            lo = g * B_GROUP_DIM
            if g > 0:
                acc = acc[:, B_GROUP_DIM:]
            while shift < win:
                acc = acc + pltpu.roll(acc, shift, 0)
                shift *= 2
            cnt = _window_counts(i * tm - HALO, n, win)
            pooled = acc[:, :B_GROUP_DIM] / cnt - p[:, lo:lo + B_GROUP_DIM]
            pooled_ref[:, lo:lo + B_GROUP_DIM] = pooled[HALO:].astype(BF16)
        for g in range(ngrp):
            cols = slice(g * B_GROUP_DIM, (g + 1) * B_GROUP_DIM)
            raw = _nn(pooled_ref[:, cols], wgrp_ref[g])
            mixed_ref[:, cols] = (raw * scale_ref[:, cols]).astype(BF16)
        m = _nn(mixed_ref[...], _two_d(wout_ref))
        m_ref[...] = m
        x1_ref[...] = xv + _rms_fwd(m, gpost_ref[...])

    vec = lambda w: _resident((1, w), (0, 0))
    per = tm // HALO
    return _call(
        body, name=name, grid=(t // tm,),
        in_specs=[_rows(tm, d), pl.BlockSpec((HALO, d), lambda i: (jnp.maximum(i * per - 1, 0), 0)), vec(d),
                  _wspec(lay.b_rows, lay.b_in[j], d), _resident((ngrp, B_GROUP_DIM, B_GROUP_DIM), (0, 0, 0)), vec(d),
                  _wspec(lay.b_rows, lay.b_out[j], d), vec(d)],
        out_specs=[_rows(tm, d)] * 5,
        out_shape=[jax.ShapeDtypeStruct((t, d), F32), jax.ShapeDtypeStruct((t, d), BF16),
                   jax.ShapeDtypeStruct((t, d), BF16), jax.ShapeDtypeStruct((t, d), BF16),
                   jax.ShapeDtypeStruct((t, d), F32)],
        compiler_params=_params("parallel"),
    )(x, x, gpre, wg, wgrp, scale, wg, gpost)


def _b_bwd(dx1, m, x, pooled, gpre, wg, lay, j, wgrp, scale, gpost, after, tm, name):
    t, d = x.shape
    n = tm + HALO
    ngrp = len(B_WINDOWS)
    steps = t // tm

    def body(dx1_ref, dx1n_ref, m_ref, mn_ref, x_ref, pooled_ref, pooledn_ref, gpre_ref, win_ref, wgrp_ref, scale_ref,
             wout_ref, gpost_ref, after_ref,
             dx_ref, dm_ref, draw_ref, dp_ref, dgpost_ref, dgpre_ref, dscale_ref, dpool_ref):
        i = pl.program_id(0)

        @pl.when(i == 0)
        def _():
            for r in (dgpost_ref, dgpre_ref, dscale_ref):
                r[...] = jnp.zeros_like(r)

        keep = jnp.where(i < steps - 1, 1.0, 0.0)
        dy = dx1_ref[...]
        dye = jnp.concatenate([dy, dx1n_ref[...] * keep], axis=0)
        me = jnp.concatenate([m_ref[...], mn_ref[...]], axis=0)
        gpost_v = gpost_ref[...]
        r = lax.rsqrt(jnp.mean(me * me, axis=-1, keepdims=True) + EPS)
        mh = me * r
        dgpost_ref[...] += jnp.sum((dye * mh)[:tm], axis=0, keepdims=True)
        dmh = dye * gpost_v
        dme = (r * (dmh - mh * jnp.mean(dmh * mh, axis=-1, keepdims=True))).astype(BF16)
        dm_ref[...] = dme[:tm]
        dmixed = _nt(dme, _two_d(wout_ref))
        pooled_e = jnp.concatenate([pooled_ref[...], pooledn_ref[...]], axis=0)
        scale_v = scale_ref[...]
        for g, win in enumerate(B_WINDOWS):
            cols = slice(g * B_GROUP_DIM, (g + 1) * B_GROUP_DIM)
            raw = _nn(pooled_e[:, cols], wgrp_ref[g])
            dscale_ref[:, cols] += jnp.sum((dmixed[:, cols] * raw)[:tm], axis=0, keepdims=True)
            draw = (dmixed[:, cols] * scale_v[:, cols]).astype(BF16)
            draw_ref[:, cols] = draw[:tm]
            dpool = _nt(draw, wgrp_ref[g])
            acc = dpool / _window_counts(i * tm, n, win)
            shift = 1
            while shift < win:
                acc = acc + pltpu.roll(acc, n - shift, 0)
                shift *= 2
            dpool_ref[:, cols] = (acc - dpool)[:tm]
        dp = dpool_ref[...].astype(BF16)
        dp_ref[...] = dp
        dh1 = _nt(dp, _two_d(win_ref))
        dxp, dgpre = _rms_bwd(x_ref[...], gpre_ref[...], dh1)
        dgpre_ref[...] += dgpre
        dx_ref[...] = dy + dxp

    vec = lambda w: _resident((1, w), (0, 0))
    acc = lambda shape: pl.BlockSpec(shape, lambda i: (0,) * len(shape))
    per = tm // HALO
    nxt = lambda i: (jnp.minimum((i + 1) * per, t // HALO - 1), 0)
    return _call(
        body, name=name, grid=(steps,),
        in_specs=[_rows(tm, d), pl.BlockSpec((HALO, d), nxt), _rows(tm, d), pl.BlockSpec((HALO, d), nxt), _rows(tm, d),
                  _rows(tm, d), pl.BlockSpec((HALO, d), nxt), vec(d), _wspec(lay.b_rows, lay.b_in[j], d),
                  _resident((ngrp, B_GROUP_DIM, B_GROUP_DIM), (0, 0, 0)), vec(d), _wspec(lay.b_rows, lay.b_out[j], d),
                  vec(d), ANY],
        out_specs=[_rows(tm, d)] * 4 + [acc((1, d))] * 3,
        out_shape=[jax.ShapeDtypeStruct((t, d), F32), jax.ShapeDtypeStruct((t, d), BF16),
                   jax.ShapeDtypeStruct((t, d), BF16), jax.ShapeDtypeStruct((t, d), BF16)]
                  + [jax.ShapeDtypeStruct((1, d), F32)] * 3,
        scratch_shapes=[pltpu.VMEM((tm, d), F32)],
        compiler_params=_params("arbitrary"),
    )(dx1, dx1, m, m, x, pooled, pooled, gpre, wg, wgrp, scale, wg, gpost, after)


def _f_fwd(x1, gpre, wg, lay, l, gpost, tm, name):
    t, d = x1.shape
    hid = N_DEV * lay.ffn_rows

    def body(x_ref, gpre_ref, wgate_ref, wup_ref, wdown_ref, gpost_ref, x2_ref, h2_ref, a_ref, b_ref, s_ref, f_ref):
        xv = x_ref[...]
        h2 = _rms_fwd(xv, gpre_ref[...]).astype(BF16)
        h2_ref[...] = h2
        a = _nt(h2, _two_d(wgate_ref))
        b = _nt(h2, _two_d(wup_ref))
        sig = 1.0 / (1.0 + jnp.exp(-a))
        silu = a * sig
        a_ref[...] = (b * (sig + silu * (1.0 - sig))).astype(BF16)
        b_ref[...] = silu.astype(BF16)
        s = (silu * b).astype(BF16)
        s_ref[...] = s
        f = _nn(s, _two_d(wdown_ref))
        f_ref[...] = f
        x2_ref[...] = xv + _rms_fwd(f, gpost_ref[...])

    vec = lambda w: _resident((1, w), (0, 0))
    return _call(
        body, name=name, grid=(t // tm,),
        in_specs=[_rows(tm, d), vec(d), _wspec(lay.ffn_rows, lay.gate[l], d), _wspec(lay.ffn_rows, lay.up[l], d),
                  _wspec(lay.ffn_rows, lay.down[l], d), vec(d)],
        out_specs=[_rows(tm, d), _rows(tm, d), _rows(tm, hid), _rows(tm, hid), _rows(tm, hid), _rows(tm, d)],
        out_shape=[jax.ShapeDtypeStruct((t, d), F32), jax.ShapeDtypeStruct((t, d), BF16),
                   jax.ShapeDtypeStruct((t, hid), BF16), jax.ShapeDtypeStruct((t, hid), BF16),
                   jax.ShapeDtypeStruct((t, hid), BF16), jax.ShapeDtypeStruct((t, d), F32)],
        compiler_params=_params("parallel"),
    )(x1, gpre, wg, wg, wg, gpost)


def _f_bwd(dx2, f, x1, a, b, gpre, wg, lay, l, gpost, after, tm, name):
    t, d = x1.shape
    hid = N_DEV * lay.ffn_rows

    def body(dx2_ref, f_ref, x_ref, a_ref, b_ref, gpre_ref, wgate_ref, wup_ref, wdown_ref, gpost_ref, after_ref,
             dx1_ref, df_ref, da_ref, db_ref, dgpost_ref, dgpre_ref):
        @pl.when(pl.program_id(0) == 0)
        def _():
            dgpost_ref[...] = jnp.zeros_like(dgpost_ref)
            dgpre_ref[...] = jnp.zeros_like(dgpre_ref)

        dy = dx2_ref[...]
        df, dgpost = _rms_bwd(f_ref[...], gpost_ref[...], dy)
        dgpost_ref[...] += dgpost
        df_bf = df.astype(BF16)
        df_ref[...] = df_bf
        ds = _nt(df_bf, _two_d(wdown_ref))
        da = (ds * a_ref[...].astype(F32)).astype(BF16)
        db = (ds * b_ref[...].astype(F32)).astype(BF16)
        da_ref[...] = da
        db_ref[...] = db
        dh2 = _nn(da, _two_d(wgate_ref)) + _nn(db, _two_d(wup_ref))
        dxp, dgpre = _rms_bwd(x_ref[...], gpre_ref[...], dh2)
        dgpre_ref[...] += dgpre
        dx1_ref[...] = dy + dxp

    vec = lambda w: _resident((1, w), (0, 0))
    acc = pl.BlockSpec((1, d), lambda i: (0, 0))
    return _call(
        body, name=name, grid=(t // tm,),
        in_specs=[_rows(tm, d), _rows(tm, d), _rows(tm, d), _rows(tm, hid), _rows(tm, hid), vec(d),
                  _wspec(lay.ffn_rows, lay.gate[l], d), _wspec(lay.ffn_rows, lay.up[l], d),
                  _wspec(lay.ffn_rows, lay.down[l], d), vec(d), ANY],
        out_specs=[_rows(tm, d), _rows(tm, d), _rows(tm, hid), _rows(tm, hid), acc, acc],
        out_shape=[jax.ShapeDtypeStruct((t, d), F32), jax.ShapeDtypeStruct((t, d), BF16),
                   jax.ShapeDtypeStruct((t, hid), BF16), jax.ShapeDtypeStruct((t, hid), BF16),
                   jax.ShapeDtypeStruct((1, d), F32), jax.ShapeDtypeStruct((1, d), F32)],
        compiler_params=_params("arbitrary"),
    )(dx2, f, x1, a, b, gpre, wg, wg, wg, gpost, after)


def _loss_head(y, target, tm, name):
    t, d = y.shape

    def body(y_ref, t_ref, dy_ref, loss_ref):
        @pl.when(pl.program_id(0) == 0)
        def _():
            loss_ref[...] = jnp.zeros_like(loss_ref)

        diff = y_ref[...] - t_ref[...]
        dy_ref[...] = diff * (1.0 / d)
        sq = jnp.sum(jnp.sum(diff * diff, axis=0, keepdims=True), axis=1, keepdims=True)
        loss_ref[...] += sq * (0.5 / d)

    return _call(
        body, name=name, grid=(t // tm,),
        in_specs=[_rows(tm, d), _rows(tm, d)],
        out_specs=[_rows(tm, d), pl.BlockSpec((8, 128), lambda i: (0, 0))],
        out_shape=[jax.ShapeDtypeStruct((t, d), F32), jax.ShapeDtypeStruct((8, 128), F32)],
        compiler_params=_params("arbitrary"),
    )(y, target)


def _grad_into(gbuf, lhs, rhs, off, rows, name, after=None):
    t, m = lhs.shape
    d = rhs.shape[1]
    assert m == N_DEV * rows and off % rows == 0
    per_tile = {352: 4, 512: 2, 256: 4, 128: 8}[rows]
    tm = per_tile * rows
    assert tm % 128 == 0 and rows % 16 == 0
    tk = 2048 if t % 2048 == 0 else 256
    ksteps = t // tk
    fresh = isinstance(gbuf, int)
    shape = (N_DEV, gbuf, d) if fresh else gbuf.shape
    extra = ([] if fresh else [gbuf]) + ([] if after is None else [after])

    def body(l_ref, r_ref, *rest):
        o_ref, acc_ref = rest[-2:]
        k = pl.program_id(1)

        @pl.when(k == 0)
        def _():
            acc_ref[...] = jnp.zeros_like(acc_ref)

        acc_ref[...] += _tn(l_ref[...], r_ref[...])

        @pl.when(k == ksteps - 1)
        def _():
            o_ref[...] = acc_ref[...].reshape(per_tile, rows, d).astype(BF16)

    return _call(
        body, name=name, grid=(N_DEV // per_tile, ksteps),
        in_specs=[pl.BlockSpec((tk, tm), lambda i, k: (k, i)), pl.BlockSpec((tk, d), lambda i, k: (k, 0))]
                 + [ANY] * len(extra),
        out_specs=pl.BlockSpec((per_tile, rows, d), lambda i, k: (i, off // rows, 0)),
        out_shape=jax.ShapeDtypeStruct(shape, BF16),
        scratch_shapes=[pltpu.VMEM((tm, d), F32)],
        input_output_aliases={} if fresh else {2: 0},
        compiler_params=_params("parallel", "arbitrary"),
    )(lhs, rhs, *extra)


def _grad_grouped(pooled, draw, name):
    t, d = pooled.shape
    ngrp = len(B_WINDOWS)
    tk = 1024 if t % 1024 == 0 else 256

    def body(p_ref, q_ref, o_ref):
        @pl.when(pl.program_id(0) == 0)
        def _():
            o_ref[...] = jnp.zeros_like(o_ref)

        for g in range(ngrp):
            cols = slice(g * B_GROUP_DIM, (g + 1) * B_GROUP_DIM)
            o_ref[g] += _tn(p_ref[:, cols], q_ref[:, cols])

    return _call(
        body, name=name, grid=(t // tk,),
        in_specs=[_rows(tk, d), _rows(tk, d)],
        out_specs=pl.BlockSpec((ngrp, B_GROUP_DIM, B_GROUP_DIM), lambda i: (0, 0, 0)),
        out_shape=jax.ShapeDtypeStruct((ngrp, B_GROUP_DIM, B_GROUP_DIM), F32),
        compiler_params=_params("arbitrary"),
    )(pooled, draw)


def _peers():
    x, y, c = lax.axis_index("x"), lax.axis_index("y"), lax.axis_index("c")
    flip = lambda v, f: 1 - v if f else v
    peers = []
    for r in range(1, N_DEV):
        px, py, pc = flip(x, r & 4), flip(y, r & 2), flip(c, r & 1)
        peers.append(((px, py, pc), 4 * px + 2 * py + pc))
    return 4 * x + 2 * y + c, peers


HBM = pl.BlockSpec(memory_space=pltpu.HBM)
SEM = pl.BlockSpec(memory_space=pltpu.SEMAPHORE)
EFFECT = pltpu.SideEffectType.DATAFLOW_SIDE_EFFECTING


def _peer_copies(scatter, srcs, lands, send_sems, recv_sems):
    me, peers = _peers()
    copies = []
    for a in range(len(srcs)):
        for r, (peer, pidx) in enumerate(peers):
            src = srcs[a].at[pidx] if scatter else srcs[a]
            mine = lands[a].at[r] if scatter else lands[a].at[pidx]
            theirs = lands[a].at[r] if scatter else lands[a].at[me]
            send = pltpu.make_async_remote_copy(src_ref=src, dst_ref=theirs, send_sem=send_sems[a].at[r],
                                                recv_sem=recv_sems[a].at[r], device_id=peer, device_id_type=MESH)
            recv = pltpu.make_async_remote_copy(src_ref=src, dst_ref=mine, send_sem=send_sems[a].at[r],
                                                recv_sem=recv_sems[a].at[r], device_id=peer, device_id_type=MESH)
            copies.append((send, recv))
    return copies


def _exchange_start(scatter, srcs, lands, after, name):
    n = len(srcs)

    def body(*refs):
        src_refs, land_refs = refs[:n], refs[n:2 * n]
        outs = refs[2 * n + 1:]
        send_sems, recv_sems, token = outs[:n], outs[n:2 * n], outs[-1]
        for send, _ in _peer_copies(scatter, src_refs, land_refs, send_sems, recv_sems):
            send.start()
        token[...] = jnp.zeros_like(token)

    hbm = lambda a: pltpu.with_memory_space_constraint(a, pltpu.HBM)
    res = _call(
        body, name=name,
        in_specs=[HBM] * (2 * n) + [ANY],
        out_specs=[SEM] * (2 * n) + [HBM] * (2 * n) + [pl.BlockSpec(memory_space=pltpu.VMEM)],
        out_shape=[pltpu.SemaphoreType.DMA((N_DEV - 1,))] * (2 * n)
                  + [pltpu.HBM(a.shape, a.dtype) for a in list(srcs) + list(lands)]
                  + [jax.ShapeDtypeStruct((8, 128), F32)],
        input_output_aliases={i: 2 * n + i for i in range(2 * n)},
        compiler_params=pltpu.CompilerParams(has_side_effects=EFFECT),
    )(*[hbm(a) for a in srcs], *[hbm(a) for a in lands], after)
    return res[:n], res[n:2 * n], res[2 * n:3 * n], res[3 * n:4 * n], res[-1]


def _exchange_wait(scatter, send_sems, recv_sems, srcs, lands, after, name):
    n = len(srcs)
    after = list(after) if isinstance(after, (list, tuple)) else [after]

    def body(*refs):
        src_refs, land_refs = refs[:n], refs[n:2 * n]
        send_refs, recv_refs = refs[2 * n:3 * n], refs[3 * n:4 * n]
        for send, recv in _peer_copies(scatter, src_refs, land_refs, send_refs, recv_refs):
            send.wait_send()
            recv.wait_recv()

    res = _call(
        body, name=name,
        in_specs=[HBM] * (2 * n) + [SEM] * (2 * n) + [ANY] * len(after),
        out_specs=[HBM] * (2 * n),
        out_shape=[pltpu.HBM(a.shape, a.dtype) for a in list(srcs) + list(lands)],
        input_output_aliases={i: i for i in range(2 * n)},
        compiler_params=pltpu.CompilerParams(has_side_effects=EFFECT),
    )(*srcs, *lands, *send_sems, *recv_sems, *after)
    return res[:n], res[n:]


def _row_tile(rows):
    if rows <= 512:
        return rows
    return max([tr for tr in range(16, 513, 16) if rows % tr == 0] or [rows])


def _sum_parts(own, got, me, name):
    _, rows, w = own.shape
    tr = _row_tile(rows)

    def body(me_ref, a_ref, b_ref, o_ref):
        s = a_ref[...].astype(F32)
        for j in range(N_DEV - 1):
            s = s + b_ref[j].astype(F32)
        o_ref[...] = s

    return _call(
        body, name=name,
        grid_spec=pltpu.PrefetchScalarGridSpec(
            num_scalar_prefetch=1, grid=(rows // tr,),
            in_specs=[pl.BlockSpec((None, tr, w), lambda i, me_ref: (me_ref[0], i, 0)),
                      pl.BlockSpec((N_DEV - 1, tr, w), lambda i, me_ref: (0, i, 0))],
            out_specs=pl.BlockSpec((tr, w), lambda i, me_ref: (i, 0))),
        out_shape=jax.ShapeDtypeStruct((rows, w), F32),
        compiler_params=_params("parallel"),
    )(me, own, got)


def _sum_devices(stacked, name):
    k, rows, w = stacked.shape
    tr = _row_tile(rows)

    def body(a_ref, o_ref):
        s = a_ref[0]
        for j in range(1, k):
            s = s + a_ref[j]
        o_ref[...] = s

    return _call(
        body, name=name, grid=(rows // tr,),
        in_specs=[pl.BlockSpec((k, tr, w), lambda i: (0, i, 0))],
        out_specs=pl.BlockSpec((tr, w), lambda i: (i, 0)),
        out_shape=jax.ShapeDtypeStruct((rows, w), F32),
        compiler_params=_params("parallel"),
    )(stacked)


def _adamw(w, g, m, v, name):
    rows, cols = w.shape
    tr = _row_tile(rows)

    def body(w_ref, g_ref, m_ref, v_ref, d_ref, nm_ref, nv_ref):
        gv = g_ref[...]
        nm = ADAM_B1 * m_ref[...] + (1.0 - ADAM_B1) * gv
        nv = ADAM_B2 * v_ref[...] + (1.0 - ADAM_B2) * (gv * gv)
        m_hat = nm / (1.0 - ADAM_B1 ** ADAM_STEP)
        v_hat = nv / (1.0 - ADAM_B2 ** ADAM_STEP)
        d_ref[...] = -ADAM_LR * (m_hat / (jnp.sqrt(v_hat) + ADAM_EPS) + ADAM_WD * w_ref[...])
        nm_ref[...] = nm
        nv_ref[...] = nv

    spec = pl.BlockSpec((tr, cols), lambda i: (i, 0))
    return _call(
        body, name=name, grid=(rows // tr,),
        in_specs=[spec] * 4, out_specs=[spec] * 3,
        out_shape=[jax.ShapeDtypeStruct((rows, cols), F32)] * 3,
        compiler_params=_params("parallel"),
    )(w, g, m, v)


SMALL = ("a_ln_g", "a_ln_b", "a_w_s", "a_b_s", "mix_pre_g", "mix_post_g", "ffn_pre_g", "ffn_post_g")


def _pack_small(parts, d, last_row=None):
    rows = [parts[k].reshape(-1, d) for k in SMALL] + ([] if last_row is None else [last_row])
    flat = jnp.concatenate(rows, axis=0)
    return jnp.pad(flat, ((0, -flat.shape[0] % 8), (0, 0)))


def _unpack_small(flat, like):
    out, r = {}, 0
    for k in SMALL:
        n = like[k].size // flat.shape[1]
        out[k] = flat[r:r + n].reshape(like[k].shape)
        r += n
    return out


def kernel(x, a_w_in, a_ln_g, a_ln_b, a_w_s, a_b_s, a_w_out, b_w_in, b_w_grp, b_scale, b_w_out, mix_pre_g, mix_post_g, ffn_pre_g, ffn_post_g, ffn_w_gate, ffn_w_up, ffn_w_down, loss_target, m_a_w_in, m_a_ln_g, m_a_ln_b, m_a_w_s, m_a_b_s, m_a_w_out, m_b_w_in, m_b_w_grp, m_b_scale, m_b_w_out, m_mix_pre_g, m_mix_post_g, m_ffn_pre_g, m_ffn_post_g, m_ffn_w_gate, m_ffn_w_up, m_ffn_w_down, v_a_w_in, v_a_ln_g, v_a_ln_b, v_a_w_s, v_a_b_s, v_a_w_out, v_b_w_in, v_b_w_grp, v_b_scale, v_b_w_out, v_mix_pre_g, v_mix_post_g, v_ffn_pre_g, v_ffn_post_g, v_ffn_w_gate, v_ffn_w_up, v_ffn_w_down):
    args = dict(locals())
    names = ("a_w_in", "a_ln_g", "a_ln_b", "a_w_s", "a_b_s", "a_w_out", "b_w_in", "b_w_grp", "b_scale", "b_w_out",
             "mix_pre_g", "mix_post_g", "ffn_pre_g", "ffn_post_g", "ffn_w_gate", "ffn_w_up", "ffn_w_down")
    w = {k: args[k] for k in names}
    mom = {k: args["m_" + k] for k in names}
    var = {k: args["v_" + k] for k in names}

    t, d = x.shape[1], x.shape[2]
    ffn_local = ffn_w_gate.shape[2]
    lay = _Layout(d, ffn_local)
    me = 4 * lax.axis_index("x") + 2 * lax.axis_index("y") + lax.axis_index("c")
    me1 = jnp.reshape(me, (1,)).astype(jnp.int32)

    def landing(block):
        zone = lax.empty((N_DEV,) + block.shape, block.dtype)
        return lax.dynamic_update_slice(zone, block[None], (me,) + (0,) * block.ndim)

    def pack(i, mixer, zero):
        j = i // 2
        if not mixer:
            parts = [ffn_w_gate[i].T, ffn_w_up[i].T, ffn_w_down[i]]
        elif i % 2 == 0:
            parts = [a_w_in[j].T, a_w_out[j]]
        else:
            parts = [b_w_in[j], b_w_out[j]]
        return (jnp.concatenate(parts, axis=0) + zero).astype(BF16)

    nsub = 2 * DEPTH
    wg = [None] * nsub
    packed0 = pack(0, True, 0.0)
    first = _exchange_start(False, [packed0], [landing(packed0)], jnp.zeros((8, 128), F32), "gather_first_start")
    zero = first[4][0, 0]
    packed = [packed0] + [pack(k // 2, k % 2 == 0, zero) for k in range(1, nsub)]
    ngrp = len(B_WINDOWS)
    grp_local = b_w_grp.shape[2]
    sdev = b_scale.shape[1]
    side_rows = 2 * ngrp * grp_local
    side = jnp.concatenate(
        [b_w_grp.reshape(side_rows, B_GROUP_DIM),
         jnp.pad(b_scale, ((0, 6), (0, B_GROUP_DIM - sdev)))], axis=0) + zero
    later = [side] + packed[1:]
    send_sems, recv_sems, later, zones, token = _exchange_start(
        False, later, [landing(b) for b in later], first[4], "gather_start")
    _, (wg[0],) = _exchange_wait(False, *first[:4], token, "gather_first_wait")

    def gathered(k, after):
        _, (zone,) = _exchange_wait(False, [send_sems[k]], [recv_sems[k]], [later[k]], [zones[k]], after,
                                    f"gather_wait_{k}")
        return zone

    row = lambda a: a.reshape(1, -1)
    bst = jnp.transpose(a_b_s, (0, 2, 1))

    tm = 256 if t % 256 == 0 else CHUNK
    tm_abwd = tm

    saved = []
    h = x[0]
    wgrp_full = scale_full = None
    for i in range(DEPTH):
        j = i // 2
        gpre = row(mix_pre_g[i])
        if i > 0:
            wg[2 * i] = gathered(2 * i, h)
        if i % 2 == 0:
            x1, h1, gp, u, vh, rs, gated, m = _a_fwd(h, gpre, wg[2 * i], lay, j, row(a_ln_g[j]), row(a_ln_b[j]),
                                                     a_w_s[j], bst[j], row(mix_post_g[i]), tm, f"a_fwd_{j}")
            mix = dict(h1=h1, gp=gp, u=u, vh=vh, rs=rs, gated=gated, m=m)
        else:
            if wgrp_full is None:
                side_g = gathered(0, h)
                wgrp_full = (side_g[:, :side_rows].reshape(N_DEV, 2, ngrp, grp_local, B_GROUP_DIM)
                             .transpose(1, 2, 0, 3, 4).reshape(2, ngrp, B_GROUP_DIM, B_GROUP_DIM).astype(BF16))
                scale_full = (side_g[:, side_rows:side_rows + 2, :sdev].transpose(1, 0, 2)
                              .reshape(2, 1, N_DEV * sdev))
            x1, h1, pooled, mixed, m = _b_fwd(h, gpre, wg[2 * i], lay, j, wgrp_full[j], scale_full[j],
                                              row(mix_post_g[i]), tm, f"b_fwd_{j}")
            mix = dict(h1=h1, pooled=pooled, mixed=mixed, m=m)
        wg[2 * i + 1] = gathered(2 * i + 1, x1)
        x2, h2, a, b, s, f = _f_fwd(x1, row(ffn_pre_g[i]), wg[2 * i + 1], lay, i, row(ffn_post_g[i]), tm,
                                    f"f_fwd_{i}")
        saved.append(dict(x=h, x1=x1, mix=mix, h2=h2, a=a, b=b, s=s, f=f))
        h = x2

    dy, loss_acc = _loss_head(h, loss_target[0], tm, "loss_head")

    small_g = {k: [None] * w[k].shape[0] for k in SMALL}
    dgrp, dscale = [None, None], [None, None]
    pending = [None] * nsub
    token = jnp.zeros((8, 128), F32)

    def scatter(k, gbuf):
        got = pltpu.with_memory_space_constraint(lax.empty((N_DEV - 1,) + gbuf.shape[1:], gbuf.dtype), pltpu.HBM)
        ss, rs, src, zone, tok = _exchange_start(True, [gbuf], [got], token, f"scatter_start_{k}")
        pending[k] = (ss, rs, src, zone)
        return tok

    def small_exchanges():
        side_grad = jnp.concatenate(
            [jnp.stack(dgrp).reshape(2, ngrp, N_DEV, grp_local, B_GROUP_DIM).transpose(2, 0, 1, 3, 4)
             .reshape(N_DEV, side_rows, B_GROUP_DIM),
             jnp.pad(jnp.stack(dscale).reshape(2, N_DEV, sdev).transpose(1, 0, 2),
                     ((0, 0), (0, 6), (0, B_GROUP_DIM - sdev)))], axis=1)
        small_part = _pack_small({k: jnp.stack(small_g[k]) for k in SMALL}, d,
                                 jnp.broadcast_to(loss_acc[:1, :1], (1, d)))
        got = pltpu.with_memory_space_constraint(lax.empty((N_DEV - 1,) + side_grad.shape[1:], F32), pltpu.HBM)
        side_x = _exchange_start(True, [side_grad], [got], token, "side_scatter_start")
        small_x = _exchange_start(False, [small_part], [landing(small_part)], side_x[4], "small_gather_start")
        return side_x[:4], small_x[:4], small_x[4]

    for i in reversed(range(DEPTH)):
        sv = saved[i]
        j = i // 2
        wf, wm = wg[2 * i + 1], wg[2 * i]
        dx1, df, da, db, dgpost, dgpre = _f_bwd(dy, sv["f"], sv["x1"], sv["a"], sv["b"], row(ffn_pre_g[i]), wf, lay, i,
                                                 row(ffn_post_g[i]), token, tm, f"f_bwd_{i}")
        small_g["ffn_post_g"][i], small_g["ffn_pre_g"][i] = dgpost[0], dgpre[0]
        gbuf = _grad_into(lay.f_total, da, sv["h2"], lay.gate[i], lay.ffn_rows, f"g_gate_{i}")
        gbuf = _grad_into(gbuf, db, sv["h2"], lay.up[i], lay.ffn_rows, f"g_up_{i}")
        gbuf = _grad_into(gbuf, sv["s"], df, lay.down[i], lay.ffn_rows, f"g_down_{i}")
        token = scatter(2 * i + 1, gbuf)
        mix = sv["mix"]
        gpost = row(mix_post_g[i])
        if i % 2 == 0:
            dx, dm, dz, dgpost, dgpre, dlng, dlnb, dws, dbt = _a_bwd(
                dx1, mix["m"], sv["x"], mix["gp"], mix["u"], mix["vh"], mix["rs"], row(mix_pre_g[i]), wm, lay, j,
                row(a_ln_g[j]), row(a_ln_b[j]), a_w_s[j], bst[j], gpost, token, tm_abwd, f"a_bwd_{j}")
            small_g["a_ln_g"][j], small_g["a_ln_b"][j] = dlng[0], dlnb[0]
            small_g["a_w_s"][j], small_g["a_b_s"][j] = dws, dbt[:, :A_GROUPS].T
            small_g["mix_post_g"][i], small_g["mix_pre_g"][i] = dgpost[0], dgpre[0]
            order = None
            if i == 0:
                side_x, small_x, order = small_exchanges()
            gbuf = _grad_into(lay.a_total, dz, mix["h1"], lay.a_in[j], lay.a_in_rows, f"g_a_in_{j}", after=order)
            gbuf = _grad_into(gbuf, mix["gated"], dm, lay.a_out[j], lay.a_out_rows, f"g_a_out_{j}")
        else:
            dx, dm, draw, dp, dgpost, dgpre, dsc = _b_bwd(
                dx1, mix["m"], sv["x"], mix["pooled"], row(mix_pre_g[i]), wm, lay, j, wgrp_full[j], scale_full[j],
                gpost, token, tm, f"b_bwd_{j}")
            dscale[j] = dsc[0]
            dgrp[j] = _grad_grouped(mix["pooled"], draw, f"g_b_grp_{j}")
            gbuf = _grad_into(lay.b_total, mix["h1"], dp, lay.b_in[j], lay.b_rows, f"g_b_in_{j}")
            gbuf = _grad_into(gbuf, mix["mixed"], dm, lay.b_out[j], lay.b_rows, f"g_b_out_{j}")
            small_g["mix_post_g"][i], small_g["mix_pre_g"][i] = dgpost[0], dgpre[0]
        token = scatter(2 * i, gbuf)
        dy = dx
    grad_x = dy[None]

    g_sub = [None] * nsub

    def arrived(k, after):
        ss, rs, src, zone = pending[k]
        (own,), (got,) = _exchange_wait(True, ss, rs, src, zone, after, f"scatter_wait_{k}")
        g_sub[k] = _sum_parts(own, got, me1, f"sum_grads_{k}")

    def rows_of(k, off, n):
        return g_sub[k][off:off + n]

    grads, delta, new_m, new_v = {}, {}, {}, {}

    def update(k):
        shape = w[k].shape
        two = lambda a: a.reshape(-1, shape[-1])
        dl, nm, nv = _adamw(two(w[k]), two(grads[k]), two(mom[k]), two(var[k]), f"adamw_{k}")
        delta[k], new_m[k], new_v[k] = dl.reshape(shape), nm.reshape(shape), nv.reshape(shape)

    for k in range(1, nsub):
        arrived(k, token)
    grads["ffn_w_gate"] = jnp.stack([rows_of(2 * l + 1, lay.gate[l], ffn_local).T for l in range(DEPTH)])
    grads["ffn_w_up"] = jnp.stack([rows_of(2 * l + 1, lay.up[l], ffn_local).T for l in range(DEPTH)])
    grads["ffn_w_down"] = jnp.stack([rows_of(2 * l + 1, lay.down[l], ffn_local) for l in range(DEPTH)])
    grads["b_w_in"] = jnp.stack([rows_of(4 * j + 2, lay.b_in[j], lay.b_rows) for j in range(2)])
    grads["b_w_out"] = jnp.stack([rows_of(4 * j + 2, lay.b_out[j], lay.b_rows) for j in range(2)])
    early = ("ffn_w_gate", "ffn_w_up", "ffn_w_down", "b_w_in", "b_w_out")
    for k in early:
        update(k)

    (side_own,), (side_got,) = _exchange_wait(True, *side_x, [delta[k] for k in early], "side_scatter_wait")
    g_side = _sum_parts(side_own, side_got, me1, "sum_side")
    grads["b_w_grp"] = g_side[:side_rows].reshape(b_w_grp.shape)
    grads["b_scale"] = g_side[side_rows:side_rows + 2, :sdev]
    update("b_w_grp")
    update("b_scale")
    _, (small_all,) = _exchange_wait(False, *small_x, [delta["b_w_grp"], delta["b_scale"]], "small_gather_wait")
    small_sum = _sum_devices(small_all, "sum_small")
    g_small = _unpack_small(small_sum, w)
    loss = small_sum[sum(w[k].size for k in SMALL) // d, 0]
    grads.update(g_small)
    dl, nm, nv = _adamw(_pack_small(w, d), _pack_small(g_small, d), _pack_small(mom, d), _pack_small(var, d),
                        "adamw_small")
    delta.update(_unpack_small(dl, w))
    new_m.update(_unpack_small(nm, w))
    new_v.update(_unpack_small(nv, w))

    arrived(0, dl)
    grads["a_w_in"] = jnp.stack([rows_of(4 * j, lay.a_in[j], lay.a_in_rows).T for j in range(2)])
    grads["a_w_out"] = jnp.stack([rows_of(4 * j, lay.a_out[j], lay.a_out_rows) for j in range(2)])
    update("a_w_in")
    update("a_w_out")

    return (loss, grad_x, *[grads[k] for k in names], *[delta[k] for k in names], *[new_m[k] for k in names],
            *[new_v[k] for k in names])
```

```python
import math

import jax
import jax.numpy as jnp
from jax import lax
from jax.experimental import pallas as pl
from jax.experimental.pallas import tpu as pltpu

F32 = jnp.float32
BF16 = jnp.bfloat16
MESH = pl.DeviceIdType.MESH
ANY = pl.BlockSpec(memory_space=pl.ANY)

N_DEV = 8
EPS = 1e-6
CHUNK = 128
A_GROUPS = 8
A_GROUP_DIM = 256
B_WINDOWS = (2, 4, 8, 16)
B_GROUP_DIM = 256
HALO = 16
DEPTH = 4

ADAM_LR = 0.001
ADAM_B1 = 0.9
ADAM_B2 = 0.999
ADAM_EPS = 1e-08
ADAM_WD = 0.01
ADAM_STEP = 10

VMEM_LIMIT_BYTES = 60 * 1024 * 1024

ERF_P = 0.3275911
ERF_A = (0.254829592, -0.284496736, 1.421413741, -1.453152027, 1.061405429)
INV_SQRT2 = 1.0 / math.sqrt(2.0)
LOG2_E = 1.0 / math.log(2.0)
INV_SQRT_2PI = 1.0 / math.sqrt(2.0 * math.pi)


def _call(body, **kw):
    return pl.pallas_call(body, **kw)


def _params(*semantics):
    return pltpu.CompilerParams(dimension_semantics=semantics or None, vmem_limit_bytes=VMEM_LIMIT_BYTES)


def _resident(shape, index):
    return pl.BlockSpec(shape, lambda *_: index, pipeline_mode=pl.Buffered(1))


def _rows(tm, width):
    return pl.BlockSpec((tm, width), lambda i: (i, 0))


def _nn(a, b):
    return jnp.dot(a, b, preferred_element_type=F32)


def _nt(a, b):
    return lax.dot_general(a, b, (((1,), (1,)), ((), ())), preferred_element_type=F32)


def _tn(a, b):
    return lax.dot_general(a, b, (((0,), (0,)), ((), ())), preferred_element_type=F32)


def _rms_fwd(x, g):
    r = lax.rsqrt(jnp.mean(x * x, axis=-1, keepdims=True) + EPS)
    return x * r * g


def _rms_bwd(x, g, dy):
    r = lax.rsqrt(jnp.mean(x * x, axis=-1, keepdims=True) + EPS)
    xh = x * r
    dg = jnp.sum(dy * xh, axis=0, keepdims=True)
    dxh = dy * g
    dx = r * (dxh - xh * jnp.mean(dxh * xh, axis=-1, keepdims=True))
    return dx, dg


def _gelu(z):
    phi = 0.5 + 0.5 * lax.erf(z * INV_SQRT2)
    e = jnp.exp2(z * z * (-0.5 * LOG2_E))
    return z * phi, phi + z * e * INV_SQRT_2PI


def _layernorm_stats(v):
    mu = jnp.mean(v, axis=-1, keepdims=True)
    xc = v - mu
    rs = lax.rsqrt(jnp.mean(xc * xc, axis=-1, keepdims=True) + EPS)
    return xc * rs, rs


def _tril_mask():
    r = lax.broadcasted_iota(jnp.int32, (CHUNK, CHUNK), 0)
    c = lax.broadcasted_iota(jnp.int32, (CHUNK, CHUNK), 1)
    return r >= c


def _two_d(ref):
    k, r, d = ref.shape
    return ref[...].reshape(k * r, d)


class _Layout:
    def __init__(self, d, ffn_rows):
        self.ffn_rows = ffn_rows
        self.gate, self.up, self.down = [0] * DEPTH, [self.ffn_rows] * DEPTH, [2 * self.ffn_rows] * DEPTH
        self.f_total = 3 * self.ffn_rows
        self.a_in_rows, self.a_out_rows, self.b_rows = 4 * d // N_DEV, 2 * d // N_DEV, d // N_DEV
        self.a_in, self.a_out = [0, 0], [self.a_in_rows] * 2
        self.a_total = self.a_in_rows + self.a_out_rows
        self.b_in, self.b_out = [0, 0], [self.b_rows] * 2
        self.b_total = 2 * self.b_rows


def _wspec(rows, off, d):
    assert off % rows == 0
    return _resident((N_DEV, rows, d), (0, off // rows, 0))


def _a_fwd(x, gpre, wg, lay, j, lng, lnb, ws, bst, gpost, tm, name):
    t, d = x.shape
    aw = 2 * d
    nch = tm // CHUNK

    def body(x_ref, gpre_ref, win_ref, lng_ref, lnb_ref, ws_ref, bst_ref, wout_ref, gpost_ref,
             x1_ref, h1_ref, gp_ref, u_ref, vh_ref, rs_ref, gated_ref, m_ref):
        xv = x_ref[...]
        h1 = _rms_fwd(xv, gpre_ref[...]).astype(BF16)
        h1_ref[...] = h1
        z = _nt(h1, _two_d(win_ref))
        u, du_dz = _gelu(z[:, :aw])
        v, dv_dz = _gelu(z[:, aw:])
        gp_ref[:, :aw] = du_dz.astype(BF16)
        gp_ref[:, aw:] = dv_dz.astype(BF16)
        u_ref[...] = u.astype(BF16)
        vh, rs = _layernorm_stats(v)
        vh_ref[...] = vh.astype(BF16)
        rs_ref[...] = jnp.broadcast_to(rs, rs_ref.shape)
        vn = (vh * lng_ref[...] + lnb_ref[...]).astype(BF16)
        mask = _tril_mask()
        for g in range(A_GROUPS):
            wm = jnp.where(mask, ws_ref[g], 0.0).astype(BF16)
            cols = slice(g * A_GROUP_DIM, (g + 1) * A_GROUP_DIM)
            for c in range(nch):
                rows = slice(c * CHUNK, (c + 1) * CHUNK)
                sv = _nn(wm, vn[rows, cols]) + bst_ref[:, g:g + 1]
                gated_ref[rows, cols] = (u[rows, cols] * sv).astype(BF16)
        m = _nn(gated_ref[...], _two_d(wout_ref))
        m_ref[...] = m
        x1_ref[...] = xv + _rms_fwd(m, gpost_ref[...])

    vec = lambda w: _resident((1, w), (0, 0))
    return _call(
        body, name=name, grid=(t // tm,),
        in_specs=[_rows(tm, d), vec(d), _wspec(lay.a_in_rows, lay.a_in[j], d), vec(aw), vec(aw),
                  _resident((A_GROUPS, CHUNK, CHUNK), (0, 0, 0)), _resident((CHUNK, A_GROUPS), (0, 0)),
                  _wspec(lay.a_out_rows, lay.a_out[j], d), vec(d)],
        out_specs=[_rows(tm, d), _rows(tm, d), _rows(tm, 2 * aw), _rows(tm, aw), _rows(tm, aw), _rows(tm, 128),
                   _rows(tm, aw), _rows(tm, d)],
        out_shape=[jax.ShapeDtypeStruct((t, d), F32), jax.ShapeDtypeStruct((t, d), BF16),
                   jax.ShapeDtypeStruct((t, 2 * aw), BF16), jax.ShapeDtypeStruct((t, aw), BF16),
                   jax.ShapeDtypeStruct((t, aw), BF16), jax.ShapeDtypeStruct((t, 128), F32),
                   jax.ShapeDtypeStruct((t, aw), BF16), jax.ShapeDtypeStruct((t, d), F32)],
        compiler_params=_params("parallel"),
    )(x, gpre, wg, lng, lnb, ws, bst, wg, gpost)


def _a_bwd(dx1, m, x, gp, u, vh, rs, gpre, wg, lay, j, lng, lnb, ws, bst, gpost, after, tm, name):
    t, d = x.shape
    aw = 2 * d
    nch = tm // CHUNK

    def body(dx1_ref, m_ref, x_ref, gp_ref, u_ref, vh_ref, rs_ref, gpre_ref, win_ref, lng_ref, lnb_ref, ws_ref, bst_ref,
             wout_ref, gpost_ref, after_ref,
             dx_ref, dm_ref, dz_ref, dgpost_ref, dgpre_ref, dlng_ref, dlnb_ref, dws_ref, dbt_ref, dvn_ref):
        @pl.when(pl.program_id(0) == 0)
        def _():
            for r in (dgpost_ref, dgpre_ref, dlng_ref, dlnb_ref, dws_ref, dbt_ref):
                r[...] = jnp.zeros_like(r)

        dy = dx1_ref[...]
        dm, dgpost = _rms_bwd(m_ref[...], gpost_ref[...], dy)
        dgpost_ref[...] += dgpost
        dm_bf = dm.astype(BF16)
        dm_ref[...] = dm_bf
        dgated = _nt(dm_bf, _two_d(wout_ref))

        vh = vh_ref[...].astype(F32)
        rs = rs_ref[:, :1]
        lng_v = lng_ref[...]
        vn = (vh * lng_v + lnb_ref[...]).astype(BF16)
        mask = _tril_mask()
        lane = lax.broadcasted_iota(jnp.int32, (CHUNK, CHUNK), 1)
        for g in range(A_GROUPS):
            wm = jnp.where(mask, ws_ref[g], 0.0).astype(BF16)
            cols = slice(g * A_GROUP_DIM, (g + 1) * A_GROUP_DIM)
            dws_g = jnp.zeros((CHUNK, CHUNK), F32)
            db_g = jnp.zeros((CHUNK, 1), F32)
            for c in range(nch):
                rows = slice(c * CHUNK, (c + 1) * CHUNK)
                vn_cg = vn[rows, cols]
                sv = _nn(wm, vn_cg) + bst_ref[:, g:g + 1]
                dg_cg = dgated[rows, cols]
                dsv = dg_cg * u_ref[rows, cols].astype(F32)
                dsv_bf = dsv.astype(BF16)
                db_g = db_g + jnp.sum(dsv, axis=1, keepdims=True)
                dws_g = dws_g + _nt(dsv_bf, vn_cg)
                dvn_ref[rows, cols] = _tn(wm, dsv_bf)
                dz_ref[rows, cols] = (dg_cg * sv * gp_ref[rows, cols].astype(F32)).astype(BF16)
            dws_ref[g] += jnp.where(mask, dws_g, 0.0)
            dbt_ref[...] += jnp.where(lane == g, db_g, 0.0)
        dvn = dvn_ref[...]
        dlng_ref[...] += jnp.sum(dvn * vh, axis=0, keepdims=True)
        dlnb_ref[...] += jnp.sum(dvn, axis=0, keepdims=True)
        dvh = dvn * lng_v
        dv = rs * (dvh - jnp.mean(dvh, axis=-1, keepdims=True) - vh * jnp.mean(dvh * vh, axis=-1, keepdims=True))
        dz_ref[:, aw:] = (dv * gp_ref[:, aw:].astype(F32)).astype(BF16)
        dh1 = _nn(dz_ref[...], _two_d(win_ref))
        dxp, dgpre = _rms_bwd(x_ref[...], gpre_ref[...], dh1)
        dgpre_ref[...] += dgpre
        dx_ref[...] = dy + dxp

    vec = lambda w: _resident((1, w), (0, 0))
    acc = lambda shape: pl.BlockSpec(shape, lambda i: (0,) * len(shape))
    return _call(
        body, name=name, grid=(t // tm,),
        in_specs=[_rows(tm, d), _rows(tm, d), _rows(tm, d), _rows(tm, 2 * aw), _rows(tm, aw), _rows(tm, aw),
                  _rows(tm, 128), vec(d), _wspec(lay.a_in_rows, lay.a_in[j], d), vec(aw), vec(aw),
                  _resident((A_GROUPS, CHUNK, CHUNK), (0, 0, 0)), _resident((CHUNK, A_GROUPS), (0, 0)),
                  _wspec(lay.a_out_rows, lay.a_out[j], d), vec(d), ANY],
        out_specs=[_rows(tm, d), _rows(tm, d), _rows(tm, 2 * aw), acc((1, d)), acc((1, d)), acc((1, aw)), acc((1, aw)),
                   acc((A_GROUPS, CHUNK, CHUNK)), acc((CHUNK, CHUNK))],
        out_shape=[jax.ShapeDtypeStruct((t, d), F32), jax.ShapeDtypeStruct((t, d), BF16),
                   jax.ShapeDtypeStruct((t, 2 * aw), BF16), jax.ShapeDtypeStruct((1, d), F32),
                   jax.ShapeDtypeStruct((1, d), F32), jax.ShapeDtypeStruct((1, aw), F32),
                   jax.ShapeDtypeStruct((1, aw), F32), jax.ShapeDtypeStruct((A_GROUPS, CHUNK, CHUNK), F32),
                   jax.ShapeDtypeStruct((CHUNK, CHUNK), F32)],
        scratch_shapes=[pltpu.VMEM((tm, aw), F32)],
        compiler_params=_params("arbitrary"),
    )(dx1, m, x, gp, u, vh, rs, gpre, wg, lng, lnb, ws, bst, wg, gpost, after)


def _window_counts(first_row, n, win):
    tpos = first_row + lax.broadcasted_iota(jnp.int32, (n, 1), 0)
    return jnp.clip(tpos + 1, 1, win).astype(F32)


def _b_fwd(x, gpre, wg, lay, j, wgrp, scale, gpost, tm, name):
    t, d = x.shape
    n = tm + HALO
    ngrp = len(B_WINDOWS)

    def body(x_ref, xprev_ref, gpre_ref, win_ref, wgrp_ref, scale_ref, wout_ref, gpost_ref,
             x1_ref, h1_ref, pooled_ref, mixed_ref, m_ref):
        i = pl.program_id(0)
        xv = x_ref[...]
        keep = jnp.where(i > 0, 1.0, 0.0)
        xe = jnp.concatenate([xprev_ref[...] * keep, xv], axis=0)
        h1e = _rms_fwd(xe, gpre_ref[...]).astype(BF16)
        h1_ref[...] = h1e[HALO:]
        p = _nn(h1e, _two_d(win_ref))
        acc = p
        shift = 1
        for g, win in enumerate(B_WINDOWS):
            lo = g * B_GROUP_DIM
            if g > 0:
                acc = acc[:, B_GROUP_DIM:]
            while shift < win:
                acc = acc + pltpu.roll(acc, shift, 0)
                shift *= 2
            cnt = _window_counts(i * tm - HALO, n, win)
            pooled = acc[:, :B_GROUP_DIM] / cnt - p[:, lo:lo + B_GROUP_DIM]
            pooled_ref[:, lo:lo + B_GROUP_DIM] = pooled[HALO:].astype(BF16)
        for g in range(ngrp):
            cols = slice(g * B_GROUP_DIM, (g + 1) * B_GROUP_DIM)
            raw = _nn(pooled_ref[:, cols], wgrp_ref[g])
            mixed_ref[:, cols] = (raw * scale_ref[:, cols]).astype(BF16)
        m = _nn(mixed_ref[...], _two_d(wout_ref))
        m_ref[...] = m
        x1_ref[...] = xv + _rms_fwd(m, gpost_ref[...])

    vec = lambda w: _resident((1, w), (0, 0))
    per = tm // HALO
    return _call(
        body, name=name, grid=(t // tm,),
        in_specs=[_rows(tm, d), pl.BlockSpec((HALO, d), lambda i: (jnp.maximum(i * per - 1, 0), 0)), vec(d),
                  _wspec(lay.b_rows, lay.b_in[j], d), _resident((ngrp, B_GROUP_DIM, B_GROUP_DIM), (0, 0, 0)), vec(d),
                  _wspec(lay.b_rows, lay.b_out[j], d), vec(d)],
        out_specs=[_rows(tm, d)] * 5,
        out_shape=[jax.ShapeDtypeStruct((t, d), F32), jax.ShapeDtypeStruct((t, d), BF16),
                   jax.ShapeDtypeStruct((t, d), BF16), jax.ShapeDtypeStruct((t, d), BF16),
                   jax.ShapeDtypeStruct((t, d), F32)],
        compiler_params=_params("parallel"),
    )(x, x, gpre, wg, wgrp, scale, wg, gpost)


def _b_bwd(dx1, m, x, pooled, gpre, wg, lay, j, wgrp, scale, gpost, after, tm, name):
    t, d = x.shape
    n = tm + HALO
    ngrp = len(B_WINDOWS)
    steps = t // tm

    def body(dx1_ref, dx1n_ref, m_ref, mn_ref, x_ref, pooled_ref, pooledn_ref, gpre_ref, win_ref, wgrp_ref, scale_ref,
             wout_ref, gpost_ref, after_ref,
             dx_ref, dm_ref, draw_ref, dp_ref, dgpost_ref, dgpre_ref, dscale_ref, dpool_ref):
        i = pl.program_id(0)

        @pl.when(i == 0)
        def _():
            for r in (dgpost_ref, dgpre_ref, dscale_ref):
                r[...] = jnp.zeros_like(r)

        keep = jnp.where(i < steps - 1, 1.0, 0.0)
        dy = dx1_ref[...]
        dye = jnp.concatenate([dy, dx1n_ref[...] * keep], axis=0)
        me = jnp.concatenate([m_ref[...], mn_ref[...]], axis=0)
        gpost_v = gpost_ref[...]
        r = lax.rsqrt(jnp.mean(me * me, axis=-1, keepdims=True) + EPS)
        mh = me * r
        dgpost_ref[...] += jnp.sum((dye * mh)[:tm], axis=0, keepdims=True)
        dmh = dye * gpost_v
        dme = (r * (dmh - mh * jnp.mean(dmh * mh, axis=-1, keepdims=True))).astype(BF16)
        dm_ref[...] = dme[:tm]
        dmixed = _nt(dme, _two_d(wout_ref))
        pooled_e = jnp.concatenate([pooled_ref[...], pooledn_ref[...]], axis=0)
        scale_v = scale_ref[...]
        for g, win in enumerate(B_WINDOWS):
            cols = slice(g * B_GROUP_DIM, (g + 1) * B_GROUP_DIM)
            raw = _nn(pooled_e[:, cols], wgrp_ref[g])
            dscale_ref[:, cols] += jnp.sum((dmixed[:, cols] * raw)[:tm], axis=0, keepdims=True)
            draw = (dmixed[:, cols] * scale_v[:, cols]).astype(BF16)
            draw_ref[:, cols] = draw[:tm]
            dpool = _nt(draw, wgrp_ref[g])
            acc = dpool / _window_counts(i * tm, n, win)
            shift = 1
            while shift < win:
                acc = acc + pltpu.roll(acc, n - shift, 0)
                shift *= 2
            dpool_ref[:, cols] = (acc - dpool)[:tm]
        dp = dpool_ref[...].astype(BF16)
        dp_ref[...] = dp
        dh1 = _nt(dp, _two_d(win_ref))
        dxp, dgpre = _rms_bwd(x_ref[...], gpre_ref[...], dh1)
        dgpre_ref[...] += dgpre
        dx_ref[...] = dy + dxp

    vec = lambda w: _resident((1, w), (0, 0))
    acc = lambda shape: pl.BlockSpec(shape, lambda i: (0,) * len(shape))
    per = tm // HALO
    nxt = lambda i: (jnp.minimum((i + 1) * per, t // HALO - 1), 0)
    return _call(
        body, name=name, grid=(steps,),
        in_specs=[_rows(tm, d), pl.BlockSpec((HALO, d), nxt), _rows(tm, d), pl.BlockSpec((HALO, d), nxt), _rows(tm, d),
                  _rows(tm, d), pl.BlockSpec((HALO, d), nxt), vec(d), _wspec(lay.b_rows, lay.b_in[j], d),
                  _resident((ngrp, B_GROUP_DIM, B_GROUP_DIM), (0, 0, 0)), vec(d), _wspec(lay.b_rows, lay.b_out[j], d),
                  vec(d), ANY],
        out_specs=[_rows(tm, d)] * 4 + [acc((1, d))] * 3,
        out_shape=[jax.ShapeDtypeStruct((t, d), F32), jax.ShapeDtypeStruct((t, d), BF16),
                   jax.ShapeDtypeStruct((t, d), BF16), jax.ShapeDtypeStruct((t, d), BF16)]
                  + [jax.ShapeDtypeStruct((1, d), F32)] * 3,
        scratch_shapes=[pltpu.VMEM((tm, d), F32)],
        compiler_params=_params("arbitrary"),
    )(dx1, dx1, m, m, x, pooled, pooled, gpre, wg, wgrp, scale, wg, gpost, after)


def _f_fwd(x1, gpre, wg, lay, l, gpost, tm, name, target=None):
    t, d = x1.shape
    hid = N_DEV * lay.ffn_rows
    head = target is not None

    def body(x_ref, gpre_ref, wgate_ref, wup_ref, wdown_ref, gpost_ref, *rest):
        x2_ref, h2_ref, a_ref, b_ref, s_ref, f_ref = rest[-7:-1] if head else rest
        xv = x_ref[...]
        h2 = _rms_fwd(xv, gpre_ref[...]).astype(BF16)
        h2_ref[...] = h2
        a = _nt(h2, _two_d(wgate_ref))
        b = _nt(h2, _two_d(wup_ref))
        sig = jax.nn.sigmoid(a)
        silu = a * sig
        a_ref[...] = (b * (sig + silu * (1.0 - sig))).astype(BF16)
        b_ref[...] = silu.astype(BF16)
        s = (silu * b).astype(BF16)
        s_ref[...] = s
        f = _nn(s, _two_d(wdown_ref))
        f_ref[...] = f
        x2 = xv + _rms_fwd(f, gpost_ref[...])
        if head:
            target_ref, loss_ref = rest[0], rest[-1]

            @pl.when(pl.program_id(0) == 0)
            def _():
                loss_ref[...] = jnp.zeros_like(loss_ref)

            diff = x2 - target_ref[...]
            x2_ref[...] = diff * (1.0 / d)
            sq = jnp.sum(jnp.sum(diff * diff, axis=0, keepdims=True), axis=1, keepdims=True)
            loss_ref[...] += sq * (0.5 / d)
        else:
            x2_ref[...] = x2

    vec = lambda w: _resident((1, w), (0, 0))
    return _call(
        body, name=name, grid=(t // tm,),
        in_specs=[_rows(tm, d), vec(d), _wspec(lay.ffn_rows, lay.gate[l], d), _wspec(lay.ffn_rows, lay.up[l], d),
                  _wspec(lay.ffn_rows, lay.down[l], d), vec(d)] + ([_rows(tm, d)] if head else []),
        out_specs=[_rows(tm, d), _rows(tm, d), _rows(tm, hid), _rows(tm, hid), _rows(tm, hid), _rows(tm, d)]
                  + ([pl.BlockSpec((8, 128), lambda i: (0, 0))] if head else []),
        out_shape=[jax.ShapeDtypeStruct((t, d), F32), jax.ShapeDtypeStruct((t, d), BF16),
                   jax.ShapeDtypeStruct((t, hid), BF16), jax.ShapeDtypeStruct((t, hid), BF16),
                   jax.ShapeDtypeStruct((t, hid), BF16), jax.ShapeDtypeStruct((t, d), F32)]
                  + ([jax.ShapeDtypeStruct((8, 128), F32)] if head else []),
        compiler_params=_params("arbitrary" if head else "parallel"),
    )(x1, gpre, wg, wg, wg, gpost, *([target] if head else []))


def _f_bwd(dx2, f, x1, a, b, gpre, wg, lay, l, gpost, after, tm, name):
    t, d = x1.shape
    hid = N_DEV * lay.ffn_rows

    def body(dx2_ref, f_ref, x_ref, a_ref, b_ref, gpre_ref, wgate_ref, wup_ref, wdown_ref, gpost_ref, after_ref,
             dx1_ref, df_ref, da_ref, db_ref, dgpost_ref, dgpre_ref):
        @pl.when(pl.program_id(0) == 0)
        def _():
            dgpost_ref[...] = jnp.zeros_like(dgpost_ref)
            dgpre_ref[...] = jnp.zeros_like(dgpre_ref)

        dy = dx2_ref[...]
        df, dgpost = _rms_bwd(f_ref[...], gpost_ref[...], dy)
        dgpost_ref[...] += dgpost
        df_bf = df.astype(BF16)
        df_ref[...] = df_bf
        ds = _nt(df_bf, _two_d(wdown_ref))
        da = (ds * a_ref[...].astype(F32)).astype(BF16)
        db = (ds * b_ref[...].astype(F32)).astype(BF16)
        da_ref[...] = da
        db_ref[...] = db
        dh2 = _nn(da, _two_d(wgate_ref)) + _nn(db, _two_d(wup_ref))
        dxp, dgpre = _rms_bwd(x_ref[...], gpre_ref[...], dh2)
        dgpre_ref[...] += dgpre
        dx1_ref[...] = dy + dxp

    vec = lambda w: _resident((1, w), (0, 0))
    acc = pl.BlockSpec((1, d), lambda i: (0, 0))
    return _call(
        body, name=name, grid=(t // tm,),
        in_specs=[_rows(tm, d), _rows(tm, d), _rows(tm, d), _rows(tm, hid), _rows(tm, hid), vec(d),
                  _wspec(lay.ffn_rows, lay.gate[l], d), _wspec(lay.ffn_rows, lay.up[l], d),
                  _wspec(lay.ffn_rows, lay.down[l], d), vec(d), ANY],
        out_specs=[_rows(tm, d), _rows(tm, d), _rows(tm, hid), _rows(tm, hid), acc, acc],
        out_shape=[jax.ShapeDtypeStruct((t, d), F32), jax.ShapeDtypeStruct((t, d), BF16),
                   jax.ShapeDtypeStruct((t, hid), BF16), jax.ShapeDtypeStruct((t, hid), BF16),
                   jax.ShapeDtypeStruct((1, d), F32), jax.ShapeDtypeStruct((1, d), F32)],
        compiler_params=_params("arbitrary"),
    )(dx2, f, x1, a, b, gpre, wg, wg, wg, gpost, after)


def _grad_into(gbuf, lhs, rhs, off, rows, name, after=None):
    t, m = lhs.shape
    d = rhs.shape[1]
    assert m == N_DEV * rows and off % rows == 0
    per_tile = {352: 4, 512: 2, 256: 4, 128: 8}[rows]
    tm = per_tile * rows
    assert tm % 128 == 0 and rows % 16 == 0
    tk = 2048 if t % 2048 == 0 else 256
    ksteps = t // tk
    fresh = isinstance(gbuf, int)
    shape = (N_DEV, gbuf, d) if fresh else gbuf.shape
    extra = ([] if fresh else [gbuf]) + ([] if after is None else [after])

    def body(l_ref, r_ref, *rest):
        o_ref, acc_ref = rest[-2:]
        k = pl.program_id(1)

        @pl.when(k == 0)
        def _():
            acc_ref[...] = jnp.zeros_like(acc_ref)

        acc_ref[...] += _tn(l_ref[...], r_ref[...])

        @pl.when(k == ksteps - 1)
        def _():
            o_ref[...] = acc_ref[...].reshape(per_tile, rows, d).astype(BF16)

    return _call(
        body, name=name, grid=(N_DEV // per_tile, ksteps),
        in_specs=[pl.BlockSpec((tk, tm), lambda i, k: (k, i)), pl.BlockSpec((tk, d), lambda i, k: (k, 0))]
                 + [ANY] * len(extra),
        out_specs=pl.BlockSpec((per_tile, rows, d), lambda i, k: (i, off // rows, 0)),
        out_shape=jax.ShapeDtypeStruct(shape, BF16),
        scratch_shapes=[pltpu.VMEM((tm, d), F32)],
        input_output_aliases={} if fresh else {2: 0},
        compiler_params=_params("parallel", "arbitrary"),
    )(lhs, rhs, *extra)


def _grad_grouped(pooled, draw, name):
    t, d = pooled.shape
    ngrp = len(B_WINDOWS)
    tk = 1024 if t % 1024 == 0 else 256

    def body(p_ref, q_ref, o_ref):
        @pl.when(pl.program_id(0) == 0)
        def _():
            o_ref[...] = jnp.zeros_like(o_ref)

        for g in range(ngrp):
            cols = slice(g * B_GROUP_DIM, (g + 1) * B_GROUP_DIM)
            o_ref[g] += _tn(p_ref[:, cols], q_ref[:, cols])

    return _call(
        body, name=name, grid=(t // tk,),
        in_specs=[_rows(tk, d), _rows(tk, d)],
        out_specs=pl.BlockSpec((ngrp, B_GROUP_DIM, B_GROUP_DIM), lambda i: (0, 0, 0)),
        out_shape=jax.ShapeDtypeStruct((ngrp, B_GROUP_DIM, B_GROUP_DIM), F32),
        compiler_params=_params("arbitrary"),
    )(pooled, draw)


def _peers():
    x, y, c = lax.axis_index("x"), lax.axis_index("y"), lax.axis_index("c")
    flip = lambda v, f: 1 - v if f else v
    peers = []
    for r in range(1, N_DEV):
        px, py, pc = flip(x, r & 4), flip(y, r & 2), flip(c, r & 1)
        peers.append(((px, py, pc), 4 * px + 2 * py + pc))
    return 4 * x + 2 * y + c, peers


HBM = pl.BlockSpec(memory_space=pltpu.HBM)
SEM = pl.BlockSpec(memory_space=pltpu.SEMAPHORE)
EFFECT = pltpu.SideEffectType.DATAFLOW_SIDE_EFFECTING


def _peer_copies(scatter, srcs, lands, send_sems, recv_sems):
    me, peers = _peers()
    copies = []
    for a in range(len(srcs)):
        for r, (peer, pidx) in enumerate(peers):
            src = srcs[a].at[pidx] if scatter else srcs[a]
            mine = lands[a].at[r] if scatter else lands[a].at[pidx]
            theirs = lands[a].at[r] if scatter else lands[a].at[me]
            send = pltpu.make_async_remote_copy(src_ref=src, dst_ref=theirs, send_sem=send_sems[a].at[r],
                                                recv_sem=recv_sems[a].at[r], device_id=peer, device_id_type=MESH)
            recv = pltpu.make_async_remote_copy(src_ref=src, dst_ref=mine, send_sem=send_sems[a].at[r],
                                                recv_sem=recv_sems[a].at[r], device_id=peer, device_id_type=MESH)
            copies.append((send, recv))
    return copies


def _exchange_start(scatter, srcs, lands, after, name):
    n = len(srcs)

    def body(*refs):
        src_refs, land_refs = refs[:n], refs[n:2 * n]
        outs = refs[2 * n + 1:]
        send_sems, recv_sems, token = outs[:n], outs[n:2 * n], outs[-1]
        for send, _ in _peer_copies(scatter, src_refs, land_refs, send_sems, recv_sems):
            send.start()
        token[...] = jnp.zeros_like(token)

    hbm = lambda a: pltpu.with_memory_space_constraint(a, pltpu.HBM)
    res = _call(
        body, name=name,
        in_specs=[HBM] * (2 * n) + [ANY],
        out_specs=[SEM] * (2 * n) + [HBM] * (2 * n) + [pl.BlockSpec(memory_space=pltpu.VMEM)],
        out_shape=[pltpu.SemaphoreType.DMA((N_DEV - 1,))] * (2 * n)
                  + [pltpu.HBM(a.shape, a.dtype) for a in list(srcs) + list(lands)]
                  + [jax.ShapeDtypeStruct((8, 128), F32)],
        input_output_aliases={i: 2 * n + i for i in range(2 * n)},
        compiler_params=pltpu.CompilerParams(has_side_effects=EFFECT),
    )(*[hbm(a) for a in srcs], *[hbm(a) for a in lands], after)
    return res[:n], res[n:2 * n], res[2 * n:3 * n], res[3 * n:4 * n], res[-1]


def _exchange_wait(scatter, send_sems, recv_sems, srcs, lands, after, name):
    n = len(srcs)
    after = list(after) if isinstance(after, (list, tuple)) else [after]

    def body(*refs):
        src_refs, land_refs = refs[:n], refs[n:2 * n]
        send_refs, recv_refs = refs[2 * n:3 * n], refs[3 * n:4 * n]
        for send, recv in _peer_copies(scatter, src_refs, land_refs, send_refs, recv_refs):
            send.wait_send()
            recv.wait_recv()

    res = _call(
        body, name=name,
        in_specs=[HBM] * (2 * n) + [SEM] * (2 * n) + [ANY] * len(after),
        out_specs=[HBM] * (2 * n),
        out_shape=[pltpu.HBM(a.shape, a.dtype) for a in list(srcs) + list(lands)],
        input_output_aliases={i: i for i in range(2 * n)},
        compiler_params=pltpu.CompilerParams(has_side_effects=EFFECT),
    )(*srcs, *lands, *send_sems, *recv_sems, *after)
    return res[:n], res[n:]


def _row_tile(rows):
    if rows <= 512:
        return rows
    return max([tr for tr in range(16, 513, 16) if rows % tr == 0] or [rows])


def _sum_parts(own, got, me, name):
    _, rows, w = own.shape
    tr = _row_tile(rows)

    def body(me_ref, a_ref, b_ref, o_ref):
        s = a_ref[...].astype(F32)
        for j in range(N_DEV - 1):
            s = s + b_ref[j].astype(F32)
        o_ref[...] = s

    return _call(
        body, name=name,
        grid_spec=pltpu.PrefetchScalarGridSpec(
            num_scalar_prefetch=1, grid=(rows // tr,),
            in_specs=[pl.BlockSpec((None, tr, w), lambda i, me_ref: (me_ref[0], i, 0)),
                      pl.BlockSpec((N_DEV - 1, tr, w), lambda i, me_ref: (0, i, 0))],
            out_specs=pl.BlockSpec((tr, w), lambda i, me_ref: (i, 0))),
        out_shape=jax.ShapeDtypeStruct((rows, w), F32),
        compiler_params=_params("parallel"),
    )(me, own, got)


def _sum_devices(stacked, name):
    k, rows, w = stacked.shape
    tr = _row_tile(rows)

    def body(a_ref, o_ref):
        s = a_ref[0]
        for j in range(1, k):
            s = s + a_ref[j]
        o_ref[...] = s

    return _call(
        body, name=name, grid=(rows // tr,),
        in_specs=[pl.BlockSpec((k, tr, w), lambda i: (0, i, 0))],
        out_specs=pl.BlockSpec((tr, w), lambda i: (i, 0)),
        out_shape=jax.ShapeDtypeStruct((rows, w), F32),
        compiler_params=_params("parallel"),
    )(stacked)


def _adamw(w, g, m, v, name):
    rows, cols = w.shape
    tr = _row_tile(rows)

    def body(w_ref, g_ref, m_ref, v_ref, d_ref, nm_ref, nv_ref):
        gv = g_ref[...]
        nm = ADAM_B1 * m_ref[...] + (1.0 - ADAM_B1) * gv
        nv = ADAM_B2 * v_ref[...] + (1.0 - ADAM_B2) * (gv * gv)
        m_hat = nm / (1.0 - ADAM_B1 ** ADAM_STEP)
        v_hat = nv / (1.0 - ADAM_B2 ** ADAM_STEP)
        d_ref[...] = -ADAM_LR * (m_hat / (jnp.sqrt(v_hat) + ADAM_EPS) + ADAM_WD * w_ref[...])
        nm_ref[...] = nm
        nv_ref[...] = nv

    spec = pl.BlockSpec((tr, cols), lambda i: (i, 0))
    return _call(
        body, name=name, grid=(rows // tr,),
        in_specs=[spec] * 4, out_specs=[spec] * 3,
        out_shape=[jax.ShapeDtypeStruct((rows, cols), F32)] * 3,
        compiler_params=_params("parallel"),
    )(w, g, m, v)


SMALL = ("a_ln_g", "a_ln_b", "a_w_s", "a_b_s", "mix_pre_g", "mix_post_g", "ffn_pre_g", "ffn_post_g")


def _pack_small(parts, d, last_row=None):
    rows = [parts[k].reshape(-1, d) for k in SMALL] + ([] if last_row is None else [last_row])
    flat = jnp.concatenate(rows, axis=0)
    return jnp.pad(flat, ((0, -flat.shape[0] % 8), (0, 0)))


def _unpack_small(flat, like):
    out, r = {}, 0
    for k in SMALL:
        n = like[k].size // flat.shape[1]
        out[k] = flat[r:r + n].reshape(like[k].shape)
        r += n
    return out


def kernel(x, a_w_in, a_ln_g, a_ln_b, a_w_s, a_b_s, a_w_out, b_w_in, b_w_grp, b_scale, b_w_out, mix_pre_g, mix_post_g, ffn_pre_g, ffn_post_g, ffn_w_gate, ffn_w_up, ffn_w_down, loss_target, m_a_w_in, m_a_ln_g, m_a_ln_b, m_a_w_s, m_a_b_s, m_a_w_out, m_b_w_in, m_b_w_grp, m_b_scale, m_b_w_out, m_mix_pre_g, m_mix_post_g, m_ffn_pre_g, m_ffn_post_g, m_ffn_w_gate, m_ffn_w_up, m_ffn_w_down, v_a_w_in, v_a_ln_g, v_a_ln_b, v_a_w_s, v_a_b_s, v_a_w_out, v_b_w_in, v_b_w_grp, v_b_scale, v_b_w_out, v_mix_pre_g, v_mix_post_g, v_ffn_pre_g, v_ffn_post_g, v_ffn_w_gate, v_ffn_w_up, v_ffn_w_down):
    args = dict(locals())
    names = ("a_w_in", "a_ln_g", "a_ln_b", "a_w_s", "a_b_s", "a_w_out", "b_w_in", "b_w_grp", "b_scale", "b_w_out",
             "mix_pre_g", "mix_post_g", "ffn_pre_g", "ffn_post_g", "ffn_w_gate", "ffn_w_up", "ffn_w_down")
    w = {k: args[k] for k in names}
    mom = {k: args["m_" + k] for k in names}
    var = {k: args["v_" + k] for k in names}

    t, d = x.shape[1], x.shape[2]
    ffn_local = ffn_w_gate.shape[2]
    lay = _Layout(d, ffn_local)
    me = 4 * lax.axis_index("x") + 2 * lax.axis_index("y") + lax.axis_index("c")
    me1 = jnp.reshape(me, (1,)).astype(jnp.int32)

    def landing(block):
        zone = lax.empty((N_DEV,) + block.shape, block.dtype)
        return lax.dynamic_update_slice(zone, block[None], (me,) + (0,) * block.ndim)

    def pack(i, mixer, zero):
        j = i // 2
        if not mixer:
            parts = [ffn_w_gate[i].T, ffn_w_up[i].T, ffn_w_down[i]]
        elif i % 2 == 0:
            parts = [a_w_in[j].T, a_w_out[j]]
        else:
            parts = [b_w_in[j], b_w_out[j]]
        return (jnp.concatenate(parts, axis=0) + zero).astype(BF16)

    nsub = 2 * DEPTH
    wg = [None] * nsub
    packed0 = pack(0, True, 0.0)
    first = _exchange_start(False, [packed0], [landing(packed0)], jnp.zeros((8, 128), F32), "gather_first_start")
    zero = first[4][0, 0]
    packed = [packed0] + [pack(k // 2, k % 2 == 0, zero) for k in range(1, nsub)]
    ngrp = len(B_WINDOWS)
    grp_local = b_w_grp.shape[2]
    sdev = b_scale.shape[1]
    side_rows = 2 * ngrp * grp_local
    side = jnp.concatenate(
        [b_w_grp.reshape(side_rows, B_GROUP_DIM),
         jnp.pad(b_scale, ((0, 6), (0, B_GROUP_DIM - sdev)))], axis=0) + zero
    later = [side] + packed[1:]
    send_sems, recv_sems, later, zones, token = _exchange_start(
        False, later, [landing(b) for b in later], first[4], "gather_start")
    _, (wg[0],) = _exchange_wait(False, *first[:4], token, "gather_first_wait")

    def gathered(k, after):
        _, (zone,) = _exchange_wait(False, [send_sems[k]], [recv_sems[k]], [later[k]], [zones[k]], after,
                                    f"gather_wait_{k}")
        return zone

    row = lambda a: a.reshape(1, -1)
    bst = jnp.transpose(a_b_s, (0, 2, 1))

    tm = 256 if t % 256 == 0 else CHUNK
    tm_abwd = tm

    saved = []
    h = x[0]
    wgrp_full = scale_full = None
    for i in range(DEPTH):
        j = i // 2
        gpre = row(mix_pre_g[i])
        if i > 0:
            wg[2 * i] = gathered(2 * i, h)
        if i % 2 == 0:
            x1, h1, gp, u, vh, rs, gated, m = _a_fwd(h, gpre, wg[2 * i], lay, j, row(a_ln_g[j]), row(a_ln_b[j]),
                                                     a_w_s[j], bst[j], row(mix_post_g[i]), tm, f"a_fwd_{j}")
            mix = dict(h1=h1, gp=gp, u=u, vh=vh, rs=rs, gated=gated, m=m)
        else:
            if wgrp_full is None:
                side_g = gathered(0, h)
                wgrp_full = (side_g[:, :side_rows].reshape(N_DEV, 2, ngrp, grp_local, B_GROUP_DIM)
                             .transpose(1, 2, 0, 3, 4).reshape(2, ngrp, B_GROUP_DIM, B_GROUP_DIM).astype(BF16))
                scale_full = (side_g[:, side_rows:side_rows + 2, :sdev].transpose(1, 0, 2)
                              .reshape(2, 1, N_DEV * sdev))
            x1, h1, pooled, mixed, m = _b_fwd(h, gpre, wg[2 * i], lay, j, wgrp_full[j], scale_full[j],
                                              row(mix_post_g[i]), tm, f"b_fwd_{j}")
            mix = dict(h1=h1, pooled=pooled, mixed=mixed, m=m)
        wg[2 * i + 1] = gathered(2 * i + 1, x1)
        x2, h2, a, b, s, f, *loss_acc = _f_fwd(x1, row(ffn_pre_g[i]), wg[2 * i + 1], lay, i, row(ffn_post_g[i]), tm,
                                               f"f_fwd_{i}", loss_target[0] if i == DEPTH - 1 else None)
        saved.append(dict(x=h, x1=x1, mix=mix, h2=h2, a=a, b=b, s=s, f=f))
        h = x2
    dy, (loss_acc,) = h, loss_acc

    small_g = {k: [None] * w[k].shape[0] for k in SMALL}
    dgrp, dscale = [None, None], [None, None]
    pending = [None] * nsub
    token = jnp.zeros((8, 128), F32)

    def scatter(k, gbuf):
        got = pltpu.with_memory_space_constraint(lax.empty((N_DEV - 1,) + gbuf.shape[1:], gbuf.dtype), pltpu.HBM)
        ss, rs, src, zone, tok = _exchange_start(True, [gbuf], [got], token, f"scatter_start_{k}")
        pending[k] = (ss, rs, src, zone)
        return tok

    def small_exchanges():
        side_grad = jnp.concatenate(
            [jnp.stack(dgrp).reshape(2, ngrp, N_DEV, grp_local, B_GROUP_DIM).transpose(2, 0, 1, 3, 4)
             .reshape(N_DEV, side_rows, B_GROUP_DIM),
             jnp.pad(jnp.stack(dscale).reshape(2, N_DEV, sdev).transpose(1, 0, 2),
                     ((0, 0), (0, 6), (0, B_GROUP_DIM - sdev)))], axis=1)
        small_part = _pack_small({k: jnp.stack(small_g[k]) for k in SMALL}, d,
                                 jnp.broadcast_to(loss_acc[:1, :1], (1, d)))
        got = pltpu.with_memory_space_constraint(lax.empty((N_DEV - 1,) + side_grad.shape[1:], F32), pltpu.HBM)
        side_x = _exchange_start(True, [side_grad], [got], token, "side_scatter_start")
        small_x = _exchange_start(False, [small_part], [landing(small_part)], side_x[4], "small_gather_start")
        return side_x[:4], small_x[:4], small_x[4]

    for i in reversed(range(DEPTH)):
        sv = saved[i]
        j = i // 2
        wf, wm = wg[2 * i + 1], wg[2 * i]
        dx1, df, da, db, dgpost, dgpre = _f_bwd(dy, sv["f"], sv["x1"], sv["a"], sv["b"], row(ffn_pre_g[i]), wf, lay, i,
                                                 row(ffn_post_g[i]), token, tm, f"f_bwd_{i}")
        small_g["ffn_post_g"][i], small_g["ffn_pre_g"][i] = dgpost[0], dgpre[0]
        gbuf = _grad_into(lay.f_total, da, sv["h2"], lay.gate[i], lay.ffn_rows, f"g_gate_{i}")
        gbuf = _grad_into(gbuf, db, sv["h2"], lay.up[i], lay.ffn_rows, f"g_up_{i}")
        gbuf = _grad_into(gbuf, sv["s"], df, lay.down[i], lay.ffn_rows, f"g_down_{i}")
        token = scatter(2 * i + 1, gbuf)
        mix = sv["mix"]
        gpost = row(mix_post_g[i])
        if i % 2 == 0:
            dx, dm, dz, dgpost, dgpre, dlng, dlnb, dws, dbt = _a_bwd(
                dx1, mix["m"], sv["x"], mix["gp"], mix["u"], mix["vh"], mix["rs"], row(mix_pre_g[i]), wm, lay, j,
                row(a_ln_g[j]), row(a_ln_b[j]), a_w_s[j], bst[j], gpost, token, tm_abwd, f"a_bwd_{j}")
            small_g["a_ln_g"][j], small_g["a_ln_b"][j] = dlng[0], dlnb[0]
            small_g["a_w_s"][j], small_g["a_b_s"][j] = dws, dbt[:, :A_GROUPS].T
            small_g["mix_post_g"][i], small_g["mix_pre_g"][i] = dgpost[0], dgpre[0]
            order = None
            if i == 0:
                side_x, small_x, order = small_exchanges()
            gbuf = _grad_into(lay.a_total, dz, mix["h1"], lay.a_in[j], lay.a_in_rows, f"g_a_in_{j}", after=order)
            gbuf = _grad_into(gbuf, mix["gated"], dm, lay.a_out[j], lay.a_out_rows, f"g_a_out_{j}")
        else:
            dx, dm, draw, dp, dgpost, dgpre, dsc = _b_bwd(
                dx1, mix["m"], sv["x"], mix["pooled"], row(mix_pre_g[i]), wm, lay, j, wgrp_full[j], scale_full[j],
                gpost, token, tm, f"b_bwd_{j}")
            dscale[j] = dsc[0]
            dgrp[j] = _grad_grouped(mix["pooled"], draw, f"g_b_grp_{j}")
            gbuf = _grad_into(lay.b_total, mix["h1"], dp, lay.b_in[j], lay.b_rows, f"g_b_in_{j}")
            gbuf = _grad_into(gbuf, mix["mixed"], dm, lay.b_out[j], lay.b_rows, f"g_b_out_{j}")
            small_g["mix_post_g"][i], small_g["mix_pre_g"][i] = dgpost[0], dgpre[0]
        token = scatter(2 * i, gbuf)
        dy = dx
    grad_x = dy[None]

    g_sub = [None] * nsub

    def arrived(k, after):
        ss, rs, src, zone = pending[k]
        (own,), (got,) = _exchange_wait(True, ss, rs, src, zone, after, f"scatter_wait_{k}")
        g_sub[k] = _sum_parts(own, got, me1, f"sum_grads_{k}")

    def rows_of(k, off, n):
        return g_sub[k][off:off + n]

    grads, delta, new_m, new_v = {}, {}, {}, {}

    def update(k):
        shape = w[k].shape
        two = lambda a: a.reshape(-1, shape[-1])
        dl, nm, nv = _adamw(two(w[k]), two(grads[k]), two(mom[k]), two(var[k]), f"adamw_{k}")
        delta[k], new_m[k], new_v[k] = dl.reshape(shape), nm.reshape(shape), nv.reshape(shape)

    for k in range(1, nsub):
        arrived(k, token)
    grads["ffn_w_gate"] = jnp.stack([rows_of(2 * l + 1, lay.gate[l], ffn_local).T for l in range(DEPTH)])
    grads["ffn_w_up"] = jnp.stack([rows_of(2 * l + 1, lay.up[l], ffn_local).T for l in range(DEPTH)])
    grads["ffn_w_down"] = jnp.stack([rows_of(2 * l + 1, lay.down[l], ffn_local) for l in range(DEPTH)])
    grads["b_w_in"] = jnp.stack([rows_of(4 * j + 2, lay.b_in[j], lay.b_rows) for j in range(2)])
    grads["b_w_out"] = jnp.stack([rows_of(4 * j + 2, lay.b_out[j], lay.b_rows) for j in range(2)])
    early = ("ffn_w_gate", "ffn_w_up", "ffn_w_down", "b_w_in", "b_w_out")
    for k in early:
        update(k)

    (side_own,), (side_got,) = _exchange_wait(True, *side_x, [delta[k] for k in early], "side_scatter_wait")
    g_side = _sum_parts(side_own, side_got, me1, "sum_side")
    grads["b_w_grp"] = g_side[:side_rows].reshape(b_w_grp.shape)
    grads["b_scale"] = g_side[side_rows:side_rows + 2, :sdev]
    update("b_w_grp")
    update("b_scale")
    _, (small_all,) = _exchange_wait(False, *small_x, [delta["b_w_grp"], delta["b_scale"]], "small_gather_wait")
    small_sum = _sum_devices(small_all, "sum_small")
    g_small = _unpack_small(small_sum, w)
    loss = small_sum[sum(w[k].size for k in SMALL) // d, 0]
    grads.update(g_small)
    dl, nm, nv = _adamw(_pack_small(w, d), _pack_small(g_small, d), _pack_small(mom, d), _pack_small(var, d),
                        "adamw_small")
    delta.update(_unpack_small(dl, w))
    new_m.update(_unpack_small(nm, w))
    new_v.update(_unpack_small(nv, w))

    arrived(0, dl)
    grads["a_w_in"] = jnp.stack([rows_of(4 * j, lay.a_in[j], lay.a_in_rows).T for j in range(2)])
    grads["a_w_out"] = jnp.stack([rows_of(4 * j, lay.a_out[j], lay.a_out_rows) for j in range(2)])
    update("a_w_in")
    update("a_w_out")

    return (loss, grad_x, *[grads[k] for k in names], *[delta[k] for k in names], *[new_m[k] for k in names],
            *[new_v[k] for k in names])
```

```python
import math

import jax
import jax.numpy as jnp
from jax import lax
from jax.experimental import pallas as pl
from jax.experimental.pallas import tpu as pltpu

F32 = jnp.float32
BF16 = jnp.bfloat16
MESH = pl.DeviceIdType.MESH
ANY = pl.BlockSpec(memory_space=pl.ANY)

N_DEV = 8
EPS = 1e-6
CHUNK = 128
A_GROUPS = 8
A_GROUP_DIM = 256
B_WINDOWS = (2, 4, 8, 16)
B_GROUP_DIM = 256
HALO = 16
DEPTH = 4

ADAM_LR = 0.001
ADAM_B1 = 0.9
ADAM_B2 = 0.999
ADAM_EPS = 1e-08
ADAM_WD = 0.01
ADAM_STEP = 10

VMEM_LIMIT_BYTES = 60 * 1024 * 1024

ERF_P = 0.3275911
ERF_A = (0.254829592, -0.284496736, 1.421413741, -1.453152027, 1.061405429)
INV_SQRT2 = 1.0 / math.sqrt(2.0)
LOG2_E = 1.0 / math.log(2.0)
INV_SQRT_2PI = 1.0 / math.sqrt(2.0 * math.pi)


def _call(body, **kw):
    return pl.pallas_call(body, **kw)


def _params(*semantics):
    return pltpu.CompilerParams(dimension_semantics=semantics or None, vmem_limit_bytes=VMEM_LIMIT_BYTES)


def _resident(shape, index):
    return pl.BlockSpec(shape, lambda *_: index, pipeline_mode=pl.Buffered(1))


def _rows(tm, width):
    return pl.BlockSpec((tm, width), lambda i: (i, 0))


def _nn(a, b):
    return jnp.dot(a, b, preferred_element_type=F32)


def _nt(a, b):
    return lax.dot_general(a, b, (((1,), (1,)), ((), ())), preferred_element_type=F32)


def _tn(a, b):
    return lax.dot_general(a, b, (((0,), (0,)), ((), ())), preferred_element_type=F32)


def _rms_fwd(x, g):
    r = lax.rsqrt(jnp.mean(x * x, axis=-1, keepdims=True) + EPS)
    return x * r * g


def _rms_bwd(x, g, dy):
    r = lax.rsqrt(jnp.mean(x * x, axis=-1, keepdims=True) + EPS)
    xh = x * r
    dg = jnp.sum(dy * xh, axis=0, keepdims=True)
    dxh = dy * g
    dx = r * (dxh - xh * jnp.mean(dxh * xh, axis=-1, keepdims=True))
    return dx, dg


SLAB = 16


def _slabs(n):
    return [slice(r, r + SLAB) for r in range(0, n, SLAB)]


def _rms_bwd_slabs(x_at, dy_at, g, n, n_sum, emit):
    acc = jnp.zeros((8, g.shape[1]), F32)
    for rows in _slabs(n):
        x = x_at(rows)
        dy = dy_at(rows)
        r = lax.rsqrt(jnp.mean(x * x, axis=-1, keepdims=True) + EPS)
        xh = x * r
        if rows.start < n_sum:
            p = dy * xh
            acc = acc + p[:8] + p[8:]
        dxh = dy * g
        emit(rows, r * (dxh - xh * jnp.mean(dxh * xh, axis=-1, keepdims=True)))
    return jnp.sum(acc, axis=0, keepdims=True)


def _gelu(z):
    phi = 0.5 + 0.5 * lax.erf(z * INV_SQRT2)
    e = jnp.exp2(z * z * (-0.5 * LOG2_E))
    return z * phi, phi + z * e * INV_SQRT_2PI


def _layernorm_stats(v):
    mu = jnp.mean(v, axis=-1, keepdims=True)
    xc = v - mu
    rs = lax.rsqrt(jnp.mean(xc * xc, axis=-1, keepdims=True) + EPS)
    return xc * rs, rs


def _tril_mask():
    r = lax.broadcasted_iota(jnp.int32, (CHUNK, CHUNK), 0)
    c = lax.broadcasted_iota(jnp.int32, (CHUNK, CHUNK), 1)
    return r >= c


class _Layout:
    def __init__(self, d, ffn_rows):
        self.ffn_rows = ffn_rows
        self.gate, self.up, self.down = [0] * DEPTH, [self.ffn_rows] * DEPTH, [2 * self.ffn_rows] * DEPTH
        self.f_total = 3 * self.ffn_rows
        self.a_in_rows, self.a_out_rows, self.b_rows = 4 * d // N_DEV, 2 * d // N_DEV, d // N_DEV
        self.a_in, self.a_out = [0, 0], [self.a_in_rows] * 2
        self.a_total = self.a_in_rows + self.a_out_rows
        self.b_in, self.b_out = [0, 0], [self.b_rows] * 2
        self.b_total = 2 * self.b_rows


def _wspec(rows, d):
    return _resident((N_DEV * rows, d), (0, 0))


def _a_fwd(x, gpre, wg, lay, j, lng, lnb, ws, bst, gpost, tm, name):
    t, d = x.shape
    aw = 2 * d
    nch = tm // CHUNK

    def body(x_ref, gpre_ref, win_ref, lng_ref, lnb_ref, ws_ref, bst_ref, wout_ref, gpost_ref,
             x1_ref, h1_ref, gp_ref, u_ref, vh_ref, rs_ref, gated_ref, m_ref):
        xv = x_ref[...]
        h1 = _rms_fwd(xv, gpre_ref[...]).astype(BF16)
        h1_ref[...] = h1
        z = _nt(h1, win_ref[...])
        u, du_dz = _gelu(z[:, :aw])
        v, dv_dz = _gelu(z[:, aw:])
        gp_ref[:, :aw] = du_dz.astype(BF16)
        gp_ref[:, aw:] = dv_dz.astype(BF16)
        u_ref[...] = u.astype(BF16)
        vh, rs = _layernorm_stats(v)
        vh_ref[...] = vh.astype(BF16)
        rs_ref[...] = jnp.broadcast_to(rs, rs_ref.shape)
        vn = (vh * lng_ref[...] + lnb_ref[...]).astype(BF16)
        mask = _tril_mask()
        for g in range(A_GROUPS):
            wm = jnp.where(mask, ws_ref[g], 0.0).astype(BF16)
            cols = slice(g * A_GROUP_DIM, (g + 1) * A_GROUP_DIM)
            for c in range(nch):
                rows = slice(c * CHUNK, (c + 1) * CHUNK)
                sv = _nn(wm, vn[rows, cols]) + bst_ref[:, g:g + 1]
                gated_ref[rows, cols] = (u[rows, cols] * sv).astype(BF16)
        m = _nn(gated_ref[...], wout_ref[...])
        m_ref[...] = m
        x1_ref[...] = xv + _rms_fwd(m, gpost_ref[...])

    vec = lambda w: _resident((1, w), (0, 0))
    return _call(
        body, name=name, grid=(t // tm,),
        in_specs=[_rows(tm, d), vec(d), _wspec(lay.a_in_rows, d), vec(aw), vec(aw),
                  _resident((A_GROUPS, CHUNK, CHUNK), (0, 0, 0)), _resident((CHUNK, A_GROUPS), (0, 0)),
                  _wspec(lay.a_out_rows, d), vec(d)],
        out_specs=[_rows(tm, d), _rows(tm, d), _rows(tm, 2 * aw), _rows(tm, aw), _rows(tm, aw), _rows(tm, 128),
                   _rows(tm, aw), _rows(tm, d)],
        out_shape=[jax.ShapeDtypeStruct((t, d), F32), jax.ShapeDtypeStruct((t, d), BF16),
                   jax.ShapeDtypeStruct((t, 2 * aw), BF16), jax.ShapeDtypeStruct((t, aw), BF16),
                   jax.ShapeDtypeStruct((t, aw), BF16), jax.ShapeDtypeStruct((t, 128), F32),
                   jax.ShapeDtypeStruct((t, aw), BF16), jax.ShapeDtypeStruct((t, d), F32)],
        compiler_params=_params("parallel"),
    )(x, gpre, wg[0], lng, lnb, ws, bst, wg[1], gpost)


def _a_bwd(dx1, m, x, gp, u, vh, rs, gpre, wg, lay, j, lng, lnb, ws, bst, gpost, after, tm, name):
    t, d = x.shape
    aw = 2 * d
    nch = tm // CHUNK

    def body(dx1_ref, m_ref, x_ref, gp_ref, u_ref, vh_ref, rs_ref, gpre_ref, win_ref, lng_ref, lnb_ref, ws_ref, bst_ref,
             wout_ref, gpost_ref, after_ref,
             dx_ref, dm_ref, dz_ref, dgpost_ref, dgpre_ref, dlng_ref, dlnb_ref, dws_ref, dbt_ref, dvn_ref):
        @pl.when(pl.program_id(0) == 0)
        def _():
            for r in (dgpost_ref, dgpre_ref, dlng_ref, dlnb_ref, dws_ref, dbt_ref):
                r[...] = jnp.zeros_like(r)

        def put_dm(rows, dx):
            dm_ref[rows, :] = dx.astype(BF16)

        dgpost_ref[...] += _rms_bwd_slabs(lambda rows: m_ref[rows, :], lambda rows: dx1_ref[rows, :], gpost_ref[...],
                                          tm, tm, put_dm)
        dgated = _nt(dm_ref[...], wout_ref[...])

        vh = vh_ref[...].astype(F32)
        rs = rs_ref[:, :1]
        lng_v = lng_ref[...]
        vn = (vh * lng_v + lnb_ref[...]).astype(BF16)
        mask = _tril_mask()
        lane = lax.broadcasted_iota(jnp.int32, (CHUNK, CHUNK), 1)
        for g in range(A_GROUPS):
            wm = jnp.where(mask, ws_ref[g], 0.0).astype(BF16)
            cols = slice(g * A_GROUP_DIM, (g + 1) * A_GROUP_DIM)
            dws_g = jnp.zeros((CHUNK, CHUNK), F32)
            db_g = jnp.zeros((CHUNK, 1), F32)
            for c in range(nch):
                rows = slice(c * CHUNK, (c + 1) * CHUNK)
                vn_cg = vn[rows, cols]
                sv = _nn(wm, vn_cg) + bst_ref[:, g:g + 1]
                dg_cg = dgated[rows, cols]
                dsv = dg_cg * u_ref[rows, cols].astype(F32)
                dsv_bf = dsv.astype(BF16)
                db_g = db_g + jnp.sum(dsv, axis=1, keepdims=True)
                dws_g = dws_g + _nt(dsv_bf, vn_cg)
                dvn_ref[rows, cols] = _tn(wm, dsv_bf)
                dz_ref[rows, cols] = (dg_cg * sv * gp_ref[rows, cols].astype(F32)).astype(BF16)
            dws_ref[g] += jnp.where(mask, dws_g, 0.0)
            dbt_ref[...] += jnp.where(lane == g, db_g, 0.0)
        dvn = dvn_ref[...]
        dlng_ref[...] += jnp.sum(dvn * vh, axis=0, keepdims=True)
        dlnb_ref[...] += jnp.sum(dvn, axis=0, keepdims=True)
        dvh = dvn * lng_v
        dv = rs * (dvh - jnp.mean(dvh, axis=-1, keepdims=True) - vh * jnp.mean(dvh * vh, axis=-1, keepdims=True))
        dz_ref[:, aw:] = (dv * gp_ref[:, aw:].astype(F32)).astype(BF16)
        dh1 = _nn(dz_ref[...], win_ref[...])

        def put_dx(rows, dx):
            dx_ref[rows, :] = dx1_ref[rows, :] + dx

        dgpre_ref[...] += _rms_bwd_slabs(lambda rows: x_ref[rows, :], lambda rows: dh1[rows, :], gpre_ref[...],
                                         tm, tm, put_dx)

    vec = lambda w: _resident((1, w), (0, 0))
    acc = lambda shape: pl.BlockSpec(shape, lambda i: (0,) * len(shape))
    return _call(
        body, name=name, grid=(t // tm,),
        in_specs=[_rows(tm, d), _rows(tm, d), _rows(tm, d), _rows(tm, 2 * aw), _rows(tm, aw), _rows(tm, aw),
                  _rows(tm, 128), vec(d), _wspec(lay.a_in_rows, d), vec(aw), vec(aw),
                  _resident((A_GROUPS, CHUNK, CHUNK), (0, 0, 0)), _resident((CHUNK, A_GROUPS), (0, 0)),
                  _wspec(lay.a_out_rows, d), vec(d), ANY],
        out_specs=[_rows(tm, d), _rows(tm, d), _rows(tm, 2 * aw), acc((1, d)), acc((1, d)), acc((1, aw)), acc((1, aw)),
                   acc((A_GROUPS, CHUNK, CHUNK)), acc((CHUNK, CHUNK))],
        out_shape=[jax.ShapeDtypeStruct((t, d), F32), jax.ShapeDtypeStruct((t, d), BF16),
                   jax.ShapeDtypeStruct((t, 2 * aw), BF16), jax.ShapeDtypeStruct((1, d), F32),
                   jax.ShapeDtypeStruct((1, d), F32), jax.ShapeDtypeStruct((1, aw), F32),
                   jax.ShapeDtypeStruct((1, aw), F32), jax.ShapeDtypeStruct((A_GROUPS, CHUNK, CHUNK), F32),
                   jax.ShapeDtypeStruct((CHUNK, CHUNK), F32)],
        scratch_shapes=[pltpu.VMEM((tm, aw), F32)],
        compiler_params=_params("arbitrary"),
    )(dx1, m, x, gp, u, vh, rs, gpre, wg[0], lng, lnb, ws, bst, wg[1], gpost, after)


def _window_counts(first_row, n, win):
    tpos = first_row + lax.broadcasted_iota(jnp.int32, (n, 1), 0)
    return jnp.clip(tpos + 1, 1, win).astype(F32)


def _b_fwd(x, gpre, wg, lay, j, wgrp, scale, gpost, tm, name):
    t, d = x.shape
    n = tm + HALO
    ngrp = len(B_WINDOWS)

    def body(x_ref, xprev_ref, gpre_ref, win_ref, wgrp_ref, scale_ref, wout_ref, gpost_ref,
             x1_ref, h1_ref, pooled_ref, mixed_ref, m_ref):
        i = pl.program_id(0)
        xv = x_ref[...]
        keep = jnp.where(i > 0, 1.0, 0.0)
        xe = jnp.concatenate([xprev_ref[...] * keep, xv], axis=0)
        h1e = _rms_fwd(xe, gpre_ref[...]).astype(BF16)
        h1_ref[...] = h1e[HALO:]
        p = _nn(h1e, win_ref[...])
        acc = p
        shift = 1
        for g, win in enumerate(B_WINDOWS):
            lo = g * B_GROUP_DIM
            if g > 0:
                acc = acc[:, B_GROUP_DIM:]
            while shift < win:
                acc = acc + pltpu.roll(acc, shift, 0)
                shift *= 2
            cnt = _window_counts(i * tm - HALO, n, win)
            pooled = acc[:, :B_GROUP_DIM] / cnt - p[:, lo:lo + B_GROUP_DIM]
            pooled_ref[:, lo:lo + B_GROUP_DIM] = pooled[HALO:].astype(BF16)
        for g in range(ngrp):
            cols = slice(g * B_GROUP_DIM, (g + 1) * B_GROUP_DIM)
            raw = _nn(pooled_ref[:, cols], wgrp_ref[g])
            mixed_ref[:, cols] = (raw * scale_ref[:, cols]).astype(BF16)
        m = _nn(mixed_ref[...], wout_ref[...])
        m_ref[...] = m
        x1_ref[...] = xv + _rms_fwd(m, gpost_ref[...])

    vec = lambda w: _resident((1, w), (0, 0))
    per = tm // HALO
    return _call(
        body, name=name, grid=(t // tm,),
        in_specs=[_rows(tm, d), pl.BlockSpec((HALO, d), lambda i: (jnp.maximum(i * per - 1, 0), 0)), vec(d),
                  _wspec(lay.b_rows, d), _resident((ngrp, B_GROUP_DIM, B_GROUP_DIM), (0, 0, 0)), vec(d),
                  _wspec(lay.b_rows, d), vec(d)],
        out_specs=[_rows(tm, d)] * 5,
        out_shape=[jax.ShapeDtypeStruct((t, d), F32), jax.ShapeDtypeStruct((t, d), BF16),
                   jax.ShapeDtypeStruct((t, d), BF16), jax.ShapeDtypeStruct((t, d), BF16),
                   jax.ShapeDtypeStruct((t, d), F32)],
        compiler_params=_params("parallel"),
    )(x, x, gpre, wg[0], wgrp, scale, wg[1], gpost)


def _b_bwd(dx1, m, x, pooled, gpre, wg, lay, j, wgrp, scale, gpost, after, tm, name):
    t, d = x.shape
    n = tm + HALO
    ngrp = len(B_WINDOWS)
    steps = t // tm

    def body(dx1_ref, dx1n_ref, m_ref, mn_ref, x_ref, pooled_ref, pooledn_ref, gpre_ref, win_ref, wgrp_ref, scale_ref,
             wout_ref, gpost_ref, after_ref,
             dx_ref, dm_ref, draw_ref, dp_ref, dgpost_ref, dgpre_ref, dscale_ref, dpool_ref):
        i = pl.program_id(0)

        @pl.when(i == 0)
        def _():
            for r in (dgpost_ref, dgpre_ref, dscale_ref):
                r[...] = jnp.zeros_like(r)

        keep = jnp.where(i < steps - 1, 1.0, 0.0)
        dy = dx1_ref[...]
        dye = jnp.concatenate([dy, dx1n_ref[...] * keep], axis=0)
        me = jnp.concatenate([m_ref[...], mn_ref[...]], axis=0)
        gpost_v = gpost_ref[...]
        r = lax.rsqrt(jnp.mean(me * me, axis=-1, keepdims=True) + EPS)
        mh = me * r
        dgpost_ref[...] += jnp.sum((dye * mh)[:tm], axis=0, keepdims=True)
        dmh = dye * gpost_v
        dme = (r * (dmh - mh * jnp.mean(dmh * mh, axis=-1, keepdims=True))).astype(BF16)
        dm_ref[...] = dme[:tm]
        dmixed = _nt(dme, wout_ref[...])
        pooled_e = jnp.concatenate([pooled_ref[...], pooledn_ref[...]], axis=0)
        scale_v = scale_ref[...]
        for g, win in enumerate(B_WINDOWS):
            cols = slice(g * B_GROUP_DIM, (g + 1) * B_GROUP_DIM)
            raw = _nn(pooled_e[:, cols], wgrp_ref[g])
            dscale_ref[:, cols] += jnp.sum((dmixed[:, cols] * raw)[:tm], axis=0, keepdims=True)
            draw = (dmixed[:, cols] * scale_v[:, cols]).astype(BF16)
            draw_ref[:, cols] = draw[:tm]
            dpool = _nt(draw, wgrp_ref[g])
            acc = dpool / _window_counts(i * tm, n, win)
            shift = 1
            while shift < win:
                acc = acc + pltpu.roll(acc, n - shift, 0)
                shift *= 2
            dpool_ref[:, cols] = (acc - dpool)[:tm]
        dp = dpool_ref[...].astype(BF16)
        dp_ref[...] = dp
        dh1 = _nt(dp, win_ref[...])
        dxp, dgpre = _rms_bwd(x_ref[...], gpre_ref[...], dh1)
        dgpre_ref[...] += dgpre
        dx_ref[...] = dy + dxp

    vec = lambda w: _resident((1, w), (0, 0))
    acc = lambda shape: pl.BlockSpec(shape, lambda i: (0,) * len(shape))
    per = tm // HALO
    nxt = lambda i: (jnp.minimum((i + 1) * per, t // HALO - 1), 0)
    return _call(
        body, name=name, grid=(steps,),
        in_specs=[_rows(tm, d), pl.BlockSpec((HALO, d), nxt), _rows(tm, d), pl.BlockSpec((HALO, d), nxt), _rows(tm, d),
                  _rows(tm, d), pl.BlockSpec((HALO, d), nxt), vec(d), _wspec(lay.b_rows, d),
                  _resident((ngrp, B_GROUP_DIM, B_GROUP_DIM), (0, 0, 0)), vec(d), _wspec(lay.b_rows, d),
                  vec(d), ANY],
        out_specs=[_rows(tm, d)] * 4 + [acc((1, d))] * 3,
        out_shape=[jax.ShapeDtypeStruct((t, d), F32), jax.ShapeDtypeStruct((t, d), BF16),
                   jax.ShapeDtypeStruct((t, d), BF16), jax.ShapeDtypeStruct((t, d), BF16)]
                  + [jax.ShapeDtypeStruct((1, d), F32)] * 3,
        scratch_shapes=[pltpu.VMEM((tm, d), F32)],
        compiler_params=_params("arbitrary"),
    )(dx1, dx1, m, m, x, pooled, pooled, gpre, wg[0], wgrp, scale, wg[1], gpost, after)


def _f_fwd(x1, gpre, wg, lay, l, gpost, tm, name, target=None):
    t, d = x1.shape
    hid = N_DEV * lay.ffn_rows
    head = target is not None

    def body(x_ref, gpre_ref, wgate_ref, wup_ref, wdown_ref, gpost_ref, *rest):
        x2_ref, h2_ref, a_ref, b_ref, s_ref, f_ref = rest[-7:-1] if head else rest
        xv = x_ref[...]
        h2 = _rms_fwd(xv, gpre_ref[...]).astype(BF16)
        h2_ref[...] = h2
        a = _nt(h2, wgate_ref[...])
        b = _nt(h2, wup_ref[...])
        sig = jax.nn.sigmoid(a)
        silu = a * sig
        a_ref[...] = (b * (sig + silu * (1.0 - sig))).astype(BF16)
        b_ref[...] = silu.astype(BF16)
        s = (silu * b).astype(BF16)
        s_ref[...] = s
        f = _nn(s, wdown_ref[...])
        f_ref[...] = f
        x2 = xv + _rms_fwd(f, gpost_ref[...])
        if head:
            target_ref, loss_ref = rest[0], rest[-1]

            @pl.when(pl.program_id(0) == 0)
            def _():
                loss_ref[...] = jnp.zeros_like(loss_ref)

            diff = x2 - target_ref[...]
            x2_ref[...] = diff * (1.0 / d)
            sq = jnp.sum(jnp.sum(diff * diff, axis=0, keepdims=True), axis=1, keepdims=True)
            loss_ref[...] += sq * (0.5 / d)
        else:
            x2_ref[...] = x2

    vec = lambda w: _resident((1, w), (0, 0))
    return _call(
        body, name=name, grid=(t // tm,),
        in_specs=[_rows(tm, d), vec(d), _wspec(lay.ffn_rows, d), _wspec(lay.ffn_rows, d),
                  _wspec(lay.ffn_rows, d), vec(d)] + ([_rows(tm, d)] if head else []),
        out_specs=[_rows(tm, d), _rows(tm, d), _rows(tm, hid), _rows(tm, hid), _rows(tm, hid), _rows(tm, d)]
                  + ([pl.BlockSpec((8, 128), lambda i: (0, 0))] if head else []),
        out_shape=[jax.ShapeDtypeStruct((t, d), F32), jax.ShapeDtypeStruct((t, d), BF16),
                   jax.ShapeDtypeStruct((t, hid), BF16), jax.ShapeDtypeStruct((t, hid), BF16),
                   jax.ShapeDtypeStruct((t, hid), BF16), jax.ShapeDtypeStruct((t, d), F32)]
                  + ([jax.ShapeDtypeStruct((8, 128), F32)] if head else []),
        compiler_params=_params("arbitrary" if head else "parallel"),
    )(x1, gpre, wg[0], wg[1], wg[2], gpost, *([target] if head else []))


def _f_bwd(dx2, f, x1, a, b, gpre, wg, lay, l, gpost, after, tm, name):
    t, d = x1.shape
    hid = N_DEV * lay.ffn_rows

    def body(dx2_ref, f_ref, x_ref, a_ref, b_ref, gpre_ref, wgate_ref, wup_ref, wdown_ref, gpost_ref, after_ref,
             dx1_ref, df_ref, da_ref, db_ref, dgpost_ref, dgpre_ref):
        @pl.when(pl.program_id(0) == 0)
        def _():
            dgpost_ref[...] = jnp.zeros_like(dgpost_ref)
            dgpre_ref[...] = jnp.zeros_like(dgpre_ref)

        def put_df(rows, dx):
            df_ref[rows, :] = dx.astype(BF16)

        dgpost_ref[...] += _rms_bwd_slabs(lambda rows: f_ref[rows, :], lambda rows: dx2_ref[rows, :], gpost_ref[...],
                                          tm, tm, put_df)
        ds = _nt(df_ref[...], wdown_ref[...])
        da_ref[...] = (ds * a_ref[...].astype(F32)).astype(BF16)
        db_ref[...] = (ds * b_ref[...].astype(F32)).astype(BF16)
        dh2 = _nn(da_ref[...], wgate_ref[...]) + _nn(db_ref[...], wup_ref[...])

        def put_dx(rows, dx):
            dx1_ref[rows, :] = dx2_ref[rows, :] + dx

        dgpre_ref[...] += _rms_bwd_slabs(lambda rows: x_ref[rows, :], lambda rows: dh2[rows, :], gpre_ref[...],
                                         tm, tm, put_dx)

    vec = lambda w: _resident((1, w), (0, 0))
    acc = pl.BlockSpec((1, d), lambda i: (0, 0))
    return _call(
        body, name=name, grid=(t // tm,),
        in_specs=[_rows(tm, d), _rows(tm, d), _rows(tm, d), _rows(tm, hid), _rows(tm, hid), vec(d),
                  _wspec(lay.ffn_rows, d), _wspec(lay.ffn_rows, d),
                  _wspec(lay.ffn_rows, d), vec(d), ANY],
        out_specs=[_rows(tm, d), _rows(tm, d), _rows(tm, hid), _rows(tm, hid), acc, acc],
        out_shape=[jax.ShapeDtypeStruct((t, d), F32), jax.ShapeDtypeStruct((t, d), BF16),
                   jax.ShapeDtypeStruct((t, hid), BF16), jax.ShapeDtypeStruct((t, hid), BF16),
                   jax.ShapeDtypeStruct((1, d), F32), jax.ShapeDtypeStruct((1, d), F32)],
        compiler_params=_params("arbitrary"),
    )(dx2, f, x1, a, b, gpre, wg[0], wg[1], wg[2], gpost, after)


def _grad_into(gbuf, lhs, rhs, off, rows, name, after=None):
    t, m = lhs.shape
    d = rhs.shape[1]
    assert m == N_DEV * rows and off % rows == 0
    per_tile = {352: 4, 512: 2, 256: 4, 128: 8}[rows]
    tm = per_tile * rows
    assert tm % 128 == 0 and rows % 16 == 0
    tk = 2048 if t % 2048 == 0 else 256
    ksteps = t // tk
    fresh = isinstance(gbuf, int)
    shape = (N_DEV, gbuf, d) if fresh else gbuf.shape
    extra = ([] if fresh else [gbuf]) + ([] if after is None else [after])

    def body(l_ref, r_ref, *rest):
        o_ref, acc_ref = rest[-2:]
        k = pl.program_id(1)

        @pl.when(k == 0)
        def _():
            acc_ref[...] = jnp.zeros_like(acc_ref)

        acc_ref[...] += _tn(l_ref[...], r_ref[...])

        @pl.when(k == ksteps - 1)
        def _():
            o_ref[...] = acc_ref[...].reshape(per_tile, rows, d).astype(BF16)

    return _call(
        body, name=name, grid=(N_DEV // per_tile, ksteps),
        in_specs=[pl.BlockSpec((tk, tm), lambda i, k: (k, i)), pl.BlockSpec((tk, d), lambda i, k: (k, 0))]
                 + [ANY] * len(extra),
        out_specs=pl.BlockSpec((per_tile, rows, d), lambda i, k: (i, off // rows, 0)),
        out_shape=jax.ShapeDtypeStruct(shape, BF16),
        scratch_shapes=[pltpu.VMEM((tm, d), F32)],
        input_output_aliases={} if fresh else {2: 0},
        compiler_params=_params("parallel", "arbitrary"),
    )(lhs, rhs, *extra)


def _grad_grouped(pooled, draw, name):
    t, d = pooled.shape
    ngrp = len(B_WINDOWS)
    tk = 1024 if t % 1024 == 0 else 256

    def body(p_ref, q_ref, o_ref):
        @pl.when(pl.program_id(0) == 0)
        def _():
            o_ref[...] = jnp.zeros_like(o_ref)

        for g in range(ngrp):
            cols = slice(g * B_GROUP_DIM, (g + 1) * B_GROUP_DIM)
            o_ref[g] += _tn(p_ref[:, cols], q_ref[:, cols])

    return _call(
        body, name=name, grid=(t // tk,),
        in_specs=[_rows(tk, d), _rows(tk, d)],
        out_specs=pl.BlockSpec((ngrp, B_GROUP_DIM, B_GROUP_DIM), lambda i: (0, 0, 0)),
        out_shape=jax.ShapeDtypeStruct((ngrp, B_GROUP_DIM, B_GROUP_DIM), F32),
        compiler_params=_params("arbitrary"),
    )(pooled, draw)


def _peers():
    x, y, c = lax.axis_index("x"), lax.axis_index("y"), lax.axis_index("c")
    flip = lambda v, f: 1 - v if f else v
    peers = []
    for r in range(1, N_DEV):
        px, py, pc = flip(x, r & 4), flip(y, r & 2), flip(c, r & 1)
        peers.append(((px, py, pc), 4 * px + 2 * py + pc))
    return 4 * x + 2 * y + c, peers


HBM = pl.BlockSpec(memory_space=pltpu.HBM)
SEM = pl.BlockSpec(memory_space=pltpu.SEMAPHORE)
EFFECT = pltpu.SideEffectType.DATAFLOW_SIDE_EFFECTING


def _peer_copies(scatter, srcs, lands, send_sems, recv_sems):
    me, peers = _peers()
    copies = []
    for a in range(len(srcs)):
        rows = srcs[a].shape[0]
        block = lambda k: lands[a].at[pl.ds(pl.multiple_of(k * rows, 8), rows)]
        for r, (peer, pidx) in enumerate(peers):
            src = srcs[a].at[pidx] if scatter else srcs[a]
            mine = lands[a].at[r] if scatter else block(pidx)
            theirs = lands[a].at[r] if scatter else block(me)
            send = pltpu.make_async_remote_copy(src_ref=src, dst_ref=theirs, send_sem=send_sems[a].at[r],
                                                recv_sem=recv_sems[a].at[r], device_id=peer, device_id_type=MESH)
            recv = pltpu.make_async_remote_copy(src_ref=src, dst_ref=mine, send_sem=send_sems[a].at[r],
                                                recv_sem=recv_sems[a].at[r], device_id=peer, device_id_type=MESH)
            copies.append((send, recv))
    return copies


def _exchange_start(scatter, srcs, lands, after, name):
    n = len(srcs)

    def body(*refs):
        src_refs, land_refs = refs[:n], refs[n:2 * n]
        outs = refs[2 * n + 1:]
        send_sems, recv_sems, token = outs[:n], outs[n:2 * n], outs[-1]
        for send, _ in _peer_copies(scatter, src_refs, land_refs, send_sems, recv_sems):
            send.start()
        token[...] = jnp.zeros_like(token)

    hbm = lambda a: pltpu.with_memory_space_constraint(a, pltpu.HBM)
    res = _call(
        body, name=name,
        in_specs=[HBM] * (2 * n) + [ANY],
        out_specs=[SEM] * (2 * n) + [HBM] * (2 * n) + [pl.BlockSpec(memory_space=pltpu.VMEM)],
        out_shape=[pltpu.SemaphoreType.DMA((N_DEV - 1,))] * (2 * n)
                  + [pltpu.HBM(a.shape, a.dtype) for a in list(srcs) + list(lands)]
                  + [jax.ShapeDtypeStruct((8, 128), F32)],
        input_output_aliases={i: 2 * n + i for i in range(2 * n)},
        compiler_params=pltpu.CompilerParams(has_side_effects=EFFECT),
    )(*[hbm(a) for a in srcs], *[hbm(a) for a in lands], after)
    return res[:n], res[n:2 * n], res[2 * n:3 * n], res[3 * n:4 * n], res[-1]


def _exchange_wait(scatter, send_sems, recv_sems, srcs, lands, after, name):
    n = len(srcs)
    after = list(after) if isinstance(after, (list, tuple)) else [after]

    def body(*refs):
        src_refs, land_refs = refs[:n], refs[n:2 * n]
        send_refs, recv_refs = refs[2 * n:3 * n], refs[3 * n:4 * n]
        for send, recv in _peer_copies(scatter, src_refs, land_refs, send_refs, recv_refs):
            send.wait_send()
            recv.wait_recv()

    res = _call(
        body, name=name,
        in_specs=[HBM] * (2 * n) + [SEM] * (2 * n) + [ANY] * len(after),
        out_specs=[HBM] * (2 * n),
        out_shape=[pltpu.HBM(a.shape, a.dtype) for a in list(srcs) + list(lands)],
        input_output_aliases={i: i for i in range(2 * n)},
        compiler_params=pltpu.CompilerParams(has_side_effects=EFFECT),
    )(*srcs, *lands, *send_sems, *recv_sems, *after)
    return res[:n], res[n:]


def _row_tile(rows):
    if rows <= 512:
        return rows
    return max([tr for tr in range(16, 513, 16) if rows % tr == 0] or [rows])


def _sum_parts(own, got, me, name):
    _, rows, w = own.shape
    tr = _row_tile(rows)

    def body(me_ref, a_ref, b_ref, o_ref):
        s = a_ref[...].astype(F32)
        for j in range(N_DEV - 1):
            s = s + b_ref[j].astype(F32)
        o_ref[...] = s

    return _call(
        body, name=name,
        grid_spec=pltpu.PrefetchScalarGridSpec(
            num_scalar_prefetch=1, grid=(rows // tr,),
            in_specs=[pl.BlockSpec((None, tr, w), lambda i, me_ref: (me_ref[0], i, 0)),
                      pl.BlockSpec((N_DEV - 1, tr, w), lambda i, me_ref: (0, i, 0))],
            out_specs=pl.BlockSpec((tr, w), lambda i, me_ref: (i, 0))),
        out_shape=jax.ShapeDtypeStruct((rows, w), F32),
        compiler_params=_params("parallel"),
    )(me, own, got)


def _sum_devices(stacked, name):
    k, rows, w = stacked.shape
    tr = _row_tile(rows)

    def body(a_ref, o_ref):
        s = a_ref[0]
        for j in range(1, k):
            s = s + a_ref[j]
        o_ref[...] = s

    return _call(
        body, name=name, grid=(rows // tr,),
        in_specs=[pl.BlockSpec((k, tr, w), lambda i: (0, i, 0))],
        out_specs=pl.BlockSpec((tr, w), lambda i: (i, 0)),
        out_shape=jax.ShapeDtypeStruct((rows, w), F32),
        compiler_params=_params("parallel"),
    )(stacked)


def _adamw(w, g, m, v, name):
    rows, cols = w.shape
    tr = _row_tile(rows)

    def body(w_ref, g_ref, m_ref, v_ref, d_ref, nm_ref, nv_ref):
        gv = g_ref[...]
        nm = ADAM_B1 * m_ref[...] + (1.0 - ADAM_B1) * gv
        nv = ADAM_B2 * v_ref[...] + (1.0 - ADAM_B2) * (gv * gv)
        m_hat = nm / (1.0 - ADAM_B1 ** ADAM_STEP)
        v_hat = nv / (1.0 - ADAM_B2 ** ADAM_STEP)
        d_ref[...] = -ADAM_LR * (m_hat / (jnp.sqrt(v_hat) + ADAM_EPS) + ADAM_WD * w_ref[...])
        nm_ref[...] = nm
        nv_ref[...] = nv

    spec = pl.BlockSpec((tr, cols), lambda i: (i, 0))
    return _call(
        body, name=name, grid=(rows // tr,),
        in_specs=[spec] * 4, out_specs=[spec] * 3,
        out_shape=[jax.ShapeDtypeStruct((rows, cols), F32)] * 3,
        compiler_params=_params("parallel"),
    )(w, g, m, v)


SMALL = ("a_ln_g", "a_ln_b", "a_w_s", "a_b_s", "mix_pre_g", "mix_post_g", "ffn_pre_g", "ffn_post_g")


def _pack_small(parts, d, last_row=None):
    rows = [parts[k].reshape(-1, d) for k in SMALL] + ([] if last_row is None else [last_row])
    flat = jnp.concatenate(rows, axis=0)
    return jnp.pad(flat, ((0, -flat.shape[0] % 8), (0, 0)))


def _unpack_small(flat, like):
    out, r = {}, 0
    for k in SMALL:
        n = like[k].size // flat.shape[1]
        out[k] = flat[r:r + n].reshape(like[k].shape)
        r += n
    return out


def kernel(x, a_w_in, a_ln_g, a_ln_b, a_w_s, a_b_s, a_w_out, b_w_in, b_w_grp, b_scale, b_w_out, mix_pre_g, mix_post_g, ffn_pre_g, ffn_post_g, ffn_w_gate, ffn_w_up, ffn_w_down, loss_target, m_a_w_in, m_a_ln_g, m_a_ln_b, m_a_w_s, m_a_b_s, m_a_w_out, m_b_w_in, m_b_w_grp, m_b_scale, m_b_w_out, m_mix_pre_g, m_mix_post_g, m_ffn_pre_g, m_ffn_post_g, m_ffn_w_gate, m_ffn_w_up, m_ffn_w_down, v_a_w_in, v_a_ln_g, v_a_ln_b, v_a_w_s, v_a_b_s, v_a_w_out, v_b_w_in, v_b_w_grp, v_b_scale, v_b_w_out, v_mix_pre_g, v_mix_post_g, v_ffn_pre_g, v_ffn_post_g, v_ffn_w_gate, v_ffn_w_up, v_ffn_w_down):
    args = dict(locals())
    names = ("a_w_in", "a_ln_g", "a_ln_b", "a_w_s", "a_b_s", "a_w_out", "b_w_in", "b_w_grp", "b_scale", "b_w_out",
             "mix_pre_g", "mix_post_g", "ffn_pre_g", "ffn_post_g", "ffn_w_gate", "ffn_w_up", "ffn_w_down")
    w = {k: args[k] for k in names}
    mom = {k: args["m_" + k] for k in names}
    var = {k: args["v_" + k] for k in names}

    t, d = x.shape[1], x.shape[2]
    ffn_local = ffn_w_gate.shape[2]
    lay = _Layout(d, ffn_local)
    me = 4 * lax.axis_index("x") + 2 * lax.axis_index("y") + lax.axis_index("c")
    me1 = jnp.reshape(me, (1,)).astype(jnp.int32)

    def landing(block):
        rows = block.shape[0]
        zone = lax.empty((N_DEV * rows,) + block.shape[1:], block.dtype)
        return lax.dynamic_update_slice(zone, block, (me * rows,) + (0,) * (block.ndim - 1))

    def shards(i, mixer, zero):
        j = i // 2
        if not mixer:
            parts = [ffn_w_gate[i].T, ffn_w_up[i].T, ffn_w_down[i]]
        elif i % 2 == 0:
            parts = [a_w_in[j].T, a_w_out[j]]
        else:
            parts = [b_w_in[j], b_w_out[j]]
        return [(p + zero).astype(BF16) for p in parts]

    nsub = 2 * DEPTH
    wg = [None] * nsub
    first = shards(0, True, 0.0)
    first = _exchange_start(False, first, [landing(b) for b in first], jnp.zeros((8, 128), F32), "gather_first_start")
    zero = first[4][0, 0]
    ngrp = len(B_WINDOWS)
    grp_local = b_w_grp.shape[2]
    sdev = b_scale.shape[1]
    side_rows = 2 * ngrp * grp_local
    side = jnp.concatenate(
        [b_w_grp.reshape(side_rows, B_GROUP_DIM),
         jnp.pad(b_scale, ((0, 6), (0, B_GROUP_DIM - sdev)))], axis=0) + zero
    later, where = [side], [slice(0, 1)]
    for k in range(1, nsub):
        new = shards(k // 2, k % 2 == 0, zero)
        where.append(slice(len(later), len(later) + len(new)))
        later += new
    send_sems, recv_sems, later, zones, token = _exchange_start(
        False, later, [landing(b) for b in later], first[4], "gather_start")
    _, wg[0] = _exchange_wait(False, *first[:4], token, "gather_first_wait")

    def gathered(k, after):
        s = where[k]
        _, got = _exchange_wait(False, send_sems[s], recv_sems[s], later[s], zones[s], after, f"gather_wait_{k}")
        return got

    row = lambda a: a.reshape(1, -1)
    bst = jnp.transpose(a_b_s, (0, 2, 1))

    tm = 256 if t % 256 == 0 else CHUNK
    tm_abwd = tm

    saved = []
    h = x[0]
    wgrp_full = scale_full = None
    for i in range(DEPTH):
        j = i // 2
        gpre = row(mix_pre_g[i])
        if i > 0:
            wg[2 * i] = gathered(2 * i, h)
        if i % 2 == 0:
            x1, h1, gp, u, vh, rs, gated, m = _a_fwd(h, gpre, wg[2 * i], lay, j, row(a_ln_g[j]), row(a_ln_b[j]),
                                                     a_w_s[j], bst[j], row(mix_post_g[i]), tm, f"a_fwd_{j}")
            mix = dict(h1=h1, gp=gp, u=u, vh=vh, rs=rs, gated=gated, m=m)
        else:
            if wgrp_full is None:
                side_g = gathered(0, h)[0].reshape(N_DEV, side_rows + 8, B_GROUP_DIM)
                wgrp_full = (side_g[:, :side_rows].reshape(N_DEV, 2, ngrp, grp_local, B_GROUP_DIM)
                             .transpose(1, 2, 0, 3, 4).reshape(2, ngrp, B_GROUP_DIM, B_GROUP_DIM).astype(BF16))
                scale_full = (side_g[:, side_rows:side_rows + 2, :sdev].transpose(1, 0, 2)
                              .reshape(2, 1, N_DEV * sdev))
            x1, h1, pooled, mixed, m = _b_fwd(h, gpre, wg[2 * i], lay, j, wgrp_full[j], scale_full[j],
                                              row(mix_post_g[i]), tm, f"b_fwd_{j}")
            mix = dict(h1=h1, pooled=pooled, mixed=mixed, m=m)
        wg[2 * i + 1] = gathered(2 * i + 1, x1)
        x2, h2, a, b, s, f, *loss_acc = _f_fwd(x1, row(ffn_pre_g[i]), wg[2 * i + 1], lay, i, row(ffn_post_g[i]), tm,
                                               f"f_fwd_{i}", loss_target[0] if i == DEPTH - 1 else None)
        saved.append(dict(x=h, x1=x1, mix=mix, h2=h2, a=a, b=b, s=s, f=f))
        h = x2
    dy, (loss_acc,) = h, loss_acc

    small_g = {k: [None] * w[k].shape[0] for k in SMALL}
    dgrp, dscale = [None, None], [None, None]
    pending = [None] * nsub
    token = jnp.zeros((8, 128), F32)

    def scatter(k, gbuf):
        got = pltpu.with_memory_space_constraint(lax.empty((N_DEV - 1,) + gbuf.shape[1:], gbuf.dtype), pltpu.HBM)
        ss, rs, src, zone, tok = _exchange_start(True, [gbuf], [got], token, f"scatter_start_{k}")
        pending[k] = (ss, rs, src, zone)
        return tok

    def small_exchanges():
        side_grad = jnp.concatenate(
            [jnp.stack(dgrp).reshape(2, ngrp, N_DEV, grp_local, B_GROUP_DIM).transpose(2, 0, 1, 3, 4)
             .reshape(N_DEV, side_rows, B_GROUP_DIM),
             jnp.pad(jnp.stack(dscale).reshape(2, N_DEV, sdev).transpose(1, 0, 2),
                     ((0, 0), (0, 6), (0, B_GROUP_DIM - sdev)))], axis=1)
        small_part = _pack_small({k: jnp.stack(small_g[k]) for k in SMALL}, d,
                                 jnp.broadcast_to(loss_acc[:1, :1], (1, d)))
        got = pltpu.with_memory_space_constraint(lax.empty((N_DEV - 1,) + side_grad.shape[1:], F32), pltpu.HBM)
        side_x = _exchange_start(True, [side_grad], [got], token, "side_scatter_start")
        small_x = _exchange_start(False, [small_part], [landing(small_part)], side_x[4], "small_gather_start")
        return side_x[:4], small_x[:4], small_x[4]

    for i in reversed(range(DEPTH)):
        sv = saved[i]
        j = i // 2
        wf, wm = wg[2 * i + 1], wg[2 * i]
        dx1, df, da, db, dgpost, dgpre = _f_bwd(dy, sv["f"], sv["x1"], sv["a"], sv["b"], row(ffn_pre_g[i]), wf, lay, i,
                                                 row(ffn_post_g[i]), token, tm, f"f_bwd_{i}")
        small_g["ffn_post_g"][i], small_g["ffn_pre_g"][i] = dgpost[0], dgpre[0]
        gbuf = _grad_into(lay.f_total, da, sv["h2"], lay.gate[i], lay.ffn_rows, f"g_gate_{i}")
        gbuf = _grad_into(gbuf, db, sv["h2"], lay.up[i], lay.ffn_rows, f"g_up_{i}")
        gbuf = _grad_into(gbuf, sv["s"], df, lay.down[i], lay.ffn_rows, f"g_down_{i}")
        token = scatter(2 * i + 1, gbuf)
        mix = sv["mix"]
        gpost = row(mix_post_g[i])
        if i % 2 == 0:
            dx, dm, dz, dgpost, dgpre, dlng, dlnb, dws, dbt = _a_bwd(
                dx1, mix["m"], sv["x"], mix["gp"], mix["u"], mix["vh"], mix["rs"], row(mix_pre_g[i]), wm, lay, j,
                row(a_ln_g[j]), row(a_ln_b[j]), a_w_s[j], bst[j], gpost, token, tm_abwd, f"a_bwd_{j}")
            small_g["a_ln_g"][j], small_g["a_ln_b"][j] = dlng[0], dlnb[0]
            small_g["a_w_s"][j], small_g["a_b_s"][j] = dws, dbt[:, :A_GROUPS].T
            small_g["mix_post_g"][i], small_g["mix_pre_g"][i] = dgpost[0], dgpre[0]
            order = None
            if i == 0:
                side_x, small_x, order = small_exchanges()
            gbuf = _grad_into(lay.a_total, dz, mix["h1"], lay.a_in[j], lay.a_in_rows, f"g_a_in_{j}", after=order)
            gbuf = _grad_into(gbuf, mix["gated"], dm, lay.a_out[j], lay.a_out_rows, f"g_a_out_{j}")
        else:
            dx, dm, draw, dp, dgpost, dgpre, dsc = _b_bwd(
                dx1, mix["m"], sv["x"], mix["pooled"], row(mix_pre_g[i]), wm, lay, j, wgrp_full[j], scale_full[j],
                gpost, token, tm, f"b_bwd_{j}")
            dscale[j] = dsc[0]
            dgrp[j] = _grad_grouped(mix["pooled"], draw, f"g_b_grp_{j}")
            gbuf = _grad_into(lay.b_total, mix["h1"], dp, lay.b_in[j], lay.b_rows, f"g_b_in_{j}")
            gbuf = _grad_into(gbuf, mix["mixed"], dm, lay.b_out[j], lay.b_rows, f"g_b_out_{j}")
            small_g["mix_post_g"][i], small_g["mix_pre_g"][i] = dgpost[0], dgpre[0]
        token = scatter(2 * i, gbuf)
        dy = dx
    grad_x = dy[None]

    g_sub = [None] * nsub

    def arrived(k, after):
        ss, rs, src, zone = pending[k]
        (own,), (got,) = _exchange_wait(True, ss, rs, src, zone, after, f"scatter_wait_{k}")
        g_sub[k] = _sum_parts(own, got, me1, f"sum_grads_{k}")

    def rows_of(k, off, n):
        return g_sub[k][off:off + n]

    grads, delta, new_m, new_v = {}, {}, {}, {}

    def update(k):
        turn = (lambda a: jnp.swapaxes(a, 1, 2)) if k in ("a_w_in", "ffn_w_gate", "ffn_w_up") else (lambda a: a)
        shape = turn(w[k]).shape
        two = lambda a: a.reshape(-1, shape[-1])
        dl, nm, nv = _adamw(two(turn(w[k])), two(grads[k]), two(turn(mom[k])), two(turn(var[k])), f"adamw_{k}")
        delta[k], new_m[k], new_v[k] = (turn(a.reshape(shape)) for a in (dl, nm, nv))
        grads[k] = turn(grads[k])

    for k in range(1, nsub):
        arrived(k, token)
    grads["ffn_w_gate"] = jnp.stack([rows_of(2 * l + 1, lay.gate[l], ffn_local) for l in range(DEPTH)])
    grads["ffn_w_up"] = jnp.stack([rows_of(2 * l + 1, lay.up[l], ffn_local) for l in range(DEPTH)])
    grads["ffn_w_down"] = jnp.stack([rows_of(2 * l + 1, lay.down[l], ffn_local) for l in range(DEPTH)])
    grads["b_w_in"] = jnp.stack([rows_of(4 * j + 2, lay.b_in[j], lay.b_rows) for j in range(2)])
    grads["b_w_out"] = jnp.stack([rows_of(4 * j + 2, lay.b_out[j], lay.b_rows) for j in range(2)])
    early = ("ffn_w_gate", "ffn_w_up", "ffn_w_down", "b_w_in", "b_w_out")
    for k in early:
        update(k)

    (side_own,), (side_got,) = _exchange_wait(True, *side_x, [delta[k] for k in early], "side_scatter_wait")
    g_side = _sum_parts(side_own, side_got, me1, "sum_side")
    grads["b_w_grp"] = g_side[:side_rows].reshape(b_w_grp.shape)
    grads["b_scale"] = g_side[side_rows:side_rows + 2, :sdev]
    update("b_w_grp")
    update("b_scale")
    _, (small_all,) = _exchange_wait(False, *small_x, [delta["b_w_grp"], delta["b_scale"]], "small_gather_wait")
    small_sum = _sum_devices(small_all.reshape(N_DEV, -1, d), "sum_small")
    g_small = _unpack_small(small_sum, w)
    loss = small_sum[sum(w[k].size for k in SMALL) // d, 0]
    grads.update(g_small)
    dl, nm, nv = _adamw(_pack_small(w, d), _pack_small(g_small, d), _pack_small(mom, d), _pack_small(var, d),
                        "adamw_small")
    delta.update(_unpack_small(dl, w))
    new_m.update(_unpack_small(nm, w))
    new_v.update(_unpack_small(nv, w))

    arrived(0, dl)
    grads["a_w_in"] = jnp.stack([rows_of(4 * j, lay.a_in[j], lay.a_in_rows) for j in range(2)])
    grads["a_w_out"] = jnp.stack([rows_of(4 * j, lay.a_out[j], lay.a_out_rows) for j in range(2)])
    update("a_w_in")
    update("a_w_out")

    return (loss, grad_x, *[grads[k] for k in names], *[delta[k] for k in names], *[new_m[k] for k in names],
            *[new_v[k] for k in names])
```

```python
import math

import jax
import jax.numpy as jnp
from jax import lax
from jax.experimental import pallas as pl
from jax.experimental.pallas import tpu as pltpu

F32 = jnp.float32
BF16 = jnp.bfloat16
MESH = pl.DeviceIdType.MESH
ANY = pl.BlockSpec(memory_space=pl.ANY)

N_DEV = 8
EPS = 1e-6
CHUNK = 128
A_GROUPS = 8
A_GROUP_DIM = 256
B_WINDOWS = (2, 4, 8, 16)
B_GROUP_DIM = 256
HALO = 16
DEPTH = 4

ADAM_LR = 0.001
ADAM_B1 = 0.9
ADAM_B2 = 0.999
ADAM_EPS = 1e-08
ADAM_WD = 0.01
ADAM_STEP = 10

VMEM_LIMIT_BYTES = 60 * 1024 * 1024

ERF_P = 0.3275911
ERF_A = (0.254829592, -0.284496736, 1.421413741, -1.453152027, 1.061405429)
INV_SQRT2 = 1.0 / math.sqrt(2.0)
LOG2_E = 1.0 / math.log(2.0)
INV_SQRT_2PI = 1.0 / math.sqrt(2.0 * math.pi)


def _call(body, **kw):
    return pl.pallas_call(body, **kw)


def _params(*semantics):
    return pltpu.CompilerParams(dimension_semantics=semantics or None, vmem_limit_bytes=VMEM_LIMIT_BYTES)


def _resident(shape, index):
    return pl.BlockSpec(shape, lambda *_: index, pipeline_mode=pl.Buffered(1))


def _rows(tm, width):
    return pl.BlockSpec((tm, width), lambda i: (i, 0))


def _nn(a, b):
    return jnp.dot(a, b, preferred_element_type=F32)


def _nt(a, b):
    return lax.dot_general(a, b, (((1,), (1,)), ((), ())), preferred_element_type=F32)


def _tn(a, b):
    return lax.dot_general(a, b, (((0,), (0,)), ((), ())), preferred_element_type=F32)


def _rms_fwd(x, g):
    r = lax.rsqrt(jnp.mean(x * x, axis=-1, keepdims=True) + EPS)
    return x * r * g


def _rms_bwd(x, g, dy):
    r = lax.rsqrt(jnp.mean(x * x, axis=-1, keepdims=True) + EPS)
    xh = x * r
    dg = jnp.sum(dy * xh, axis=0, keepdims=True)
    dxh = dy * g
    dx = r * (dxh - xh * jnp.mean(dxh * xh, axis=-1, keepdims=True))
    return dx, dg


SLAB = 16


def _slabs(n):
    return [slice(r, r + SLAB) for r in range(0, n, SLAB)]


def _rms_bwd_slabs(x_at, dy_at, g, n, n_sum, emit):
    acc = jnp.zeros((8, g.shape[1]), F32)
    for rows in _slabs(n):
        x = x_at(rows)
        dy = dy_at(rows)
        r = lax.rsqrt(jnp.mean(x * x, axis=-1, keepdims=True) + EPS)
        xh = x * r
        if rows.start < n_sum:
            p = dy * xh
            acc = acc + p[:8] + p[8:]
        dxh = dy * g
        emit(rows, r * (dxh - xh * jnp.mean(dxh * xh, axis=-1, keepdims=True)))
    return jnp.sum(acc, axis=0, keepdims=True)


def _gelu(z):
    phi = 0.5 + 0.5 * lax.erf(z * INV_SQRT2)
    e = jnp.exp2(z * z * (-0.5 * LOG2_E))
    return z * phi, phi + z * e * INV_SQRT_2PI


def _layernorm_stats(v):
    mu = jnp.mean(v, axis=-1, keepdims=True)
    xc = v - mu
    rs = lax.rsqrt(jnp.mean(xc * xc, axis=-1, keepdims=True) + EPS)
    return xc * rs, rs


def _tril_mask():
    r = lax.broadcasted_iota(jnp.int32, (CHUNK, CHUNK), 0)
    c = lax.broadcasted_iota(jnp.int32, (CHUNK, CHUNK), 1)
    return r >= c


class _Layout:
    def __init__(self, d, ffn_rows):
        self.ffn_rows = ffn_rows
        self.gate, self.up, self.down = [0] * DEPTH, [self.ffn_rows] * DEPTH, [2 * self.ffn_rows] * DEPTH
        self.f_total = 3 * self.ffn_rows
        self.a_in_rows, self.a_out_rows, self.b_rows = 4 * d // N_DEV, 2 * d // N_DEV, d // N_DEV
        self.a_in, self.a_out = [0, 0], [self.a_in_rows] * 2
        self.a_total = self.a_in_rows + self.a_out_rows
        self.b_in, self.b_out = [0, 0], [self.b_rows] * 2
        self.b_total = 2 * self.b_rows


def _wspec(rows, d):
    return _resident((N_DEV * rows, d), (0, 0))


def _a_fwd(x, gpre, wg, lay, j, lng, lnb, ws, bst, gpost, tm, name):
    t, d = x.shape
    aw = 2 * d
    nch = tm // CHUNK

    def body(x_ref, gpre_ref, win_ref, lng_ref, lnb_ref, ws_ref, bst_ref, wout_ref, gpost_ref,
             x1_ref, h1_ref, gp_ref, u_ref, vh_ref, rs_ref, gated_ref, m_ref):
        xv = x_ref[...]
        h1 = _rms_fwd(xv, gpre_ref[...]).astype(BF16)
        h1_ref[...] = h1
        z = _nt(h1, win_ref[...])
        u, du_dz = _gelu(z[:, :aw])
        v, dv_dz = _gelu(z[:, aw:])
        gp_ref[:, :aw] = du_dz.astype(BF16)
        gp_ref[:, aw:] = dv_dz.astype(BF16)
        u_ref[...] = u.astype(BF16)
        vh, rs = _layernorm_stats(v)
        vh_ref[...] = vh.astype(BF16)
        rs_ref[...] = jnp.broadcast_to(rs, rs_ref.shape)
        vn = (vh * lng_ref[...] + lnb_ref[...]).astype(BF16)
        mask = _tril_mask()
        for g in range(A_GROUPS):
            wm = jnp.where(mask, ws_ref[g], 0.0).astype(BF16)
            cols = slice(g * A_GROUP_DIM, (g + 1) * A_GROUP_DIM)
            for c in range(nch):
                rows = slice(c * CHUNK, (c + 1) * CHUNK)
                sv = _nn(wm, vn[rows, cols]) + bst_ref[:, g:g + 1]
                gated_ref[rows, cols] = (u[rows, cols] * sv).astype(BF16)
        m = _nn(gated_ref[...], wout_ref[...])
        m_ref[...] = m
        x1_ref[...] = xv + _rms_fwd(m, gpost_ref[...])

    vec = lambda w: _resident((1, w), (0, 0))
    return _call(
        body, name=name, grid=(t // tm,),
        in_specs=[_rows(tm, d), vec(d), _wspec(lay.a_in_rows, d), vec(aw), vec(aw),
                  _resident((A_GROUPS, CHUNK, CHUNK), (0, 0, 0)), _resident((CHUNK, A_GROUPS), (0, 0)),
                  _wspec(lay.a_out_rows, d), vec(d)],
        out_specs=[_rows(tm, d), _rows(tm, d), _rows(tm, 2 * aw), _rows(tm, aw), _rows(tm, aw), _rows(tm, 128),
                   _rows(tm, aw), _rows(tm, d)],
        out_shape=[jax.ShapeDtypeStruct((t, d), F32), jax.ShapeDtypeStruct((t, d), BF16),
                   jax.ShapeDtypeStruct((t, 2 * aw), BF16), jax.ShapeDtypeStruct((t, aw), BF16),
                   jax.ShapeDtypeStruct((t, aw), BF16), jax.ShapeDtypeStruct((t, 128), F32),
                   jax.ShapeDtypeStruct((t, aw), BF16), jax.ShapeDtypeStruct((t, d), F32)],
        compiler_params=_params("parallel"),
    )(x, gpre, wg[0], lng, lnb, ws, bst, wg[1], gpost)


def _a_bwd(dx1, m, x, gp, u, vh, rs, gpre, wg, lay, j, lng, lnb, ws, bst, gpost, after, tm, name):
    t, d = x.shape
    aw = 2 * d
    nch = tm // CHUNK

    def body(dx1_ref, m_ref, x_ref, gp_ref, u_ref, vh_ref, rs_ref, gpre_ref, win_ref, lng_ref, lnb_ref, ws_ref, bst_ref,
             wout_ref, gpost_ref, after_ref,
             dx_ref, dm_ref, dz_ref, dgpost_ref, dgpre_ref, dlng_ref, dlnb_ref, dws_ref, dbt_ref, dvn_ref):
        @pl.when(pl.program_id(0) == 0)
        def _():
            for r in (dgpost_ref, dgpre_ref, dlng_ref, dlnb_ref, dws_ref, dbt_ref):
                r[...] = jnp.zeros_like(r)

        def put_dm(rows, dx):
            dm_ref[rows, :] = dx.astype(BF16)

        dgpost_ref[...] += _rms_bwd_slabs(lambda rows: m_ref[rows, :], lambda rows: dx1_ref[rows, :], gpost_ref[...],
                                          tm, tm, put_dm)
        dgated = _nt(dm_ref[...], wout_ref[...])

        vh = vh_ref[...].astype(F32)
        rs = rs_ref[:, :1]
        lng_v = lng_ref[...]
        vn = (vh * lng_v + lnb_ref[...]).astype(BF16)
        mask = _tril_mask()
        lane = lax.broadcasted_iota(jnp.int32, (CHUNK, CHUNK), 1)
        for g in range(A_GROUPS):
            wm = jnp.where(mask, ws_ref[g], 0.0).astype(BF16)
            cols = slice(g * A_GROUP_DIM, (g + 1) * A_GROUP_DIM)
            dws_g = jnp.zeros((CHUNK, CHUNK), F32)
            db_g = jnp.zeros((CHUNK, 1), F32)
            for c in range(nch):
                rows = slice(c * CHUNK, (c + 1) * CHUNK)
                vn_cg = vn[rows, cols]
                sv = _nn(wm, vn_cg) + bst_ref[:, g:g + 1]
                dg_cg = dgated[rows, cols]
                dsv = dg_cg * u_ref[rows, cols].astype(F32)
                dsv_bf = dsv.astype(BF16)
                db_g = db_g + jnp.sum(dsv, axis=1, keepdims=True)
                dws_g = dws_g + _nt(dsv_bf, vn_cg)
                dvn_ref[rows, cols] = _tn(wm, dsv_bf)
                dz_ref[rows, cols] = (dg_cg * sv * gp_ref[rows, cols].astype(F32)).astype(BF16)
            dws_ref[g] += jnp.where(mask, dws_g, 0.0)
            dbt_ref[...] += jnp.where(lane == g, db_g, 0.0)
        dvn = dvn_ref[...]
        dlng_ref[...] += jnp.sum(dvn * vh, axis=0, keepdims=True)
        dlnb_ref[...] += jnp.sum(dvn, axis=0, keepdims=True)
        dvh = dvn * lng_v
        dv = rs * (dvh - jnp.mean(dvh, axis=-1, keepdims=True) - vh * jnp.mean(dvh * vh, axis=-1, keepdims=True))
        dz_ref[:, aw:] = (dv * gp_ref[:, aw:].astype(F32)).astype(BF16)
        dh1 = _nn(dz_ref[...], win_ref[...])

        def put_dx(rows, dx):
            dx_ref[rows, :] = dx1_ref[rows, :] + dx

        dgpre_ref[...] += _rms_bwd_slabs(lambda rows: x_ref[rows, :], lambda rows: dh1[rows, :], gpre_ref[...],
                                         tm, tm, put_dx)

    vec = lambda w: _resident((1, w), (0, 0))
    acc = lambda shape: pl.BlockSpec(shape, lambda i: (0,) * len(shape))
    return _call(
        body, name=name, grid=(t // tm,),
        in_specs=[_rows(tm, d), _rows(tm, d), _rows(tm, d), _rows(tm, 2 * aw), _rows(tm, aw), _rows(tm, aw),
                  _rows(tm, 128), vec(d), _wspec(lay.a_in_rows, d), vec(aw), vec(aw),
                  _resident((A_GROUPS, CHUNK, CHUNK), (0, 0, 0)), _resident((CHUNK, A_GROUPS), (0, 0)),
                  _wspec(lay.a_out_rows, d), vec(d), ANY],
        out_specs=[_rows(tm, d), _rows(tm, d), _rows(tm, 2 * aw), acc((1, d)), acc((1, d)), acc((1, aw)), acc((1, aw)),
                   acc((A_GROUPS, CHUNK, CHUNK)), acc((CHUNK, CHUNK))],
        out_shape=[jax.ShapeDtypeStruct((t, d), F32), jax.ShapeDtypeStruct((t, d), BF16),
                   jax.ShapeDtypeStruct((t, 2 * aw), BF16), jax.ShapeDtypeStruct((1, d), F32),
                   jax.ShapeDtypeStruct((1, d), F32), jax.ShapeDtypeStruct((1, aw), F32),
                   jax.ShapeDtypeStruct((1, aw), F32), jax.ShapeDtypeStruct((A_GROUPS, CHUNK, CHUNK), F32),
                   jax.ShapeDtypeStruct((CHUNK, CHUNK), F32)],
        scratch_shapes=[pltpu.VMEM((tm, aw), F32)],
        compiler_params=_params("arbitrary"),
    )(dx1, m, x, gp, u, vh, rs, gpre, wg[0], lng, lnb, ws, bst, wg[1], gpost, after)


def _window_counts(first_row, n, win):
    tpos = first_row + lax.broadcasted_iota(jnp.int32, (n, 1), 0)
    return jnp.clip(tpos + 1, 1, win).astype(F32)


def _b_fwd(x, gpre, wg, lay, j, wgrp, scale, gpost, tm, name):
    t, d = x.shape
    n = tm + HALO
    ngrp = len(B_WINDOWS)

    def body(x_ref, xprev_ref, gpre_ref, win_ref, wgrp_ref, scale_ref, wout_ref, gpost_ref,
             x1_ref, h1_ref, pooled_ref, mixed_ref, m_ref):
        i = pl.program_id(0)
        xv = x_ref[...]
        keep = jnp.where(i > 0, 1.0, 0.0)
        xe = jnp.concatenate([xprev_ref[...] * keep, xv], axis=0)
        h1e = _rms_fwd(xe, gpre_ref[...]).astype(BF16)
        h1_ref[...] = h1e[HALO:]
        p = _nn(h1e, win_ref[...])
        acc = p
        shift = 1
        for g, win in enumerate(B_WINDOWS):
            lo = g * B_GROUP_DIM
            if g > 0:
                acc = acc[:, B_GROUP_DIM:]
            while shift < win:
                acc = acc + pltpu.roll(acc, shift, 0)
                shift *= 2
            cnt = _window_counts(i * tm - HALO, n, win)
            pooled = acc[:, :B_GROUP_DIM] / cnt - p[:, lo:lo + B_GROUP_DIM]
            pooled_ref[:, lo:lo + B_GROUP_DIM] = pooled[HALO:].astype(BF16)
        for g in range(ngrp):
            cols = slice(g * B_GROUP_DIM, (g + 1) * B_GROUP_DIM)
            raw = _nn(pooled_ref[:, cols], wgrp_ref[g])
            mixed_ref[:, cols] = (raw * scale_ref[:, cols]).astype(BF16)
        m = _nn(mixed_ref[...], wout_ref[...])
        m_ref[...] = m
        x1_ref[...] = xv + _rms_fwd(m, gpost_ref[...])

    vec = lambda w: _resident((1, w), (0, 0))
    per = tm // HALO
    return _call(
        body, name=name, grid=(t // tm,),
        in_specs=[_rows(tm, d), pl.BlockSpec((HALO, d), lambda i: (jnp.maximum(i * per - 1, 0), 0)), vec(d),
                  _wspec(lay.b_rows, d), _resident((ngrp, B_GROUP_DIM, B_GROUP_DIM), (0, 0, 0)), vec(d),
                  _wspec(lay.b_rows, d), vec(d)],
        out_specs=[_rows(tm, d)] * 5,
        out_shape=[jax.ShapeDtypeStruct((t, d), F32), jax.ShapeDtypeStruct((t, d), BF16),
                   jax.ShapeDtypeStruct((t, d), BF16), jax.ShapeDtypeStruct((t, d), BF16),
                   jax.ShapeDtypeStruct((t, d), F32)],
        compiler_params=_params("parallel"),
    )(x, x, gpre, wg[0], wgrp, scale, wg[1], gpost)


def _b_bwd(dx1, m, x, pooled, gpre, wg, lay, j, wgrp, scale, gpost, after, tm, name):
    t, d = x.shape
    n = tm + HALO
    ngrp = len(B_WINDOWS)
    steps = t // tm

    def body(dx1_ref, dx1n_ref, m_ref, mn_ref, x_ref, pooled_ref, pooledn_ref, gpre_ref, win_ref, wgrp_ref, scale_ref,
             wout_ref, gpost_ref, after_ref,
             dx_ref, dm_ref, draw_ref, dp_ref, dgpost_ref, dgpre_ref, dscale_ref, dpool_ref):
        i = pl.program_id(0)

        @pl.when(i == 0)
        def _():
            for r in (dgpost_ref, dgpre_ref, dscale_ref):
                r[...] = jnp.zeros_like(r)

        keep = jnp.where(i < steps - 1, 1.0, 0.0)
        dy = dx1_ref[...]
        dye = jnp.concatenate([dy, dx1n_ref[...] * keep], axis=0)
        me = jnp.concatenate([m_ref[...], mn_ref[...]], axis=0)
        gpost_v = gpost_ref[...]
        r = lax.rsqrt(jnp.mean(me * me, axis=-1, keepdims=True) + EPS)
        mh = me * r
        dgpost_ref[...] += jnp.sum((dye * mh)[:tm], axis=0, keepdims=True)
        dmh = dye * gpost_v
        dme = (r * (dmh - mh * jnp.mean(dmh * mh, axis=-1, keepdims=True))).astype(BF16)
        dm_ref[...] = dme[:tm]
        dmixed = _nt(dme, wout_ref[...])
        pooled_e = jnp.concatenate([pooled_ref[...], pooledn_ref[...]], axis=0)
        scale_v = scale_ref[...]
        for g, win in enumerate(B_WINDOWS):
            cols = slice(g * B_GROUP_DIM, (g + 1) * B_GROUP_DIM)
            raw = _nn(pooled_e[:, cols], wgrp_ref[g])
            dscale_ref[:, cols] += jnp.sum((dmixed[:, cols] * raw)[:tm], axis=0, keepdims=True)
            draw = (dmixed[:, cols] * scale_v[:, cols]).astype(BF16)
            draw_ref[:, cols] = draw[:tm]
            dpool = _nt(draw, wgrp_ref[g])
            acc = dpool / _window_counts(i * tm, n, win)
            shift = 1
            while shift < win:
                acc = acc + pltpu.roll(acc, n - shift, 0)
                shift *= 2
            dpool_ref[:, cols] = (acc - dpool)[:tm]
        dp = dpool_ref[...].astype(BF16)
        dp_ref[...] = dp
        dh1 = _nt(dp, win_ref[...])
        dxp, dgpre = _rms_bwd(x_ref[...], gpre_ref[...], dh1)
        dgpre_ref[...] += dgpre
        dx_ref[...] = dy + dxp

    vec = lambda w: _resident((1, w), (0, 0))
    acc = lambda shape: pl.BlockSpec(shape, lambda i: (0,) * len(shape))
    per = tm // HALO
    nxt = lambda i: (jnp.minimum((i + 1) * per, t // HALO - 1), 0)
    return _call(
        body, name=name, grid=(steps,),
        in_specs=[_rows(tm, d), pl.BlockSpec((HALO, d), nxt), _rows(tm, d), pl.BlockSpec((HALO, d), nxt), _rows(tm, d),
                  _rows(tm, d), pl.BlockSpec((HALO, d), nxt), vec(d), _wspec(lay.b_rows, d),
                  _resident((ngrp, B_GROUP_DIM, B_GROUP_DIM), (0, 0, 0)), vec(d), _wspec(lay.b_rows, d),
                  vec(d), ANY],
        out_specs=[_rows(tm, d)] * 4 + [acc((1, d))] * 3,
        out_shape=[jax.ShapeDtypeStruct((t, d), F32), jax.ShapeDtypeStruct((t, d), BF16),
                   jax.ShapeDtypeStruct((t, d), BF16), jax.ShapeDtypeStruct((t, d), BF16)]
                  + [jax.ShapeDtypeStruct((1, d), F32)] * 3,
        scratch_shapes=[pltpu.VMEM((tm, d), F32)],
        compiler_params=_params("arbitrary"),
    )(dx1, dx1, m, m, x, pooled, pooled, gpre, wg[0], wgrp, scale, wg[1], gpost, after)


def _f_fwd(x1, gpre, wg, lay, l, gpost, tm, name, target=None):
    t, d = x1.shape
    hid = N_DEV * lay.ffn_rows
    head = target is not None

    def body(x_ref, gpre_ref, wgate_ref, wup_ref, wdown_ref, gpost_ref, *rest):
        x2_ref, h2_ref, a_ref, b_ref, s_ref, f_ref = rest[-7:-1] if head else rest
        xv = x_ref[...]
        h2 = _rms_fwd(xv, gpre_ref[...]).astype(BF16)
        h2_ref[...] = h2
        a = _nt(h2, wgate_ref[...])
        b = _nt(h2, wup_ref[...])
        sig = jax.nn.sigmoid(a)
        silu = a * sig
        a_ref[...] = (b * (sig + silu * (1.0 - sig))).astype(BF16)
        b_ref[...] = silu.astype(BF16)
        s = (silu * b).astype(BF16)
        s_ref[...] = s
        f = _nn(s, wdown_ref[...])
        f_ref[...] = f
        x2 = xv + _rms_fwd(f, gpost_ref[...])
        if head:
            target_ref, loss_ref = rest[0], rest[-1]

            @pl.when(pl.program_id(0) == 0)
            def _():
                loss_ref[...] = jnp.zeros_like(loss_ref)

            diff = x2 - target_ref[...]
            x2_ref[...] = diff * (1.0 / d)
            sq = jnp.sum(jnp.sum(diff * diff, axis=0, keepdims=True), axis=1, keepdims=True)
            loss_ref[...] += sq * (0.5 / d)
        else:
            x2_ref[...] = x2

    vec = lambda w: _resident((1, w), (0, 0))
    return _call(
        body, name=name, grid=(t // tm,),
        in_specs=[_rows(tm, d), vec(d), _wspec(lay.ffn_rows, d), _wspec(lay.ffn_rows, d),
                  _wspec(lay.ffn_rows, d), vec(d)] + ([_rows(tm, d)] if head else []),
        out_specs=[_rows(tm, d), _rows(tm, d), _rows(tm, hid), _rows(tm, hid), _rows(tm, hid), _rows(tm, d)]
                  + ([pl.BlockSpec((8, 128), lambda i: (0, 0))] if head else []),
        out_shape=[jax.ShapeDtypeStruct((t, d), F32), jax.ShapeDtypeStruct((t, d), BF16),
                   jax.ShapeDtypeStruct((t, hid), BF16), jax.ShapeDtypeStruct((t, hid), BF16),
                   jax.ShapeDtypeStruct((t, hid), BF16), jax.ShapeDtypeStruct((t, d), F32)]
                  + ([jax.ShapeDtypeStruct((8, 128), F32)] if head else []),
        compiler_params=_params("arbitrary" if head else "parallel"),
    )(x1, gpre, wg[0], wg[1], wg[2], gpost, *([target] if head else []))


def _f_bwd(dx2, f, x1, a, b, gpre, wg, lay, l, gpost, after, tm, name):
    t, d = x1.shape
    hid = N_DEV * lay.ffn_rows

    def body(dx2_ref, f_ref, x_ref, a_ref, b_ref, gpre_ref, wgate_ref, wup_ref, wdown_ref, gpost_ref, after_ref,
             dx1_ref, df_ref, da_ref, db_ref, dgpost_ref, dgpre_ref):
        @pl.when(pl.program_id(0) == 0)
        def _():
            dgpost_ref[...] = jnp.zeros_like(dgpost_ref)
            dgpre_ref[...] = jnp.zeros_like(dgpre_ref)

        def put_df(rows, dx):
            df_ref[rows, :] = dx.astype(BF16)

        dgpost_ref[...] += _rms_bwd_slabs(lambda rows: f_ref[rows, :], lambda rows: dx2_ref[rows, :], gpost_ref[...],
                                          tm, tm, put_df)
        ds = _nt(df_ref[...], wdown_ref[...])
        da_ref[...] = (ds * a_ref[...].astype(F32)).astype(BF16)
        db_ref[...] = (ds * b_ref[...].astype(F32)).astype(BF16)
        dh2 = _nn(da_ref[...], wgate_ref[...]) + _nn(db_ref[...], wup_ref[...])

        def put_dx(rows, dx):
            dx1_ref[rows, :] = dx2_ref[rows, :] + dx

        dgpre_ref[...] += _rms_bwd_slabs(lambda rows: x_ref[rows, :], lambda rows: dh2[rows, :], gpre_ref[...],
                                         tm, tm, put_dx)

    vec = lambda w: _resident((1, w), (0, 0))
    acc = pl.BlockSpec((1, d), lambda i: (0, 0))
    return _call(
        body, name=name, grid=(t // tm,),
        in_specs=[_rows(tm, d), _rows(tm, d), _rows(tm, d), _rows(tm, hid), _rows(tm, hid), vec(d),
                  _wspec(lay.ffn_rows, d), _wspec(lay.ffn_rows, d),
                  _wspec(lay.ffn_rows, d), vec(d), ANY],
        out_specs=[_rows(tm, d), _rows(tm, d), _rows(tm, hid), _rows(tm, hid), acc, acc],
        out_shape=[jax.ShapeDtypeStruct((t, d), F32), jax.ShapeDtypeStruct((t, d), BF16),
                   jax.ShapeDtypeStruct((t, hid), BF16), jax.ShapeDtypeStruct((t, hid), BF16),
                   jax.ShapeDtypeStruct((1, d), F32), jax.ShapeDtypeStruct((1, d), F32)],
        compiler_params=_params("arbitrary"),
    )(dx2, f, x1, a, b, gpre, wg[0], wg[1], wg[2], gpost, after)


def _grad_into(gbuf, lhs, rhs, off, rows, name, after=None):
    t, m = lhs.shape
    d = rhs.shape[1]
    assert m == N_DEV * rows and off % rows == 0
    per_tile = {352: 4, 512: 2, 256: 4, 128: 8}[rows]
    tm = per_tile * rows
    assert tm % 128 == 0 and rows % 16 == 0
    tk = 2048 if t % 2048 == 0 else 256
    ksteps = t // tk
    fresh = isinstance(gbuf, int)
    shape = (N_DEV, gbuf, d) if fresh else gbuf.shape
    extra = ([] if fresh else [gbuf]) + ([] if after is None else [after])

    def body(l_ref, r_ref, *rest):
        o_ref, acc_ref = rest[-2:]
        k = pl.program_id(1)

        @pl.when(k == 0)
        def _():
            acc_ref[...] = jnp.zeros_like(acc_ref)

        acc_ref[...] += _tn(l_ref[...], r_ref[...])

        @pl.when(k == ksteps - 1)
        def _():
            o_ref[...] = acc_ref[...].reshape(per_tile, rows, d).astype(BF16)

    return _call(
        body, name=name, grid=(N_DEV // per_tile, ksteps),
        in_specs=[pl.BlockSpec((tk, tm), lambda i, k: (k, i)), pl.BlockSpec((tk, d), lambda i, k: (k, 0))]
                 + [ANY] * len(extra),
        out_specs=pl.BlockSpec((per_tile, rows, d), lambda i, k: (i, off // rows, 0)),
        out_shape=jax.ShapeDtypeStruct(shape, BF16),
        scratch_shapes=[pltpu.VMEM((tm, d), F32)],
        input_output_aliases={} if fresh else {2: 0},
        compiler_params=_params("parallel", "arbitrary"),
    )(lhs, rhs, *extra)


def _grad_grouped(pooled, draw, name):
    t, d = pooled.shape
    ngrp = len(B_WINDOWS)
    tk = 1024 if t % 1024 == 0 else 256

    def body(p_ref, q_ref, o_ref):
        @pl.when(pl.program_id(0) == 0)
        def _():
            o_ref[...] = jnp.zeros_like(o_ref)

        for g in range(ngrp):
            cols = slice(g * B_GROUP_DIM, (g + 1) * B_GROUP_DIM)
            o_ref[g] += _tn(p_ref[:, cols], q_ref[:, cols])

    return _call(
        body, name=name, grid=(t // tk,),
        in_specs=[_rows(tk, d), _rows(tk, d)],
        out_specs=pl.BlockSpec((ngrp, B_GROUP_DIM, B_GROUP_DIM), lambda i: (0, 0, 0)),
        out_shape=jax.ShapeDtypeStruct((ngrp, B_GROUP_DIM, B_GROUP_DIM), F32),
        compiler_params=_params("arbitrary"),
    )(pooled, draw)


def _peers():
    x, y, c = lax.axis_index("x"), lax.axis_index("y"), lax.axis_index("c")
    flip = lambda v, f: 1 - v if f else v
    peers = []
    for r in range(1, N_DEV):
        px, py, pc = flip(x, r & 4), flip(y, r & 2), flip(c, r & 1)
        peers.append(((px, py, pc), 4 * px + 2 * py + pc))
    return 4 * x + 2 * y + c, peers


HBM = pl.BlockSpec(memory_space=pltpu.HBM)
SEM = pl.BlockSpec(memory_space=pltpu.SEMAPHORE)
EFFECT = pltpu.SideEffectType.DATAFLOW_SIDE_EFFECTING


def _peer_copies(scatter, srcs, lands, send_sems, recv_sems):
    me, peers = _peers()
    copies = []
    for a in range(len(srcs)):
        rows = srcs[a].shape[0]
        block = lambda k: lands[a].at[pl.ds(pl.multiple_of(k * rows, 8), rows)]
        for r, (peer, pidx) in enumerate(peers):
            src = srcs[a].at[pidx] if scatter else srcs[a]
            mine = lands[a].at[r] if scatter else block(pidx)
            theirs = lands[a].at[r] if scatter else block(me)
            send = pltpu.make_async_remote_copy(src_ref=src, dst_ref=theirs, send_sem=send_sems[a].at[r],
                                                recv_sem=recv_sems[a].at[r], device_id=peer, device_id_type=MESH)
            recv = pltpu.make_async_remote_copy(src_ref=src, dst_ref=mine, send_sem=send_sems[a].at[r],
                                                recv_sem=recv_sems[a].at[r], device_id=peer, device_id_type=MESH)
            copies.append((send, recv))
    return copies


def _own_copies(srcs, lands, send_sems):
    me, _ = _peers()
    copies = []
    for a in range(len(srcs)):
        rows = srcs[a].shape[0]
        copies.append(pltpu.make_async_copy(srcs[a], lands[a].at[pl.ds(pl.multiple_of(me * rows, 8), rows)],
                                            send_sems[a].at[N_DEV - 1]))
    return copies


def _exchange_start(scatter, srcs, lands, after, name):
    n = len(srcs)

    def body(*refs):
        src_refs, land_refs = refs[:n], refs[n:2 * n]
        outs = refs[2 * n + 1:]
        send_sems, recv_sems, token = outs[:n], outs[n:2 * n], outs[-1]
        for send, _ in _peer_copies(scatter, src_refs, land_refs, send_sems, recv_sems):
            send.start()
        if not scatter:
            for own in _own_copies(src_refs, land_refs, send_sems):
                own.start()
        token[...] = jnp.zeros_like(token)

    hbm = lambda a: pltpu.with_memory_space_constraint(a, pltpu.HBM)
    res = _call(
        body, name=name,
        in_specs=[HBM] * (2 * n) + [ANY],
        out_specs=[SEM] * (2 * n) + [HBM] * (2 * n) + [pl.BlockSpec(memory_space=pltpu.VMEM)],
        out_shape=[pltpu.SemaphoreType.DMA((N_DEV,))] * (2 * n)
                  + [pltpu.HBM(a.shape, a.dtype) for a in list(srcs) + list(lands)]
                  + [jax.ShapeDtypeStruct((8, 128), F32)],
        input_output_aliases={i: 2 * n + i for i in range(2 * n)},
        compiler_params=pltpu.CompilerParams(has_side_effects=EFFECT),
    )(*[hbm(a) for a in srcs], *[hbm(a) for a in lands], after)
    return res[:n], res[n:2 * n], res[2 * n:3 * n], res[3 * n:4 * n], res[-1]


def _exchange_wait(scatter, send_sems, recv_sems, srcs, lands, after, name):
    n = len(srcs)
    after = list(after) if isinstance(after, (list, tuple)) else [after]

    def body(*refs):
        src_refs, land_refs = refs[:n], refs[n:2 * n]
        send_refs, recv_refs = refs[2 * n:3 * n], refs[3 * n:4 * n]
        for send, recv in _peer_copies(scatter, src_refs, land_refs, send_refs, recv_refs):
            send.wait_send()
            recv.wait_recv()
        if not scatter:
            for own in _own_copies(src_refs, land_refs, send_refs):
                own.wait()

    res = _call(
        body, name=name,
        in_specs=[HBM] * (2 * n) + [SEM] * (2 * n) + [ANY] * len(after),
        out_specs=[HBM] * (2 * n),
        out_shape=[pltpu.HBM(a.shape, a.dtype) for a in list(srcs) + list(lands)],
        input_output_aliases={i: i for i in range(2 * n)},
        compiler_params=pltpu.CompilerParams(has_side_effects=EFFECT),
    )(*srcs, *lands, *send_sems, *recv_sems, *after)
    return res[:n], res[n:]


def _row_tile(rows):
    if rows <= 512:
        return rows
    return max([tr for tr in range(16, 513, 16) if rows % tr == 0] or [rows])


def _sum_parts(own, got, me, name):
    _, rows, w = own.shape
    tr = _row_tile(rows)

    def body(me_ref, a_ref, b_ref, o_ref):
        s = a_ref[...].astype(F32)
        for j in range(N_DEV - 1):
            s = s + b_ref[j].astype(F32)
        o_ref[...] = s

    return _call(
        body, name=name,
        grid_spec=pltpu.PrefetchScalarGridSpec(
            num_scalar_prefetch=1, grid=(rows // tr,),
            in_specs=[pl.BlockSpec((None, tr, w), lambda i, me_ref: (me_ref[0], i, 0)),
                      pl.BlockSpec((N_DEV - 1, tr, w), lambda i, me_ref: (0, i, 0))],
            out_specs=pl.BlockSpec((tr, w), lambda i, me_ref: (i, 0))),
        out_shape=jax.ShapeDtypeStruct((rows, w), F32),
        compiler_params=_params("parallel"),
    )(me, own, got)


def _sum_devices(stacked, name):
    k, rows, w = stacked.shape
    tr = _row_tile(rows)

    def body(a_ref, o_ref):
        s = a_ref[0]
        for j in range(1, k):
            s = s + a_ref[j]
        o_ref[...] = s

    return _call(
        body, name=name, grid=(rows // tr,),
        in_specs=[pl.BlockSpec((k, tr, w), lambda i: (0, i, 0))],
        out_specs=pl.BlockSpec((tr, w), lambda i: (i, 0)),
        out_shape=jax.ShapeDtypeStruct((rows, w), F32),
        compiler_params=_params("parallel"),
    )(stacked)


def _adamw(w, g, m, v, name):
    rows, cols = w.shape
    tr = _row_tile(rows)

    def body(w_ref, g_ref, m_ref, v_ref, d_ref, nm_ref, nv_ref):
        gv = g_ref[...]
        nm = ADAM_B1 * m_ref[...] + (1.0 - ADAM_B1) * gv
        nv = ADAM_B2 * v_ref[...] + (1.0 - ADAM_B2) * (gv * gv)
        m_hat = nm / (1.0 - ADAM_B1 ** ADAM_STEP)
        v_hat = nv / (1.0 - ADAM_B2 ** ADAM_STEP)
        d_ref[...] = -ADAM_LR * (m_hat / (jnp.sqrt(v_hat) + ADAM_EPS) + ADAM_WD * w_ref[...])
        nm_ref[...] = nm
        nv_ref[...] = nv

    spec = pl.BlockSpec((tr, cols), lambda i: (i, 0))
    return _call(
        body, name=name, grid=(rows // tr,),
        in_specs=[spec] * 4, out_specs=[spec] * 3,
        out_shape=[jax.ShapeDtypeStruct((rows, cols), F32)] * 3,
        compiler_params=_params("parallel"),
    )(w, g, m, v)


SMALL = ("a_ln_g", "a_ln_b", "a_w_s", "a_b_s", "mix_pre_g", "mix_post_g", "ffn_pre_g", "ffn_post_g")


def _pack_small(parts, d, last_row=None):
    rows = [parts[k].reshape(-1, d) for k in SMALL] + ([] if last_row is None else [last_row])
    flat = jnp.concatenate(rows, axis=0)
    return jnp.pad(flat, ((0, -flat.shape[0] % 8), (0, 0)))


def _unpack_small(flat, like):
    out, r = {}, 0
    for k in SMALL:
        n = like[k].size // flat.shape[1]
        out[k] = flat[r:r + n].reshape(like[k].shape)
        r += n
    return out


def kernel(x, a_w_in, a_ln_g, a_ln_b, a_w_s, a_b_s, a_w_out, b_w_in, b_w_grp, b_scale, b_w_out, mix_pre_g, mix_post_g, ffn_pre_g, ffn_post_g, ffn_w_gate, ffn_w_up, ffn_w_down, loss_target, m_a_w_in, m_a_ln_g, m_a_ln_b, m_a_w_s, m_a_b_s, m_a_w_out, m_b_w_in, m_b_w_grp, m_b_scale, m_b_w_out, m_mix_pre_g, m_mix_post_g, m_ffn_pre_g, m_ffn_post_g, m_ffn_w_gate, m_ffn_w_up, m_ffn_w_down, v_a_w_in, v_a_ln_g, v_a_ln_b, v_a_w_s, v_a_b_s, v_a_w_out, v_b_w_in, v_b_w_grp, v_b_scale, v_b_w_out, v_mix_pre_g, v_mix_post_g, v_ffn_pre_g, v_ffn_post_g, v_ffn_w_gate, v_ffn_w_up, v_ffn_w_down):
    args = dict(locals())
    names = ("a_w_in", "a_ln_g", "a_ln_b", "a_w_s", "a_b_s", "a_w_out", "b_w_in", "b_w_grp", "b_scale", "b_w_out",
             "mix_pre_g", "mix_post_g", "ffn_pre_g", "ffn_post_g", "ffn_w_gate", "ffn_w_up", "ffn_w_down")
    w = {k: args[k] for k in names}
    mom = {k: args["m_" + k] for k in names}
    var = {k: args["v_" + k] for k in names}

    t, d = x.shape[1], x.shape[2]
    ffn_local = ffn_w_gate.shape[2]
    lay = _Layout(d, ffn_local)
    me = 4 * lax.axis_index("x") + 2 * lax.axis_index("y") + lax.axis_index("c")
    me1 = jnp.reshape(me, (1,)).astype(jnp.int32)

    def landing(block):
        return lax.empty((N_DEV * block.shape[0],) + block.shape[1:], block.dtype)

    def shards(i, mixer, zero):
        j = i // 2
        if not mixer:
            parts = [ffn_w_gate[i].T, ffn_w_up[i].T, ffn_w_down[i]]
        elif i % 2 == 0:
            parts = [a_w_in[j].T, a_w_out[j]]
        else:
            parts = [b_w_in[j], b_w_out[j]]
        return [(p + zero).astype(BF16) for p in parts]

    nsub = 2 * DEPTH
    wg = [None] * nsub
    first = shards(0, True, 0.0)
    first = _exchange_start(False, first, [landing(b) for b in first], jnp.zeros((8, 128), F32), "gather_first_start")
    zero = first[4][0, 0]
    ngrp = len(B_WINDOWS)
    grp_local = b_w_grp.shape[2]
    sdev = b_scale.shape[1]
    side_rows = 2 * ngrp * grp_local
    side = jnp.concatenate(
        [b_w_grp.reshape(side_rows, B_GROUP_DIM),
         jnp.pad(b_scale, ((0, 6), (0, B_GROUP_DIM - sdev)))], axis=0) + zero
    later, where = [side], [slice(0, 1)]
    for k in range(1, nsub):
        new = shards(k // 2, k % 2 == 0, zero)
        where.append(slice(len(later), len(later) + len(new)))
        later += new
    send_sems, recv_sems, later, zones, token = _exchange_start(
        False, later, [landing(b) for b in later], first[4], "gather_start")
    _, wg[0] = _exchange_wait(False, *first[:4], token, "gather_first_wait")

    def gathered(k, after):
        s = where[k]
        _, got = _exchange_wait(False, send_sems[s], recv_sems[s], later[s], zones[s], after, f"gather_wait_{k}")
        return got

    row = lambda a: a.reshape(1, -1)
    bst = jnp.transpose(a_b_s, (0, 2, 1))

    tm = 256 if t % 256 == 0 else CHUNK
    tm_abwd = tm
    tm_b = 512 if t % 512 == 0 else tm

    saved = []
    h = x[0]
    wgrp_full = scale_full = None
    for i in range(DEPTH):
        j = i // 2
        gpre = row(mix_pre_g[i])
        if i > 0:
            wg[2 * i] = gathered(2 * i, h)
        if i % 2 == 0:
            x1, h1, gp, u, vh, rs, gated, m = _a_fwd(h, gpre, wg[2 * i], lay, j, row(a_ln_g[j]), row(a_ln_b[j]),
                                                     a_w_s[j], bst[j], row(mix_post_g[i]), tm, f"a_fwd_{j}")
            mix = dict(h1=h1, gp=gp, u=u, vh=vh, rs=rs, gated=gated, m=m)
        else:
            if wgrp_full is None:
                side_g = gathered(0, h)[0].reshape(N_DEV, side_rows + 8, B_GROUP_DIM)
                wgrp_full = (side_g[:, :side_rows].reshape(N_DEV, 2, ngrp, grp_local, B_GROUP_DIM)
                             .transpose(1, 2, 0, 3, 4).reshape(2, ngrp, B_GROUP_DIM, B_GROUP_DIM).astype(BF16))
                scale_full = (side_g[:, side_rows:side_rows + 2, :sdev].transpose(1, 0, 2)
                              .reshape(2, 1, N_DEV * sdev))
            x1, h1, pooled, mixed, m = _b_fwd(h, gpre, wg[2 * i], lay, j, wgrp_full[j], scale_full[j],
                                              row(mix_post_g[i]), tm_b, f"b_fwd_{j}")
            mix = dict(h1=h1, pooled=pooled, mixed=mixed, m=m)
        wg[2 * i + 1] = gathered(2 * i + 1, x1)
        x2, h2, a, b, s, f, *loss_acc = _f_fwd(x1, row(ffn_pre_g[i]), wg[2 * i + 1], lay, i, row(ffn_post_g[i]), tm,
                                               f"f_fwd_{i}", loss_target[0] if i == DEPTH - 1 else None)
        saved.append(dict(x=h, x1=x1, mix=mix, h2=h2, a=a, b=b, s=s, f=f))
        h = x2
    dy, (loss_acc,) = h, loss_acc

    small_g = {k: [None] * w[k].shape[0] for k in SMALL}
    dgrp, dscale = [None, None], [None, None]
    pending = [None] * nsub
    token = jnp.zeros((8, 128), F32)

    def scatter(k, gbuf):
        got = pltpu.with_memory_space_constraint(lax.empty((N_DEV - 1,) + gbuf.shape[1:], gbuf.dtype), pltpu.HBM)
        ss, rs, src, zone, tok = _exchange_start(True, [gbuf], [got], token, f"scatter_start_{k}")
        pending[k] = (ss, rs, src, zone)
        return tok

    def small_exchanges():
        side_grad = jnp.concatenate(
            [jnp.stack(dgrp).reshape(2, ngrp, N_DEV, grp_local, B_GROUP_DIM).transpose(2, 0, 1, 3, 4)
             .reshape(N_DEV, side_rows, B_GROUP_DIM),
             jnp.pad(jnp.stack(dscale).reshape(2, N_DEV, sdev).transpose(1, 0, 2),
                     ((0, 0), (0, 6), (0, B_GROUP_DIM - sdev)))], axis=1)
        small_part = _pack_small({k: jnp.stack(small_g[k]) for k in SMALL}, d,
                                 jnp.broadcast_to(loss_acc[:1, :1], (1, d)))
        got = pltpu.with_memory_space_constraint(lax.empty((N_DEV - 1,) + side_grad.shape[1:], F32), pltpu.HBM)
        side_x = _exchange_start(True, [side_grad], [got], token, "side_scatter_start")
        small_x = _exchange_start(False, [small_part], [landing(small_part)], side_x[4], "small_gather_start")
        return side_x[:4], small_x[:4], small_x[4]

    for i in reversed(range(DEPTH)):
        sv = saved[i]
        j = i // 2
        wf, wm = wg[2 * i + 1], wg[2 * i]
        dx1, df, da, db, dgpost, dgpre = _f_bwd(dy, sv["f"], sv["x1"], sv["a"], sv["b"], row(ffn_pre_g[i]), wf, lay, i,
                                                 row(ffn_post_g[i]), token, tm, f"f_bwd_{i}")
        small_g["ffn_post_g"][i], small_g["ffn_pre_g"][i] = dgpost[0], dgpre[0]
        gbuf = _grad_into(lay.f_total, da, sv["h2"], lay.gate[i], lay.ffn_rows, f"g_gate_{i}")
        gbuf = _grad_into(gbuf, db, sv["h2"], lay.up[i], lay.ffn_rows, f"g_up_{i}")
        gbuf = _grad_into(gbuf, sv["s"], df, lay.down[i], lay.ffn_rows, f"g_down_{i}")
        token = scatter(2 * i + 1, gbuf)
        mix = sv["mix"]
        gpost = row(mix_post_g[i])
        if i % 2 == 0:
            dx, dm, dz, dgpost, dgpre, dlng, dlnb, dws, dbt = _a_bwd(
                dx1, mix["m"], sv["x"], mix["gp"], mix["u"], mix["vh"], mix["rs"], row(mix_pre_g[i]), wm, lay, j,
                row(a_ln_g[j]), row(a_ln_b[j]), a_w_s[j], bst[j], gpost, token, tm_abwd, f"a_bwd_{j}")
            small_g["a_ln_g"][j], small_g["a_ln_b"][j] = dlng[0], dlnb[0]
            small_g["a_w_s"][j], small_g["a_b_s"][j] = dws, dbt[:, :A_GROUPS].T
            small_g["mix_post_g"][i], small_g["mix_pre_g"][i] = dgpost[0], dgpre[0]
            order = None
            if i == 0:
                side_x, small_x, order = small_exchanges()
            gbuf = _grad_into(lay.a_total, dz, mix["h1"], lay.a_in[j], lay.a_in_rows, f"g_a_in_{j}", after=order)
            gbuf = _grad_into(gbuf, mix["gated"], dm, lay.a_out[j], lay.a_out_rows, f"g_a_out_{j}")
        else:
            dx, dm, draw, dp, dgpost, dgpre, dsc = _b_bwd(
                dx1, mix["m"], sv["x"], mix["pooled"], row(mix_pre_g[i]), wm, lay, j, wgrp_full[j], scale_full[j],
                gpost, token, tm_b, f"b_bwd_{j}")
            dscale[j] = dsc[0]
            dgrp[j] = _grad_grouped(mix["pooled"], draw, f"g_b_grp_{j}")
            gbuf = _grad_into(lay.b_total, mix["h1"], dp, lay.b_in[j], lay.b_rows, f"g_b_in_{j}")
            gbuf = _grad_into(gbuf, mix["mixed"], dm, lay.b_out[j], lay.b_rows, f"g_b_out_{j}")
            small_g["mix_post_g"][i], small_g["mix_pre_g"][i] = dgpost[0], dgpre[0]
        token = scatter(2 * i, gbuf)
        dy = dx
    grad_x = dy[None]

    g_sub = [None] * nsub

    def arrived(k, after):
        ss, rs, src, zone = pending[k]
        (own,), (got,) = _exchange_wait(True, ss, rs, src, zone, after, f"scatter_wait_{k}")
        g_sub[k] = _sum_parts(own, got, me1, f"sum_grads_{k}")

    def rows_of(k, off, n):
        return g_sub[k][off:off + n]

    grads, delta, new_m, new_v = {}, {}, {}, {}

    def update(k):
        turn = (lambda a: jnp.swapaxes(a, 1, 2)) if k in ("a_w_in", "ffn_w_gate", "ffn_w_up") else (lambda a: a)
        shape = turn(w[k]).shape
        two = lambda a: a.reshape(-1, shape[-1])
        dl, nm, nv = _adamw(two(turn(w[k])), two(grads[k]), two(turn(mom[k])), two(turn(var[k])), f"adamw_{k}")
        delta[k], new_m[k], new_v[k] = (turn(a.reshape(shape)) for a in (dl, nm, nv))
        grads[k] = turn(grads[k])

    for k in range(1, nsub):
        arrived(k, token)
    grads["ffn_w_gate"] = jnp.stack([rows_of(2 * l + 1, lay.gate[l], ffn_local) for l in range(DEPTH)])
    grads["ffn_w_up"] = jnp.stack([rows_of(2 * l + 1, lay.up[l], ffn_local) for l in range(DEPTH)])
    grads["ffn_w_down"] = jnp.stack([rows_of(2 * l + 1, lay.down[l], ffn_local) for l in range(DEPTH)])
    grads["b_w_in"] = jnp.stack([rows_of(4 * j + 2, lay.b_in[j], lay.b_rows) for j in range(2)])
    grads["b_w_out"] = jnp.stack([rows_of(4 * j + 2, lay.b_out[j], lay.b_rows) for j in range(2)])
    early = ("ffn_w_gate", "ffn_w_up", "ffn_w_down", "b_w_in", "b_w_out")
    for k in early:
        update(k)

    (side_own,), (side_got,) = _exchange_wait(True, *side_x, [delta[k] for k in early], "side_scatter_wait")
    g_side = _sum_parts(side_own, side_got, me1, "sum_side")
    grads["b_w_grp"] = g_side[:side_rows].reshape(b_w_grp.shape)
    grads["b_scale"] = g_side[side_rows:side_rows + 2, :sdev]
    update("b_w_grp")
    update("b_scale")
    _, (small_all,) = _exchange_wait(False, *small_x, [delta["b_w_grp"], delta["b_scale"]], "small_gather_wait")
    small_sum = _sum_devices(small_all.reshape(N_DEV, -1, d), "sum_small")
    g_small = _unpack_small(small_sum, w)
    loss = small_sum[sum(w[k].size for k in SMALL) // d, 0]
    grads.update(g_small)
    dl, nm, nv = _adamw(_pack_small(w, d), _pack_small(g_small, d), _pack_small(mom, d), _pack_small(var, d),
                        "adamw_small")
    delta.update(_unpack_small(dl, w))
    new_m.update(_unpack_small(nm, w))
    new_v.update(_unpack_small(nv, w))

    arrived(0, dl)
    grads["a_w_in"] = jnp.stack([rows_of(4 * j, lay.a_in[j], lay.a_in_rows) for j in range(2)])
    grads["a_w_out"] = jnp.stack([rows_of(4 * j, lay.a_out[j], lay.a_out_rows) for j in range(2)])
    update("a_w_in")
    update("a_w_out")

    return (loss, grad_x, *[grads[k] for k in names], *[delta[k] for k in names], *[new_m[k] for k in names],
            *[new_v[k] for k in names])
```

```python
import math

import jax
import jax.numpy as jnp
from jax import lax
from jax.experimental import pallas as pl
from jax.experimental.pallas import tpu as pltpu

F32 = jnp.float32
BF16 = jnp.bfloat16
MESH = pl.DeviceIdType.MESH
ANY = pl.BlockSpec(memory_space=pl.ANY)

N_DEV = 8
EPS = 1e-6
CHUNK = 128
A_GROUPS = 8
A_GROUP_DIM = 256
B_WINDOWS = (2, 4, 8, 16)
B_GROUP_DIM = 256
HALO = 16
DEPTH = 4

ADAM_LR = 0.001
ADAM_B1 = 0.9
ADAM_B2 = 0.999
ADAM_EPS = 1e-08
ADAM_WD = 0.01
ADAM_STEP = 10

VMEM_LIMIT_BYTES = 60 * 1024 * 1024

ERF_P = 0.3275911
ERF_A = (0.254829592, -0.284496736, 1.421413741, -1.453152027, 1.061405429)
INV_SQRT2 = 1.0 / math.sqrt(2.0)
LOG2_E = 1.0 / math.log(2.0)
INV_SQRT_2PI = 1.0 / math.sqrt(2.0 * math.pi)


def _call(body, **kw):
    return pl.pallas_call(body, **kw)


def _params(*semantics):
    return pltpu.CompilerParams(dimension_semantics=semantics or None, vmem_limit_bytes=VMEM_LIMIT_BYTES)


def _resident(shape, index):
    return pl.BlockSpec(shape, lambda *_: index, pipeline_mode=pl.Buffered(1))


def _rows(tm, width):
    return pl.BlockSpec((tm, width), lambda i: (i, 0))


def _nn(a, b):
    return jnp.dot(a, b, preferred_element_type=F32)


def _nt(a, b):
    return lax.dot_general(a, b, (((1,), (1,)), ((), ())), preferred_element_type=F32)


def _tn(a, b):
    return lax.dot_general(a, b, (((0,), (0,)), ((), ())), preferred_element_type=F32)


def _rms_fwd(x, g):
    r = lax.rsqrt(jnp.mean(x * x, axis=-1, keepdims=True) + EPS)
    return x * r * g


def _rms_bwd(x, g, dy):
    r = lax.rsqrt(jnp.mean(x * x, axis=-1, keepdims=True) + EPS)
    xh = x * r
    dg = jnp.sum(dy * xh, axis=0, keepdims=True)
    dxh = dy * g
    dx = r * (dxh - xh * jnp.mean(dxh * xh, axis=-1, keepdims=True))
    return dx, dg


SLAB = 16


def _slabs(n):
    return [slice(r, r + SLAB) for r in range(0, n, SLAB)]


def _rms_bwd_slabs(x_at, dy_at, g, n, n_sum, emit):
    acc = jnp.zeros((8, g.shape[1]), F32)
    for rows in _slabs(n):
        x = x_at(rows)
        dy = dy_at(rows)
        r = lax.rsqrt(jnp.mean(x * x, axis=-1, keepdims=True) + EPS)
        xh = x * r
        if rows.start < n_sum:
            p = dy * xh
            acc = acc + p[:8] + p[8:]
        dxh = dy * g
        emit(rows, r * (dxh - xh * jnp.mean(dxh * xh, axis=-1, keepdims=True)))
    return jnp.sum(acc, axis=0, keepdims=True)


def _gelu(z):
    phi = 0.5 + 0.5 * lax.erf(z * INV_SQRT2)
    e = jnp.exp2(z * z * (-0.5 * LOG2_E))
    return z * phi, phi + z * e * INV_SQRT_2PI


def _layernorm_stats(v):
    mu = jnp.mean(v, axis=-1, keepdims=True)
    xc = v - mu
    rs = lax.rsqrt(jnp.mean(xc * xc, axis=-1, keepdims=True) + EPS)
    return xc * rs, rs


def _tril_mask():
    r = lax.broadcasted_iota(jnp.int32, (CHUNK, CHUNK), 0)
    c = lax.broadcasted_iota(jnp.int32, (CHUNK, CHUNK), 1)
    return r >= c


class _Layout:
    def __init__(self, d, ffn_rows):
        self.ffn_rows = ffn_rows
        self.gate, self.up, self.down = [0] * DEPTH, [self.ffn_rows] * DEPTH, [2 * self.ffn_rows] * DEPTH
        self.f_total = 3 * self.ffn_rows
        self.a_in_rows, self.a_out_rows, self.b_rows = 4 * d // N_DEV, 2 * d // N_DEV, d // N_DEV
        self.a_in, self.a_out = [0, 0], [self.a_in_rows] * 2
        self.a_total = self.a_in_rows + self.a_out_rows
        self.b_in, self.b_out = [0, 0], [self.b_rows] * 2
        self.b_total = 2 * self.b_rows


def _wspec(rows, d):
    return _resident((N_DEV * rows, d), (0, 0))


def _a_fwd(x, gpre, wg, lay, j, lng, lnb, ws, bst, gpost, tm, name):
    t, d = x.shape
    aw = 2 * d
    nch = tm // CHUNK

    def body(x_ref, gpre_ref, win_ref, lng_ref, lnb_ref, ws_ref, bst_ref, wout_ref, gpost_ref,
             x1_ref, h1_ref, gp_ref, u_ref, vh_ref, rs_ref, gated_ref, m_ref):
        xv = x_ref[...]
        h1 = _rms_fwd(xv, gpre_ref[...]).astype(BF16)
        h1_ref[...] = h1
        z = _nt(h1, win_ref[...])
        u, du_dz = _gelu(z[:, :aw])
        v, dv_dz = _gelu(z[:, aw:])
        gp_ref[:, :aw] = du_dz.astype(BF16)
        gp_ref[:, aw:] = dv_dz.astype(BF16)
        u_ref[...] = u.astype(BF16)
        vh, rs = _layernorm_stats(v)
        vh_ref[...] = vh.astype(BF16)
        rs_ref[...] = jnp.broadcast_to(rs, rs_ref.shape)
        vn = (vh * lng_ref[...] + lnb_ref[...]).astype(BF16)
        mask = _tril_mask()
        for g in range(A_GROUPS):
            wm = jnp.where(mask, ws_ref[g], 0.0).astype(BF16)
            cols = slice(g * A_GROUP_DIM, (g + 1) * A_GROUP_DIM)
            for c in range(nch):
                rows = slice(c * CHUNK, (c + 1) * CHUNK)
                sv = _nn(wm, vn[rows, cols]) + bst_ref[:, g:g + 1]
                gated_ref[rows, cols] = (u[rows, cols] * sv).astype(BF16)
        m = _nn(gated_ref[...], wout_ref[...])
        m_ref[...] = m
        x1_ref[...] = xv + _rms_fwd(m, gpost_ref[...])

    vec = lambda w: _resident((1, w), (0, 0))
    return _call(
        body, name=name, grid=(t // tm,),
        in_specs=[_rows(tm, d), vec(d), _wspec(lay.a_in_rows, d), vec(aw), vec(aw),
                  _resident((A_GROUPS, CHUNK, CHUNK), (0, 0, 0)), _resident((CHUNK, A_GROUPS), (0, 0)),
                  _wspec(lay.a_out_rows, d), vec(d)],
        out_specs=[_rows(tm, d), _rows(tm, d), _rows(tm, 2 * aw), _rows(tm, aw), _rows(tm, aw), _rows(tm, 128),
                   _rows(tm, aw), _rows(tm, d)],
        out_shape=[jax.ShapeDtypeStruct((t, d), F32), jax.ShapeDtypeStruct((t, d), BF16),
                   jax.ShapeDtypeStruct((t, 2 * aw), BF16), jax.ShapeDtypeStruct((t, aw), BF16),
                   jax.ShapeDtypeStruct((t, aw), BF16), jax.ShapeDtypeStruct((t, 128), F32),
                   jax.ShapeDtypeStruct((t, aw), BF16), jax.ShapeDtypeStruct((t, d), F32)],
        compiler_params=_params("parallel"),
    )(x, gpre, wg[0], lng, lnb, ws, bst, wg[1], gpost)


def _a_bwd(dx1, m, x, gp, u, vh, rs, gpre, wg, lay, j, lng, lnb, ws, bst, gpost, after, tm, name):
    t, d = x.shape
    aw = 2 * d
    nch = tm // CHUNK

    def body(dx1_ref, m_ref, x_ref, gp_ref, u_ref, vh_ref, rs_ref, gpre_ref, win_ref, lng_ref, lnb_ref, ws_ref, bst_ref,
             wout_ref, gpost_ref, after_ref,
             dx_ref, dm_ref, dz_ref, dgpost_ref, dgpre_ref, dlng_ref, dlnb_ref, dws_ref, dbt_ref, dvn_ref):
        @pl.when(pl.program_id(0) == 0)
        def _():
            for r in (dgpost_ref, dgpre_ref, dlng_ref, dlnb_ref, dws_ref, dbt_ref):
                r[...] = jnp.zeros_like(r)

        def put_dm(rows, dx):
            dm_ref[rows, :] = dx.astype(BF16)

        dgpost_ref[...] += _rms_bwd_slabs(lambda rows: m_ref[rows, :], lambda rows: dx1_ref[rows, :], gpost_ref[...],
                                          tm, tm, put_dm)
        dgated = _nt(dm_ref[...], wout_ref[...])

        vh = vh_ref[...].astype(F32)
        rs = rs_ref[:, :1]
        lng_v = lng_ref[...]
        vn = (vh * lng_v + lnb_ref[...]).astype(BF16)
        mask = _tril_mask()
        lane = lax.broadcasted_iota(jnp.int32, (CHUNK, CHUNK), 1)
        for g in range(A_GROUPS):
            wm = jnp.where(mask, ws_ref[g], 0.0).astype(BF16)
            cols = slice(g * A_GROUP_DIM, (g + 1) * A_GROUP_DIM)
            dws_g = jnp.zeros((CHUNK, CHUNK), F32)
            db_g = jnp.zeros((CHUNK, 1), F32)
            for c in range(nch):
                rows = slice(c * CHUNK, (c + 1) * CHUNK)
                vn_cg = vn[rows, cols]
                sv = _nn(wm, vn_cg) + bst_ref[:, g:g + 1]
                dg_cg = dgated[rows, cols]
                dsv = dg_cg * u_ref[rows, cols].astype(F32)
                dsv_bf = dsv.astype(BF16)
                db_g = db_g + jnp.sum(dsv, axis=1, keepdims=True)
                dws_g = dws_g + _nt(dsv_bf, vn_cg)
                dvn_ref[rows, cols] = _tn(wm, dsv_bf)
                dz_ref[rows, cols] = (dg_cg * sv * gp_ref[rows, cols].astype(F32)).astype(BF16)
            dws_ref[g] += jnp.where(mask, dws_g, 0.0)
            dbt_ref[...] += jnp.where(lane == g, db_g, 0.0)
        dvn = dvn_ref[...]
        dlng_ref[...] += jnp.sum(dvn * vh, axis=0, keepdims=True)
        dlnb_ref[...] += jnp.sum(dvn, axis=0, keepdims=True)
        dvh = dvn * lng_v
        dv = rs * (dvh - jnp.mean(dvh, axis=-1, keepdims=True) - vh * jnp.mean(dvh * vh, axis=-1, keepdims=True))
        dz_ref[:, aw:] = (dv * gp_ref[:, aw:].astype(F32)).astype(BF16)
        dh1 = _nn(dz_ref[...], win_ref[...])

        def put_dx(rows, dx):
            dx_ref[rows, :] = dx1_ref[rows, :] + dx

        dgpre_ref[...] += _rms_bwd_slabs(lambda rows: x_ref[rows, :], lambda rows: dh1[rows, :], gpre_ref[...],
                                         tm, tm, put_dx)

    vec = lambda w: _resident((1, w), (0, 0))
    acc = lambda shape: pl.BlockSpec(shape, lambda i: (0,) * len(shape))
    return _call(
        body, name=name, grid=(t // tm,),
        in_specs=[_rows(tm, d), _rows(tm, d), _rows(tm, d), _rows(tm, 2 * aw), _rows(tm, aw), _rows(tm, aw),
                  _rows(tm, 128), vec(d), _wspec(lay.a_in_rows, d), vec(aw), vec(aw),
                  _resident((A_GROUPS, CHUNK, CHUNK), (0, 0, 0)), _resident((CHUNK, A_GROUPS), (0, 0)),
                  _wspec(lay.a_out_rows, d), vec(d), ANY],
        out_specs=[_rows(tm, d), _rows(tm, d), _rows(tm, 2 * aw), acc((1, d)), acc((1, d)), acc((1, aw)), acc((1, aw)),
                   acc((A_GROUPS, CHUNK, CHUNK)), acc((CHUNK, CHUNK))],
        out_shape=[jax.ShapeDtypeStruct((t, d), F32), jax.ShapeDtypeStruct((t, d), BF16),
                   jax.ShapeDtypeStruct((t, 2 * aw), BF16), jax.ShapeDtypeStruct((1, d), F32),
                   jax.ShapeDtypeStruct((1, d), F32), jax.ShapeDtypeStruct((1, aw), F32),
                   jax.ShapeDtypeStruct((1, aw), F32), jax.ShapeDtypeStruct((A_GROUPS, CHUNK, CHUNK), F32),
                   jax.ShapeDtypeStruct((CHUNK, CHUNK), F32)],
        scratch_shapes=[pltpu.VMEM((tm, aw), F32)],
        compiler_params=_params("arbitrary"),
    )(dx1, m, x, gp, u, vh, rs, gpre, wg[0], lng, lnb, ws, bst, wg[1], gpost, after)


def _window_counts(first_row, n, win):
    tpos = first_row + lax.broadcasted_iota(jnp.int32, (n, 1), 0)
    return jnp.clip(tpos + 1, 1, win).astype(F32)


def _b_fwd(x, gpre, wg, lay, j, wgrp, scale, gpost, tm, name):
    t, d = x.shape
    n = tm + HALO
    ngrp = len(B_WINDOWS)

    def body(x_ref, xprev_ref, gpre_ref, win_ref, wgrp_ref, scale_ref, wout_ref, gpost_ref,
             x1_ref, h1_ref, pooled_ref, mixed_ref, m_ref):
        i = pl.program_id(0)
        xv = x_ref[...]
        keep = jnp.where(i > 0, 1.0, 0.0)
        xe = jnp.concatenate([xprev_ref[...] * keep, xv], axis=0)
        h1e = _rms_fwd(xe, gpre_ref[...]).astype(BF16)
        h1_ref[...] = h1e[HALO:]
        p = _nn(h1e, win_ref[...])
        acc = p
        shift = 1
        for g, win in enumerate(B_WINDOWS):
            lo = g * B_GROUP_DIM
            if g > 0:
                acc = acc[:, B_GROUP_DIM:]
            while shift < win:
                acc = acc + pltpu.roll(acc, shift, 0)
                shift *= 2
            cnt = _window_counts(i * tm - HALO, n, win)
            pooled = acc[:, :B_GROUP_DIM] / cnt - p[:, lo:lo + B_GROUP_DIM]
            pooled_ref[:, lo:lo + B_GROUP_DIM] = pooled[HALO:].astype(BF16)
        for g in range(ngrp):
            cols = slice(g * B_GROUP_DIM, (g + 1) * B_GROUP_DIM)
            raw = _nn(pooled_ref[:, cols], wgrp_ref[g])
            mixed_ref[:, cols] = (raw * scale_ref[:, cols]).astype(BF16)
        m = _nn(mixed_ref[...], wout_ref[...])
        m_ref[...] = m
        x1_ref[...] = xv + _rms_fwd(m, gpost_ref[...])

    vec = lambda w: _resident((1, w), (0, 0))
    per = tm // HALO
    return _call(
        body, name=name, grid=(t // tm,),
        in_specs=[_rows(tm, d), pl.BlockSpec((HALO, d), lambda i: (jnp.maximum(i * per - 1, 0), 0)), vec(d),
                  _wspec(lay.b_rows, d), _resident((ngrp, B_GROUP_DIM, B_GROUP_DIM), (0, 0, 0)), vec(d),
                  _wspec(lay.b_rows, d), vec(d)],
        out_specs=[_rows(tm, d)] * 5,
        out_shape=[jax.ShapeDtypeStruct((t, d), F32), jax.ShapeDtypeStruct((t, d), BF16),
                   jax.ShapeDtypeStruct((t, d), BF16), jax.ShapeDtypeStruct((t, d), BF16),
                   jax.ShapeDtypeStruct((t, d), F32)],
        compiler_params=_params("parallel"),
    )(x, x, gpre, wg[0], wgrp, scale, wg[1], gpost)


def _b_bwd(dx1, m, x, pooled, gpre, wg, lay, j, wgrp, scale, gpost, after, tm, name):
    t, d = x.shape
    n = tm + HALO
    ngrp = len(B_WINDOWS)
    steps = t // tm

    def body(dx1_ref, dx1n_ref, m_ref, mn_ref, x_ref, pooled_ref, pooledn_ref, gpre_ref, win_ref, wgrp_ref, scale_ref,
             wout_ref, gpost_ref, after_ref,
             dx_ref, dm_ref, draw_ref, dp_ref, dgpost_ref, dgpre_ref, dscale_ref, dpool_ref):
        i = pl.program_id(0)

        @pl.when(i == 0)
        def _():
            for r in (dgpost_ref, dgpre_ref, dscale_ref):
                r[...] = jnp.zeros_like(r)

        keep = jnp.where(i < steps - 1, 1.0, 0.0)
        dy = dx1_ref[...]
        dye = jnp.concatenate([dy, dx1n_ref[...] * keep], axis=0)
        me = jnp.concatenate([m_ref[...], mn_ref[...]], axis=0)
        gpost_v = gpost_ref[...]
        r = lax.rsqrt(jnp.mean(me * me, axis=-1, keepdims=True) + EPS)
        mh = me * r
        dgpost_ref[...] += jnp.sum((dye * mh)[:tm], axis=0, keepdims=True)
        dmh = dye * gpost_v
        dme = (r * (dmh - mh * jnp.mean(dmh * mh, axis=-1, keepdims=True))).astype(BF16)
        dm_ref[...] = dme[:tm]
        dmixed = _nt(dme, wout_ref[...])
        pooled_e = jnp.concatenate([pooled_ref[...], pooledn_ref[...]], axis=0)
        scale_v = scale_ref[...]
        for g, win in enumerate(B_WINDOWS):
            cols = slice(g * B_GROUP_DIM, (g + 1) * B_GROUP_DIM)
            raw = _nn(pooled_e[:, cols], wgrp_ref[g])
            dscale_ref[:, cols] += jnp.sum((dmixed[:, cols] * raw)[:tm], axis=0, keepdims=True)
            draw = (dmixed[:, cols] * scale_v[:, cols]).astype(BF16)
            draw_ref[:, cols] = draw[:tm]
            dpool = _nt(draw, wgrp_ref[g])
            acc = dpool / _window_counts(i * tm, n, win)
            shift = 1
            while shift < win:
                acc = acc + pltpu.roll(acc, n - shift, 0)
                shift *= 2
            dpool_ref[:, cols] = (acc - dpool)[:tm]
        dp = dpool_ref[...].astype(BF16)
        dp_ref[...] = dp
        dh1 = _nt(dp, win_ref[...])
        dxp, dgpre = _rms_bwd(x_ref[...], gpre_ref[...], dh1)
        dgpre_ref[...] += dgpre
        dx_ref[...] = dy + dxp

    vec = lambda w: _resident((1, w), (0, 0))
    acc = lambda shape: pl.BlockSpec(shape, lambda i: (0,) * len(shape))
    per = tm // HALO
    nxt = lambda i: (jnp.minimum((i + 1) * per, t // HALO - 1), 0)
    return _call(
        body, name=name, grid=(steps,),
        in_specs=[_rows(tm, d), pl.BlockSpec((HALO, d), nxt), _rows(tm, d), pl.BlockSpec((HALO, d), nxt), _rows(tm, d),
                  _rows(tm, d), pl.BlockSpec((HALO, d), nxt), vec(d), _wspec(lay.b_rows, d),
                  _resident((ngrp, B_GROUP_DIM, B_GROUP_DIM), (0, 0, 0)), vec(d), _wspec(lay.b_rows, d),
                  vec(d), ANY],
        out_specs=[_rows(tm, d)] * 4 + [acc((1, d))] * 3,
        out_shape=[jax.ShapeDtypeStruct((t, d), F32), jax.ShapeDtypeStruct((t, d), BF16),
                   jax.ShapeDtypeStruct((t, d), BF16), jax.ShapeDtypeStruct((t, d), BF16)]
                  + [jax.ShapeDtypeStruct((1, d), F32)] * 3,
        scratch_shapes=[pltpu.VMEM((tm, d), F32)],
        compiler_params=_params("arbitrary"),
    )(dx1, dx1, m, m, x, pooled, pooled, gpre, wg[0], wgrp, scale, wg[1], gpost, after)


def _f_fwd(x1, gpre, wg, lay, l, gpost, tm, name, target=None):
    t, d = x1.shape
    hid = N_DEV * lay.ffn_rows
    head = target is not None

    def body(x_ref, gpre_ref, wgate_ref, wup_ref, wdown_ref, gpost_ref, *rest):
        x2_ref, h2_ref, a_ref, b_ref, s_ref, f_ref = rest[-7:-1] if head else rest
        xv = x_ref[...]
        h2 = _rms_fwd(xv, gpre_ref[...]).astype(BF16)
        h2_ref[...] = h2
        a = _nt(h2, wgate_ref[...])
        b = _nt(h2, wup_ref[...])
        sig = jax.nn.sigmoid(a)
        silu = a * sig
        a_ref[...] = (b * (sig + silu * (1.0 - sig))).astype(BF16)
        b_ref[...] = silu.astype(BF16)
        s = (silu * b).astype(BF16)
        s_ref[...] = s
        f = _nn(s, wdown_ref[...])
        f_ref[...] = f
        x2 = xv + _rms_fwd(f, gpost_ref[...])
        if head:
            target_ref, loss_ref = rest[0], rest[-1]

            @pl.when(pl.program_id(0) == 0)
            def _():
                loss_ref[...] = jnp.zeros_like(loss_ref)

            diff = x2 - target_ref[...]
            x2_ref[...] = diff * (1.0 / d)
            sq = jnp.sum(jnp.sum(diff * diff, axis=0, keepdims=True), axis=1, keepdims=True)
            loss_ref[...] += sq * (0.5 / d)
        else:
            x2_ref[...] = x2

    vec = lambda w: _resident((1, w), (0, 0))
    return _call(
        body, name=name, grid=(t // tm,),
        in_specs=[_rows(tm, d), vec(d), _wspec(lay.ffn_rows, d), _wspec(lay.ffn_rows, d),
                  _wspec(lay.ffn_rows, d), vec(d)] + ([_rows(tm, d)] if head else []),
        out_specs=[_rows(tm, d), _rows(tm, d), _rows(tm, hid), _rows(tm, hid), _rows(tm, hid), _rows(tm, d)]
                  + ([pl.BlockSpec((8, 128), lambda i: (0, 0))] if head else []),
        out_shape=[jax.ShapeDtypeStruct((t, d), F32), jax.ShapeDtypeStruct((t, d), BF16),
                   jax.ShapeDtypeStruct((t, hid), BF16), jax.ShapeDtypeStruct((t, hid), BF16),
                   jax.ShapeDtypeStruct((t, hid), BF16), jax.ShapeDtypeStruct((t, d), F32)]
                  + ([jax.ShapeDtypeStruct((8, 128), F32)] if head else []),
        compiler_params=_params("arbitrary" if head else "parallel"),
    )(x1, gpre, wg[0], wg[1], wg[2], gpost, *([target] if head else []))


def _f_bwd(dx2, f, x1, a, b, gpre, wg, lay, l, gpost, after, tm, name):
    t, d = x1.shape
    hid = N_DEV * lay.ffn_rows

    def body(dx2_ref, f_ref, x_ref, a_ref, b_ref, gpre_ref, wgate_ref, wup_ref, wdown_ref, gpost_ref, after_ref,
             dx1_ref, df_ref, da_ref, db_ref, dgpost_ref, dgpre_ref):
        @pl.when(pl.program_id(0) == 0)
        def _():
            dgpost_ref[...] = jnp.zeros_like(dgpost_ref)
            dgpre_ref[...] = jnp.zeros_like(dgpre_ref)

        def put_df(rows, dx):
            df_ref[rows, :] = dx.astype(BF16)

        dgpost_ref[...] += _rms_bwd_slabs(lambda rows: f_ref[rows, :], lambda rows: dx2_ref[rows, :], gpost_ref[...],
                                          tm, tm, put_df)
        ds = _nt(df_ref[...], wdown_ref[...])
        da_ref[...] = (ds * a_ref[...].astype(F32)).astype(BF16)
        db_ref[...] = (ds * b_ref[...].astype(F32)).astype(BF16)
        dh2 = _nn(da_ref[...], wgate_ref[...]) + _nn(db_ref[...], wup_ref[...])

        def put_dx(rows, dx):
            dx1_ref[rows, :] = dx2_ref[rows, :] + dx

        dgpre_ref[...] += _rms_bwd_slabs(lambda rows: x_ref[rows, :], lambda rows: dh2[rows, :], gpre_ref[...],
                                         tm, tm, put_dx)

    vec = lambda w: _resident((1, w), (0, 0))
    acc = pl.BlockSpec((1, d), lambda i: (0, 0))
    return _call(
        body, name=name, grid=(t // tm,),
        in_specs=[_rows(tm, d), _rows(tm, d), _rows(tm, d), _rows(tm, hid), _rows(tm, hid), vec(d),
                  _wspec(lay.ffn_rows, d), _wspec(lay.ffn_rows, d),
                  _wspec(lay.ffn_rows, d), vec(d), ANY],
        out_specs=[_rows(tm, d), _rows(tm, d), _rows(tm, hid), _rows(tm, hid), acc, acc],
        out_shape=[jax.ShapeDtypeStruct((t, d), F32), jax.ShapeDtypeStruct((t, d), BF16),
                   jax.ShapeDtypeStruct((t, hid), BF16), jax.ShapeDtypeStruct((t, hid), BF16),
                   jax.ShapeDtypeStruct((1, d), F32), jax.ShapeDtypeStruct((1, d), F32)],
        compiler_params=_params("arbitrary"),
    )(dx2, f, x1, a, b, gpre, wg[0], wg[1], wg[2], gpost, after)


def _grad_into(gbuf, lhs, rhs, off, rows, name, after=None):
    t, m = lhs.shape
    d = rhs.shape[1]
    assert m == N_DEV * rows and off % rows == 0
    per_tile = {352: 4, 512: 2, 256: 4, 128: 8}[rows]
    tm = per_tile * rows
    assert tm % 128 == 0 and rows % 16 == 0
    tk = 2048 if t % 2048 == 0 else 256
    ksteps = t // tk
    fresh = isinstance(gbuf, int)
    shape = (N_DEV, gbuf, d) if fresh else gbuf.shape
    extra = ([] if fresh else [gbuf]) + ([] if after is None else [after])

    def body(l_ref, r_ref, *rest):
        o_ref, acc_ref = rest[-2:]
        k = pl.program_id(1)

        @pl.when(k == 0)
        def _():
            acc_ref[...] = jnp.zeros_like(acc_ref)

        acc_ref[...] += _tn(l_ref[...], r_ref[...])

        @pl.when(k == ksteps - 1)
        def _():
            o_ref[...] = acc_ref[...].reshape(per_tile, rows, d).astype(BF16)

    return _call(
        body, name=name, grid=(N_DEV // per_tile, ksteps),
        in_specs=[pl.BlockSpec((tk, tm), lambda i, k: (k, i)), pl.BlockSpec((tk, d), lambda i, k: (k, 0))]
                 + [ANY] * len(extra),
        out_specs=pl.BlockSpec((per_tile, rows, d), lambda i, k: (i, off // rows, 0)),
        out_shape=jax.ShapeDtypeStruct(shape, BF16),
        scratch_shapes=[pltpu.VMEM((tm, d), F32)],
        input_output_aliases={} if fresh else {2: 0},
        compiler_params=_params("parallel", "arbitrary"),
    )(lhs, rhs, *extra)


def _grad_grouped(pooled, draw, name):
    t, d = pooled.shape
    ngrp = len(B_WINDOWS)
    tk = 1024 if t % 1024 == 0 else 256

    def body(p_ref, q_ref, o_ref):
        @pl.when(pl.program_id(0) == 0)
        def _():
            o_ref[...] = jnp.zeros_like(o_ref)

        for g in range(ngrp):
            cols = slice(g * B_GROUP_DIM, (g + 1) * B_GROUP_DIM)
            o_ref[g] += _tn(p_ref[:, cols], q_ref[:, cols])

    return _call(
        body, name=name, grid=(t // tk,),
        in_specs=[_rows(tk, d), _rows(tk, d)],
        out_specs=pl.BlockSpec((ngrp, B_GROUP_DIM, B_GROUP_DIM), lambda i: (0, 0, 0)),
        out_shape=jax.ShapeDtypeStruct((ngrp, B_GROUP_DIM, B_GROUP_DIM), F32),
        compiler_params=_params("arbitrary"),
    )(pooled, draw)


def _peers():
    x, y, c = lax.axis_index("x"), lax.axis_index("y"), lax.axis_index("c")
    flip = lambda v, f: 1 - v if f else v
    peers = []
    for r in range(1, N_DEV):
        px, py, pc = flip(x, r & 4), flip(y, r & 2), flip(c, r & 1)
        peers.append(((px, py, pc), 4 * px + 2 * py + pc))
    return 4 * x + 2 * y + c, peers


HBM = pl.BlockSpec(memory_space=pltpu.HBM)
SEM = pl.BlockSpec(memory_space=pltpu.SEMAPHORE)
EFFECT = pltpu.SideEffectType.DATAFLOW_SIDE_EFFECTING


def _peer_copies(scatter, srcs, lands, send_sems, recv_sems):
    me, peers = _peers()
    copies = []
    for a in range(len(srcs)):
        rows = srcs[a].shape[0]
        block = lambda k: lands[a].at[pl.ds(pl.multiple_of(k * rows, 8), rows)]
        for r, (peer, pidx) in enumerate(peers):
            src = srcs[a].at[pidx] if scatter else srcs[a]
            mine = lands[a].at[r] if scatter else block(pidx)
            theirs = lands[a].at[r] if scatter else block(me)
            send = pltpu.make_async_remote_copy(src_ref=src, dst_ref=theirs, send_sem=send_sems[a].at[r],
                                                recv_sem=recv_sems[a].at[r], device_id=peer, device_id_type=MESH)
            recv = pltpu.make_async_remote_copy(src_ref=src, dst_ref=mine, send_sem=send_sems[a].at[r],
                                                recv_sem=recv_sems[a].at[r], device_id=peer, device_id_type=MESH)
            copies.append((send, recv))
    return copies


def _own_copies(srcs, lands, send_sems):
    me, _ = _peers()
    copies = []
    for a in range(len(srcs)):
        rows = srcs[a].shape[0]
        copies.append(pltpu.make_async_copy(srcs[a], lands[a].at[pl.ds(pl.multiple_of(me * rows, 8), rows)],
                                            send_sems[a].at[N_DEV - 1]))
    return copies


def _exchange_start(scatter, srcs, lands, after, name):
    n = len(srcs)

    def body(*refs):
        src_refs, land_refs = refs[:n], refs[n:2 * n]
        outs = refs[2 * n + 1:]
        send_sems, recv_sems, token = outs[:n], outs[n:2 * n], outs[-1]
        for send, _ in _peer_copies(scatter, src_refs, land_refs, send_sems, recv_sems):
            send.start()
        if not scatter:
            for own in _own_copies(src_refs, land_refs, send_sems):
                own.start()
        token[...] = jnp.zeros_like(token)

    hbm = lambda a: pltpu.with_memory_space_constraint(a, pltpu.HBM)
    res = _call(
        body, name=name,
        in_specs=[HBM] * (2 * n) + [ANY],
        out_specs=[SEM] * (2 * n) + [HBM] * (2 * n) + [pl.BlockSpec(memory_space=pltpu.VMEM)],
        out_shape=[pltpu.SemaphoreType.DMA((N_DEV,))] * (2 * n)
                  + [pltpu.HBM(a.shape, a.dtype) for a in list(srcs) + list(lands)]
                  + [jax.ShapeDtypeStruct((8, 128), F32)],
        input_output_aliases={i: 2 * n + i for i in range(2 * n)},
        compiler_params=pltpu.CompilerParams(has_side_effects=EFFECT),
    )(*[hbm(a) for a in srcs], *[hbm(a) for a in lands], after)
    return res[:n], res[n:2 * n], res[2 * n:3 * n], res[3 * n:4 * n], res[-1]


def _exchange_wait(scatter, send_sems, recv_sems, srcs, lands, after, name):
    n = len(srcs)
    after = list(after) if isinstance(after, (list, tuple)) else [after]

    def body(*refs):
        src_refs, land_refs = refs[:n], refs[n:2 * n]
        send_refs, recv_refs = refs[2 * n:3 * n], refs[3 * n:4 * n]
        for send, recv in _peer_copies(scatter, src_refs, land_refs, send_refs, recv_refs):
            send.wait_send()
            recv.wait_recv()
        if not scatter:
            for own in _own_copies(src_refs, land_refs, send_refs):
                own.wait()

    res = _call(
        body, name=name,
        in_specs=[HBM] * (2 * n) + [SEM] * (2 * n) + [ANY] * len(after),
        out_specs=[HBM] * (2 * n),
        out_shape=[pltpu.HBM(a.shape, a.dtype) for a in list(srcs) + list(lands)],
        input_output_aliases={i: i for i in range(2 * n)},
        compiler_params=pltpu.CompilerParams(has_side_effects=EFFECT),
    )(*srcs, *lands, *send_sems, *recv_sems, *after)
    return res[:n], res[n:]


def _row_tile(rows):
    if rows <= 512:
        return rows
    return max([tr for tr in range(16, 513, 16) if rows % tr == 0] or [rows])


def _sum_parts(own, got, me, name):
    _, rows, w = own.shape
    tr = _row_tile(rows)

    def body(me_ref, a_ref, b_ref, o_ref):
        s = a_ref[...].astype(F32)
        for j in range(N_DEV - 1):
            s = s + b_ref[j].astype(F32)
        o_ref[...] = s

    return _call(
        body, name=name,
        grid_spec=pltpu.PrefetchScalarGridSpec(
            num_scalar_prefetch=1, grid=(rows // tr,),
            in_specs=[pl.BlockSpec((None, tr, w), lambda i, me_ref: (me_ref[0], i, 0)),
                      pl.BlockSpec((N_DEV - 1, tr, w), lambda i, me_ref: (0, i, 0))],
            out_specs=pl.BlockSpec((tr, w), lambda i, me_ref: (i, 0))),
        out_shape=jax.ShapeDtypeStruct((rows, w), F32),
        compiler_params=_params("parallel"),
    )(me, own, got)


def _sum_devices(stacked, name):
    k, rows, w = stacked.shape
    tr = _row_tile(rows)

    def body(a_ref, o_ref):
        s = a_ref[0]
        for j in range(1, k):
            s = s + a_ref[j]
        o_ref[...] = s

    return _call(
        body, name=name, grid=(rows // tr,),
        in_specs=[pl.BlockSpec((k, tr, w), lambda i: (0, i, 0))],
        out_specs=pl.BlockSpec((tr, w), lambda i: (i, 0)),
        out_shape=jax.ShapeDtypeStruct((rows, w), F32),
        compiler_params=_params("parallel"),
    )(stacked)


def _adamw(w, g, m, v, name):
    rows, cols = w.shape
    tr = _row_tile(rows)

    def body(w_ref, g_ref, m_ref, v_ref, d_ref, nm_ref, nv_ref):
        gv = g_ref[...]
        nm = ADAM_B1 * m_ref[...] + (1.0 - ADAM_B1) * gv
        nv = ADAM_B2 * v_ref[...] + (1.0 - ADAM_B2) * (gv * gv)
        m_hat = nm / (1.0 - ADAM_B1 ** ADAM_STEP)
        v_hat = nv / (1.0 - ADAM_B2 ** ADAM_STEP)
        d_ref[...] = -ADAM_LR * (m_hat / (jnp.sqrt(v_hat) + ADAM_EPS) + ADAM_WD * w_ref[...])
        nm_ref[...] = nm
        nv_ref[...] = nv

    spec = pl.BlockSpec((tr, cols), lambda i: (i, 0))
    return _call(
        body, name=name, grid=(rows // tr,),
        in_specs=[spec] * 4, out_specs=[spec] * 3,
        out_shape=[jax.ShapeDtypeStruct((rows, cols), F32)] * 3,
        compiler_params=_params("parallel"),
    )(w, g, m, v)


SMALL = ("a_ln_g", "a_ln_b", "a_w_s", "a_b_s", "mix_pre_g", "mix_post_g", "ffn_pre_g", "ffn_post_g")


def _pack_small(parts, d, last_row=None):
    rows = [parts[k].reshape(-1, d) for k in SMALL] + ([] if last_row is None else [last_row])
    flat = jnp.concatenate(rows, axis=0)
    return jnp.pad(flat, ((0, -flat.shape[0] % 8), (0, 0)))


def _unpack_small(flat, like):
    out, r = {}, 0
    for k in SMALL:
        n = like[k].size // flat.shape[1]
        out[k] = flat[r:r + n].reshape(like[k].shape)
        r += n
    return out


def kernel(x, a_w_in, a_ln_g, a_ln_b, a_w_s, a_b_s, a_w_out, b_w_in, b_w_grp, b_scale, b_w_out, mix_pre_g, mix_post_g, ffn_pre_g, ffn_post_g, ffn_w_gate, ffn_w_up, ffn_w_down, loss_target, m_a_w_in, m_a_ln_g, m_a_ln_b, m_a_w_s, m_a_b_s, m_a_w_out, m_b_w_in, m_b_w_grp, m_b_scale, m_b_w_out, m_mix_pre_g, m_mix_post_g, m_ffn_pre_g, m_ffn_post_g, m_ffn_w_gate, m_ffn_w_up, m_ffn_w_down, v_a_w_in, v_a_ln_g, v_a_ln_b, v_a_w_s, v_a_b_s, v_a_w_out, v_b_w_in, v_b_w_grp, v_b_scale, v_b_w_out, v_mix_pre_g, v_mix_post_g, v_ffn_pre_g, v_ffn_post_g, v_ffn_w_gate, v_ffn_w_up, v_ffn_w_down):
    args = dict(locals())
    names = ("a_w_in", "a_ln_g", "a_ln_b", "a_w_s", "a_b_s", "a_w_out", "b_w_in", "b_w_grp", "b_scale", "b_w_out",
             "mix_pre_g", "mix_post_g", "ffn_pre_g", "ffn_post_g", "ffn_w_gate", "ffn_w_up", "ffn_w_down")
    w = {k: args[k] for k in names}
    mom = {k: args["m_" + k] for k in names}
    var = {k: args["v_" + k] for k in names}

    t, d = x.shape[1], x.shape[2]
    ffn_local = ffn_w_gate.shape[2]
    lay = _Layout(d, ffn_local)
    me = 4 * lax.axis_index("x") + 2 * lax.axis_index("y") + lax.axis_index("c")
    me1 = jnp.reshape(me, (1,)).astype(jnp.int32)

    def landing(block):
        return lax.empty((N_DEV * block.shape[0],) + block.shape[1:], block.dtype)

    def shards(i, mixer, zero):
        j = i // 2
        if not mixer:
            parts = [ffn_w_gate[i].T, ffn_w_up[i].T, ffn_w_down[i]]
        elif i % 2 == 0:
            parts = [a_w_in[j].T, a_w_out[j]]
        else:
            parts = [b_w_in[j], b_w_out[j]]
        return [(p + zero).astype(BF16) for p in parts]

    nsub = 2 * DEPTH
    wg = [None] * nsub
    first = shards(0, True, 0.0)
    first = _exchange_start(False, first, [landing(b) for b in first], jnp.zeros((8, 128), F32), "gather_first_start")
    zero = first[4][0, 0]
    ngrp = len(B_WINDOWS)
    grp_local = b_w_grp.shape[2]
    sdev = b_scale.shape[1]
    side_rows = 2 * ngrp * grp_local
    side = jnp.concatenate(
        [b_w_grp.reshape(side_rows, B_GROUP_DIM),
         jnp.pad(b_scale, ((0, 6), (0, B_GROUP_DIM - sdev)))], axis=0) + zero
    later, where = [side], [slice(0, 1)]
    for k in range(1, nsub):
        new = shards(k // 2, k % 2 == 0, zero)
        where.append(slice(len(later), len(later) + len(new)))
        later += new
    send_sems, recv_sems, later, zones, token = _exchange_start(
        False, later, [landing(b) for b in later], first[4], "gather_start")
    turned = ("a_w_in", "ffn_w_gate", "ffn_w_up")
    turn = lambda a: jnp.swapaxes(a, 1, 2)
    state = {k: tuple(turn(a[k] + zero) for a in (w, mom, var)) for k in turned}
    state["small"] = tuple(_pack_small(a, d) + zero for a in (w, mom, var))
    ready = [a for group in state.values() for a in group]
    _, wg[0] = _exchange_wait(False, *first[:4], [token] + ready, "gather_first_wait")

    def gathered(k, after):
        s = where[k]
        _, got = _exchange_wait(False, send_sems[s], recv_sems[s], later[s], zones[s], after, f"gather_wait_{k}")
        return got

    row = lambda a: a.reshape(1, -1)
    bst = jnp.transpose(a_b_s, (0, 2, 1))

    tm = 256 if t % 256 == 0 else CHUNK
    tm_abwd = tm
    tm_b = 512 if t % 512 == 0 else tm
    tm_f = tm

    saved = []
    h = x[0]
    wgrp_full = scale_full = None
    for i in range(DEPTH):
        j = i // 2
        gpre = row(mix_pre_g[i])
        if i > 0:
            wg[2 * i] = gathered(2 * i, h)
        if i % 2 == 0:
            x1, h1, gp, u, vh, rs, gated, m = _a_fwd(h, gpre, wg[2 * i], lay, j, row(a_ln_g[j]), row(a_ln_b[j]),
                                                     a_w_s[j], bst[j], row(mix_post_g[i]), tm, f"a_fwd_{j}")
            mix = dict(h1=h1, gp=gp, u=u, vh=vh, rs=rs, gated=gated, m=m)
        else:
            if wgrp_full is None:
                side_g = gathered(0, h)[0].reshape(N_DEV, side_rows + 8, B_GROUP_DIM)
                wgrp_full = (side_g[:, :side_rows].reshape(N_DEV, 2, ngrp, grp_local, B_GROUP_DIM)
                             .transpose(1, 2, 0, 3, 4).reshape(2, ngrp, B_GROUP_DIM, B_GROUP_DIM).astype(BF16))
                scale_full = (side_g[:, side_rows:side_rows + 2, :sdev].transpose(1, 0, 2)
                              .reshape(2, 1, N_DEV * sdev))
            x1, h1, pooled, mixed, m = _b_fwd(h, gpre, wg[2 * i], lay, j, wgrp_full[j], scale_full[j],
                                              row(mix_post_g[i]), tm_b, f"b_fwd_{j}")
            mix = dict(h1=h1, pooled=pooled, mixed=mixed, m=m)
        wg[2 * i + 1] = gathered(2 * i + 1, x1)
        x2, h2, a, b, s, f, *loss_acc = _f_fwd(x1, row(ffn_pre_g[i]), wg[2 * i + 1], lay, i, row(ffn_post_g[i]), tm_f,
                                               f"f_fwd_{i}", loss_target[0] if i == DEPTH - 1 else None)
        saved.append(dict(x=h, x1=x1, mix=mix, h2=h2, a=a, b=b, s=s, f=f))
        h = x2
    dy, (loss_acc,) = h, loss_acc

    small_g = {k: [None] * w[k].shape[0] for k in SMALL}
    dgrp, dscale = [None, None], [None, None]
    pending = [None] * nsub
    token = jnp.zeros((8, 128), F32)

    def scatter(k, gbuf):
        got = pltpu.with_memory_space_constraint(lax.empty((N_DEV - 1,) + gbuf.shape[1:], gbuf.dtype), pltpu.HBM)
        ss, rs, src, zone, tok = _exchange_start(True, [gbuf], [got], token, f"scatter_start_{k}")
        pending[k] = (ss, rs, src, zone)
        return tok

    def small_exchanges():
        side_grad = jnp.concatenate(
            [jnp.stack(dgrp).reshape(2, ngrp, N_DEV, grp_local, B_GROUP_DIM).transpose(2, 0, 1, 3, 4)
             .reshape(N_DEV, side_rows, B_GROUP_DIM),
             jnp.pad(jnp.stack(dscale).reshape(2, N_DEV, sdev).transpose(1, 0, 2),
                     ((0, 0), (0, 6), (0, B_GROUP_DIM - sdev)))], axis=1)
        small_part = _pack_small({k: jnp.stack(small_g[k]) for k in SMALL}, d,
                                 jnp.broadcast_to(loss_acc[:1, :1], (1, d)))
        got = pltpu.with_memory_space_constraint(lax.empty((N_DEV - 1,) + side_grad.shape[1:], F32), pltpu.HBM)
        side_x = _exchange_start(True, [side_grad], [got], token, "side_scatter_start")
        small_x = _exchange_start(False, [small_part], [landing(small_part)], side_x[4], "small_gather_start")
        return side_x[:4], small_x[:4], small_x[4]

    for i in reversed(range(DEPTH)):
        sv = saved[i]
        j = i // 2
        wf, wm = wg[2 * i + 1], wg[2 * i]
        dx1, df, da, db, dgpost, dgpre = _f_bwd(dy, sv["f"], sv["x1"], sv["a"], sv["b"], row(ffn_pre_g[i]), wf, lay, i,
                                                 row(ffn_post_g[i]), token, tm_f, f"f_bwd_{i}")
        small_g["ffn_post_g"][i], small_g["ffn_pre_g"][i] = dgpost[0], dgpre[0]
        gbuf = _grad_into(lay.f_total, da, sv["h2"], lay.gate[i], lay.ffn_rows, f"g_gate_{i}")
        gbuf = _grad_into(gbuf, db, sv["h2"], lay.up[i], lay.ffn_rows, f"g_up_{i}")
        gbuf = _grad_into(gbuf, sv["s"], df, lay.down[i], lay.ffn_rows, f"g_down_{i}")
        token = scatter(2 * i + 1, gbuf)
        mix = sv["mix"]
        gpost = row(mix_post_g[i])
        if i % 2 == 0:
            dx, dm, dz, dgpost, dgpre, dlng, dlnb, dws, dbt = _a_bwd(
                dx1, mix["m"], sv["x"], mix["gp"], mix["u"], mix["vh"], mix["rs"], row(mix_pre_g[i]), wm, lay, j,
                row(a_ln_g[j]), row(a_ln_b[j]), a_w_s[j], bst[j], gpost, token, tm_abwd, f"a_bwd_{j}")
            small_g["a_ln_g"][j], small_g["a_ln_b"][j] = dlng[0], dlnb[0]
            small_g["a_w_s"][j], small_g["a_b_s"][j] = dws, dbt[:, :A_GROUPS].T
            small_g["mix_post_g"][i], small_g["mix_pre_g"][i] = dgpost[0], dgpre[0]
            order = None
            if i == 0:
                side_x, small_x, order = small_exchanges()
            gbuf = _grad_into(lay.a_total, dz, mix["h1"], lay.a_in[j], lay.a_in_rows, f"g_a_in_{j}", after=order)
            gbuf = _grad_into(gbuf, mix["gated"], dm, lay.a_out[j], lay.a_out_rows, f"g_a_out_{j}")
        else:
            dx, dm, draw, dp, dgpost, dgpre, dsc = _b_bwd(
                dx1, mix["m"], sv["x"], mix["pooled"], row(mix_pre_g[i]), wm, lay, j, wgrp_full[j], scale_full[j],
                gpost, token, tm_b, f"b_bwd_{j}")
            dscale[j] = dsc[0]
            dgrp[j] = _grad_grouped(mix["pooled"], draw, f"g_b_grp_{j}")
            gbuf = _grad_into(lay.b_total, mix["h1"], dp, lay.b_in[j], lay.b_rows, f"g_b_in_{j}")
            gbuf = _grad_into(gbuf, mix["mixed"], dm, lay.b_out[j], lay.b_rows, f"g_b_out_{j}")
            small_g["mix_post_g"][i], small_g["mix_pre_g"][i] = dgpost[0], dgpre[0]
        token = scatter(2 * i, gbuf)
        dy = dx
    grad_x = dy[None]

    g_sub = [None] * nsub

    def arrived(k, after):
        ss, rs, src, zone = pending[k]
        (own,), (got,) = _exchange_wait(True, ss, rs, src, zone, after, f"scatter_wait_{k}")
        g_sub[k] = _sum_parts(own, got, me1, f"sum_grads_{k}")

    def rows_of(k, off, n):
        return g_sub[k][off:off + n]

    grads, delta, new_m, new_v = {}, {}, {}, {}

    def update(k):
        back = turn if k in turned else (lambda a: a)
        wk, mk, vk = state[k] if k in turned else (w[k], mom[k], var[k])
        shape = wk.shape
        two = lambda a: a.reshape(-1, shape[-1])
        dl, nm, nv = _adamw(two(wk), two(grads[k]), two(mk), two(vk), f"adamw_{k}")
        delta[k], new_m[k], new_v[k] = (back(a.reshape(shape)) for a in (dl, nm, nv))
        grads[k] = back(grads[k])

    for k in range(1, nsub):
        arrived(k, token)
    grads["ffn_w_gate"] = jnp.stack([rows_of(2 * l + 1, lay.gate[l], ffn_local) for l in range(DEPTH)])
    grads["ffn_w_up"] = jnp.stack([rows_of(2 * l + 1, lay.up[l], ffn_local) for l in range(DEPTH)])
    grads["ffn_w_down"] = jnp.stack([rows_of(2 * l + 1, lay.down[l], ffn_local) for l in range(DEPTH)])
    grads["b_w_in"] = jnp.stack([rows_of(4 * j + 2, lay.b_in[j], lay.b_rows) for j in range(2)])
    grads["b_w_out"] = jnp.stack([rows_of(4 * j + 2, lay.b_out[j], lay.b_rows) for j in range(2)])
    early = ("ffn_w_gate", "ffn_w_up", "ffn_w_down", "b_w_in", "b_w_out")
    for k in early:
        update(k)

    (side_own,), (side_got,) = _exchange_wait(True, *side_x, [delta[k] for k in early], "side_scatter_wait")
    g_side = _sum_parts(side_own, side_got, me1, "sum_side")
    grads["b_w_grp"] = g_side[:side_rows].reshape(b_w_grp.shape)
    grads["b_scale"] = g_side[side_rows:side_rows + 2, :sdev]
    update("b_w_grp")
    update("b_scale")
    _, (small_all,) = _exchange_wait(False, *small_x, [delta["b_w_grp"], delta["b_scale"]], "small_gather_wait")
    small_sum = _sum_devices(small_all.reshape(N_DEV, -1, d), "sum_small")
    g_small = _unpack_small(small_sum, w)
    loss = small_sum[sum(w[k].size for k in SMALL) // d, 0]
    grads.update(g_small)
    dl, nm, nv = _adamw(state["small"][0], _pack_small(g_small, d), state["small"][1], state["small"][2],
                        "adamw_small")
    delta.update(_unpack_small(dl, w))
    new_m.update(_unpack_small(nm, w))
    new_v.update(_unpack_small(nv, w))

    arrived(0, dl)
    grads["a_w_in"] = jnp.stack([rows_of(4 * j, lay.a_in[j], lay.a_in_rows) for j in range(2)])
    grads["a_w_out"] = jnp.stack([rows_of(4 * j, lay.a_out[j], lay.a_out_rows) for j in range(2)])
    update("a_w_in")
    update("a_w_out")

    return (loss, grad_x, *[grads[k] for k in names], *[delta[k] for k in names], *[new_m[k] for k in names],
            *[new_v[k] for k in names])
```

```python
import math

import jax
import jax.numpy as jnp
from jax import lax
from jax.experimental import pallas as pl
from jax.experimental.pallas import tpu as pltpu

F32 = jnp.float32
BF16 = jnp.bfloat16
MESH = pl.DeviceIdType.MESH
ANY = pl.BlockSpec(memory_space=pl.ANY)

N_DEV = 8
EPS = 1e-6
CHUNK = 128
A_GROUPS = 8
A_GROUP_DIM = 256
B_WINDOWS = (2, 4, 8, 16)
B_GROUP_DIM = 256
HALO = 16
DEPTH = 4

ADAM_LR = 0.001
ADAM_B1 = 0.9
ADAM_B2 = 0.999
ADAM_EPS = 1e-08
ADAM_WD = 0.01
ADAM_STEP = 10

VMEM_LIMIT_BYTES = 60 * 1024 * 1024

ERF_P = 0.3275911
ERF_A = (0.254829592, -0.284496736, 1.421413741, -1.453152027, 1.061405429)
INV_SQRT2 = 1.0 / math.sqrt(2.0)
LOG2_E = 1.0 / math.log(2.0)
INV_SQRT_2PI = 1.0 / math.sqrt(2.0 * math.pi)


def _call(body, **kw):
    return pl.pallas_call(body, **kw)


def _params(*semantics):
    return pltpu.CompilerParams(dimension_semantics=semantics or None, vmem_limit_bytes=VMEM_LIMIT_BYTES)


def _resident(shape, index):
    return pl.BlockSpec(shape, lambda *_: index, pipeline_mode=pl.Buffered(1))


def _rows(tm, width):
    return pl.BlockSpec((tm, width), lambda i: (i, 0))


def _nn(a, b):
    return jnp.dot(a, b, preferred_element_type=F32)


def _nt(a, b):
    return lax.dot_general(a, b, (((1,), (1,)), ((), ())), preferred_element_type=F32)


def _tn(a, b):
    return lax.dot_general(a, b, (((0,), (0,)), ((), ())), preferred_element_type=F32)


def _rms_fwd(x, g):
    r = lax.rsqrt(jnp.mean(x * x, axis=-1, keepdims=True) + EPS)
    return x * r * g


def _rms_bwd(x, g, dy):
    r = lax.rsqrt(jnp.mean(x * x, axis=-1, keepdims=True) + EPS)
    xh = x * r
    dg = jnp.sum(dy * xh, axis=0, keepdims=True)
    dxh = dy * g
    dx = r * (dxh - xh * jnp.mean(dxh * xh, axis=-1, keepdims=True))
    return dx, dg


SLAB = 16


def _slabs(n):
    return [slice(r, r + SLAB) for r in range(0, n, SLAB)]


def _rms_bwd_slabs(x_at, dy_at, g, n, n_sum, emit):
    acc = jnp.zeros((8, g.shape[1]), F32)
    for rows in _slabs(n):
        x = x_at(rows)
        dy = dy_at(rows)
        r = lax.rsqrt(jnp.mean(x * x, axis=-1, keepdims=True) + EPS)
        xh = x * r
        if rows.start < n_sum:
            p = dy * xh
            acc = acc + p[:8] + p[8:]
        dxh = dy * g
        emit(rows, r * (dxh - xh * jnp.mean(dxh * xh, axis=-1, keepdims=True)))
    return jnp.sum(acc, axis=0, keepdims=True)


def _gelu(z):
    phi = 0.5 + 0.5 * lax.erf(z * INV_SQRT2)
    e = jnp.exp2(z * z * (-0.5 * LOG2_E))
    return z * phi, phi + z * e * INV_SQRT_2PI


def _layernorm_stats(v):
    mu = jnp.mean(v, axis=-1, keepdims=True)
    xc = v - mu
    rs = lax.rsqrt(jnp.mean(xc * xc, axis=-1, keepdims=True) + EPS)
    return xc * rs, rs


def _tril_mask():
    r = lax.broadcasted_iota(jnp.int32, (CHUNK, CHUNK), 0)
    c = lax.broadcasted_iota(jnp.int32, (CHUNK, CHUNK), 1)
    return r >= c


class _Layout:
    def __init__(self, d, ffn_rows):
        self.ffn_rows = ffn_rows
        self.gate, self.up, self.down = [0] * DEPTH, [self.ffn_rows] * DEPTH, [2 * self.ffn_rows] * DEPTH
        self.f_total = 3 * self.ffn_rows
        self.a_in_rows, self.a_out_rows, self.b_rows = 4 * d // N_DEV, 2 * d // N_DEV, d // N_DEV
        self.a_in, self.a_out = [0, 0], [self.a_in_rows] * 2
        self.a_total = self.a_in_rows + self.a_out_rows
        self.b_in, self.b_out = [0, 0], [self.b_rows] * 2
        self.b_total = 2 * self.b_rows


def _wspec(rows, d):
    return _resident((N_DEV * rows, d), (0, 0))


def _a_fwd(x, gpre, wg, lay, j, lng, lnb, ws, bst, gpost, tm, name):
    t, d = x.shape
    aw = 2 * d
    nch = tm // CHUNK

    def body(x_ref, gpre_ref, win_ref, lng_ref, lnb_ref, ws_ref, bst_ref, wout_ref, gpost_ref,
             x1_ref, h1_ref, gp_ref, u_ref, vh_ref, rs_ref, gated_ref, m_ref):
        xv = x_ref[...]
        h1 = _rms_fwd(xv, gpre_ref[...]).astype(BF16)
        h1_ref[...] = h1
        z = _nt(h1, win_ref[...])
        u, du_dz = _gelu(z[:, :aw])
        v, dv_dz = _gelu(z[:, aw:])
        gp_ref[:, :aw] = du_dz.astype(BF16)
        gp_ref[:, aw:] = dv_dz.astype(BF16)
        u_ref[...] = u.astype(BF16)
        vh, rs = _layernorm_stats(v)
        vh_ref[...] = vh.astype(BF16)
        rs_ref[...] = jnp.broadcast_to(rs, rs_ref.shape)
        vn = (vh * lng_ref[...] + lnb_ref[...]).astype(BF16)
        mask = _tril_mask()
        for g in range(A_GROUPS):
            wm = jnp.where(mask, ws_ref[g], 0.0).astype(BF16)
            cols = slice(g * A_GROUP_DIM, (g + 1) * A_GROUP_DIM)
            for c in range(nch):
                rows = slice(c * CHUNK, (c + 1) * CHUNK)
                sv = _nn(wm, vn[rows, cols]) + bst_ref[:, g:g + 1]
                gated_ref[rows, cols] = (u[rows, cols] * sv).astype(BF16)
        m = _nn(gated_ref[...], wout_ref[...])
        m_ref[...] = m
        x1_ref[...] = xv + _rms_fwd(m, gpost_ref[...])

    vec = lambda w: _resident((1, w), (0, 0))
    return _call(
        body, name=name, grid=(t // tm,),
        in_specs=[_rows(tm, d), vec(d), _wspec(lay.a_in_rows, d), vec(aw), vec(aw),
                  _resident((A_GROUPS, CHUNK, CHUNK), (0, 0, 0)), _resident((CHUNK, A_GROUPS), (0, 0)),
                  _wspec(lay.a_out_rows, d), vec(d)],
        out_specs=[_rows(tm, d), _rows(tm, d), _rows(tm, 2 * aw), _rows(tm, aw), _rows(tm, aw), _rows(tm, 128),
                   _rows(tm, aw), _rows(tm, d)],
        out_shape=[jax.ShapeDtypeStruct((t, d), F32), jax.ShapeDtypeStruct((t, d), BF16),
                   jax.ShapeDtypeStruct((t, 2 * aw), BF16), jax.ShapeDtypeStruct((t, aw), BF16),
                   jax.ShapeDtypeStruct((t, aw), BF16), jax.ShapeDtypeStruct((t, 128), F32),
                   jax.ShapeDtypeStruct((t, aw), BF16), jax.ShapeDtypeStruct((t, d), F32)],
        compiler_params=_params("parallel"),
    )(x, gpre, wg[0], lng, lnb, ws, bst, wg[1], gpost)


def _a_bwd(dx1, m, x, gp, u, vh, rs, gpre, wg, lay, j, lng, lnb, ws, bst, gpost, after, tm, name):
    t, d = x.shape
    aw = 2 * d
    nch = tm // CHUNK

    def body(dx1_ref, m_ref, x_ref, gp_ref, u_ref, vh_ref, rs_ref, gpre_ref, win_ref, lng_ref, lnb_ref, ws_ref, bst_ref,
             wout_ref, gpost_ref, after_ref,
             dx_ref, dm_ref, dz_ref, dgpost_ref, dgpre_ref, dlng_ref, dlnb_ref, dws_ref, dbt_ref, dvn_ref):
        @pl.when(pl.program_id(0) == 0)
        def _():
            for r in (dgpost_ref, dgpre_ref, dlng_ref, dlnb_ref, dws_ref, dbt_ref):
                r[...] = jnp.zeros_like(r)

        def put_dm(rows, dx):
            dm_ref[rows, :] = dx.astype(BF16)

        dgpost_ref[...] += _rms_bwd_slabs(lambda rows: m_ref[rows, :], lambda rows: dx1_ref[rows, :], gpost_ref[...],
                                          tm, tm, put_dm)
        dgated = _nt(dm_ref[...], wout_ref[...])

        vh = vh_ref[...].astype(F32)
        rs = rs_ref[:, :1]
        lng_v = lng_ref[...]
        vn = (vh * lng_v + lnb_ref[...]).astype(BF16)
        mask = _tril_mask()
        lane = lax.broadcasted_iota(jnp.int32, (CHUNK, CHUNK), 1)
        for g in range(A_GROUPS):
            wm = jnp.where(mask, ws_ref[g], 0.0).astype(BF16)
            cols = slice(g * A_GROUP_DIM, (g + 1) * A_GROUP_DIM)
            dws_g = jnp.zeros((CHUNK, CHUNK), F32)
            db_g = jnp.zeros((CHUNK, 1), F32)
            for c in range(nch):
                rows = slice(c * CHUNK, (c + 1) * CHUNK)
                vn_cg = vn[rows, cols]
                sv = _nn(wm, vn_cg) + bst_ref[:, g:g + 1]
                dg_cg = dgated[rows, cols]
                dsv = dg_cg * u_ref[rows, cols].astype(F32)
                dsv_bf = dsv.astype(BF16)
                db_g = db_g + jnp.sum(dsv, axis=1, keepdims=True)
                dws_g = dws_g + _nt(dsv_bf, vn_cg)
                dvn_ref[rows, cols] = _tn(wm, dsv_bf)
                dz_ref[rows, cols] = (dg_cg * sv * gp_ref[rows, cols].astype(F32)).astype(BF16)
            dws_ref[g] += jnp.where(mask, dws_g, 0.0)
            dbt_ref[...] += jnp.where(lane == g, db_g, 0.0)
        dvn = dvn_ref[...]
        dlng_ref[...] += jnp.sum(dvn * vh, axis=0, keepdims=True)
        dlnb_ref[...] += jnp.sum(dvn, axis=0, keepdims=True)
        dvh = dvn * lng_v
        dv = rs * (dvh - jnp.mean(dvh, axis=-1, keepdims=True) - vh * jnp.mean(dvh * vh, axis=-1, keepdims=True))
        dz_ref[:, aw:] = (dv * gp_ref[:, aw:].astype(F32)).astype(BF16)
        dh1 = _nn(dz_ref[...], win_ref[...])

        def put_dx(rows, dx):
            dx_ref[rows, :] = dx1_ref[rows, :] + dx

        dgpre_ref[...] += _rms_bwd_slabs(lambda rows: x_ref[rows, :], lambda rows: dh1[rows, :], gpre_ref[...],
                                         tm, tm, put_dx)

    vec = lambda w: _resident((1, w), (0, 0))
    acc = lambda shape: pl.BlockSpec(shape, lambda i: (0,) * len(shape))
    return _call(
        body, name=name, grid=(t // tm,),
        in_specs=[_rows(tm, d), _rows(tm, d), _rows(tm, d), _rows(tm, 2 * aw), _rows(tm, aw), _rows(tm, aw),
                  _rows(tm, 128), vec(d), _wspec(lay.a_in_rows, d), vec(aw), vec(aw),
                  _resident((A_GROUPS, CHUNK, CHUNK), (0, 0, 0)), _resident((CHUNK, A_GROUPS), (0, 0)),
                  _wspec(lay.a_out_rows, d), vec(d), ANY],
        out_specs=[_rows(tm, d), _rows(tm, d), _rows(tm, 2 * aw), acc((1, d)), acc((1, d)), acc((1, aw)), acc((1, aw)),
                   acc((A_GROUPS, CHUNK, CHUNK)), acc((CHUNK, CHUNK))],
        out_shape=[jax.ShapeDtypeStruct((t, d), F32), jax.ShapeDtypeStruct((t, d), BF16),
                   jax.ShapeDtypeStruct((t, 2 * aw), BF16), jax.ShapeDtypeStruct((1, d), F32),
                   jax.ShapeDtypeStruct((1, d), F32), jax.ShapeDtypeStruct((1, aw), F32),
                   jax.ShapeDtypeStruct((1, aw), F32), jax.ShapeDtypeStruct((A_GROUPS, CHUNK, CHUNK), F32),
                   jax.ShapeDtypeStruct((CHUNK, CHUNK), F32)],
        scratch_shapes=[pltpu.VMEM((tm, aw), F32)],
        compiler_params=_params("arbitrary"),
    )(dx1, m, x, gp, u, vh, rs, gpre, wg[0], lng, lnb, ws, bst, wg[1], gpost, after)


def _window_counts(first_row, n, win):
    tpos = first_row + lax.broadcasted_iota(jnp.int32, (n, 1), 0)
    return jnp.clip(tpos + 1, 1, win).astype(F32)


def _b_fwd(x, gpre, wg, lay, j, wgrp, scale, gpost, tm, name):
    t, d = x.shape
    n = tm + HALO
    ngrp = len(B_WINDOWS)

    def body(x_ref, xprev_ref, gpre_ref, win_ref, wgrp_ref, scale_ref, wout_ref, gpost_ref,
             x1_ref, h1_ref, pooled_ref, mixed_ref, m_ref):
        i = pl.program_id(0)
        xv = x_ref[...]
        keep = jnp.where(i > 0, 1.0, 0.0)
        xe = jnp.concatenate([xprev_ref[...] * keep, xv], axis=0)
        h1e = _rms_fwd(xe, gpre_ref[...]).astype(BF16)
        h1_ref[...] = h1e[HALO:]
        p = _nn(h1e, win_ref[...])
        acc = p
        shift = 1
        for g, win in enumerate(B_WINDOWS):
            lo = g * B_GROUP_DIM
            if g > 0:
                acc = acc[:, B_GROUP_DIM:]
            while shift < win:
                acc = acc + pltpu.roll(acc, shift, 0)
                shift *= 2
            cnt = _window_counts(i * tm - HALO, n, win)
            pooled = acc[:, :B_GROUP_DIM] / cnt - p[:, lo:lo + B_GROUP_DIM]
            pooled_ref[:, lo:lo + B_GROUP_DIM] = pooled[HALO:].astype(BF16)
        for g in range(ngrp):
            cols = slice(g * B_GROUP_DIM, (g + 1) * B_GROUP_DIM)
            raw = _nn(pooled_ref[:, cols], wgrp_ref[g])
            mixed_ref[:, cols] = (raw * scale_ref[:, cols]).astype(BF16)
        m = _nn(mixed_ref[...], wout_ref[...])
        m_ref[...] = m
        x1_ref[...] = xv + _rms_fwd(m, gpost_ref[...])

    vec = lambda w: _resident((1, w), (0, 0))
    per = tm // HALO
    return _call(
        body, name=name, grid=(t // tm,),
        in_specs=[_rows(tm, d), pl.BlockSpec((HALO, d), lambda i: (jnp.maximum(i * per - 1, 0), 0)), vec(d),
                  _wspec(lay.b_rows, d), _resident((ngrp, B_GROUP_DIM, B_GROUP_DIM), (0, 0, 0)), vec(d),
                  _wspec(lay.b_rows, d), vec(d)],
        out_specs=[_rows(tm, d)] * 5,
        out_shape=[jax.ShapeDtypeStruct((t, d), F32), jax.ShapeDtypeStruct((t, d), BF16),
                   jax.ShapeDtypeStruct((t, d), BF16), jax.ShapeDtypeStruct((t, d), BF16),
                   jax.ShapeDtypeStruct((t, d), F32)],
        compiler_params=_params("parallel"),
    )(x, x, gpre, wg[0], wgrp, scale, wg[1], gpost)


def _b_bwd(dx1, m, x, pooled, gpre, wg, lay, j, wgrp, scale, gpost, after, tm, name):
    t, d = x.shape
    n = tm + HALO
    ngrp = len(B_WINDOWS)
    steps = t // tm

    def body(dx1_ref, dx1n_ref, m_ref, mn_ref, x_ref, pooled_ref, pooledn_ref, gpre_ref, win_ref, wgrp_ref, scale_ref,
             wout_ref, gpost_ref, after_ref,
             dx_ref, dm_ref, draw_ref, dp_ref, dgpost_ref, dgpre_ref, dscale_ref, dpool_ref):
        i = pl.program_id(0)

        @pl.when(i == 0)
        def _():
            for r in (dgpost_ref, dgpre_ref, dscale_ref):
                r[...] = jnp.zeros_like(r)

        keep = jnp.where(i < steps - 1, 1.0, 0.0)
        dy = dx1_ref[...]
        dye = jnp.concatenate([dy, dx1n_ref[...] * keep], axis=0)
        me = jnp.concatenate([m_ref[...], mn_ref[...]], axis=0)
        gpost_v = gpost_ref[...]
        r = lax.rsqrt(jnp.mean(me * me, axis=-1, keepdims=True) + EPS)
        mh = me * r
        dgpost_ref[...] += jnp.sum((dye * mh)[:tm], axis=0, keepdims=True)
        dmh = dye * gpost_v
        dme = (r * (dmh - mh * jnp.mean(dmh * mh, axis=-1, keepdims=True))).astype(BF16)
        dm_ref[...] = dme[:tm]
        dmixed = _nt(dme, wout_ref[...])
        pooled_e = jnp.concatenate([pooled_ref[...], pooledn_ref[...]], axis=0)
        scale_v = scale_ref[...]
        for g, win in enumerate(B_WINDOWS):
            cols = slice(g * B_GROUP_DIM, (g + 1) * B_GROUP_DIM)
            raw = _nn(pooled_e[:, cols], wgrp_ref[g])
            dscale_ref[:, cols] += jnp.sum((dmixed[:, cols] * raw)[:tm], axis=0, keepdims=True)
            draw = (dmixed[:, cols] * scale_v[:, cols]).astype(BF16)
            draw_ref[:, cols] = draw[:tm]
            dpool = _nt(draw, wgrp_ref[g])
            acc = dpool / _window_counts(i * tm, n, win)
            shift = 1
            while shift < win:
                acc = acc + pltpu.roll(acc, n - shift, 0)
                shift *= 2
            dpool_ref[:, cols] = (acc - dpool)[:tm]
        dp = dpool_ref[...].astype(BF16)
        dp_ref[...] = dp
        dh1 = _nt(dp, win_ref[...])
        dxp, dgpre = _rms_bwd(x_ref[...], gpre_ref[...], dh1)
        dgpre_ref[...] += dgpre
        dx_ref[...] = dy + dxp

    vec = lambda w: _resident((1, w), (0, 0))
    acc = lambda shape: pl.BlockSpec(shape, lambda i: (0,) * len(shape))
    per = tm // HALO
    nxt = lambda i: (jnp.minimum((i + 1) * per, t // HALO - 1), 0)
    return _call(
        body, name=name, grid=(steps,),
        in_specs=[_rows(tm, d), pl.BlockSpec((HALO, d), nxt), _rows(tm, d), pl.BlockSpec((HALO, d), nxt), _rows(tm, d),
                  _rows(tm, d), pl.BlockSpec((HALO, d), nxt), vec(d), _wspec(lay.b_rows, d),
                  _resident((ngrp, B_GROUP_DIM, B_GROUP_DIM), (0, 0, 0)), vec(d), _wspec(lay.b_rows, d),
                  vec(d), ANY],
        out_specs=[_rows(tm, d)] * 4 + [acc((1, d))] * 3,
        out_shape=[jax.ShapeDtypeStruct((t, d), F32), jax.ShapeDtypeStruct((t, d), BF16),
                   jax.ShapeDtypeStruct((t, d), BF16), jax.ShapeDtypeStruct((t, d), BF16)]
                  + [jax.ShapeDtypeStruct((1, d), F32)] * 3,
        scratch_shapes=[pltpu.VMEM((tm, d), F32)],
        compiler_params=_params("arbitrary"),
    )(dx1, dx1, m, m, x, pooled, pooled, gpre, wg[0], wgrp, scale, wg[1], gpost, after)


def _f_fwd(x1, gpre, wg, lay, l, gpost, tm, name, target=None):
    t, d = x1.shape
    hid = N_DEV * lay.ffn_rows
    head = target is not None

    def body(x_ref, gpre_ref, wgate_ref, wup_ref, wdown_ref, gpost_ref, *rest):
        x2_ref, h2_ref, a_ref, b_ref, s_ref, f_ref = rest[-7:-1] if head else rest
        xv = x_ref[...]
        h2 = _rms_fwd(xv, gpre_ref[...]).astype(BF16)
        h2_ref[...] = h2
        a = _nt(h2, wgate_ref[...])
        b = _nt(h2, wup_ref[...])
        sig = jax.nn.sigmoid(a)
        silu = a * sig
        a_ref[...] = (b * (sig + silu * (1.0 - sig))).astype(BF16)
        b_ref[...] = silu.astype(BF16)
        s = (silu * b).astype(BF16)
        s_ref[...] = s
        f = _nn(s, wdown_ref[...])
        f_ref[...] = f
        x2 = xv + _rms_fwd(f, gpost_ref[...])
        if head:
            target_ref, loss_ref = rest[0], rest[-1]

            @pl.when(pl.program_id(0) == 0)
            def _():
                loss_ref[...] = jnp.zeros_like(loss_ref)

            diff = x2 - target_ref[...]
            x2_ref[...] = diff * (1.0 / d)
            sq = jnp.sum(jnp.sum(diff * diff, axis=0, keepdims=True), axis=1, keepdims=True)
            loss_ref[...] += sq * (0.5 / d)
        else:
            x2_ref[...] = x2

    vec = lambda w: _resident((1, w), (0, 0))
    return _call(
        body, name=name, grid=(t // tm,),
        in_specs=[_rows(tm, d), vec(d), _wspec(lay.ffn_rows, d), _wspec(lay.ffn_rows, d),
                  _wspec(lay.ffn_rows, d), vec(d)] + ([_rows(tm, d)] if head else []),
        out_specs=[_rows(tm, d), _rows(tm, d), _rows(tm, hid), _rows(tm, hid), _rows(tm, hid), _rows(tm, d)]
                  + ([pl.BlockSpec((8, 128), lambda i: (0, 0))] if head else []),
        out_shape=[jax.ShapeDtypeStruct((t, d), F32), jax.ShapeDtypeStruct((t, d), BF16),
                   jax.ShapeDtypeStruct((t, hid), BF16), jax.ShapeDtypeStruct((t, hid), BF16),
                   jax.ShapeDtypeStruct((t, hid), BF16), jax.ShapeDtypeStruct((t, d), F32)]
                  + ([jax.ShapeDtypeStruct((8, 128), F32)] if head else []),
        compiler_params=_params("arbitrary" if head else "parallel"),
    )(x1, gpre, wg[0], wg[1], wg[2], gpost, *([target] if head else []))


def _f_bwd(dx2, f, x1, a, b, gpre, wg, lay, l, gpost, after, tm, name):
    t, d = x1.shape
    hid = N_DEV * lay.ffn_rows

    def body(dx2_ref, f_ref, x_ref, a_ref, b_ref, gpre_ref, wgate_ref, wup_ref, wdown_ref, gpost_ref, after_ref,
             dx1_ref, df_ref, da_ref, db_ref, dgpost_ref, dgpre_ref):
        @pl.when(pl.program_id(0) == 0)
        def _():
            dgpost_ref[...] = jnp.zeros_like(dgpost_ref)
            dgpre_ref[...] = jnp.zeros_like(dgpre_ref)

        def put_df(rows, dx):
            df_ref[rows, :] = dx.astype(BF16)

        dgpost_ref[...] += _rms_bwd_slabs(lambda rows: f_ref[rows, :], lambda rows: dx2_ref[rows, :], gpost_ref[...],
                                          tm, tm, put_df)
        ds = _nt(df_ref[...], wdown_ref[...])
        da_ref[...] = (ds * a_ref[...].astype(F32)).astype(BF16)
        db_ref[...] = (ds * b_ref[...].astype(F32)).astype(BF16)
        dh2 = _nn(da_ref[...], wgate_ref[...]) + _nn(db_ref[...], wup_ref[...])

        def put_dx(rows, dx):
            dx1_ref[rows, :] = dx2_ref[rows, :] + dx

        dgpre_ref[...] += _rms_bwd_slabs(lambda rows: x_ref[rows, :], lambda rows: dh2[rows, :], gpre_ref[...],
                                         tm, tm, put_dx)

    vec = lambda w: _resident((1, w), (0, 0))
    acc = pl.BlockSpec((1, d), lambda i: (0, 0))
    return _call(
        body, name=name, grid=(t // tm,),
        in_specs=[_rows(tm, d), _rows(tm, d), _rows(tm, d), _rows(tm, hid), _rows(tm, hid), vec(d),
                  _wspec(lay.ffn_rows, d), _wspec(lay.ffn_rows, d),
                  _wspec(lay.ffn_rows, d), vec(d), ANY],
        out_specs=[_rows(tm, d), _rows(tm, d), _rows(tm, hid), _rows(tm, hid), acc, acc],
        out_shape=[jax.ShapeDtypeStruct((t, d), F32), jax.ShapeDtypeStruct((t, d), BF16),
                   jax.ShapeDtypeStruct((t, hid), BF16), jax.ShapeDtypeStruct((t, hid), BF16),
                   jax.ShapeDtypeStruct((1, d), F32), jax.ShapeDtypeStruct((1, d), F32)],
        compiler_params=_params("arbitrary"),
    )(dx2, f, x1, a, b, gpre, wg[0], wg[1], wg[2], gpost, after)


def _grad_into(gbuf, lhs, rhs, off, rows, name, after=None):
    t, m = lhs.shape
    d = rhs.shape[1]
    assert m == N_DEV * rows and off % rows == 0
    per_tile = {352: 4, 512: 2, 256: 4, 128: 8}[rows]
    tm = per_tile * rows
    assert tm % 128 == 0 and rows % 16 == 0
    tk = 2048 if t % 2048 == 0 else 256
    ksteps = t // tk
    fresh = isinstance(gbuf, int)
    shape = (N_DEV, gbuf, d) if fresh else gbuf.shape
    extra = ([] if fresh else [gbuf]) + ([] if after is None else [after])

    def body(l_ref, r_ref, *rest):
        o_ref, acc_ref = rest[-2:]
        k = pl.program_id(1)

        @pl.when(k == 0)
        def _():
            acc_ref[...] = jnp.zeros_like(acc_ref)

        acc_ref[...] += _tn(l_ref[...], r_ref[...])

        @pl.when(k == ksteps - 1)
        def _():
            o_ref[...] = acc_ref[...].reshape(per_tile, rows, d).astype(BF16)

    return _call(
        body, name=name, grid=(N_DEV // per_tile, ksteps),
        in_specs=[pl.BlockSpec((tk, tm), lambda i, k: (k, i)), pl.BlockSpec((tk, d), lambda i, k: (k, 0))]
                 + [ANY] * len(extra),
        out_specs=pl.BlockSpec((per_tile, rows, d), lambda i, k: (i, off // rows, 0)),
        out_shape=jax.ShapeDtypeStruct(shape, BF16),
        scratch_shapes=[pltpu.VMEM((tm, d), F32)],
        input_output_aliases={} if fresh else {2: 0},
        compiler_params=_params("parallel", "arbitrary"),
    )(lhs, rhs, *extra)


def _grad_grouped(pooled, draw, name):
    t, d = pooled.shape
    ngrp = len(B_WINDOWS)
    tk = 1024 if t % 1024 == 0 else 256

    def body(p_ref, q_ref, o_ref):
        @pl.when(pl.program_id(0) == 0)
        def _():
            o_ref[...] = jnp.zeros_like(o_ref)

        for g in range(ngrp):
            cols = slice(g * B_GROUP_DIM, (g + 1) * B_GROUP_DIM)
            o_ref[g] += _tn(p_ref[:, cols], q_ref[:, cols])

    return _call(
        body, name=name, grid=(t // tk,),
        in_specs=[_rows(tk, d), _rows(tk, d)],
        out_specs=pl.BlockSpec((ngrp, B_GROUP_DIM, B_GROUP_DIM), lambda i: (0, 0, 0)),
        out_shape=jax.ShapeDtypeStruct((ngrp, B_GROUP_DIM, B_GROUP_DIM), F32),
        compiler_params=_params("arbitrary"),
    )(pooled, draw)


def _peers():
    x, y, c = lax.axis_index("x"), lax.axis_index("y"), lax.axis_index("c")
    flip = lambda v, f: 1 - v if f else v
    peers = []
    for r in range(1, N_DEV):
        px, py, pc = flip(x, r & 4), flip(y, r & 2), flip(c, r & 1)
        peers.append(((px, py, pc), 4 * px + 2 * py + pc))
    return 4 * x + 2 * y + c, peers


HBM = pl.BlockSpec(memory_space=pltpu.HBM)
SEM = pl.BlockSpec(memory_space=pltpu.SEMAPHORE)
EFFECT = pltpu.SideEffectType.DATAFLOW_SIDE_EFFECTING


def _peer_copies(scatter, srcs, lands, send_sems, recv_sems):
    me, peers = _peers()
    copies = []
    for a in range(len(srcs)):
        rows = srcs[a].shape[0]
        block = lambda k: lands[a].at[pl.ds(pl.multiple_of(k * rows, 8), rows)]
        for r, (peer, pidx) in enumerate(peers):
            src = srcs[a].at[pidx] if scatter else srcs[a]
            mine = lands[a].at[r] if scatter else block(pidx)
            theirs = lands[a].at[r] if scatter else block(me)
            send = pltpu.make_async_remote_copy(src_ref=src, dst_ref=theirs, send_sem=send_sems[a].at[r],
                                                recv_sem=recv_sems[a].at[r], device_id=peer, device_id_type=MESH)
            recv = pltpu.make_async_remote_copy(src_ref=src, dst_ref=mine, send_sem=send_sems[a].at[r],
                                                recv_sem=recv_sems[a].at[r], device_id=peer, device_id_type=MESH)
            copies.append((send, recv))
    return copies


def _own_copies(srcs, lands, send_sems):
    me, _ = _peers()
    copies = []
    for a in range(len(srcs)):
        rows = srcs[a].shape[0]
        copies.append(pltpu.make_async_copy(srcs[a], lands[a].at[pl.ds(pl.multiple_of(me * rows, 8), rows)],
                                            send_sems[a].at[N_DEV - 1]))
    return copies


def _exchange_start(scatter, srcs, lands, after, name):
    n = len(srcs)

    def body(*refs):
        src_refs, land_refs = refs[:n], refs[n:2 * n]
        outs = refs[2 * n + 1:]
        send_sems, recv_sems, token = outs[:n], outs[n:2 * n], outs[-1]
        for send, _ in _peer_copies(scatter, src_refs, land_refs, send_sems, recv_sems):
            send.start()
        if not scatter:
            for own in _own_copies(src_refs, land_refs, send_sems):
                own.start()
        token[...] = jnp.zeros_like(token)

    hbm = lambda a: pltpu.with_memory_space_constraint(a, pltpu.HBM)
    res = _call(
        body, name=name,
        in_specs=[HBM] * (2 * n) + [ANY],
        out_specs=[SEM] * (2 * n) + [HBM] * (2 * n) + [pl.BlockSpec(memory_space=pltpu.VMEM)],
        out_shape=[pltpu.SemaphoreType.DMA((N_DEV,))] * (2 * n)
                  + [pltpu.HBM(a.shape, a.dtype) for a in list(srcs) + list(lands)]
                  + [jax.ShapeDtypeStruct((8, 128), F32)],
        input_output_aliases={i: 2 * n + i for i in range(2 * n)},
        compiler_params=pltpu.CompilerParams(has_side_effects=EFFECT),
    )(*[hbm(a) for a in srcs], *[hbm(a) for a in lands], after)
    return res[:n], res[n:2 * n], res[2 * n:3 * n], res[3 * n:4 * n], res[-1]


def _exchange_wait(scatter, send_sems, recv_sems, srcs, lands, after, name):
    n = len(srcs)
    after = list(after) if isinstance(after, (list, tuple)) else [after]

    def body(*refs):
        src_refs, land_refs = refs[:n], refs[n:2 * n]
        send_refs, recv_refs = refs[2 * n:3 * n], refs[3 * n:4 * n]
        for send, recv in _peer_copies(scatter, src_refs, land_refs, send_refs, recv_refs):
            send.wait_send()
            recv.wait_recv()
        if not scatter:
            for own in _own_copies(src_refs, land_refs, send_refs):
                own.wait()

    res = _call(
        body, name=name,
        in_specs=[HBM] * (2 * n) + [SEM] * (2 * n) + [ANY] * len(after),
        out_specs=[HBM] * (2 * n),
        out_shape=[pltpu.HBM(a.shape, a.dtype) for a in list(srcs) + list(lands)],
        input_output_aliases={i: i for i in range(2 * n)},
        compiler_params=pltpu.CompilerParams(has_side_effects=EFFECT),
    )(*srcs, *lands, *send_sems, *recv_sems, *after)
    return res[:n], res[n:]


def _row_tile(rows):
    if rows <= 512:
        return rows
    return max([tr for tr in range(16, 513, 16) if rows % tr == 0] or [rows])


def _sum_parts(own, got, me, name):
    _, rows, w = own.shape
    tr = _row_tile(rows)

    def body(me_ref, a_ref, b_ref, o_ref):
        s = a_ref[...].astype(F32)
        for j in range(N_DEV - 1):
            s = s + b_ref[j].astype(F32)
        o_ref[...] = s

    return _call(
        body, name=name,
        grid_spec=pltpu.PrefetchScalarGridSpec(
            num_scalar_prefetch=1, grid=(rows // tr,),
            in_specs=[pl.BlockSpec((None, tr, w), lambda i, me_ref: (me_ref[0], i, 0)),
                      pl.BlockSpec((N_DEV - 1, tr, w), lambda i, me_ref: (0, i, 0))],
            out_specs=pl.BlockSpec((tr, w), lambda i, me_ref: (i, 0))),
        out_shape=jax.ShapeDtypeStruct((rows, w), F32),
        compiler_params=_params("parallel"),
    )(me, own, got)


def _sum_devices(stacked, name):
    k, rows, w = stacked.shape
    tr = _row_tile(rows)

    def body(a_ref, o_ref):
        s = a_ref[0]
        for j in range(1, k):
            s = s + a_ref[j]
        o_ref[...] = s

    return _call(
        body, name=name, grid=(rows // tr,),
        in_specs=[pl.BlockSpec((k, tr, w), lambda i: (0, i, 0))],
        out_specs=pl.BlockSpec((tr, w), lambda i: (i, 0)),
        out_shape=jax.ShapeDtypeStruct((rows, w), F32),
        compiler_params=_params("parallel"),
    )(stacked)


def _adamw(w, g, m, v, name):
    rows, cols = w.shape
    tr = _row_tile(rows)

    def body(w_ref, g_ref, m_ref, v_ref, d_ref, nm_ref, nv_ref):
        gv = g_ref[...]
        nm = ADAM_B1 * m_ref[...] + (1.0 - ADAM_B1) * gv
        nv = ADAM_B2 * v_ref[...] + (1.0 - ADAM_B2) * (gv * gv)
        m_hat = nm / (1.0 - ADAM_B1 ** ADAM_STEP)
        v_hat = nv / (1.0 - ADAM_B2 ** ADAM_STEP)
        d_ref[...] = -ADAM_LR * (m_hat / (jnp.sqrt(v_hat) + ADAM_EPS) + ADAM_WD * w_ref[...])
        nm_ref[...] = nm
        nv_ref[...] = nv

    spec = pl.BlockSpec((tr, cols), lambda i: (i, 0))
    return _call(
        body, name=name, grid=(rows // tr,),
        in_specs=[spec] * 4, out_specs=[spec] * 3,
        out_shape=[jax.ShapeDtypeStruct((rows, cols), F32)] * 3,
        compiler_params=_params("parallel"),
    )(w, g, m, v)


SMALL = ("a_ln_g", "a_ln_b", "a_w_s", "a_b_s", "mix_pre_g", "mix_post_g", "ffn_pre_g", "ffn_post_g")


def _pack_small(parts, d, last_row=None):
    rows = [parts[k].reshape(-1, d) for k in SMALL] + ([] if last_row is None else [last_row])
    flat = jnp.concatenate(rows, axis=0)
    return jnp.pad(flat, ((0, -flat.shape[0] % 8), (0, 0)))


def _unpack_small(flat, like):
    out, r = {}, 0
    for k in SMALL:
        n = like[k].size // flat.shape[1]
        out[k] = flat[r:r + n].reshape(like[k].shape)
        r += n
    return out


def kernel(x, a_w_in, a_ln_g, a_ln_b, a_w_s, a_b_s, a_w_out, b_w_in, b_w_grp, b_scale, b_w_out, mix_pre_g, mix_post_g, ffn_pre_g, ffn_post_g, ffn_w_gate, ffn_w_up, ffn_w_down, loss_target, m_a_w_in, m_a_ln_g, m_a_ln_b, m_a_w_s, m_a_b_s, m_a_w_out, m_b_w_in, m_b_w_grp, m_b_scale, m_b_w_out, m_mix_pre_g, m_mix_post_g, m_ffn_pre_g, m_ffn_post_g, m_ffn_w_gate, m_ffn_w_up, m_ffn_w_down, v_a_w_in, v_a_ln_g, v_a_ln_b, v_a_w_s, v_a_b_s, v_a_w_out, v_b_w_in, v_b_w_grp, v_b_scale, v_b_w_out, v_mix_pre_g, v_mix_post_g, v_ffn_pre_g, v_ffn_post_g, v_ffn_w_gate, v_ffn_w_up, v_ffn_w_down):
    args = dict(locals())
    names = ("a_w_in", "a_ln_g", "a_ln_b", "a_w_s", "a_b_s", "a_w_out", "b_w_in", "b_w_grp", "b_scale", "b_w_out",
             "mix_pre_g", "mix_post_g", "ffn_pre_g", "ffn_post_g", "ffn_w_gate", "ffn_w_up", "ffn_w_down")
    w = {k: args[k] for k in names}
    mom = {k: args["m_" + k] for k in names}
    var = {k: args["v_" + k] for k in names}

    t, d = x.shape[1], x.shape[2]
    ffn_local = ffn_w_gate.shape[2]
    lay = _Layout(d, ffn_local)
    me = 4 * lax.axis_index("x") + 2 * lax.axis_index("y") + lax.axis_index("c")
    me1 = jnp.reshape(me, (1,)).astype(jnp.int32)

    def landing(block):
        return lax.empty((N_DEV * block.shape[0],) + block.shape[1:], block.dtype)

    def shards(i, mixer, zero):
        j = i // 2
        if not mixer:
            parts = [ffn_w_gate[i].T, ffn_w_up[i].T, ffn_w_down[i]]
        elif i % 2 == 0:
            parts = [a_w_in[j].T, a_w_out[j]]
        else:
            parts = [b_w_in[j], b_w_out[j]]
        return [(p + zero).astype(BF16) for p in parts]

    nsub = 2 * DEPTH
    wg = [None] * nsub
    first = shards(0, True, 0.0)
    first = _exchange_start(False, first, [landing(b) for b in first], jnp.zeros((8, 128), F32), "gather_first_start")
    zero = first[4][0, 0]
    ngrp = len(B_WINDOWS)
    grp_local = b_w_grp.shape[2]
    sdev = b_scale.shape[1]
    side_rows = 2 * ngrp * grp_local
    side = jnp.concatenate(
        [b_w_grp.reshape(side_rows, B_GROUP_DIM),
         jnp.pad(b_scale, ((0, 6), (0, B_GROUP_DIM - sdev)))], axis=0) + zero
    later, where = [side], [slice(0, 1)]
    for k in range(1, nsub):
        new = shards(k // 2, k % 2 == 0, zero)
        where.append(slice(len(later), len(later) + len(new)))
        later += new
    send_sems, recv_sems, later, zones, token = _exchange_start(
        False, later, [landing(b) for b in later], first[4], "gather_start")
    turned = ("a_w_in", "ffn_w_gate", "ffn_w_up")
    turn = lambda a: jnp.swapaxes(a, 1, 2)
    state = {k: tuple(turn(a[k]) for a in (w, mom, var)) for k in turned}
    state["small"] = tuple(_pack_small(a, d) for a in (w, mom, var))
    ready = [a for group in state.values() for a in group]
    _, wg[0] = _exchange_wait(False, *first[:4], [token] + ready, "gather_first_wait")

    def gathered(k, after):
        s = where[k]
        _, got = _exchange_wait(False, send_sems[s], recv_sems[s], later[s], zones[s], after, f"gather_wait_{k}")
        return got

    row = lambda a: a.reshape(1, -1)
    bst = jnp.transpose(a_b_s, (0, 2, 1))

    tm = 256 if t % 256 == 0 else CHUNK
    tm_abwd = tm
    tm_b = 512 if t % 512 == 0 else tm
    tm_f = tm

    saved = []
    h = x[0]
    wgrp_full = scale_full = None
    for i in range(DEPTH):
        j = i // 2
        gpre = row(mix_pre_g[i])
        if i > 0:
            wg[2 * i] = gathered(2 * i, h)
        if i % 2 == 0:
            x1, h1, gp, u, vh, rs, gated, m = _a_fwd(h, gpre, wg[2 * i], lay, j, row(a_ln_g[j]), row(a_ln_b[j]),
                                                     a_w_s[j], bst[j], row(mix_post_g[i]), tm, f"a_fwd_{j}")
            mix = dict(h1=h1, gp=gp, u=u, vh=vh, rs=rs, gated=gated, m=m)
        else:
            if wgrp_full is None:
                side_g = gathered(0, h)[0].reshape(N_DEV, side_rows + 8, B_GROUP_DIM)
                wgrp_full = (side_g[:, :side_rows].reshape(N_DEV, 2, ngrp, grp_local, B_GROUP_DIM)
                             .transpose(1, 2, 0, 3, 4).reshape(2, ngrp, B_GROUP_DIM, B_GROUP_DIM).astype(BF16))
                scale_full = (side_g[:, side_rows:side_rows + 2, :sdev].transpose(1, 0, 2)
                              .reshape(2, 1, N_DEV * sdev))
            x1, h1, pooled, mixed, m = _b_fwd(h, gpre, wg[2 * i], lay, j, wgrp_full[j], scale_full[j],
                                              row(mix_post_g[i]), tm_b, f"b_fwd_{j}")
            mix = dict(h1=h1, pooled=pooled, mixed=mixed, m=m)
        wg[2 * i + 1] = gathered(2 * i + 1, x1)
        x2, h2, a, b, s, f, *loss_acc = _f_fwd(x1, row(ffn_pre_g[i]), wg[2 * i + 1], lay, i, row(ffn_post_g[i]), tm_f,
                                               f"f_fwd_{i}", loss_target[0] if i == DEPTH - 1 else None)
        saved.append(dict(x=h, x1=x1, mix=mix, h2=h2, a=a, b=b, s=s, f=f))
        h = x2
    dy, (loss_acc,) = h, loss_acc

    small_g = {k: [None] * w[k].shape[0] for k in SMALL}
    dgrp, dscale = [None, None], [None, None]
    pending = [None] * nsub
    token = jnp.zeros((8, 128), F32)

    def scatter(k, gbuf):
        got = pltpu.with_memory_space_constraint(lax.empty((N_DEV - 1,) + gbuf.shape[1:], gbuf.dtype), pltpu.HBM)
        ss, rs, src, zone, tok = _exchange_start(True, [gbuf], [got], token, f"scatter_start_{k}")
        pending[k] = (ss, rs, src, zone)
        return tok

    def small_exchanges():
        side_grad = jnp.concatenate(
            [jnp.stack(dgrp).reshape(2, ngrp, N_DEV, grp_local, B_GROUP_DIM).transpose(2, 0, 1, 3, 4)
             .reshape(N_DEV, side_rows, B_GROUP_DIM),
             jnp.pad(jnp.stack(dscale).reshape(2, N_DEV, sdev).transpose(1, 0, 2),
                     ((0, 0), (0, 6), (0, B_GROUP_DIM - sdev)))], axis=1)
        small_part = _pack_small({k: jnp.stack(small_g[k]) for k in SMALL}, d,
                                 jnp.broadcast_to(loss_acc[:1, :1], (1, d)))
        got = pltpu.with_memory_space_constraint(lax.empty((N_DEV - 1,) + side_grad.shape[1:], F32), pltpu.HBM)
        side_x = _exchange_start(True, [side_grad], [got], token, "side_scatter_start")
        small_x = _exchange_start(False, [small_part], [landing(small_part)], side_x[4], "small_gather_start")
        return side_x[:4], small_x[:4], small_x[4]

    for i in reversed(range(DEPTH)):
        sv = saved[i]
        j = i // 2
        wf, wm = wg[2 * i + 1], wg[2 * i]
        dx1, df, da, db, dgpost, dgpre = _f_bwd(dy, sv["f"], sv["x1"], sv["a"], sv["b"], row(ffn_pre_g[i]), wf, lay, i,
                                                 row(ffn_post_g[i]), token, tm_f, f"f_bwd_{i}")
        small_g["ffn_post_g"][i], small_g["ffn_pre_g"][i] = dgpost[0], dgpre[0]
        gbuf = _grad_into(lay.f_total, da, sv["h2"], lay.gate[i], lay.ffn_rows, f"g_gate_{i}")
        gbuf = _grad_into(gbuf, db, sv["h2"], lay.up[i], lay.ffn_rows, f"g_up_{i}")
        gbuf = _grad_into(gbuf, sv["s"], df, lay.down[i], lay.ffn_rows, f"g_down_{i}")
        token = scatter(2 * i + 1, gbuf)
        mix = sv["mix"]
        gpost = row(mix_post_g[i])
        if i % 2 == 0:
            dx, dm, dz, dgpost, dgpre, dlng, dlnb, dws, dbt = _a_bwd(
                dx1, mix["m"], sv["x"], mix["gp"], mix["u"], mix["vh"], mix["rs"], row(mix_pre_g[i]), wm, lay, j,
                row(a_ln_g[j]), row(a_ln_b[j]), a_w_s[j], bst[j], gpost, token, tm_abwd, f"a_bwd_{j}")
            small_g["a_ln_g"][j], small_g["a_ln_b"][j] = dlng[0], dlnb[0]
            small_g["a_w_s"][j], small_g["a_b_s"][j] = dws, dbt[:, :A_GROUPS].T
            small_g["mix_post_g"][i], small_g["mix_pre_g"][i] = dgpost[0], dgpre[0]
            order = None
            if i == 0:
                side_x, small_x, order = small_exchanges()
            gbuf = _grad_into(lay.a_total, dz, mix["h1"], lay.a_in[j], lay.a_in_rows, f"g_a_in_{j}", after=order)
            gbuf = _grad_into(gbuf, mix["gated"], dm, lay.a_out[j], lay.a_out_rows, f"g_a_out_{j}")
        else:
            dx, dm, draw, dp, dgpost, dgpre, dsc = _b_bwd(
                dx1, mix["m"], sv["x"], mix["pooled"], row(mix_pre_g[i]), wm, lay, j, wgrp_full[j], scale_full[j],
                gpost, token, tm_b, f"b_bwd_{j}")
            dscale[j] = dsc[0]
            dgrp[j] = _grad_grouped(mix["pooled"], draw, f"g_b_grp_{j}")
            gbuf = _grad_into(lay.b_total, mix["h1"], dp, lay.b_in[j], lay.b_rows, f"g_b_in_{j}")
            gbuf = _grad_into(gbuf, mix["mixed"], dm, lay.b_out[j], lay.b_rows, f"g_b_out_{j}")
            small_g["mix_post_g"][i], small_g["mix_pre_g"][i] = dgpost[0], dgpre[0]
        token = scatter(2 * i, gbuf)
        dy = dx
    grad_x = dy[None]

    g_sub = [None] * nsub

    def arrived(k, after):
        ss, rs, src, zone = pending[k]
        (own,), (got,) = _exchange_wait(True, ss, rs, src, zone, after, f"scatter_wait_{k}")
        g_sub[k] = _sum_parts(own, got, me1, f"sum_grads_{k}")

    def rows_of(k, off, n):
        return g_sub[k][off:off + n]

    grads, delta, new_m, new_v = {}, {}, {}, {}

    def update(k):
        back = turn if k in turned else (lambda a: a)
        wk, mk, vk = state[k] if k in turned else (w[k], mom[k], var[k])
        shape = wk.shape
        two = lambda a: a.reshape(-1, shape[-1])
        dl, nm, nv = _adamw(two(wk), two(grads[k]), two(mk), two(vk), f"adamw_{k}")
        delta[k], new_m[k], new_v[k] = (back(a.reshape(shape)) for a in (dl, nm, nv))
        grads[k] = back(grads[k])

    for k in range(1, nsub):
        arrived(k, token)
    grads["ffn_w_gate"] = jnp.stack([rows_of(2 * l + 1, lay.gate[l], ffn_local) for l in range(DEPTH)])
    grads["ffn_w_up"] = jnp.stack([rows_of(2 * l + 1, lay.up[l], ffn_local) for l in range(DEPTH)])
    grads["ffn_w_down"] = jnp.stack([rows_of(2 * l + 1, lay.down[l], ffn_local) for l in range(DEPTH)])
    grads["b_w_in"] = jnp.stack([rows_of(4 * j + 2, lay.b_in[j], lay.b_rows) for j in range(2)])
    grads["b_w_out"] = jnp.stack([rows_of(4 * j + 2, lay.b_out[j], lay.b_rows) for j in range(2)])
    early = ("ffn_w_gate", "ffn_w_up", "ffn_w_down", "b_w_in", "b_w_out")
    for k in early:
        update(k)

    (side_own,), (side_got,) = _exchange_wait(True, *side_x, [delta[k] for k in early], "side_scatter_wait")
    g_side = _sum_parts(side_own, side_got, me1, "sum_side")
    grads["b_w_grp"] = g_side[:side_rows].reshape(b_w_grp.shape)
    grads["b_scale"] = g_side[side_rows:side_rows + 2, :sdev]
    update("b_w_grp")
    update("b_scale")
    _, (small_all,) = _exchange_wait(False, *small_x, [delta["b_w_grp"], delta["b_scale"]], "small_gather_wait")
    small_sum = _sum_devices(small_all.reshape(N_DEV, -1, d), "sum_small")
    g_small = _unpack_small(small_sum, w)
    loss = small_sum[sum(w[k].size for k in SMALL) // d, 0]
    grads.update(g_small)
    dl, nm, nv = _adamw(state["small"][0], _pack_small(g_small, d), state["small"][1], state["small"][2],
                        "adamw_small")
    delta.update(_unpack_small(dl, w))
    new_m.update(_unpack_small(nm, w))
    new_v.update(_unpack_small(nv, w))

    arrived(0, dl)
    grads["a_w_in"] = jnp.stack([rows_of(4 * j, lay.a_in[j], lay.a_in_rows) for j in range(2)])
    grads["a_w_out"] = jnp.stack([rows_of(4 * j, lay.a_out[j], lay.a_out_rows) for j in range(2)])
    update("a_w_in")
    update("a_w_out")

    return (loss, grad_x, *[grads[k] for k in names], *[delta[k] for k in names], *[new_m[k] for k in names],
            *[new_v[k] for k in names])
```

```python
import math

import jax
import jax.numpy as jnp
from jax import lax
from jax.experimental import pallas as pl
from jax.experimental.pallas import tpu as pltpu

F32 = jnp.float32
BF16 = jnp.bfloat16
MESH = pl.DeviceIdType.MESH
ANY = pl.BlockSpec(memory_space=pl.ANY)

N_DEV = 8
EPS = 1e-6
CHUNK = 128
A_GROUPS = 8
A_GROUP_DIM = 256
B_WINDOWS = (2, 4, 8, 16)
B_GROUP_DIM = 256
HALO = 16
DEPTH = 4

ADAM_LR = 0.001
ADAM_B1 = 0.9
ADAM_B2 = 0.999
ADAM_EPS = 1e-08
ADAM_WD = 0.01
ADAM_STEP = 10

VMEM_LIMIT_BYTES = 60 * 1024 * 1024

INV_SQRT2 = 1.0 / math.sqrt(2.0)
LOG2_E = 1.0 / math.log(2.0)
INV_SQRT_2PI = 1.0 / math.sqrt(2.0 * math.pi)


def _call(body, **kw):
    return pl.pallas_call(body, **kw)


def _params(*semantics):
    return pltpu.CompilerParams(dimension_semantics=semantics or None, vmem_limit_bytes=VMEM_LIMIT_BYTES)


def _resident(shape, index):
    return pl.BlockSpec(shape, lambda *_: index, pipeline_mode=pl.Buffered(1))


def _rows(tm, width):
    return pl.BlockSpec((tm, width), lambda i: (i, 0))


def _nn(a, b):
    return jnp.dot(a, b, preferred_element_type=F32)


def _nt(a, b):
    return lax.dot_general(a, b, (((1,), (1,)), ((), ())), preferred_element_type=F32)


def _tn(a, b):
    return lax.dot_general(a, b, (((0,), (0,)), ((), ())), preferred_element_type=F32)


def _rms_fwd(x, g):
    r = lax.rsqrt(jnp.mean(x * x, axis=-1, keepdims=True) + EPS)
    return x * r * g


def _rms_bwd(x, g, dy):
    r = lax.rsqrt(jnp.mean(x * x, axis=-1, keepdims=True) + EPS)
    xh = x * r
    dg = jnp.sum(dy * xh, axis=0, keepdims=True)
    dxh = dy * g
    dx = r * (dxh - xh * jnp.mean(dxh * xh, axis=-1, keepdims=True))
    return dx, dg


SLAB = 16


def _slabs(n):
    return [slice(r, r + SLAB) for r in range(0, n, SLAB)]


def _rms_bwd_slabs(x_at, dy_at, g, n, n_sum, emit):
    acc = jnp.zeros((8, g.shape[1]), F32)
    for rows in _slabs(n):
        x = x_at(rows)
        dy = dy_at(rows)
        r = lax.rsqrt(jnp.mean(x * x, axis=-1, keepdims=True) + EPS)
        xh = x * r
        if rows.start < n_sum:
            p = dy * xh
            acc = acc + p[:8] + p[8:]
        dxh = dy * g
        emit(rows, r * (dxh - xh * jnp.mean(dxh * xh, axis=-1, keepdims=True)))
    return jnp.sum(acc, axis=0, keepdims=True)


def _gelu(z):
    phi = 0.5 + 0.5 * lax.erf(z * INV_SQRT2)
    e = jnp.exp2(z * z * (-0.5 * LOG2_E))
    return z * phi, phi + z * e * INV_SQRT_2PI


def _layernorm_stats(v):
    mu = jnp.mean(v, axis=-1, keepdims=True)
    xc = v - mu
    rs = lax.rsqrt(jnp.mean(xc * xc, axis=-1, keepdims=True) + EPS)
    return xc * rs, rs


def _tril_mask():
    r = lax.broadcasted_iota(jnp.int32, (CHUNK, CHUNK), 0)
    c = lax.broadcasted_iota(jnp.int32, (CHUNK, CHUNK), 1)
    return r >= c


class _Layout:
    def __init__(self, d, ffn_rows):
        self.ffn_rows = ffn_rows
        self.gate, self.up, self.down = [0] * DEPTH, [self.ffn_rows] * DEPTH, [2 * self.ffn_rows] * DEPTH
        self.f_total = 3 * self.ffn_rows
        self.a_in_rows, self.a_out_rows, self.b_rows = 4 * d // N_DEV, 2 * d // N_DEV, d // N_DEV
        self.a_in, self.a_out = [0, 0], [self.a_in_rows] * 2
        self.a_total = self.a_in_rows + self.a_out_rows
        self.b_in, self.b_out = [0, 0], [self.b_rows] * 2
        self.b_total = 2 * self.b_rows


def _wspec(rows, d):
    return _resident((N_DEV * rows, d), (0, 0))


def _a_fwd(x, gpre, wg, lay, j, lng, lnb, ws, bst, gpost, tm, name):
    t, d = x.shape
    aw = 2 * d
    nch = tm // CHUNK

    def body(x_ref, gpre_ref, win_ref, lng_ref, lnb_ref, ws_ref, bst_ref, wout_ref, gpost_ref,
             x1_ref, h1_ref, gp_ref, u_ref, vh_ref, rs_ref, gated_ref, m_ref):
        xv = x_ref[...]
        h1 = _rms_fwd(xv, gpre_ref[...]).astype(BF16)
        h1_ref[...] = h1
        z = _nt(h1, win_ref[...])
        u, du_dz = _gelu(z[:, :aw])
        v, dv_dz = _gelu(z[:, aw:])
        gp_ref[:, :aw] = du_dz.astype(BF16)
        gp_ref[:, aw:] = dv_dz.astype(BF16)
        u_ref[...] = u.astype(BF16)
        vh, rs = _layernorm_stats(v)
        vh_ref[...] = vh.astype(BF16)
        rs_ref[...] = jnp.broadcast_to(rs, rs_ref.shape)
        vn = (vh * lng_ref[...] + lnb_ref[...]).astype(BF16)
        mask = _tril_mask()
        for g in range(A_GROUPS):
            wm = jnp.where(mask, ws_ref[g], 0.0).astype(BF16)
            cols = slice(g * A_GROUP_DIM, (g + 1) * A_GROUP_DIM)
            for c in range(nch):
                rows = slice(c * CHUNK, (c + 1) * CHUNK)
                sv = _nn(wm, vn[rows, cols]) + bst_ref[:, g:g + 1]
                gated_ref[rows, cols] = (u[rows, cols] * sv).astype(BF16)
        m = _nn(gated_ref[...], wout_ref[...])
        m_ref[...] = m
        x1_ref[...] = xv + _rms_fwd(m, gpost_ref[...])

    vec = lambda w: _resident((1, w), (0, 0))
    return _call(
        body, name=name, grid=(t // tm,),
        in_specs=[_rows(tm, d), vec(d), _wspec(lay.a_in_rows, d), vec(aw), vec(aw),
                  _resident((A_GROUPS, CHUNK, CHUNK), (0, 0, 0)), _resident((CHUNK, A_GROUPS), (0, 0)),
                  _wspec(lay.a_out_rows, d), vec(d)],
        out_specs=[_rows(tm, d), _rows(tm, d), _rows(tm, 2 * aw), _rows(tm, aw), _rows(tm, aw), _rows(tm, 128),
                   _rows(tm, aw), _rows(tm, d)],
        out_shape=[jax.ShapeDtypeStruct((t, d), F32), jax.ShapeDtypeStruct((t, d), BF16),
                   jax.ShapeDtypeStruct((t, 2 * aw), BF16), jax.ShapeDtypeStruct((t, aw), BF16),
                   jax.ShapeDtypeStruct((t, aw), BF16), jax.ShapeDtypeStruct((t, 128), F32),
                   jax.ShapeDtypeStruct((t, aw), BF16), jax.ShapeDtypeStruct((t, d), F32)],
        compiler_params=_params("parallel"),
    )(x, gpre, wg[0], lng, lnb, ws, bst, wg[1], gpost)


def _a_bwd(dx1, m, x, gp, u, vh, rs, gpre, wg, lay, j, lng, lnb, ws, bst, gpost, after, tm, name):
    t, d = x.shape
    aw = 2 * d
    nch = tm // CHUNK

    def body(dx1_ref, m_ref, x_ref, gp_ref, u_ref, vh_ref, rs_ref, gpre_ref, win_ref, lng_ref, lnb_ref, ws_ref, bst_ref,
             wout_ref, gpost_ref, after_ref,
             dx_ref, dm_ref, dz_ref, dgpost_ref, dgpre_ref, dlng_ref, dlnb_ref, dws_ref, dbt_ref, dvn_ref):
        @pl.when(pl.program_id(0) == 0)
        def _():
            for r in (dgpost_ref, dgpre_ref, dlng_ref, dlnb_ref, dws_ref, dbt_ref):
                r[...] = jnp.zeros_like(r)

        def put_dm(rows, dx):
            dm_ref[rows, :] = dx.astype(BF16)

        dgpost_ref[...] += _rms_bwd_slabs(lambda rows: m_ref[rows, :], lambda rows: dx1_ref[rows, :], gpost_ref[...],
                                          tm, tm, put_dm)
        dgated = _nt(dm_ref[...], wout_ref[...])

        vh = vh_ref[...].astype(F32)
        rs = rs_ref[:, :1]
        lng_v = lng_ref[...]
        vn = (vh * lng_v + lnb_ref[...]).astype(BF16)
        mask = _tril_mask()
        lane = lax.broadcasted_iota(jnp.int32, (CHUNK, CHUNK), 1)
        for g in range(A_GROUPS):
            wm = jnp.where(mask, ws_ref[g], 0.0).astype(BF16)
            cols = slice(g * A_GROUP_DIM, (g + 1) * A_GROUP_DIM)
            dws_g = jnp.zeros((CHUNK, CHUNK), F32)
            db_g = jnp.zeros((CHUNK, 1), F32)
            for c in range(nch):
                rows = slice(c * CHUNK, (c + 1) * CHUNK)
                vn_cg = vn[rows, cols]
                sv = _nn(wm, vn_cg) + bst_ref[:, g:g + 1]
                dg_cg = dgated[rows, cols]
                dsv = dg_cg * u_ref[rows, cols].astype(F32)
                dsv_bf = dsv.astype(BF16)
                db_g = db_g + jnp.sum(dsv, axis=1, keepdims=True)
                dws_g = dws_g + _nt(dsv_bf, vn_cg)
                dvn_ref[rows, cols] = _tn(wm, dsv_bf)
                dz_ref[rows, cols] = (dg_cg * sv * gp_ref[rows, cols].astype(F32)).astype(BF16)
            dws_ref[g] += jnp.where(mask, dws_g, 0.0)
            dbt_ref[...] += jnp.where(lane == g, db_g, 0.0)
        dvn = dvn_ref[...]
        dlng_ref[...] += jnp.sum(dvn * vh, axis=0, keepdims=True)
        dlnb_ref[...] += jnp.sum(dvn, axis=0, keepdims=True)
        dvh = dvn * lng_v
        dv = rs * (dvh - jnp.mean(dvh, axis=-1, keepdims=True) - vh * jnp.mean(dvh * vh, axis=-1, keepdims=True))
        dz_ref[:, aw:] = (dv * gp_ref[:, aw:].astype(F32)).astype(BF16)
        dh1 = _nn(dz_ref[...], win_ref[...])

        def put_dx(rows, dx):
            dx_ref[rows, :] = dx1_ref[rows, :] + dx

        dgpre_ref[...] += _rms_bwd_slabs(lambda rows: x_ref[rows, :], lambda rows: dh1[rows, :], gpre_ref[...],
                                         tm, tm, put_dx)

    vec = lambda w: _resident((1, w), (0, 0))
    acc = lambda shape: pl.BlockSpec(shape, lambda i: (0,) * len(shape))
    return _call(
        body, name=name, grid=(t // tm,),
        in_specs=[_rows(tm, d), _rows(tm, d), _rows(tm, d), _rows(tm, 2 * aw), _rows(tm, aw), _rows(tm, aw),
                  _rows(tm, 128), vec(d), _wspec(lay.a_in_rows, d), vec(aw), vec(aw),
                  _resident((A_GROUPS, CHUNK, CHUNK), (0, 0, 0)), _resident((CHUNK, A_GROUPS), (0, 0)),
                  _wspec(lay.a_out_rows, d), vec(d), ANY],
        out_specs=[_rows(tm, d), _rows(tm, d), _rows(tm, 2 * aw), acc((1, d)), acc((1, d)), acc((1, aw)), acc((1, aw)),
                   acc((A_GROUPS, CHUNK, CHUNK)), acc((CHUNK, CHUNK))],
        out_shape=[jax.ShapeDtypeStruct((t, d), F32), jax.ShapeDtypeStruct((t, d), BF16),
                   jax.ShapeDtypeStruct((t, 2 * aw), BF16), jax.ShapeDtypeStruct((1, d), F32),
                   jax.ShapeDtypeStruct((1, d), F32), jax.ShapeDtypeStruct((1, aw), F32),
                   jax.ShapeDtypeStruct((1, aw), F32), jax.ShapeDtypeStruct((A_GROUPS, CHUNK, CHUNK), F32),
                   jax.ShapeDtypeStruct((CHUNK, CHUNK), F32)],
        scratch_shapes=[pltpu.VMEM((tm, aw), F32)],
        compiler_params=_params("arbitrary"),
    )(dx1, m, x, gp, u, vh, rs, gpre, wg[0], lng, lnb, ws, bst, wg[1], gpost, after)


def _window_counts(first_row, n, win):
    tpos = first_row + lax.broadcasted_iota(jnp.int32, (n, 1), 0)
    return jnp.clip(tpos + 1, 1, win).astype(F32)


def _b_fwd(x, gpre, wg, lay, j, wgrp, scale, gpost, tm, name):
    t, d = x.shape
    n = tm + HALO
    ngrp = len(B_WINDOWS)

    def body(x_ref, xprev_ref, gpre_ref, win_ref, wgrp_ref, scale_ref, wout_ref, gpost_ref,
             x1_ref, h1_ref, pooled_ref, mixed_ref, m_ref):
        i = pl.program_id(0)
        xv = x_ref[...]
        keep = jnp.where(i > 0, 1.0, 0.0)
        xe = jnp.concatenate([xprev_ref[...] * keep, xv], axis=0)
        h1e = _rms_fwd(xe, gpre_ref[...]).astype(BF16)
        h1_ref[...] = h1e[HALO:]
        p = _nn(h1e, win_ref[...])
        acc = p
        shift = 1
        for g, win in enumerate(B_WINDOWS):
            lo = g * B_GROUP_DIM
            if g > 0:
                acc = acc[:, B_GROUP_DIM:]
            while shift < win:
                acc = acc + pltpu.roll(acc, shift, 0)
                shift *= 2
            cnt = _window_counts(i * tm - HALO, n, win)
            pooled = acc[:, :B_GROUP_DIM] / cnt - p[:, lo:lo + B_GROUP_DIM]
            pooled_ref[:, lo:lo + B_GROUP_DIM] = pooled[HALO:].astype(BF16)
        for g in range(ngrp):
            cols = slice(g * B_GROUP_DIM, (g + 1) * B_GROUP_DIM)
            raw = _nn(pooled_ref[:, cols], wgrp_ref[g])
            mixed_ref[:, cols] = (raw * scale_ref[:, cols]).astype(BF16)
        m = _nn(mixed_ref[...], wout_ref[...])
        m_ref[...] = m
        x1_ref[...] = xv + _rms_fwd(m, gpost_ref[...])

    vec = lambda w: _resident((1, w), (0, 0))
    per = tm // HALO
    return _call(
        body, name=name, grid=(t // tm,),
        in_specs=[_rows(tm, d), pl.BlockSpec((HALO, d), lambda i: (jnp.maximum(i * per - 1, 0), 0)), vec(d),
                  _wspec(lay.b_rows, d), _resident((ngrp, B_GROUP_DIM, B_GROUP_DIM), (0, 0, 0)), vec(d),
                  _wspec(lay.b_rows, d), vec(d)],
        out_specs=[_rows(tm, d)] * 5,
        out_shape=[jax.ShapeDtypeStruct((t, d), F32), jax.ShapeDtypeStruct((t, d), BF16),
                   jax.ShapeDtypeStruct((t, d), BF16), jax.ShapeDtypeStruct((t, d), BF16),
                   jax.ShapeDtypeStruct((t, d), F32)],
        compiler_params=_params("parallel"),
    )(x, x, gpre, wg[0], wgrp, scale, wg[1], gpost)


def _b_bwd(dx1, m, x, pooled, gpre, wg, lay, j, wgrp, scale, gpost, after, tm, name):
    t, d = x.shape
    n = tm + HALO
    ngrp = len(B_WINDOWS)
    steps = t // tm

    def body(dx1_ref, dx1n_ref, m_ref, mn_ref, x_ref, pooled_ref, pooledn_ref, gpre_ref, win_ref, wgrp_ref, scale_ref,
             wout_ref, gpost_ref, after_ref,
             dx_ref, dm_ref, draw_ref, dp_ref, dgpost_ref, dgpre_ref, dscale_ref, dpool_ref):
        i = pl.program_id(0)

        @pl.when(i == 0)
        def _():
            for r in (dgpost_ref, dgpre_ref, dscale_ref):
                r[...] = jnp.zeros_like(r)

        keep = jnp.where(i < steps - 1, 1.0, 0.0)
        dy = dx1_ref[...]
        dye = jnp.concatenate([dy, dx1n_ref[...] * keep], axis=0)
        me = jnp.concatenate([m_ref[...], mn_ref[...]], axis=0)
        gpost_v = gpost_ref[...]
        r = lax.rsqrt(jnp.mean(me * me, axis=-1, keepdims=True) + EPS)
        mh = me * r
        dgpost_ref[...] += jnp.sum((dye * mh)[:tm], axis=0, keepdims=True)
        dmh = dye * gpost_v
        dme = (r * (dmh - mh * jnp.mean(dmh * mh, axis=-1, keepdims=True))).astype(BF16)
        dm_ref[...] = dme[:tm]
        dmixed = _nt(dme, wout_ref[...])
        pooled_e = jnp.concatenate([pooled_ref[...], pooledn_ref[...]], axis=0)
        scale_v = scale_ref[...]
        for g, win in enumerate(B_WINDOWS):
            cols = slice(g * B_GROUP_DIM, (g + 1) * B_GROUP_DIM)
            raw = _nn(pooled_e[:, cols], wgrp_ref[g])
            dscale_ref[:, cols] += jnp.sum((dmixed[:, cols] * raw)[:tm], axis=0, keepdims=True)
            draw = (dmixed[:, cols] * scale_v[:, cols]).astype(BF16)
            draw_ref[:, cols] = draw[:tm]
            dpool = _nt(draw, wgrp_ref[g])
            acc = dpool / _window_counts(i * tm, n, win)
            shift = 1
            while shift < win:
                acc = acc + pltpu.roll(acc, n - shift, 0)
                shift *= 2
            dpool_ref[:, cols] = (acc - dpool)[:tm]
        dp = dpool_ref[...].astype(BF16)
        dp_ref[...] = dp
        dh1 = _nt(dp, win_ref[...])
        dxp, dgpre = _rms_bwd(x_ref[...], gpre_ref[...], dh1)
        dgpre_ref[...] += dgpre
        dx_ref[...] = dy + dxp

    vec = lambda w: _resident((1, w), (0, 0))
    acc = lambda shape: pl.BlockSpec(shape, lambda i: (0,) * len(shape))
    per = tm // HALO
    nxt = lambda i: (jnp.minimum((i + 1) * per, t // HALO - 1), 0)
    return _call(
        body, name=name, grid=(steps,),
        in_specs=[_rows(tm, d), pl.BlockSpec((HALO, d), nxt), _rows(tm, d), pl.BlockSpec((HALO, d), nxt), _rows(tm, d),
                  _rows(tm, d), pl.BlockSpec((HALO, d), nxt), vec(d), _wspec(lay.b_rows, d),
                  _resident((ngrp, B_GROUP_DIM, B_GROUP_DIM), (0, 0, 0)), vec(d), _wspec(lay.b_rows, d),
                  vec(d), ANY],
        out_specs=[_rows(tm, d)] * 4 + [acc((1, d))] * 3,
        out_shape=[jax.ShapeDtypeStruct((t, d), F32), jax.ShapeDtypeStruct((t, d), BF16),
                   jax.ShapeDtypeStruct((t, d), BF16), jax.ShapeDtypeStruct((t, d), BF16)]
                  + [jax.ShapeDtypeStruct((1, d), F32)] * 3,
        scratch_shapes=[pltpu.VMEM((tm, d), F32)],
        compiler_params=_params("arbitrary"),
    )(dx1, dx1, m, m, x, pooled, pooled, gpre, wg[0], wgrp, scale, wg[1], gpost, after)


def _f_fwd(x1, gpre, wg, lay, l, gpost, tm, name, target=None):
    t, d = x1.shape
    hid = N_DEV * lay.ffn_rows
    head = target is not None

    def body(x_ref, gpre_ref, wgate_ref, wup_ref, wdown_ref, gpost_ref, *rest):
        x2_ref, h2_ref, abs_ref, f_ref = rest[-5:-1] if head else rest
        xv = x_ref[...]
        h2 = _rms_fwd(xv, gpre_ref[...]).astype(BF16)
        h2_ref[...] = h2
        a = _nt(h2, wgate_ref[...])
        b = _nt(h2, wup_ref[...])
        sig = jax.nn.sigmoid(a)
        silu = a * sig
        abs_ref[:, :hid] = (b * (sig + silu * (1.0 - sig))).astype(BF16)
        abs_ref[:, hid:2 * hid] = silu.astype(BF16)
        s = (silu * b).astype(BF16)
        abs_ref[:, 2 * hid:] = s
        f = _nn(s, wdown_ref[...])
        f_ref[...] = f
        x2 = xv + _rms_fwd(f, gpost_ref[...])
        if head:
            target_ref, loss_ref = rest[0], rest[-1]

            @pl.when(pl.program_id(0) == 0)
            def _():
                loss_ref[...] = jnp.zeros_like(loss_ref)

            diff = x2 - target_ref[...]
            x2_ref[...] = diff * (1.0 / d)
            sq = jnp.sum(jnp.sum(diff * diff, axis=0, keepdims=True), axis=1, keepdims=True)
            loss_ref[...] += sq * (0.5 / d)
        else:
            x2_ref[...] = x2

    vec = lambda w: _resident((1, w), (0, 0))
    return _call(
        body, name=name, grid=(t // tm,),
        in_specs=[_rows(tm, d), vec(d), _wspec(lay.ffn_rows, d), _wspec(lay.ffn_rows, d),
                  _wspec(lay.ffn_rows, d), vec(d)] + ([_rows(tm, d)] if head else []),
        out_specs=[_rows(tm, d), _rows(tm, d), _rows(tm, 3 * hid), _rows(tm, d)]
                  + ([pl.BlockSpec((8, 128), lambda i: (0, 0))] if head else []),
        out_shape=[jax.ShapeDtypeStruct((t, d), F32), jax.ShapeDtypeStruct((t, d), BF16),
                   jax.ShapeDtypeStruct((t, 3 * hid), BF16), jax.ShapeDtypeStruct((t, d), F32)]
                  + ([jax.ShapeDtypeStruct((8, 128), F32)] if head else []),
        compiler_params=_params("arbitrary" if head else "parallel"),
    )(x1, gpre, wg[0], wg[1], wg[2], gpost, *([target] if head else []))


def _f_bwd(dx2, f, x1, acts, gpre, wg, lay, l, gpost, after, tm, name):
    t, d = x1.shape
    hid = N_DEV * lay.ffn_rows

    def body(dx2_ref, f_ref, x_ref, ab_ref, gpre_ref, wgate_ref, wup_ref, wdown_ref, gpost_ref, after_ref,
             dx1_ref, df_ref, dab_ref, dgpost_ref, dgpre_ref):
        @pl.when(pl.program_id(0) == 0)
        def _():
            dgpost_ref[...] = jnp.zeros_like(dgpost_ref)
            dgpre_ref[...] = jnp.zeros_like(dgpre_ref)

        def put_df(rows, dx):
            df_ref[rows, :] = dx.astype(BF16)

        dgpost_ref[...] += _rms_bwd_slabs(lambda rows: f_ref[rows, :], lambda rows: dx2_ref[rows, :], gpost_ref[...],
                                          tm, tm, put_df)
        ds = _nt(df_ref[...], wdown_ref[...])
        dab_ref[:, :hid] = (ds * ab_ref[:, :hid].astype(F32)).astype(BF16)
        dab_ref[:, hid:] = (ds * ab_ref[:, hid:].astype(F32)).astype(BF16)
        dh2 = _nn(dab_ref[:, :hid], wgate_ref[...]) + _nn(dab_ref[:, hid:], wup_ref[...])

        def put_dx(rows, dx):
            dx1_ref[rows, :] = dx2_ref[rows, :] + dx

        dgpre_ref[...] += _rms_bwd_slabs(lambda rows: x_ref[rows, :], lambda rows: dh2[rows, :], gpre_ref[...],
                                         tm, tm, put_dx)

    vec = lambda w: _resident((1, w), (0, 0))
    acc = pl.BlockSpec((1, d), lambda i: (0, 0))
    return _call(
        body, name=name, grid=(t // tm,),
        in_specs=[_rows(tm, d), _rows(tm, d), _rows(tm, d), _rows(tm, 2 * hid), vec(d),
                  _wspec(lay.ffn_rows, d), _wspec(lay.ffn_rows, d),
                  _wspec(lay.ffn_rows, d), vec(d), ANY],
        out_specs=[_rows(tm, d), _rows(tm, d), _rows(tm, 2 * hid), acc, acc],
        out_shape=[jax.ShapeDtypeStruct((t, d), F32), jax.ShapeDtypeStruct((t, d), BF16),
                   jax.ShapeDtypeStruct((t, 2 * hid), BF16),
                   jax.ShapeDtypeStruct((1, d), F32), jax.ShapeDtypeStruct((1, d), F32)],
        compiler_params=_params("arbitrary"),
    )(dx2, f, x1, acts, gpre, wg[0], wg[1], wg[2], gpost, after)


def _grad_into(gbuf, lhs, rhs, off, rows, name, after=None, lhs_off=0):
    t = lhs.shape[0]
    d = rhs.shape[1]
    per_tile = {352: 4, 512: 2, 256: 4, 128: 8}[rows]
    tm = per_tile * rows
    assert lhs.shape[1] >= lhs_off + N_DEV * rows and off % rows == 0 and lhs_off % tm == 0
    assert tm % 128 == 0 and rows % 16 == 0
    tk = 2048 if t % 2048 == 0 else 256
    ksteps = t // tk
    fresh = isinstance(gbuf, int)
    shape = (N_DEV, gbuf, d) if fresh else gbuf.shape
    extra = ([] if fresh else [gbuf]) + ([] if after is None else [after])

    def body(l_ref, r_ref, *rest):
        o_ref, acc_ref = rest[-2:]
        k = pl.program_id(1)

        @pl.when(k == 0)
        def _():
            acc_ref[...] = jnp.zeros_like(acc_ref)

        acc_ref[...] += _tn(l_ref[...], r_ref[...])

        @pl.when(k == ksteps - 1)
        def _():
            o_ref[...] = acc_ref[...].reshape(per_tile, rows, d).astype(BF16)

    return _call(
        body, name=name, grid=(N_DEV // per_tile, ksteps),
        in_specs=[pl.BlockSpec((tk, tm), lambda i, k: (k, i + lhs_off // tm)), pl.BlockSpec((tk, d), lambda i, k: (k, 0))]
                 + [ANY] * len(extra),
        out_specs=pl.BlockSpec((per_tile, rows, d), lambda i, k: (i, off // rows, 0)),
        out_shape=jax.ShapeDtypeStruct(shape, BF16),
        scratch_shapes=[pltpu.VMEM((tm, d), F32)],
        input_output_aliases={} if fresh else {2: 0},
        compiler_params=_params("parallel", "arbitrary"),
    )(lhs, rhs, *extra)


def _grad_grouped(pooled, draw, name):
    t, d = pooled.shape
    ngrp = len(B_WINDOWS)
    tk = 1024 if t % 1024 == 0 else 256

    def body(p_ref, q_ref, o_ref):
        @pl.when(pl.program_id(0) == 0)
        def _():
            o_ref[...] = jnp.zeros_like(o_ref)

        for g in range(ngrp):
            cols = slice(g * B_GROUP_DIM, (g + 1) * B_GROUP_DIM)
            o_ref[g] += _tn(p_ref[:, cols], q_ref[:, cols])

    return _call(
        body, name=name, grid=(t // tk,),
        in_specs=[_rows(tk, d), _rows(tk, d)],
        out_specs=pl.BlockSpec((ngrp, B_GROUP_DIM, B_GROUP_DIM), lambda i: (0, 0, 0)),
        out_shape=jax.ShapeDtypeStruct((ngrp, B_GROUP_DIM, B_GROUP_DIM), F32),
        compiler_params=_params("arbitrary"),
    )(pooled, draw)


def _peers():
    x, y, c = lax.axis_index("x"), lax.axis_index("y"), lax.axis_index("c")
    flip = lambda v, f: 1 - v if f else v
    peers = []
    for r in range(1, N_DEV):
        px, py, pc = flip(x, r & 4), flip(y, r & 2), flip(c, r & 1)
        peers.append(((px, py, pc), 4 * px + 2 * py + pc))
    return 4 * x + 2 * y + c, peers


HBM = pl.BlockSpec(memory_space=pltpu.HBM)
SEM = pl.BlockSpec(memory_space=pltpu.SEMAPHORE)
EFFECT = pltpu.SideEffectType.DATAFLOW_SIDE_EFFECTING


def _peer_copies(scatter, srcs, lands, send_sems, recv_sems):
    me, peers = _peers()
    copies = []
    for a in range(len(srcs)):
        rows = srcs[a].shape[0]
        block = lambda k: lands[a].at[pl.ds(pl.multiple_of(k * rows, 8), rows)]
        for r, (peer, pidx) in enumerate(peers):
            src = srcs[a].at[pidx] if scatter else srcs[a]
            mine = lands[a].at[r] if scatter else block(pidx)
            theirs = lands[a].at[r] if scatter else block(me)
            send = pltpu.make_async_remote_copy(src_ref=src, dst_ref=theirs, send_sem=send_sems[a].at[r],
                                                recv_sem=recv_sems[a].at[r], device_id=peer, device_id_type=MESH)
            recv = pltpu.make_async_remote_copy(src_ref=src, dst_ref=mine, send_sem=send_sems[a].at[r],
                                                recv_sem=recv_sems[a].at[r], device_id=peer, device_id_type=MESH)
            copies.append((send, recv))
    return copies


def _own_copies(srcs, lands, send_sems):
    me, _ = _peers()
    copies = []
    for a in range(len(srcs)):
        rows = srcs[a].shape[0]
        copies.append(pltpu.make_async_copy(srcs[a], lands[a].at[pl.ds(pl.multiple_of(me * rows, 8), rows)],
                                            send_sems[a].at[N_DEV - 1]))
    return copies


def _exchange_start(scatter, srcs, lands, after, name):
    n = len(srcs)

    def body(*refs):
        src_refs, land_refs = refs[:n], refs[n:2 * n]
        outs = refs[2 * n + 1:]
        send_sems, recv_sems, token = outs[:n], outs[n:2 * n], outs[-1]
        for send, _ in _peer_copies(scatter, src_refs, land_refs, send_sems, recv_sems):
            send.start()
        if not scatter:
            for own in _own_copies(src_refs, land_refs, send_sems):
                own.start()
        token[...] = jnp.zeros_like(token)

    hbm = lambda a: pltpu.with_memory_space_constraint(a, pltpu.HBM)
    res = _call(
        body, name=name,
        in_specs=[HBM] * (2 * n) + [ANY],
        out_specs=[SEM] * (2 * n) + [HBM] * (2 * n) + [pl.BlockSpec(memory_space=pltpu.VMEM)],
        out_shape=[pltpu.SemaphoreType.DMA((N_DEV,))] * (2 * n)
                  + [pltpu.HBM(a.shape, a.dtype) for a in list(srcs) + list(lands)]
                  + [jax.ShapeDtypeStruct((8, 128), F32)],
        input_output_aliases={i: 2 * n + i for i in range(2 * n)},
        compiler_params=pltpu.CompilerParams(has_side_effects=EFFECT),
    )(*[hbm(a) for a in srcs], *[hbm(a) for a in lands], after)
    return res[:n], res[n:2 * n], res[2 * n:3 * n], res[3 * n:4 * n], res[-1]


def _exchange_wait(scatter, send_sems, recv_sems, srcs, lands, after, name):
    n = len(srcs)
    after = list(after) if isinstance(after, (list, tuple)) else [after]

    def body(*refs):
        src_refs, land_refs = refs[:n], refs[n:2 * n]
        send_refs, recv_refs = refs[2 * n:3 * n], refs[3 * n:4 * n]
        for send, recv in _peer_copies(scatter, src_refs, land_refs, send_refs, recv_refs):
            send.wait_send()
            recv.wait_recv()
        if not scatter:
            for own in _own_copies(src_refs, land_refs, send_refs):
                own.wait()

    res = _call(
        body, name=name,
        in_specs=[HBM] * (2 * n) + [SEM] * (2 * n) + [ANY] * len(after),
        out_specs=[HBM] * (2 * n),
        out_shape=[pltpu.HBM(a.shape, a.dtype) for a in list(srcs) + list(lands)],
        input_output_aliases={i: i for i in range(2 * n)},
        compiler_params=pltpu.CompilerParams(has_side_effects=EFFECT),
    )(*srcs, *lands, *send_sems, *recv_sems, *after)
    return res[:n], res[n:]


def _row_tile(rows):
    if rows <= 512:
        return rows
    return max([tr for tr in range(16, 513, 16) if rows % tr == 0] or [rows])


def _sum_parts(own, got, me, name):
    _, rows, w = own.shape
    tr = _row_tile(rows)

    def body(me_ref, a_ref, b_ref, o_ref):
        s = a_ref[...].astype(F32)
        for j in range(N_DEV - 1):
            s = s + b_ref[j].astype(F32)
        o_ref[...] = s

    return _call(
        body, name=name,
        grid_spec=pltpu.PrefetchScalarGridSpec(
            num_scalar_prefetch=1, grid=(rows // tr,),
            in_specs=[pl.BlockSpec((None, tr, w), lambda i, me_ref: (me_ref[0], i, 0)),
                      pl.BlockSpec((N_DEV - 1, tr, w), lambda i, me_ref: (0, i, 0))],
            out_specs=pl.BlockSpec((tr, w), lambda i, me_ref: (i, 0))),
        out_shape=jax.ShapeDtypeStruct((rows, w), F32),
        compiler_params=_params("parallel"),
    )(me, own, got)


def _sum_devices(stacked, name):
    k, rows, w = stacked.shape
    tr = _row_tile(rows)

    def body(a_ref, o_ref):
        s = a_ref[0]
        for j in range(1, k):
            s = s + a_ref[j]
        o_ref[...] = s

    return _call(
        body, name=name, grid=(rows // tr,),
        in_specs=[pl.BlockSpec((k, tr, w), lambda i: (0, i, 0))],
        out_specs=pl.BlockSpec((tr, w), lambda i: (i, 0)),
        out_shape=jax.ShapeDtypeStruct((rows, w), F32),
        compiler_params=_params("parallel"),
    )(stacked)


def _adamw(w, g, m, v, name):
    rows, cols = w.shape
    tr = _row_tile(rows)

    def body(w_ref, g_ref, m_ref, v_ref, d_ref, nm_ref, nv_ref):
        gv = g_ref[...]
        nm = ADAM_B1 * m_ref[...] + (1.0 - ADAM_B1) * gv
        nv = ADAM_B2 * v_ref[...] + (1.0 - ADAM_B2) * (gv * gv)
        m_hat = nm / (1.0 - ADAM_B1 ** ADAM_STEP)
        v_hat = nv / (1.0 - ADAM_B2 ** ADAM_STEP)
        d_ref[...] = -ADAM_LR * (m_hat / (jnp.sqrt(v_hat) + ADAM_EPS) + ADAM_WD * w_ref[...])
        nm_ref[...] = nm
        nv_ref[...] = nv

    spec = pl.BlockSpec((tr, cols), lambda i: (i, 0))
    return _call(
        body, name=name, grid=(rows // tr,),
        in_specs=[spec] * 4, out_specs=[spec] * 3,
        out_shape=[jax.ShapeDtypeStruct((rows, cols), F32)] * 3,
        compiler_params=_params("parallel"),
    )(w, g, m, v)


SMALL = ("a_ln_g", "a_ln_b", "a_w_s", "a_b_s", "mix_pre_g", "mix_post_g", "ffn_pre_g", "ffn_post_g")


def _pack_small(parts, d, last_row=None):
    rows = [parts[k].reshape(-1, d) for k in SMALL] + ([] if last_row is None else [last_row])
    flat = jnp.concatenate(rows, axis=0)
    return jnp.pad(flat, ((0, -flat.shape[0] % 8), (0, 0)))


def _unpack_small(flat, like):
    out, r = {}, 0
    for k in SMALL:
        n = like[k].size // flat.shape[1]
        out[k] = flat[r:r + n].reshape(like[k].shape)
        r += n
    return out


def kernel(x, a_w_in, a_ln_g, a_ln_b, a_w_s, a_b_s, a_w_out, b_w_in, b_w_grp, b_scale, b_w_out, mix_pre_g, mix_post_g, ffn_pre_g, ffn_post_g, ffn_w_gate, ffn_w_up, ffn_w_down, loss_target, m_a_w_in, m_a_ln_g, m_a_ln_b, m_a_w_s, m_a_b_s, m_a_w_out, m_b_w_in, m_b_w_grp, m_b_scale, m_b_w_out, m_mix_pre_g, m_mix_post_g, m_ffn_pre_g, m_ffn_post_g, m_ffn_w_gate, m_ffn_w_up, m_ffn_w_down, v_a_w_in, v_a_ln_g, v_a_ln_b, v_a_w_s, v_a_b_s, v_a_w_out, v_b_w_in, v_b_w_grp, v_b_scale, v_b_w_out, v_mix_pre_g, v_mix_post_g, v_ffn_pre_g, v_ffn_post_g, v_ffn_w_gate, v_ffn_w_up, v_ffn_w_down):
    args = dict(locals())
    names = ("a_w_in", "a_ln_g", "a_ln_b", "a_w_s", "a_b_s", "a_w_out", "b_w_in", "b_w_grp", "b_scale", "b_w_out",
             "mix_pre_g", "mix_post_g", "ffn_pre_g", "ffn_post_g", "ffn_w_gate", "ffn_w_up", "ffn_w_down")
    w = {k: args[k] for k in names}
    mom = {k: args["m_" + k] for k in names}
    var = {k: args["v_" + k] for k in names}

    t, d = x.shape[1], x.shape[2]
    ffn_local = ffn_w_gate.shape[2]
    lay = _Layout(d, ffn_local)
    me = 4 * lax.axis_index("x") + 2 * lax.axis_index("y") + lax.axis_index("c")
    me1 = jnp.reshape(me, (1,)).astype(jnp.int32)

    def landing(block):
        return lax.empty((N_DEV * block.shape[0],) + block.shape[1:], block.dtype)

    def shards(i, mixer, zero):
        j = i // 2
        if not mixer:
            parts = [ffn_w_gate[i].T, ffn_w_up[i].T, ffn_w_down[i]]
        elif i % 2 == 0:
            parts = [a_w_in[j].T, a_w_out[j]]
        else:
            parts = [b_w_in[j], b_w_out[j]]
        return [(p + zero).astype(BF16) for p in parts]

    nsub = 2 * DEPTH
    wg = [None] * nsub
    first = shards(0, True, 0.0)
    first = _exchange_start(False, first, [landing(b) for b in first], jnp.zeros((8, 128), F32), "gather_first_start")
    zero = first[4][0, 0]
    ngrp = len(B_WINDOWS)
    grp_local = b_w_grp.shape[2]
    sdev = b_scale.shape[1]
    side_rows = 2 * ngrp * grp_local
    side = jnp.concatenate(
        [b_w_grp.reshape(side_rows, B_GROUP_DIM),
         jnp.pad(b_scale, ((0, 6), (0, B_GROUP_DIM - sdev)))], axis=0) + zero
    later, where = [side], [slice(0, 1)]
    for k in range(1, nsub):
        new = shards(k // 2, k % 2 == 0, zero)
        where.append(slice(len(later), len(later) + len(new)))
        later += new
    send_sems, recv_sems, later, zones, token = _exchange_start(
        False, later, [landing(b) for b in later], first[4], "gather_start")
    turned = ("a_w_in", "ffn_w_gate", "ffn_w_up")
    turn = lambda a: jnp.swapaxes(a, 1, 2)
    state = {k: tuple(turn(a[k]) for a in (w, mom, var)) for k in turned}
    state["small"] = tuple(_pack_small(a, d) for a in (w, mom, var))
    ready = [a for group in state.values() for a in group]
    _, wg[0] = _exchange_wait(False, *first[:4], [token] + ready, "gather_first_wait")

    def gathered(k, after):
        s = where[k]
        _, got = _exchange_wait(False, send_sems[s], recv_sems[s], later[s], zones[s], after, f"gather_wait_{k}")
        return got

    row = lambda a: a.reshape(1, -1)
    bst = jnp.transpose(a_b_s, (0, 2, 1))

    tm = 256 if t % 256 == 0 else CHUNK
    tm_abwd = tm
    tm_b = 512 if t % 512 == 0 else tm
    tm_f = tm

    saved = []
    h = x[0]
    wgrp_full = scale_full = None
    for i in range(DEPTH):
        j = i // 2
        gpre = row(mix_pre_g[i])
        if i > 0:
            wg[2 * i] = gathered(2 * i, h)
        if i % 2 == 0:
            x1, h1, gp, u, vh, rs, gated, m = _a_fwd(h, gpre, wg[2 * i], lay, j, row(a_ln_g[j]), row(a_ln_b[j]),
                                                     a_w_s[j], bst[j], row(mix_post_g[i]), tm, f"a_fwd_{j}")
            mix = dict(h1=h1, gp=gp, u=u, vh=vh, rs=rs, gated=gated, m=m)
        else:
            if wgrp_full is None:
                side_g = gathered(0, h)[0].reshape(N_DEV, side_rows + 8, B_GROUP_DIM)
                wgrp_full = (side_g[:, :side_rows].reshape(N_DEV, 2, ngrp, grp_local, B_GROUP_DIM)
                             .transpose(1, 2, 0, 3, 4).reshape(2, ngrp, B_GROUP_DIM, B_GROUP_DIM).astype(BF16))
                scale_full = (side_g[:, side_rows:side_rows + 2, :sdev].transpose(1, 0, 2)
                              .reshape(2, 1, N_DEV * sdev))
            x1, h1, pooled, mixed, m = _b_fwd(h, gpre, wg[2 * i], lay, j, wgrp_full[j], scale_full[j],
                                              row(mix_post_g[i]), tm_b, f"b_fwd_{j}")
            mix = dict(h1=h1, pooled=pooled, mixed=mixed, m=m)
        wg[2 * i + 1] = gathered(2 * i + 1, x1)
        x2, h2, acts, f, *loss_acc = _f_fwd(x1, row(ffn_pre_g[i]), wg[2 * i + 1], lay, i, row(ffn_post_g[i]), tm_f,
                                               f"f_fwd_{i}", loss_target[0] if i == DEPTH - 1 else None)
        saved.append(dict(x=h, x1=x1, mix=mix, h2=h2, acts=acts, f=f))
        h = x2
    dy, (loss_acc,) = h, loss_acc

    small_g = {k: [None] * w[k].shape[0] for k in SMALL}
    dgrp, dscale = [None, None], [None, None]
    pending = [None] * nsub
    token = jnp.zeros((8, 128), F32)

    def scatter(k, gbuf):
        got = pltpu.with_memory_space_constraint(lax.empty((N_DEV - 1,) + gbuf.shape[1:], gbuf.dtype), pltpu.HBM)
        ss, rs, src, zone, tok = _exchange_start(True, [gbuf], [got], token, f"scatter_start_{k}")
        pending[k] = (ss, rs, src, zone)
        return tok

    def small_exchanges():
        side_grad = jnp.concatenate(
            [jnp.stack(dgrp).reshape(2, ngrp, N_DEV, grp_local, B_GROUP_DIM).transpose(2, 0, 1, 3, 4)
             .reshape(N_DEV, side_rows, B_GROUP_DIM),
             jnp.pad(jnp.stack(dscale).reshape(2, N_DEV, sdev).transpose(1, 0, 2),
                     ((0, 0), (0, 6), (0, B_GROUP_DIM - sdev)))], axis=1)
        small_part = _pack_small({k: jnp.stack(small_g[k]) for k in SMALL}, d,
                                 jnp.broadcast_to(loss_acc[:1, :1], (1, d)))
        got = pltpu.with_memory_space_constraint(lax.empty((N_DEV - 1,) + side_grad.shape[1:], F32), pltpu.HBM)
        side_x = _exchange_start(True, [side_grad], [got], token, "side_scatter_start")
        small_x = _exchange_start(False, [small_part], [landing(small_part)], side_x[4], "small_gather_start")
        return side_x[:4], small_x[:4], small_x[4]

    for i in reversed(range(DEPTH)):
        sv = saved[i]
        j = i // 2
        wf, wm = wg[2 * i + 1], wg[2 * i]
        hid = N_DEV * lay.ffn_rows
        dx1, df, dab, dgpost, dgpre = _f_bwd(dy, sv["f"], sv["x1"], sv["acts"], row(ffn_pre_g[i]), wf, lay, i,
                                             row(ffn_post_g[i]), token, tm_f, f"f_bwd_{i}")
        small_g["ffn_post_g"][i], small_g["ffn_pre_g"][i] = dgpost[0], dgpre[0]
        gbuf = _grad_into(lay.f_total, dab, sv["h2"], lay.gate[i], lay.ffn_rows, f"g_gate_{i}")
        gbuf = _grad_into(gbuf, dab, sv["h2"], lay.up[i], lay.ffn_rows, f"g_up_{i}", lhs_off=hid)
        gbuf = _grad_into(gbuf, sv["acts"], df, lay.down[i], lay.ffn_rows, f"g_down_{i}", lhs_off=2 * hid)
        token = scatter(2 * i + 1, gbuf)
        mix = sv["mix"]
        gpost = row(mix_post_g[i])
        if i % 2 == 0:
            dx, dm, dz, dgpost, dgpre, dlng, dlnb, dws, dbt = _a_bwd(
                dx1, mix["m"], sv["x"], mix["gp"], mix["u"], mix["vh"], mix["rs"], row(mix_pre_g[i]), wm, lay, j,
                row(a_ln_g[j]), row(a_ln_b[j]), a_w_s[j], bst[j], gpost, token, tm_abwd, f"a_bwd_{j}")
            small_g["a_ln_g"][j], small_g["a_ln_b"][j] = dlng[0], dlnb[0]
            small_g["a_w_s"][j], small_g["a_b_s"][j] = dws, dbt[:, :A_GROUPS].T
            small_g["mix_post_g"][i], small_g["mix_pre_g"][i] = dgpost[0], dgpre[0]
            order = None
            if i == 0:
                side_x, small_x, order = small_exchanges()
            gbuf = _grad_into(lay.a_total, dz, mix["h1"], lay.a_in[j], lay.a_in_rows, f"g_a_in_{j}", after=order)
            gbuf = _grad_into(gbuf, mix["gated"], dm, lay.a_out[j], lay.a_out_rows, f"g_a_out_{j}")
        else:
            dx, dm, draw, dp, dgpost, dgpre, dsc = _b_bwd(
                dx1, mix["m"], sv["x"], mix["pooled"], row(mix_pre_g[i]), wm, lay, j, wgrp_full[j], scale_full[j],
                gpost, token, tm_b, f"b_bwd_{j}")
            dscale[j] = dsc[0]
            dgrp[j] = _grad_grouped(mix["pooled"], draw, f"g_b_grp_{j}")
            gbuf = _grad_into(lay.b_total, mix["h1"], dp, lay.b_in[j], lay.b_rows, f"g_b_in_{j}")
            gbuf = _grad_into(gbuf, mix["mixed"], dm, lay.b_out[j], lay.b_rows, f"g_b_out_{j}")
            small_g["mix_post_g"][i], small_g["mix_pre_g"][i] = dgpost[0], dgpre[0]
        token = scatter(2 * i, gbuf)
        dy = dx
    grad_x = dy[None]

    g_sub = [None] * nsub

    def arrived(k, after):
        ss, rs, src, zone = pending[k]
        (own,), (got,) = _exchange_wait(True, ss, rs, src, zone, after, f"scatter_wait_{k}")
        g_sub[k] = _sum_parts(own, got, me1, f"sum_grads_{k}")

    def rows_of(k, off, n):
        return g_sub[k][off:off + n]

    grads, delta, new_m, new_v = {}, {}, {}, {}

    def update(k):
        back = turn if k in turned else (lambda a: a)
        wk, mk, vk = state[k] if k in turned else (w[k], mom[k], var[k])
        shape = wk.shape
        two = lambda a: a.reshape(-1, shape[-1])
        dl, nm, nv = _adamw(two(wk), two(grads[k]), two(mk), two(vk), f"adamw_{k}")
        delta[k], new_m[k], new_v[k] = (back(a.reshape(shape)) for a in (dl, nm, nv))
        grads[k] = back(grads[k])

    for k in range(1, nsub):
        arrived(k, token)
    grads["ffn_w_gate"] = jnp.stack([rows_of(2 * l + 1, lay.gate[l], ffn_local) for l in range(DEPTH)])
    grads["ffn_w_up"] = jnp.stack([rows_of(2 * l + 1, lay.up[l], ffn_local) for l in range(DEPTH)])
    grads["ffn_w_down"] = jnp.stack([rows_of(2 * l + 1, lay.down[l], ffn_local) for l in range(DEPTH)])
    grads["b_w_in"] = jnp.stack([rows_of(4 * j + 2, lay.b_in[j], lay.b_rows) for j in range(2)])
    grads["b_w_out"] = jnp.stack([rows_of(4 * j + 2, lay.b_out[j], lay.b_rows) for j in range(2)])
    early = ("ffn_w_gate", "ffn_w_up", "ffn_w_down", "b_w_in", "b_w_out")
    for k in early:
        update(k)

    (side_own,), (side_got,) = _exchange_wait(True, *side_x, [delta[k] for k in early], "side_scatter_wait")
    g_side = _sum_parts(side_own, side_got, me1, "sum_side")
    grads["b_w_grp"] = g_side[:side_rows].reshape(b_w_grp.shape)
    grads["b_scale"] = g_side[side_rows:side_rows + 2, :sdev]
    update("b_w_grp")
    update("b_scale")
    _, (small_all,) = _exchange_wait(False, *small_x, [delta["b_w_grp"], delta["b_scale"]], "small_gather_wait")
    small_sum = _sum_devices(small_all.reshape(N_DEV, -1, d), "sum_small")
    g_small = _unpack_small(small_sum, w)
    loss = small_sum[sum(w[k].size for k in SMALL) // d, 0]
    grads.update(g_small)
    dl, nm, nv = _adamw(state["small"][0], _pack_small(g_small, d), state["small"][1], state["small"][2],
                        "adamw_small")
    delta.update(_unpack_small(dl, w))
    new_m.update(_unpack_small(nm, w))
    new_v.update(_unpack_small(nv, w))

    arrived(0, dl)
    grads["a_w_in"] = jnp.stack([rows_of(4 * j, lay.a_in[j], lay.a_in_rows) for j in range(2)])
    grads["a_w_out"] = jnp.stack([rows_of(4 * j, lay.a_out[j], lay.a_out_rows) for j in range(2)])
    update("a_w_in")
    update("a_w_out")

    return (loss, grad_x, *[grads[k] for k in names], *[delta[k] for k in names], *[new_m[k] for k in names],
            *[new_v[k] for k in names])
```

```python
import math

import jax
import jax.numpy as jnp
from jax import lax
from jax.experimental import pallas as pl
from jax.experimental.pallas import tpu as pltpu

F32 = jnp.float32
BF16 = jnp.bfloat16
MESH = pl.DeviceIdType.MESH
ANY = pl.BlockSpec(memory_space=pl.ANY)

N_DEV = 8
EPS = 1e-6
CHUNK = 128
A_GROUPS = 8
A_GROUP_DIM = 256
B_WINDOWS = (2, 4, 8, 16)
B_GROUP_DIM = 256
HALO = 16
DEPTH = 4

ADAM_LR = 0.001
ADAM_B1 = 0.9
ADAM_B2 = 0.999
ADAM_EPS = 1e-08
ADAM_WD = 0.01
ADAM_STEP = 10

VMEM_LIMIT_BYTES = 60 * 1024 * 1024

INV_SQRT2 = 1.0 / math.sqrt(2.0)
LOG2_E = 1.0 / math.log(2.0)
INV_SQRT_2PI = 1.0 / math.sqrt(2.0 * math.pi)


def _call(body, **kw):
    return pl.pallas_call(body, **kw)


def _params(*semantics):
    return pltpu.CompilerParams(dimension_semantics=semantics or None, vmem_limit_bytes=VMEM_LIMIT_BYTES)


def _resident(shape, index):
    return pl.BlockSpec(shape, lambda *_: index, pipeline_mode=pl.Buffered(1))


def _rows(tm, width):
    return pl.BlockSpec((tm, width), lambda i: (i, 0))


def _nn(a, b):
    return jnp.dot(a, b, preferred_element_type=F32)


def _nt(a, b):
    return lax.dot_general(a, b, (((1,), (1,)), ((), ())), preferred_element_type=F32)


def _tn(a, b):
    return lax.dot_general(a, b, (((0,), (0,)), ((), ())), preferred_element_type=F32)


def _rms_fwd(x, g):
    r = lax.rsqrt(jnp.mean(x * x, axis=-1, keepdims=True) + EPS)
    return x * r * g


def _rms_bwd(x, g, dy):
    r = lax.rsqrt(jnp.mean(x * x, axis=-1, keepdims=True) + EPS)
    xh = x * r
    dg = jnp.sum(dy * xh, axis=0, keepdims=True)
    dxh = dy * g
    dx = r * (dxh - xh * jnp.mean(dxh * xh, axis=-1, keepdims=True))
    return dx, dg


SLAB = 16


def _slabs(n):
    return [slice(r, r + SLAB) for r in range(0, n, SLAB)]


def _rms_bwd_slabs(x_at, dy_at, g, n, n_sum, emit):
    acc = jnp.zeros((8, g.shape[1]), F32)
    for rows in _slabs(n):
        x = x_at(rows)
        dy = dy_at(rows)
        r = lax.rsqrt(jnp.mean(x * x, axis=-1, keepdims=True) + EPS)
        xh = x * r
        if rows.start < n_sum:
            p = dy * xh
            acc = acc + p[:8] + p[8:]
        dxh = dy * g
        emit(rows, r * (dxh - xh * jnp.mean(dxh * xh, axis=-1, keepdims=True)))
    return jnp.sum(acc, axis=0, keepdims=True)


def _gelu(z):
    phi = 0.5 + 0.5 * lax.erf(z * INV_SQRT2)
    e = jnp.exp2(z * z * (-0.5 * LOG2_E))
    return z * phi, phi + z * e * INV_SQRT_2PI


def _layernorm_stats(v):
    mu = jnp.mean(v, axis=-1, keepdims=True)
    xc = v - mu
    rs = lax.rsqrt(jnp.mean(xc * xc, axis=-1, keepdims=True) + EPS)
    return xc * rs, rs


def _tril_mask():
    r = lax.broadcasted_iota(jnp.int32, (CHUNK, CHUNK), 0)
    c = lax.broadcasted_iota(jnp.int32, (CHUNK, CHUNK), 1)
    return r >= c


class _Layout:
    def __init__(self, d, ffn_rows):
        self.ffn_rows = ffn_rows
        self.gate, self.up, self.down = [0] * DEPTH, [self.ffn_rows] * DEPTH, [2 * self.ffn_rows] * DEPTH
        self.f_total = 3 * self.ffn_rows
        self.a_in_rows, self.a_out_rows, self.b_rows = 4 * d // N_DEV, 2 * d // N_DEV, d // N_DEV
        self.a_in, self.a_out = [0, 0], [self.a_in_rows] * 2
        self.a_total = self.a_in_rows + self.a_out_rows
        self.b_in, self.b_out = [0, 0], [self.b_rows] * 2
        self.b_total = 2 * self.b_rows


def _wspec(rows, d):
    return _resident((N_DEV * rows, d), (0, 0))


def _a_fwd(x, gpre, wg, lay, j, lng, lnb, ws, bst, gpost, tm, name):
    t, d = x.shape
    aw = 2 * d
    nch = tm // CHUNK

    def body(x_ref, gpre_ref, win_ref, lng_ref, lnb_ref, ws_ref, bst_ref, wout_ref, gpost_ref,
             x1_ref, h1_ref, gp_ref, u_ref, vh_ref, rs_ref, gated_ref, m_ref):
        xv = x_ref[...]
        h1 = _rms_fwd(xv, gpre_ref[...]).astype(BF16)
        h1_ref[...] = h1
        z = _nt(h1, win_ref[...])
        u, du_dz = _gelu(z[:, :aw])
        v, dv_dz = _gelu(z[:, aw:])
        gp_ref[:, :aw] = du_dz.astype(BF16)
        gp_ref[:, aw:] = dv_dz.astype(BF16)
        u_ref[...] = u.astype(BF16)
        vh, rs = _layernorm_stats(v)
        vh_ref[...] = vh.astype(BF16)
        rs_ref[...] = jnp.broadcast_to(rs, rs_ref.shape)
        vn = (vh * lng_ref[...] + lnb_ref[...]).astype(BF16)
        mask = _tril_mask()
        for g in range(A_GROUPS):
            wm = jnp.where(mask, ws_ref[g], 0.0).astype(BF16)
            cols = slice(g * A_GROUP_DIM, (g + 1) * A_GROUP_DIM)
            for c in range(nch):
                rows = slice(c * CHUNK, (c + 1) * CHUNK)
                sv = _nn(wm, vn[rows, cols]) + bst_ref[:, g:g + 1]
                gated_ref[rows, cols] = (u[rows, cols] * sv).astype(BF16)
        m = _nn(gated_ref[...], wout_ref[...])
        m_ref[...] = m
        x1_ref[...] = xv + _rms_fwd(m, gpost_ref[...])

    vec = lambda w: _resident((1, w), (0, 0))
    return _call(
        body, name=name, grid=(t // tm,),
        in_specs=[_rows(tm, d), vec(d), _wspec(lay.a_in_rows, d), vec(aw), vec(aw),
                  _resident((A_GROUPS, CHUNK, CHUNK), (0, 0, 0)), _resident((CHUNK, A_GROUPS), (0, 0)),
                  _wspec(lay.a_out_rows, d), vec(d)],
        out_specs=[_rows(tm, d), _rows(tm, d), _rows(tm, 2 * aw), _rows(tm, aw), _rows(tm, aw), _rows(tm, 128),
                   _rows(tm, aw), _rows(tm, d)],
        out_shape=[jax.ShapeDtypeStruct((t, d), F32), jax.ShapeDtypeStruct((t, d), BF16),
                   jax.ShapeDtypeStruct((t, 2 * aw), BF16), jax.ShapeDtypeStruct((t, aw), BF16),
                   jax.ShapeDtypeStruct((t, aw), BF16), jax.ShapeDtypeStruct((t, 128), F32),
                   jax.ShapeDtypeStruct((t, aw), BF16), jax.ShapeDtypeStruct((t, d), F32)],
        compiler_params=_params("parallel"),
    )(x, gpre, wg[0], lng, lnb, ws, bst, wg[1], gpost)


def _a_bwd(dx1, m, x, gp, u, vh, rs, gpre, wg, lay, j, lng, lnb, ws, bst, gpost, after, tm, name):
    t, d = x.shape
    aw = 2 * d
    nch = tm // CHUNK

    def body(dx1_ref, m_ref, x_ref, gp_ref, u_ref, vh_ref, rs_ref, gpre_ref, win_ref, lng_ref, lnb_ref, ws_ref, bst_ref,
             wout_ref, gpost_ref, after_ref,
             dx_ref, dm_ref, dz_ref, dgpost_ref, dgpre_ref, dlng_ref, dlnb_ref, dws_ref, dbt_ref, dvn_ref):
        @pl.when(pl.program_id(0) == 0)
        def _():
            for r in (dgpost_ref, dgpre_ref, dlng_ref, dlnb_ref, dws_ref, dbt_ref):
                r[...] = jnp.zeros_like(r)

        def put_dm(rows, dx):
            dm_ref[rows, :] = dx.astype(BF16)

        dgpost_ref[...] += _rms_bwd_slabs(lambda rows: m_ref[rows, :], lambda rows: dx1_ref[rows, :], gpost_ref[...],
                                          tm, tm, put_dm)
        dgated = _nt(dm_ref[...], wout_ref[...])

        vh = vh_ref[...].astype(F32)
        rs = rs_ref[:, :1]
        lng_v = lng_ref[...]
        vn = (vh * lng_v + lnb_ref[...]).astype(BF16)
        mask = _tril_mask()
        lane = lax.broadcasted_iota(jnp.int32, (CHUNK, CHUNK), 1)
        for g in range(A_GROUPS):
            wm = jnp.where(mask, ws_ref[g], 0.0).astype(BF16)
            cols = slice(g * A_GROUP_DIM, (g + 1) * A_GROUP_DIM)
            dws_g = jnp.zeros((CHUNK, CHUNK), F32)
            db_g = jnp.zeros((CHUNK, 1), F32)
            for c in range(nch):
                rows = slice(c * CHUNK, (c + 1) * CHUNK)
                vn_cg = vn[rows, cols]
                sv = _nn(wm, vn_cg) + bst_ref[:, g:g + 1]
                dg_cg = dgated[rows, cols]
                dsv = dg_cg * u_ref[rows, cols].astype(F32)
                dsv_bf = dsv.astype(BF16)
                db_g = db_g + jnp.sum(dsv, axis=1, keepdims=True)
                dws_g = dws_g + _nt(dsv_bf, vn_cg)
                dvn_ref[rows, cols] = _tn(wm, dsv_bf)
                dz_ref[rows, cols] = (dg_cg * sv * gp_ref[rows, cols].astype(F32)).astype(BF16)
            dws_ref[g] += jnp.where(mask, dws_g, 0.0)
            dbt_ref[...] += jnp.where(lane == g, db_g, 0.0)
        dvn = dvn_ref[...]
        dlng_ref[...] += jnp.sum(dvn * vh, axis=0, keepdims=True)
        dlnb_ref[...] += jnp.sum(dvn, axis=0, keepdims=True)
        dvh = dvn * lng_v
        dv = rs * (dvh - jnp.mean(dvh, axis=-1, keepdims=True) - vh * jnp.mean(dvh * vh, axis=-1, keepdims=True))
        dz_ref[:, aw:] = (dv * gp_ref[:, aw:].astype(F32)).astype(BF16)
        dh1 = _nn(dz_ref[...], win_ref[...])

        def put_dx(rows, dx):
            dx_ref[rows, :] = dx1_ref[rows, :] + dx

        dgpre_ref[...] += _rms_bwd_slabs(lambda rows: x_ref[rows, :], lambda rows: dh1[rows, :], gpre_ref[...],
                                         tm, tm, put_dx)

    vec = lambda w: _resident((1, w), (0, 0))
    acc = lambda shape: pl.BlockSpec(shape, lambda i: (0,) * len(shape))
    return _call(
        body, name=name, grid=(t // tm,),
        in_specs=[_rows(tm, d), _rows(tm, d), _rows(tm, d), _rows(tm, 2 * aw), _rows(tm, aw), _rows(tm, aw),
                  _rows(tm, 128), vec(d), _wspec(lay.a_in_rows, d), vec(aw), vec(aw),
                  _resident((A_GROUPS, CHUNK, CHUNK), (0, 0, 0)), _resident((CHUNK, A_GROUPS), (0, 0)),
                  _wspec(lay.a_out_rows, d), vec(d), ANY],
        out_specs=[_rows(tm, d), _rows(tm, d), _rows(tm, 2 * aw), acc((1, d)), acc((1, d)), acc((1, aw)), acc((1, aw)),
                   acc((A_GROUPS, CHUNK, CHUNK)), acc((CHUNK, CHUNK))],
        out_shape=[jax.ShapeDtypeStruct((t, d), F32), jax.ShapeDtypeStruct((t, d), BF16),
                   jax.ShapeDtypeStruct((t, 2 * aw), BF16), jax.ShapeDtypeStruct((1, d), F32),
                   jax.ShapeDtypeStruct((1, d), F32), jax.ShapeDtypeStruct((1, aw), F32),
                   jax.ShapeDtypeStruct((1, aw), F32), jax.ShapeDtypeStruct((A_GROUPS, CHUNK, CHUNK), F32),
                   jax.ShapeDtypeStruct((CHUNK, CHUNK), F32)],
        scratch_shapes=[pltpu.VMEM((tm, aw), F32)],
        compiler_params=_params("arbitrary"),
    )(dx1, m, x, gp, u, vh, rs, gpre, wg[0], lng, lnb, ws, bst, wg[1], gpost, after)


def _window_counts(first_row, n, win):
    tpos = first_row + lax.broadcasted_iota(jnp.int32, (n, 1), 0)
    return jnp.clip(tpos + 1, 1, win).astype(F32)


def _b_fwd(x, gpre, wg, lay, j, wgrp, scale, gpost, tm, name):
    t, d = x.shape
    n = tm + HALO
    ngrp = len(B_WINDOWS)

    def body(x_ref, xprev_ref, gpre_ref, win_ref, wgrp_ref, scale_ref, wout_ref, gpost_ref,
             x1_ref, h1_ref, pooled_ref, mixed_ref, m_ref):
        i = pl.program_id(0)
        xv = x_ref[...]
        keep = jnp.where(i > 0, 1.0, 0.0)
        xe = jnp.concatenate([xprev_ref[...] * keep, xv], axis=0)
        h1e = _rms_fwd(xe, gpre_ref[...]).astype(BF16)
        h1_ref[...] = h1e[HALO:]
        p = _nn(h1e, win_ref[...])
        acc = p
        shift = 1
        for g, win in enumerate(B_WINDOWS):
            lo = g * B_GROUP_DIM
            if g > 0:
                acc = acc[:, B_GROUP_DIM:]
            while shift < win:
                acc = acc + pltpu.roll(acc, shift, 0)
                shift *= 2
            cnt = _window_counts(i * tm - HALO, n, win)
            pooled = acc[:, :B_GROUP_DIM] / cnt - p[:, lo:lo + B_GROUP_DIM]
            pooled_ref[:, lo:lo + B_GROUP_DIM] = pooled[HALO:].astype(BF16)
        for g in range(ngrp):
            cols = slice(g * B_GROUP_DIM, (g + 1) * B_GROUP_DIM)
            raw = _nn(pooled_ref[:, cols], wgrp_ref[g])
            mixed_ref[:, cols] = (raw * scale_ref[:, cols]).astype(BF16)
        m = _nn(mixed_ref[...], wout_ref[...])
        m_ref[...] = m
        x1_ref[...] = xv + _rms_fwd(m, gpost_ref[...])

    vec = lambda w: _resident((1, w), (0, 0))
    per = tm // HALO
    return _call(
        body, name=name, grid=(t // tm,),
        in_specs=[_rows(tm, d), pl.BlockSpec((HALO, d), lambda i: (jnp.maximum(i * per - 1, 0), 0)), vec(d),
                  _wspec(lay.b_rows, d), _resident((ngrp, B_GROUP_DIM, B_GROUP_DIM), (0, 0, 0)), vec(d),
                  _wspec(lay.b_rows, d), vec(d)],
        out_specs=[_rows(tm, d)] * 5,
        out_shape=[jax.ShapeDtypeStruct((t, d), F32), jax.ShapeDtypeStruct((t, d), BF16),
                   jax.ShapeDtypeStruct((t, d), BF16), jax.ShapeDtypeStruct((t, d), BF16),
                   jax.ShapeDtypeStruct((t, d), F32)],
        compiler_params=_params("parallel"),
    )(x, x, gpre, wg[0], wgrp, scale, wg[1], gpost)


def _b_bwd(dx1, m, x, pooled, gpre, wg, lay, j, wgrp, scale, gpost, after, tm, name):
    t, d = x.shape
    n = tm + HALO
    ngrp = len(B_WINDOWS)
    steps = t // tm

    def body(dx1_ref, dx1n_ref, m_ref, mn_ref, x_ref, pooled_ref, pooledn_ref, gpre_ref, win_ref, wgrp_ref, scale_ref,
             wout_ref, gpost_ref, after_ref,
             dx_ref, dm_ref, draw_ref, dp_ref, dgpost_ref, dgpre_ref, dscale_ref, dpool_ref):
        i = pl.program_id(0)

        @pl.when(i == 0)
        def _():
            for r in (dgpost_ref, dgpre_ref, dscale_ref):
                r[...] = jnp.zeros_like(r)

        keep = jnp.where(i < steps - 1, 1.0, 0.0)
        dy = dx1_ref[...]
        dye = jnp.concatenate([dy, dx1n_ref[...] * keep], axis=0)
        me = jnp.concatenate([m_ref[...], mn_ref[...]], axis=0)
        gpost_v = gpost_ref[...]
        r = lax.rsqrt(jnp.mean(me * me, axis=-1, keepdims=True) + EPS)
        mh = me * r
        dgpost_ref[...] += jnp.sum((dye * mh)[:tm], axis=0, keepdims=True)
        dmh = dye * gpost_v
        dme = (r * (dmh - mh * jnp.mean(dmh * mh, axis=-1, keepdims=True))).astype(BF16)
        dm_ref[...] = dme[:tm]
        dmixed = _nt(dme, wout_ref[...])
        pooled_e = jnp.concatenate([pooled_ref[...], pooledn_ref[...]], axis=0)
        scale_v = scale_ref[...]
        for g, win in enumerate(B_WINDOWS):
            cols = slice(g * B_GROUP_DIM, (g + 1) * B_GROUP_DIM)
            raw = _nn(pooled_e[:, cols], wgrp_ref[g])
            dscale_ref[:, cols] += jnp.sum((dmixed[:, cols] * raw)[:tm], axis=0, keepdims=True)
            draw = (dmixed[:, cols] * scale_v[:, cols]).astype(BF16)
            draw_ref[:, cols] = draw[:tm]
            dpool = _nt(draw, wgrp_ref[g])
            acc = dpool / _window_counts(i * tm, n, win)
            shift = 1
            while shift < win:
                acc = acc + pltpu.roll(acc, n - shift, 0)
                shift *= 2
            dpool_ref[:, cols] = (acc - dpool)[:tm]
        dp = dpool_ref[...].astype(BF16)
        dp_ref[...] = dp
        dh1 = _nt(dp, win_ref[...])
        dxp, dgpre = _rms_bwd(x_ref[...], gpre_ref[...], dh1)
        dgpre_ref[...] += dgpre
        dx_ref[...] = dy + dxp

    vec = lambda w: _resident((1, w), (0, 0))
    acc = lambda shape: pl.BlockSpec(shape, lambda i: (0,) * len(shape))
    per = tm // HALO
    nxt = lambda i: (jnp.minimum((i + 1) * per, t // HALO - 1), 0)
    return _call(
        body, name=name, grid=(steps,),
        in_specs=[_rows(tm, d), pl.BlockSpec((HALO, d), nxt), _rows(tm, d), pl.BlockSpec((HALO, d), nxt), _rows(tm, d),
                  _rows(tm, d), pl.BlockSpec((HALO, d), nxt), vec(d), _wspec(lay.b_rows, d),
                  _resident((ngrp, B_GROUP_DIM, B_GROUP_DIM), (0, 0, 0)), vec(d), _wspec(lay.b_rows, d),
                  vec(d), ANY],
        out_specs=[_rows(tm, d)] * 4 + [acc((1, d))] * 3,
        out_shape=[jax.ShapeDtypeStruct((t, d), F32), jax.ShapeDtypeStruct((t, d), BF16),
                   jax.ShapeDtypeStruct((t, d), BF16), jax.ShapeDtypeStruct((t, d), BF16)]
                  + [jax.ShapeDtypeStruct((1, d), F32)] * 3,
        scratch_shapes=[pltpu.VMEM((tm, d), F32)],
        compiler_params=_params("arbitrary"),
    )(dx1, dx1, m, m, x, pooled, pooled, gpre, wg[0], wgrp, scale, wg[1], gpost, after)


def _f_fwd(x1, gpre, wg, lay, l, gpost, tm, name, target=None):
    t, d = x1.shape
    hid = N_DEV * lay.ffn_rows
    head = target is not None

    def body(x_ref, gpre_ref, wgate_ref, wup_ref, wdown_ref, gpost_ref, *rest):
        x2_ref, h2_ref, abs_ref, f_ref = rest[-5:-1] if head else rest
        xv = x_ref[...]
        h2 = _rms_fwd(xv, gpre_ref[...]).astype(BF16)
        h2_ref[...] = h2
        a = _nt(h2, wgate_ref[...])
        b = _nt(h2, wup_ref[...])
        sig = jax.nn.sigmoid(a)
        silu = a * sig
        abs_ref[:, :hid] = (b * (sig + silu * (1.0 - sig))).astype(BF16)
        abs_ref[:, hid:2 * hid] = silu.astype(BF16)
        s = (silu * b).astype(BF16)
        abs_ref[:, 2 * hid:] = s
        f = _nn(s, wdown_ref[...])
        f_ref[...] = f
        x2 = xv + _rms_fwd(f, gpost_ref[...])
        if head:
            target_ref, loss_ref = rest[0], rest[-1]

            @pl.when(pl.program_id(0) == 0)
            def _():
                loss_ref[...] = jnp.zeros_like(loss_ref)

            diff = x2 - target_ref[...]
            x2_ref[...] = diff * (1.0 / d)
            sq = jnp.sum(jnp.sum(diff * diff, axis=0, keepdims=True), axis=1, keepdims=True)
            loss_ref[...] += sq * (0.5 / d)
        else:
            x2_ref[...] = x2

    vec = lambda w: _resident((1, w), (0, 0))
    return _call(
        body, name=name, grid=(t // tm,),
        in_specs=[_rows(tm, d), vec(d), _wspec(lay.ffn_rows, d), _wspec(lay.ffn_rows, d),
                  _wspec(lay.ffn_rows, d), vec(d)] + ([_rows(tm, d)] if head else []),
        out_specs=[_rows(tm, d), _rows(tm, d), _rows(tm, 3 * hid), _rows(tm, d)]
                  + ([pl.BlockSpec((8, 128), lambda i: (0, 0))] if head else []),
        out_shape=[jax.ShapeDtypeStruct((t, d), F32), jax.ShapeDtypeStruct((t, d), BF16),
                   jax.ShapeDtypeStruct((t, 3 * hid), BF16), jax.ShapeDtypeStruct((t, d), F32)]
                  + ([jax.ShapeDtypeStruct((8, 128), F32)] if head else []),
        compiler_params=_params("arbitrary" if head else "parallel"),
    )(x1, gpre, wg[0], wg[1], wg[2], gpost, *([target] if head else []))


def _f_bwd(dx2, f, x1, acts, gpre, wg, lay, l, gpost, after, tm, name):
    t, d = x1.shape
    hid = N_DEV * lay.ffn_rows

    def body(dx2_ref, f_ref, x_ref, ab_ref, gpre_ref, wgate_ref, wup_ref, wdown_ref, gpost_ref, after_ref,
             dx1_ref, df_ref, dab_ref, dgpost_ref, dgpre_ref):
        @pl.when(pl.program_id(0) == 0)
        def _():
            dgpost_ref[...] = jnp.zeros_like(dgpost_ref)
            dgpre_ref[...] = jnp.zeros_like(dgpre_ref)

        def put_df(rows, dx):
            df_ref[rows, :] = dx.astype(BF16)

        dgpost_ref[...] += _rms_bwd_slabs(lambda rows: f_ref[rows, :], lambda rows: dx2_ref[rows, :], gpost_ref[...],
                                          tm, tm, put_df)
        ds = _nt(df_ref[...], wdown_ref[...])
        dab_ref[:, :hid] = (ds * ab_ref[:, :hid].astype(F32)).astype(BF16)
        dab_ref[:, hid:] = (ds * ab_ref[:, hid:].astype(F32)).astype(BF16)
        dh2 = _nn(dab_ref[:, :hid], wgate_ref[...]) + _nn(dab_ref[:, hid:], wup_ref[...])

        def put_dx(rows, dx):
            dx1_ref[rows, :] = dx2_ref[rows, :] + dx

        dgpre_ref[...] += _rms_bwd_slabs(lambda rows: x_ref[rows, :], lambda rows: dh2[rows, :], gpre_ref[...],
                                         tm, tm, put_dx)

    vec = lambda w: _resident((1, w), (0, 0))
    acc = pl.BlockSpec((1, d), lambda i: (0, 0))
    return _call(
        body, name=name, grid=(t // tm,),
        in_specs=[_rows(tm, d), _rows(tm, d), _rows(tm, d), _rows(tm, 2 * hid), vec(d),
                  _wspec(lay.ffn_rows, d), _wspec(lay.ffn_rows, d),
                  _wspec(lay.ffn_rows, d), vec(d), ANY],
        out_specs=[_rows(tm, d), _rows(tm, d), _rows(tm, 2 * hid), acc, acc],
        out_shape=[jax.ShapeDtypeStruct((t, d), F32), jax.ShapeDtypeStruct((t, d), BF16),
                   jax.ShapeDtypeStruct((t, 3 * hid), BF16),
                   jax.ShapeDtypeStruct((1, d), F32), jax.ShapeDtypeStruct((1, d), F32)],
        input_output_aliases={3: 2},
        compiler_params=_params("arbitrary"),
    )(dx2, f, x1, acts, gpre, wg[0], wg[1], wg[2], gpost, after)


def _grad_into(gbuf, lhs, rhs, off, rows, name, after=None):
    t, m = lhs.shape
    d = rhs.shape[1]
    assert m == N_DEV * rows and off % rows == 0
    per_tile = {512: 2, 256: 4, 128: 8}[rows]
    tm = per_tile * rows
    assert tm % 128 == 0 and rows % 16 == 0
    tk = 2048 if t % 2048 == 0 else 256
    ksteps = t // tk
    fresh = isinstance(gbuf, int)
    shape = (N_DEV, gbuf, d) if fresh else gbuf.shape
    extra = ([] if fresh else [gbuf]) + ([] if after is None else [after])

    def body(l_ref, r_ref, *rest):
        o_ref, acc_ref = rest[-2:]
        k = pl.program_id(1)

        @pl.when(k == 0)
        def _():
            acc_ref[...] = jnp.zeros_like(acc_ref)

        acc_ref[...] += _tn(l_ref[...], r_ref[...])

        @pl.when(k == ksteps - 1)
        def _():
            o_ref[...] = acc_ref[...].reshape(per_tile, rows, d).astype(BF16)

    return _call(
        body, name=name, grid=(N_DEV // per_tile, ksteps),
        in_specs=[pl.BlockSpec((tk, tm), lambda i, k: (k, i)), pl.BlockSpec((tk, d), lambda i, k: (k, 0))]
                 + [ANY] * len(extra),
        out_specs=pl.BlockSpec((per_tile, rows, d), lambda i, k: (i, off // rows, 0)),
        out_shape=jax.ShapeDtypeStruct(shape, BF16),
        scratch_shapes=[pltpu.VMEM((tm, d), F32)],
        input_output_aliases={} if fresh else {2: 0},
        compiler_params=_params("parallel", "arbitrary"),
    )(lhs, rhs, *extra)


def _grad_ffn(acts, h2, df, rows, name):
    t, d = h2.shape
    per_tile = 4
    tm = per_tile * rows
    tiles = N_DEV // per_tile
    assert acts.shape[1] == 3 * N_DEV * rows and tm % 128 == 0 and rows % 16 == 0
    tk = 2048 if t % 2048 == 0 else 256
    ksteps = t // tk

    def body(l_ref, h2_ref, df_ref, o_ref, acc_ref):
        i, k = pl.program_id(0), pl.program_id(1)

        @pl.when(k == 0)
        def _():
            acc_ref[...] = jnp.zeros_like(acc_ref)

        @pl.when(i < 2 * tiles)
        def _():
            acc_ref[...] += _tn(l_ref[...], h2_ref[...])

        @pl.when(i >= 2 * tiles)
        def _():
            acc_ref[...] += _tn(l_ref[...], df_ref[...])

        @pl.when(k == ksteps - 1)
        def _():
            o_ref[...] = acc_ref[...].reshape(per_tile, rows, d).astype(BF16)

    return _call(
        body, name=name, grid=(3 * tiles, ksteps),
        in_specs=[pl.BlockSpec((tk, tm), lambda i, k: (k, i)),
                  pl.BlockSpec((tk, d), lambda i, k: (jnp.where(i < 2 * tiles, k, ksteps - 1), 0)),
                  pl.BlockSpec((tk, d), lambda i, k: (jnp.where(i >= 2 * tiles, k, 0), 0))],
        out_specs=pl.BlockSpec((per_tile, rows, d), lambda i, k: (i % tiles, i // tiles, 0)),
        out_shape=jax.ShapeDtypeStruct((N_DEV, 3 * rows, d), BF16),
        scratch_shapes=[pltpu.VMEM((tm, d), F32)],
        compiler_params=_params("arbitrary", "arbitrary"),
    )(acts, h2, df)


def _grad_grouped(pooled, draw, name):
    t, d = pooled.shape
    ngrp = len(B_WINDOWS)
    tk = 1024 if t % 1024 == 0 else 256

    def body(p_ref, q_ref, o_ref):
        @pl.when(pl.program_id(0) == 0)
        def _():
            o_ref[...] = jnp.zeros_like(o_ref)

        for g in range(ngrp):
            cols = slice(g * B_GROUP_DIM, (g + 1) * B_GROUP_DIM)
            o_ref[g] += _tn(p_ref[:, cols], q_ref[:, cols])

    return _call(
        body, name=name, grid=(t // tk,),
        in_specs=[_rows(tk, d), _rows(tk, d)],
        out_specs=pl.BlockSpec((ngrp, B_GROUP_DIM, B_GROUP_DIM), lambda i: (0, 0, 0)),
        out_shape=jax.ShapeDtypeStruct((ngrp, B_GROUP_DIM, B_GROUP_DIM), F32),
        compiler_params=_params("arbitrary"),
    )(pooled, draw)


def _peers():
    x, y, c = lax.axis_index("x"), lax.axis_index("y"), lax.axis_index("c")
    flip = lambda v, f: 1 - v if f else v
    peers = []
    for r in range(1, N_DEV):
        px, py, pc = flip(x, r & 4), flip(y, r & 2), flip(c, r & 1)
        peers.append(((px, py, pc), 4 * px + 2 * py + pc))
    return 4 * x + 2 * y + c, peers


HBM = pl.BlockSpec(memory_space=pltpu.HBM)
SEM = pl.BlockSpec(memory_space=pltpu.SEMAPHORE)
EFFECT = pltpu.SideEffectType.DATAFLOW_SIDE_EFFECTING


def _peer_copies(scatter, srcs, lands, send_sems, recv_sems):
    me, peers = _peers()
    copies = []
    for a in range(len(srcs)):
        rows = srcs[a].shape[0]
        block = lambda k: lands[a].at[pl.ds(pl.multiple_of(k * rows, 8), rows)]
        for r, (peer, pidx) in enumerate(peers):
            src = srcs[a].at[pidx] if scatter else srcs[a]
            mine = lands[a].at[r] if scatter else block(pidx)
            theirs = lands[a].at[r] if scatter else block(me)
            send = pltpu.make_async_remote_copy(src_ref=src, dst_ref=theirs, send_sem=send_sems[a].at[r],
                                                recv_sem=recv_sems[a].at[r], device_id=peer, device_id_type=MESH)
            recv = pltpu.make_async_remote_copy(src_ref=src, dst_ref=mine, send_sem=send_sems[a].at[r],
                                                recv_sem=recv_sems[a].at[r], device_id=peer, device_id_type=MESH)
            copies.append((send, recv))
    return copies


def _own_copies(srcs, lands, send_sems):
    me, _ = _peers()
    copies = []
    for a in range(len(srcs)):
        rows = srcs[a].shape[0]
        copies.append(pltpu.make_async_copy(srcs[a], lands[a].at[pl.ds(pl.multiple_of(me * rows, 8), rows)],
                                            send_sems[a].at[N_DEV - 1]))
    return copies


def _exchange_start(scatter, srcs, lands, after, name):
    n = len(srcs)

    def body(*refs):
        src_refs, land_refs = refs[:n], refs[n:2 * n]
        outs = refs[2 * n + 1:]
        send_sems, recv_sems, token = outs[:n], outs[n:2 * n], outs[-1]
        for send, _ in _peer_copies(scatter, src_refs, land_refs, send_sems, recv_sems):
            send.start()
        if not scatter:
            for own in _own_copies(src_refs, land_refs, send_sems):
                own.start()
        token[...] = jnp.zeros_like(token)

    hbm = lambda a: pltpu.with_memory_space_constraint(a, pltpu.HBM)
    res = _call(
        body, name=name,
        in_specs=[HBM] * (2 * n) + [ANY],
        out_specs=[SEM] * (2 * n) + [HBM] * (2 * n) + [pl.BlockSpec(memory_space=pltpu.VMEM)],
        out_shape=[pltpu.SemaphoreType.DMA((N_DEV,))] * (2 * n)
                  + [pltpu.HBM(a.shape, a.dtype) for a in list(srcs) + list(lands)]
                  + [jax.ShapeDtypeStruct((8, 128), F32)],
        input_output_aliases={i: 2 * n + i for i in range(2 * n)},
        compiler_params=pltpu.CompilerParams(has_side_effects=EFFECT),
    )(*[hbm(a) for a in srcs], *[hbm(a) for a in lands], after)
    return res[:n], res[n:2 * n], res[2 * n:3 * n], res[3 * n:4 * n], res[-1]


def _exchange_wait(scatter, send_sems, recv_sems, srcs, lands, after, name):
    n = len(srcs)
    after = list(after) if isinstance(after, (list, tuple)) else [after]

    def body(*refs):
        src_refs, land_refs = refs[:n], refs[n:2 * n]
        send_refs, recv_refs = refs[2 * n:3 * n], refs[3 * n:4 * n]
        for send, recv in _peer_copies(scatter, src_refs, land_refs, send_refs, recv_refs):
            send.wait_send()
            recv.wait_recv()
        if not scatter:
            for own in _own_copies(src_refs, land_refs, send_refs):
                own.wait()

    res = _call(
        body, name=name,
        in_specs=[HBM] * (2 * n) + [SEM] * (2 * n) + [ANY] * len(after),
        out_specs=[HBM] * (2 * n),
        out_shape=[pltpu.HBM(a.shape, a.dtype) for a in list(srcs) + list(lands)],
        input_output_aliases={i: i for i in range(2 * n)},
        compiler_params=pltpu.CompilerParams(has_side_effects=EFFECT),
    )(*srcs, *lands, *send_sems, *recv_sems, *after)
    return res[:n], res[n:]


def _row_tile(rows):
    if rows <= 512:
        return rows
    return max([tr for tr in range(16, 513, 16) if rows % tr == 0] or [rows])


def _sum_parts(own, got, me, name):
    _, rows, w = own.shape
    tr = _row_tile(rows)

    def body(me_ref, a_ref, b_ref, o_ref):
        s = a_ref[...].astype(F32)
        for j in range(N_DEV - 1):
            s = s + b_ref[j].astype(F32)
        o_ref[...] = s

    return _call(
        body, name=name,
        grid_spec=pltpu.PrefetchScalarGridSpec(
            num_scalar_prefetch=1, grid=(rows // tr,),
            in_specs=[pl.BlockSpec((None, tr, w), lambda i, me_ref: (me_ref[0], i, 0)),
                      pl.BlockSpec((N_DEV - 1, tr, w), lambda i, me_ref: (0, i, 0))],
            out_specs=pl.BlockSpec((tr, w), lambda i, me_ref: (i, 0))),
        out_shape=jax.ShapeDtypeStruct((rows, w), F32),
        compiler_params=_params("parallel"),
    )(me, own, got)


def _sum_devices(stacked, name):
    k, rows, w = stacked.shape
    tr = _row_tile(rows)

    def body(a_ref, o_ref):
        s = a_ref[0]
        for j in range(1, k):
            s = s + a_ref[j]
        o_ref[...] = s

    return _call(
        body, name=name, grid=(rows // tr,),
        in_specs=[pl.BlockSpec((k, tr, w), lambda i: (0, i, 0))],
        out_specs=pl.BlockSpec((tr, w), lambda i: (i, 0)),
        out_shape=jax.ShapeDtypeStruct((rows, w), F32),
        compiler_params=_params("parallel"),
    )(stacked)


def _adamw(w, g, m, v, name):
    rows, cols = w.shape
    tr = _row_tile(rows)

    def body(w_ref, g_ref, m_ref, v_ref, d_ref, nm_ref, nv_ref):
        gv = g_ref[...]
        nm = ADAM_B1 * m_ref[...] + (1.0 - ADAM_B1) * gv
        nv = ADAM_B2 * v_ref[...] + (1.0 - ADAM_B2) * (gv * gv)
        m_hat = nm / (1.0 - ADAM_B1 ** ADAM_STEP)
        v_hat = nv / (1.0 - ADAM_B2 ** ADAM_STEP)
        d_ref[...] = -ADAM_LR * (m_hat / (jnp.sqrt(v_hat) + ADAM_EPS) + ADAM_WD * w_ref[...])
        nm_ref[...] = nm
        nv_ref[...] = nv

    spec = pl.BlockSpec((tr, cols), lambda i: (i, 0))
    return _call(
        body, name=name, grid=(rows // tr,),
        in_specs=[spec] * 4, out_specs=[spec] * 3,
        out_shape=[jax.ShapeDtypeStruct((rows, cols), F32)] * 3,
        compiler_params=_params("parallel"),
    )(w, g, m, v)


SMALL = ("a_ln_g", "a_ln_b", "a_w_s", "a_b_s", "mix_pre_g", "mix_post_g", "ffn_pre_g", "ffn_post_g")


def _pack_small(parts, d, last_row=None):
    rows = [parts[k].reshape(-1, d) for k in SMALL] + ([] if last_row is None else [last_row])
    flat = jnp.concatenate(rows, axis=0)
    return jnp.pad(flat, ((0, -flat.shape[0] % 8), (0, 0)))


def _unpack_small(flat, like):
    out, r = {}, 0
    for k in SMALL:
        n = like[k].size // flat.shape[1]
        out[k] = flat[r:r + n].reshape(like[k].shape)
        r += n
    return out


def kernel(x, a_w_in, a_ln_g, a_ln_b, a_w_s, a_b_s, a_w_out, b_w_in, b_w_grp, b_scale, b_w_out, mix_pre_g, mix_post_g, ffn_pre_g, ffn_post_g, ffn_w_gate, ffn_w_up, ffn_w_down, loss_target, m_a_w_in, m_a_ln_g, m_a_ln_b, m_a_w_s, m_a_b_s, m_a_w_out, m_b_w_in, m_b_w_grp, m_b_scale, m_b_w_out, m_mix_pre_g, m_mix_post_g, m_ffn_pre_g, m_ffn_post_g, m_ffn_w_gate, m_ffn_w_up, m_ffn_w_down, v_a_w_in, v_a_ln_g, v_a_ln_b, v_a_w_s, v_a_b_s, v_a_w_out, v_b_w_in, v_b_w_grp, v_b_scale, v_b_w_out, v_mix_pre_g, v_mix_post_g, v_ffn_pre_g, v_ffn_post_g, v_ffn_w_gate, v_ffn_w_up, v_ffn_w_down):
    args = dict(locals())
    names = ("a_w_in", "a_ln_g", "a_ln_b", "a_w_s", "a_b_s", "a_w_out", "b_w_in", "b_w_grp", "b_scale", "b_w_out",
             "mix_pre_g", "mix_post_g", "ffn_pre_g", "ffn_post_g", "ffn_w_gate", "ffn_w_up", "ffn_w_down")
    w = {k: args[k] for k in names}
    mom = {k: args["m_" + k] for k in names}
    var = {k: args["v_" + k] for k in names}

    t, d = x.shape[1], x.shape[2]
    ffn_local = ffn_w_gate.shape[2]
    lay = _Layout(d, ffn_local)
    me = 4 * lax.axis_index("x") + 2 * lax.axis_index("y") + lax.axis_index("c")
    me1 = jnp.reshape(me, (1,)).astype(jnp.int32)

    def landing(block):
        return lax.empty((N_DEV * block.shape[0],) + block.shape[1:], block.dtype)

    def shards(i, mixer, zero):
        j = i // 2
        if not mixer:
            parts = [ffn_w_gate[i].T, ffn_w_up[i].T, ffn_w_down[i]]
        elif i % 2 == 0:
            parts = [a_w_in[j].T, a_w_out[j]]
        else:
            parts = [b_w_in[j], b_w_out[j]]
        return [(p + zero).astype(BF16) for p in parts]

    nsub = 2 * DEPTH
    wg = [None] * nsub
    first = shards(0, True, 0.0)
    first = _exchange_start(False, first, [landing(b) for b in first], jnp.zeros((8, 128), F32), "gather_first_start")
    zero = first[4][0, 0]
    ngrp = len(B_WINDOWS)
    grp_local = b_w_grp.shape[2]
    sdev = b_scale.shape[1]
    side_rows = 2 * ngrp * grp_local
    side = jnp.concatenate(
        [b_w_grp.reshape(side_rows, B_GROUP_DIM),
         jnp.pad(b_scale, ((0, 6), (0, B_GROUP_DIM - sdev)))], axis=0) + zero
    later, where = [side], [slice(0, 1)]
    for k in range(1, nsub):
        new = shards(k // 2, k % 2 == 0, zero)
        where.append(slice(len(later), len(later) + len(new)))
        later += new
    send_sems, recv_sems, later, zones, token = _exchange_start(
        False, later, [landing(b) for b in later], first[4], "gather_start")
    turned = ("ffn_w_gate", "ffn_w_up")
    turn = lambda a: jnp.swapaxes(a, 1, 2)
    state = {k: tuple(turn(a[k]) for a in (w, mom, var)) for k in turned}
    state["small"] = tuple(_pack_small(a, d) for a in (w, mom, var))
    ready = [a for group in state.values() for a in group]
    _, wg[0] = _exchange_wait(False, *first[:4], [token] + ready, "gather_first_wait")

    def gathered(k, after):
        s = where[k]
        _, got = _exchange_wait(False, send_sems[s], recv_sems[s], later[s], zones[s], after, f"gather_wait_{k}")
        return got

    row = lambda a: a.reshape(1, -1)
    bst = jnp.transpose(a_b_s, (0, 2, 1))

    tm = 256 if t % 256 == 0 else CHUNK
    tm_abwd = tm
    tm_b = 512 if t % 512 == 0 else tm
    tm_f = tm

    saved = []
    h = x[0]
    wgrp_full = scale_full = None
    for i in range(DEPTH):
        j = i // 2
        gpre = row(mix_pre_g[i])
        if i > 0:
            wg[2 * i] = gathered(2 * i, h)
        if i % 2 == 0:
            x1, h1, gp, u, vh, rs, gated, m = _a_fwd(h, gpre, wg[2 * i], lay, j, row(a_ln_g[j]), row(a_ln_b[j]),
                                                     a_w_s[j], bst[j], row(mix_post_g[i]), tm, f"a_fwd_{j}")
            mix = dict(h1=h1, gp=gp, u=u, vh=vh, rs=rs, gated=gated, m=m)
        else:
            if wgrp_full is None:
                side_g = gathered(0, h)[0].reshape(N_DEV, side_rows + 8, B_GROUP_DIM)
                wgrp_full = (side_g[:, :side_rows].reshape(N_DEV, 2, ngrp, grp_local, B_GROUP_DIM)
                             .transpose(1, 2, 0, 3, 4).reshape(2, ngrp, B_GROUP_DIM, B_GROUP_DIM).astype(BF16))
                scale_full = (side_g[:, side_rows:side_rows + 2, :sdev].transpose(1, 0, 2)
                              .reshape(2, 1, N_DEV * sdev))
            x1, h1, pooled, mixed, m = _b_fwd(h, gpre, wg[2 * i], lay, j, wgrp_full[j], scale_full[j],
                                              row(mix_post_g[i]), tm_b, f"b_fwd_{j}")
            mix = dict(h1=h1, pooled=pooled, mixed=mixed, m=m)
        wg[2 * i + 1] = gathered(2 * i + 1, x1)
        x2, h2, acts, f, *loss_acc = _f_fwd(x1, row(ffn_pre_g[i]), wg[2 * i + 1], lay, i, row(ffn_post_g[i]), tm_f,
                                               f"f_fwd_{i}", loss_target[0] if i == DEPTH - 1 else None)
        saved.append(dict(x=h, x1=x1, mix=mix, h2=h2, acts=acts, f=f))
        h = x2
    dy, (loss_acc,) = h, loss_acc

    small_g = {k: [None] * w[k].shape[0] for k in SMALL}
    dgrp, dscale = [None, None], [None, None]
    pending = [None] * nsub
    token = jnp.zeros((8, 128), F32)

    def scatter(k, gbuf):
        got = pltpu.with_memory_space_constraint(lax.empty((N_DEV - 1,) + gbuf.shape[1:], gbuf.dtype), pltpu.HBM)
        ss, rs, src, zone, tok = _exchange_start(True, [gbuf], [got], token, f"scatter_start_{k}")
        pending[k] = (ss, rs, src, zone)
        return tok

    def small_exchanges():
        side_grad = jnp.concatenate(
            [jnp.stack(dgrp).reshape(2, ngrp, N_DEV, grp_local, B_GROUP_DIM).transpose(2, 0, 1, 3, 4)
             .reshape(N_DEV, side_rows, B_GROUP_DIM),
             jnp.pad(jnp.stack(dscale).reshape(2, N_DEV, sdev).transpose(1, 0, 2),
                     ((0, 0), (0, 6), (0, B_GROUP_DIM - sdev)))], axis=1)
        small_part = _pack_small({k: jnp.stack(small_g[k]) for k in SMALL}, d,
                                 jnp.broadcast_to(loss_acc[:1, :1], (1, d)))
        got = pltpu.with_memory_space_constraint(lax.empty((N_DEV - 1,) + side_grad.shape[1:], F32), pltpu.HBM)
        side_x = _exchange_start(True, [side_grad], [got], token, "side_scatter_start")
        small_x = _exchange_start(False, [small_part], [landing(small_part)], side_x[4], "small_gather_start")
        return side_x[:4], small_x[:4], small_x[4]

    for i in reversed(range(DEPTH)):
        sv = saved[i]
        j = i // 2
        wf, wm = wg[2 * i + 1], wg[2 * i]
        dx1, df, dacts, dgpost, dgpre = _f_bwd(dy, sv["f"], sv["x1"], sv["acts"], row(ffn_pre_g[i]), wf, lay, i,
                                               row(ffn_post_g[i]), token, tm_f, f"f_bwd_{i}")
        small_g["ffn_post_g"][i], small_g["ffn_pre_g"][i] = dgpost[0], dgpre[0]
        gbuf = _grad_ffn(dacts, sv["h2"], df, lay.ffn_rows, f"g_ffn_{i}")
        token = scatter(2 * i + 1, gbuf)
        mix = sv["mix"]
        gpost = row(mix_post_g[i])
        if i % 2 == 0:
            dx, dm, dz, dgpost, dgpre, dlng, dlnb, dws, dbt = _a_bwd(
                dx1, mix["m"], sv["x"], mix["gp"], mix["u"], mix["vh"], mix["rs"], row(mix_pre_g[i]), wm, lay, j,
                row(a_ln_g[j]), row(a_ln_b[j]), a_w_s[j], bst[j], gpost, token, tm_abwd, f"a_bwd_{j}")
            small_g["a_ln_g"][j], small_g["a_ln_b"][j] = dlng[0], dlnb[0]
            small_g["a_w_s"][j], small_g["a_b_s"][j] = dws, dbt[:, :A_GROUPS].T
            small_g["mix_post_g"][i], small_g["mix_pre_g"][i] = dgpost[0], dgpre[0]
            order = None
            if i == 0:
                side_x, small_x, order = small_exchanges()
            gbuf = _grad_into(lay.a_total, dz, mix["h1"], lay.a_in[j], lay.a_in_rows, f"g_a_in_{j}", after=order)
            gbuf = _grad_into(gbuf, mix["gated"], dm, lay.a_out[j], lay.a_out_rows, f"g_a_out_{j}")
        else:
            dx, dm, draw, dp, dgpost, dgpre, dsc = _b_bwd(
                dx1, mix["m"], sv["x"], mix["pooled"], row(mix_pre_g[i]), wm, lay, j, wgrp_full[j], scale_full[j],
                gpost, token, tm_b, f"b_bwd_{j}")
            dscale[j] = dsc[0]
            dgrp[j] = _grad_grouped(mix["pooled"], draw, f"g_b_grp_{j}")
            gbuf = _grad_into(lay.b_total, mix["h1"], dp, lay.b_in[j], lay.b_rows, f"g_b_in_{j}")
            gbuf = _grad_into(gbuf, mix["mixed"], dm, lay.b_out[j], lay.b_rows, f"g_b_out_{j}")
            small_g["mix_post_g"][i], small_g["mix_pre_g"][i] = dgpost[0], dgpre[0]
        token = scatter(2 * i, gbuf)
        dy = dx
    grad_x = dy[None]

    g_sub = [None] * nsub

    def arrived(k, after):
        ss, rs, src, zone = pending[k]
        (own,), (got,) = _exchange_wait(True, ss, rs, src, zone, after, f"scatter_wait_{k}")
        g_sub[k] = _sum_parts(own, got, me1, f"sum_grads_{k}")

    def rows_of(k, off, n):
        return g_sub[k][off:off + n]

    grads, delta, new_m, new_v = {}, {}, {}, {}

    def update(k):
        back = turn if k in turned else (lambda a: a)
        wk, mk, vk = state[k] if k in turned else (w[k], mom[k], var[k])
        shape = wk.shape
        two = lambda a: a.reshape(-1, shape[-1])
        dl, nm, nv = _adamw(two(wk), two(grads[k]), two(mk), two(vk), f"adamw_{k}")
        delta[k], new_m[k], new_v[k] = (back(a.reshape(shape)) for a in (dl, nm, nv))
        grads[k] = back(grads[k])

    for k in range(1, nsub):
        arrived(k, token)
    grads["ffn_w_gate"] = jnp.stack([rows_of(2 * l + 1, lay.gate[l], ffn_local) for l in range(DEPTH)])
    grads["ffn_w_up"] = jnp.stack([rows_of(2 * l + 1, lay.up[l], ffn_local) for l in range(DEPTH)])
    grads["ffn_w_down"] = jnp.stack([rows_of(2 * l + 1, lay.down[l], ffn_local) for l in range(DEPTH)])
    grads["b_w_in"] = jnp.stack([rows_of(4 * j + 2, lay.b_in[j], lay.b_rows) for j in range(2)])
    grads["b_w_out"] = jnp.stack([rows_of(4 * j + 2, lay.b_out[j], lay.b_rows) for j in range(2)])
    early = ("ffn_w_gate", "ffn_w_up", "ffn_w_down", "b_w_in", "b_w_out")
    for k in early:
        update(k)

    (side_own,), (side_got,) = _exchange_wait(True, *side_x, [delta[k] for k in early], "side_scatter_wait")
    g_side = _sum_parts(side_own, side_got, me1, "sum_side")
    grads["b_w_grp"] = g_side[:side_rows].reshape(b_w_grp.shape)
    grads["b_scale"] = g_side[side_rows:side_rows + 2, :sdev]
    update("b_w_grp")
    update("b_scale")
    _, (small_all,) = _exchange_wait(False, *small_x, [delta["b_w_grp"], delta["b_scale"]], "small_gather_wait")
    small_sum = _sum_devices(small_all.reshape(N_DEV, -1, d), "sum_small")
    g_small = _unpack_small(small_sum, w)
    loss = small_sum[sum(w[k].size for k in SMALL) // d, 0]
    grads.update(g_small)
    dl, nm, nv = _adamw(state["small"][0], _pack_small(g_small, d), state["small"][1], state["small"][2],
                        "adamw_small")
    delta.update(_unpack_small(dl, w))
    new_m.update(_unpack_small(nm, w))
    new_v.update(_unpack_small(nv, w))

    arrived(0, dl)
    grads["a_w_in"] = jnp.stack([rows_of(4 * j, lay.a_in[j], lay.a_in_rows).T for j in range(2)])
    grads["a_w_out"] = jnp.stack([rows_of(4 * j, lay.a_out[j], lay.a_out_rows) for j in range(2)])
    update("a_w_in")
    update("a_w_out")

    return (loss, grad_x, *[grads[k] for k in names], *[delta[k] for k in names], *[new_m[k] for k in names],
            *[new_v[k] for k in names])
```

```python
import math

import jax
import jax.numpy as jnp
from jax import lax
from jax.experimental import pallas as pl
from jax.experimental.pallas import tpu as pltpu

F32 = jnp.float32
BF16 = jnp.bfloat16
MESH = pl.DeviceIdType.MESH
ANY = pl.BlockSpec(memory_space=pl.ANY)

N_DEV = 8
EPS = 1e-6
CHUNK = 128
A_GROUPS = 8
A_GROUP_DIM = 256
B_WINDOWS = (2, 4, 8, 16)
B_GROUP_DIM = 256
HALO = 16
DEPTH = 4

ADAM_LR = 0.001
ADAM_B1 = 0.9
ADAM_B2 = 0.999
ADAM_EPS = 1e-08
ADAM_WD = 0.01
ADAM_STEP = 10

VMEM_LIMIT_BYTES = 60 * 1024 * 1024

INV_SQRT2 = 1.0 / math.sqrt(2.0)
LOG2_E = 1.0 / math.log(2.0)
INV_SQRT_2PI = 1.0 / math.sqrt(2.0 * math.pi)


def _call(body, **kw):
    return pl.pallas_call(body, **kw)


def _params(*semantics):
    return pltpu.CompilerParams(dimension_semantics=semantics or None, vmem_limit_bytes=VMEM_LIMIT_BYTES)


def _resident(shape, index):
    return pl.BlockSpec(shape, lambda *_: index, pipeline_mode=pl.Buffered(1))


def _rows(tm, width):
    return pl.BlockSpec((tm, width), lambda i: (i, 0))


def _nn(a, b):
    return jnp.dot(a, b, preferred_element_type=F32)


def _nt(a, b):
    return lax.dot_general(a, b, (((1,), (1,)), ((), ())), preferred_element_type=F32)


def _tn(a, b):
    return lax.dot_general(a, b, (((0,), (0,)), ((), ())), preferred_element_type=F32)


def _rms_fwd(x, g):
    r = lax.rsqrt(jnp.mean(x * x, axis=-1, keepdims=True) + EPS)
    return x * r * g


def _rms_bwd(x, g, dy):
    r = lax.rsqrt(jnp.mean(x * x, axis=-1, keepdims=True) + EPS)
    xh = x * r
    dg = jnp.sum(dy * xh, axis=0, keepdims=True)
    dxh = dy * g
    dx = r * (dxh - xh * jnp.mean(dxh * xh, axis=-1, keepdims=True))
    return dx, dg


SLAB = 16


def _slabs(n):
    return [slice(r, r + SLAB) for r in range(0, n, SLAB)]


def _rms_bwd_slabs(x_at, dy_at, g, n, n_sum, emit):
    acc = jnp.zeros((8, g.shape[1]), F32)
    for rows in _slabs(n):
        x = x_at(rows)
        dy = dy_at(rows)
        r = lax.rsqrt(jnp.mean(x * x, axis=-1, keepdims=True) + EPS)
        xh = x * r
        if rows.start < n_sum:
            p = dy * xh
            acc = acc + p[:8] + p[8:]
        dxh = dy * g
        emit(rows, r * (dxh - xh * jnp.mean(dxh * xh, axis=-1, keepdims=True)))
    return jnp.sum(acc, axis=0, keepdims=True)


def _gelu(z):
    phi = 0.5 + 0.5 * lax.erf(z * INV_SQRT2)
    e = jnp.exp2(z * z * (-0.5 * LOG2_E))
    return z * phi, phi + z * e * INV_SQRT_2PI


def _layernorm_stats(v):
    mu = jnp.mean(v, axis=-1, keepdims=True)
    xc = v - mu
    rs = lax.rsqrt(jnp.mean(xc * xc, axis=-1, keepdims=True) + EPS)
    return xc * rs, rs


def _tril_mask():
    r = lax.broadcasted_iota(jnp.int32, (CHUNK, CHUNK), 0)
    c = lax.broadcasted_iota(jnp.int32, (CHUNK, CHUNK), 1)
    return r >= c


class _Layout:
    def __init__(self, d, ffn_rows):
        self.ffn_rows = ffn_rows
        self.gate, self.up, self.down = [0] * DEPTH, [self.ffn_rows] * DEPTH, [2 * self.ffn_rows] * DEPTH
        self.f_total = 3 * self.ffn_rows
        self.a_in_rows, self.a_out_rows, self.b_rows = 4 * d // N_DEV, 2 * d // N_DEV, d // N_DEV
        self.a_in, self.a_out = [0, 0], [self.a_in_rows] * 2
        self.a_total = self.a_in_rows + self.a_out_rows
        self.b_in, self.b_out = [0, 0], [self.b_rows] * 2
        self.b_total = 2 * self.b_rows


def _wspec(rows, d):
    return _resident((N_DEV * rows, d), (0, 0))


def _a_fwd(x, gpre, wg, lay, j, lng, lnb, ws, bst, gpost, tm, name):
    t, d = x.shape
    aw = 2 * d
    nch = tm // CHUNK

    def body(x_ref, gpre_ref, win_ref, lng_ref, lnb_ref, ws_ref, bst_ref, wout_ref, gpost_ref,
             x1_ref, h1_ref, gp_ref, u_ref, vh_ref, rs_ref, gated_ref, m_ref):
        xv = x_ref[...]
        h1 = _rms_fwd(xv, gpre_ref[...]).astype(BF16)
        h1_ref[...] = h1
        z = _nt(h1, win_ref[...])
        u, du_dz = _gelu(z[:, :aw])
        v, dv_dz = _gelu(z[:, aw:])
        gp_ref[:, :aw] = du_dz.astype(BF16)
        gp_ref[:, aw:] = dv_dz.astype(BF16)
        u_ref[...] = u.astype(BF16)
        vh, rs = _layernorm_stats(v)
        vh_ref[...] = vh.astype(BF16)
        rs_ref[...] = jnp.broadcast_to(rs, rs_ref.shape)
        vn = (vh * lng_ref[...] + lnb_ref[...]).astype(BF16)
        mask = _tril_mask()
        for g in range(A_GROUPS):
            wm = jnp.where(mask, ws_ref[g], 0.0).astype(BF16)
            cols = slice(g * A_GROUP_DIM, (g + 1) * A_GROUP_DIM)
            for c in range(nch):
                rows = slice(c * CHUNK, (c + 1) * CHUNK)
                sv = _nn(wm, vn[rows, cols]) + bst_ref[:, g:g + 1]
                gated_ref[rows, cols] = (u[rows, cols] * sv).astype(BF16)
        m = _nn(gated_ref[...], wout_ref[...])
        m_ref[...] = m
        x1_ref[...] = xv + _rms_fwd(m, gpost_ref[...])

    vec = lambda w: _resident((1, w), (0, 0))
    return _call(
        body, name=name, grid=(t // tm,),
        in_specs=[_rows(tm, d), vec(d), _wspec(lay.a_in_rows, d), vec(aw), vec(aw),
                  _resident((A_GROUPS, CHUNK, CHUNK), (0, 0, 0)), _resident((CHUNK, A_GROUPS), (0, 0)),
                  _wspec(lay.a_out_rows, d), vec(d)],
        out_specs=[_rows(tm, d), _rows(tm, d), _rows(tm, 2 * aw), _rows(tm, aw), _rows(tm, aw), _rows(tm, 128),
                   _rows(tm, aw), _rows(tm, d)],
        out_shape=[jax.ShapeDtypeStruct((t, d), F32), jax.ShapeDtypeStruct((t, d), BF16),
                   jax.ShapeDtypeStruct((t, 2 * aw), BF16), jax.ShapeDtypeStruct((t, aw), BF16),
                   jax.ShapeDtypeStruct((t, aw), BF16), jax.ShapeDtypeStruct((t, 128), F32),
                   jax.ShapeDtypeStruct((t, aw), BF16), jax.ShapeDtypeStruct((t, d), F32)],
        compiler_params=_params("parallel"),
    )(x, gpre, wg[0], lng, lnb, ws, bst, wg[1], gpost)


def _a_bwd(dx1, m, x, gp, u, vh, rs, gpre, wg, lay, j, lng, lnb, ws, bst, gpost, after, tm, name):
    t, d = x.shape
    aw = 2 * d
    nch = tm // CHUNK

    def body(dx1_ref, m_ref, x_ref, gp_ref, u_ref, vh_ref, rs_ref, gpre_ref, win_ref, lng_ref, lnb_ref, ws_ref, bst_ref,
             wout_ref, gpost_ref, after_ref,
             dx_ref, dm_ref, dz_ref, dgpost_ref, dgpre_ref, dlng_ref, dlnb_ref, dws_ref, dbt_ref, dvn_ref):
        @pl.when(pl.program_id(0) == 0)
        def _():
            for r in (dgpost_ref, dgpre_ref, dlng_ref, dlnb_ref, dws_ref, dbt_ref):
                r[...] = jnp.zeros_like(r)

        def put_dm(rows, dx):
            dm_ref[rows, :] = dx.astype(BF16)

        dgpost_ref[...] += _rms_bwd_slabs(lambda rows: m_ref[rows, :], lambda rows: dx1_ref[rows, :], gpost_ref[...],
                                          tm, tm, put_dm)
        dgated = _nt(dm_ref[...], wout_ref[...])

        vh = vh_ref[...].astype(F32)
        rs = rs_ref[:, :1]
        lng_v = lng_ref[...]
        vn = (vh * lng_v + lnb_ref[...]).astype(BF16)
        mask = _tril_mask()
        lane = lax.broadcasted_iota(jnp.int32, (CHUNK, CHUNK), 1)
        for g in range(A_GROUPS):
            wm = jnp.where(mask, ws_ref[g], 0.0).astype(BF16)
            cols = slice(g * A_GROUP_DIM, (g + 1) * A_GROUP_DIM)
            dws_g = jnp.zeros((CHUNK, CHUNK), F32)
            db_g = jnp.zeros((CHUNK, 1), F32)
            for c in range(nch):
                rows = slice(c * CHUNK, (c + 1) * CHUNK)
                vn_cg = vn[rows, cols]
                sv = _nn(wm, vn_cg) + bst_ref[:, g:g + 1]
                dg_cg = dgated[rows, cols]
                dsv = dg_cg * u_ref[rows, cols].astype(F32)
                dsv_bf = dsv.astype(BF16)
                db_g = db_g + jnp.sum(dsv, axis=1, keepdims=True)
                dws_g = dws_g + _nt(dsv_bf, vn_cg)
                dvn_ref[rows, cols] = _tn(wm, dsv_bf)
                dz_ref[rows, cols] = (dg_cg * sv * gp_ref[rows, cols].astype(F32)).astype(BF16)
            dws_ref[g] += jnp.where(mask, dws_g, 0.0)
            dbt_ref[...] += jnp.where(lane == g, db_g, 0.0)
        dvn = dvn_ref[...]
        dlng_ref[...] += jnp.sum(dvn * vh, axis=0, keepdims=True)
        dlnb_ref[...] += jnp.sum(dvn, axis=0, keepdims=True)
        dvh = dvn * lng_v
        dv = rs * (dvh - jnp.mean(dvh, axis=-1, keepdims=True) - vh * jnp.mean(dvh * vh, axis=-1, keepdims=True))
        dz_ref[:, aw:] = (dv * gp_ref[:, aw:].astype(F32)).astype(BF16)
        dh1 = _nn(dz_ref[...], win_ref[...])

        def put_dx(rows, dx):
            dx_ref[rows, :] = dx1_ref[rows, :] + dx

        dgpre_ref[...] += _rms_bwd_slabs(lambda rows: x_ref[rows, :], lambda rows: dh1[rows, :], gpre_ref[...],
                                         tm, tm, put_dx)

    vec = lambda w: _resident((1, w), (0, 0))
    acc = lambda shape: pl.BlockSpec(shape, lambda i: (0,) * len(shape))
    return _call(
        body, name=name, grid=(t // tm,),
        in_specs=[_rows(tm, d), _rows(tm, d), _rows(tm, d), _rows(tm, 2 * aw), _rows(tm, aw), _rows(tm, aw),
                  _rows(tm, 128), vec(d), _wspec(lay.a_in_rows, d), vec(aw), vec(aw),
                  _resident((A_GROUPS, CHUNK, CHUNK), (0, 0, 0)), _resident((CHUNK, A_GROUPS), (0, 0)),
                  _wspec(lay.a_out_rows, d), vec(d), ANY],
        out_specs=[_rows(tm, d), _rows(tm, d), _rows(tm, 2 * aw), acc((1, d)), acc((1, d)), acc((1, aw)), acc((1, aw)),
                   acc((A_GROUPS, CHUNK, CHUNK)), acc((CHUNK, CHUNK))],
        out_shape=[jax.ShapeDtypeStruct((t, d), F32), jax.ShapeDtypeStruct((t, d), BF16),
                   jax.ShapeDtypeStruct((t, 2 * aw), BF16), jax.ShapeDtypeStruct((1, d), F32),
                   jax.ShapeDtypeStruct((1, d), F32), jax.ShapeDtypeStruct((1, aw), F32),
                   jax.ShapeDtypeStruct((1, aw), F32), jax.ShapeDtypeStruct((A_GROUPS, CHUNK, CHUNK), F32),
                   jax.ShapeDtypeStruct((CHUNK, CHUNK), F32)],
        scratch_shapes=[pltpu.VMEM((tm, aw), F32)],
        compiler_params=_params("arbitrary"),
    )(dx1, m, x, gp, u, vh, rs, gpre, wg[0], lng, lnb, ws, bst, wg[1], gpost, after)


def _window_counts(first_row, n, win):
    tpos = first_row + lax.broadcasted_iota(jnp.int32, (n, 1), 0)
    return jnp.clip(tpos + 1, 1, win).astype(F32)


def _b_fwd(x, gpre, wg, lay, j, wgrp, scale, gpost, tm, name):
    t, d = x.shape
    n = tm + HALO
    ngrp = len(B_WINDOWS)

    def body(x_ref, xprev_ref, gpre_ref, win_ref, wgrp_ref, scale_ref, wout_ref, gpost_ref,
             x1_ref, h1_ref, pooled_ref, mixed_ref, m_ref):
        i = pl.program_id(0)
        xv = x_ref[...]
        keep = jnp.where(i > 0, 1.0, 0.0)
        xe = jnp.concatenate([xprev_ref[...] * keep, xv], axis=0)
        h1e = _rms_fwd(xe, gpre_ref[...]).astype(BF16)
        h1_ref[...] = h1e[HALO:]
        p = _nn(h1e, win_ref[...])
        acc = p
        shift = 1
        for g, win in enumerate(B_WINDOWS):
            lo = g * B_GROUP_DIM
            if g > 0:
                acc = acc[:, B_GROUP_DIM:]
            while shift < win:
                acc = acc + pltpu.roll(acc, shift, 0)
                shift *= 2
            cnt = _window_counts(i * tm - HALO, n, win)
            pooled = acc[:, :B_GROUP_DIM] / cnt - p[:, lo:lo + B_GROUP_DIM]
            pooled_ref[:, lo:lo + B_GROUP_DIM] = pooled[HALO:].astype(BF16)
        for g in range(ngrp):
            cols = slice(g * B_GROUP_DIM, (g + 1) * B_GROUP_DIM)
            raw = _nn(pooled_ref[:, cols], wgrp_ref[g])
            mixed_ref[:, cols] = (raw * scale_ref[:, cols]).astype(BF16)
        m = _nn(mixed_ref[...], wout_ref[...])
        m_ref[...] = m
        x1_ref[...] = xv + _rms_fwd(m, gpost_ref[...])

    vec = lambda w: _resident((1, w), (0, 0))
    per = tm // HALO
    return _call(
        body, name=name, grid=(t // tm,),
        in_specs=[_rows(tm, d), pl.BlockSpec((HALO, d), lambda i: (jnp.maximum(i * per - 1, 0), 0)), vec(d),
                  _wspec(lay.b_rows, d), _resident((ngrp, B_GROUP_DIM, B_GROUP_DIM), (0, 0, 0)), vec(d),
                  _wspec(lay.b_rows, d), vec(d)],
        out_specs=[_rows(tm, d)] * 5,
        out_shape=[jax.ShapeDtypeStruct((t, d), F32), jax.ShapeDtypeStruct((t, d), BF16),
                   jax.ShapeDtypeStruct((t, d), BF16), jax.ShapeDtypeStruct((t, d), BF16),
                   jax.ShapeDtypeStruct((t, d), F32)],
        compiler_params=_params("parallel"),
    )(x, x, gpre, wg[0], wgrp, scale, wg[1], gpost)


def _b_bwd(dx1, m, x, pooled, gpre, wg, lay, j, wgrp, scale, gpost, after, tm, name):
    t, d = x.shape
    n = tm + HALO
    ngrp = len(B_WINDOWS)
    steps = t // tm

    def body(dx1_ref, dx1n_ref, m_ref, mn_ref, x_ref, pooled_ref, pooledn_ref, gpre_ref, win_ref, wgrp_ref, scale_ref,
             wout_ref, gpost_ref, after_ref,
             dx_ref, dm_ref, draw_ref, dp_ref, dgpost_ref, dgpre_ref, dscale_ref, dpool_ref):
        i = pl.program_id(0)

        @pl.when(i == 0)
        def _():
            for r in (dgpost_ref, dgpre_ref, dscale_ref):
                r[...] = jnp.zeros_like(r)

        keep = jnp.where(i < steps - 1, 1.0, 0.0)
        dy = dx1_ref[...]
        dye = jnp.concatenate([dy, dx1n_ref[...] * keep], axis=0)
        me = jnp.concatenate([m_ref[...], mn_ref[...]], axis=0)
        gpost_v = gpost_ref[...]
        r = lax.rsqrt(jnp.mean(me * me, axis=-1, keepdims=True) + EPS)
        mh = me * r
        dgpost_ref[...] += jnp.sum((dye * mh)[:tm], axis=0, keepdims=True)
        dmh = dye * gpost_v
        dme = (r * (dmh - mh * jnp.mean(dmh * mh, axis=-1, keepdims=True))).astype(BF16)
        dm_ref[...] = dme[:tm]
        dmixed = _nt(dme, wout_ref[...])
        pooled_e = jnp.concatenate([pooled_ref[...], pooledn_ref[...]], axis=0)
        scale_v = scale_ref[...]
        for g, win in enumerate(B_WINDOWS):
            cols = slice(g * B_GROUP_DIM, (g + 1) * B_GROUP_DIM)
            raw = _nn(pooled_e[:, cols], wgrp_ref[g])
            dscale_ref[:, cols] += jnp.sum((dmixed[:, cols] * raw)[:tm], axis=0, keepdims=True)
            draw = (dmixed[:, cols] * scale_v[:, cols]).astype(BF16)
            draw_ref[:, cols] = draw[:tm]
            dpool = _nt(draw, wgrp_ref[g])
            acc = dpool / _window_counts(i * tm, n, win)
            shift = 1
            while shift < win:
                acc = acc + pltpu.roll(acc, n - shift, 0)
                shift *= 2
            dpool_ref[:, cols] = (acc - dpool)[:tm]
        dp = dpool_ref[...].astype(BF16)
        dp_ref[...] = dp
        dh1 = _nt(dp, win_ref[...])
        dxp, dgpre = _rms_bwd(x_ref[...], gpre_ref[...], dh1)
        dgpre_ref[...] += dgpre
        dx_ref[...] = dy + dxp

    vec = lambda w: _resident((1, w), (0, 0))
    acc = lambda shape: pl.BlockSpec(shape, lambda i: (0,) * len(shape))
    per = tm // HALO
    nxt = lambda i: (jnp.minimum((i + 1) * per, t // HALO - 1), 0)
    return _call(
        body, name=name, grid=(steps,),
        in_specs=[_rows(tm, d), pl.BlockSpec((HALO, d), nxt), _rows(tm, d), pl.BlockSpec((HALO, d), nxt), _rows(tm, d),
                  _rows(tm, d), pl.BlockSpec((HALO, d), nxt), vec(d), _wspec(lay.b_rows, d),
                  _resident((ngrp, B_GROUP_DIM, B_GROUP_DIM), (0, 0, 0)), vec(d), _wspec(lay.b_rows, d),
                  vec(d), ANY],
        out_specs=[_rows(tm, d)] * 4 + [acc((1, d))] * 3,
        out_shape=[jax.ShapeDtypeStruct((t, d), F32), jax.ShapeDtypeStruct((t, d), BF16),
                   jax.ShapeDtypeStruct((t, d), BF16), jax.ShapeDtypeStruct((t, d), BF16)]
                  + [jax.ShapeDtypeStruct((1, d), F32)] * 3,
        scratch_shapes=[pltpu.VMEM((tm, d), F32)],
        compiler_params=_params("arbitrary"),
    )(dx1, dx1, m, m, x, pooled, pooled, gpre, wg[0], wgrp, scale, wg[1], gpost, after)


def _f_fwd(x1, gpre, wg, lay, l, gpost, tm, name, target=None):
    t, d = x1.shape
    hid = N_DEV * lay.ffn_rows
    head = target is not None

    def body(x_ref, gpre_ref, wgate_ref, wup_ref, wdown_ref, gpost_ref, *rest):
        x2_ref, h2_ref, abs_ref, f_ref = rest[-5:-1] if head else rest
        xv = x_ref[...]
        h2 = _rms_fwd(xv, gpre_ref[...]).astype(BF16)
        h2_ref[...] = h2
        a = _nt(h2, wgate_ref[...])
        b = _nt(h2, wup_ref[...])
        sig = jax.nn.sigmoid(a)
        silu = a * sig
        abs_ref[:, :hid] = (b * (sig + silu * (1.0 - sig))).astype(BF16)
        abs_ref[:, hid:2 * hid] = silu.astype(BF16)
        s = (silu * b).astype(BF16)
        abs_ref[:, 2 * hid:] = s
        f = _nn(s, wdown_ref[...])
        f_ref[...] = f
        x2 = xv + _rms_fwd(f, gpost_ref[...])
        if head:
            target_ref, loss_ref = rest[0], rest[-1]

            @pl.when(pl.program_id(0) == 0)
            def _():
                loss_ref[...] = jnp.zeros_like(loss_ref)

            diff = x2 - target_ref[...]
            x2_ref[...] = diff * (1.0 / d)
            sq = jnp.sum(jnp.sum(diff * diff, axis=0, keepdims=True), axis=1, keepdims=True)
            loss_ref[...] += sq * (0.5 / d)
        else:
            x2_ref[...] = x2

    vec = lambda w: _resident((1, w), (0, 0))
    return _call(
        body, name=name, grid=(t // tm,),
        in_specs=[_rows(tm, d), vec(d), _wspec(lay.ffn_rows, d), _wspec(lay.ffn_rows, d),
                  _wspec(lay.ffn_rows, d), vec(d)] + ([_rows(tm, d)] if head else []),
        out_specs=[_rows(tm, d), _rows(tm, d), _rows(tm, 3 * hid), _rows(tm, d)]
                  + ([pl.BlockSpec((8, 128), lambda i: (0, 0))] if head else []),
        out_shape=[jax.ShapeDtypeStruct((t, d), F32), jax.ShapeDtypeStruct((t, d), BF16),
                   jax.ShapeDtypeStruct((t, 3 * hid), BF16), jax.ShapeDtypeStruct((t, d), F32)]
                  + ([jax.ShapeDtypeStruct((8, 128), F32)] if head else []),
        compiler_params=_params("arbitrary" if head else "parallel"),
    )(x1, gpre, wg[0], wg[1], wg[2], gpost, *([target] if head else []))


def _f_bwd(dx2, f, x1, acts, gpre, wg, lay, l, gpost, after, tm, name):
    t, d = x1.shape
    hid = N_DEV * lay.ffn_rows

    def body(dx2_ref, f_ref, x_ref, ab_ref, gpre_ref, wgate_ref, wup_ref, wdown_ref, gpost_ref, after_ref,
             dx1_ref, df_ref, dab_ref, dgpost_ref, dgpre_ref):
        @pl.when(pl.program_id(0) == 0)
        def _():
            dgpost_ref[...] = jnp.zeros_like(dgpost_ref)
            dgpre_ref[...] = jnp.zeros_like(dgpre_ref)

        def put_df(rows, dx):
            df_ref[rows, :] = dx.astype(BF16)

        dgpost_ref[...] += _rms_bwd_slabs(lambda rows: f_ref[rows, :], lambda rows: dx2_ref[rows, :], gpost_ref[...],
                                          tm, tm, put_df)
        ds = _nt(df_ref[...], wdown_ref[...])
        dab_ref[:, :hid] = (ds * ab_ref[:, :hid].astype(F32)).astype(BF16)
        dab_ref[:, hid:] = (ds * ab_ref[:, hid:].astype(F32)).astype(BF16)
        dh2 = _nn(dab_ref[:, :hid], wgate_ref[...]) + _nn(dab_ref[:, hid:], wup_ref[...])

        def put_dx(rows, dx):
            dx1_ref[rows, :] = dx2_ref[rows, :] + dx

        dgpre_ref[...] += _rms_bwd_slabs(lambda rows: x_ref[rows, :], lambda rows: dh2[rows, :], gpre_ref[...],
                                         tm, tm, put_dx)

    vec = lambda w: _resident((1, w), (0, 0))
    acc = pl.BlockSpec((1, d), lambda i: (0, 0))
    return _call(
        body, name=name, grid=(t // tm,),
        in_specs=[_rows(tm, d), _rows(tm, d), _rows(tm, d), _rows(tm, 2 * hid), vec(d),
                  _wspec(lay.ffn_rows, d), _wspec(lay.ffn_rows, d),
                  _wspec(lay.ffn_rows, d), vec(d), ANY],
        out_specs=[_rows(tm, d), _rows(tm, d), _rows(tm, 2 * hid), acc, acc],
        out_shape=[jax.ShapeDtypeStruct((t, d), F32), jax.ShapeDtypeStruct((t, d), BF16),
                   jax.ShapeDtypeStruct((t, 3 * hid), BF16),
                   jax.ShapeDtypeStruct((1, d), F32), jax.ShapeDtypeStruct((1, d), F32)],
        input_output_aliases={3: 2},
        compiler_params=_params("arbitrary"),
    )(dx2, f, x1, acts, gpre, wg[0], wg[1], wg[2], gpost, after)


def _grad_into(gbuf, lhs, rhs, off, rows, name, after=None):
    t, m = lhs.shape
    d = rhs.shape[1]
    assert m == N_DEV * rows and off % rows == 0
    per_tile = {512: 2, 256: 4, 128: 8}[rows]
    tm = per_tile * rows
    assert tm % 128 == 0 and rows % 16 == 0
    tk = 2048 if t % 2048 == 0 else 256
    ksteps = t // tk
    fresh = isinstance(gbuf, int)
    shape = (N_DEV, gbuf, d) if fresh else gbuf.shape
    extra = ([] if fresh else [gbuf]) + ([] if after is None else [after])

    def body(l_ref, r_ref, *rest):
        o_ref, acc_ref = rest[-2:]
        k = pl.program_id(1)

        @pl.when(k == 0)
        def _():
            acc_ref[...] = jnp.zeros_like(acc_ref)

        acc_ref[...] += _tn(l_ref[...], r_ref[...])

        @pl.when(k == ksteps - 1)
        def _():
            o_ref[...] = acc_ref[...].reshape(per_tile, rows, d).astype(BF16)

    return _call(
        body, name=name, grid=(N_DEV // per_tile, ksteps),
        in_specs=[pl.BlockSpec((tk, tm), lambda i, k: (k, i)), pl.BlockSpec((tk, d), lambda i, k: (k, 0))]
                 + [ANY] * len(extra),
        out_specs=pl.BlockSpec((per_tile, rows, d), lambda i, k: (i, off // rows, 0)),
        out_shape=jax.ShapeDtypeStruct(shape, BF16),
        scratch_shapes=[pltpu.VMEM((tm, d), F32)],
        input_output_aliases={} if fresh else {2: 0},
        compiler_params=_params("parallel", "arbitrary"),
    )(lhs, rhs, *extra)


def _grad_ffn(acts, h2, df, rows, name):
    t, d = h2.shape
    per_tile = 4
    tm = per_tile * rows
    tiles = N_DEV // per_tile
    assert acts.shape[1] == 3 * N_DEV * rows and tm % 128 == 0 and rows % 16 == 0
    tk = 2048 if t % 2048 == 0 else 256
    ksteps = t // tk

    def body(l_ref, h2_ref, df_ref, o_ref, acc_ref):
        i, k = pl.program_id(0), pl.program_id(1)

        @pl.when(k == 0)
        def _():
            acc_ref[...] = jnp.zeros_like(acc_ref)

        @pl.when(i < 2 * tiles)
        def _():
            acc_ref[...] += _tn(l_ref[...], h2_ref[...])

        @pl.when(i >= 2 * tiles)
        def _():
            acc_ref[...] += _tn(l_ref[...], df_ref[...])

        @pl.when(k == ksteps - 1)
        def _():
            o_ref[...] = acc_ref[...].reshape(per_tile, rows, d).astype(BF16)

    return _call(
        body, name=name, grid=(3 * tiles, ksteps),
        in_specs=[pl.BlockSpec((tk, tm), lambda i, k: (k, i)),
                  pl.BlockSpec((tk, d), lambda i, k: (jnp.where(i < 2 * tiles, k, ksteps - 1), 0)),
                  pl.BlockSpec((tk, d), lambda i, k: (jnp.where(i >= 2 * tiles, k, 0), 0))],
        out_specs=pl.BlockSpec((per_tile, rows, d), lambda i, k: (i % tiles, i // tiles, 0)),
        out_shape=jax.ShapeDtypeStruct((N_DEV, 3 * rows, d), BF16),
        scratch_shapes=[pltpu.VMEM((tm, d), F32)],
        compiler_params=_params("arbitrary", "arbitrary"),
    )(acts, h2, df)


def _grad_grouped(pooled, draw, name):
    t, d = pooled.shape
    ngrp = len(B_WINDOWS)
    tk = 1024 if t % 1024 == 0 else 256

    def body(p_ref, q_ref, o_ref):
        @pl.when(pl.program_id(0) == 0)
        def _():
            o_ref[...] = jnp.zeros_like(o_ref)

        for g in range(ngrp):
            cols = slice(g * B_GROUP_DIM, (g + 1) * B_GROUP_DIM)
            o_ref[g] += _tn(p_ref[:, cols], q_ref[:, cols])

    return _call(
        body, name=name, grid=(t // tk,),
        in_specs=[_rows(tk, d), _rows(tk, d)],
        out_specs=pl.BlockSpec((ngrp, B_GROUP_DIM, B_GROUP_DIM), lambda i: (0, 0, 0)),
        out_shape=jax.ShapeDtypeStruct((ngrp, B_GROUP_DIM, B_GROUP_DIM), F32),
        compiler_params=_params("arbitrary"),
    )(pooled, draw)


def _peers():
    x, y, c = lax.axis_index("x"), lax.axis_index("y"), lax.axis_index("c")
    flip = lambda v, f: 1 - v if f else v
    peers = []
    for r in range(1, N_DEV):
        px, py, pc = flip(x, r & 4), flip(y, r & 2), flip(c, r & 1)
        peers.append(((px, py, pc), 4 * px + 2 * py + pc))
    return 4 * x + 2 * y + c, peers


HBM = pl.BlockSpec(memory_space=pltpu.HBM)
SEM = pl.BlockSpec(memory_space=pltpu.SEMAPHORE)
EFFECT = pltpu.SideEffectType.DATAFLOW_SIDE_EFFECTING


def _peer_copies(scatter, srcs, lands, send_sems, recv_sems):
    me, peers = _peers()
    copies = []
    for a in range(len(srcs)):
        rows = srcs[a].shape[0]
        block = lambda k: lands[a].at[pl.ds(pl.multiple_of(k * rows, 8), rows)]
        for r, (peer, pidx) in enumerate(peers):
            src = srcs[a].at[pidx] if scatter else srcs[a]
            mine = lands[a].at[r] if scatter else block(pidx)
            theirs = lands[a].at[r] if scatter else block(me)
            send = pltpu.make_async_remote_copy(src_ref=src, dst_ref=theirs, send_sem=send_sems[a].at[r],
                                                recv_sem=recv_sems[a].at[r], device_id=peer, device_id_type=MESH)
            recv = pltpu.make_async_remote_copy(src_ref=src, dst_ref=mine, send_sem=send_sems[a].at[r],
                                                recv_sem=recv_sems[a].at[r], device_id=peer, device_id_type=MESH)
            copies.append((send, recv))
    return copies


def _own_copies(srcs, lands, send_sems):
    me, _ = _peers()
    copies = []
    for a in range(len(srcs)):
        rows = srcs[a].shape[0]
        copies.append(pltpu.make_async_copy(srcs[a], lands[a].at[pl.ds(pl.multiple_of(me * rows, 8), rows)],
                                            send_sems[a].at[N_DEV - 1]))
    return copies


def _exchange_start(scatter, srcs, lands, after, name):
    n = len(srcs)

    def body(*refs):
        src_refs, land_refs = refs[:n], refs[n:2 * n]
        outs = refs[2 * n + 1:]
        send_sems, recv_sems, token = outs[:n], outs[n:2 * n], outs[-1]
        for send, _ in _peer_copies(scatter, src_refs, land_refs, send_sems, recv_sems):
            send.start()
        if not scatter:
            for own in _own_copies(src_refs, land_refs, send_sems):
                own.start()
        token[...] = jnp.zeros_like(token)

    hbm = lambda a: pltpu.with_memory_space_constraint(a, pltpu.HBM)
    res = _call(
        body, name=name,
        in_specs=[HBM] * (2 * n) + [ANY],
        out_specs=[SEM] * (2 * n) + [HBM] * (2 * n) + [pl.BlockSpec(memory_space=pltpu.VMEM)],
        out_shape=[pltpu.SemaphoreType.DMA((N_DEV,))] * (2 * n)
                  + [pltpu.HBM(a.shape, a.dtype) for a in list(srcs) + list(lands)]
                  + [jax.ShapeDtypeStruct((8, 128), F32)],
        input_output_aliases={i: 2 * n + i for i in range(2 * n)},
        compiler_params=pltpu.CompilerParams(has_side_effects=EFFECT),
    )(*[hbm(a) for a in srcs], *[hbm(a) for a in lands], after)
    return res[:n], res[n:2 * n], res[2 * n:3 * n], res[3 * n:4 * n], res[-1]


def _chip_peers():
    x, y, c = lax.axis_index("x"), lax.axis_index("y"), lax.axis_index("c")
    far = []
    for px, py in ((1 - x, y), (x, 1 - y), (1 - x, 1 - y)):
        far.append(((px, py, c), 4 * px + 2 * py + c, 4 * px + 2 * py + 1 - c))
    return 4 * x + 2 * y + c, ((x, y, 1 - c), 4 * x + 2 * y + 1 - c), far


def _block(land, rows, k):
    return land.at[pl.ds(pl.multiple_of(k * rows, 8), rows)]


def _gather2_first(srcs, lands, after, name):
    n = len(srcs)

    def body(*refs):
        src_refs, land_refs = refs[:n], refs[n:2 * n]
        outs = refs[2 * n + 1:]
        send, recv_sib, recv_far, token = outs[:n], outs[n:2 * n], outs[2 * n:3 * n], outs[-1]
        me, (sib, _), far = _chip_peers()
        for a in range(n):
            rows = src_refs[a].shape[0]
            mine = _block(land_refs[a], rows, me)
            pltpu.make_async_copy(src_refs[a], mine, send[a].at[4]).start()
            pltpu.make_async_remote_copy(src_ref=src_refs[a], dst_ref=mine, send_sem=send[a].at[0],
                                         recv_sem=recv_sib[a].at[0], device_id=sib, device_id_type=MESH).start()
            for j, (peer, _, _) in enumerate(far):
                pltpu.make_async_remote_copy(src_ref=src_refs[a], dst_ref=mine, send_sem=send[a].at[1 + j],
                                             recv_sem=recv_far[a].at[j], device_id=peer, device_id_type=MESH).start()
        token[...] = jnp.zeros_like(token)

    hbm = lambda a: pltpu.with_memory_space_constraint(a, pltpu.HBM)
    res = _call(
        body, name=name,
        in_specs=[HBM] * (2 * n) + [ANY],
        out_specs=[SEM] * (3 * n) + [HBM] * (2 * n) + [pl.BlockSpec(memory_space=pltpu.VMEM)],
        out_shape=[pltpu.SemaphoreType.DMA((5,))] * n + [pltpu.SemaphoreType.DMA((1,))] * n
                  + [pltpu.SemaphoreType.DMA((3,))] * n
                  + [pltpu.HBM(a.shape, a.dtype) for a in list(srcs) + list(lands)]
                  + [jax.ShapeDtypeStruct((8, 128), F32)],
        input_output_aliases={i: 3 * n + i for i in range(2 * n)},
        compiler_params=pltpu.CompilerParams(has_side_effects=EFFECT),
    )(*[hbm(a) for a in srcs], *[hbm(a) for a in lands], after)
    return res[:n], res[n:2 * n], res[2 * n:3 * n], res[3 * n:4 * n], res[4 * n:5 * n], res[-1]


def _gather2_forward(recv_far, srcs, lands, after, name):
    n = len(lands)
    after = list(after) if isinstance(after, (list, tuple)) else [after]

    def body(*refs):
        src_refs, land_refs, far_sems = refs[:n], refs[n:2 * n], refs[2 * n:3 * n]
        outs = refs[3 * n + len(after):]
        send, recv, token = outs[:n], outs[n:2 * n], outs[-1]
        _, (sib, _), far = _chip_peers()
        for a in range(n):
            rows = src_refs[a].shape[0]
            for j, (peer, pidx, _) in enumerate(far):
                got = _block(land_refs[a], rows, pidx)
                pltpu.make_async_remote_copy(src_ref=src_refs[a], dst_ref=got, send_sem=send[a].at[j],
                                             recv_sem=far_sems[a].at[j], device_id=peer,
                                             device_id_type=MESH).wait_recv()
                pltpu.make_async_remote_copy(src_ref=got, dst_ref=got, send_sem=send[a].at[j], recv_sem=recv[a].at[j],
                                             device_id=sib, device_id_type=MESH).start()
        token[...] = jnp.zeros_like(token)

    res = _call(
        body, name=name,
        in_specs=[HBM] * (2 * n) + [SEM] * n + [ANY] * len(after),
        out_specs=[SEM] * (2 * n) + [HBM] * n + [pl.BlockSpec(memory_space=pltpu.VMEM)],
        out_shape=[pltpu.SemaphoreType.DMA((3,))] * (2 * n) + [pltpu.HBM(a.shape, a.dtype) for a in lands]
                  + [jax.ShapeDtypeStruct((8, 128), F32)],
        input_output_aliases={n + i: 2 * n + i for i in range(n)},
        compiler_params=pltpu.CompilerParams(has_side_effects=EFFECT),
    )(*srcs, *lands, *recv_far, *after)
    return res[:n], res[n:2 * n], res[2 * n:3 * n], res[-1]


def _gather2_wait(send, recv_sib, fwd_send, fwd_recv, srcs, lands, after, name):
    n = len(lands)

    def body(*refs):
        src_refs, land_refs = refs[:n], refs[n:2 * n]
        s_refs, rs_refs, fs_refs, fr_refs = (refs[(2 + q) * n:(3 + q) * n] for q in range(4))
        me, (sib, sib_idx), far = _chip_peers()
        for a in range(n):
            rows = src_refs[a].shape[0]
            mine = _block(land_refs[a], rows, me)
            pltpu.make_async_copy(src_refs[a], mine, s_refs[a].at[4]).wait()
            to_sib = pltpu.make_async_remote_copy(src_ref=src_refs[a], dst_ref=_block(land_refs[a], rows, sib_idx),
                                                  send_sem=s_refs[a].at[0], recv_sem=rs_refs[a].at[0], device_id=sib,
                                                  device_id_type=MESH)
            to_sib.wait_send()
            to_sib.wait_recv()
            for j, (peer, pidx, pair_idx) in enumerate(far):
                pltpu.make_async_remote_copy(src_ref=src_refs[a], dst_ref=mine, send_sem=s_refs[a].at[1 + j],
                                             recv_sem=fr_refs[a].at[j], device_id=peer,
                                             device_id_type=MESH).wait_send()
                fwd = pltpu.make_async_remote_copy(src_ref=_block(land_refs[a], rows, pidx),
                                                   dst_ref=_block(land_refs[a], rows, pair_idx),
                                                   send_sem=fs_refs[a].at[j], recv_sem=fr_refs[a].at[j], device_id=sib,
                                                   device_id_type=MESH)
                fwd.wait_send()
                fwd.wait_recv()

    res = _call(
        body, name=name,
        in_specs=[HBM] * (2 * n) + [SEM] * (4 * n) + [ANY],
        out_specs=[HBM] * (2 * n),
        out_shape=[pltpu.HBM(a.shape, a.dtype) for a in list(srcs) + list(lands)],
        input_output_aliases={i: i for i in range(2 * n)},
        compiler_params=pltpu.CompilerParams(has_side_effects=EFFECT),
    )(*srcs, *lands, *send, *recv_sib, *fwd_send, *fwd_recv, after)
    return res[n:]


def _exchange_wait(scatter, send_sems, recv_sems, srcs, lands, after, name):
    n = len(srcs)
    after = list(after) if isinstance(after, (list, tuple)) else [after]

    def body(*refs):
        src_refs, land_refs = refs[:n], refs[n:2 * n]
        send_refs, recv_refs = refs[2 * n:3 * n], refs[3 * n:4 * n]
        for send, recv in _peer_copies(scatter, src_refs, land_refs, send_refs, recv_refs):
            send.wait_send()
            recv.wait_recv()
        if not scatter:
            for own in _own_copies(src_refs, land_refs, send_refs):
                own.wait()

    res = _call(
        body, name=name,
        in_specs=[HBM] * (2 * n) + [SEM] * (2 * n) + [ANY] * len(after),
        out_specs=[HBM] * (2 * n),
        out_shape=[pltpu.HBM(a.shape, a.dtype) for a in list(srcs) + list(lands)],
        input_output_aliases={i: i for i in range(2 * n)},
        compiler_params=pltpu.CompilerParams(has_side_effects=EFFECT),
    )(*srcs, *lands, *send_sems, *recv_sems, *after)
    return res[:n], res[n:]


def _row_tile(rows):
    if rows <= 512:
        return rows
    return max([tr for tr in range(16, 513, 16) if rows % tr == 0] or [rows])


def _sum_parts(own, got, me, name):
    _, rows, w = own.shape
    tr = _row_tile(rows)

    def body(me_ref, a_ref, b_ref, o_ref):
        s = a_ref[...].astype(F32)
        for j in range(N_DEV - 1):
            s = s + b_ref[j].astype(F32)
        o_ref[...] = s

    return _call(
        body, name=name,
        grid_spec=pltpu.PrefetchScalarGridSpec(
            num_scalar_prefetch=1, grid=(rows // tr,),
            in_specs=[pl.BlockSpec((None, tr, w), lambda i, me_ref: (me_ref[0], i, 0)),
                      pl.BlockSpec((N_DEV - 1, tr, w), lambda i, me_ref: (0, i, 0))],
            out_specs=pl.BlockSpec((tr, w), lambda i, me_ref: (i, 0))),
        out_shape=jax.ShapeDtypeStruct((rows, w), F32),
        compiler_params=_params("parallel"),
    )(me, own, got)


def _sum_devices(stacked, name):
    k, rows, w = stacked.shape
    tr = _row_tile(rows)

    def body(a_ref, o_ref):
        s = a_ref[0]
        for j in range(1, k):
            s = s + a_ref[j]
        o_ref[...] = s

    return _call(
        body, name=name, grid=(rows // tr,),
        in_specs=[pl.BlockSpec((k, tr, w), lambda i: (0, i, 0))],
        out_specs=pl.BlockSpec((tr, w), lambda i: (i, 0)),
        out_shape=jax.ShapeDtypeStruct((rows, w), F32),
        compiler_params=_params("parallel"),
    )(stacked)


def _adamw(w, g, m, v, name):
    rows, cols = w.shape
    tr = _row_tile(rows)

    def body(w_ref, g_ref, m_ref, v_ref, d_ref, nm_ref, nv_ref):
        gv = g_ref[...]
        nm = ADAM_B1 * m_ref[...] + (1.0 - ADAM_B1) * gv
        nv = ADAM_B2 * v_ref[...] + (1.0 - ADAM_B2) * (gv * gv)
        m_hat = nm / (1.0 - ADAM_B1 ** ADAM_STEP)
        v_hat = nv / (1.0 - ADAM_B2 ** ADAM_STEP)
        d_ref[...] = -ADAM_LR * (m_hat / (jnp.sqrt(v_hat) + ADAM_EPS) + ADAM_WD * w_ref[...])
        nm_ref[...] = nm
        nv_ref[...] = nv

    spec = pl.BlockSpec((tr, cols), lambda i: (i, 0))
    return _call(
        body, name=name, grid=(rows // tr,),
        in_specs=[spec] * 4, out_specs=[spec] * 3,
        out_shape=[jax.ShapeDtypeStruct((rows, cols), F32)] * 3,
        compiler_params=_params("parallel"),
    )(w, g, m, v)


SMALL = ("a_ln_g", "a_ln_b", "a_w_s", "a_b_s", "mix_pre_g", "mix_post_g", "ffn_pre_g", "ffn_post_g")


def _pack_small(parts, d, last_row=None):
    rows = [parts[k].reshape(-1, d) for k in SMALL] + ([] if last_row is None else [last_row])
    flat = jnp.concatenate(rows, axis=0)
    return jnp.pad(flat, ((0, -flat.shape[0] % 8), (0, 0)))


def _unpack_small(flat, like):
    out, r = {}, 0
    for k in SMALL:
        n = like[k].size // flat.shape[1]
        out[k] = flat[r:r + n].reshape(like[k].shape)
        r += n
    return out


def kernel(x, a_w_in, a_ln_g, a_ln_b, a_w_s, a_b_s, a_w_out, b_w_in, b_w_grp, b_scale, b_w_out, mix_pre_g, mix_post_g, ffn_pre_g, ffn_post_g, ffn_w_gate, ffn_w_up, ffn_w_down, loss_target, m_a_w_in, m_a_ln_g, m_a_ln_b, m_a_w_s, m_a_b_s, m_a_w_out, m_b_w_in, m_b_w_grp, m_b_scale, m_b_w_out, m_mix_pre_g, m_mix_post_g, m_ffn_pre_g, m_ffn_post_g, m_ffn_w_gate, m_ffn_w_up, m_ffn_w_down, v_a_w_in, v_a_ln_g, v_a_ln_b, v_a_w_s, v_a_b_s, v_a_w_out, v_b_w_in, v_b_w_grp, v_b_scale, v_b_w_out, v_mix_pre_g, v_mix_post_g, v_ffn_pre_g, v_ffn_post_g, v_ffn_w_gate, v_ffn_w_up, v_ffn_w_down):
    args = dict(locals())
    names = ("a_w_in", "a_ln_g", "a_ln_b", "a_w_s", "a_b_s", "a_w_out", "b_w_in", "b_w_grp", "b_scale", "b_w_out",
             "mix_pre_g", "mix_post_g", "ffn_pre_g", "ffn_post_g", "ffn_w_gate", "ffn_w_up", "ffn_w_down")
    w = {k: args[k] for k in names}
    mom = {k: args["m_" + k] for k in names}
    var = {k: args["v_" + k] for k in names}

    t, d = x.shape[1], x.shape[2]
    ffn_local = ffn_w_gate.shape[2]
    lay = _Layout(d, ffn_local)
    me = 4 * lax.axis_index("x") + 2 * lax.axis_index("y") + lax.axis_index("c")
    me1 = jnp.reshape(me, (1,)).astype(jnp.int32)

    def landing(block):
        return lax.empty((N_DEV * block.shape[0],) + block.shape[1:], block.dtype)

    def shards(i, mixer, zero):
        j = i // 2
        if not mixer:
            parts = [ffn_w_gate[i].T, ffn_w_up[i].T, ffn_w_down[i]]
        elif i % 2 == 0:
            parts = [a_w_in[j].T, a_w_out[j]]
        else:
            parts = [b_w_in[j], b_w_out[j]]
        return [(p + zero).astype(BF16) for p in parts]

    nsub = 2 * DEPTH
    wg = [None] * nsub
    first = shards(0, True, 0.0)
    f_send, f_sib, f_far, first, f_zones, f_token = _gather2_first(
        first, [landing(b) for b in first], jnp.zeros((8, 128), F32), "gather_first_start")
    zero = f_token[0, 0]
    ngrp = len(B_WINDOWS)
    grp_local = b_w_grp.shape[2]
    sdev = b_scale.shape[1]
    side_rows = 2 * ngrp * grp_local
    side = jnp.concatenate(
        [b_w_grp.reshape(side_rows, B_GROUP_DIM),
         jnp.pad(b_scale, ((0, 6), (0, B_GROUP_DIM - sdev)))], axis=0) + zero
    later, where = [side], [slice(0, 1)]
    for k in range(1, nsub):
        new = shards(k // 2, k % 2 == 0, zero)
        where.append(slice(len(later), len(later) + len(new)))
        later += new
    send_sems, recv_sems, later, zones, token = _exchange_start(
        False, later, [landing(b) for b in later], f_token, "gather_start")
    turned = ("ffn_w_gate", "ffn_w_up")
    turn = lambda a: jnp.swapaxes(a, 1, 2)
    state = {k: tuple(turn(a[k]) for a in (w, mom, var)) for k in turned}
    state["small"] = tuple(_pack_small(a, d) for a in (w, mom, var))
    ready = [a for group in state.values() for a in group]
    fwd_send, fwd_recv, f_zones, fwd_token = _gather2_forward(f_far, first, f_zones, [token] + ready,
                                                              "gather_first_forward")
    wg[0] = _gather2_wait(f_send, f_sib, fwd_send, fwd_recv, first, f_zones, fwd_token, "gather_first_wait")

    def gathered(k, after):
        s = where[k]
        _, got = _exchange_wait(False, send_sems[s], recv_sems[s], later[s], zones[s], after, f"gather_wait_{k}")
        return got

    row = lambda a: a.reshape(1, -1)
    bst = jnp.transpose(a_b_s, (0, 2, 1))

    tm = 256 if t % 256 == 0 else CHUNK
    tm_abwd = tm
    tm_b = 512 if t % 512 == 0 else tm
    tm_f = tm

    saved = []
    h = x[0]
    wgrp_full = scale_full = None
    for i in range(DEPTH):
        j = i // 2
        gpre = row(mix_pre_g[i])
        if i > 0:
            wg[2 * i] = gathered(2 * i, h)
        if i % 2 == 0:
            x1, h1, gp, u, vh, rs, gated, m = _a_fwd(h, gpre, wg[2 * i], lay, j, row(a_ln_g[j]), row(a_ln_b[j]),
                                                     a_w_s[j], bst[j], row(mix_post_g[i]), tm, f"a_fwd_{j}")
            mix = dict(h1=h1, gp=gp, u=u, vh=vh, rs=rs, gated=gated, m=m)
        else:
            if wgrp_full is None:
                side_g = gathered(0, h)[0].reshape(N_DEV, side_rows + 8, B_GROUP_DIM)
                wgrp_full = (side_g[:, :side_rows].reshape(N_DEV, 2, ngrp, grp_local, B_GROUP_DIM)
                             .transpose(1, 2, 0, 3, 4).reshape(2, ngrp, B_GROUP_DIM, B_GROUP_DIM).astype(BF16))
                scale_full = (side_g[:, side_rows:side_rows + 2, :sdev].transpose(1, 0, 2)
                              .reshape(2, 1, N_DEV * sdev))
            x1, h1, pooled, mixed, m = _b_fwd(h, gpre, wg[2 * i], lay, j, wgrp_full[j], scale_full[j],
                                              row(mix_post_g[i]), tm_b, f"b_fwd_{j}")
            mix = dict(h1=h1, pooled=pooled, mixed=mixed, m=m)
        wg[2 * i + 1] = gathered(2 * i + 1, x1)
        x2, h2, acts, f, *loss_acc = _f_fwd(x1, row(ffn_pre_g[i]), wg[2 * i + 1], lay, i, row(ffn_post_g[i]), tm_f,
                                               f"f_fwd_{i}", loss_target[0] if i == DEPTH - 1 else None)
        saved.append(dict(x=h, x1=x1, mix=mix, h2=h2, acts=acts, f=f))
        h = x2
    dy, (loss_acc,) = h, loss_acc

    small_g = {k: [None] * w[k].shape[0] for k in SMALL}
    dgrp, dscale = [None, None], [None, None]
    pending = [None] * nsub
    token = jnp.zeros((8, 128), F32)

    def scatter(k, gbuf):
        got = pltpu.with_memory_space_constraint(lax.empty((N_DEV - 1,) + gbuf.shape[1:], gbuf.dtype), pltpu.HBM)
        ss, rs, src, zone, tok = _exchange_start(True, [gbuf], [got], token, f"scatter_start_{k}")
        pending[k] = (ss, rs, src, zone)
        return tok

    def small_exchanges():
        side_grad = jnp.concatenate(
            [jnp.stack(dgrp).reshape(2, ngrp, N_DEV, grp_local, B_GROUP_DIM).transpose(2, 0, 1, 3, 4)
             .reshape(N_DEV, side_rows, B_GROUP_DIM),
             jnp.pad(jnp.stack(dscale).reshape(2, N_DEV, sdev).transpose(1, 0, 2),
                     ((0, 0), (0, 6), (0, B_GROUP_DIM - sdev)))], axis=1)
        small_part = _pack_small({k: jnp.stack(small_g[k]) for k in SMALL}, d,
                                 jnp.broadcast_to(loss_acc[:1, :1], (1, d)))
        got = pltpu.with_memory_space_constraint(lax.empty((N_DEV - 1,) + side_grad.shape[1:], F32), pltpu.HBM)
        side_x = _exchange_start(True, [side_grad], [got], token, "side_scatter_start")
        small_x = _exchange_start(False, [small_part], [landing(small_part)], side_x[4], "small_gather_start")
        return side_x[:4], small_x[:4], small_x[4]

    for i in reversed(range(DEPTH)):
        sv = saved[i]
        j = i // 2
        wf, wm = wg[2 * i + 1], wg[2 * i]
        dx1, df, dacts, dgpost, dgpre = _f_bwd(dy, sv["f"], sv["x1"], sv["acts"], row(ffn_pre_g[i]), wf, lay, i,
                                               row(ffn_post_g[i]), token, tm_f, f"f_bwd_{i}")
        small_g["ffn_post_g"][i], small_g["ffn_pre_g"][i] = dgpost[0], dgpre[0]
        gbuf = _grad_ffn(dacts, sv["h2"], df, lay.ffn_rows, f"g_ffn_{i}")
        token = scatter(2 * i + 1, gbuf)
        mix = sv["mix"]
        gpost = row(mix_post_g[i])
        if i % 2 == 0:
            dx, dm, dz, dgpost, dgpre, dlng, dlnb, dws, dbt = _a_bwd(
                dx1, mix["m"], sv["x"], mix["gp"], mix["u"], mix["vh"], mix["rs"], row(mix_pre_g[i]), wm, lay, j,
                row(a_ln_g[j]), row(a_ln_b[j]), a_w_s[j], bst[j], gpost, token, tm_abwd, f"a_bwd_{j}")
            small_g["a_ln_g"][j], small_g["a_ln_b"][j] = dlng[0], dlnb[0]
            small_g["a_w_s"][j], small_g["a_b_s"][j] = dws, dbt[:, :A_GROUPS].T
            small_g["mix_post_g"][i], small_g["mix_pre_g"][i] = dgpost[0], dgpre[0]
            order = None
            if i == 0:
                side_x, small_x, order = small_exchanges()
            gbuf = _grad_into(lay.a_total, dz, mix["h1"], lay.a_in[j], lay.a_in_rows, f"g_a_in_{j}", after=order)
            gbuf = _grad_into(gbuf, mix["gated"], dm, lay.a_out[j], lay.a_out_rows, f"g_a_out_{j}")
        else:
            dx, dm, draw, dp, dgpost, dgpre, dsc = _b_bwd(
                dx1, mix["m"], sv["x"], mix["pooled"], row(mix_pre_g[i]), wm, lay, j, wgrp_full[j], scale_full[j],
                gpost, token, tm_b, f"b_bwd_{j}")
            dscale[j] = dsc[0]
            dgrp[j] = _grad_grouped(mix["pooled"], draw, f"g_b_grp_{j}")
            gbuf = _grad_into(lay.b_total, mix["h1"], dp, lay.b_in[j], lay.b_rows, f"g_b_in_{j}")
            gbuf = _grad_into(gbuf, mix["mixed"], dm, lay.b_out[j], lay.b_rows, f"g_b_out_{j}")
            small_g["mix_post_g"][i], small_g["mix_pre_g"][i] = dgpost[0], dgpre[0]
        token = scatter(2 * i, gbuf)
        dy = dx
    grad_x = dy[None]

    g_sub = [None] * nsub

    def arrived(k, after):
        ss, rs, src, zone = pending[k]
        (own,), (got,) = _exchange_wait(True, ss, rs, src, zone, after, f"scatter_wait_{k}")
        g_sub[k] = _sum_parts(own, got, me1, f"sum_grads_{k}")

    def rows_of(k, off, n):
        return g_sub[k][off:off + n]

    grads, delta, new_m, new_v = {}, {}, {}, {}

    def update(k):
        back = turn if k in turned else (lambda a: a)
        wk, mk, vk = state[k] if k in turned else (w[k], mom[k], var[k])
        shape = wk.shape
        two = lambda a: a.reshape(-1, shape[-1])
        dl, nm, nv = _adamw(two(wk), two(grads[k]), two(mk), two(vk), f"adamw_{k}")
        delta[k], new_m[k], new_v[k] = (back(a.reshape(shape)) for a in (dl, nm, nv))
        grads[k] = back(grads[k])

    for k in range(1, nsub):
        arrived(k, token)
    grads["ffn_w_gate"] = jnp.stack([rows_of(2 * l + 1, lay.gate[l], ffn_local) for l in range(DEPTH)])
    grads["ffn_w_up"] = jnp.stack([rows_of(2 * l + 1, lay.up[l], ffn_local) for l in range(DEPTH)])
    grads["ffn_w_down"] = jnp.stack([rows_of(2 * l + 1, lay.down[l], ffn_local) for l in range(DEPTH)])
    grads["b_w_in"] = jnp.stack([rows_of(4 * j + 2, lay.b_in[j], lay.b_rows) for j in range(2)])
    grads["b_w_out"] = jnp.stack([rows_of(4 * j + 2, lay.b_out[j], lay.b_rows) for j in range(2)])
    early = ("ffn_w_gate", "ffn_w_up", "ffn_w_down", "b_w_in", "b_w_out")
    for k in early:
        update(k)

    (side_own,), (side_got,) = _exchange_wait(True, *side_x, [delta[k] for k in early], "side_scatter_wait")
    g_side = _sum_parts(side_own, side_got, me1, "sum_side")
    grads["b_w_grp"] = g_side[:side_rows].reshape(b_w_grp.shape)
    grads["b_scale"] = g_side[side_rows:side_rows + 2, :sdev]
    update("b_w_grp")
    update("b_scale")
    _, (small_all,) = _exchange_wait(False, *small_x, [delta["b_w_grp"], delta["b_scale"]], "small_gather_wait")
    small_sum = _sum_devices(small_all.reshape(N_DEV, -1, d), "sum_small")
    g_small = _unpack_small(small_sum, w)
    loss = small_sum[sum(w[k].size for k in SMALL) // d, 0]
    grads.update(g_small)
    dl, nm, nv = _adamw(state["small"][0], _pack_small(g_small, d), state["small"][1], state["small"][2],
                        "adamw_small")
    delta.update(_unpack_small(dl, w))
    new_m.update(_unpack_small(nm, w))
    new_v.update(_unpack_small(nv, w))

    arrived(0, dl)
    grads["a_w_in"] = jnp.stack([rows_of(4 * j, lay.a_in[j], lay.a_in_rows).T for j in range(2)])
    grads["a_w_out"] = jnp.stack([rows_of(4 * j, lay.a_out[j], lay.a_out_rows) for j in range(2)])
    update("a_w_in")
    update("a_w_out")

    return (loss, grad_x, *[grads[k] for k in names], *[delta[k] for k in names], *[new_m[k] for k in names],
            *[new_v[k] for k in names])
```

```python
import math

import jax
import jax.numpy as jnp
from jax import lax
from jax.experimental import pallas as pl
from jax.experimental.pallas import tpu as pltpu

F32 = jnp.float32
BF16 = jnp.bfloat16
MESH = pl.DeviceIdType.MESH
ANY = pl.BlockSpec(memory_space=pl.ANY)

N_DEV = 8
EPS = 1e-6
CHUNK = 128
A_GROUPS = 8
A_GROUP_DIM = 256
B_WINDOWS = (2, 4, 8, 16)
B_GROUP_DIM = 256
HALO = 16
DEPTH = 4

ADAM_LR = 0.001
ADAM_B1 = 0.9
ADAM_B2 = 0.999
ADAM_EPS = 1e-08
ADAM_WD = 0.01
ADAM_STEP = 10

VMEM_LIMIT_BYTES = 60 * 1024 * 1024

INV_SQRT2 = 1.0 / math.sqrt(2.0)
LOG2_E = 1.0 / math.log(2.0)
INV_SQRT_2PI = 1.0 / math.sqrt(2.0 * math.pi)


def _call(body, **kw):
    return pl.pallas_call(body, **kw)


def _params(*semantics):
    return pltpu.CompilerParams(dimension_semantics=semantics or None, vmem_limit_bytes=VMEM_LIMIT_BYTES)


def _resident(shape, index):
    return pl.BlockSpec(shape, lambda *_: index, pipeline_mode=pl.Buffered(1))


def _rows(tm, width):
    return pl.BlockSpec((tm, width), lambda i: (i, 0))


def _nn(a, b):
    return jnp.dot(a, b, preferred_element_type=F32)


def _nt(a, b):
    return lax.dot_general(a, b, (((1,), (1,)), ((), ())), preferred_element_type=F32)


def _tn(a, b):
    return lax.dot_general(a, b, (((0,), (0,)), ((), ())), preferred_element_type=F32)


def _rms_fwd(x, g):
    r = lax.rsqrt(jnp.mean(x * x, axis=-1, keepdims=True) + EPS)
    return x * r * g


def _rms_bwd(x, g, dy):
    r = lax.rsqrt(jnp.mean(x * x, axis=-1, keepdims=True) + EPS)
    xh = x * r
    dg = jnp.sum(dy * xh, axis=0, keepdims=True)
    dxh = dy * g
    dx = r * (dxh - xh * jnp.mean(dxh * xh, axis=-1, keepdims=True))
    return dx, dg


SLAB = 16


def _slabs(n):
    return [slice(r, r + SLAB) for r in range(0, n, SLAB)]


def _rms_bwd_slabs(x_at, dy_at, g, n, n_sum, emit):
    acc = jnp.zeros((8, g.shape[1]), F32)
    for rows in _slabs(n):
        x = x_at(rows)
        dy = dy_at(rows)
        r = lax.rsqrt(jnp.mean(x * x, axis=-1, keepdims=True) + EPS)
        xh = x * r
        if rows.start < n_sum:
            p = dy * xh
            acc = acc + p[:8] + p[8:]
        dxh = dy * g
        emit(rows, r * (dxh - xh * jnp.mean(dxh * xh, axis=-1, keepdims=True)))
    return jnp.sum(acc, axis=0, keepdims=True)


def _gelu(z):
    phi = 0.5 + 0.5 * lax.erf(z * INV_SQRT2)
    e = jnp.exp2(z * z * (-0.5 * LOG2_E))
    return z * phi, phi + z * e * INV_SQRT_2PI


def _layernorm_stats(v):
    mu = jnp.mean(v, axis=-1, keepdims=True)
    xc = v - mu
    rs = lax.rsqrt(jnp.mean(xc * xc, axis=-1, keepdims=True) + EPS)
    return xc * rs, rs


def _tril_mask():
    r = lax.broadcasted_iota(jnp.int32, (CHUNK, CHUNK), 0)
    c = lax.broadcasted_iota(jnp.int32, (CHUNK, CHUNK), 1)
    return r >= c


class _Layout:
    def __init__(self, d, ffn_rows):
        self.ffn_rows = ffn_rows
        self.gate, self.up, self.down = [0] * DEPTH, [self.ffn_rows] * DEPTH, [2 * self.ffn_rows] * DEPTH
        self.f_total = 3 * self.ffn_rows
        self.a_in_rows, self.a_out_rows, self.b_rows = 4 * d // N_DEV, 2 * d // N_DEV, d // N_DEV
        self.a_in, self.a_out = [0, 0], [self.a_in_rows] * 2
        self.a_total = self.a_in_rows + self.a_out_rows
        self.b_in, self.b_out = [0, 0], [self.b_rows] * 2
        self.b_total = 2 * self.b_rows


def _wspec(rows, d):
    return _resident((N_DEV * rows, d), (0, 0))


def _a_fwd(x, gpre, wg, lay, j, lng, lnb, ws, bst, gpost, tm, name):
    t, d = x.shape
    aw = 2 * d
    nch = tm // CHUNK

    def body(x_ref, gpre_ref, win_ref, lng_ref, lnb_ref, ws_ref, bst_ref, wout_ref, gpost_ref,
             x1_ref, h1_ref, gp_ref, u_ref, vh_ref, rs_ref, gated_ref, m_ref):
        xv = x_ref[...]
        h1 = _rms_fwd(xv, gpre_ref[...]).astype(BF16)
        h1_ref[...] = h1
        z = _nt(h1, win_ref[...])
        u, du_dz = _gelu(z[:, :aw])
        v, dv_dz = _gelu(z[:, aw:])
        gp_ref[:, :aw] = du_dz.astype(BF16)
        gp_ref[:, aw:] = dv_dz.astype(BF16)
        u_ref[...] = u.astype(BF16)
        vh, rs = _layernorm_stats(v)
        vh_ref[...] = vh.astype(BF16)
        rs_ref[...] = jnp.broadcast_to(rs, rs_ref.shape)
        vn = (vh * lng_ref[...] + lnb_ref[...]).astype(BF16)
        mask = _tril_mask()
        for g in range(A_GROUPS):
            wm = jnp.where(mask, ws_ref[g], 0.0).astype(BF16)
            cols = slice(g * A_GROUP_DIM, (g + 1) * A_GROUP_DIM)
            for c in range(nch):
                rows = slice(c * CHUNK, (c + 1) * CHUNK)
                sv = _nn(wm, vn[rows, cols]) + bst_ref[:, g:g + 1]
                gated_ref[rows, cols] = (u[rows, cols] * sv).astype(BF16)
        m = _nn(gated_ref[...], wout_ref[...])
        m_ref[...] = m
        x1_ref[...] = xv + _rms_fwd(m, gpost_ref[...])

    vec = lambda w: _resident((1, w), (0, 0))
    return _call(
        body, name=name, grid=(t // tm,),
        in_specs=[_rows(tm, d), vec(d), _wspec(lay.a_in_rows, d), vec(aw), vec(aw),
                  _resident((A_GROUPS, CHUNK, CHUNK), (0, 0, 0)), _resident((CHUNK, A_GROUPS), (0, 0)),
                  _wspec(lay.a_out_rows, d), vec(d)],
        out_specs=[_rows(tm, d), _rows(tm, d), _rows(tm, 2 * aw), _rows(tm, aw), _rows(tm, aw), _rows(tm, 128),
                   _rows(tm, aw), _rows(tm, d)],
        out_shape=[jax.ShapeDtypeStruct((t, d), F32), jax.ShapeDtypeStruct((t, d), BF16),
                   jax.ShapeDtypeStruct((t, 2 * aw), BF16), jax.ShapeDtypeStruct((t, aw), BF16),
                   jax.ShapeDtypeStruct((t, aw), BF16), jax.ShapeDtypeStruct((t, 128), F32),
                   jax.ShapeDtypeStruct((t, aw), BF16), jax.ShapeDtypeStruct((t, d), F32)],
        compiler_params=_params("parallel"),
    )(x, gpre, wg[0], lng, lnb, ws, bst, wg[1], gpost)


def _a_bwd(dx1, m, x, gp, u, vh, rs, gpre, wg, lay, j, lng, lnb, ws, bst, gpost, after, tm, name):
    t, d = x.shape
    aw = 2 * d
    nch = tm // CHUNK

    def body(dx1_ref, m_ref, x_ref, gp_ref, u_ref, vh_ref, rs_ref, gpre_ref, win_ref, lng_ref, lnb_ref, ws_ref, bst_ref,
             wout_ref, gpost_ref, after_ref,
             dx_ref, dm_ref, dz_ref, dgpost_ref, dgpre_ref, dlng_ref, dlnb_ref, dws_ref, dbt_ref, dvn_ref):
        @pl.when(pl.program_id(0) == 0)
        def _():
            for r in (dgpost_ref, dgpre_ref, dlng_ref, dlnb_ref, dws_ref, dbt_ref):
                r[...] = jnp.zeros_like(r)

        def put_dm(rows, dx):
            dm_ref[rows, :] = dx.astype(BF16)

        dgpost_ref[...] += _rms_bwd_slabs(lambda rows: m_ref[rows, :], lambda rows: dx1_ref[rows, :], gpost_ref[...],
                                          tm, tm, put_dm)
        dgated = _nt(dm_ref[...], wout_ref[...])

        vh = vh_ref[...].astype(F32)
        rs = rs_ref[:, :1]
        lng_v = lng_ref[...]
        vn = (vh * lng_v + lnb_ref[...]).astype(BF16)
        mask = _tril_mask()
        lane = lax.broadcasted_iota(jnp.int32, (CHUNK, CHUNK), 1)
        for g in range(A_GROUPS):
            wm = jnp.where(mask, ws_ref[g], 0.0).astype(BF16)
            cols = slice(g * A_GROUP_DIM, (g + 1) * A_GROUP_DIM)
            dws_g = jnp.zeros((CHUNK, CHUNK), F32)
            db_g = jnp.zeros((CHUNK, 1), F32)
            for c in range(nch):
                rows = slice(c * CHUNK, (c + 1) * CHUNK)
                vn_cg = vn[rows, cols]
                sv = _nn(wm, vn_cg) + bst_ref[:, g:g + 1]
                dg_cg = dgated[rows, cols]
                dsv = dg_cg * u_ref[rows, cols].astype(F32)
                dsv_bf = dsv.astype(BF16)
                db_g = db_g + jnp.sum(dsv, axis=1, keepdims=True)
                dws_g = dws_g + _nt(dsv_bf, vn_cg)
                dvn_ref[rows, cols] = _tn(wm, dsv_bf)
                dz_ref[rows, cols] = (dg_cg * sv * gp_ref[rows, cols].astype(F32)).astype(BF16)
            dws_ref[g] += jnp.where(mask, dws_g, 0.0)
            dbt_ref[...] += jnp.where(lane == g, db_g, 0.0)
        dvn = dvn_ref[...]
        dlng_ref[...] += jnp.sum(dvn * vh, axis=0, keepdims=True)
        dlnb_ref[...] += jnp.sum(dvn, axis=0, keepdims=True)
        dvh = dvn * lng_v
        dv = rs * (dvh - jnp.mean(dvh, axis=-1, keepdims=True) - vh * jnp.mean(dvh * vh, axis=-1, keepdims=True))
        dz_ref[:, aw:] = (dv * gp_ref[:, aw:].astype(F32)).astype(BF16)
        dh1 = _nn(dz_ref[...], win_ref[...])

        def put_dx(rows, dx):
            dx_ref[rows, :] = dx1_ref[rows, :] + dx

        dgpre_ref[...] += _rms_bwd_slabs(lambda rows: x_ref[rows, :], lambda rows: dh1[rows, :], gpre_ref[...],
                                         tm, tm, put_dx)

    vec = lambda w: _resident((1, w), (0, 0))
    acc = lambda shape: pl.BlockSpec(shape, lambda i: (0,) * len(shape))
    return _call(
        body, name=name, grid=(t // tm,),
        in_specs=[_rows(tm, d), _rows(tm, d), _rows(tm, d), _rows(tm, 2 * aw), _rows(tm, aw), _rows(tm, aw),
                  _rows(tm, 128), vec(d), _wspec(lay.a_in_rows, d), vec(aw), vec(aw),
                  _resident((A_GROUPS, CHUNK, CHUNK), (0, 0, 0)), _resident((CHUNK, A_GROUPS), (0, 0)),
                  _wspec(lay.a_out_rows, d), vec(d), ANY],
        out_specs=[_rows(tm, d), _rows(tm, d), _rows(tm, 2 * aw), acc((1, d)), acc((1, d)), acc((1, aw)), acc((1, aw)),
                   acc((A_GROUPS, CHUNK, CHUNK)), acc((CHUNK, CHUNK))],
        out_shape=[jax.ShapeDtypeStruct((t, d), F32), jax.ShapeDtypeStruct((t, d), BF16),
                   jax.ShapeDtypeStruct((t, 2 * aw), BF16), jax.ShapeDtypeStruct((1, d), F32),
                   jax.ShapeDtypeStruct((1, d), F32), jax.ShapeDtypeStruct((1, aw), F32),
                   jax.ShapeDtypeStruct((1, aw), F32), jax.ShapeDtypeStruct((A_GROUPS, CHUNK, CHUNK), F32),
                   jax.ShapeDtypeStruct((CHUNK, CHUNK), F32)],
        scratch_shapes=[pltpu.VMEM((tm, aw), F32)],
        compiler_params=_params("arbitrary"),
    )(dx1, m, x, gp, u, vh, rs, gpre, wg[0], lng, lnb, ws, bst, wg[1], gpost, after)


def _window_counts(first_row, n, win):
    tpos = first_row + lax.broadcasted_iota(jnp.int32, (n, 1), 0)
    return jnp.clip(tpos + 1, 1, win).astype(F32)


def _b_fwd(x, gpre, wg, lay, j, wgrp, scale, gpost, tm, name):
    t, d = x.shape
    n = tm + HALO
    ngrp = len(B_WINDOWS)

    def body(x_ref, xprev_ref, gpre_ref, win_ref, wgrp_ref, scale_ref, wout_ref, gpost_ref,
             x1_ref, h1_ref, pooled_ref, mixed_ref, m_ref):
        i = pl.program_id(0)
        xv = x_ref[...]
        keep = jnp.where(i > 0, 1.0, 0.0)
        xe = jnp.concatenate([xprev_ref[...] * keep, xv], axis=0)
        h1e = _rms_fwd(xe, gpre_ref[...]).astype(BF16)
        h1_ref[...] = h1e[HALO:]
        p = _nn(h1e, win_ref[...])
        acc = p
        shift = 1
        for g, win in enumerate(B_WINDOWS):
            lo = g * B_GROUP_DIM
            if g > 0:
                acc = acc[:, B_GROUP_DIM:]
            while shift < win:
                acc = acc + pltpu.roll(acc, shift, 0)
                shift *= 2
            cnt = _window_counts(i * tm - HALO, n, win)
            pooled = acc[:, :B_GROUP_DIM] / cnt - p[:, lo:lo + B_GROUP_DIM]
            pooled_ref[:, lo:lo + B_GROUP_DIM] = pooled[HALO:].astype(BF16)
        for g in range(ngrp):
            cols = slice(g * B_GROUP_DIM, (g + 1) * B_GROUP_DIM)
            raw = _nn(pooled_ref[:, cols], wgrp_ref[g])
            mixed_ref[:, cols] = (raw * scale_ref[:, cols]).astype(BF16)
        m = _nn(mixed_ref[...], wout_ref[...])
        m_ref[...] = m
        x1_ref[...] = xv + _rms_fwd(m, gpost_ref[...])

    vec = lambda w: _resident((1, w), (0, 0))
    per = tm // HALO
    return _call(
        body, name=name, grid=(t // tm,),
        in_specs=[_rows(tm, d), pl.BlockSpec((HALO, d), lambda i: (jnp.maximum(i * per - 1, 0), 0)), vec(d),
                  _wspec(lay.b_rows, d), _resident((ngrp, B_GROUP_DIM, B_GROUP_DIM), (0, 0, 0)), vec(d),
                  _wspec(lay.b_rows, d), vec(d)],
        out_specs=[_rows(tm, d)] * 5,
        out_shape=[jax.ShapeDtypeStruct((t, d), F32), jax.ShapeDtypeStruct((t, d), BF16),
                   jax.ShapeDtypeStruct((t, d), BF16), jax.ShapeDtypeStruct((t, d), BF16),
                   jax.ShapeDtypeStruct((t, d), F32)],
        compiler_params=_params("parallel"),
    )(x, x, gpre, wg[0], wgrp, scale, wg[1], gpost)


def _b_bwd(dx1, m, x, pooled, gpre, wg, lay, j, wgrp, scale, gpost, after, tm, name):
    t, d = x.shape
    n = tm + HALO
    ngrp = len(B_WINDOWS)
    steps = t // tm

    def body(dx1_ref, dx1n_ref, m_ref, mn_ref, x_ref, pooled_ref, pooledn_ref, gpre_ref, win_ref, wgrp_ref, scale_ref,
             wout_ref, gpost_ref, after_ref,
             dx_ref, dm_ref, draw_ref, dp_ref, dgpost_ref, dgpre_ref, dscale_ref, dpool_ref):
        i = pl.program_id(0)

        @pl.when(i == 0)
        def _():
            for r in (dgpost_ref, dgpre_ref, dscale_ref):
                r[...] = jnp.zeros_like(r)

        keep = jnp.where(i < steps - 1, 1.0, 0.0)
        dy = dx1_ref[...]
        dye = jnp.concatenate([dy, dx1n_ref[...] * keep], axis=0)
        me = jnp.concatenate([m_ref[...], mn_ref[...]], axis=0)
        gpost_v = gpost_ref[...]
        r = lax.rsqrt(jnp.mean(me * me, axis=-1, keepdims=True) + EPS)
        mh = me * r
        dgpost_ref[...] += jnp.sum((dye * mh)[:tm], axis=0, keepdims=True)
        dmh = dye * gpost_v
        dme = (r * (dmh - mh * jnp.mean(dmh * mh, axis=-1, keepdims=True))).astype(BF16)
        dm_ref[...] = dme[:tm]
        dmixed = _nt(dme, wout_ref[...])
        pooled_e = jnp.concatenate([pooled_ref[...], pooledn_ref[...]], axis=0)
        scale_v = scale_ref[...]
        for g, win in enumerate(B_WINDOWS):
            cols = slice(g * B_GROUP_DIM, (g + 1) * B_GROUP_DIM)
            raw = _nn(pooled_e[:, cols], wgrp_ref[g])
            dscale_ref[:, cols] += jnp.sum((dmixed[:, cols] * raw)[:tm], axis=0, keepdims=True)
            draw = (dmixed[:, cols] * scale_v[:, cols]).astype(BF16)
            draw_ref[:, cols] = draw[:tm]
            dpool = _nt(draw, wgrp_ref[g])
            acc = dpool / _window_counts(i * tm, n, win)
            shift = 1
            while shift < win:
                acc = acc + pltpu.roll(acc, n - shift, 0)
                shift *= 2
            dpool_ref[:, cols] = (acc - dpool)[:tm]
        dp = dpool_ref[...].astype(BF16)
        dp_ref[...] = dp
        dh1 = _nt(dp, win_ref[...])
        dxp, dgpre = _rms_bwd(x_ref[...], gpre_ref[...], dh1)
        dgpre_ref[...] += dgpre
        dx_ref[...] = dy + dxp

    vec = lambda w: _resident((1, w), (0, 0))
    acc = lambda shape: pl.BlockSpec(shape, lambda i: (0,) * len(shape))
    per = tm // HALO
    nxt = lambda i: (jnp.minimum((i + 1) * per, t // HALO - 1), 0)
    return _call(
        body, name=name, grid=(steps,),
        in_specs=[_rows(tm, d), pl.BlockSpec((HALO, d), nxt), _rows(tm, d), pl.BlockSpec((HALO, d), nxt), _rows(tm, d),
                  _rows(tm, d), pl.BlockSpec((HALO, d), nxt), vec(d), _wspec(lay.b_rows, d),
                  _resident((ngrp, B_GROUP_DIM, B_GROUP_DIM), (0, 0, 0)), vec(d), _wspec(lay.b_rows, d),
                  vec(d), ANY],
        out_specs=[_rows(tm, d)] * 4 + [acc((1, d))] * 3,
        out_shape=[jax.ShapeDtypeStruct((t, d), F32), jax.ShapeDtypeStruct((t, d), BF16),
                   jax.ShapeDtypeStruct((t, d), BF16), jax.ShapeDtypeStruct((t, d), BF16)]
                  + [jax.ShapeDtypeStruct((1, d), F32)] * 3,
        scratch_shapes=[pltpu.VMEM((tm, d), F32)],
        compiler_params=_params("arbitrary"),
    )(dx1, dx1, m, m, x, pooled, pooled, gpre, wg[0], wgrp, scale, wg[1], gpost, after)


def _f_fwd(x1, gpre, wg, lay, l, gpost, tm, name, target=None):
    t, d = x1.shape
    hid = N_DEV * lay.ffn_rows
    head = target is not None

    def body(x_ref, gpre_ref, wgate_ref, wup_ref, wdown_ref, gpost_ref, *rest):
        x2_ref, h2_ref, abs_ref, f_ref = rest[-5:-1] if head else rest
        xv = x_ref[...]
        h2 = _rms_fwd(xv, gpre_ref[...]).astype(BF16)
        h2_ref[...] = h2
        a = _nt(h2, wgate_ref[...])
        b = _nt(h2, wup_ref[...])
        sig = jax.nn.sigmoid(a)
        silu = a * sig
        abs_ref[:, :hid] = (b * (sig + silu * (1.0 - sig))).astype(BF16)
        abs_ref[:, hid:2 * hid] = silu.astype(BF16)
        s = (silu * b).astype(BF16)
        abs_ref[:, 2 * hid:] = s
        f = _nn(s, wdown_ref[...])
        f_ref[...] = f
        x2 = xv + _rms_fwd(f, gpost_ref[...])
        if head:
            target_ref, loss_ref = rest[0], rest[-1]

            @pl.when(pl.program_id(0) == 0)
            def _():
                loss_ref[...] = jnp.zeros_like(loss_ref)

            diff = x2 - target_ref[...]
            x2_ref[...] = diff * (1.0 / d)
            sq = jnp.sum(jnp.sum(diff * diff, axis=0, keepdims=True), axis=1, keepdims=True)
            loss_ref[...] += sq * (0.5 / d)
        else:
            x2_ref[...] = x2

    vec = lambda w: _resident((1, w), (0, 0))
    return _call(
        body, name=name, grid=(t // tm,),
        in_specs=[_rows(tm, d), vec(d), _wspec(lay.ffn_rows, d), _wspec(lay.ffn_rows, d),
                  _wspec(lay.ffn_rows, d), vec(d)] + ([_rows(tm, d)] if head else []),
        out_specs=[_rows(tm, d), _rows(tm, d), _rows(tm, 3 * hid), _rows(tm, d)]
                  + ([pl.BlockSpec((8, 128), lambda i: (0, 0))] if head else []),
        out_shape=[jax.ShapeDtypeStruct((t, d), F32), jax.ShapeDtypeStruct((t, d), BF16),
                   jax.ShapeDtypeStruct((t, 3 * hid), BF16), jax.ShapeDtypeStruct((t, d), F32)]
                  + ([jax.ShapeDtypeStruct((8, 128), F32)] if head else []),
        compiler_params=_params("arbitrary" if head else "parallel"),
    )(x1, gpre, wg[0], wg[1], wg[2], gpost, *([target] if head else []))


def _f_bwd(dx2, f, x1, acts, gpre, wg, lay, l, gpost, after, tm, name):
    t, d = x1.shape
    hid = N_DEV * lay.ffn_rows

    def body(dx2_ref, f_ref, x_ref, ab_ref, gpre_ref, wgate_ref, wup_ref, wdown_ref, gpost_ref, after_ref,
             dx1_ref, df_ref, dab_ref, dgpost_ref, dgpre_ref):
        @pl.when(pl.program_id(0) == 0)
        def _():
            dgpost_ref[...] = jnp.zeros_like(dgpost_ref)
            dgpre_ref[...] = jnp.zeros_like(dgpre_ref)

        def put_df(rows, dx):
            df_ref[rows, :] = dx.astype(BF16)

        dgpost_ref[...] += _rms_bwd_slabs(lambda rows: f_ref[rows, :], lambda rows: dx2_ref[rows, :], gpost_ref[...],
                                          tm, tm, put_df)
        ds = _nt(df_ref[...], wdown_ref[...])
        dab_ref[:, :hid] = (ds * ab_ref[:, :hid].astype(F32)).astype(BF16)
        dab_ref[:, hid:] = (ds * ab_ref[:, hid:].astype(F32)).astype(BF16)
        dh2 = _nn(dab_ref[:, :hid], wgate_ref[...]) + _nn(dab_ref[:, hid:], wup_ref[...])

        def put_dx(rows, dx):
            dx1_ref[rows, :] = dx2_ref[rows, :] + dx

        dgpre_ref[...] += _rms_bwd_slabs(lambda rows: x_ref[rows, :], lambda rows: dh2[rows, :], gpre_ref[...],
                                         tm, tm, put_dx)

    vec = lambda w: _resident((1, w), (0, 0))
    acc = pl.BlockSpec((1, d), lambda i: (0, 0))
    return _call(
        body, name=name, grid=(t // tm,),
        in_specs=[_rows(tm, d), _rows(tm, d), _rows(tm, d), _rows(tm, 2 * hid), vec(d),
                  _wspec(lay.ffn_rows, d), _wspec(lay.ffn_rows, d),
                  _wspec(lay.ffn_rows, d), vec(d), ANY],
        out_specs=[_rows(tm, d), _rows(tm, d), _rows(tm, 2 * hid), acc, acc],
        out_shape=[jax.ShapeDtypeStruct((t, d), F32), jax.ShapeDtypeStruct((t, d), BF16),
                   jax.ShapeDtypeStruct((t, 3 * hid), BF16),
                   jax.ShapeDtypeStruct((1, d), F32), jax.ShapeDtypeStruct((1, d), F32)],
        input_output_aliases={3: 2},
        compiler_params=_params("arbitrary"),
    )(dx2, f, x1, acts, gpre, wg[0], wg[1], wg[2], gpost, after)


def _grad_into(gbuf, lhs, rhs, off, rows, name, after=None):
    t, m = lhs.shape
    d = rhs.shape[1]
    assert m == N_DEV * rows and off % rows == 0
    per_tile = {512: 2, 256: 4, 128: 8}[rows]
    tm = per_tile * rows
    assert tm % 128 == 0 and rows % 16 == 0
    tk = 2048 if t % 2048 == 0 else 256
    ksteps = t // tk
    fresh = isinstance(gbuf, int)
    shape = (N_DEV, gbuf, d) if fresh else gbuf.shape
    extra = ([] if fresh else [gbuf]) + ([] if after is None else [after])

    def body(l_ref, r_ref, *rest):
        o_ref, acc_ref = rest[-2:]
        k = pl.program_id(1)

        @pl.when(k == 0)
        def _():
            acc_ref[...] = jnp.zeros_like(acc_ref)

        acc_ref[...] += _tn(l_ref[...], r_ref[...])

        @pl.when(k == ksteps - 1)
        def _():
            o_ref[...] = acc_ref[...].reshape(per_tile, rows, d).astype(BF16)

    return _call(
        body, name=name, grid=(N_DEV // per_tile, ksteps),
        in_specs=[pl.BlockSpec((tk, tm), lambda i, k: (k, i)), pl.BlockSpec((tk, d), lambda i, k: (k, 0))]
                 + [ANY] * len(extra),
        out_specs=pl.BlockSpec((per_tile, rows, d), lambda i, k: (i, off // rows, 0)),
        out_shape=jax.ShapeDtypeStruct(shape, BF16),
        scratch_shapes=[pltpu.VMEM((tm, d), F32)],
        input_output_aliases={} if fresh else {2: 0},
        compiler_params=_params("parallel", "arbitrary"),
    )(lhs, rhs, *extra)


def _grad_ffn(acts, h2, df, rows, name):
    t, d = h2.shape
    per_tile = 4
    tm = per_tile * rows
    tiles = N_DEV // per_tile
    assert acts.shape[1] == 3 * N_DEV * rows and tm % 128 == 0 and rows % 16 == 0
    tk = 2048 if t % 2048 == 0 else 256
    ksteps = t // tk

    def body(l_ref, h2_ref, df_ref, o_ref, acc_ref):
        i, k = pl.program_id(0), pl.program_id(1)

        @pl.when(k == 0)
        def _():
            acc_ref[...] = jnp.zeros_like(acc_ref)

        @pl.when(i < 2 * tiles)
        def _():
            acc_ref[...] += _tn(l_ref[...], h2_ref[...])

        @pl.when(i >= 2 * tiles)
        def _():
            acc_ref[...] += _tn(l_ref[...], df_ref[...])

        @pl.when(k == ksteps - 1)
        def _():
            o_ref[...] = acc_ref[...].reshape(per_tile, rows, d).astype(BF16)

    return _call(
        body, name=name, grid=(3 * tiles, ksteps),
        in_specs=[pl.BlockSpec((tk, tm), lambda i, k: (k, i)),
                  pl.BlockSpec((tk, d), lambda i, k: (jnp.where(i < 2 * tiles, k, ksteps - 1), 0)),
                  pl.BlockSpec((tk, d), lambda i, k: (jnp.where(i >= 2 * tiles, k, 0), 0))],
        out_specs=pl.BlockSpec((per_tile, rows, d), lambda i, k: (i % tiles, i // tiles, 0)),
        out_shape=jax.ShapeDtypeStruct((N_DEV, 3 * rows, d), BF16),
        scratch_shapes=[pltpu.VMEM((tm, d), F32)],
        compiler_params=_params("arbitrary", "arbitrary"),
    )(acts, h2, df)


def _grad_grouped(pooled, draw, name):
    t, d = pooled.shape
    ngrp = len(B_WINDOWS)
    tk = 1024 if t % 1024 == 0 else 256

    def body(p_ref, q_ref, o_ref):
        @pl.when(pl.program_id(0) == 0)
        def _():
            o_ref[...] = jnp.zeros_like(o_ref)

        for g in range(ngrp):
            cols = slice(g * B_GROUP_DIM, (g + 1) * B_GROUP_DIM)
            o_ref[g] += _tn(p_ref[:, cols], q_ref[:, cols])

    return _call(
        body, name=name, grid=(t // tk,),
        in_specs=[_rows(tk, d), _rows(tk, d)],
        out_specs=pl.BlockSpec((ngrp, B_GROUP_DIM, B_GROUP_DIM), lambda i: (0, 0, 0)),
        out_shape=jax.ShapeDtypeStruct((ngrp, B_GROUP_DIM, B_GROUP_DIM), F32),
        compiler_params=_params("arbitrary"),
    )(pooled, draw)


def _peers():
    x, y, c = lax.axis_index("x"), lax.axis_index("y"), lax.axis_index("c")
    flip = lambda v, f: 1 - v if f else v
    peers = []
    for r in range(1, N_DEV):
        px, py, pc = flip(x, r & 4), flip(y, r & 2), flip(c, r & 1)
        peers.append(((px, py, pc), 4 * px + 2 * py + pc))
    return 4 * x + 2 * y + c, peers


HBM = pl.BlockSpec(memory_space=pltpu.HBM)
SEM = pl.BlockSpec(memory_space=pltpu.SEMAPHORE)
EFFECT = pltpu.SideEffectType.DATAFLOW_SIDE_EFFECTING


def _peer_copies(scatter, srcs, lands, send_sems, recv_sems):
    me, peers = _peers()
    copies = []
    for a in range(len(srcs)):
        rows = srcs[a].shape[0]
        block = lambda k: lands[a].at[pl.ds(pl.multiple_of(k * rows, 8), rows)]
        for r, (peer, pidx) in enumerate(peers):
            src = srcs[a].at[pidx] if scatter else srcs[a]
            mine = lands[a].at[r] if scatter else block(pidx)
            theirs = lands[a].at[r] if scatter else block(me)
            send = pltpu.make_async_remote_copy(src_ref=src, dst_ref=theirs, send_sem=send_sems[a].at[r],
                                                recv_sem=recv_sems[a].at[r], device_id=peer, device_id_type=MESH)
            recv = pltpu.make_async_remote_copy(src_ref=src, dst_ref=mine, send_sem=send_sems[a].at[r],
                                                recv_sem=recv_sems[a].at[r], device_id=peer, device_id_type=MESH)
            copies.append((send, recv))
    return copies


def _own_copies(srcs, lands, send_sems):
    me, _ = _peers()
    copies = []
    for a in range(len(srcs)):
        rows = srcs[a].shape[0]
        copies.append(pltpu.make_async_copy(srcs[a], lands[a].at[pl.ds(pl.multiple_of(me * rows, 8), rows)],
                                            send_sems[a].at[N_DEV - 1]))
    return copies


def _exchange_start(scatter, srcs, lands, after, name):
    n = len(srcs)

    def body(*refs):
        src_refs, land_refs = refs[:n], refs[n:2 * n]
        outs = refs[2 * n + 1:]
        send_sems, recv_sems, token = outs[:n], outs[n:2 * n], outs[-1]
        for send, _ in _peer_copies(scatter, src_refs, land_refs, send_sems, recv_sems):
            send.start()
        if not scatter:
            for own in _own_copies(src_refs, land_refs, send_sems):
                own.start()
        token[...] = jnp.zeros_like(token)

    hbm = lambda a: pltpu.with_memory_space_constraint(a, pltpu.HBM)
    res = _call(
        body, name=name,
        in_specs=[HBM] * (2 * n) + [ANY],
        out_specs=[SEM] * (2 * n) + [HBM] * (2 * n) + [pl.BlockSpec(memory_space=pltpu.VMEM)],
        out_shape=[pltpu.SemaphoreType.DMA((N_DEV,))] * (2 * n)
                  + [pltpu.HBM(a.shape, a.dtype) for a in list(srcs) + list(lands)]
                  + [jax.ShapeDtypeStruct((8, 128), F32)],
        input_output_aliases={i: 2 * n + i for i in range(2 * n)},
        compiler_params=pltpu.CompilerParams(has_side_effects=EFFECT),
    )(*[hbm(a) for a in srcs], *[hbm(a) for a in lands], after)
    return res[:n], res[n:2 * n], res[2 * n:3 * n], res[3 * n:4 * n], res[-1]


def _chip_peers():
    x, y, c = lax.axis_index("x"), lax.axis_index("y"), lax.axis_index("c")
    far = []
    for px, py in ((1 - x, y), (x, 1 - y), (1 - x, 1 - y)):
        far.append(((px, py, c), 4 * px + 2 * py + c, 4 * px + 2 * py + 1 - c))
    return 4 * x + 2 * y + c, ((x, y, 1 - c), 4 * x + 2 * y + 1 - c), far


def _block(land, rows, k):
    return land.at[pl.ds(pl.multiple_of(k * rows, 8), rows)]


def _gather2_first(srcs, lands, after, name):
    n = len(srcs)

    def body(*refs):
        src_refs, land_refs = refs[:n], refs[n:2 * n]
        outs = refs[2 * n + 1:]
        send, recv_sib, recv_far, token = outs[:n], outs[n:2 * n], outs[2 * n:3 * n], outs[-1]
        me, (sib, _), far = _chip_peers()
        for a in range(n):
            rows = src_refs[a].shape[0]
            mine = _block(land_refs[a], rows, me)
            pltpu.make_async_copy(src_refs[a], mine, send[a].at[4]).start()
            pltpu.make_async_remote_copy(src_ref=src_refs[a], dst_ref=mine, send_sem=send[a].at[0],
                                         recv_sem=recv_sib[a].at[0], device_id=sib, device_id_type=MESH).start()
            for j, (peer, _, _) in enumerate(far):
                pltpu.make_async_remote_copy(src_ref=src_refs[a], dst_ref=mine, send_sem=send[a].at[1 + j],
                                             recv_sem=recv_far[a].at[j], device_id=peer, device_id_type=MESH).start()
        token[...] = jnp.zeros_like(token)

    hbm = lambda a: pltpu.with_memory_space_constraint(a, pltpu.HBM)
    res = _call(
        body, name=name,
        in_specs=[HBM] * (2 * n) + [ANY],
        out_specs=[SEM] * (3 * n) + [HBM] * (2 * n) + [pl.BlockSpec(memory_space=pltpu.VMEM)],
        out_shape=[pltpu.SemaphoreType.DMA((5,))] * n + [pltpu.SemaphoreType.DMA((1,))] * n
                  + [pltpu.SemaphoreType.DMA((3,))] * n
                  + [pltpu.HBM(a.shape, a.dtype) for a in list(srcs) + list(lands)]
                  + [jax.ShapeDtypeStruct((8, 128), F32)],
        input_output_aliases={i: 3 * n + i for i in range(2 * n)},
        compiler_params=pltpu.CompilerParams(has_side_effects=EFFECT),
    )(*[hbm(a) for a in srcs], *[hbm(a) for a in lands], after)
    return res[:n], res[n:2 * n], res[2 * n:3 * n], res[3 * n:4 * n], res[4 * n:5 * n], res[-1]


def _gather2_forward(recv_far, srcs, lands, after, name):
    n = len(lands)
    after = list(after) if isinstance(after, (list, tuple)) else [after]

    def body(*refs):
        src_refs, land_refs, far_sems = refs[:n], refs[n:2 * n], refs[2 * n:3 * n]
        outs = refs[3 * n + len(after):]
        send, recv, token = outs[:n], outs[n:2 * n], outs[-1]
        _, (sib, _), far = _chip_peers()
        for a in range(n):
            rows = src_refs[a].shape[0]
            for j, (peer, pidx, _) in enumerate(far):
                got = _block(land_refs[a], rows, pidx)
                pltpu.make_async_remote_copy(src_ref=src_refs[a], dst_ref=got, send_sem=send[a].at[j],
                                             recv_sem=far_sems[a].at[j], device_id=peer,
                                             device_id_type=MESH).wait_recv()
                pltpu.make_async_remote_copy(src_ref=got, dst_ref=got, send_sem=send[a].at[j], recv_sem=recv[a].at[j],
                                             device_id=sib, device_id_type=MESH).start()
        token[...] = jnp.zeros_like(token)

    res = _call(
        body, name=name,
        in_specs=[HBM] * (2 * n) + [SEM] * n + [ANY] * len(after),
        out_specs=[SEM] * (2 * n) + [HBM] * n + [pl.BlockSpec(memory_space=pltpu.VMEM)],
        out_shape=[pltpu.SemaphoreType.DMA((3,))] * (2 * n) + [pltpu.HBM(a.shape, a.dtype) for a in lands]
                  + [jax.ShapeDtypeStruct((8, 128), F32)],
        input_output_aliases={n + i: 2 * n + i for i in range(n)},
        compiler_params=pltpu.CompilerParams(has_side_effects=EFFECT),
    )(*srcs, *lands, *recv_far, *after)
    return res[:n], res[n:2 * n], res[2 * n:3 * n], res[-1]


def _gather2_wait(send, recv_sib, fwd_send, fwd_recv, srcs, lands, after, name):
    n = len(lands)

    def body(*refs):
        src_refs, land_refs = refs[:n], refs[n:2 * n]
        s_refs, rs_refs, fs_refs, fr_refs = (refs[(2 + q) * n:(3 + q) * n] for q in range(4))
        me, (sib, sib_idx), far = _chip_peers()
        for a in range(n):
            rows = src_refs[a].shape[0]
            mine = _block(land_refs[a], rows, me)
            pltpu.make_async_copy(src_refs[a], mine, s_refs[a].at[4]).wait()
            to_sib = pltpu.make_async_remote_copy(src_ref=src_refs[a], dst_ref=_block(land_refs[a], rows, sib_idx),
                                                  send_sem=s_refs[a].at[0], recv_sem=rs_refs[a].at[0], device_id=sib,
                                                  device_id_type=MESH)
            to_sib.wait_send()
            to_sib.wait_recv()
            for j, (peer, pidx, pair_idx) in enumerate(far):
                pltpu.make_async_remote_copy(src_ref=src_refs[a], dst_ref=mine, send_sem=s_refs[a].at[1 + j],
                                             recv_sem=fr_refs[a].at[j], device_id=peer,
                                             device_id_type=MESH).wait_send()
                fwd = pltpu.make_async_remote_copy(src_ref=_block(land_refs[a], rows, pidx),
                                                   dst_ref=_block(land_refs[a], rows, pair_idx),
                                                   send_sem=fs_refs[a].at[j], recv_sem=fr_refs[a].at[j], device_id=sib,
                                                   device_id_type=MESH)
                fwd.wait_send()
                fwd.wait_recv()

    res = _call(
        body, name=name,
        in_specs=[HBM] * (2 * n) + [SEM] * (4 * n) + [ANY],
        out_specs=[HBM] * (2 * n),
        out_shape=[pltpu.HBM(a.shape, a.dtype) for a in list(srcs) + list(lands)],
        input_output_aliases={i: i for i in range(2 * n)},
        compiler_params=pltpu.CompilerParams(has_side_effects=EFFECT),
    )(*srcs, *lands, *send, *recv_sib, *fwd_send, *fwd_recv, after)
    return res[n:]


def _exchange_wait(scatter, send_sems, recv_sems, srcs, lands, after, name):
    n = len(srcs)
    after = list(after) if isinstance(after, (list, tuple)) else [after]

    def body(*refs):
        src_refs, land_refs = refs[:n], refs[n:2 * n]
        send_refs, recv_refs = refs[2 * n:3 * n], refs[3 * n:4 * n]
        for send, recv in _peer_copies(scatter, src_refs, land_refs, send_refs, recv_refs):
            send.wait_send()
            recv.wait_recv()
        if not scatter:
            for own in _own_copies(src_refs, land_refs, send_refs):
                own.wait()

    res = _call(
        body, name=name,
        in_specs=[HBM] * (2 * n) + [SEM] * (2 * n) + [ANY] * len(after),
        out_specs=[HBM] * (2 * n),
        out_shape=[pltpu.HBM(a.shape, a.dtype) for a in list(srcs) + list(lands)],
        input_output_aliases={i: i for i in range(2 * n)},
        compiler_params=pltpu.CompilerParams(has_side_effects=EFFECT),
    )(*srcs, *lands, *send_sems, *recv_sems, *after)
    return res[:n], res[n:]


def _row_tile(rows):
    if rows <= 512:
        return rows
    return max([tr for tr in range(16, 513, 16) if rows % tr == 0] or [rows])


def _sum_parts(own, got, me, name):
    _, rows, w = own.shape
    tr = _row_tile(rows)

    def body(me_ref, a_ref, b_ref, o_ref):
        s = a_ref[...].astype(F32)
        for j in range(N_DEV - 1):
            s = s + b_ref[j].astype(F32)
        o_ref[...] = s

    return _call(
        body, name=name,
        grid_spec=pltpu.PrefetchScalarGridSpec(
            num_scalar_prefetch=1, grid=(rows // tr,),
            in_specs=[pl.BlockSpec((None, tr, w), lambda i, me_ref: (me_ref[0], i, 0)),
                      pl.BlockSpec((N_DEV - 1, tr, w), lambda i, me_ref: (0, i, 0))],
            out_specs=pl.BlockSpec((tr, w), lambda i, me_ref: (i, 0))),
        out_shape=jax.ShapeDtypeStruct((rows, w), F32),
        compiler_params=_params("parallel"),
    )(me, own, got)


def _sum_devices(stacked, name):
    k, rows, w = stacked.shape
    tr = _row_tile(rows)

    def body(a_ref, o_ref):
        s = a_ref[0]
        for j in range(1, k):
            s = s + a_ref[j]
        o_ref[...] = s

    return _call(
        body, name=name, grid=(rows // tr,),
        in_specs=[pl.BlockSpec((k, tr, w), lambda i: (0, i, 0))],
        out_specs=pl.BlockSpec((tr, w), lambda i: (i, 0)),
        out_shape=jax.ShapeDtypeStruct((rows, w), F32),
        compiler_params=_params("parallel"),
    )(stacked)


def _adamw_math(w, g, m, v):
    nm = ADAM_B1 * m + (1.0 - ADAM_B1) * g
    nv = ADAM_B2 * v + (1.0 - ADAM_B2) * (g * g)
    m_hat = nm / (1.0 - ADAM_B1 ** ADAM_STEP)
    v_hat = nv / (1.0 - ADAM_B2 ** ADAM_STEP)
    return -ADAM_LR * (m_hat / (jnp.sqrt(v_hat) + ADAM_EPS) + ADAM_WD * w), nm, nv


def _sum_adamw(owns, gots, me, off, rows, w, m, v, name):
    nl = len(owns)
    d = w.shape[-1]
    assert off % rows == 0 and w.shape == (nl, rows, d)

    def body(me_ref, *refs):
        own_refs, got_refs = refs[:nl], refs[nl:2 * nl]
        w_ref, m_ref, v_ref, g_ref, d_ref, nm_ref, nv_ref = refs[2 * nl:]
        for l in range(nl):
            @pl.when(pl.program_id(0) == l)
            def _(l=l):
                gv = own_refs[l][...].astype(F32)
                for r in range(N_DEV - 1):
                    gv = gv + got_refs[l][r].astype(F32)
                g_ref[...] = gv
                d_ref[...], nm_ref[...], nv_ref[...] = _adamw_math(w_ref[...], gv, m_ref[...], v_ref[...])

    layer = pl.BlockSpec((None, rows, d), lambda i, me_ref: (i, 0, 0))
    own = pl.BlockSpec((None, rows, d), lambda i, me_ref: (me_ref[0], off // rows, 0), pipeline_mode=pl.Buffered(1))
    got = pl.BlockSpec((N_DEV - 1, rows, d), lambda i, me_ref: (0, off // rows, 0), pipeline_mode=pl.Buffered(1))
    return _call(
        body, name=name,
        grid_spec=pltpu.PrefetchScalarGridSpec(
            num_scalar_prefetch=1, grid=(nl,),
            in_specs=[own] * nl + [got] * nl + [layer] * 3, out_specs=[layer] * 4),
        out_shape=[jax.ShapeDtypeStruct((nl, rows, d), F32)] * 4,
        compiler_params=_params("arbitrary"),
    )(me, *owns, *gots, w, m, v)


def _adamw(w, g, m, v, name):
    rows, cols = w.shape
    tr = _row_tile(rows)

    def body(w_ref, g_ref, m_ref, v_ref, d_ref, nm_ref, nv_ref):
        d_ref[...], nm_ref[...], nv_ref[...] = _adamw_math(w_ref[...], g_ref[...], m_ref[...], v_ref[...])

    spec = pl.BlockSpec((tr, cols), lambda i: (i, 0))
    return _call(
        body, name=name, grid=(rows // tr,),
        in_specs=[spec] * 4, out_specs=[spec] * 3,
        out_shape=[jax.ShapeDtypeStruct((rows, cols), F32)] * 3,
        compiler_params=_params("parallel"),
    )(w, g, m, v)


SMALL = ("a_ln_g", "a_ln_b", "a_w_s", "a_b_s", "mix_pre_g", "mix_post_g", "ffn_pre_g", "ffn_post_g")


def _pack_small(parts, d, last_row=None):
    rows = [parts[k].reshape(-1, d) for k in SMALL] + ([] if last_row is None else [last_row])
    flat = jnp.concatenate(rows, axis=0)
    return jnp.pad(flat, ((0, -flat.shape[0] % 8), (0, 0)))


def _unpack_small(flat, like):
    out, r = {}, 0
    for k in SMALL:
        n = like[k].size // flat.shape[1]
        out[k] = flat[r:r + n].reshape(like[k].shape)
        r += n
    return out


def kernel(x, a_w_in, a_ln_g, a_ln_b, a_w_s, a_b_s, a_w_out, b_w_in, b_w_grp, b_scale, b_w_out, mix_pre_g, mix_post_g, ffn_pre_g, ffn_post_g, ffn_w_gate, ffn_w_up, ffn_w_down, loss_target, m_a_w_in, m_a_ln_g, m_a_ln_b, m_a_w_s, m_a_b_s, m_a_w_out, m_b_w_in, m_b_w_grp, m_b_scale, m_b_w_out, m_mix_pre_g, m_mix_post_g, m_ffn_pre_g, m_ffn_post_g, m_ffn_w_gate, m_ffn_w_up, m_ffn_w_down, v_a_w_in, v_a_ln_g, v_a_ln_b, v_a_w_s, v_a_b_s, v_a_w_out, v_b_w_in, v_b_w_grp, v_b_scale, v_b_w_out, v_mix_pre_g, v_mix_post_g, v_ffn_pre_g, v_ffn_post_g, v_ffn_w_gate, v_ffn_w_up, v_ffn_w_down):
    args = dict(locals())
    names = ("a_w_in", "a_ln_g", "a_ln_b", "a_w_s", "a_b_s", "a_w_out", "b_w_in", "b_w_grp", "b_scale", "b_w_out",
             "mix_pre_g", "mix_post_g", "ffn_pre_g", "ffn_post_g", "ffn_w_gate", "ffn_w_up", "ffn_w_down")
    w = {k: args[k] for k in names}
    mom = {k: args["m_" + k] for k in names}
    var = {k: args["v_" + k] for k in names}

    t, d = x.shape[1], x.shape[2]
    ffn_local = ffn_w_gate.shape[2]
    lay = _Layout(d, ffn_local)
    me = 4 * lax.axis_index("x") + 2 * lax.axis_index("y") + lax.axis_index("c")
    me1 = jnp.reshape(me, (1,)).astype(jnp.int32)

    def landing(block):
        return lax.empty((N_DEV * block.shape[0],) + block.shape[1:], block.dtype)

    def shards(i, mixer, zero):
        j = i // 2
        if not mixer:
            parts = [ffn_w_gate[i].T, ffn_w_up[i].T, ffn_w_down[i]]
        elif i % 2 == 0:
            parts = [a_w_in[j].T, a_w_out[j]]
        else:
            parts = [b_w_in[j], b_w_out[j]]
        return [(p + zero).astype(BF16) for p in parts]

    nsub = 2 * DEPTH
    wg = [None] * nsub
    first = shards(0, True, 0.0)
    f_send, f_sib, f_far, first, f_zones, f_token = _gather2_first(
        first, [landing(b) for b in first], jnp.zeros((8, 128), F32), "gather_first_start")
    zero = f_token[0, 0]
    ngrp = len(B_WINDOWS)
    grp_local = b_w_grp.shape[2]
    sdev = b_scale.shape[1]
    side_rows = 2 * ngrp * grp_local
    side = jnp.concatenate(
        [b_w_grp.reshape(side_rows, B_GROUP_DIM),
         jnp.pad(b_scale, ((0, 6), (0, B_GROUP_DIM - sdev)))], axis=0) + zero
    later, where = [side], [slice(0, 1)]
    for k in range(1, nsub):
        new = shards(k // 2, k % 2 == 0, zero)
        where.append(slice(len(later), len(later) + len(new)))
        later += new
    send_sems, recv_sems, later, zones, token = _exchange_start(
        False, later, [landing(b) for b in later], f_token, "gather_start")
    turned = ("ffn_w_gate", "ffn_w_up")
    turn = lambda a: jnp.swapaxes(a, 1, 2)
    state = {k: tuple(turn(a[k]) for a in (w, mom, var)) for k in turned}
    state["small"] = tuple(_pack_small(a, d) for a in (w, mom, var))
    ready = [a for group in state.values() for a in group]
    fwd_send, fwd_recv, f_zones, fwd_token = _gather2_forward(f_far, first, f_zones, [token] + ready,
                                                              "gather_first_forward")
    wg[0] = _gather2_wait(f_send, f_sib, fwd_send, fwd_recv, first, f_zones, fwd_token, "gather_first_wait")

    def gathered(k, after):
        s = where[k]
        _, got = _exchange_wait(False, send_sems[s], recv_sems[s], later[s], zones[s], after, f"gather_wait_{k}")
        return got

    row = lambda a: a.reshape(1, -1)
    bst = jnp.transpose(a_b_s, (0, 2, 1))

    tm = 256 if t % 256 == 0 else CHUNK
    tm_abwd = tm
    tm_b = 512 if t % 512 == 0 else tm
    tm_f = tm

    saved = []
    h = x[0]
    wgrp_full = scale_full = None
    for i in range(DEPTH):
        j = i // 2
        gpre = row(mix_pre_g[i])
        if i > 0:
            wg[2 * i] = gathered(2 * i, h)
        if i % 2 == 0:
            x1, h1, gp, u, vh, rs, gated, m = _a_fwd(h, gpre, wg[2 * i], lay, j, row(a_ln_g[j]), row(a_ln_b[j]),
                                                     a_w_s[j], bst[j], row(mix_post_g[i]), tm, f"a_fwd_{j}")
            mix = dict(h1=h1, gp=gp, u=u, vh=vh, rs=rs, gated=gated, m=m)
        else:
            if wgrp_full is None:
                side_g = gathered(0, h)[0].reshape(N_DEV, side_rows + 8, B_GROUP_DIM)
                wgrp_full = (side_g[:, :side_rows].reshape(N_DEV, 2, ngrp, grp_local, B_GROUP_DIM)
                             .transpose(1, 2, 0, 3, 4).reshape(2, ngrp, B_GROUP_DIM, B_GROUP_DIM).astype(BF16))
                scale_full = (side_g[:, side_rows:side_rows + 2, :sdev].transpose(1, 0, 2)
                              .reshape(2, 1, N_DEV * sdev))
            x1, h1, pooled, mixed, m = _b_fwd(h, gpre, wg[2 * i], lay, j, wgrp_full[j], scale_full[j],
                                              row(mix_post_g[i]), tm_b, f"b_fwd_{j}")
            mix = dict(h1=h1, pooled=pooled, mixed=mixed, m=m)
        wg[2 * i + 1] = gathered(2 * i + 1, x1)
        x2, h2, acts, f, *loss_acc = _f_fwd(x1, row(ffn_pre_g[i]), wg[2 * i + 1], lay, i, row(ffn_post_g[i]), tm_f,
                                               f"f_fwd_{i}", loss_target[0] if i == DEPTH - 1 else None)
        saved.append(dict(x=h, x1=x1, mix=mix, h2=h2, acts=acts, f=f))
        h = x2
    dy, (loss_acc,) = h, loss_acc

    small_g = {k: [None] * w[k].shape[0] for k in SMALL}
    dgrp, dscale = [None, None], [None, None]
    pending = [None] * nsub
    token = jnp.zeros((8, 128), F32)

    def scatter(k, gbuf):
        got = pltpu.with_memory_space_constraint(lax.empty((N_DEV - 1,) + gbuf.shape[1:], gbuf.dtype), pltpu.HBM)
        ss, rs, src, zone, tok = _exchange_start(True, [gbuf], [got], token, f"scatter_start_{k}")
        pending[k] = (ss, rs, src, zone)
        return tok

    def small_exchanges():
        side_grad = jnp.concatenate(
            [jnp.stack(dgrp).reshape(2, ngrp, N_DEV, grp_local, B_GROUP_DIM).transpose(2, 0, 1, 3, 4)
             .reshape(N_DEV, side_rows, B_GROUP_DIM),
             jnp.pad(jnp.stack(dscale).reshape(2, N_DEV, sdev).transpose(1, 0, 2),
                     ((0, 0), (0, 6), (0, B_GROUP_DIM - sdev)))], axis=1)
        small_part = _pack_small({k: jnp.stack(small_g[k]) for k in SMALL}, d,
                                 jnp.broadcast_to(loss_acc[:1, :1], (1, d)))
        got = pltpu.with_memory_space_constraint(lax.empty((N_DEV - 1,) + side_grad.shape[1:], F32), pltpu.HBM)
        side_x = _exchange_start(True, [side_grad], [got], token, "side_scatter_start")
        small_x = _exchange_start(False, [small_part], [landing(small_part)], side_x[4], "small_gather_start")
        return side_x[:4], small_x[:4], small_x[4]

    for i in reversed(range(DEPTH)):
        sv = saved[i]
        j = i // 2
        wf, wm = wg[2 * i + 1], wg[2 * i]
        dx1, df, dacts, dgpost, dgpre = _f_bwd(dy, sv["f"], sv["x1"], sv["acts"], row(ffn_pre_g[i]), wf, lay, i,
                                               row(ffn_post_g[i]), token, tm_f, f"f_bwd_{i}")
        small_g["ffn_post_g"][i], small_g["ffn_pre_g"][i] = dgpost[0], dgpre[0]
        gbuf = _grad_ffn(dacts, sv["h2"], df, lay.ffn_rows, f"g_ffn_{i}")
        token = scatter(2 * i + 1, gbuf)
        mix = sv["mix"]
        gpost = row(mix_post_g[i])
        if i % 2 == 0:
            dx, dm, dz, dgpost, dgpre, dlng, dlnb, dws, dbt = _a_bwd(
                dx1, mix["m"], sv["x"], mix["gp"], mix["u"], mix["vh"], mix["rs"], row(mix_pre_g[i]), wm, lay, j,
                row(a_ln_g[j]), row(a_ln_b[j]), a_w_s[j], bst[j], gpost, token, tm_abwd, f"a_bwd_{j}")
            small_g["a_ln_g"][j], small_g["a_ln_b"][j] = dlng[0], dlnb[0]
            small_g["a_w_s"][j], small_g["a_b_s"][j] = dws, dbt[:, :A_GROUPS].T
            small_g["mix_post_g"][i], small_g["mix_pre_g"][i] = dgpost[0], dgpre[0]
            order = None
            if i == 0:
                side_x, small_x, order = small_exchanges()
            gbuf = _grad_into(lay.a_total, dz, mix["h1"], lay.a_in[j], lay.a_in_rows, f"g_a_in_{j}", after=order)
            gbuf = _grad_into(gbuf, mix["gated"], dm, lay.a_out[j], lay.a_out_rows, f"g_a_out_{j}")
        else:
            dx, dm, draw, dp, dgpost, dgpre, dsc = _b_bwd(
                dx1, mix["m"], sv["x"], mix["pooled"], row(mix_pre_g[i]), wm, lay, j, wgrp_full[j], scale_full[j],
                gpost, token, tm_b, f"b_bwd_{j}")
            dscale[j] = dsc[0]
            dgrp[j] = _grad_grouped(mix["pooled"], draw, f"g_b_grp_{j}")
            gbuf = _grad_into(lay.b_total, mix["h1"], dp, lay.b_in[j], lay.b_rows, f"g_b_in_{j}")
            gbuf = _grad_into(gbuf, mix["mixed"], dm, lay.b_out[j], lay.b_rows, f"g_b_out_{j}")
            small_g["mix_post_g"][i], small_g["mix_pre_g"][i] = dgpost[0], dgpre[0]
        token = scatter(2 * i, gbuf)
        dy = dx
    grad_x = dy[None]

    g_sub = [None] * nsub

    parts = [None] * nsub

    def arrived(k, after):
        ss, rs, src, zone = pending[k]
        (own,), (got,) = _exchange_wait(True, ss, rs, src, zone, after, f"scatter_wait_{k}")
        parts[k] = (own, got)
        if k % 2 == 0:
            g_sub[k] = _sum_parts(own, got, me1, f"sum_grads_{k}")

    def rows_of(k, off, n):
        return g_sub[k][off:off + n]

    grads, delta, new_m, new_v = {}, {}, {}, {}

    def update(k):
        back = turn if k in turned else (lambda a: a)
        wk, mk, vk = state[k] if k in turned else (w[k], mom[k], var[k])
        shape = wk.shape
        two = lambda a: a.reshape(-1, shape[-1])
        dl, nm, nv = _adamw(two(wk), two(grads[k]), two(mk), two(vk), f"adamw_{k}")
        delta[k], new_m[k], new_v[k] = (back(a.reshape(shape)) for a in (dl, nm, nv))
        grads[k] = back(grads[k])

    for k in range(1, nsub):
        arrived(k, token)
    ffn_parts = [parts[2 * l + 1] for l in range(DEPTH)]
    for k, off in (("ffn_w_gate", lay.gate[0]), ("ffn_w_up", lay.up[0]), ("ffn_w_down", lay.down[0])):
        back = turn if k in turned else (lambda a: a)
        wk, mk, vk = state[k] if k in turned else (w[k], mom[k], var[k])
        out = _sum_adamw([p[0] for p in ffn_parts], [p[1] for p in ffn_parts], me1, off, ffn_local, wk, mk, vk,
                         f"update_{k}")
        grads[k], delta[k], new_m[k], new_v[k] = (back(a) for a in out)
    grads["b_w_in"] = jnp.stack([rows_of(4 * j + 2, lay.b_in[j], lay.b_rows) for j in range(2)])
    grads["b_w_out"] = jnp.stack([rows_of(4 * j + 2, lay.b_out[j], lay.b_rows) for j in range(2)])
    update("b_w_in")
    update("b_w_out")
    early = ("ffn_w_gate", "ffn_w_up", "ffn_w_down", "b_w_in", "b_w_out")

    (side_own,), (side_got,) = _exchange_wait(True, *side_x, [delta[k] for k in early], "side_scatter_wait")
    g_side = _sum_parts(side_own, side_got, me1, "sum_side")
    grads["b_w_grp"] = g_side[:side_rows].reshape(b_w_grp.shape)
    grads["b_scale"] = g_side[side_rows:side_rows + 2, :sdev]
    update("b_w_grp")
    update("b_scale")
    _, (small_all,) = _exchange_wait(False, *small_x, [delta["b_w_grp"], delta["b_scale"]], "small_gather_wait")
    small_sum = _sum_devices(small_all.reshape(N_DEV, -1, d), "sum_small")
    g_small = _unpack_small(small_sum, w)
    loss = small_sum[sum(w[k].size for k in SMALL) // d, 0]
    grads.update(g_small)
    dl, nm, nv = _adamw(state["small"][0], _pack_small(g_small, d), state["small"][1], state["small"][2],
                        "adamw_small")
    delta.update(_unpack_small(dl, w))
    new_m.update(_unpack_small(nm, w))
    new_v.update(_unpack_small(nv, w))

    arrived(0, dl)
    grads["a_w_in"] = jnp.stack([rows_of(4 * j, lay.a_in[j], lay.a_in_rows).T for j in range(2)])
    grads["a_w_out"] = jnp.stack([rows_of(4 * j, lay.a_out[j], lay.a_out_rows) for j in range(2)])
    update("a_w_in")
    update("a_w_out")

    return (loss, grad_x, *[grads[k] for k in names], *[delta[k] for k in names], *[new_m[k] for k in names],
            *[new_v[k] for k in names])
```

```python
import math

import jax
import jax.numpy as jnp
from jax import lax
from jax.experimental import pallas as pl
from jax.experimental.pallas import tpu as pltpu

F32 = jnp.float32
BF16 = jnp.bfloat16
MESH = pl.DeviceIdType.MESH
ANY = pl.BlockSpec(memory_space=pl.ANY)

N_DEV = 8
EPS = 1e-6
CHUNK = 128
A_GROUPS = 8
A_GROUP_DIM = 256
B_WINDOWS = (2, 4, 8, 16)
B_GROUP_DIM = 256
HALO = 16
DEPTH = 4

ADAM_LR = 0.001
ADAM_B1 = 0.9
ADAM_B2 = 0.999
ADAM_EPS = 1e-08
ADAM_WD = 0.01
ADAM_STEP = 10

VMEM_LIMIT_BYTES = 60 * 1024 * 1024

INV_SQRT2 = 1.0 / math.sqrt(2.0)
LOG2_E = 1.0 / math.log(2.0)
INV_SQRT_2PI = 1.0 / math.sqrt(2.0 * math.pi)


def _call(body, **kw):
    return pl.pallas_call(body, **kw)


def _params(*semantics):
    return pltpu.CompilerParams(dimension_semantics=semantics or None, vmem_limit_bytes=VMEM_LIMIT_BYTES)


def _resident(shape, index):
    return pl.BlockSpec(shape, lambda *_: index, pipeline_mode=pl.Buffered(1))


def _rows(tm, width):
    return pl.BlockSpec((tm, width), lambda i: (i, 0))


def _nn(a, b):
    return jnp.dot(a, b, preferred_element_type=F32)


def _nt(a, b):
    return lax.dot_general(a, b, (((1,), (1,)), ((), ())), preferred_element_type=F32)


def _tn(a, b):
    return lax.dot_general(a, b, (((0,), (0,)), ((), ())), preferred_element_type=F32)


def _rms_fwd(x, g):
    r = lax.rsqrt(jnp.mean(x * x, axis=-1, keepdims=True) + EPS)
    return x * r * g


def _rms_bwd(x, g, dy):
    r = lax.rsqrt(jnp.mean(x * x, axis=-1, keepdims=True) + EPS)
    xh = x * r
    dg = jnp.sum(dy * xh, axis=0, keepdims=True)
    dxh = dy * g
    dx = r * (dxh - xh * jnp.mean(dxh * xh, axis=-1, keepdims=True))
    return dx, dg


SLAB = 16


def _slabs(n):
    return [slice(r, r + SLAB) for r in range(0, n, SLAB)]


def _rms_bwd_slabs(x_at, dy_at, g, n, n_sum, emit):
    acc = jnp.zeros((8, g.shape[1]), F32)
    for rows in _slabs(n):
        x = x_at(rows)
        dy = dy_at(rows)
        r = lax.rsqrt(jnp.mean(x * x, axis=-1, keepdims=True) + EPS)
        xh = x * r
        if rows.start < n_sum:
            p = dy * xh
            acc = acc + p[:8] + p[8:]
        dxh = dy * g
        emit(rows, r * (dxh - xh * jnp.mean(dxh * xh, axis=-1, keepdims=True)))
    return jnp.sum(acc, axis=0, keepdims=True)


def _gelu(z):
    phi = 0.5 + 0.5 * lax.erf(z * INV_SQRT2)
    e = jnp.exp2(z * z * (-0.5 * LOG2_E))
    return z * phi, phi + z * e * INV_SQRT_2PI


def _layernorm_stats(v):
    mu = jnp.mean(v, axis=-1, keepdims=True)
    xc = v - mu
    rs = lax.rsqrt(jnp.mean(xc * xc, axis=-1, keepdims=True) + EPS)
    return xc * rs, rs


def _tril_mask():
    r = lax.broadcasted_iota(jnp.int32, (CHUNK, CHUNK), 0)
    c = lax.broadcasted_iota(jnp.int32, (CHUNK, CHUNK), 1)
    return r >= c


class _Layout:
    def __init__(self, d, ffn_rows):
        self.ffn_rows = ffn_rows
        self.gate, self.up, self.down = [0] * DEPTH, [self.ffn_rows] * DEPTH, [2 * self.ffn_rows] * DEPTH
        self.f_total = 3 * self.ffn_rows
        self.a_in_rows, self.a_out_rows, self.b_rows = 4 * d // N_DEV, 2 * d // N_DEV, d // N_DEV
        self.a_in, self.a_out = [0, 0], [self.a_in_rows] * 2
        self.a_total = self.a_in_rows + self.a_out_rows
        self.b_in, self.b_out = [0, 0], [self.b_rows] * 2
        self.b_total = 2 * self.b_rows


def _wspec(rows, d):
    return _resident((N_DEV * rows, d), (0, 0))


def _a_fwd(x, gpre, wg, lay, j, lng, lnb, ws, bst, gpost, tm, name):
    t, d = x.shape
    aw = 2 * d
    nch = tm // CHUNK

    def body(x_ref, gpre_ref, win_ref, lng_ref, lnb_ref, ws_ref, bst_ref, wout_ref, gpost_ref,
             x1_ref, h1_ref, gp_ref, u_ref, vh_ref, rs_ref, gated_ref, m_ref):
        xv = x_ref[...]
        h1 = _rms_fwd(xv, gpre_ref[...]).astype(BF16)
        h1_ref[...] = h1
        z = _nt(h1, win_ref[...])
        u, du_dz = _gelu(z[:, :aw])
        v, dv_dz = _gelu(z[:, aw:])
        gp_ref[:, :aw] = du_dz.astype(BF16)
        gp_ref[:, aw:] = dv_dz.astype(BF16)
        u_ref[...] = u.astype(BF16)
        vh, rs = _layernorm_stats(v)
        vh_ref[...] = vh.astype(BF16)
        rs_ref[...] = jnp.broadcast_to(rs, rs_ref.shape)
        vn = (vh * lng_ref[...] + lnb_ref[...]).astype(BF16)
        mask = _tril_mask()
        for g in range(A_GROUPS):
            wm = jnp.where(mask, ws_ref[g], 0.0).astype(BF16)
            cols = slice(g * A_GROUP_DIM, (g + 1) * A_GROUP_DIM)
            for c in range(nch):
                rows = slice(c * CHUNK, (c + 1) * CHUNK)
                sv = _nn(wm, vn[rows, cols]) + bst_ref[:, g:g + 1]
                gated_ref[rows, cols] = (u[rows, cols] * sv).astype(BF16)
        m = _nn(gated_ref[...], wout_ref[...])
        m_ref[...] = m
        x1_ref[...] = xv + _rms_fwd(m, gpost_ref[...])

    vec = lambda w: _resident((1, w), (0, 0))
    return _call(
        body, name=name, grid=(t // tm,),
        in_specs=[_rows(tm, d), vec(d), _wspec(lay.a_in_rows, d), vec(aw), vec(aw),
                  _resident((A_GROUPS, CHUNK, CHUNK), (0, 0, 0)), _resident((CHUNK, A_GROUPS), (0, 0)),
                  _wspec(lay.a_out_rows, d), vec(d)],
        out_specs=[_rows(tm, d), _rows(tm, d), _rows(tm, 2 * aw), _rows(tm, aw), _rows(tm, aw), _rows(tm, 128),
                   _rows(tm, aw), _rows(tm, d)],
        out_shape=[jax.ShapeDtypeStruct((t, d), F32), jax.ShapeDtypeStruct((t, d), BF16),
                   jax.ShapeDtypeStruct((t, 2 * aw), BF16), jax.ShapeDtypeStruct((t, aw), BF16),
                   jax.ShapeDtypeStruct((t, aw), BF16), jax.ShapeDtypeStruct((t, 128), F32),
                   jax.ShapeDtypeStruct((t, aw), BF16), jax.ShapeDtypeStruct((t, d), F32)],
        compiler_params=_params("parallel"),
    )(x, gpre, wg[0], lng, lnb, ws, bst, wg[1], gpost)


def _a_bwd(dx1, m, x, gp, u, vh, rs, gpre, wg, lay, j, lng, lnb, ws, bst, gpost, after, tm, name):
    t, d = x.shape
    aw = 2 * d
    nch = tm // CHUNK

    def body(dx1_ref, m_ref, x_ref, gp_ref, u_ref, vh_ref, rs_ref, gpre_ref, win_ref, lng_ref, lnb_ref, ws_ref, bst_ref,
             wout_ref, gpost_ref, after_ref,
             dx_ref, dm_ref, dz_ref, dgpost_ref, dgpre_ref, dlng_ref, dlnb_ref, dws_ref, dbt_ref, dvn_ref):
        @pl.when(pl.program_id(0) == 0)
        def _():
            for r in (dgpost_ref, dgpre_ref, dlng_ref, dlnb_ref, dws_ref, dbt_ref):
                r[...] = jnp.zeros_like(r)

        def put_dm(rows, dx):
            dm_ref[rows, :] = dx.astype(BF16)

        dgpost_ref[...] += _rms_bwd_slabs(lambda rows: m_ref[rows, :], lambda rows: dx1_ref[rows, :], gpost_ref[...],
                                          tm, tm, put_dm)
        dgated = _nt(dm_ref[...], wout_ref[...])

        vh = vh_ref[...].astype(F32)
        rs = rs_ref[:, :1]
        lng_v = lng_ref[...]
        vn = (vh * lng_v + lnb_ref[...]).astype(BF16)
        mask = _tril_mask()
        lane = lax.broadcasted_iota(jnp.int32, (CHUNK, CHUNK), 1)
        for g in range(A_GROUPS):
            wm = jnp.where(mask, ws_ref[g], 0.0).astype(BF16)
            cols = slice(g * A_GROUP_DIM, (g + 1) * A_GROUP_DIM)
            dws_g = jnp.zeros((CHUNK, CHUNK), F32)
            db_g = jnp.zeros((CHUNK, 1), F32)
            for c in range(nch):
                rows = slice(c * CHUNK, (c + 1) * CHUNK)
                vn_cg = vn[rows, cols]
                sv = _nn(wm, vn_cg) + bst_ref[:, g:g + 1]
                dg_cg = dgated[rows, cols]
                dsv = dg_cg * u_ref[rows, cols].astype(F32)
                dsv_bf = dsv.astype(BF16)
                db_g = db_g + jnp.sum(dsv, axis=1, keepdims=True)
                dws_g = dws_g + _nt(dsv_bf, vn_cg)
                dvn_ref[rows, cols] = _tn(wm, dsv_bf)
                dz_ref[rows, cols] = (dg_cg * sv * gp_ref[rows, cols].astype(F32)).astype(BF16)
            dws_ref[g] += jnp.where(mask, dws_g, 0.0)
            dbt_ref[...] += jnp.where(lane == g, db_g, 0.0)
        dvn = dvn_ref[...]
        dlng_ref[...] += jnp.sum(dvn * vh, axis=0, keepdims=True)
        dlnb_ref[...] += jnp.sum(dvn, axis=0, keepdims=True)
        dvh = dvn * lng_v
        dv = rs * (dvh - jnp.mean(dvh, axis=-1, keepdims=True) - vh * jnp.mean(dvh * vh, axis=-1, keepdims=True))
        dz_ref[:, aw:] = (dv * gp_ref[:, aw:].astype(F32)).astype(BF16)
        dh1 = _nn(dz_ref[...], win_ref[...])

        def put_dx(rows, dx):
            dx_ref[rows, :] = dx1_ref[rows, :] + dx

        dgpre_ref[...] += _rms_bwd_slabs(lambda rows: x_ref[rows, :], lambda rows: dh1[rows, :], gpre_ref[...],
                                         tm, tm, put_dx)

    vec = lambda w: _resident((1, w), (0, 0))
    acc = lambda shape: pl.BlockSpec(shape, lambda i: (0,) * len(shape))
    return _call(
        body, name=name, grid=(t // tm,),
        in_specs=[_rows(tm, d), _rows(tm, d), _rows(tm, d), _rows(tm, 2 * aw), _rows(tm, aw), _rows(tm, aw),
                  _rows(tm, 128), vec(d), _wspec(lay.a_in_rows, d), vec(aw), vec(aw),
                  _resident((A_GROUPS, CHUNK, CHUNK), (0, 0, 0)), _resident((CHUNK, A_GROUPS), (0, 0)),
                  _wspec(lay.a_out_rows, d), vec(d), ANY],
        out_specs=[_rows(tm, d), _rows(tm, d), _rows(tm, 2 * aw), acc((1, d)), acc((1, d)), acc((1, aw)), acc((1, aw)),
                   acc((A_GROUPS, CHUNK, CHUNK)), acc((CHUNK, CHUNK))],
        out_shape=[jax.ShapeDtypeStruct((t, d), F32), jax.ShapeDtypeStruct((t, d), BF16),
                   jax.ShapeDtypeStruct((t, 2 * aw), BF16), jax.ShapeDtypeStruct((1, d), F32),
                   jax.ShapeDtypeStruct((1, d), F32), jax.ShapeDtypeStruct((1, aw), F32),
                   jax.ShapeDtypeStruct((1, aw), F32), jax.ShapeDtypeStruct((A_GROUPS, CHUNK, CHUNK), F32),
                   jax.ShapeDtypeStruct((CHUNK, CHUNK), F32)],
        scratch_shapes=[pltpu.VMEM((tm, aw), F32)],
        compiler_params=_params("arbitrary"),
    )(dx1, m, x, gp, u, vh, rs, gpre, wg[0], lng, lnb, ws, bst, wg[1], gpost, after)


def _window_counts(first_row, n, win):
    tpos = first_row + lax.broadcasted_iota(jnp.int32, (n, 1), 0)
    return jnp.clip(tpos + 1, 1, win).astype(F32)


def _b_fwd(x, gpre, wg, lay, j, wgrp, scale, gpost, tm, name):
    t, d = x.shape
    n = tm + HALO
    ngrp = len(B_WINDOWS)

    def body(x_ref, xprev_ref, gpre_ref, win_ref, wgrp_ref, scale_ref, wout_ref, gpost_ref,
             x1_ref, h1_ref, pooled_ref, mixed_ref, m_ref):
        i = pl.program_id(0)
        xv = x_ref[...]
        keep = jnp.where(i > 0, 1.0, 0.0)
        xe = jnp.concatenate([xprev_ref[...] * keep, xv], axis=0)
        h1e = _rms_fwd(xe, gpre_ref[...]).astype(BF16)
        h1_ref[...] = h1e[HALO:]
        p = _nn(h1e, win_ref[...])
        acc = p
        shift = 1
        for g, win in enumerate(B_WINDOWS):
            lo = g * B_GROUP_DIM
            if g > 0:
                acc = acc[:, B_GROUP_DIM:]
            while shift < win:
                acc = acc + pltpu.roll(acc, shift, 0)
                shift *= 2
            cnt = _window_counts(i * tm - HALO, n, win)
            pooled = acc[:, :B_GROUP_DIM] / cnt - p[:, lo:lo + B_GROUP_DIM]
            pooled_ref[:, lo:lo + B_GROUP_DIM] = pooled[HALO:].astype(BF16)
        for g in range(ngrp):
            cols = slice(g * B_GROUP_DIM, (g + 1) * B_GROUP_DIM)
            raw = _nn(pooled_ref[:, cols], wgrp_ref[g])
            mixed_ref[:, cols] = (raw * scale_ref[:, cols]).astype(BF16)
        m = _nn(mixed_ref[...], wout_ref[...])
        m_ref[...] = m
        x1_ref[...] = xv + _rms_fwd(m, gpost_ref[...])

    vec = lambda w: _resident((1, w), (0, 0))
    per = tm // HALO
    return _call(
        body, name=name, grid=(t // tm,),
        in_specs=[_rows(tm, d), pl.BlockSpec((HALO, d), lambda i: (jnp.maximum(i * per - 1, 0), 0)), vec(d),
                  _wspec(lay.b_rows, d), _resident((ngrp, B_GROUP_DIM, B_GROUP_DIM), (0, 0, 0)), vec(d),
                  _wspec(lay.b_rows, d), vec(d)],
        out_specs=[_rows(tm, d)] * 5,
        out_shape=[jax.ShapeDtypeStruct((t, d), F32), jax.ShapeDtypeStruct((t, d), BF16),
                   jax.ShapeDtypeStruct((t, d), BF16), jax.ShapeDtypeStruct((t, d), BF16),
                   jax.ShapeDtypeStruct((t, d), F32)],
        compiler_params=_params("parallel"),
    )(x, x, gpre, wg[0], wgrp, scale, wg[1], gpost)


def _b_bwd(dx1, m, x, pooled, gpre, wg, lay, j, wgrp, scale, gpost, after, tm, name):
    t, d = x.shape
    n = tm + HALO
    ngrp = len(B_WINDOWS)
    steps = t // tm

    def body(dx1_ref, dx1n_ref, m_ref, mn_ref, x_ref, pooled_ref, pooledn_ref, gpre_ref, win_ref, wgrp_ref, scale_ref,
             wout_ref, gpost_ref, after_ref,
             dx_ref, dm_ref, draw_ref, dp_ref, dgpost_ref, dgpre_ref, dscale_ref, dpool_ref):
        i = pl.program_id(0)

        @pl.when(i == 0)
        def _():
            for r in (dgpost_ref, dgpre_ref, dscale_ref):
                r[...] = jnp.zeros_like(r)

        keep = jnp.where(i < steps - 1, 1.0, 0.0)
        dy = dx1_ref[...]
        dye = jnp.concatenate([dy, dx1n_ref[...] * keep], axis=0)
        me = jnp.concatenate([m_ref[...], mn_ref[...]], axis=0)
        gpost_v = gpost_ref[...]
        r = lax.rsqrt(jnp.mean(me * me, axis=-1, keepdims=True) + EPS)
        mh = me * r
        dgpost_ref[...] += jnp.sum((dye * mh)[:tm], axis=0, keepdims=True)
        dmh = dye * gpost_v
        dme = (r * (dmh - mh * jnp.mean(dmh * mh, axis=-1, keepdims=True))).astype(BF16)
        dm_ref[...] = dme[:tm]
        dmixed = _nt(dme, wout_ref[...])
        pooled_e = jnp.concatenate([pooled_ref[...], pooledn_ref[...]], axis=0)
        scale_v = scale_ref[...]
        for g, win in enumerate(B_WINDOWS):
            cols = slice(g * B_GROUP_DIM, (g + 1) * B_GROUP_DIM)
            raw = _nn(pooled_e[:, cols], wgrp_ref[g])
            dscale_ref[:, cols] += jnp.sum((dmixed[:, cols] * raw)[:tm], axis=0, keepdims=True)
            draw = (dmixed[:, cols] * scale_v[:, cols]).astype(BF16)
            draw_ref[:, cols] = draw[:tm]
            dpool = _nt(draw, wgrp_ref[g])
            acc = dpool / _window_counts(i * tm, n, win)
            shift = 1
            while shift < win:
                acc = acc + pltpu.roll(acc, n - shift, 0)
                shift *= 2
            dpool_ref[:, cols] = (acc - dpool)[:tm]
        dp = dpool_ref[...].astype(BF16)
        dp_ref[...] = dp
        dh1 = _nt(dp, win_ref[...])
        dxp, dgpre = _rms_bwd(x_ref[...], gpre_ref[...], dh1)
        dgpre_ref[...] += dgpre
        dx_ref[...] = dy + dxp

    vec = lambda w: _resident((1, w), (0, 0))
    acc = lambda shape: pl.BlockSpec(shape, lambda i: (0,) * len(shape))
    per = tm // HALO
    nxt = lambda i: (jnp.minimum((i + 1) * per, t // HALO - 1), 0)
    return _call(
        body, name=name, grid=(steps,),
        in_specs=[_rows(tm, d), pl.BlockSpec((HALO, d), nxt), _rows(tm, d), pl.BlockSpec((HALO, d), nxt), _rows(tm, d),
                  _rows(tm, d), pl.BlockSpec((HALO, d), nxt), vec(d), _wspec(lay.b_rows, d),
                  _resident((ngrp, B_GROUP_DIM, B_GROUP_DIM), (0, 0, 0)), vec(d), _wspec(lay.b_rows, d),
                  vec(d), ANY],
        out_specs=[_rows(tm, d)] * 4 + [acc((1, d))] * 3,
        out_shape=[jax.ShapeDtypeStruct((t, d), F32), jax.ShapeDtypeStruct((t, d), BF16),
                   jax.ShapeDtypeStruct((t, d), BF16), jax.ShapeDtypeStruct((t, d), BF16)]
                  + [jax.ShapeDtypeStruct((1, d), F32)] * 3,
        scratch_shapes=[pltpu.VMEM((tm, d), F32)],
        compiler_params=_params("arbitrary"),
    )(dx1, dx1, m, m, x, pooled, pooled, gpre, wg[0], wgrp, scale, wg[1], gpost, after)


def _f_fwd(x1, gpre, wg, lay, l, gpost, tm, name, target=None):
    t, d = x1.shape
    hid = N_DEV * lay.ffn_rows
    head = target is not None

    def body(x_ref, gpre_ref, wgate_ref, wup_ref, wdown_ref, gpost_ref, *rest):
        x2_ref, h2_ref, abs_ref, f_ref = rest[-5:-1] if head else rest
        xv = x_ref[...]
        h2 = _rms_fwd(xv, gpre_ref[...]).astype(BF16)
        h2_ref[...] = h2
        a = _nt(h2, wgate_ref[...])
        b = _nt(h2, wup_ref[...])
        sig = jax.nn.sigmoid(a)
        silu = a * sig
        abs_ref[:, :hid] = (b * (sig + silu * (1.0 - sig))).astype(BF16)
        abs_ref[:, hid:2 * hid] = silu.astype(BF16)
        s = (silu * b).astype(BF16)
        abs_ref[:, 2 * hid:] = s
        f = _nn(s, wdown_ref[...])
        f_ref[...] = f
        x2 = xv + _rms_fwd(f, gpost_ref[...])
        if head:
            target_ref, loss_ref = rest[0], rest[-1]

            @pl.when(pl.program_id(0) == 0)
            def _():
                loss_ref[...] = jnp.zeros_like(loss_ref)

            diff = x2 - target_ref[...]
            x2_ref[...] = diff * (1.0 / d)
            sq = jnp.sum(jnp.sum(diff * diff, axis=0, keepdims=True), axis=1, keepdims=True)
            loss_ref[...] += sq * (0.5 / d)
        else:
            x2_ref[...] = x2

    vec = lambda w: _resident((1, w), (0, 0))
    return _call(
        body, name=name, grid=(t // tm,),
        in_specs=[_rows(tm, d), vec(d), _wspec(lay.ffn_rows, d), _wspec(lay.ffn_rows, d),
                  _wspec(lay.ffn_rows, d), vec(d)] + ([_rows(tm, d)] if head else []),
        out_specs=[_rows(tm, d), _rows(tm, d), _rows(tm, 3 * hid), _rows(tm, d)]
                  + ([pl.BlockSpec((8, 128), lambda i: (0, 0))] if head else []),
        out_shape=[jax.ShapeDtypeStruct((t, d), F32), jax.ShapeDtypeStruct((t, d), BF16),
                   jax.ShapeDtypeStruct((t, 3 * hid), BF16), jax.ShapeDtypeStruct((t, d), F32)]
                  + ([jax.ShapeDtypeStruct((8, 128), F32)] if head else []),
        compiler_params=_params("arbitrary" if head else "parallel"),
    )(x1, gpre, wg[0], wg[1], wg[2], gpost, *([target] if head else []))


def _f_bwd(dx2, f, x1, acts, gpre, wg, lay, l, gpost, after, tm, name):
    t, d = x1.shape
    hid = N_DEV * lay.ffn_rows

    def body(dx2_ref, f_ref, x_ref, ab_ref, gpre_ref, wgate_ref, wup_ref, wdown_ref, gpost_ref, after_ref,
             dx1_ref, df_ref, dab_ref, dgpost_ref, dgpre_ref):
        @pl.when(pl.program_id(0) == 0)
        def _():
            dgpost_ref[...] = jnp.zeros_like(dgpost_ref)
            dgpre_ref[...] = jnp.zeros_like(dgpre_ref)

        def put_df(rows, dx):
            df_ref[rows, :] = dx.astype(BF16)

        dgpost_ref[...] += _rms_bwd_slabs(lambda rows: f_ref[rows, :], lambda rows: dx2_ref[rows, :], gpost_ref[...],
                                          tm, tm, put_df)
        ds = _nt(df_ref[...], wdown_ref[...])
        dab_ref[:, :hid] = (ds * ab_ref[:, :hid].astype(F32)).astype(BF16)
        dab_ref[:, hid:] = (ds * ab_ref[:, hid:].astype(F32)).astype(BF16)
        dh2 = _nn(dab_ref[:, :hid], wgate_ref[...]) + _nn(dab_ref[:, hid:], wup_ref[...])

        def put_dx(rows, dx):
            dx1_ref[rows, :] = dx2_ref[rows, :] + dx

        dgpre_ref[...] += _rms_bwd_slabs(lambda rows: x_ref[rows, :], lambda rows: dh2[rows, :], gpre_ref[...],
                                         tm, tm, put_dx)

    vec = lambda w: _resident((1, w), (0, 0))
    acc = pl.BlockSpec((1, d), lambda i: (0, 0))
    return _call(
        body, name=name, grid=(t // tm,),
        in_specs=[_rows(tm, d), _rows(tm, d), _rows(tm, d), _rows(tm, 2 * hid), vec(d),
                  _wspec(lay.ffn_rows, d), _wspec(lay.ffn_rows, d),
                  _wspec(lay.ffn_rows, d), vec(d), ANY],
        out_specs=[_rows(tm, d), _rows(tm, d), _rows(tm, 2 * hid), acc, acc],
        out_shape=[jax.ShapeDtypeStruct((t, d), F32), jax.ShapeDtypeStruct((t, d), BF16),
                   jax.ShapeDtypeStruct((t, 3 * hid), BF16),
                   jax.ShapeDtypeStruct((1, d), F32), jax.ShapeDtypeStruct((1, d), F32)],
        input_output_aliases={3: 2},
        compiler_params=_params("arbitrary"),
    )(dx2, f, x1, acts, gpre, wg[0], wg[1], wg[2], gpost, after)


def _grad_into(gbuf, lhs, rhs, off, rows, name, after=None):
    t, m = lhs.shape
    d = rhs.shape[1]
    assert m == N_DEV * rows and off % rows == 0
    per_tile = {512: 2, 256: 4, 128: 8}[rows]
    tm = per_tile * rows
    assert tm % 128 == 0 and rows % 16 == 0
    tk = 2048 if t % 2048 == 0 else 256
    ksteps = t // tk
    fresh = isinstance(gbuf, int)
    shape = (N_DEV, gbuf, d) if fresh else gbuf.shape
    extra = ([] if fresh else [gbuf]) + ([] if after is None else [after])

    def body(l_ref, r_ref, *rest):
        o_ref, acc_ref = rest[-2:]
        k = pl.program_id(1)

        @pl.when(k == 0)
        def _():
            acc_ref[...] = jnp.zeros_like(acc_ref)

        acc_ref[...] += _tn(l_ref[...], r_ref[...])

        @pl.when(k == ksteps - 1)
        def _():
            o_ref[...] = acc_ref[...].reshape(per_tile, rows, d).astype(BF16)

    return _call(
        body, name=name, grid=(N_DEV // per_tile, ksteps),
        in_specs=[pl.BlockSpec((tk, tm), lambda i, k: (k, i)), pl.BlockSpec((tk, d), lambda i, k: (k, 0))]
                 + [ANY] * len(extra),
        out_specs=pl.BlockSpec((per_tile, rows, d), lambda i, k: (i, off // rows, 0)),
        out_shape=jax.ShapeDtypeStruct(shape, BF16),
        scratch_shapes=[pltpu.VMEM((tm, d), F32)],
        input_output_aliases={} if fresh else {2: 0},
        compiler_params=_params("parallel", "arbitrary"),
    )(lhs, rhs, *extra)


def _grad_ffn(acts, h2, df, rows, name):
    t, d = h2.shape
    per_tile = 4
    tm = per_tile * rows
    tiles = N_DEV // per_tile
    assert acts.shape[1] == 3 * N_DEV * rows and tm % 128 == 0 and rows % 16 == 0
    tk = 2048 if t % 2048 == 0 else 256
    ksteps = t // tk

    def body(l_ref, h2_ref, df_ref, o_ref, acc_ref):
        i, k = pl.program_id(0), pl.program_id(1)

        @pl.when(k == 0)
        def _():
            acc_ref[...] = jnp.zeros_like(acc_ref)

        @pl.when(i < 2 * tiles)
        def _():
            acc_ref[...] += _tn(l_ref[...], h2_ref[...])

        @pl.when(i >= 2 * tiles)
        def _():
            acc_ref[...] += _tn(l_ref[...], df_ref[...])

        @pl.when(k == ksteps - 1)
        def _():
            o_ref[...] = acc_ref[...].reshape(per_tile, rows, d).astype(BF16)

    return _call(
        body, name=name, grid=(3 * tiles, ksteps),
        in_specs=[pl.BlockSpec((tk, tm), lambda i, k: (k, i)),
                  pl.BlockSpec((tk, d), lambda i, k: (jnp.where(i < 2 * tiles, k, ksteps - 1), 0)),
                  pl.BlockSpec((tk, d), lambda i, k: (jnp.where(i >= 2 * tiles, k, 0), 0))],
        out_specs=pl.BlockSpec((per_tile, rows, d), lambda i, k: (i % tiles, i // tiles, 0)),
        out_shape=jax.ShapeDtypeStruct((N_DEV, 3 * rows, d), BF16),
        scratch_shapes=[pltpu.VMEM((tm, d), F32)],
        compiler_params=_params("arbitrary", "arbitrary"),
    )(acts, h2, df)


def _grad_grouped(pooled, draw, name):
    t, d = pooled.shape
    ngrp = len(B_WINDOWS)
    tk = 1024 if t % 1024 == 0 else 256

    def body(p_ref, q_ref, o_ref):
        @pl.when(pl.program_id(0) == 0)
        def _():
            o_ref[...] = jnp.zeros_like(o_ref)

        for g in range(ngrp):
            cols = slice(g * B_GROUP_DIM, (g + 1) * B_GROUP_DIM)
            o_ref[g] += _tn(p_ref[:, cols], q_ref[:, cols])

    return _call(
        body, name=name, grid=(t // tk,),
        in_specs=[_rows(tk, d), _rows(tk, d)],
        out_specs=pl.BlockSpec((ngrp, B_GROUP_DIM, B_GROUP_DIM), lambda i: (0, 0, 0)),
        out_shape=jax.ShapeDtypeStruct((ngrp, B_GROUP_DIM, B_GROUP_DIM), F32),
        compiler_params=_params("arbitrary"),
    )(pooled, draw)


def _peers():
    x, y, c = lax.axis_index("x"), lax.axis_index("y"), lax.axis_index("c")
    flip = lambda v, f: 1 - v if f else v
    peers = []
    for r in range(1, N_DEV):
        px, py, pc = flip(x, r & 4), flip(y, r & 2), flip(c, r & 1)
        peers.append(((px, py, pc), 4 * px + 2 * py + pc))
    return 4 * x + 2 * y + c, peers


HBM = pl.BlockSpec(memory_space=pltpu.HBM)
SEM = pl.BlockSpec(memory_space=pltpu.SEMAPHORE)
EFFECT = pltpu.SideEffectType.DATAFLOW_SIDE_EFFECTING


def _peer_copies(scatter, srcs, lands, send_sems, recv_sems):
    me, peers = _peers()
    copies = []
    for a in range(len(srcs)):
        rows = srcs[a].shape[0]
        block = lambda k: lands[a].at[pl.ds(pl.multiple_of(k * rows, 8), rows)]
        for r, (peer, pidx) in enumerate(peers):
            src = srcs[a].at[pidx] if scatter else srcs[a]
            mine = lands[a].at[r] if scatter else block(pidx)
            theirs = lands[a].at[r] if scatter else block(me)
            send = pltpu.make_async_remote_copy(src_ref=src, dst_ref=theirs, send_sem=send_sems[a].at[r],
                                                recv_sem=recv_sems[a].at[r], device_id=peer, device_id_type=MESH)
            recv = pltpu.make_async_remote_copy(src_ref=src, dst_ref=mine, send_sem=send_sems[a].at[r],
                                                recv_sem=recv_sems[a].at[r], device_id=peer, device_id_type=MESH)
            copies.append((send, recv))
    return copies


def _own_copies(srcs, lands, send_sems):
    me, _ = _peers()
    copies = []
    for a in range(len(srcs)):
        rows = srcs[a].shape[0]
        copies.append(pltpu.make_async_copy(srcs[a], lands[a].at[pl.ds(pl.multiple_of(me * rows, 8), rows)],
                                            send_sems[a].at[N_DEV - 1]))
    return copies


def _exchange_start(scatter, srcs, lands, after, name):
    n = len(srcs)

    def body(*refs):
        src_refs, land_refs = refs[:n], refs[n:2 * n]
        outs = refs[2 * n + 1:]
        send_sems, recv_sems, token = outs[:n], outs[n:2 * n], outs[-1]
        for send, _ in _peer_copies(scatter, src_refs, land_refs, send_sems, recv_sems):
            send.start()
        if not scatter:
            for own in _own_copies(src_refs, land_refs, send_sems):
                own.start()
        token[...] = jnp.zeros_like(token)

    hbm = lambda a: pltpu.with_memory_space_constraint(a, pltpu.HBM)
    res = _call(
        body, name=name,
        in_specs=[HBM] * (2 * n) + [ANY],
        out_specs=[SEM] * (2 * n) + [HBM] * (2 * n) + [pl.BlockSpec(memory_space=pltpu.VMEM)],
        out_shape=[pltpu.SemaphoreType.DMA((N_DEV,))] * (2 * n)
                  + [pltpu.HBM(a.shape, a.dtype) for a in list(srcs) + list(lands)]
                  + [jax.ShapeDtypeStruct((8, 128), F32)],
        input_output_aliases={i: 2 * n + i for i in range(2 * n)},
        compiler_params=pltpu.CompilerParams(has_side_effects=EFFECT),
    )(*[hbm(a) for a in srcs], *[hbm(a) for a in lands], after)
    return res[:n], res[n:2 * n], res[2 * n:3 * n], res[3 * n:4 * n], res[-1]


def _chip_peers():
    x, y, c = lax.axis_index("x"), lax.axis_index("y"), lax.axis_index("c")
    far = []
    for px, py in ((1 - x, y), (x, 1 - y), (1 - x, 1 - y)):
        far.append(((px, py, c), 4 * px + 2 * py + c, 4 * px + 2 * py + 1 - c))
    return 4 * x + 2 * y + c, ((x, y, 1 - c), 4 * x + 2 * y + 1 - c), far


def _block(land, rows, k):
    return land.at[pl.ds(pl.multiple_of(k * rows, 8), rows)]


def _gather2_first(srcs, lands, after, name):
    n = len(srcs)

    def body(*refs):
        src_refs, land_refs = refs[:n], refs[n:2 * n]
        outs = refs[2 * n + 1:]
        send, recv_sib, recv_far, token = outs[:n], outs[n:2 * n], outs[2 * n:3 * n], outs[-1]
        me, (sib, _), far = _chip_peers()
        for a in range(n):
            rows = src_refs[a].shape[0]
            mine = _block(land_refs[a], rows, me)
            pltpu.make_async_copy(src_refs[a], mine, send[a].at[4]).start()
            pltpu.make_async_remote_copy(src_ref=src_refs[a], dst_ref=mine, send_sem=send[a].at[0],
                                         recv_sem=recv_sib[a].at[0], device_id=sib, device_id_type=MESH).start()
            for j, (peer, _, _) in enumerate(far):
                pltpu.make_async_remote_copy(src_ref=src_refs[a], dst_ref=mine, send_sem=send[a].at[1 + j],
                                             recv_sem=recv_far[a].at[j], device_id=peer, device_id_type=MESH).start()
        token[...] = jnp.zeros_like(token)

    hbm = lambda a: pltpu.with_memory_space_constraint(a, pltpu.HBM)
    res = _call(
        body, name=name,
        in_specs=[HBM] * (2 * n) + [ANY],
        out_specs=[SEM] * (3 * n) + [HBM] * (2 * n) + [pl.BlockSpec(memory_space=pltpu.VMEM)],
        out_shape=[pltpu.SemaphoreType.DMA((5,))] * n + [pltpu.SemaphoreType.DMA((1,))] * n
                  + [pltpu.SemaphoreType.DMA((3,))] * n
                  + [pltpu.HBM(a.shape, a.dtype) for a in list(srcs) + list(lands)]
                  + [jax.ShapeDtypeStruct((8, 128), F32)],
        input_output_aliases={i: 3 * n + i for i in range(2 * n)},
        compiler_params=pltpu.CompilerParams(has_side_effects=EFFECT),
    )(*[hbm(a) for a in srcs], *[hbm(a) for a in lands], after)
    return res[:n], res[n:2 * n], res[2 * n:3 * n], res[3 * n:4 * n], res[4 * n:5 * n], res[-1]


def _gather2_forward(recv_far, srcs, lands, after, name):
    n = len(lands)
    after = list(after) if isinstance(after, (list, tuple)) else [after]

    def body(*refs):
        src_refs, land_refs, far_sems = refs[:n], refs[n:2 * n], refs[2 * n:3 * n]
        outs = refs[3 * n + len(after):]
        send, recv, token = outs[:n], outs[n:2 * n], outs[-1]
        _, (sib, _), far = _chip_peers()
        for a in range(n):
            rows = src_refs[a].shape[0]
            for j, (peer, pidx, _) in enumerate(far):
                got = _block(land_refs[a], rows, pidx)
                pltpu.make_async_remote_copy(src_ref=src_refs[a], dst_ref=got, send_sem=send[a].at[j],
                                             recv_sem=far_sems[a].at[j], device_id=peer,
                                             device_id_type=MESH).wait_recv()
                pltpu.make_async_remote_copy(src_ref=got, dst_ref=got, send_sem=send[a].at[j], recv_sem=recv[a].at[j],
                                             device_id=sib, device_id_type=MESH).start()
        token[...] = jnp.zeros_like(token)

    res = _call(
        body, name=name,
        in_specs=[HBM] * (2 * n) + [SEM] * n + [ANY] * len(after),
        out_specs=[SEM] * (2 * n) + [HBM] * n + [pl.BlockSpec(memory_space=pltpu.VMEM)],
        out_shape=[pltpu.SemaphoreType.DMA((3,))] * (2 * n) + [pltpu.HBM(a.shape, a.dtype) for a in lands]
                  + [jax.ShapeDtypeStruct((8, 128), F32)],
        input_output_aliases={n + i: 2 * n + i for i in range(n)},
        compiler_params=pltpu.CompilerParams(has_side_effects=EFFECT),
    )(*srcs, *lands, *recv_far, *after)
    return res[:n], res[n:2 * n], res[2 * n:3 * n], res[-1]


def _gather2_wait(send, recv_sib, fwd_send, fwd_recv, srcs, lands, after, name):
    n = len(lands)

    def body(*refs):
        src_refs, land_refs = refs[:n], refs[n:2 * n]
        s_refs, rs_refs, fs_refs, fr_refs = (refs[(2 + q) * n:(3 + q) * n] for q in range(4))
        me, (sib, sib_idx), far = _chip_peers()
        for a in range(n):
            rows = src_refs[a].shape[0]
            mine = _block(land_refs[a], rows, me)
            pltpu.make_async_copy(src_refs[a], mine, s_refs[a].at[4]).wait()
            to_sib = pltpu.make_async_remote_copy(src_ref=src_refs[a], dst_ref=_block(land_refs[a], rows, sib_idx),
                                                  send_sem=s_refs[a].at[0], recv_sem=rs_refs[a].at[0], device_id=sib,
                                                  device_id_type=MESH)
            to_sib.wait_send()
            to_sib.wait_recv()
            for j, (peer, pidx, pair_idx) in enumerate(far):
                pltpu.make_async_remote_copy(src_ref=src_refs[a], dst_ref=mine, send_sem=s_refs[a].at[1 + j],
                                             recv_sem=fr_refs[a].at[j], device_id=peer,
                                             device_id_type=MESH).wait_send()
                fwd = pltpu.make_async_remote_copy(src_ref=_block(land_refs[a], rows, pidx),
                                                   dst_ref=_block(land_refs[a], rows, pair_idx),
                                                   send_sem=fs_refs[a].at[j], recv_sem=fr_refs[a].at[j], device_id=sib,
                                                   device_id_type=MESH)
                fwd.wait_send()
                fwd.wait_recv()

    res = _call(
        body, name=name,
        in_specs=[HBM] * (2 * n) + [SEM] * (4 * n) + [ANY],
        out_specs=[HBM] * (2 * n),
        out_shape=[pltpu.HBM(a.shape, a.dtype) for a in list(srcs) + list(lands)],
        input_output_aliases={i: i for i in range(2 * n)},
        compiler_params=pltpu.CompilerParams(has_side_effects=EFFECT),
    )(*srcs, *lands, *send, *recv_sib, *fwd_send, *fwd_recv, after)
    return res[n:]


def _exchange_wait(scatter, send_sems, recv_sems, srcs, lands, after, name):
    n = len(srcs)
    after = list(after) if isinstance(after, (list, tuple)) else [after]

    def body(*refs):
        src_refs, land_refs = refs[:n], refs[n:2 * n]
        send_refs, recv_refs = refs[2 * n:3 * n], refs[3 * n:4 * n]
        for send, recv in _peer_copies(scatter, src_refs, land_refs, send_refs, recv_refs):
            send.wait_send()
            recv.wait_recv()
        if not scatter:
            for own in _own_copies(src_refs, land_refs, send_refs):
                own.wait()

    res = _call(
        body, name=name,
        in_specs=[HBM] * (2 * n) + [SEM] * (2 * n) + [ANY] * len(after),
        out_specs=[HBM] * (2 * n),
        out_shape=[pltpu.HBM(a.shape, a.dtype) for a in list(srcs) + list(lands)],
        input_output_aliases={i: i for i in range(2 * n)},
        compiler_params=pltpu.CompilerParams(has_side_effects=EFFECT),
    )(*srcs, *lands, *send_sems, *recv_sems, *after)
    return res[:n], res[n:]


def _row_tile(rows):
    if rows <= 512:
        return rows
    return max([tr for tr in range(16, 513, 16) if rows % tr == 0] or [rows])


def _sum_parts(own, got, me, name):
    _, rows, w = own.shape
    tr = _row_tile(rows)

    def body(me_ref, a_ref, b_ref, o_ref):
        s = a_ref[...].astype(F32)
        for j in range(N_DEV - 1):
            s = s + b_ref[j].astype(F32)
        o_ref[...] = s

    return _call(
        body, name=name,
        grid_spec=pltpu.PrefetchScalarGridSpec(
            num_scalar_prefetch=1, grid=(rows // tr,),
            in_specs=[pl.BlockSpec((None, tr, w), lambda i, me_ref: (me_ref[0], i, 0)),
                      pl.BlockSpec((N_DEV - 1, tr, w), lambda i, me_ref: (0, i, 0))],
            out_specs=pl.BlockSpec((tr, w), lambda i, me_ref: (i, 0))),
        out_shape=jax.ShapeDtypeStruct((rows, w), F32),
        compiler_params=_params("parallel"),
    )(me, own, got)


def _sum_devices(stacked, name):
    k, rows, w = stacked.shape
    tr = _row_tile(rows)

    def body(a_ref, o_ref):
        s = a_ref[0]
        for j in range(1, k):
            s = s + a_ref[j]
        o_ref[...] = s

    return _call(
        body, name=name, grid=(rows // tr,),
        in_specs=[pl.BlockSpec((k, tr, w), lambda i: (0, i, 0))],
        out_specs=pl.BlockSpec((tr, w), lambda i: (i, 0)),
        out_shape=jax.ShapeDtypeStruct((rows, w), F32),
        compiler_params=_params("parallel"),
    )(stacked)


def _adamw_math(w, g, m, v):
    nm = ADAM_B1 * m + (1.0 - ADAM_B1) * g
    nv = ADAM_B2 * v + (1.0 - ADAM_B2) * (g * g)
    m_hat = nm / (1.0 - ADAM_B1 ** ADAM_STEP)
    v_hat = nv / (1.0 - ADAM_B2 ** ADAM_STEP)
    return -ADAM_LR * (m_hat / (jnp.sqrt(v_hat) + ADAM_EPS) + ADAM_WD * w), nm, nv


def _sum_adamw(owns, gots, me, off, rows, w, m, v, name):
    nl = len(owns)
    d = w.shape[-1]
    assert off % rows == 0 and w.shape == (nl, rows, d)

    def body(me_ref, *refs):
        own_refs, got_refs = refs[:nl], refs[nl:2 * nl]
        w_ref, m_ref, v_ref, g_ref, d_ref, nm_ref, nv_ref = refs[2 * nl:]
        for l in range(nl):
            @pl.when(pl.program_id(0) == l)
            def _(l=l):
                gv = own_refs[l][...].astype(F32)
                for r in range(N_DEV - 1):
                    gv = gv + got_refs[l][r].astype(F32)
                g_ref[...] = gv
                d_ref[...], nm_ref[...], nv_ref[...] = _adamw_math(w_ref[...], gv, m_ref[...], v_ref[...])

    layer = pl.BlockSpec((None, rows, d), lambda i, me_ref: (i, 0, 0))
    own = pl.BlockSpec((None, rows, d), lambda i, me_ref: (me_ref[0], off // rows, 0), pipeline_mode=pl.Buffered(1))
    got = pl.BlockSpec((N_DEV - 1, rows, d), lambda i, me_ref: (0, off // rows, 0), pipeline_mode=pl.Buffered(1))
    return _call(
        body, name=name,
        grid_spec=pltpu.PrefetchScalarGridSpec(
            num_scalar_prefetch=1, grid=(nl,),
            in_specs=[own] * nl + [got] * nl + [layer] * 3, out_specs=[layer] * 4),
        out_shape=[jax.ShapeDtypeStruct((nl, rows, d), F32)] * 4,
        compiler_params=_params("arbitrary"),
    )(me, *owns, *gots, w, m, v)


def _adamw(w, g, m, v, name):
    rows, cols = w.shape
    tr = _row_tile(rows)

    def body(w_ref, g_ref, m_ref, v_ref, d_ref, nm_ref, nv_ref):
        d_ref[...], nm_ref[...], nv_ref[...] = _adamw_math(w_ref[...], g_ref[...], m_ref[...], v_ref[...])

    spec = pl.BlockSpec((tr, cols), lambda i: (i, 0))
    return _call(
        body, name=name, grid=(rows // tr,),
        in_specs=[spec] * 4, out_specs=[spec] * 3,
        out_shape=[jax.ShapeDtypeStruct((rows, cols), F32)] * 3,
        compiler_params=_params("parallel"),
    )(w, g, m, v)


SMALL = ("a_ln_g", "a_ln_b", "a_w_s", "a_b_s", "mix_pre_g", "mix_post_g", "ffn_pre_g", "ffn_post_g")


def _pack_small(parts, d, last_row=None):
    rows = [parts[k].reshape(-1, d) for k in SMALL] + ([] if last_row is None else [last_row])
    flat = jnp.concatenate(rows, axis=0)
    return jnp.pad(flat, ((0, -flat.shape[0] % 8), (0, 0)))


def _unpack_small(flat, like):
    out, r = {}, 0
    for k in SMALL:
        n = like[k].size // flat.shape[1]
        out[k] = flat[r:r + n].reshape(like[k].shape)
        r += n
    return out


def kernel(x, a_w_in, a_ln_g, a_ln_b, a_w_s, a_b_s, a_w_out, b_w_in, b_w_grp, b_scale, b_w_out, mix_pre_g, mix_post_g, ffn_pre_g, ffn_post_g, ffn_w_gate, ffn_w_up, ffn_w_down, loss_target, m_a_w_in, m_a_ln_g, m_a_ln_b, m_a_w_s, m_a_b_s, m_a_w_out, m_b_w_in, m_b_w_grp, m_b_scale, m_b_w_out, m_mix_pre_g, m_mix_post_g, m_ffn_pre_g, m_ffn_post_g, m_ffn_w_gate, m_ffn_w_up, m_ffn_w_down, v_a_w_in, v_a_ln_g, v_a_ln_b, v_a_w_s, v_a_b_s, v_a_w_out, v_b_w_in, v_b_w_grp, v_b_scale, v_b_w_out, v_mix_pre_g, v_mix_post_g, v_ffn_pre_g, v_ffn_post_g, v_ffn_w_gate, v_ffn_w_up, v_ffn_w_down):
    args = dict(locals())
    names = ("a_w_in", "a_ln_g", "a_ln_b", "a_w_s", "a_b_s", "a_w_out", "b_w_in", "b_w_grp", "b_scale", "b_w_out",
             "mix_pre_g", "mix_post_g", "ffn_pre_g", "ffn_post_g", "ffn_w_gate", "ffn_w_up", "ffn_w_down")
    w = {k: args[k] for k in names}
    mom = {k: args["m_" + k] for k in names}
    var = {k: args["v_" + k] for k in names}

    t, d = x.shape[1], x.shape[2]
    ffn_local = ffn_w_gate.shape[2]
    lay = _Layout(d, ffn_local)
    me = 4 * lax.axis_index("x") + 2 * lax.axis_index("y") + lax.axis_index("c")
    me1 = jnp.reshape(me, (1,)).astype(jnp.int32)

    def landing(block):
        return lax.empty((N_DEV * block.shape[0],) + block.shape[1:], block.dtype)

    def shards(i, mixer, zero):
        j = i // 2
        if not mixer:
            parts = [ffn_w_gate[i].T, ffn_w_up[i].T, ffn_w_down[i]]
        elif i % 2 == 0:
            parts = [a_w_in[j].T, a_w_out[j]]
        else:
            parts = [b_w_in[j], b_w_out[j]]
        return [(p + zero).astype(BF16) for p in parts]

    nsub = 2 * DEPTH
    wg = [None] * nsub
    first = shards(0, True, 0.0)
    f_send, f_sib, f_far, first, f_zones, f_token = _gather2_first(
        first, [landing(b) for b in first], jnp.zeros((8, 128), F32), "gather_first_start")
    zero = f_token[0, 0]
    ngrp = len(B_WINDOWS)
    grp_local = b_w_grp.shape[2]
    sdev = b_scale.shape[1]
    side_rows = 2 * ngrp * grp_local
    side = jnp.concatenate(
        [b_w_grp.reshape(side_rows, B_GROUP_DIM),
         jnp.pad(b_scale, ((0, 6), (0, B_GROUP_DIM - sdev)))], axis=0) + zero
    later, where = [side], [slice(0, 1)]
    for k in range(1, nsub):
        new = shards(k // 2, k % 2 == 0, zero)
        where.append(slice(len(later), len(later) + len(new)))
        later += new
    send_sems, recv_sems, later, zones, token = _exchange_start(
        False, later, [landing(b) for b in later], f_token, "gather_start")
    turned = ("ffn_w_gate", "ffn_w_up")
    turn = lambda a: jnp.swapaxes(a, 1, 2)
    state = {k: tuple(turn(a[k]) for a in (w, mom, var)) for k in turned}
    state["small"] = tuple(_pack_small(a, d) for a in (w, mom, var))
    ready = [a for group in state.values() for a in group]
    fwd_send, fwd_recv, f_zones, fwd_token = _gather2_forward(f_far, first, f_zones, [token] + ready,
                                                              "gather_first_forward")
    wg[0] = _gather2_wait(f_send, f_sib, fwd_send, fwd_recv, first, f_zones, fwd_token, "gather_first_wait")

    def gathered(k, after):
        s = where[k]
        _, got = _exchange_wait(False, send_sems[s], recv_sems[s], later[s], zones[s], after, f"gather_wait_{k}")
        return got

    row = lambda a: a.reshape(1, -1)
    bst = jnp.transpose(a_b_s, (0, 2, 1))

    tm = 256 if t % 256 == 0 else CHUNK
    tm_abwd = tm
    tm_b = 512 if t % 512 == 0 else tm
    tm_f = tm

    saved = []
    h = x[0]
    wgrp_full = scale_full = None
    for i in range(DEPTH):
        j = i // 2
        gpre = row(mix_pre_g[i])
        if i > 0:
            wg[2 * i] = gathered(2 * i, h)
        if i % 2 == 0:
            x1, h1, gp, u, vh, rs, gated, m = _a_fwd(h, gpre, wg[2 * i], lay, j, row(a_ln_g[j]), row(a_ln_b[j]),
                                                     a_w_s[j], bst[j], row(mix_post_g[i]), tm, f"a_fwd_{j}")
            mix = dict(h1=h1, gp=gp, u=u, vh=vh, rs=rs, gated=gated, m=m)
        else:
            if wgrp_full is None:
                side_g = gathered(0, h)[0].reshape(N_DEV, side_rows + 8, B_GROUP_DIM)
                wgrp_full = (side_g[:, :side_rows].reshape(N_DEV, 2, ngrp, grp_local, B_GROUP_DIM)
                             .transpose(1, 2, 0, 3, 4).reshape(2, ngrp, B_GROUP_DIM, B_GROUP_DIM).astype(BF16))
                scale_full = (side_g[:, side_rows:side_rows + 2, :sdev].transpose(1, 0, 2)
                              .reshape(2, 1, N_DEV * sdev))
            x1, h1, pooled, mixed, m = _b_fwd(h, gpre, wg[2 * i], lay, j, wgrp_full[j], scale_full[j],
                                              row(mix_post_g[i]), tm_b, f"b_fwd_{j}")
            mix = dict(h1=h1, pooled=pooled, mixed=mixed, m=m)
        wg[2 * i + 1] = gathered(2 * i + 1, x1)
        x2, h2, acts, f, *loss_acc = _f_fwd(x1, row(ffn_pre_g[i]), wg[2 * i + 1], lay, i, row(ffn_post_g[i]), tm_f,
                                               f"f_fwd_{i}", loss_target[0] if i == DEPTH - 1 else None)
        saved.append(dict(x=h, x1=x1, mix=mix, h2=h2, acts=acts, f=f))
        h = x2
    dy, (loss_acc,) = h, loss_acc

    small_g = {k: [None] * w[k].shape[0] for k in SMALL}
    dgrp, dscale = [None, None], [None, None]
    pending = [None] * nsub
    token = jnp.zeros((8, 128), F32)

    def scatter(ks, gbufs):
        gots = [pltpu.with_memory_space_constraint(lax.empty((N_DEV - 1,) + g.shape[1:], g.dtype), pltpu.HBM)
                for g in gbufs]
        ss, rs, src, zone, tok = _exchange_start(True, gbufs, gots, token, f"scatter_start_{ks[0]}")
        for n, k in enumerate(ks):
            pending[k] = (ss[n:n + 1], rs[n:n + 1], src[n:n + 1], zone[n:n + 1])
        return tok

    def small_exchanges():
        side_grad = jnp.concatenate(
            [jnp.stack(dgrp).reshape(2, ngrp, N_DEV, grp_local, B_GROUP_DIM).transpose(2, 0, 1, 3, 4)
             .reshape(N_DEV, side_rows, B_GROUP_DIM),
             jnp.pad(jnp.stack(dscale).reshape(2, N_DEV, sdev).transpose(1, 0, 2),
                     ((0, 0), (0, 6), (0, B_GROUP_DIM - sdev)))], axis=1)
        small_part = _pack_small({k: jnp.stack(small_g[k]) for k in SMALL}, d,
                                 jnp.broadcast_to(loss_acc[:1, :1], (1, d)))
        got = pltpu.with_memory_space_constraint(lax.empty((N_DEV - 1,) + side_grad.shape[1:], F32), pltpu.HBM)
        side_x = _exchange_start(True, [side_grad], [got], token, "side_scatter_start")
        small_x = _exchange_start(False, [small_part], [landing(small_part)], side_x[4], "small_gather_start")
        return side_x[:4], small_x[:4], small_x[4]

    for i in reversed(range(DEPTH)):
        sv = saved[i]
        j = i // 2
        wf, wm = wg[2 * i + 1], wg[2 * i]
        dx1, df, dacts, dgpost, dgpre = _f_bwd(dy, sv["f"], sv["x1"], sv["acts"], row(ffn_pre_g[i]), wf, lay, i,
                                               row(ffn_post_g[i]), token, tm_f, f"f_bwd_{i}")
        small_g["ffn_post_g"][i], small_g["ffn_pre_g"][i] = dgpost[0], dgpre[0]
        gbuf_f = _grad_ffn(dacts, sv["h2"], df, lay.ffn_rows, f"g_ffn_{i}")
        if i == 0:
            token = scatter([2 * i + 1], [gbuf_f])
        mix = sv["mix"]
        gpost = row(mix_post_g[i])
        if i % 2 == 0:
            dx, dm, dz, dgpost, dgpre, dlng, dlnb, dws, dbt = _a_bwd(
                dx1, mix["m"], sv["x"], mix["gp"], mix["u"], mix["vh"], mix["rs"], row(mix_pre_g[i]), wm, lay, j,
                row(a_ln_g[j]), row(a_ln_b[j]), a_w_s[j], bst[j], gpost, token, tm_abwd, f"a_bwd_{j}")
            small_g["a_ln_g"][j], small_g["a_ln_b"][j] = dlng[0], dlnb[0]
            small_g["a_w_s"][j], small_g["a_b_s"][j] = dws, dbt[:, :A_GROUPS].T
            small_g["mix_post_g"][i], small_g["mix_pre_g"][i] = dgpost[0], dgpre[0]
            order = None
            if i == 0:
                side_x, small_x, order = small_exchanges()
            gbuf = _grad_into(lay.a_total, dz, mix["h1"], lay.a_in[j], lay.a_in_rows, f"g_a_in_{j}", after=order)
            gbuf = _grad_into(gbuf, mix["gated"], dm, lay.a_out[j], lay.a_out_rows, f"g_a_out_{j}")
        else:
            dx, dm, draw, dp, dgpost, dgpre, dsc = _b_bwd(
                dx1, mix["m"], sv["x"], mix["pooled"], row(mix_pre_g[i]), wm, lay, j, wgrp_full[j], scale_full[j],
                gpost, token, tm_b, f"b_bwd_{j}")
            dscale[j] = dsc[0]
            dgrp[j] = _grad_grouped(mix["pooled"], draw, f"g_b_grp_{j}")
            gbuf = _grad_into(lay.b_total, mix["h1"], dp, lay.b_in[j], lay.b_rows, f"g_b_in_{j}")
            gbuf = _grad_into(gbuf, mix["mixed"], dm, lay.b_out[j], lay.b_rows, f"g_b_out_{j}")
            small_g["mix_post_g"][i], small_g["mix_pre_g"][i] = dgpost[0], dgpre[0]
        token = scatter([2 * i], [gbuf]) if i == 0 else scatter([2 * i + 1, 2 * i], [gbuf_f, gbuf])
        dy = dx
    grad_x = dy[None]

    g_sub = [None] * nsub

    parts = [None] * nsub

    def arrived(k, after):
        ss, rs, src, zone = pending[k]
        (own,), (got,) = _exchange_wait(True, ss, rs, src, zone, after, f"scatter_wait_{k}")
        parts[k] = (own, got)
        if k % 4 == 0:
            g_sub[k] = _sum_parts(own, got, me1, f"sum_grads_{k}")

    def fused_update(k, subs, off, rows):
        back = turn if k in turned else (lambda a: a)
        wk, mk, vk = state[k] if k in turned else (w[k], mom[k], var[k])
        out = _sum_adamw([parts[s][0] for s in subs], [parts[s][1] for s in subs], me1, off, rows, wk, mk, vk,
                         f"update_{k}")
        grads[k], delta[k], new_m[k], new_v[k] = (back(a) for a in out)

    def rows_of(k, off, n):
        return g_sub[k][off:off + n]

    grads, delta, new_m, new_v = {}, {}, {}, {}

    def update(k):
        shape = w[k].shape
        two = lambda a: a.reshape(-1, shape[-1])
        dl, nm, nv = _adamw(two(w[k]), two(grads[k]), two(mom[k]), two(var[k]), f"adamw_{k}")
        delta[k], new_m[k], new_v[k] = dl.reshape(shape), nm.reshape(shape), nv.reshape(shape)

    for k in range(1, nsub):
        arrived(k, token)
    ffn_subs = [2 * l + 1 for l in range(DEPTH)]
    fused_update("ffn_w_gate", ffn_subs, lay.gate[0], ffn_local)
    fused_update("ffn_w_up", ffn_subs, lay.up[0], ffn_local)
    fused_update("ffn_w_down", ffn_subs, lay.down[0], ffn_local)
    fused_update("b_w_in", [2, 6], lay.b_in[0], lay.b_rows)
    fused_update("b_w_out", [2, 6], lay.b_out[0], lay.b_rows)
    early = ("ffn_w_gate", "ffn_w_up", "ffn_w_down", "b_w_in", "b_w_out")

    (side_own,), (side_got,) = _exchange_wait(True, *side_x, [delta[k] for k in early], "side_scatter_wait")
    g_side = _sum_parts(side_own, side_got, me1, "sum_side")
    grads["b_w_grp"] = g_side[:side_rows].reshape(b_w_grp.shape)
    grads["b_scale"] = g_side[side_rows:side_rows + 2, :sdev]
    update("b_w_grp")
    update("b_scale")
    _, (small_all,) = _exchange_wait(False, *small_x, [delta["b_w_grp"], delta["b_scale"]], "small_gather_wait")
    small_sum = _sum_devices(small_all.reshape(N_DEV, -1, d), "sum_small")
    g_small = _unpack_small(small_sum, w)
    loss = small_sum[sum(w[k].size for k in SMALL) // d, 0]
    grads.update(g_small)
    dl, nm, nv = _adamw(state["small"][0], _pack_small(g_small, d), state["small"][1], state["small"][2],
                        "adamw_small")
    delta.update(_unpack_small(dl, w))
    new_m.update(_unpack_small(nm, w))
    new_v.update(_unpack_small(nv, w))

    arrived(0, dl)
    grads["a_w_in"] = jnp.stack([rows_of(4 * j, lay.a_in[j], lay.a_in_rows).T for j in range(2)])
    update("a_w_in")
    fused_update("a_w_out", [0, 4], lay.a_out[0], lay.a_out_rows)

    return (loss, grad_x, *[grads[k] for k in names], *[delta[k] for k in names], *[new_m[k] for k in names],
            *[new_v[k] for k in names])
```

```python
import math

import jax
import jax.numpy as jnp
from jax import lax
from jax.experimental import pallas as pl
from jax.experimental.pallas import tpu as pltpu

F32 = jnp.float32
BF16 = jnp.bfloat16
MESH = pl.DeviceIdType.MESH
ANY = pl.BlockSpec(memory_space=pl.ANY)

N_DEV = 8
EPS = 1e-6
CHUNK = 128
A_GROUPS = 8
A_GROUP_DIM = 256
B_WINDOWS = (2, 4, 8, 16)
B_GROUP_DIM = 256
HALO = 16
DEPTH = 4

ADAM_LR = 0.001
ADAM_B1 = 0.9
ADAM_B2 = 0.999
ADAM_EPS = 1e-08
ADAM_WD = 0.01
ADAM_STEP = 10

VMEM_LIMIT_BYTES = 60 * 1024 * 1024

INV_SQRT2 = 1.0 / math.sqrt(2.0)
LOG2_E = 1.0 / math.log(2.0)
INV_SQRT_2PI = 1.0 / math.sqrt(2.0 * math.pi)


def _call(body, **kw):
    return pl.pallas_call(body, **kw)


def _params(*semantics):
    return pltpu.CompilerParams(dimension_semantics=semantics or None, vmem_limit_bytes=VMEM_LIMIT_BYTES)


def _resident(shape, index):
    return pl.BlockSpec(shape, lambda *_: index, pipeline_mode=pl.Buffered(1))


def _rows(tm, width):
    return pl.BlockSpec((tm, width), lambda i: (i, 0))


def _nn(a, b):
    return jnp.dot(a, b, preferred_element_type=F32)


def _nt(a, b):
    return lax.dot_general(a, b, (((1,), (1,)), ((), ())), preferred_element_type=F32)


def _tn(a, b):
    return lax.dot_general(a, b, (((0,), (0,)), ((), ())), preferred_element_type=F32)


def _rms_fwd(x, g):
    r = lax.rsqrt(jnp.mean(x * x, axis=-1, keepdims=True) + EPS)
    return x * r * g


def _rms_bwd(x, g, dy):
    r = lax.rsqrt(jnp.mean(x * x, axis=-1, keepdims=True) + EPS)
    xh = x * r
    dg = jnp.sum(dy * xh, axis=0, keepdims=True)
    dxh = dy * g
    dx = r * (dxh - xh * jnp.mean(dxh * xh, axis=-1, keepdims=True))
    return dx, dg


SLAB = 16


def _slabs(n):
    return [slice(r, r + SLAB) for r in range(0, n, SLAB)]


def _rms_bwd_slabs(x_at, dy_at, g, n, n_sum, emit):
    acc = jnp.zeros((8, g.shape[1]), F32)
    for rows in _slabs(n):
        x = x_at(rows)
        dy = dy_at(rows)
        r = lax.rsqrt(jnp.mean(x * x, axis=-1, keepdims=True) + EPS)
        xh = x * r
        if rows.start < n_sum:
            p = dy * xh
            acc = acc + p[:8] + p[8:]
        dxh = dy * g
        emit(rows, r * (dxh - xh * jnp.mean(dxh * xh, axis=-1, keepdims=True)))
    return jnp.sum(acc, axis=0, keepdims=True)


def _gelu(z):
    phi = 0.5 + 0.5 * lax.erf(z * INV_SQRT2)
    e = jnp.exp2(z * z * (-0.5 * LOG2_E))
    return z * phi, phi + z * e * INV_SQRT_2PI


def _layernorm_stats(v):
    mu = jnp.mean(v, axis=-1, keepdims=True)
    xc = v - mu
    rs = lax.rsqrt(jnp.mean(xc * xc, axis=-1, keepdims=True) + EPS)
    return xc * rs, rs


def _tril_mask():
    r = lax.broadcasted_iota(jnp.int32, (CHUNK, CHUNK), 0)
    c = lax.broadcasted_iota(jnp.int32, (CHUNK, CHUNK), 1)
    return r >= c


class _Layout:
    def __init__(self, d, ffn_rows):
        self.ffn_rows = ffn_rows
        self.gate, self.up, self.down = [0] * DEPTH, [self.ffn_rows] * DEPTH, [2 * self.ffn_rows] * DEPTH
        self.f_total = 3 * self.ffn_rows
        self.a_in_rows, self.a_out_rows, self.b_rows = 4 * d // N_DEV, 2 * d // N_DEV, d // N_DEV
        self.a_in, self.a_out = [0, 0], [self.a_in_rows] * 2
        self.a_total = self.a_in_rows + self.a_out_rows
        self.b_in, self.b_out = [0, 0], [self.b_rows] * 2
        self.b_total = 2 * self.b_rows


def _wspec(rows, d):
    return _resident((N_DEV * rows, d), (0, 0))


def _a_fwd(x, gpre, wg, lay, j, lng, lnb, ws, bst, gpost, tm, name):
    t, d = x.shape
    aw = 2 * d
    nch = tm // CHUNK

    def body(x_ref, gpre_ref, win_ref, lng_ref, lnb_ref, ws_ref, bst_ref, wout_ref, gpost_ref,
             x1_ref, h1_ref, gp_ref, u_ref, vh_ref, rs_ref, gated_ref, m_ref):
        xv = x_ref[...]
        h1 = _rms_fwd(xv, gpre_ref[...]).astype(BF16)
        h1_ref[...] = h1
        z = _nt(h1, win_ref[...])
        u, du_dz = _gelu(z[:, :aw])
        v, dv_dz = _gelu(z[:, aw:])
        gp_ref[:, :aw] = du_dz.astype(BF16)
        gp_ref[:, aw:] = dv_dz.astype(BF16)
        u_ref[...] = u.astype(BF16)
        vh, rs = _layernorm_stats(v)
        vh_ref[...] = vh.astype(BF16)
        rs_ref[...] = jnp.broadcast_to(rs, rs_ref.shape)
        vn = (vh * lng_ref[...] + lnb_ref[...]).astype(BF16)
        mask = _tril_mask()
        for g in range(A_GROUPS):
            wm = jnp.where(mask, ws_ref[g], 0.0).astype(BF16)
            cols = slice(g * A_GROUP_DIM, (g + 1) * A_GROUP_DIM)
            for c in range(nch):
                rows = slice(c * CHUNK, (c + 1) * CHUNK)
                sv = _nn(wm, vn[rows, cols]) + bst_ref[:, g:g + 1]
                gated_ref[rows, cols] = (u[rows, cols] * sv).astype(BF16)
        m = _nn(gated_ref[...], wout_ref[...])
        m_ref[...] = m
        x1_ref[...] = xv + _rms_fwd(m, gpost_ref[...])

    vec = lambda w: _resident((1, w), (0, 0))
    return _call(
        body, name=name, grid=(t // tm,),
        in_specs=[_rows(tm, d), vec(d), _wspec(lay.a_in_rows, d), vec(aw), vec(aw),
                  _resident((A_GROUPS, CHUNK, CHUNK), (0, 0, 0)), _resident((CHUNK, A_GROUPS), (0, 0)),
                  _wspec(lay.a_out_rows, d), vec(d)],
        out_specs=[_rows(tm, d), _rows(tm, d), _rows(tm, 2 * aw), _rows(tm, aw), _rows(tm, aw), _rows(tm, 128),
                   _rows(tm, aw), _rows(tm, d)],
        out_shape=[jax.ShapeDtypeStruct((t, d), F32), jax.ShapeDtypeStruct((t, d), BF16),
                   jax.ShapeDtypeStruct((t, 2 * aw), BF16), jax.ShapeDtypeStruct((t, aw), BF16),
                   jax.ShapeDtypeStruct((t, aw), BF16), jax.ShapeDtypeStruct((t, 128), F32),
                   jax.ShapeDtypeStruct((t, aw), BF16), jax.ShapeDtypeStruct((t, d), F32)],
        compiler_params=_params("parallel"),
    )(x, gpre, wg[0], lng, lnb, ws, bst, wg[1], gpost)


def _a_bwd(dx1, m, x, gp, u, vh, rs, gpre, wg, lay, j, lng, lnb, ws, bst, gpost, after, tm, name):
    t, d = x.shape
    aw = 2 * d
    nch = tm // CHUNK

    def body(dx1_ref, m_ref, x_ref, gp_ref, u_ref, vh_ref, rs_ref, gpre_ref, win_ref, lng_ref, lnb_ref, ws_ref, bst_ref,
             wout_ref, gpost_ref, after_ref,
             dx_ref, dm_ref, dz_ref, dgpost_ref, dgpre_ref, dlng_ref, dlnb_ref, dws_ref, dbt_ref, dvn_ref):
        @pl.when(pl.program_id(0) == 0)
        def _():
            for r in (dgpost_ref, dgpre_ref, dlng_ref, dlnb_ref, dws_ref, dbt_ref):
                r[...] = jnp.zeros_like(r)

        def put_dm(rows, dx):
            dm_ref[rows, :] = dx.astype(BF16)

        dgpost_ref[...] += _rms_bwd_slabs(lambda rows: m_ref[rows, :], lambda rows: dx1_ref[rows, :], gpost_ref[...],
                                          tm, tm, put_dm)
        dgated = _nt(dm_ref[...], wout_ref[...])

        vh = vh_ref[...].astype(F32)
        rs = rs_ref[:, :1]
        lng_v = lng_ref[...]
        vn = (vh * lng_v + lnb_ref[...]).astype(BF16)
        mask = _tril_mask()
        lane = lax.broadcasted_iota(jnp.int32, (CHUNK, CHUNK), 1)
        for g in range(A_GROUPS):
            wm = jnp.where(mask, ws_ref[g], 0.0).astype(BF16)
            cols = slice(g * A_GROUP_DIM, (g + 1) * A_GROUP_DIM)
            dws_g = jnp.zeros((CHUNK, CHUNK), F32)
            db_g = jnp.zeros((CHUNK, 1), F32)
            for c in range(nch):
                rows = slice(c * CHUNK, (c + 1) * CHUNK)
                vn_cg = vn[rows, cols]
                sv = _nn(wm, vn_cg) + bst_ref[:, g:g + 1]
                dg_cg = dgated[rows, cols]
                dsv = dg_cg * u_ref[rows, cols].astype(F32)
                dsv_bf = dsv.astype(BF16)
                db_g = db_g + jnp.sum(dsv, axis=1, keepdims=True)
                dws_g = dws_g + _nt(dsv_bf, vn_cg)
                dvn_ref[rows, cols] = _tn(wm, dsv_bf)
                dz_ref[rows, cols] = (dg_cg * sv * gp_ref[rows, cols].astype(F32)).astype(BF16)
            dws_ref[g] += jnp.where(mask, dws_g, 0.0)
            dbt_ref[...] += jnp.where(lane == g, db_g, 0.0)
        dvn = dvn_ref[...]
        dlng_ref[...] += jnp.sum(dvn * vh, axis=0, keepdims=True)
        dlnb_ref[...] += jnp.sum(dvn, axis=0, keepdims=True)
        dvh = dvn * lng_v
        dv = rs * (dvh - jnp.mean(dvh, axis=-1, keepdims=True) - vh * jnp.mean(dvh * vh, axis=-1, keepdims=True))
        dz_ref[:, aw:] = (dv * gp_ref[:, aw:].astype(F32)).astype(BF16)
        dh1 = _nn(dz_ref[...], win_ref[...])

        def put_dx(rows, dx):
            dx_ref[rows, :] = dx1_ref[rows, :] + dx

        dgpre_ref[...] += _rms_bwd_slabs(lambda rows: x_ref[rows, :], lambda rows: dh1[rows, :], gpre_ref[...],
                                         tm, tm, put_dx)

    vec = lambda w: _resident((1, w), (0, 0))
    acc = lambda shape: pl.BlockSpec(shape, lambda i: (0,) * len(shape))
    return _call(
        body, name=name, grid=(t // tm,),
        in_specs=[_rows(tm, d), _rows(tm, d), _rows(tm, d), _rows(tm, 2 * aw), _rows(tm, aw), _rows(tm, aw),
                  _rows(tm, 128), vec(d), _wspec(lay.a_in_rows, d), vec(aw), vec(aw),
                  _resident((A_GROUPS, CHUNK, CHUNK), (0, 0, 0)), _resident((CHUNK, A_GROUPS), (0, 0)),
                  _wspec(lay.a_out_rows, d), vec(d), ANY],
        out_specs=[_rows(tm, d), _rows(tm, d), _rows(tm, 2 * aw), acc((1, d)), acc((1, d)), acc((1, aw)), acc((1, aw)),
                   acc((A_GROUPS, CHUNK, CHUNK)), acc((CHUNK, CHUNK))],
        out_shape=[jax.ShapeDtypeStruct((t, d), F32), jax.ShapeDtypeStruct((t, d), BF16),
                   jax.ShapeDtypeStruct((t, 2 * aw), BF16), jax.ShapeDtypeStruct((1, d), F32),
                   jax.ShapeDtypeStruct((1, d), F32), jax.ShapeDtypeStruct((1, aw), F32),
                   jax.ShapeDtypeStruct((1, aw), F32), jax.ShapeDtypeStruct((A_GROUPS, CHUNK, CHUNK), F32),
                   jax.ShapeDtypeStruct((CHUNK, CHUNK), F32)],
        scratch_shapes=[pltpu.VMEM((tm, aw), F32)],
        compiler_params=_params("arbitrary"),
    )(dx1, m, x, gp, u, vh, rs, gpre, wg[0], lng, lnb, ws, bst, wg[1], gpost, after)


def _window_counts(first_row, n, win):
    tpos = first_row + lax.broadcasted_iota(jnp.int32, (n, 1), 0)
    return jnp.clip(tpos + 1, 1, win).astype(F32)


def _b_fwd(x, gpre, wg, lay, j, wgrp, scale, gpost, tm, name):
    t, d = x.shape
    n = tm + HALO
    ngrp = len(B_WINDOWS)

    def body(x_ref, xprev_ref, gpre_ref, win_ref, wgrp_ref, scale_ref, wout_ref, gpost_ref,
             x1_ref, h1_ref, pooled_ref, mixed_ref, m_ref):
        i = pl.program_id(0)
        xv = x_ref[...]
        keep = jnp.where(i > 0, 1.0, 0.0)
        xe = jnp.concatenate([xprev_ref[...] * keep, xv], axis=0)
        h1e = _rms_fwd(xe, gpre_ref[...]).astype(BF16)
        h1_ref[...] = h1e[HALO:]
        p = _nn(h1e, win_ref[...])
        acc = p
        shift = 1
        for g, win in enumerate(B_WINDOWS):
            lo = g * B_GROUP_DIM
            if g > 0:
                acc = acc[:, B_GROUP_DIM:]
            while shift < win:
                acc = acc + pltpu.roll(acc, shift, 0)
                shift *= 2
            cnt = _window_counts(i * tm - HALO, n, win)
            pooled = acc[:, :B_GROUP_DIM] / cnt - p[:, lo:lo + B_GROUP_DIM]
            pooled_ref[:, lo:lo + B_GROUP_DIM] = pooled[HALO:].astype(BF16)
        for g in range(ngrp):
            cols = slice(g * B_GROUP_DIM, (g + 1) * B_GROUP_DIM)
            raw = _nn(pooled_ref[:, cols], wgrp_ref[g])
            mixed_ref[:, cols] = (raw * scale_ref[:, cols]).astype(BF16)
        m = _nn(mixed_ref[...], wout_ref[...])
        m_ref[...] = m
        x1_ref[...] = xv + _rms_fwd(m, gpost_ref[...])

    vec = lambda w: _resident((1, w), (0, 0))
    per = tm // HALO
    return _call(
        body, name=name, grid=(t // tm,),
        in_specs=[_rows(tm, d), pl.BlockSpec((HALO, d), lambda i: (jnp.maximum(i * per - 1, 0), 0)), vec(d),
                  _wspec(lay.b_rows, d), _resident((ngrp, B_GROUP_DIM, B_GROUP_DIM), (0, 0, 0)), vec(d),
                  _wspec(lay.b_rows, d), vec(d)],
        out_specs=[_rows(tm, d)] * 5,
        out_shape=[jax.ShapeDtypeStruct((t, d), F32), jax.ShapeDtypeStruct((t, d), BF16),
                   jax.ShapeDtypeStruct((t, d), BF16), jax.ShapeDtypeStruct((t, d), BF16),
                   jax.ShapeDtypeStruct((t, d), F32)],
        compiler_params=_params("parallel"),
    )(x, x, gpre, wg[0], wgrp, scale, wg[1], gpost)


def _b_bwd(dx1, m, x, pooled, gpre, wg, lay, j, wgrp, scale, gpost, after, tm, name):
    t, d = x.shape
    n = tm + HALO
    ngrp = len(B_WINDOWS)
    steps = t // tm

    def body(dx1_ref, dx1n_ref, m_ref, mn_ref, x_ref, pooled_ref, pooledn_ref, gpre_ref, win_ref, wgrp_ref, scale_ref,
             wout_ref, gpost_ref, after_ref,
             dx_ref, dm_ref, draw_ref, dp_ref, dgpost_ref, dgpre_ref, dscale_ref, dpool_ref):
        i = pl.program_id(0)

        @pl.when(i == 0)
        def _():
            for r in (dgpost_ref, dgpre_ref, dscale_ref):
                r[...] = jnp.zeros_like(r)

        keep = jnp.where(i < steps - 1, 1.0, 0.0)
        dy = dx1_ref[...]
        dye = jnp.concatenate([dy, dx1n_ref[...] * keep], axis=0)
        me = jnp.concatenate([m_ref[...], mn_ref[...]], axis=0)
        gpost_v = gpost_ref[...]
        r = lax.rsqrt(jnp.mean(me * me, axis=-1, keepdims=True) + EPS)
        mh = me * r
        dgpost_ref[...] += jnp.sum((dye * mh)[:tm], axis=0, keepdims=True)
        dmh = dye * gpost_v
        dme = (r * (dmh - mh * jnp.mean(dmh * mh, axis=-1, keepdims=True))).astype(BF16)
        dm_ref[...] = dme[:tm]
        dmixed = _nt(dme, wout_ref[...])
        pooled_e = jnp.concatenate([pooled_ref[...], pooledn_ref[...]], axis=0)
        scale_v = scale_ref[...]
        for g, win in enumerate(B_WINDOWS):
            cols = slice(g * B_GROUP_DIM, (g + 1) * B_GROUP_DIM)
            raw = _nn(pooled_e[:, cols], wgrp_ref[g])
            dscale_ref[:, cols] += jnp.sum((dmixed[:, cols] * raw)[:tm], axis=0, keepdims=True)
            draw = (dmixed[:, cols] * scale_v[:, cols]).astype(BF16)
            draw_ref[:, cols] = draw[:tm]
            dpool = _nt(draw, wgrp_ref[g])
            acc = dpool / _window_counts(i * tm, n, win)
            shift = 1
            while shift < win:
                acc = acc + pltpu.roll(acc, n - shift, 0)
                shift *= 2
            dpool_ref[:, cols] = (acc - dpool)[:tm]
        dp = dpool_ref[...].astype(BF16)
        dp_ref[...] = dp
        dh1 = _nt(dp, win_ref[...])
        dxp, dgpre = _rms_bwd(x_ref[...], gpre_ref[...], dh1)
        dgpre_ref[...] += dgpre
        dx_ref[...] = dy + dxp

    vec = lambda w: _resident((1, w), (0, 0))
    acc = lambda shape: pl.BlockSpec(shape, lambda i: (0,) * len(shape))
    per = tm // HALO
    nxt = lambda i: (jnp.minimum((i + 1) * per, t // HALO - 1), 0)
    return _call(
        body, name=name, grid=(steps,),
        in_specs=[_rows(tm, d), pl.BlockSpec((HALO, d), nxt), _rows(tm, d), pl.BlockSpec((HALO, d), nxt), _rows(tm, d),
                  _rows(tm, d), pl.BlockSpec((HALO, d), nxt), vec(d), _wspec(lay.b_rows, d),
                  _resident((ngrp, B_GROUP_DIM, B_GROUP_DIM), (0, 0, 0)), vec(d), _wspec(lay.b_rows, d),
                  vec(d), ANY],
        out_specs=[_rows(tm, d)] * 4 + [acc((1, d))] * 3,
        out_shape=[jax.ShapeDtypeStruct((t, d), F32), jax.ShapeDtypeStruct((t, d), BF16),
                   jax.ShapeDtypeStruct((t, d), BF16), jax.ShapeDtypeStruct((t, d), BF16)]
                  + [jax.ShapeDtypeStruct((1, d), F32)] * 3,
        scratch_shapes=[pltpu.VMEM((tm, d), F32)],
        compiler_params=_params("arbitrary"),
    )(dx1, dx1, m, m, x, pooled, pooled, gpre, wg[0], wgrp, scale, wg[1], gpost, after)


def _f_fwd(x1, gpre, wg, lay, l, gpost, tm, name, target=None):
    t, d = x1.shape
    hid = N_DEV * lay.ffn_rows
    head = target is not None

    def body(x_ref, gpre_ref, wgate_ref, wup_ref, wdown_ref, gpost_ref, *rest):
        x2_ref, h2_ref, abs_ref, f_ref = rest[-5:-1] if head else rest
        xv = x_ref[...]
        h2 = _rms_fwd(xv, gpre_ref[...]).astype(BF16)
        h2_ref[...] = h2
        a = _nt(h2, wgate_ref[...])
        b = _nt(h2, wup_ref[...])
        sig = jax.nn.sigmoid(a)
        silu = a * sig
        abs_ref[:, :hid] = (b * (sig + silu * (1.0 - sig))).astype(BF16)
        abs_ref[:, hid:2 * hid] = silu.astype(BF16)
        s = (silu * b).astype(BF16)
        abs_ref[:, 2 * hid:] = s
        f = _nn(s, wdown_ref[...])
        f_ref[...] = f
        x2 = xv + _rms_fwd(f, gpost_ref[...])
        if head:
            target_ref, loss_ref = rest[0], rest[-1]

            @pl.when(pl.program_id(0) == 0)
            def _():
                loss_ref[...] = jnp.zeros_like(loss_ref)

            diff = x2 - target_ref[...]
            x2_ref[...] = diff * (1.0 / d)
            sq = jnp.sum(jnp.sum(diff * diff, axis=0, keepdims=True), axis=1, keepdims=True)
            loss_ref[...] += sq * (0.5 / d)
        else:
            x2_ref[...] = x2

    vec = lambda w: _resident((1, w), (0, 0))
    return _call(
        body, name=name, grid=(t // tm,),
        in_specs=[_rows(tm, d), vec(d), _wspec(lay.ffn_rows, d), _wspec(lay.ffn_rows, d),
                  _wspec(lay.ffn_rows, d), vec(d)] + ([_rows(tm, d)] if head else []),
        out_specs=[_rows(tm, d), _rows(tm, d), _rows(tm, 3 * hid), _rows(tm, d)]
                  + ([pl.BlockSpec((8, 128), lambda i: (0, 0))] if head else []),
        out_shape=[jax.ShapeDtypeStruct((t, d), F32), jax.ShapeDtypeStruct((t, d), BF16),
                   jax.ShapeDtypeStruct((t, 3 * hid), BF16), jax.ShapeDtypeStruct((t, d), F32)]
                  + ([jax.ShapeDtypeStruct((8, 128), F32)] if head else []),
        compiler_params=_params("arbitrary" if head else "parallel"),
    )(x1, gpre, wg[0], wg[1], wg[2], gpost, *([target] if head else []))


def _f_bwd(dx2, f, x1, acts, gpre, wg, lay, l, gpost, after, tm, name):
    t, d = x1.shape
    hid = N_DEV * lay.ffn_rows

    def body(dx2_ref, f_ref, x_ref, ab_ref, gpre_ref, wgate_ref, wup_ref, wdown_ref, gpost_ref, after_ref,
             dx1_ref, df_ref, dab_ref, dgpost_ref, dgpre_ref):
        @pl.when(pl.program_id(0) == 0)
        def _():
            dgpost_ref[...] = jnp.zeros_like(dgpost_ref)
            dgpre_ref[...] = jnp.zeros_like(dgpre_ref)

        def put_df(rows, dx):
            df_ref[rows, :] = dx.astype(BF16)

        dgpost_ref[...] += _rms_bwd_slabs(lambda rows: f_ref[rows, :], lambda rows: dx2_ref[rows, :], gpost_ref[...],
                                          tm, tm, put_df)
        ds = _nt(df_ref[...], wdown_ref[...])
        dab_ref[:, :hid] = (ds * ab_ref[:, :hid].astype(F32)).astype(BF16)
        dab_ref[:, hid:] = (ds * ab_ref[:, hid:].astype(F32)).astype(BF16)
        dh2 = _nn(dab_ref[:, :hid], wgate_ref[...]) + _nn(dab_ref[:, hid:], wup_ref[...])

        def put_dx(rows, dx):
            dx1_ref[rows, :] = dx2_ref[rows, :] + dx

        dgpre_ref[...] += _rms_bwd_slabs(lambda rows: x_ref[rows, :], lambda rows: dh2[rows, :], gpre_ref[...],
                                         tm, tm, put_dx)

    vec = lambda w: _resident((1, w), (0, 0))
    acc = pl.BlockSpec((1, d), lambda i: (0, 0))
    return _call(
        body, name=name, grid=(t // tm,),
        in_specs=[_rows(tm, d), _rows(tm, d), _rows(tm, d), _rows(tm, 2 * hid), vec(d),
                  _wspec(lay.ffn_rows, d), _wspec(lay.ffn_rows, d),
                  _wspec(lay.ffn_rows, d), vec(d), ANY],
        out_specs=[_rows(tm, d), _rows(tm, d), _rows(tm, 2 * hid), acc, acc],
        out_shape=[jax.ShapeDtypeStruct((t, d), F32), jax.ShapeDtypeStruct((t, d), BF16),
                   jax.ShapeDtypeStruct((t, 3 * hid), BF16),
                   jax.ShapeDtypeStruct((1, d), F32), jax.ShapeDtypeStruct((1, d), F32)],
        input_output_aliases={3: 2},
        compiler_params=_params("arbitrary"),
    )(dx2, f, x1, acts, gpre, wg[0], wg[1], wg[2], gpost, after)


def _grad_into(gbuf, lhs, rhs, off, rows, name, after=None):
    t, m = lhs.shape
    d = rhs.shape[1]
    assert m == N_DEV * rows and off % rows == 0
    per_tile = {512: 2, 256: 4}[rows]
    tm = per_tile * rows
    assert tm % 128 == 0 and rows % 16 == 0
    tk = 2048 if t % 2048 == 0 else 256
    ksteps = t // tk
    fresh = isinstance(gbuf, int)
    shape = (N_DEV, gbuf, d) if fresh else gbuf.shape
    extra = ([] if fresh else [gbuf]) + ([] if after is None else [after])

    def body(l_ref, r_ref, *rest):
        o_ref, acc_ref = rest[-2:]
        k = pl.program_id(1)

        @pl.when(k == 0)
        def _():
            acc_ref[...] = jnp.zeros_like(acc_ref)

        acc_ref[...] += _tn(l_ref[...], r_ref[...])

        @pl.when(k == ksteps - 1)
        def _():
            o_ref[...] = acc_ref[...].reshape(per_tile, rows, d).astype(BF16)

    return _call(
        body, name=name, grid=(N_DEV // per_tile, ksteps),
        in_specs=[pl.BlockSpec((tk, tm), lambda i, k: (k, i)), pl.BlockSpec((tk, d), lambda i, k: (k, 0))]
                 + [ANY] * len(extra),
        out_specs=pl.BlockSpec((per_tile, rows, d), lambda i, k: (i, off // rows, 0)),
        out_shape=jax.ShapeDtypeStruct(shape, BF16),
        scratch_shapes=[pltpu.VMEM((tm, d), F32)],
        input_output_aliases={} if fresh else {2: 0},
        compiler_params=_params("parallel", "arbitrary"),
    )(lhs, rhs, *extra)


def _grad_pool_mixer(h1, dp, mixed, dm, pooled, draw, rows, name):
    t, d = h1.shape
    ngrp = len(B_WINDOWS)
    assert d == N_DEV * rows and d == ngrp * B_GROUP_DIM
    tk = 1024 if t % 1024 == 0 else 256
    ksteps = t // tk

    def body(h1_ref, dp_ref, mixed_ref, dm_ref, pooled_ref, draw_ref, o_ref, grp_ref, acc_in, acc_out):
        k = pl.program_id(0)

        @pl.when(k == 0)
        def _():
            acc_in[...] = jnp.zeros_like(acc_in)
            acc_out[...] = jnp.zeros_like(acc_out)
            grp_ref[...] = jnp.zeros_like(grp_ref)

        acc_in[...] += _tn(h1_ref[...], dp_ref[...])
        acc_out[...] += _tn(mixed_ref[...], dm_ref[...])
        for g in range(ngrp):
            cols = slice(g * B_GROUP_DIM, (g + 1) * B_GROUP_DIM)
            grp_ref[g] += _tn(pooled_ref[:, cols], draw_ref[:, cols])

        @pl.when(k == ksteps - 1)
        def _():
            o_ref[:, :rows, :] = acc_in[...].reshape(N_DEV, rows, d).astype(BF16)
            o_ref[:, rows:, :] = acc_out[...].reshape(N_DEV, rows, d).astype(BF16)

    return _call(
        body, name=name, grid=(ksteps,),
        in_specs=[_rows(tk, d)] * 6,
        out_specs=[pl.BlockSpec((N_DEV, 2 * rows, d), lambda k: (0, 0, 0)),
                   pl.BlockSpec((ngrp, B_GROUP_DIM, B_GROUP_DIM), lambda k: (0, 0, 0))],
        out_shape=[jax.ShapeDtypeStruct((N_DEV, 2 * rows, d), BF16),
                   jax.ShapeDtypeStruct((ngrp, B_GROUP_DIM, B_GROUP_DIM), F32)],
        scratch_shapes=[pltpu.VMEM((d, d), F32), pltpu.VMEM((d, d), F32)],
        compiler_params=_params("arbitrary"),
    )(h1, dp, mixed, dm, pooled, draw)


def _grad_ffn(acts, h2, df, rows, name):
    t, d = h2.shape
    per_tile = 4
    tm = per_tile * rows
    tiles = N_DEV // per_tile
    assert acts.shape[1] == 3 * N_DEV * rows and tm % 128 == 0 and rows % 16 == 0
    tk = 2048 if t % 2048 == 0 else 256
    ksteps = t // tk

    def body(l_ref, h2_ref, df_ref, o_ref, acc_ref):
        i, k = pl.program_id(0), pl.program_id(1)

        @pl.when(k == 0)
        def _():
            acc_ref[...] = jnp.zeros_like(acc_ref)

        @pl.when(i < 2 * tiles)
        def _():
            acc_ref[...] += _tn(l_ref[...], h2_ref[...])

        @pl.when(i >= 2 * tiles)
        def _():
            acc_ref[...] += _tn(l_ref[...], df_ref[...])

        @pl.when(k == ksteps - 1)
        def _():
            o_ref[...] = acc_ref[...].reshape(per_tile, rows, d).astype(BF16)

    return _call(
        body, name=name, grid=(3 * tiles, ksteps),
        in_specs=[pl.BlockSpec((tk, tm), lambda i, k: (k, i)),
                  pl.BlockSpec((tk, d), lambda i, k: (jnp.where(i < 2 * tiles, k, ksteps - 1), 0)),
                  pl.BlockSpec((tk, d), lambda i, k: (jnp.where(i >= 2 * tiles, k, 0), 0))],
        out_specs=pl.BlockSpec((per_tile, rows, d), lambda i, k: (i % tiles, i // tiles, 0)),
        out_shape=jax.ShapeDtypeStruct((N_DEV, 3 * rows, d), BF16),
        scratch_shapes=[pltpu.VMEM((tm, d), F32)],
        compiler_params=_params("arbitrary", "arbitrary"),
    )(acts, h2, df)


def _peers():
    x, y, c = lax.axis_index("x"), lax.axis_index("y"), lax.axis_index("c")
    flip = lambda v, f: 1 - v if f else v
    peers = []
    for r in range(1, N_DEV):
        px, py, pc = flip(x, r & 4), flip(y, r & 2), flip(c, r & 1)
        peers.append(((px, py, pc), 4 * px + 2 * py + pc))
    return 4 * x + 2 * y + c, peers


HBM = pl.BlockSpec(memory_space=pltpu.HBM)
SEM = pl.BlockSpec(memory_space=pltpu.SEMAPHORE)
EFFECT = pltpu.SideEffectType.DATAFLOW_SIDE_EFFECTING


def _peer_copies(scatter, srcs, lands, send_sems, recv_sems):
    me, peers = _peers()
    copies = []
    for a in range(len(srcs)):
        rows = srcs[a].shape[0]
        block = lambda k: lands[a].at[pl.ds(pl.multiple_of(k * rows, 8), rows)]
        for r, (peer, pidx) in enumerate(peers):
            src = srcs[a].at[pidx] if scatter else srcs[a]
            mine = lands[a].at[r] if scatter else block(pidx)
            theirs = lands[a].at[r] if scatter else block(me)
            send = pltpu.make_async_remote_copy(src_ref=src, dst_ref=theirs, send_sem=send_sems[a].at[r],
                                                recv_sem=recv_sems[a].at[r], device_id=peer, device_id_type=MESH)
            recv = pltpu.make_async_remote_copy(src_ref=src, dst_ref=mine, send_sem=send_sems[a].at[r],
                                                recv_sem=recv_sems[a].at[r], device_id=peer, device_id_type=MESH)
            copies.append((send, recv))
    return copies


def _own_copies(srcs, lands, send_sems):
    me, _ = _peers()
    copies = []
    for a in range(len(srcs)):
        rows = srcs[a].shape[0]
        copies.append(pltpu.make_async_copy(srcs[a], lands[a].at[pl.ds(pl.multiple_of(me * rows, 8), rows)],
                                            send_sems[a].at[N_DEV - 1]))
    return copies


def _exchange_start(scatter, srcs, lands, after, name):
    n = len(srcs)

    def body(*refs):
        src_refs, land_refs = refs[:n], refs[n:2 * n]
        outs = refs[2 * n + 1:]
        send_sems, recv_sems, token = outs[:n], outs[n:2 * n], outs[-1]
        for send, _ in _peer_copies(scatter, src_refs, land_refs, send_sems, recv_sems):
            send.start()
        if not scatter:
            for own in _own_copies(src_refs, land_refs, send_sems):
                own.start()
        token[...] = jnp.zeros_like(token)

    hbm = lambda a: pltpu.with_memory_space_constraint(a, pltpu.HBM)
    res = _call(
        body, name=name,
        in_specs=[HBM] * (2 * n) + [ANY],
        out_specs=[SEM] * (2 * n) + [HBM] * (2 * n) + [pl.BlockSpec(memory_space=pltpu.VMEM)],
        out_shape=[pltpu.SemaphoreType.DMA((N_DEV,))] * (2 * n)
                  + [pltpu.HBM(a.shape, a.dtype) for a in list(srcs) + list(lands)]
                  + [jax.ShapeDtypeStruct((8, 128), F32)],
        input_output_aliases={i: 2 * n + i for i in range(2 * n)},
        compiler_params=pltpu.CompilerParams(has_side_effects=EFFECT),
    )(*[hbm(a) for a in srcs], *[hbm(a) for a in lands], after)
    return res[:n], res[n:2 * n], res[2 * n:3 * n], res[3 * n:4 * n], res[-1]


def _chip_peers():
    x, y, c = lax.axis_index("x"), lax.axis_index("y"), lax.axis_index("c")
    far = []
    for px, py in ((1 - x, y), (x, 1 - y), (1 - x, 1 - y)):
        far.append(((px, py, c), 4 * px + 2 * py + c, 4 * px + 2 * py + 1 - c))
    return 4 * x + 2 * y + c, ((x, y, 1 - c), 4 * x + 2 * y + 1 - c), far


def _block(land, rows, k):
    return land.at[pl.ds(pl.multiple_of(k * rows, 8), rows)]


def _gather2_first(srcs, lands, after, name):
    n = len(srcs)

    def body(*refs):
        src_refs, land_refs = refs[:n], refs[n:2 * n]
        outs = refs[2 * n + 1:]
        send, recv_sib, recv_far, token = outs[:n], outs[n:2 * n], outs[2 * n:3 * n], outs[-1]
        me, (sib, _), far = _chip_peers()
        for a in range(n):
            rows = src_refs[a].shape[0]
            mine = _block(land_refs[a], rows, me)
            pltpu.make_async_copy(src_refs[a], mine, send[a].at[4]).start()
            pltpu.make_async_remote_copy(src_ref=src_refs[a], dst_ref=mine, send_sem=send[a].at[0],
                                         recv_sem=recv_sib[a].at[0], device_id=sib, device_id_type=MESH).start()
            for j, (peer, _, _) in enumerate(far):
                pltpu.make_async_remote_copy(src_ref=src_refs[a], dst_ref=mine, send_sem=send[a].at[1 + j],
                                             recv_sem=recv_far[a].at[j], device_id=peer, device_id_type=MESH).start()
        token[...] = jnp.zeros_like(token)

    hbm = lambda a: pltpu.with_memory_space_constraint(a, pltpu.HBM)
    res = _call(
        body, name=name,
        in_specs=[HBM] * (2 * n) + [ANY],
        out_specs=[SEM] * (3 * n) + [HBM] * (2 * n) + [pl.BlockSpec(memory_space=pltpu.VMEM)],
        out_shape=[pltpu.SemaphoreType.DMA((5,))] * n + [pltpu.SemaphoreType.DMA((1,))] * n
                  + [pltpu.SemaphoreType.DMA((3,))] * n
                  + [pltpu.HBM(a.shape, a.dtype) for a in list(srcs) + list(lands)]
                  + [jax.ShapeDtypeStruct((8, 128), F32)],
        input_output_aliases={i: 3 * n + i for i in range(2 * n)},
        compiler_params=pltpu.CompilerParams(has_side_effects=EFFECT),
    )(*[hbm(a) for a in srcs], *[hbm(a) for a in lands], after)
    return res[:n], res[n:2 * n], res[2 * n:3 * n], res[3 * n:4 * n], res[4 * n:5 * n], res[-1]


def _gather2_forward(recv_far, srcs, lands, after, name):
    n = len(lands)
    after = list(after) if isinstance(after, (list, tuple)) else [after]

    def body(*refs):
        src_refs, land_refs, far_sems = refs[:n], refs[n:2 * n], refs[2 * n:3 * n]
        outs = refs[3 * n + len(after):]
        send, recv, token = outs[:n], outs[n:2 * n], outs[-1]
        _, (sib, _), far = _chip_peers()
        for a in range(n):
            rows = src_refs[a].shape[0]
            for j, (peer, pidx, _) in enumerate(far):
                got = _block(land_refs[a], rows, pidx)
                pltpu.make_async_remote_copy(src_ref=src_refs[a], dst_ref=got, send_sem=send[a].at[j],
                                             recv_sem=far_sems[a].at[j], device_id=peer,
                                             device_id_type=MESH).wait_recv()
                pltpu.make_async_remote_copy(src_ref=got, dst_ref=got, send_sem=send[a].at[j], recv_sem=recv[a].at[j],
                                             device_id=sib, device_id_type=MESH).start()
        token[...] = jnp.zeros_like(token)

    res = _call(
        body, name=name,
        in_specs=[HBM] * (2 * n) + [SEM] * n + [ANY] * len(after),
        out_specs=[SEM] * (2 * n) + [HBM] * n + [pl.BlockSpec(memory_space=pltpu.VMEM)],
        out_shape=[pltpu.SemaphoreType.DMA((3,))] * (2 * n) + [pltpu.HBM(a.shape, a.dtype) for a in lands]
                  + [jax.ShapeDtypeStruct((8, 128), F32)],
        input_output_aliases={n + i: 2 * n + i for i in range(n)},
        compiler_params=pltpu.CompilerParams(has_side_effects=EFFECT),
    )(*srcs, *lands, *recv_far, *after)
    return res[:n], res[n:2 * n], res[2 * n:3 * n], res[-1]


def _gather2_wait(send, recv_sib, fwd_send, fwd_recv, srcs, lands, after, name):
    n = len(lands)

    def body(*refs):
        src_refs, land_refs = refs[:n], refs[n:2 * n]
        s_refs, rs_refs, fs_refs, fr_refs = (refs[(2 + q) * n:(3 + q) * n] for q in range(4))
        me, (sib, sib_idx), far = _chip_peers()
        for a in range(n):
            rows = src_refs[a].shape[0]
            mine = _block(land_refs[a], rows, me)
            pltpu.make_async_copy(src_refs[a], mine, s_refs[a].at[4]).wait()
            to_sib = pltpu.make_async_remote_copy(src_ref=src_refs[a], dst_ref=_block(land_refs[a], rows, sib_idx),
                                                  send_sem=s_refs[a].at[0], recv_sem=rs_refs[a].at[0], device_id=sib,
                                                  device_id_type=MESH)
            to_sib.wait_send()
            to_sib.wait_recv()
            for j, (peer, pidx, pair_idx) in enumerate(far):
                pltpu.make_async_remote_copy(src_ref=src_refs[a], dst_ref=mine, send_sem=s_refs[a].at[1 + j],
                                             recv_sem=fr_refs[a].at[j], device_id=peer,
                                             device_id_type=MESH).wait_send()
                fwd = pltpu.make_async_remote_copy(src_ref=_block(land_refs[a], rows, pidx),
                                                   dst_ref=_block(land_refs[a], rows, pair_idx),
                                                   send_sem=fs_refs[a].at[j], recv_sem=fr_refs[a].at[j], device_id=sib,
                                                   device_id_type=MESH)
                fwd.wait_send()
                fwd.wait_recv()

    res = _call(
        body, name=name,
        in_specs=[HBM] * (2 * n) + [SEM] * (4 * n) + [ANY],
        out_specs=[HBM] * (2 * n),
        out_shape=[pltpu.HBM(a.shape, a.dtype) for a in list(srcs) + list(lands)],
        input_output_aliases={i: i for i in range(2 * n)},
        compiler_params=pltpu.CompilerParams(has_side_effects=EFFECT),
    )(*srcs, *lands, *send, *recv_sib, *fwd_send, *fwd_recv, after)
    return res[n:]


def _exchange_wait(scatter, send_sems, recv_sems, srcs, lands, after, name):
    n = len(srcs)
    after = list(after) if isinstance(after, (list, tuple)) else [after]

    def body(*refs):
        src_refs, land_refs = refs[:n], refs[n:2 * n]
        send_refs, recv_refs = refs[2 * n:3 * n], refs[3 * n:4 * n]
        for send, recv in _peer_copies(scatter, src_refs, land_refs, send_refs, recv_refs):
            send.wait_send()
            recv.wait_recv()
        if not scatter:
            for own in _own_copies(src_refs, land_refs, send_refs):
                own.wait()

    res = _call(
        body, name=name,
        in_specs=[HBM] * (2 * n) + [SEM] * (2 * n) + [ANY] * len(after),
        out_specs=[HBM] * (2 * n),
        out_shape=[pltpu.HBM(a.shape, a.dtype) for a in list(srcs) + list(lands)],
        input_output_aliases={i: i for i in range(2 * n)},
        compiler_params=pltpu.CompilerParams(has_side_effects=EFFECT),
    )(*srcs, *lands, *send_sems, *recv_sems, *after)
    return res[:n], res[n:]


def _row_tile(rows):
    if rows <= 512:
        return rows
    return max([tr for tr in range(16, 513, 16) if rows % tr == 0] or [rows])


def _sum_parts(own, got, me, name):
    _, rows, w = own.shape
    tr = _row_tile(rows)

    def body(me_ref, a_ref, b_ref, o_ref):
        s = a_ref[...].astype(F32)
        for j in range(N_DEV - 1):
            s = s + b_ref[j].astype(F32)
        o_ref[...] = s

    return _call(
        body, name=name,
        grid_spec=pltpu.PrefetchScalarGridSpec(
            num_scalar_prefetch=1, grid=(rows // tr,),
            in_specs=[pl.BlockSpec((None, tr, w), lambda i, me_ref: (me_ref[0], i, 0)),
                      pl.BlockSpec((N_DEV - 1, tr, w), lambda i, me_ref: (0, i, 0))],
            out_specs=pl.BlockSpec((tr, w), lambda i, me_ref: (i, 0))),
        out_shape=jax.ShapeDtypeStruct((rows, w), F32),
        compiler_params=_params("parallel"),
    )(me, own, got)


def _sum_devices(stacked, name):
    k, rows, w = stacked.shape
    tr = _row_tile(rows)

    def body(a_ref, o_ref):
        s = a_ref[0]
        for j in range(1, k):
            s = s + a_ref[j]
        o_ref[...] = s

    return _call(
        body, name=name, grid=(rows // tr,),
        in_specs=[pl.BlockSpec((k, tr, w), lambda i: (0, i, 0))],
        out_specs=pl.BlockSpec((tr, w), lambda i: (i, 0)),
        out_shape=jax.ShapeDtypeStruct((rows, w), F32),
        compiler_params=_params("parallel"),
    )(stacked)


def _adamw_math(w, g, m, v):
    nm = ADAM_B1 * m + (1.0 - ADAM_B1) * g
    nv = ADAM_B2 * v + (1.0 - ADAM_B2) * (g * g)
    m_hat = nm / (1.0 - ADAM_B1 ** ADAM_STEP)
    v_hat = nv / (1.0 - ADAM_B2 ** ADAM_STEP)
    return -ADAM_LR * (m_hat / (jnp.sqrt(v_hat) + ADAM_EPS) + ADAM_WD * w), nm, nv


def _sum_adamw(owns, gots, me, off, rows, w, m, v, name):
    nl = len(owns)
    d = w.shape[-1]
    assert off % rows == 0 and w.shape == (nl, rows, d)

    def body(me_ref, *refs):
        own_refs, got_refs = refs[:nl], refs[nl:2 * nl]
        w_ref, m_ref, v_ref, g_ref, d_ref, nm_ref, nv_ref = refs[2 * nl:]
        for l in range(nl):
            @pl.when(pl.program_id(0) == l)
            def _(l=l):
                gv = own_refs[l][...].astype(F32)
                for r in range(N_DEV - 1):
                    gv = gv + got_refs[l][r].astype(F32)
                g_ref[...] = gv
                d_ref[...], nm_ref[...], nv_ref[...] = _adamw_math(w_ref[...], gv, m_ref[...], v_ref[...])

    layer = pl.BlockSpec((None, rows, d), lambda i, me_ref: (i, 0, 0))
    own = pl.BlockSpec((None, rows, d), lambda i, me_ref: (me_ref[0], off // rows, 0), pipeline_mode=pl.Buffered(1))
    got = pl.BlockSpec((N_DEV - 1, rows, d), lambda i, me_ref: (0, off // rows, 0), pipeline_mode=pl.Buffered(1))
    return _call(
        body, name=name,
        grid_spec=pltpu.PrefetchScalarGridSpec(
            num_scalar_prefetch=1, grid=(nl,),
            in_specs=[own] * nl + [got] * nl + [layer] * 3, out_specs=[layer] * 4),
        out_shape=[jax.ShapeDtypeStruct((nl, rows, d), F32)] * 4,
        compiler_params=_params("arbitrary"),
    )(me, *owns, *gots, w, m, v)


def _adamw(w, g, m, v, name):
    rows, cols = w.shape
    tr = _row_tile(rows)

    def body(w_ref, g_ref, m_ref, v_ref, d_ref, nm_ref, nv_ref):
        d_ref[...], nm_ref[...], nv_ref[...] = _adamw_math(w_ref[...], g_ref[...], m_ref[...], v_ref[...])

    spec = pl.BlockSpec((tr, cols), lambda i: (i, 0))
    return _call(
        body, name=name, grid=(rows // tr,),
        in_specs=[spec] * 4, out_specs=[spec] * 3,
        out_shape=[jax.ShapeDtypeStruct((rows, cols), F32)] * 3,
        compiler_params=_params("parallel"),
    )(w, g, m, v)


SMALL = ("a_ln_g", "a_ln_b", "a_w_s", "a_b_s", "mix_pre_g", "mix_post_g", "ffn_pre_g", "ffn_post_g")


def _pack_small(parts, d, last_row=None):
    rows = [parts[k].reshape(-1, d) for k in SMALL] + ([] if last_row is None else [last_row])
    flat = jnp.concatenate(rows, axis=0)
    return jnp.pad(flat, ((0, -flat.shape[0] % 8), (0, 0)))


def _unpack_small(flat, like):
    out, r = {}, 0
    for k in SMALL:
        n = like[k].size // flat.shape[1]
        out[k] = flat[r:r + n].reshape(like[k].shape)
        r += n
    return out


def kernel(x, a_w_in, a_ln_g, a_ln_b, a_w_s, a_b_s, a_w_out, b_w_in, b_w_grp, b_scale, b_w_out, mix_pre_g, mix_post_g, ffn_pre_g, ffn_post_g, ffn_w_gate, ffn_w_up, ffn_w_down, loss_target, m_a_w_in, m_a_ln_g, m_a_ln_b, m_a_w_s, m_a_b_s, m_a_w_out, m_b_w_in, m_b_w_grp, m_b_scale, m_b_w_out, m_mix_pre_g, m_mix_post_g, m_ffn_pre_g, m_ffn_post_g, m_ffn_w_gate, m_ffn_w_up, m_ffn_w_down, v_a_w_in, v_a_ln_g, v_a_ln_b, v_a_w_s, v_a_b_s, v_a_w_out, v_b_w_in, v_b_w_grp, v_b_scale, v_b_w_out, v_mix_pre_g, v_mix_post_g, v_ffn_pre_g, v_ffn_post_g, v_ffn_w_gate, v_ffn_w_up, v_ffn_w_down):
    args = dict(locals())
    names = ("a_w_in", "a_ln_g", "a_ln_b", "a_w_s", "a_b_s", "a_w_out", "b_w_in", "b_w_grp", "b_scale", "b_w_out",
             "mix_pre_g", "mix_post_g", "ffn_pre_g", "ffn_post_g", "ffn_w_gate", "ffn_w_up", "ffn_w_down")
    w = {k: args[k] for k in names}
    mom = {k: args["m_" + k] for k in names}
    var = {k: args["v_" + k] for k in names}

    t, d = x.shape[1], x.shape[2]
    ffn_local = ffn_w_gate.shape[2]
    lay = _Layout(d, ffn_local)
    me = 4 * lax.axis_index("x") + 2 * lax.axis_index("y") + lax.axis_index("c")
    me1 = jnp.reshape(me, (1,)).astype(jnp.int32)

    def landing(block):
        return lax.empty((N_DEV * block.shape[0],) + block.shape[1:], block.dtype)

    def shards(i, mixer, zero):
        j = i // 2
        if not mixer:
            parts = [ffn_w_gate[i].T, ffn_w_up[i].T, ffn_w_down[i]]
        elif i % 2 == 0:
            parts = [a_w_in[j].T, a_w_out[j]]
        else:
            parts = [b_w_in[j], b_w_out[j]]
        return [(p + zero).astype(BF16) for p in parts]

    nsub = 2 * DEPTH
    wg = [None] * nsub
    first = shards(0, True, 0.0)
    f_send, f_sib, f_far, first, f_zones, f_token = _gather2_first(
        first, [landing(b) for b in first], jnp.zeros((8, 128), F32), "gather_first_start")
    zero = f_token[0, 0]
    ngrp = len(B_WINDOWS)
    grp_local = b_w_grp.shape[2]
    sdev = b_scale.shape[1]
    side_rows = 2 * ngrp * grp_local
    side = jnp.concatenate(
        [b_w_grp.reshape(side_rows, B_GROUP_DIM),
         jnp.pad(b_scale, ((0, 6), (0, B_GROUP_DIM - sdev)))], axis=0) + zero
    later, where = [side], [slice(0, 1)]
    for k in range(1, nsub):
        new = shards(k // 2, k % 2 == 0, zero)
        where.append(slice(len(later), len(later) + len(new)))
        later += new
    send_sems, recv_sems, later, zones, token = _exchange_start(
        False, later, [landing(b) for b in later], f_token, "gather_start")
    turned = ("ffn_w_gate", "ffn_w_up")
    turn = lambda a: jnp.swapaxes(a, 1, 2)
    state = {k: tuple(turn(a[k]) for a in (w, mom, var)) for k in turned}
    state["small"] = tuple(_pack_small(a, d) for a in (w, mom, var))
    ready = [a for group in state.values() for a in group]
    fwd_send, fwd_recv, f_zones, fwd_token = _gather2_forward(f_far, first, f_zones, [token] + ready,
                                                              "gather_first_forward")
    wg[0] = _gather2_wait(f_send, f_sib, fwd_send, fwd_recv, first, f_zones, fwd_token, "gather_first_wait")

    def gathered(k, after):
        s = where[k]
        _, got = _exchange_wait(False, send_sems[s], recv_sems[s], later[s], zones[s], after, f"gather_wait_{k}")
        return got

    row = lambda a: a.reshape(1, -1)
    bst = jnp.transpose(a_b_s, (0, 2, 1))

    tm = 256 if t % 256 == 0 else CHUNK
    tm_abwd = tm
    tm_b = 512 if t % 512 == 0 else tm
    tm_f = tm

    saved = []
    h = x[0]
    wgrp_full = scale_full = None
    for i in range(DEPTH):
        j = i // 2
        gpre = row(mix_pre_g[i])
        if i > 0:
            wg[2 * i] = gathered(2 * i, h)
        if i % 2 == 0:
            x1, h1, gp, u, vh, rs, gated, m = _a_fwd(h, gpre, wg[2 * i], lay, j, row(a_ln_g[j]), row(a_ln_b[j]),
                                                     a_w_s[j], bst[j], row(mix_post_g[i]), tm, f"a_fwd_{j}")
            mix = dict(h1=h1, gp=gp, u=u, vh=vh, rs=rs, gated=gated, m=m)
        else:
            if wgrp_full is None:
                side_g = gathered(0, h)[0].reshape(N_DEV, side_rows + 8, B_GROUP_DIM)
                wgrp_full = (side_g[:, :side_rows].reshape(N_DEV, 2, ngrp, grp_local, B_GROUP_DIM)
                             .transpose(1, 2, 0, 3, 4).reshape(2, ngrp, B_GROUP_DIM, B_GROUP_DIM).astype(BF16))
                scale_full = (side_g[:, side_rows:side_rows + 2, :sdev].transpose(1, 0, 2)
                              .reshape(2, 1, N_DEV * sdev))
            x1, h1, pooled, mixed, m = _b_fwd(h, gpre, wg[2 * i], lay, j, wgrp_full[j], scale_full[j],
                                              row(mix_post_g[i]), tm_b, f"b_fwd_{j}")
            mix = dict(h1=h1, pooled=pooled, mixed=mixed, m=m)
        wg[2 * i + 1] = gathered(2 * i + 1, x1)
        x2, h2, acts, f, *loss_acc = _f_fwd(x1, row(ffn_pre_g[i]), wg[2 * i + 1], lay, i, row(ffn_post_g[i]), tm_f,
                                               f"f_fwd_{i}", loss_target[0] if i == DEPTH - 1 else None)
        saved.append(dict(x=h, x1=x1, mix=mix, h2=h2, acts=acts, f=f))
        h = x2
    dy, (loss_acc,) = h, loss_acc

    small_g = {k: [None] * w[k].shape[0] for k in SMALL}
    dgrp, dscale = [None, None], [None, None]
    pending = [None] * nsub
    token = jnp.zeros((8, 128), F32)

    def scatter(ks, gbufs):
        gots = [pltpu.with_memory_space_constraint(lax.empty((N_DEV - 1,) + g.shape[1:], g.dtype), pltpu.HBM)
                for g in gbufs]
        ss, rs, src, zone, tok = _exchange_start(True, gbufs, gots, token, f"scatter_start_{ks[0]}")
        for n, k in enumerate(ks):
            pending[k] = (ss[n:n + 1], rs[n:n + 1], src[n:n + 1], zone[n:n + 1])
        return tok

    def small_exchanges():
        side_grad = jnp.concatenate(
            [jnp.stack(dgrp).reshape(2, ngrp, N_DEV, grp_local, B_GROUP_DIM).transpose(2, 0, 1, 3, 4)
             .reshape(N_DEV, side_rows, B_GROUP_DIM),
             jnp.pad(jnp.stack(dscale).reshape(2, N_DEV, sdev).transpose(1, 0, 2),
                     ((0, 0), (0, 6), (0, B_GROUP_DIM - sdev)))], axis=1)
        small_part = _pack_small({k: jnp.stack(small_g[k]) for k in SMALL}, d,
                                 jnp.broadcast_to(loss_acc[:1, :1], (1, d)))
        got = pltpu.with_memory_space_constraint(lax.empty((N_DEV - 1,) + side_grad.shape[1:], F32), pltpu.HBM)
        side_x = _exchange_start(True, [side_grad], [got], token, "side_scatter_start")
        small_x = _exchange_start(False, [small_part], [landing(small_part)], side_x[4], "small_gather_start")
        return side_x[:4], small_x[:4], small_x[4]

    for i in reversed(range(DEPTH)):
        sv = saved[i]
        j = i // 2
        wf, wm = wg[2 * i + 1], wg[2 * i]
        dx1, df, dacts, dgpost, dgpre = _f_bwd(dy, sv["f"], sv["x1"], sv["acts"], row(ffn_pre_g[i]), wf, lay, i,
                                               row(ffn_post_g[i]), token, tm_f, f"f_bwd_{i}")
        small_g["ffn_post_g"][i], small_g["ffn_pre_g"][i] = dgpost[0], dgpre[0]
        gbuf_f = _grad_ffn(dacts, sv["h2"], df, lay.ffn_rows, f"g_ffn_{i}")
        if i == 0:
            token = scatter([2 * i + 1], [gbuf_f])
        mix = sv["mix"]
        gpost = row(mix_post_g[i])
        if i % 2 == 0:
            dx, dm, dz, dgpost, dgpre, dlng, dlnb, dws, dbt = _a_bwd(
                dx1, mix["m"], sv["x"], mix["gp"], mix["u"], mix["vh"], mix["rs"], row(mix_pre_g[i]), wm, lay, j,
                row(a_ln_g[j]), row(a_ln_b[j]), a_w_s[j], bst[j], gpost, token, tm_abwd, f"a_bwd_{j}")
            small_g["a_ln_g"][j], small_g["a_ln_b"][j] = dlng[0], dlnb[0]
            small_g["a_w_s"][j], small_g["a_b_s"][j] = dws, dbt[:, :A_GROUPS].T
            small_g["mix_post_g"][i], small_g["mix_pre_g"][i] = dgpost[0], dgpre[0]
            order = None
            if i == 0:
                side_x, small_x, order = small_exchanges()
            gbuf = _grad_into(lay.a_total, dz, mix["h1"], lay.a_in[j], lay.a_in_rows, f"g_a_in_{j}", after=order)
            gbuf = _grad_into(gbuf, mix["gated"], dm, lay.a_out[j], lay.a_out_rows, f"g_a_out_{j}")
        else:
            dx, dm, draw, dp, dgpost, dgpre, dsc = _b_bwd(
                dx1, mix["m"], sv["x"], mix["pooled"], row(mix_pre_g[i]), wm, lay, j, wgrp_full[j], scale_full[j],
                gpost, token, tm_b, f"b_bwd_{j}")
            dscale[j] = dsc[0]
            gbuf, dgrp[j] = _grad_pool_mixer(mix["h1"], dp, mix["mixed"], dm, mix["pooled"], draw, lay.b_rows,
                                             f"g_b_{j}")
            small_g["mix_post_g"][i], small_g["mix_pre_g"][i] = dgpost[0], dgpre[0]
        token = scatter([2 * i], [gbuf]) if i == 0 else scatter([2 * i + 1, 2 * i], [gbuf_f, gbuf])
        dy = dx
    grad_x = dy[None]

    g_sub = [None] * nsub

    parts = [None] * nsub

    def arrived(k, after):
        ss, rs, src, zone = pending[k]
        (own,), (got,) = _exchange_wait(True, ss, rs, src, zone, after, f"scatter_wait_{k}")
        parts[k] = (own, got)
        if k % 4 == 0:
            g_sub[k] = _sum_parts(own, got, me1, f"sum_grads_{k}")

    def fused_update(k, subs, off, rows):
        back = turn if k in turned else (lambda a: a)
        wk, mk, vk = state[k] if k in turned else (w[k], mom[k], var[k])
        out = _sum_adamw([parts[s][0] for s in subs], [parts[s][1] for s in subs], me1, off, rows, wk, mk, vk,
                         f"update_{k}")
        grads[k], delta[k], new_m[k], new_v[k] = (back(a) for a in out)

    def rows_of(k, off, n):
        return g_sub[k][off:off + n]

    grads, delta, new_m, new_v = {}, {}, {}, {}

    def update(k):
        shape = w[k].shape
        two = lambda a: a.reshape(-1, shape[-1])
        dl, nm, nv = _adamw(two(w[k]), two(grads[k]), two(mom[k]), two(var[k]), f"adamw_{k}")
        delta[k], new_m[k], new_v[k] = dl.reshape(shape), nm.reshape(shape), nv.reshape(shape)

    for k in range(1, nsub):
        arrived(k, token)
    ffn_subs = [2 * l + 1 for l in range(DEPTH)]
    fused_update("ffn_w_gate", ffn_subs, lay.gate[0], ffn_local)
    fused_update("ffn_w_up", ffn_subs, lay.up[0], ffn_local)
    fused_update("ffn_w_down", ffn_subs, lay.down[0], ffn_local)
    fused_update("b_w_in", [2, 6], lay.b_in[0], lay.b_rows)
    fused_update("b_w_out", [2, 6], lay.b_out[0], lay.b_rows)
    early = ("ffn_w_gate", "ffn_w_up", "ffn_w_down", "b_w_in", "b_w_out")

    (side_own,), (side_got,) = _exchange_wait(True, *side_x, [delta[k] for k in early], "side_scatter_wait")
    g_side = _sum_parts(side_own, side_got, me1, "sum_side")
    grads["b_w_grp"] = g_side[:side_rows].reshape(b_w_grp.shape)
    grads["b_scale"] = g_side[side_rows:side_rows + 2, :sdev]
    update("b_w_grp")
    update("b_scale")
    _, (small_all,) = _exchange_wait(False, *small_x, [delta["b_w_grp"], delta["b_scale"]], "small_gather_wait")
    small_sum = _sum_devices(small_all.reshape(N_DEV, -1, d), "sum_small")
    g_small = _unpack_small(small_sum, w)
    loss = small_sum[sum(w[k].size for k in SMALL) // d, 0]
    grads.update(g_small)
    dl, nm, nv = _adamw(state["small"][0], _pack_small(g_small, d), state["small"][1], state["small"][2],
                        "adamw_small")
    delta.update(_unpack_small(dl, w))
    new_m.update(_unpack_small(nm, w))
    new_v.update(_unpack_small(nv, w))

    arrived(0, dl)
    grads["a_w_in"] = jnp.stack([rows_of(4 * j, lay.a_in[j], lay.a_in_rows).T for j in range(2)])
    update("a_w_in")
    fused_update("a_w_out", [0, 4], lay.a_out[0], lay.a_out_rows)

    return (loss, grad_x, *[grads[k] for k in names], *[delta[k] for k in names], *[new_m[k] for k in names],
            *[new_v[k] for k in names])
```

```python
import math

import jax
import jax.numpy as jnp
from jax import lax
from jax.experimental import pallas as pl
from jax.experimental.pallas import tpu as pltpu

F32 = jnp.float32
BF16 = jnp.bfloat16
MESH = pl.DeviceIdType.MESH
ANY = pl.BlockSpec(memory_space=pl.ANY)

N_DEV = 8
EPS = 1e-6
CHUNK = 128
A_GROUPS = 8
A_GROUP_DIM = 256
B_WINDOWS = (2, 4, 8, 16)
B_GROUP_DIM = 256
HALO = 16
DEPTH = 4

ADAM_LR = 0.001
ADAM_B1 = 0.9
ADAM_B2 = 0.999
ADAM_EPS = 1e-08
ADAM_WD = 0.01
ADAM_STEP = 10

VMEM_LIMIT_BYTES = 60 * 1024 * 1024

INV_SQRT2 = 1.0 / math.sqrt(2.0)
LOG2_E = 1.0 / math.log(2.0)
INV_SQRT_2PI = 1.0 / math.sqrt(2.0 * math.pi)


def _call(body, **kw):
    return pl.pallas_call(body, **kw)


def _params(*semantics):
    return pltpu.CompilerParams(dimension_semantics=semantics or None, vmem_limit_bytes=VMEM_LIMIT_BYTES)


def _resident(shape, index):
    return pl.BlockSpec(shape, lambda *_: index, pipeline_mode=pl.Buffered(1))


def _rows(tm, width):
    return pl.BlockSpec((tm, width), lambda i: (i, 0))


def _nn(a, b):
    return jnp.dot(a, b, preferred_element_type=F32)


def _nt(a, b):
    return lax.dot_general(a, b, (((1,), (1,)), ((), ())), preferred_element_type=F32)


def _tn(a, b):
    return lax.dot_general(a, b, (((0,), (0,)), ((), ())), preferred_element_type=F32)


def _rms_fwd(x, g):
    r = lax.rsqrt(jnp.mean(x * x, axis=-1, keepdims=True) + EPS)
    return x * r * g


def _rms_bwd(x, g, dy):
    r = lax.rsqrt(jnp.mean(x * x, axis=-1, keepdims=True) + EPS)
    xh = x * r
    dg = jnp.sum(dy * xh, axis=0, keepdims=True)
    dxh = dy * g
    dx = r * (dxh - xh * jnp.mean(dxh * xh, axis=-1, keepdims=True))
    return dx, dg


SLAB = 16


def _slabs(n):
    return [slice(r, r + SLAB) for r in range(0, n, SLAB)]


def _rms_bwd_slabs(x_at, dy_at, g, n, n_sum, emit):
    acc = jnp.zeros((8, g.shape[1]), F32)
    for rows in _slabs(n):
        x = x_at(rows)
        dy = dy_at(rows)
        r = lax.rsqrt(jnp.mean(x * x, axis=-1, keepdims=True) + EPS)
        xh = x * r
        if rows.start < n_sum:
            p = dy * xh
            acc = acc + p[:8] + p[8:]
        dxh = dy * g
        emit(rows, r * (dxh - xh * jnp.mean(dxh * xh, axis=-1, keepdims=True)))
    return jnp.sum(acc, axis=0, keepdims=True)


def _gelu(z):
    phi = 0.5 + 0.5 * lax.erf(z * INV_SQRT2)
    e = jnp.exp2(z * z * (-0.5 * LOG2_E))
    return z * phi, phi + z * e * INV_SQRT_2PI


def _layernorm_stats(v):
    mu = jnp.mean(v, axis=-1, keepdims=True)
    xc = v - mu
    rs = lax.rsqrt(jnp.mean(xc * xc, axis=-1, keepdims=True) + EPS)
    return xc * rs, rs


def _tril_mask():
    r = lax.broadcasted_iota(jnp.int32, (CHUNK, CHUNK), 0)
    c = lax.broadcasted_iota(jnp.int32, (CHUNK, CHUNK), 1)
    return r >= c


class _Layout:
    def __init__(self, d, ffn_rows):
        self.ffn_rows = ffn_rows
        self.gate, self.up, self.down = [0] * DEPTH, [self.ffn_rows] * DEPTH, [2 * self.ffn_rows] * DEPTH
        self.f_total = 3 * self.ffn_rows
        self.a_in_rows, self.a_out_rows, self.b_rows = 4 * d // N_DEV, 2 * d // N_DEV, d // N_DEV
        self.a_in, self.a_out = [0, 0], [self.a_in_rows] * 2
        self.a_total = self.a_in_rows + self.a_out_rows
        self.b_in, self.b_out = [0, 0], [self.b_rows] * 2
        self.b_total = 2 * self.b_rows


def _wspec(rows, d):
    return _resident((N_DEV * rows, d), (0, 0))


def _a_fwd(x, gpre, wg, lay, j, lng, lnb, ws, bst, gpost, tm, name):
    t, d = x.shape
    aw = 2 * d
    nch = tm // CHUNK

    def body(x_ref, gpre_ref, win_ref, lng_ref, lnb_ref, ws_ref, bst_ref, wout_ref, gpost_ref,
             x1_ref, h1_ref, gp_ref, u_ref, vh_ref, rs_ref, gated_ref, m_ref):
        xv = x_ref[...]
        h1 = _rms_fwd(xv, gpre_ref[...]).astype(BF16)
        h1_ref[...] = h1
        z = _nt(h1, win_ref[...])
        u, du_dz = _gelu(z[:, :aw])
        v, dv_dz = _gelu(z[:, aw:])
        gp_ref[:, :aw] = du_dz.astype(BF16)
        gp_ref[:, aw:] = dv_dz.astype(BF16)
        u_ref[...] = u.astype(BF16)
        vh, rs = _layernorm_stats(v)
        vh_ref[...] = vh.astype(BF16)
        rs_ref[...] = jnp.broadcast_to(rs, rs_ref.shape)
        vn = (vh * lng_ref[...] + lnb_ref[...]).astype(BF16)
        mask = _tril_mask()
        for g in range(A_GROUPS):
            wm = jnp.where(mask, ws_ref[g], 0.0).astype(BF16)
            cols = slice(g * A_GROUP_DIM, (g + 1) * A_GROUP_DIM)
            for c in range(nch):
                rows = slice(c * CHUNK, (c + 1) * CHUNK)
                sv = _nn(wm, vn[rows, cols]) + bst_ref[:, g:g + 1]
                gated_ref[rows, cols] = (u[rows, cols] * sv).astype(BF16)
        m = _nn(gated_ref[...], wout_ref[...])
        m_ref[...] = m
        x1_ref[...] = xv + _rms_fwd(m, gpost_ref[...])

    vec = lambda w: _resident((1, w), (0, 0))
    return _call(
        body, name=name, grid=(t // tm,),
        in_specs=[_rows(tm, d), vec(d), _wspec(lay.a_in_rows, d), vec(aw), vec(aw),
                  _resident((A_GROUPS, CHUNK, CHUNK), (0, 0, 0)), _resident((CHUNK, A_GROUPS), (0, 0)),
                  _wspec(lay.a_out_rows, d), vec(d)],
        out_specs=[_rows(tm, d), _rows(tm, d), _rows(tm, 2 * aw), _rows(tm, aw), _rows(tm, aw), _rows(tm, 128),
                   _rows(tm, aw), _rows(tm, d)],
        out_shape=[jax.ShapeDtypeStruct((t, d), F32), jax.ShapeDtypeStruct((t, d), BF16),
                   jax.ShapeDtypeStruct((t, 2 * aw), BF16), jax.ShapeDtypeStruct((t, aw), BF16),
                   jax.ShapeDtypeStruct((t, aw), BF16), jax.ShapeDtypeStruct((t, 128), F32),
                   jax.ShapeDtypeStruct((t, aw), BF16), jax.ShapeDtypeStruct((t, d), F32)],
        compiler_params=_params("parallel"),
    )(x, gpre, wg[0], lng, lnb, ws, bst, wg[1], gpost)


def _a_bwd(dx1, m, x, gp, u, vh, rs, gpre, wg, lay, j, lng, lnb, ws, bst, gpost, after, tm, name):
    t, d = x.shape
    aw = 2 * d
    nch = tm // CHUNK

    def body(dx1_ref, m_ref, x_ref, gp_ref, u_ref, vh_ref, rs_ref, gpre_ref, win_ref, lng_ref, lnb_ref, ws_ref, bst_ref,
             wout_ref, gpost_ref, after_ref,
             dx_ref, dm_ref, dz_ref, dgpost_ref, dgpre_ref, dlng_ref, dlnb_ref, dws_ref, dbt_ref, dvn_ref):
        @pl.when(pl.program_id(0) == 0)
        def _():
            for r in (dgpost_ref, dgpre_ref, dlng_ref, dlnb_ref, dws_ref, dbt_ref):
                r[...] = jnp.zeros_like(r)

        def put_dm(rows, dx):
            dm_ref[rows, :] = dx.astype(BF16)

        dgpost_ref[...] += _rms_bwd_slabs(lambda rows: m_ref[rows, :], lambda rows: dx1_ref[rows, :], gpost_ref[...],
                                          tm, tm, put_dm)
        dgated = _nt(dm_ref[...], wout_ref[...])

        vh = vh_ref[...].astype(F32)
        rs = rs_ref[:, :1]
        lng_v = lng_ref[...]
        vn = (vh * lng_v + lnb_ref[...]).astype(BF16)
        mask = _tril_mask()
        lane = lax.broadcasted_iota(jnp.int32, (CHUNK, CHUNK), 1)
        for g in range(A_GROUPS):
            wm = jnp.where(mask, ws_ref[g], 0.0).astype(BF16)
            cols = slice(g * A_GROUP_DIM, (g + 1) * A_GROUP_DIM)
            dws_g = jnp.zeros((CHUNK, CHUNK), F32)
            db_g = jnp.zeros((CHUNK, 1), F32)
            for c in range(nch):
                rows = slice(c * CHUNK, (c + 1) * CHUNK)
                vn_cg = vn[rows, cols]
                sv = _nn(wm, vn_cg) + bst_ref[:, g:g + 1]
                dg_cg = dgated[rows, cols]
                dsv = dg_cg * u_ref[rows, cols].astype(F32)
                dsv_bf = dsv.astype(BF16)
                db_g = db_g + jnp.sum(dsv, axis=1, keepdims=True)
                dws_g = dws_g + _nt(dsv_bf, vn_cg)
                dvn_ref[rows, cols] = _tn(wm, dsv_bf)
                dz_ref[rows, cols] = (dg_cg * sv * gp_ref[rows, cols].astype(F32)).astype(BF16)
            dws_ref[g] += jnp.where(mask, dws_g, 0.0)
            dbt_ref[...] += jnp.where(lane == g, db_g, 0.0)
        dvn = dvn_ref[...]
        dlng_ref[...] += jnp.sum(dvn * vh, axis=0, keepdims=True)
        dlnb_ref[...] += jnp.sum(dvn, axis=0, keepdims=True)
        dvh = dvn * lng_v
        dv = rs * (dvh - jnp.mean(dvh, axis=-1, keepdims=True) - vh * jnp.mean(dvh * vh, axis=-1, keepdims=True))
        dz_ref[:, aw:] = (dv * gp_ref[:, aw:].astype(F32)).astype(BF16)
        dh1 = _nn(dz_ref[...], win_ref[...])

        def put_dx(rows, dx):
            dx_ref[rows, :] = dx1_ref[rows, :] + dx

        dgpre_ref[...] += _rms_bwd_slabs(lambda rows: x_ref[rows, :], lambda rows: dh1[rows, :], gpre_ref[...],
                                         tm, tm, put_dx)

    vec = lambda w: _resident((1, w), (0, 0))
    acc = lambda shape: pl.BlockSpec(shape, lambda i: (0,) * len(shape))
    return _call(
        body, name=name, grid=(t // tm,),
        in_specs=[_rows(tm, d), _rows(tm, d), _rows(tm, d), _rows(tm, 2 * aw), _rows(tm, aw), _rows(tm, aw),
                  _rows(tm, 128), vec(d), _wspec(lay.a_in_rows, d), vec(aw), vec(aw),
                  _resident((A_GROUPS, CHUNK, CHUNK), (0, 0, 0)), _resident((CHUNK, A_GROUPS), (0, 0)),
                  _wspec(lay.a_out_rows, d), vec(d), ANY],
        out_specs=[_rows(tm, d), _rows(tm, d), _rows(tm, 2 * aw), acc((1, d)), acc((1, d)), acc((1, aw)), acc((1, aw)),
                   acc((A_GROUPS, CHUNK, CHUNK)), acc((CHUNK, CHUNK))],
        out_shape=[jax.ShapeDtypeStruct((t, d), F32), jax.ShapeDtypeStruct((t, d), BF16),
                   jax.ShapeDtypeStruct((t, 2 * aw), BF16), jax.ShapeDtypeStruct((1, d), F32),
                   jax.ShapeDtypeStruct((1, d), F32), jax.ShapeDtypeStruct((1, aw), F32),
                   jax.ShapeDtypeStruct((1, aw), F32), jax.ShapeDtypeStruct((A_GROUPS, CHUNK, CHUNK), F32),
                   jax.ShapeDtypeStruct((CHUNK, CHUNK), F32)],
        scratch_shapes=[pltpu.VMEM((tm, aw), F32)],
        compiler_params=_params("arbitrary"),
    )(dx1, m, x, gp, u, vh, rs, gpre, wg[0], lng, lnb, ws, bst, wg[1], gpost, after)


def _window_counts(first_row, n, win):
    tpos = first_row + lax.broadcasted_iota(jnp.int32, (n, 1), 0)
    return jnp.clip(tpos + 1, 1, win).astype(F32)


def _b_fwd(x, gpre, wg, lay, j, wgrp, scale, gpost, tm, name):
    t, d = x.shape
    n = tm + HALO
    ngrp = len(B_WINDOWS)

    def body(x_ref, xprev_ref, gpre_ref, win_ref, wgrp_ref, scale_ref, wout_ref, gpost_ref,
             x1_ref, h1_ref, pooled_ref, mixed_ref, m_ref):
        i = pl.program_id(0)
        xv = x_ref[...]
        keep = jnp.where(i > 0, 1.0, 0.0)
        xe = jnp.concatenate([xprev_ref[...] * keep, xv], axis=0)
        h1e = _rms_fwd(xe, gpre_ref[...]).astype(BF16)
        h1_ref[...] = h1e[HALO:]
        p = _nn(h1e, win_ref[...])
        acc = p
        shift = 1
        for g, win in enumerate(B_WINDOWS):
            lo = g * B_GROUP_DIM
            if g > 0:
                acc = acc[:, B_GROUP_DIM:]
            while shift < win:
                acc = acc + pltpu.roll(acc, shift, 0)
                shift *= 2
            cnt = _window_counts(i * tm - HALO, n, win)
            pooled = acc[:, :B_GROUP_DIM] / cnt - p[:, lo:lo + B_GROUP_DIM]
            pooled_ref[:, lo:lo + B_GROUP_DIM] = pooled[HALO:].astype(BF16)
        for g in range(ngrp):
            cols = slice(g * B_GROUP_DIM, (g + 1) * B_GROUP_DIM)
            raw = _nn(pooled_ref[:, cols], wgrp_ref[g])
            mixed_ref[:, cols] = (raw * scale_ref[:, cols]).astype(BF16)
        m = _nn(mixed_ref[...], wout_ref[...])
        m_ref[...] = m
        x1_ref[...] = xv + _rms_fwd(m, gpost_ref[...])

    vec = lambda w: _resident((1, w), (0, 0))
    per = tm // HALO
    return _call(
        body, name=name, grid=(t // tm,),
        in_specs=[_rows(tm, d), pl.BlockSpec((HALO, d), lambda i: (jnp.maximum(i * per - 1, 0), 0)), vec(d),
                  _wspec(lay.b_rows, d), _resident((ngrp, B_GROUP_DIM, B_GROUP_DIM), (0, 0, 0)), vec(d),
                  _wspec(lay.b_rows, d), vec(d)],
        out_specs=[_rows(tm, d)] * 5,
        out_shape=[jax.ShapeDtypeStruct((t, d), F32), jax.ShapeDtypeStruct((t, d), BF16),
                   jax.ShapeDtypeStruct((t, d), BF16), jax.ShapeDtypeStruct((t, d), BF16),
                   jax.ShapeDtypeStruct((t, d), F32)],
        compiler_params=_params("parallel"),
    )(x, x, gpre, wg[0], wgrp, scale, wg[1], gpost)


def _b_bwd(dx1, m, x, pooled, gpre, wg, lay, j, wgrp, scale, gpost, after, tm, name):
    t, d = x.shape
    n = tm + HALO
    ngrp = len(B_WINDOWS)
    steps = t // tm

    def body(dx1_ref, dx1n_ref, m_ref, mn_ref, x_ref, pooled_ref, pooledn_ref, gpre_ref, win_ref, wgrp_ref, scale_ref,
             wout_ref, gpost_ref, after_ref,
             dx_ref, dm_ref, draw_ref, dp_ref, dgpost_ref, dgpre_ref, dscale_ref, dpool_ref):
        i = pl.program_id(0)

        @pl.when(i == 0)
        def _():
            for r in (dgpost_ref, dgpre_ref, dscale_ref):
                r[...] = jnp.zeros_like(r)

        keep = jnp.where(i < steps - 1, 1.0, 0.0)
        dy = dx1_ref[...]
        dye = jnp.concatenate([dy, dx1n_ref[...] * keep], axis=0)
        me = jnp.concatenate([m_ref[...], mn_ref[...]], axis=0)
        gpost_v = gpost_ref[...]
        r = lax.rsqrt(jnp.mean(me * me, axis=-1, keepdims=True) + EPS)
        mh = me * r
        dgpost_ref[...] += jnp.sum((dye * mh)[:tm], axis=0, keepdims=True)
        dmh = dye * gpost_v
        dme = (r * (dmh - mh * jnp.mean(dmh * mh, axis=-1, keepdims=True))).astype(BF16)
        dm_ref[...] = dme[:tm]
        dmixed = _nt(dme, wout_ref[...])
        pooled_e = jnp.concatenate([pooled_ref[...], pooledn_ref[...]], axis=0)
        scale_v = scale_ref[...]
        for g, win in enumerate(B_WINDOWS):
            cols = slice(g * B_GROUP_DIM, (g + 1) * B_GROUP_DIM)
            raw = _nn(pooled_e[:, cols], wgrp_ref[g])
            dscale_ref[:, cols] += jnp.sum((dmixed[:, cols] * raw)[:tm], axis=0, keepdims=True)
            draw = (dmixed[:, cols] * scale_v[:, cols]).astype(BF16)
            draw_ref[:, cols] = draw[:tm]
            dpool = _nt(draw, wgrp_ref[g])
            acc = dpool / _window_counts(i * tm, n, win)
            shift = 1
            while shift < win:
                acc = acc + pltpu.roll(acc, n - shift, 0)
                shift *= 2
            dpool_ref[:, cols] = (acc - dpool)[:tm]
        dp = dpool_ref[...].astype(BF16)
        dp_ref[...] = dp
        dh1 = _nt(dp, win_ref[...])
        dxp, dgpre = _rms_bwd(x_ref[...], gpre_ref[...], dh1)
        dgpre_ref[...] += dgpre
        dx_ref[...] = dy + dxp

    vec = lambda w: _resident((1, w), (0, 0))
    acc = lambda shape: pl.BlockSpec(shape, lambda i: (0,) * len(shape))
    per = tm // HALO
    nxt = lambda i: (jnp.minimum((i + 1) * per, t // HALO - 1), 0)
    return _call(
        body, name=name, grid=(steps,),
        in_specs=[_rows(tm, d), pl.BlockSpec((HALO, d), nxt), _rows(tm, d), pl.BlockSpec((HALO, d), nxt), _rows(tm, d),
                  _rows(tm, d), pl.BlockSpec((HALO, d), nxt), vec(d), _wspec(lay.b_rows, d),
                  _resident((ngrp, B_GROUP_DIM, B_GROUP_DIM), (0, 0, 0)), vec(d), _wspec(lay.b_rows, d),
                  vec(d), ANY],
        out_specs=[_rows(tm, d)] * 4 + [acc((1, d))] * 3,
        out_shape=[jax.ShapeDtypeStruct((t, d), F32), jax.ShapeDtypeStruct((t, d), BF16),
                   jax.ShapeDtypeStruct((t, d), BF16), jax.ShapeDtypeStruct((t, d), BF16)]
                  + [jax.ShapeDtypeStruct((1, d), F32)] * 3,
        scratch_shapes=[pltpu.VMEM((tm, d), F32)],
        compiler_params=_params("arbitrary"),
    )(dx1, dx1, m, m, x, pooled, pooled, gpre, wg[0], wgrp, scale, wg[1], gpost, after)


def _f_fwd(x1, gpre, wg, lay, l, gpost, tm, name, target=None):
    t, d = x1.shape
    hid = N_DEV * lay.ffn_rows
    head = target is not None

    def body(x_ref, gpre_ref, wgate_ref, wup_ref, wdown_ref, gpost_ref, *rest):
        x2_ref, h2_ref, abs_ref, f_ref = rest[-5:-1] if head else rest
        xv = x_ref[...]
        h2 = _rms_fwd(xv, gpre_ref[...]).astype(BF16)
        h2_ref[...] = h2
        a = _nt(h2, wgate_ref[...])
        b = _nt(h2, wup_ref[...])
        sig = jax.nn.sigmoid(a)
        silu = a * sig
        abs_ref[:, :hid] = (b * (sig + silu * (1.0 - sig))).astype(BF16)
        abs_ref[:, hid:2 * hid] = silu.astype(BF16)
        s = (silu * b).astype(BF16)
        abs_ref[:, 2 * hid:] = s
        f = _nn(s, wdown_ref[...])
        f_ref[...] = f
        x2 = xv + _rms_fwd(f, gpost_ref[...])
        if head:
            target_ref, loss_ref = rest[0], rest[-1]

            @pl.when(pl.program_id(0) == 0)
            def _():
                loss_ref[...] = jnp.zeros_like(loss_ref)

            diff = x2 - target_ref[...]
            x2_ref[...] = diff * (1.0 / d)
            sq = jnp.sum(jnp.sum(diff * diff, axis=0, keepdims=True), axis=1, keepdims=True)
            loss_ref[...] += sq * (0.5 / d)
        else:
            x2_ref[...] = x2

    vec = lambda w: _resident((1, w), (0, 0))
    return _call(
        body, name=name, grid=(t // tm,),
        in_specs=[_rows(tm, d), vec(d), _wspec(lay.ffn_rows, d), _wspec(lay.ffn_rows, d),
                  _wspec(lay.ffn_rows, d), vec(d)] + ([_rows(tm, d)] if head else []),
        out_specs=[_rows(tm, d), _rows(tm, d), _rows(tm, 3 * hid), _rows(tm, d)]
                  + ([pl.BlockSpec((8, 128), lambda i: (0, 0))] if head else []),
        out_shape=[jax.ShapeDtypeStruct((t, d), F32), jax.ShapeDtypeStruct((t, d), BF16),
                   jax.ShapeDtypeStruct((t, 3 * hid), BF16), jax.ShapeDtypeStruct((t, d), F32)]
                  + ([jax.ShapeDtypeStruct((8, 128), F32)] if head else []),
        compiler_params=_params("arbitrary" if head else "parallel"),
    )(x1, gpre, wg[0], wg[1], wg[2], gpost, *([target] if head else []))


def _f_bwd(dx2, f, x1, acts, gpre, wg, lay, l, gpost, after, tm, name):
    t, d = x1.shape
    hid = N_DEV * lay.ffn_rows

    def body(dx2_ref, f_ref, x_ref, ab_ref, gpre_ref, wgate_ref, wup_ref, wdown_ref, gpost_ref, after_ref,
             dx1_ref, df_ref, dab_ref, dgpost_ref, dgpre_ref):
        @pl.when(pl.program_id(0) == 0)
        def _():
            dgpost_ref[...] = jnp.zeros_like(dgpost_ref)
            dgpre_ref[...] = jnp.zeros_like(dgpre_ref)

        def put_df(rows, dx):
            df_ref[rows, :] = dx.astype(BF16)

        dgpost_ref[...] += _rms_bwd_slabs(lambda rows: f_ref[rows, :], lambda rows: dx2_ref[rows, :], gpost_ref[...],
                                          tm, tm, put_df)
        ds = _nt(df_ref[...], wdown_ref[...])
        dab_ref[:, :hid] = (ds * ab_ref[:, :hid].astype(F32)).astype(BF16)
        dab_ref[:, hid:] = (ds * ab_ref[:, hid:].astype(F32)).astype(BF16)
        dh2 = _nn(dab_ref[:, :hid], wgate_ref[...]) + _nn(dab_ref[:, hid:], wup_ref[...])

        def put_dx(rows, dx):
            dx1_ref[rows, :] = dx2_ref[rows, :] + dx

        dgpre_ref[...] += _rms_bwd_slabs(lambda rows: x_ref[rows, :], lambda rows: dh2[rows, :], gpre_ref[...],
                                         tm, tm, put_dx)

    vec = lambda w: _resident((1, w), (0, 0))
    acc = pl.BlockSpec((1, d), lambda i: (0, 0))
    return _call(
        body, name=name, grid=(t // tm,),
        in_specs=[_rows(tm, d), _rows(tm, d), _rows(tm, d), _rows(tm, 2 * hid), vec(d),
                  _wspec(lay.ffn_rows, d), _wspec(lay.ffn_rows, d),
                  _wspec(lay.ffn_rows, d), vec(d), ANY],
        out_specs=[_rows(tm, d), _rows(tm, d), _rows(tm, 2 * hid), acc, acc],
        out_shape=[jax.ShapeDtypeStruct((t, d), F32), jax.ShapeDtypeStruct((t, d), BF16),
                   jax.ShapeDtypeStruct((t, 3 * hid), BF16),
                   jax.ShapeDtypeStruct((1, d), F32), jax.ShapeDtypeStruct((1, d), F32)],
        input_output_aliases={3: 2},
        compiler_params=_params("arbitrary"),
    )(dx2, f, x1, acts, gpre, wg[0], wg[1], wg[2], gpost, after)


def _grad_into(gbuf, lhs, rhs, off, rows, name, after=None):
    t, m = lhs.shape
    d = rhs.shape[1]
    assert m == N_DEV * rows and off % rows == 0
    per_tile = {512: 2, 256: 4}[rows]
    tm = per_tile * rows
    assert tm % 128 == 0 and rows % 16 == 0
    tk = 2048 if t % 2048 == 0 else 256
    ksteps = t // tk
    fresh = isinstance(gbuf, int)
    shape = (N_DEV, gbuf, d) if fresh else gbuf.shape
    extra = ([] if fresh else [gbuf]) + ([] if after is None else [after])

    def body(l_ref, r_ref, *rest):
        o_ref, acc_ref = rest[-2:]
        k = pl.program_id(1)

        @pl.when(k == 0)
        def _():
            acc_ref[...] = jnp.zeros_like(acc_ref)

        acc_ref[...] += _tn(l_ref[...], r_ref[...])

        @pl.when(k == ksteps - 1)
        def _():
            o_ref[...] = acc_ref[...].reshape(per_tile, rows, d).astype(BF16)

    return _call(
        body, name=name, grid=(N_DEV // per_tile, ksteps),
        in_specs=[pl.BlockSpec((tk, tm), lambda i, k: (k, i)), pl.BlockSpec((tk, d), lambda i, k: (k, 0))]
                 + [ANY] * len(extra),
        out_specs=pl.BlockSpec((per_tile, rows, d), lambda i, k: (i, off // rows, 0)),
        out_shape=jax.ShapeDtypeStruct(shape, BF16),
        scratch_shapes=[pltpu.VMEM((tm, d), F32)],
        input_output_aliases={} if fresh else {2: 0},
        compiler_params=_params("parallel", "arbitrary"),
    )(lhs, rhs, *extra)


def _grad_pool_mixer(h1, dp, mixed, dm, pooled, draw, rows, name):
    t, d = h1.shape
    ngrp = len(B_WINDOWS)
    assert d == N_DEV * rows and d == ngrp * B_GROUP_DIM
    tk = 1024 if t % 1024 == 0 else 256
    ksteps = t // tk

    def body(h1_ref, dp_ref, mixed_ref, dm_ref, pooled_ref, draw_ref, o_ref, grp_ref, acc_in, acc_out):
        k = pl.program_id(0)

        @pl.when(k == 0)
        def _():
            acc_in[...] = jnp.zeros_like(acc_in)
            acc_out[...] = jnp.zeros_like(acc_out)
            grp_ref[...] = jnp.zeros_like(grp_ref)

        acc_in[...] += _tn(h1_ref[...], dp_ref[...])
        acc_out[...] += _tn(mixed_ref[...], dm_ref[...])
        for g in range(ngrp):
            cols = slice(g * B_GROUP_DIM, (g + 1) * B_GROUP_DIM)
            grp_ref[g] += _tn(pooled_ref[:, cols], draw_ref[:, cols])

        @pl.when(k == ksteps - 1)
        def _():
            o_ref[:, :rows, :] = acc_in[...].reshape(N_DEV, rows, d).astype(BF16)
            o_ref[:, rows:, :] = acc_out[...].reshape(N_DEV, rows, d).astype(BF16)

    return _call(
        body, name=name, grid=(ksteps,),
        in_specs=[_rows(tk, d)] * 6,
        out_specs=[pl.BlockSpec((N_DEV, 2 * rows, d), lambda k: (0, 0, 0)),
                   pl.BlockSpec((ngrp, B_GROUP_DIM, B_GROUP_DIM), lambda k: (0, 0, 0))],
        out_shape=[jax.ShapeDtypeStruct((N_DEV, 2 * rows, d), BF16),
                   jax.ShapeDtypeStruct((ngrp, B_GROUP_DIM, B_GROUP_DIM), F32)],
        scratch_shapes=[pltpu.VMEM((d, d), F32), pltpu.VMEM((d, d), F32)],
        compiler_params=_params("arbitrary"),
    )(h1, dp, mixed, dm, pooled, draw)


def _grad_ffn(acts, h2, df, rows, name):
    t, d = h2.shape
    per_tile = 4
    tm = per_tile * rows
    tiles = N_DEV // per_tile
    assert acts.shape[1] == 3 * N_DEV * rows and tm % 128 == 0 and rows % 16 == 0
    tk = 2048 if t % 2048 == 0 else 256
    ksteps = t // tk

    def body(l_ref, h2_ref, df_ref, o_ref, acc_ref):
        i, k = pl.program_id(0), pl.program_id(1)

        @pl.when(k == 0)
        def _():
            acc_ref[...] = jnp.zeros_like(acc_ref)

        @pl.when(i < 2 * tiles)
        def _():
            acc_ref[...] += _tn(l_ref[...], h2_ref[...])

        @pl.when(i >= 2 * tiles)
        def _():
            acc_ref[...] += _tn(l_ref[...], df_ref[...])

        @pl.when(k == ksteps - 1)
        def _():
            o_ref[...] = acc_ref[...].reshape(per_tile, rows, d).astype(BF16)

    return _call(
        body, name=name, grid=(3 * tiles, ksteps),
        in_specs=[pl.BlockSpec((tk, tm), lambda i, k: (k, i)),
                  pl.BlockSpec((tk, d), lambda i, k: (jnp.where(i < 2 * tiles, k, ksteps - 1), 0)),
                  pl.BlockSpec((tk, d), lambda i, k: (jnp.where(i >= 2 * tiles, k, 0), 0))],
        out_specs=pl.BlockSpec((per_tile, rows, d), lambda i, k: (i % tiles, i // tiles, 0)),
        out_shape=jax.ShapeDtypeStruct((N_DEV, 3 * rows, d), BF16),
        scratch_shapes=[pltpu.VMEM((tm, d), F32)],
        compiler_params=_params("arbitrary", "arbitrary"),
    )(acts, h2, df)


def _peers():
    x, y, c = lax.axis_index("x"), lax.axis_index("y"), lax.axis_index("c")
    flip = lambda v, f: 1 - v if f else v
    peers = []
    for r in range(1, N_DEV):
        px, py, pc = flip(x, r & 4), flip(y, r & 2), flip(c, r & 1)
        peers.append(((px, py, pc), 4 * px + 2 * py + pc))
    return 4 * x + 2 * y + c, peers


HBM = pl.BlockSpec(memory_space=pltpu.HBM)
SEM = pl.BlockSpec(memory_space=pltpu.SEMAPHORE)
EFFECT = pltpu.SideEffectType.DATAFLOW_SIDE_EFFECTING


def _peer_copies(scatter, srcs, lands, send_sems, recv_sems):
    me, peers = _peers()
    copies = []
    for a in range(len(srcs)):
        rows = srcs[a].shape[0]
        block = lambda k: lands[a].at[pl.ds(pl.multiple_of(k * rows, 8), rows)]
        for r, (peer, pidx) in enumerate(peers):
            src = srcs[a].at[pidx] if scatter else srcs[a]
            mine = lands[a].at[r] if scatter else block(pidx)
            theirs = lands[a].at[r] if scatter else block(me)
            send = pltpu.make_async_remote_copy(src_ref=src, dst_ref=theirs, send_sem=send_sems[a].at[r],
                                                recv_sem=recv_sems[a].at[r], device_id=peer, device_id_type=MESH)
            recv = pltpu.make_async_remote_copy(src_ref=src, dst_ref=mine, send_sem=send_sems[a].at[r],
                                                recv_sem=recv_sems[a].at[r], device_id=peer, device_id_type=MESH)
            copies.append((send, recv))
    return copies


def _own_copies(srcs, lands, send_sems):
    me, _ = _peers()
    copies = []
    for a in range(len(srcs)):
        rows = srcs[a].shape[0]
        copies.append(pltpu.make_async_copy(srcs[a], lands[a].at[pl.ds(pl.multiple_of(me * rows, 8), rows)],
                                            send_sems[a].at[N_DEV - 1]))
    return copies


def _exchange_start(scatter, srcs, lands, after, name):
    n = len(srcs)

    def body(*refs):
        src_refs, land_refs = refs[:n], refs[n:2 * n]
        outs = refs[2 * n + 1:]
        send_sems, recv_sems, token = outs[:n], outs[n:2 * n], outs[-1]
        for send, _ in _peer_copies(scatter, src_refs, land_refs, send_sems, recv_sems):
            send.start()
        if not scatter:
            for own in _own_copies(src_refs, land_refs, send_sems):
                own.start()
        token[...] = jnp.zeros_like(token)

    hbm = lambda a: pltpu.with_memory_space_constraint(a, pltpu.HBM)
    res = _call(
        body, name=name,
        in_specs=[HBM] * (2 * n) + [ANY],
        out_specs=[SEM] * (2 * n) + [HBM] * (2 * n) + [pl.BlockSpec(memory_space=pltpu.VMEM)],
        out_shape=[pltpu.SemaphoreType.DMA((N_DEV,))] * (2 * n)
                  + [pltpu.HBM(a.shape, a.dtype) for a in list(srcs) + list(lands)]
                  + [jax.ShapeDtypeStruct((8, 128), F32)],
        input_output_aliases={i: 2 * n + i for i in range(2 * n)},
        compiler_params=pltpu.CompilerParams(has_side_effects=EFFECT),
    )(*[hbm(a) for a in srcs], *[hbm(a) for a in lands], after)
    return res[:n], res[n:2 * n], res[2 * n:3 * n], res[3 * n:4 * n], res[-1]


def _chip_peers():
    x, y, c = lax.axis_index("x"), lax.axis_index("y"), lax.axis_index("c")
    far = []
    for px, py in ((1 - x, y), (x, 1 - y), (1 - x, 1 - y)):
        far.append(((px, py, c), 4 * px + 2 * py + c, 4 * px + 2 * py + 1 - c))
    return 4 * x + 2 * y + c, ((x, y, 1 - c), 4 * x + 2 * y + 1 - c), far


def _block(land, rows, k):
    return land.at[pl.ds(pl.multiple_of(k * rows, 8), rows)]


def _gather2_first(srcs, lands, after, name):
    n = len(srcs)

    def body(*refs):
        src_refs, land_refs = refs[:n], refs[n:2 * n]
        outs = refs[2 * n + 1:]
        send, recv_sib, recv_far, token = outs[:n], outs[n:2 * n], outs[2 * n:3 * n], outs[-1]
        me, (sib, _), far = _chip_peers()
        for a in range(n):
            rows = src_refs[a].shape[0]
            mine = _block(land_refs[a], rows, me)
            pltpu.make_async_copy(src_refs[a], mine, send[a].at[4]).start()
            pltpu.make_async_remote_copy(src_ref=src_refs[a], dst_ref=mine, send_sem=send[a].at[0],
                                         recv_sem=recv_sib[a].at[0], device_id=sib, device_id_type=MESH).start()
            for j, (peer, _, _) in enumerate(far):
                pltpu.make_async_remote_copy(src_ref=src_refs[a], dst_ref=mine, send_sem=send[a].at[1 + j],
                                             recv_sem=recv_far[a].at[j], device_id=peer, device_id_type=MESH).start()
        token[...] = jnp.zeros_like(token)

    hbm = lambda a: pltpu.with_memory_space_constraint(a, pltpu.HBM)
    res = _call(
        body, name=name,
        in_specs=[HBM] * (2 * n) + [ANY],
        out_specs=[SEM] * (3 * n) + [HBM] * (2 * n) + [pl.BlockSpec(memory_space=pltpu.VMEM)],
        out_shape=[pltpu.SemaphoreType.DMA((5,))] * n + [pltpu.SemaphoreType.DMA((1,))] * n
                  + [pltpu.SemaphoreType.DMA((3,))] * n
                  + [pltpu.HBM(a.shape, a.dtype) for a in list(srcs) + list(lands)]
                  + [jax.ShapeDtypeStruct((8, 128), F32)],
        input_output_aliases={i: 3 * n + i for i in range(2 * n)},
        compiler_params=pltpu.CompilerParams(has_side_effects=EFFECT),
    )(*[hbm(a) for a in srcs], *[hbm(a) for a in lands], after)
    return res[:n], res[n:2 * n], res[2 * n:3 * n], res[3 * n:4 * n], res[4 * n:5 * n], res[-1]


def _gather2_forward(recv_far, srcs, lands, after, name):
    n = len(lands)
    after = list(after) if isinstance(after, (list, tuple)) else [after]

    def body(*refs):
        src_refs, land_refs, far_sems = refs[:n], refs[n:2 * n], refs[2 * n:3 * n]
        outs = refs[3 * n + len(after):]
        send, recv, token = outs[:n], outs[n:2 * n], outs[-1]
        _, (sib, _), far = _chip_peers()
        for a in range(n):
            rows = src_refs[a].shape[0]
            for j, (peer, pidx, _) in enumerate(far):
                got = _block(land_refs[a], rows, pidx)
                pltpu.make_async_remote_copy(src_ref=src_refs[a], dst_ref=got, send_sem=send[a].at[j],
                                             recv_sem=far_sems[a].at[j], device_id=peer,
                                             device_id_type=MESH).wait_recv()
                pltpu.make_async_remote_copy(src_ref=got, dst_ref=got, send_sem=send[a].at[j], recv_sem=recv[a].at[j],
                                             device_id=sib, device_id_type=MESH).start()
        token[...] = jnp.zeros_like(token)

    res = _call(
        body, name=name,
        in_specs=[HBM] * (2 * n) + [SEM] * n + [ANY] * len(after),
        out_specs=[SEM] * (2 * n) + [HBM] * n + [pl.BlockSpec(memory_space=pltpu.VMEM)],
        out_shape=[pltpu.SemaphoreType.DMA((3,))] * (2 * n) + [pltpu.HBM(a.shape, a.dtype) for a in lands]
                  + [jax.ShapeDtypeStruct((8, 128), F32)],
        input_output_aliases={n + i: 2 * n + i for i in range(n)},
        compiler_params=pltpu.CompilerParams(has_side_effects=EFFECT),
    )(*srcs, *lands, *recv_far, *after)
    return res[:n], res[n:2 * n], res[2 * n:3 * n], res[-1]


def _gather2_wait(send, recv_sib, fwd_send, fwd_recv, srcs, lands, after, name):
    n = len(lands)

    def body(*refs):
        src_refs, land_refs = refs[:n], refs[n:2 * n]
        s_refs, rs_refs, fs_refs, fr_refs = (refs[(2 + q) * n:(3 + q) * n] for q in range(4))
        me, (sib, sib_idx), far = _chip_peers()
        for a in range(n):
            rows = src_refs[a].shape[0]
            mine = _block(land_refs[a], rows, me)
            pltpu.make_async_copy(src_refs[a], mine, s_refs[a].at[4]).wait()
            to_sib = pltpu.make_async_remote_copy(src_ref=src_refs[a], dst_ref=_block(land_refs[a], rows, sib_idx),
                                                  send_sem=s_refs[a].at[0], recv_sem=rs_refs[a].at[0], device_id=sib,
                                                  device_id_type=MESH)
            to_sib.wait_send()
            to_sib.wait_recv()
            for j, (peer, pidx, pair_idx) in enumerate(far):
                pltpu.make_async_remote_copy(src_ref=src_refs[a], dst_ref=mine, send_sem=s_refs[a].at[1 + j],
                                             recv_sem=fr_refs[a].at[j], device_id=peer,
                                             device_id_type=MESH).wait_send()
                fwd = pltpu.make_async_remote_copy(src_ref=_block(land_refs[a], rows, pidx),
                                                   dst_ref=_block(land_refs[a], rows, pair_idx),
                                                   send_sem=fs_refs[a].at[j], recv_sem=fr_refs[a].at[j], device_id=sib,
                                                   device_id_type=MESH)
                fwd.wait_send()
                fwd.wait_recv()

    res = _call(
        body, name=name,
        in_specs=[HBM] * (2 * n) + [SEM] * (4 * n) + [ANY],
        out_specs=[HBM] * (2 * n),
        out_shape=[pltpu.HBM(a.shape, a.dtype) for a in list(srcs) + list(lands)],
        input_output_aliases={i: i for i in range(2 * n)},
        compiler_params=pltpu.CompilerParams(has_side_effects=EFFECT),
    )(*srcs, *lands, *send, *recv_sib, *fwd_send, *fwd_recv, after)
    return res[n:]


def _exchange_wait(scatter, send_sems, recv_sems, srcs, lands, after, name):
    n = len(srcs)
    after = list(after) if isinstance(after, (list, tuple)) else [after]

    def body(*refs):
        src_refs, land_refs = refs[:n], refs[n:2 * n]
        send_refs, recv_refs = refs[2 * n:3 * n], refs[3 * n:4 * n]
        for send, recv in _peer_copies(scatter, src_refs, land_refs, send_refs, recv_refs):
            send.wait_send()
            recv.wait_recv()
        if not scatter:
            for own in _own_copies(src_refs, land_refs, send_refs):
                own.wait()

    res = _call(
        body, name=name,
        in_specs=[HBM] * (2 * n) + [SEM] * (2 * n) + [ANY] * len(after),
        out_specs=[HBM] * (2 * n),
        out_shape=[pltpu.HBM(a.shape, a.dtype) for a in list(srcs) + list(lands)],
        input_output_aliases={i: i for i in range(2 * n)},
        compiler_params=pltpu.CompilerParams(has_side_effects=EFFECT),
    )(*srcs, *lands, *send_sems, *recv_sems, *after)
    return res[:n], res[n:]


def _row_tile(rows):
    if rows <= 512:
        return rows
    return max([tr for tr in range(16, 513, 16) if rows % tr == 0] or [rows])


def _sum_parts(own, got, me, name):
    _, rows, w = own.shape
    tr = _row_tile(rows)

    def body(me_ref, a_ref, b_ref, o_ref):
        s = a_ref[...].astype(F32)
        for j in range(N_DEV - 1):
            s = s + b_ref[j].astype(F32)
        o_ref[...] = s

    return _call(
        body, name=name,
        grid_spec=pltpu.PrefetchScalarGridSpec(
            num_scalar_prefetch=1, grid=(rows // tr,),
            in_specs=[pl.BlockSpec((None, tr, w), lambda i, me_ref: (me_ref[0], i, 0)),
                      pl.BlockSpec((N_DEV - 1, tr, w), lambda i, me_ref: (0, i, 0))],
            out_specs=pl.BlockSpec((tr, w), lambda i, me_ref: (i, 0))),
        out_shape=jax.ShapeDtypeStruct((rows, w), F32),
        compiler_params=_params("parallel"),
    )(me, own, got)


def _sum_devices(stacked, name):
    k, rows, w = stacked.shape
    tr = _row_tile(rows)

    def body(a_ref, o_ref):
        s = a_ref[0]
        for j in range(1, k):
            s = s + a_ref[j]
        o_ref[...] = s

    return _call(
        body, name=name, grid=(rows // tr,),
        in_specs=[pl.BlockSpec((k, tr, w), lambda i: (0, i, 0))],
        out_specs=pl.BlockSpec((tr, w), lambda i: (i, 0)),
        out_shape=jax.ShapeDtypeStruct((rows, w), F32),
        compiler_params=_params("parallel"),
    )(stacked)


def _adamw_math(w, g, m, v):
    nm = ADAM_B1 * m + (1.0 - ADAM_B1) * g
    nv = ADAM_B2 * v + (1.0 - ADAM_B2) * (g * g)
    m_hat = nm / (1.0 - ADAM_B1 ** ADAM_STEP)
    v_hat = nv / (1.0 - ADAM_B2 ** ADAM_STEP)
    return -ADAM_LR * (m_hat / (jnp.sqrt(v_hat) + ADAM_EPS) + ADAM_WD * w), nm, nv


def _sum_adamw(owns, gots, me, off, rows, w, m, v, name):
    nl = len(owns)
    d = w.shape[-1]
    assert off % rows == 0 and w.shape == (nl, rows, d)

    def body(me_ref, *refs):
        own_refs, got_refs = refs[:nl], refs[nl:2 * nl]
        w_ref, m_ref, v_ref, g_ref, d_ref, nm_ref, nv_ref = refs[2 * nl:]
        for l in range(nl):
            @pl.when(pl.program_id(0) == l)
            def _(l=l):
                gv = own_refs[l][...].astype(F32)
                for r in range(N_DEV - 1):
                    gv = gv + got_refs[l][r].astype(F32)
                g_ref[...] = gv
                d_ref[...], nm_ref[...], nv_ref[...] = _adamw_math(w_ref[...], gv, m_ref[...], v_ref[...])

    layer = pl.BlockSpec((None, rows, d), lambda i, me_ref: (i, 0, 0))
    own = pl.BlockSpec((None, rows, d), lambda i, me_ref: (me_ref[0], off // rows, 0), pipeline_mode=pl.Buffered(1))
    got = pl.BlockSpec((N_DEV - 1, rows, d), lambda i, me_ref: (0, off // rows, 0), pipeline_mode=pl.Buffered(1))
    return _call(
        body, name=name,
        grid_spec=pltpu.PrefetchScalarGridSpec(
            num_scalar_prefetch=1, grid=(nl,),
            in_specs=[own] * nl + [got] * nl + [layer] * 3, out_specs=[layer] * 4),
        out_shape=[jax.ShapeDtypeStruct((nl, rows, d), F32)] * 4,
        compiler_params=_params("arbitrary"),
    )(me, *owns, *gots, w, m, v)


def _adamw(w, g, m, v, name):
    rows, cols = w.shape
    tr = _row_tile(rows)

    def body(w_ref, g_ref, m_ref, v_ref, d_ref, nm_ref, nv_ref):
        d_ref[...], nm_ref[...], nv_ref[...] = _adamw_math(w_ref[...], g_ref[...], m_ref[...], v_ref[...])

    spec = pl.BlockSpec((tr, cols), lambda i: (i, 0))
    return _call(
        body, name=name, grid=(rows // tr,),
        in_specs=[spec] * 4, out_specs=[spec] * 3,
        out_shape=[jax.ShapeDtypeStruct((rows, cols), F32)] * 3,
        compiler_params=_params("parallel"),
    )(w, g, m, v)


SMALL = ("a_ln_g", "a_ln_b", "a_w_s", "a_b_s", "mix_pre_g", "mix_post_g", "ffn_pre_g", "ffn_post_g")


def _pack_small(parts, d, last_row=None):
    rows = [parts[k].reshape(-1, d) for k in SMALL] + ([] if last_row is None else [last_row])
    flat = jnp.concatenate(rows, axis=0)
    return jnp.pad(flat, ((0, -flat.shape[0] % 8), (0, 0)))


def _unpack_small(flat, like):
    out, r = {}, 0
    for k in SMALL:
        n = like[k].size // flat.shape[1]
        out[k] = flat[r:r + n].reshape(like[k].shape)
        r += n
    return out


def kernel(x, a_w_in, a_ln_g, a_ln_b, a_w_s, a_b_s, a_w_out, b_w_in, b_w_grp, b_scale, b_w_out, mix_pre_g, mix_post_g, ffn_pre_g, ffn_post_g, ffn_w_gate, ffn_w_up, ffn_w_down, loss_target, m_a_w_in, m_a_ln_g, m_a_ln_b, m_a_w_s, m_a_b_s, m_a_w_out, m_b_w_in, m_b_w_grp, m_b_scale, m_b_w_out, m_mix_pre_g, m_mix_post_g, m_ffn_pre_g, m_ffn_post_g, m_ffn_w_gate, m_ffn_w_up, m_ffn_w_down, v_a_w_in, v_a_ln_g, v_a_ln_b, v_a_w_s, v_a_b_s, v_a_w_out, v_b_w_in, v_b_w_grp, v_b_scale, v_b_w_out, v_mix_pre_g, v_mix_post_g, v_ffn_pre_g, v_ffn_post_g, v_ffn_w_gate, v_ffn_w_up, v_ffn_w_down):
    args = dict(locals())
    names = ("a_w_in", "a_ln_g", "a_ln_b", "a_w_s", "a_b_s", "a_w_out", "b_w_in", "b_w_grp", "b_scale", "b_w_out",
             "mix_pre_g", "mix_post_g", "ffn_pre_g", "ffn_post_g", "ffn_w_gate", "ffn_w_up", "ffn_w_down")
    w = {k: args[k] for k in names}
    mom = {k: args["m_" + k] for k in names}
    var = {k: args["v_" + k] for k in names}

    t, d = x.shape[1], x.shape[2]
    ffn_local = ffn_w_gate.shape[2]
    lay = _Layout(d, ffn_local)
    me = 4 * lax.axis_index("x") + 2 * lax.axis_index("y") + lax.axis_index("c")
    me1 = jnp.reshape(me, (1,)).astype(jnp.int32)

    def landing(block):
        return lax.empty((N_DEV * block.shape[0],) + block.shape[1:], block.dtype)

    def shards(i, mixer, zero):
        j = i // 2
        if not mixer:
            parts = [ffn_w_gate[i].T, ffn_w_up[i].T, ffn_w_down[i]]
        elif i % 2 == 0:
            parts = [a_w_in[j].T, a_w_out[j]]
        else:
            parts = [b_w_in[j], b_w_out[j]]
        return [(p + zero).astype(BF16) for p in parts]

    nsub = 2 * DEPTH
    wg = [None] * nsub
    first = shards(0, True, 0.0)
    f_send, f_sib, f_far, first, f_zones, f_token = _gather2_first(
        first, [landing(b) for b in first], jnp.zeros((8, 128), F32), "gather_first_start")
    zero = f_token[0, 0]
    ngrp = len(B_WINDOWS)
    grp_local = b_w_grp.shape[2]
    sdev = b_scale.shape[1]
    side_rows = 2 * ngrp * grp_local
    side = jnp.concatenate(
        [b_w_grp.reshape(side_rows, B_GROUP_DIM),
         jnp.pad(b_scale, ((0, 6), (0, B_GROUP_DIM - sdev)))], axis=0) + zero
    later, where = [side], [slice(0, 1)]
    for k in range(1, nsub):
        new = shards(k // 2, k % 2 == 0, zero)
        where.append(slice(len(later), len(later) + len(new)))
        later += new
    send_sems, recv_sems, later, zones, token = _exchange_start(
        False, later, [landing(b) for b in later], f_token, "gather_start")
    turned = ("ffn_w_gate", "ffn_w_up")
    turn = lambda a: jnp.swapaxes(a, 1, 2)
    state = {k: tuple(turn(a[k]) for a in (w, mom, var)) for k in turned}
    state["small"] = tuple(_pack_small(a, d) for a in (w, mom, var))
    ready = [a for group in state.values() for a in group]
    fwd_send, fwd_recv, f_zones, fwd_token = _gather2_forward(f_far, first, f_zones, [token] + ready,
                                                              "gather_first_forward")
    wg[0] = _gather2_wait(f_send, f_sib, fwd_send, fwd_recv, first, f_zones, fwd_token, "gather_first_wait")

    def gathered(k, after):
        s = where[k]
        _, got = _exchange_wait(False, send_sems[s], recv_sems[s], later[s], zones[s], after, f"gather_wait_{k}")
        return got

    row = lambda a: a.reshape(1, -1)
    bst = jnp.transpose(a_b_s, (0, 2, 1))

    tm = 256 if t % 256 == 0 else CHUNK
    tm_abwd = tm
    tm_afwd = 512 if t % 512 == 0 else tm
    tm_b = 512 if t % 512 == 0 else tm
    tm_f = tm

    saved = []
    h = x[0]
    wgrp_full = scale_full = None
    for i in range(DEPTH):
        j = i // 2
        gpre = row(mix_pre_g[i])
        if i > 0:
            wg[2 * i] = gathered(2 * i, h)
        if i % 2 == 0:
            x1, h1, gp, u, vh, rs, gated, m = _a_fwd(h, gpre, wg[2 * i], lay, j, row(a_ln_g[j]), row(a_ln_b[j]),
                                                     a_w_s[j], bst[j], row(mix_post_g[i]), tm_afwd, f"a_fwd_{j}")
            mix = dict(h1=h1, gp=gp, u=u, vh=vh, rs=rs, gated=gated, m=m)
        else:
            if wgrp_full is None:
                side_g = gathered(0, h)[0].reshape(N_DEV, side_rows + 8, B_GROUP_DIM)
                wgrp_full = (side_g[:, :side_rows].reshape(N_DEV, 2, ngrp, grp_local, B_GROUP_DIM)
                             .transpose(1, 2, 0, 3, 4).reshape(2, ngrp, B_GROUP_DIM, B_GROUP_DIM).astype(BF16))
                scale_full = (side_g[:, side_rows:side_rows + 2, :sdev].transpose(1, 0, 2)
                              .reshape(2, 1, N_DEV * sdev))
            x1, h1, pooled, mixed, m = _b_fwd(h, gpre, wg[2 * i], lay, j, wgrp_full[j], scale_full[j],
                                              row(mix_post_g[i]), tm_b, f"b_fwd_{j}")
            mix = dict(h1=h1, pooled=pooled, mixed=mixed, m=m)
        wg[2 * i + 1] = gathered(2 * i + 1, x1)
        x2, h2, acts, f, *loss_acc = _f_fwd(x1, row(ffn_pre_g[i]), wg[2 * i + 1], lay, i, row(ffn_post_g[i]), tm_f,
                                               f"f_fwd_{i}", loss_target[0] if i == DEPTH - 1 else None)
        saved.append(dict(x=h, x1=x1, mix=mix, h2=h2, acts=acts, f=f))
        h = x2
    dy, (loss_acc,) = h, loss_acc

    small_g = {k: [None] * w[k].shape[0] for k in SMALL}
    dgrp, dscale = [None, None], [None, None]
    pending = [None] * nsub
    token = jnp.zeros((8, 128), F32)

    def scatter(ks, gbufs):
        gots = [pltpu.with_memory_space_constraint(lax.empty((N_DEV - 1,) + g.shape[1:], g.dtype), pltpu.HBM)
                for g in gbufs]
        ss, rs, src, zone, tok = _exchange_start(True, gbufs, gots, token, f"scatter_start_{ks[0]}")
        for n, k in enumerate(ks):
            pending[k] = (ss[n:n + 1], rs[n:n + 1], src[n:n + 1], zone[n:n + 1])
        return tok

    def small_exchanges():
        side_grad = jnp.concatenate(
            [jnp.stack(dgrp).reshape(2, ngrp, N_DEV, grp_local, B_GROUP_DIM).transpose(2, 0, 1, 3, 4)
             .reshape(N_DEV, side_rows, B_GROUP_DIM),
             jnp.pad(jnp.stack(dscale).reshape(2, N_DEV, sdev).transpose(1, 0, 2),
                     ((0, 0), (0, 6), (0, B_GROUP_DIM - sdev)))], axis=1)
        small_part = _pack_small({k: jnp.stack(small_g[k]) for k in SMALL}, d,
                                 jnp.broadcast_to(loss_acc[:1, :1], (1, d)))
        got = pltpu.with_memory_space_constraint(lax.empty((N_DEV - 1,) + side_grad.shape[1:], F32), pltpu.HBM)
        side_x = _exchange_start(True, [side_grad], [got], token, "side_scatter_start")
        small_x = _exchange_start(False, [small_part], [landing(small_part)], side_x[4], "small_gather_start")
        return side_x[:4], small_x[:4], small_x[4]

    for i in reversed(range(DEPTH)):
        sv = saved[i]
        j = i // 2
        wf, wm = wg[2 * i + 1], wg[2 * i]
        dx1, df, dacts, dgpost, dgpre = _f_bwd(dy, sv["f"], sv["x1"], sv["acts"], row(ffn_pre_g[i]), wf, lay, i,
                                               row(ffn_post_g[i]), token, tm_f, f"f_bwd_{i}")
        small_g["ffn_post_g"][i], small_g["ffn_pre_g"][i] = dgpost[0], dgpre[0]
        gbuf_f = _grad_ffn(dacts, sv["h2"], df, lay.ffn_rows, f"g_ffn_{i}")
        if i == 0:
            token = scatter([2 * i + 1], [gbuf_f])
        mix = sv["mix"]
        gpost = row(mix_post_g[i])
        if i % 2 == 0:
            dx, dm, dz, dgpost, dgpre, dlng, dlnb, dws, dbt = _a_bwd(
                dx1, mix["m"], sv["x"], mix["gp"], mix["u"], mix["vh"], mix["rs"], row(mix_pre_g[i]), wm, lay, j,
                row(a_ln_g[j]), row(a_ln_b[j]), a_w_s[j], bst[j], gpost, token, tm_abwd, f"a_bwd_{j}")
            small_g["a_ln_g"][j], small_g["a_ln_b"][j] = dlng[0], dlnb[0]
            small_g["a_w_s"][j], small_g["a_b_s"][j] = dws, dbt[:, :A_GROUPS].T
            small_g["mix_post_g"][i], small_g["mix_pre_g"][i] = dgpost[0], dgpre[0]
            order = None
            if i == 0:
                side_x, small_x, order = small_exchanges()
            gbuf = _grad_into(lay.a_total, dz, mix["h1"], lay.a_in[j], lay.a_in_rows, f"g_a_in_{j}", after=order)
            gbuf = _grad_into(gbuf, mix["gated"], dm, lay.a_out[j], lay.a_out_rows, f"g_a_out_{j}")
        else:
            dx, dm, draw, dp, dgpost, dgpre, dsc = _b_bwd(
                dx1, mix["m"], sv["x"], mix["pooled"], row(mix_pre_g[i]), wm, lay, j, wgrp_full[j], scale_full[j],
                gpost, token, tm_b, f"b_bwd_{j}")
            dscale[j] = dsc[0]
            gbuf, dgrp[j] = _grad_pool_mixer(mix["h1"], dp, mix["mixed"], dm, mix["pooled"], draw, lay.b_rows,
                                             f"g_b_{j}")
            small_g["mix_post_g"][i], small_g["mix_pre_g"][i] = dgpost[0], dgpre[0]
        token = scatter([2 * i], [gbuf]) if i == 0 else scatter([2 * i + 1, 2 * i], [gbuf_f, gbuf])
        dy = dx
    grad_x = dy[None]

    g_sub = [None] * nsub

    parts = [None] * nsub

    def arrived(k, after):
        ss, rs, src, zone = pending[k]
        (own,), (got,) = _exchange_wait(True, ss, rs, src, zone, after, f"scatter_wait_{k}")
        parts[k] = (own, got)
        if k % 4 == 0:
            g_sub[k] = _sum_parts(own, got, me1, f"sum_grads_{k}")

    def fused_update(k, subs, off, rows):
        back = turn if k in turned else (lambda a: a)
        wk, mk, vk = state[k] if k in turned else (w[k], mom[k], var[k])
        out = _sum_adamw([parts[s][0] for s in subs], [parts[s][1] for s in subs], me1, off, rows, wk, mk, vk,
                         f"update_{k}")
        grads[k], delta[k], new_m[k], new_v[k] = (back(a) for a in out)

    def rows_of(k, off, n):
        return g_sub[k][off:off + n]

    grads, delta, new_m, new_v = {}, {}, {}, {}

    def update(k):
        shape = w[k].shape
        two = lambda a: a.reshape(-1, shape[-1])
        dl, nm, nv = _adamw(two(w[k]), two(grads[k]), two(mom[k]), two(var[k]), f"adamw_{k}")
        delta[k], new_m[k], new_v[k] = dl.reshape(shape), nm.reshape(shape), nv.reshape(shape)

    for k in range(1, nsub):
        arrived(k, token)
    ffn_subs = [2 * l + 1 for l in range(DEPTH)]
    fused_update("ffn_w_gate", ffn_subs, lay.gate[0], ffn_local)
    fused_update("ffn_w_up", ffn_subs, lay.up[0], ffn_local)
    fused_update("ffn_w_down", ffn_subs, lay.down[0], ffn_local)
    fused_update("b_w_in", [2, 6], lay.b_in[0], lay.b_rows)
    fused_update("b_w_out", [2, 6], lay.b_out[0], lay.b_rows)
    early = ("ffn_w_gate", "ffn_w_up", "ffn_w_down", "b_w_in", "b_w_out")

    (side_own,), (side_got,) = _exchange_wait(True, *side_x, [delta[k] for k in early], "side_scatter_wait")
    g_side = _sum_parts(side_own, side_got, me1, "sum_side")
    grads["b_w_grp"] = g_side[:side_rows].reshape(b_w_grp.shape)
    grads["b_scale"] = g_side[side_rows:side_rows + 2, :sdev]
    update("b_w_grp")
    update("b_scale")
    _, (small_all,) = _exchange_wait(False, *small_x, [delta["b_w_grp"], delta["b_scale"]], "small_gather_wait")
    small_sum = _sum_devices(small_all.reshape(N_DEV, -1, d), "sum_small")
    g_small = _unpack_small(small_sum, w)
    loss = small_sum[sum(w[k].size for k in SMALL) // d, 0]
    grads.update(g_small)
    dl, nm, nv = _adamw(state["small"][0], _pack_small(g_small, d), state["small"][1], state["small"][2],
                        "adamw_small")
    delta.update(_unpack_small(dl, w))
    new_m.update(_unpack_small(nm, w))
    new_v.update(_unpack_small(nv, w))

    arrived(0, dl)
    grads["a_w_in"] = jnp.stack([rows_of(4 * j, lay.a_in[j], lay.a_in_rows).T for j in range(2)])
    update("a_w_in")
    fused_update("a_w_out", [0, 4], lay.a_out[0], lay.a_out_rows)

    return (loss, grad_x, *[grads[k] for k in names], *[delta[k] for k in names], *[new_m[k] for k in names],
            *[new_v[k] for k in names])
```

```python
import math

import jax
import jax.numpy as jnp
from jax import lax
from jax.experimental import pallas as pl
from jax.experimental.pallas import tpu as pltpu

F32 = jnp.float32
BF16 = jnp.bfloat16
MESH = pl.DeviceIdType.MESH
ANY = pl.BlockSpec(memory_space=pl.ANY)

N_DEV = 8
EPS = 1e-6
CHUNK = 128
A_GROUPS = 8
A_GROUP_DIM = 256
B_WINDOWS = (2, 4, 8, 16)
B_GROUP_DIM = 256
HALO = 16
DEPTH = 4

ADAM_LR = 0.001
ADAM_B1 = 0.9
ADAM_B2 = 0.999
ADAM_EPS = 1e-08
ADAM_WD = 0.01
ADAM_STEP = 10

VMEM_LIMIT_BYTES = 60 * 1024 * 1024

INV_SQRT2 = 1.0 / math.sqrt(2.0)
LOG2_E = 1.0 / math.log(2.0)
INV_SQRT_2PI = 1.0 / math.sqrt(2.0 * math.pi)


def _call(body, **kw):
    return pl.pallas_call(body, **kw)


def _params(*semantics):
    return pltpu.CompilerParams(dimension_semantics=semantics or None, vmem_limit_bytes=VMEM_LIMIT_BYTES)


def _resident(shape, index):
    return pl.BlockSpec(shape, lambda *_: index, pipeline_mode=pl.Buffered(1))


def _rows(tm, width):
    return pl.BlockSpec((tm, width), lambda i: (i, 0))


def _nn(a, b):
    return jnp.dot(a, b, preferred_element_type=F32)


def _nt(a, b):
    return lax.dot_general(a, b, (((1,), (1,)), ((), ())), preferred_element_type=F32)


def _tn(a, b):
    return lax.dot_general(a, b, (((0,), (0,)), ((), ())), preferred_element_type=F32)


def _rms_fwd(x, g):
    r = lax.rsqrt(jnp.mean(x * x, axis=-1, keepdims=True) + EPS)
    return x * r * g


def _rms_bwd(x, g, dy):
    r = lax.rsqrt(jnp.mean(x * x, axis=-1, keepdims=True) + EPS)
    xh = x * r
    dg = jnp.sum(dy * xh, axis=0, keepdims=True)
    dxh = dy * g
    dx = r * (dxh - xh * jnp.mean(dxh * xh, axis=-1, keepdims=True))
    return dx, dg


SLAB = 16


def _slabs(n):
    return [slice(r, r + SLAB) for r in range(0, n, SLAB)]


def _rms_bwd_slabs(x_at, dy_at, g, n, n_sum, emit):
    acc = jnp.zeros((8, g.shape[1]), F32)
    for rows in _slabs(n):
        x = x_at(rows)
        dy = dy_at(rows)
        r = lax.rsqrt(jnp.mean(x * x, axis=-1, keepdims=True) + EPS)
        xh = x * r
        if rows.start < n_sum:
            p = dy * xh
            acc = acc + p[:8] + p[8:]
        dxh = dy * g
        emit(rows, r * (dxh - xh * jnp.mean(dxh * xh, axis=-1, keepdims=True)))
    return jnp.sum(acc, axis=0, keepdims=True)


def _gelu(z):
    phi = 0.5 + 0.5 * lax.erf(z * INV_SQRT2)
    e = jnp.exp2(z * z * (-0.5 * LOG2_E))
    return z * phi, phi + z * e * INV_SQRT_2PI


def _layernorm_stats(v):
    mu = jnp.mean(v, axis=-1, keepdims=True)
    xc = v - mu
    rs = lax.rsqrt(jnp.mean(xc * xc, axis=-1, keepdims=True) + EPS)
    return xc * rs, rs


def _tril_mask():
    r = lax.broadcasted_iota(jnp.int32, (CHUNK, CHUNK), 0)
    c = lax.broadcasted_iota(jnp.int32, (CHUNK, CHUNK), 1)
    return r >= c


class _Layout:
    def __init__(self, d, ffn_rows):
        self.ffn_rows = ffn_rows
        self.gate, self.up, self.down = [0] * DEPTH, [self.ffn_rows] * DEPTH, [2 * self.ffn_rows] * DEPTH
        self.f_total = 3 * self.ffn_rows
        self.a_in_rows, self.a_out_rows, self.b_rows = 4 * d // N_DEV, 2 * d // N_DEV, d // N_DEV
        self.a_in, self.a_out = [0, 0], [self.a_in_rows] * 2
        self.a_total = self.a_in_rows + self.a_out_rows
        self.b_in, self.b_out = [0, 0], [self.b_rows] * 2
        self.b_total = 2 * self.b_rows


def _wspec(rows, d):
    return _resident((N_DEV * rows, d), (0, 0))


def _a_fwd(x, gpre, wg, lay, j, lng, lnb, ws, bst, gpost, tm, name):
    t, d = x.shape
    aw = 2 * d
    nch = tm // CHUNK

    def body(x_ref, gpre_ref, win_ref, lng_ref, lnb_ref, ws_ref, bst_ref, wout_ref, gpost_ref,
             x1_ref, h1_ref, gp_ref, u_ref, vh_ref, rs_ref, gated_ref, m_ref):
        xv = x_ref[...]
        h1 = _rms_fwd(xv, gpre_ref[...]).astype(BF16)
        h1_ref[...] = h1
        z = _nt(h1, win_ref[...])
        u, du_dz = _gelu(z[:, :aw])
        v, dv_dz = _gelu(z[:, aw:])
        gp_ref[:, :aw] = du_dz.astype(BF16)
        gp_ref[:, aw:] = dv_dz.astype(BF16)
        u_ref[...] = u.astype(BF16)
        vh, rs = _layernorm_stats(v)
        vh_ref[...] = vh.astype(BF16)
        rs_ref[...] = jnp.broadcast_to(rs, rs_ref.shape)
        vn = (vh * lng_ref[...] + lnb_ref[...]).astype(BF16)
        mask = _tril_mask()
        for g in range(A_GROUPS):
            wm = jnp.where(mask, ws_ref[g], 0.0).astype(BF16)
            cols = slice(g * A_GROUP_DIM, (g + 1) * A_GROUP_DIM)
            for c in range(nch):
                rows = slice(c * CHUNK, (c + 1) * CHUNK)
                sv = _nn(wm, vn[rows, cols]) + bst_ref[:, g:g + 1]
                gated_ref[rows, cols] = (u[rows, cols] * sv).astype(BF16)
        m = _nn(gated_ref[...], wout_ref[...])
        m_ref[...] = m
        x1_ref[...] = xv + _rms_fwd(m, gpost_ref[...])

    vec = lambda w: _resident((1, w), (0, 0))
    return _call(
        body, name=name, grid=(t // tm,),
        in_specs=[_rows(tm, d), vec(d), _wspec(lay.a_in_rows, d), vec(aw), vec(aw),
                  _resident((A_GROUPS, CHUNK, CHUNK), (0, 0, 0)), _resident((CHUNK, A_GROUPS), (0, 0)),
                  _wspec(lay.a_out_rows, d), vec(d)],
        out_specs=[_rows(tm, d), _rows(tm, d), _rows(tm, 2 * aw), _rows(tm, aw), _rows(tm, aw), _rows(tm, 128),
                   _rows(tm, aw), _rows(tm, d)],
        out_shape=[jax.ShapeDtypeStruct((t, d), F32), jax.ShapeDtypeStruct((t, d), BF16),
                   jax.ShapeDtypeStruct((t, 2 * aw), BF16), jax.ShapeDtypeStruct((t, aw), BF16),
                   jax.ShapeDtypeStruct((t, aw), BF16), jax.ShapeDtypeStruct((t, 128), F32),
                   jax.ShapeDtypeStruct((t, aw), BF16), jax.ShapeDtypeStruct((t, d), F32)],
        compiler_params=_params("parallel"),
    )(x, gpre, wg[0], lng, lnb, ws, bst, wg[1], gpost)


def _a_bwd(dx1, m, x, gp, u, vh, rs, gpre, wg, lay, j, lng, lnb, ws, bst, gpost, after, tm, name):
    t, d = x.shape
    aw = 2 * d
    nch = tm // CHUNK

    steps = t // tm
    ring = 3

    def body(dx1_ref, m_ref, x_ref, gp_hbm, u_ref, vh_ref, rs_ref, gpre_ref, win_ref, lng_ref, lnb_ref, ws_ref, bst_ref,
             wout_ref, gpost_ref, after_ref,
             dx_ref, dm_ref, dz_ref, dgpost_ref, dgpre_ref, dlng_ref, dlnb_ref, dws_ref, dbt_ref, dvn_ref,
             gp_buf, gp_sems):
        i = pl.program_id(0)

        def fetch(step, slot):
            src = gp_hbm.at[pl.ds(pl.multiple_of(step * tm, tm), tm), :]
            return pltpu.make_async_copy(src, gp_buf.at[slot], gp_sems.at[slot])

        @pl.when(i == 0)
        def _():
            for r in (dgpost_ref, dgpre_ref, dlng_ref, dlnb_ref, dws_ref, dbt_ref):
                r[...] = jnp.zeros_like(r)
            for s in range(min(ring - 1, steps)):
                fetch(s, s).start()

        @pl.when(i + ring - 1 < steps)
        def _():
            fetch(i + ring - 1, (i + ring - 1) % ring).start()

        fetch(i, i % ring).wait()
        gp_ref = gp_buf.at[i % ring]

        def put_dm(rows, dx):
            dm_ref[rows, :] = dx.astype(BF16)

        dgpost_ref[...] += _rms_bwd_slabs(lambda rows: m_ref[rows, :], lambda rows: dx1_ref[rows, :], gpost_ref[...],
                                          tm, tm, put_dm)
        dgated = _nt(dm_ref[...], wout_ref[...])

        vh = vh_ref[...].astype(F32)
        rs = rs_ref[:, :1]
        lng_v = lng_ref[...]
        vn = (vh * lng_v + lnb_ref[...]).astype(BF16)
        mask = _tril_mask()
        lane = lax.broadcasted_iota(jnp.int32, (CHUNK, CHUNK), 1)
        for g in range(A_GROUPS):
            wm = jnp.where(mask, ws_ref[g], 0.0).astype(BF16)
            cols = slice(g * A_GROUP_DIM, (g + 1) * A_GROUP_DIM)
            dws_g = jnp.zeros((CHUNK, CHUNK), F32)
            db_g = jnp.zeros((CHUNK, 1), F32)
            for c in range(nch):
                rows = slice(c * CHUNK, (c + 1) * CHUNK)
                vn_cg = vn[rows, cols]
                sv = _nn(wm, vn_cg) + bst_ref[:, g:g + 1]
                dg_cg = dgated[rows, cols]
                dsv = dg_cg * u_ref[rows, cols].astype(F32)
                dsv_bf = dsv.astype(BF16)
                db_g = db_g + jnp.sum(dsv, axis=1, keepdims=True)
                dws_g = dws_g + _nt(dsv_bf, vn_cg)
                dvn_ref[rows, cols] = _tn(wm, dsv_bf)
                dz_ref[rows, cols] = (dg_cg * sv * gp_ref[rows, cols].astype(F32)).astype(BF16)
            dws_ref[g] += jnp.where(mask, dws_g, 0.0)
            dbt_ref[...] += jnp.where(lane == g, db_g, 0.0)
        dvn = dvn_ref[...]
        dlng_ref[...] += jnp.sum(dvn * vh, axis=0, keepdims=True)
        dlnb_ref[...] += jnp.sum(dvn, axis=0, keepdims=True)
        dvh = dvn * lng_v
        dv = rs * (dvh - jnp.mean(dvh, axis=-1, keepdims=True) - vh * jnp.mean(dvh * vh, axis=-1, keepdims=True))
        dz_ref[:, aw:] = (dv * gp_ref[:, aw:].astype(F32)).astype(BF16)
        dh1 = _nn(dz_ref[...], win_ref[...])

        def put_dx(rows, dx):
            dx_ref[rows, :] = dx1_ref[rows, :] + dx

        dgpre_ref[...] += _rms_bwd_slabs(lambda rows: x_ref[rows, :], lambda rows: dh1[rows, :], gpre_ref[...],
                                         tm, tm, put_dx)

    vec = lambda w: _resident((1, w), (0, 0))
    acc = lambda shape: pl.BlockSpec(shape, lambda i: (0,) * len(shape))
    return _call(
        body, name=name, grid=(t // tm,),
        in_specs=[_rows(tm, d), _rows(tm, d), _rows(tm, d), ANY, _rows(tm, aw), _rows(tm, aw),
                  _rows(tm, 128), vec(d), _wspec(lay.a_in_rows, d), vec(aw), vec(aw),
                  _resident((A_GROUPS, CHUNK, CHUNK), (0, 0, 0)), _resident((CHUNK, A_GROUPS), (0, 0)),
                  _wspec(lay.a_out_rows, d), vec(d), ANY],
        out_specs=[_rows(tm, d), _rows(tm, d), _rows(tm, 2 * aw), acc((1, d)), acc((1, d)), acc((1, aw)), acc((1, aw)),
                   acc((A_GROUPS, CHUNK, CHUNK)), acc((CHUNK, CHUNK))],
        out_shape=[jax.ShapeDtypeStruct((t, d), F32), jax.ShapeDtypeStruct((t, d), BF16),
                   jax.ShapeDtypeStruct((t, 2 * aw), BF16), jax.ShapeDtypeStruct((1, d), F32),
                   jax.ShapeDtypeStruct((1, d), F32), jax.ShapeDtypeStruct((1, aw), F32),
                   jax.ShapeDtypeStruct((1, aw), F32), jax.ShapeDtypeStruct((A_GROUPS, CHUNK, CHUNK), F32),
                   jax.ShapeDtypeStruct((CHUNK, CHUNK), F32)],
        scratch_shapes=[pltpu.VMEM((tm, aw), F32), pltpu.VMEM((ring, tm, 2 * aw), BF16),
                        pltpu.SemaphoreType.DMA((ring,))],
        compiler_params=_params("arbitrary"),
    )(dx1, m, x, gp, u, vh, rs, gpre, wg[0], lng, lnb, ws, bst, wg[1], gpost, after)


def _window_counts(first_row, n, win):
    tpos = first_row + lax.broadcasted_iota(jnp.int32, (n, 1), 0)
    return jnp.clip(tpos + 1, 1, win).astype(F32)


def _b_fwd(x, gpre, wg, lay, j, wgrp, scale, gpost, tm, name):
    t, d = x.shape
    n = tm + HALO
    ngrp = len(B_WINDOWS)

    def body(x_ref, xprev_ref, gpre_ref, win_ref, wgrp_ref, scale_ref, wout_ref, gpost_ref,
             x1_ref, h1_ref, pooled_ref, mixed_ref, m_ref):
        i = pl.program_id(0)
        xv = x_ref[...]
        keep = jnp.where(i > 0, 1.0, 0.0)
        xe = jnp.concatenate([xprev_ref[...] * keep, xv], axis=0)
        h1e = _rms_fwd(xe, gpre_ref[...]).astype(BF16)
        h1_ref[...] = h1e[HALO:]
        p = _nn(h1e, win_ref[...])
        acc = p
        shift = 1
        for g, win in enumerate(B_WINDOWS):
            lo = g * B_GROUP_DIM
            if g > 0:
                acc = acc[:, B_GROUP_DIM:]
            while shift < win:
                acc = acc + pltpu.roll(acc, shift, 0)
                shift *= 2
            cnt = _window_counts(i * tm - HALO, n, win)
            pooled = acc[:, :B_GROUP_DIM] / cnt - p[:, lo:lo + B_GROUP_DIM]
            pooled_ref[:, lo:lo + B_GROUP_DIM] = pooled[HALO:].astype(BF16)
        for g in range(ngrp):
            cols = slice(g * B_GROUP_DIM, (g + 1) * B_GROUP_DIM)
            raw = _nn(pooled_ref[:, cols], wgrp_ref[g])
            mixed_ref[:, cols] = (raw * scale_ref[:, cols]).astype(BF16)
        m = _nn(mixed_ref[...], wout_ref[...])
        m_ref[...] = m
        x1_ref[...] = xv + _rms_fwd(m, gpost_ref[...])

    vec = lambda w: _resident((1, w), (0, 0))
    per = tm // HALO
    return _call(
        body, name=name, grid=(t // tm,),
        in_specs=[_rows(tm, d), pl.BlockSpec((HALO, d), lambda i: (jnp.maximum(i * per - 1, 0), 0)), vec(d),
                  _wspec(lay.b_rows, d), _resident((ngrp, B_GROUP_DIM, B_GROUP_DIM), (0, 0, 0)), vec(d),
                  _wspec(lay.b_rows, d), vec(d)],
        out_specs=[_rows(tm, d)] * 5,
        out_shape=[jax.ShapeDtypeStruct((t, d), F32), jax.ShapeDtypeStruct((t, d), BF16),
                   jax.ShapeDtypeStruct((t, d), BF16), jax.ShapeDtypeStruct((t, d), BF16),
                   jax.ShapeDtypeStruct((t, d), F32)],
        compiler_params=_params("parallel"),
    )(x, x, gpre, wg[0], wgrp, scale, wg[1], gpost)


def _b_bwd(dx1, m, x, pooled, gpre, wg, lay, j, wgrp, scale, gpost, after, tm, name):
    t, d = x.shape
    n = tm + HALO
    ngrp = len(B_WINDOWS)
    steps = t // tm

    def body(dx1_ref, dx1n_ref, m_ref, mn_ref, x_ref, pooled_ref, pooledn_ref, gpre_ref, win_ref, wgrp_ref, scale_ref,
             wout_ref, gpost_ref, after_ref,
             dx_ref, dm_ref, draw_ref, dp_ref, dgpost_ref, dgpre_ref, dscale_ref, dpool_ref):
        i = pl.program_id(0)

        @pl.when(i == 0)
        def _():
            for r in (dgpost_ref, dgpre_ref, dscale_ref):
                r[...] = jnp.zeros_like(r)

        keep = jnp.where(i < steps - 1, 1.0, 0.0)
        dy = dx1_ref[...]
        dye = jnp.concatenate([dy, dx1n_ref[...] * keep], axis=0)
        me = jnp.concatenate([m_ref[...], mn_ref[...]], axis=0)
        gpost_v = gpost_ref[...]
        r = lax.rsqrt(jnp.mean(me * me, axis=-1, keepdims=True) + EPS)
        mh = me * r
        dgpost_ref[...] += jnp.sum((dye * mh)[:tm], axis=0, keepdims=True)
        dmh = dye * gpost_v
        dme = (r * (dmh - mh * jnp.mean(dmh * mh, axis=-1, keepdims=True))).astype(BF16)
        dm_ref[...] = dme[:tm]
        dmixed = _nt(dme, wout_ref[...])
        pooled_e = jnp.concatenate([pooled_ref[...], pooledn_ref[...]], axis=0)
        scale_v = scale_ref[...]
        for g, win in enumerate(B_WINDOWS):
            cols = slice(g * B_GROUP_DIM, (g + 1) * B_GROUP_DIM)
            raw = _nn(pooled_e[:, cols], wgrp_ref[g])
            dscale_ref[:, cols] += jnp.sum((dmixed[:, cols] * raw)[:tm], axis=0, keepdims=True)
            draw = (dmixed[:, cols] * scale_v[:, cols]).astype(BF16)
            draw_ref[:, cols] = draw[:tm]
            dpool = _nt(draw, wgrp_ref[g])
            acc = dpool / _window_counts(i * tm, n, win)
            shift = 1
            while shift < win:
                acc = acc + pltpu.roll(acc, n - shift, 0)
                shift *= 2
            dpool_ref[:, cols] = (acc - dpool)[:tm]
        dp = dpool_ref[...].astype(BF16)
        dp_ref[...] = dp
        dh1 = _nt(dp, win_ref[...])
        dxp, dgpre = _rms_bwd(x_ref[...], gpre_ref[...], dh1)
        dgpre_ref[...] += dgpre
        dx_ref[...] = dy + dxp

    vec = lambda w: _resident((1, w), (0, 0))
    acc = lambda shape: pl.BlockSpec(shape, lambda i: (0,) * len(shape))
    per = tm // HALO
    nxt = lambda i: (jnp.minimum((i + 1) * per, t // HALO - 1), 0)
    return _call(
        body, name=name, grid=(steps,),
        in_specs=[_rows(tm, d), pl.BlockSpec((HALO, d), nxt), _rows(tm, d), pl.BlockSpec((HALO, d), nxt), _rows(tm, d),
                  _rows(tm, d), pl.BlockSpec((HALO, d), nxt), vec(d), _wspec(lay.b_rows, d),
                  _resident((ngrp, B_GROUP_DIM, B_GROUP_DIM), (0, 0, 0)), vec(d), _wspec(lay.b_rows, d),
                  vec(d), ANY],
        out_specs=[_rows(tm, d)] * 4 + [acc((1, d))] * 3,
        out_shape=[jax.ShapeDtypeStruct((t, d), F32), jax.ShapeDtypeStruct((t, d), BF16),
                   jax.ShapeDtypeStruct((t, d), BF16), jax.ShapeDtypeStruct((t, d), BF16)]
                  + [jax.ShapeDtypeStruct((1, d), F32)] * 3,
        scratch_shapes=[pltpu.VMEM((tm, d), F32)],
        compiler_params=_params("arbitrary"),
    )(dx1, dx1, m, m, x, pooled, pooled, gpre, wg[0], wgrp, scale, wg[1], gpost, after)


def _f_fwd(x1, gpre, wg, lay, l, gpost, tm, name, target=None):
    t, d = x1.shape
    hid = N_DEV * lay.ffn_rows
    head = target is not None

    def body(x_ref, gpre_ref, wgate_ref, wup_ref, wdown_ref, gpost_ref, *rest):
        x2_ref, h2_ref, abs_ref, f_ref = rest[-5:-1] if head else rest
        xv = x_ref[...]
        h2 = _rms_fwd(xv, gpre_ref[...]).astype(BF16)
        h2_ref[...] = h2
        a = _nt(h2, wgate_ref[...])
        b = _nt(h2, wup_ref[...])
        sig = jax.nn.sigmoid(a)
        silu = a * sig
        abs_ref[:, :hid] = (b * (sig + silu * (1.0 - sig))).astype(BF16)
        abs_ref[:, hid:2 * hid] = silu.astype(BF16)
        s = (silu * b).astype(BF16)
        abs_ref[:, 2 * hid:] = s
        f = _nn(s, wdown_ref[...])
        f_ref[...] = f
        x2 = xv + _rms_fwd(f, gpost_ref[...])
        if head:
            target_ref, loss_ref = rest[0], rest[-1]

            @pl.when(pl.program_id(0) == 0)
            def _():
                loss_ref[...] = jnp.zeros_like(loss_ref)

            diff = x2 - target_ref[...]
            x2_ref[...] = diff * (1.0 / d)
            sq = jnp.sum(jnp.sum(diff * diff, axis=0, keepdims=True), axis=1, keepdims=True)
            loss_ref[...] += sq * (0.5 / d)
        else:
            x2_ref[...] = x2

    vec = lambda w: _resident((1, w), (0, 0))
    return _call(
        body, name=name, grid=(t // tm,),
        in_specs=[_rows(tm, d), vec(d), _wspec(lay.ffn_rows, d), _wspec(lay.ffn_rows, d),
                  _wspec(lay.ffn_rows, d), vec(d)] + ([_rows(tm, d)] if head else []),
        out_specs=[_rows(tm, d), _rows(tm, d), _rows(tm, 3 * hid), _rows(tm, d)]
                  + ([pl.BlockSpec((8, 128), lambda i: (0, 0))] if head else []),
        out_shape=[jax.ShapeDtypeStruct((t, d), F32), jax.ShapeDtypeStruct((t, d), BF16),
                   jax.ShapeDtypeStruct((t, 3 * hid), BF16), jax.ShapeDtypeStruct((t, d), F32)]
                  + ([jax.ShapeDtypeStruct((8, 128), F32)] if head else []),
        compiler_params=_params("arbitrary" if head else "parallel"),
    )(x1, gpre, wg[0], wg[1], wg[2], gpost, *([target] if head else []))


def _f_bwd(dx2, f, x1, acts, gpre, wg, lay, l, gpost, after, tm, name):
    t, d = x1.shape
    hid = N_DEV * lay.ffn_rows

    def body(dx2_ref, f_ref, x_ref, ab_ref, gpre_ref, wgate_ref, wup_ref, wdown_ref, gpost_ref, after_ref,
             dx1_ref, df_ref, dab_ref, dgpost_ref, dgpre_ref):
        @pl.when(pl.program_id(0) == 0)
        def _():
            dgpost_ref[...] = jnp.zeros_like(dgpost_ref)
            dgpre_ref[...] = jnp.zeros_like(dgpre_ref)

        def put_df(rows, dx):
            df_ref[rows, :] = dx.astype(BF16)

        dgpost_ref[...] += _rms_bwd_slabs(lambda rows: f_ref[rows, :], lambda rows: dx2_ref[rows, :], gpost_ref[...],
                                          tm, tm, put_df)
        ds = _nt(df_ref[...], wdown_ref[...])
        dab_ref[:, :hid] = (ds * ab_ref[:, :hid].astype(F32)).astype(BF16)
        dab_ref[:, hid:] = (ds * ab_ref[:, hid:].astype(F32)).astype(BF16)
        dh2 = _nn(dab_ref[:, :hid], wgate_ref[...]) + _nn(dab_ref[:, hid:], wup_ref[...])

        def put_dx(rows, dx):
            dx1_ref[rows, :] = dx2_ref[rows, :] + dx

        dgpre_ref[...] += _rms_bwd_slabs(lambda rows: x_ref[rows, :], lambda rows: dh2[rows, :], gpre_ref[...],
                                         tm, tm, put_dx)

    vec = lambda w: _resident((1, w), (0, 0))
    acc = pl.BlockSpec((1, d), lambda i: (0, 0))
    return _call(
        body, name=name, grid=(t // tm,),
        in_specs=[_rows(tm, d), _rows(tm, d), _rows(tm, d), _rows(tm, 2 * hid), vec(d),
                  _wspec(lay.ffn_rows, d), _wspec(lay.ffn_rows, d),
                  _wspec(lay.ffn_rows, d), vec(d), ANY],
        out_specs=[_rows(tm, d), _rows(tm, d), _rows(tm, 2 * hid), acc, acc],
        out_shape=[jax.ShapeDtypeStruct((t, d), F32), jax.ShapeDtypeStruct((t, d), BF16),
                   jax.ShapeDtypeStruct((t, 3 * hid), BF16),
                   jax.ShapeDtypeStruct((1, d), F32), jax.ShapeDtypeStruct((1, d), F32)],
        input_output_aliases={3: 2},
        compiler_params=_params("arbitrary"),
    )(dx2, f, x1, acts, gpre, wg[0], wg[1], wg[2], gpost, after)


def _grad_into(gbuf, lhs, rhs, off, rows, name, after=None):
    t, m = lhs.shape
    d = rhs.shape[1]
    assert m == N_DEV * rows and off % rows == 0
    per_tile = {512: 2, 256: 4}[rows]
    tm = per_tile * rows
    assert tm % 128 == 0 and rows % 16 == 0
    tk = 2048 if t % 2048 == 0 else 256
    ksteps = t // tk
    fresh = isinstance(gbuf, int)
    shape = (N_DEV, gbuf, d) if fresh else gbuf.shape
    extra = ([] if fresh else [gbuf]) + ([] if after is None else [after])

    def body(l_ref, r_ref, *rest):
        o_ref, acc_ref = rest[-2:]
        k = pl.program_id(1)

        @pl.when(k == 0)
        def _():
            acc_ref[...] = jnp.zeros_like(acc_ref)

        acc_ref[...] += _tn(l_ref[...], r_ref[...])

        @pl.when(k == ksteps - 1)
        def _():
            o_ref[...] = acc_ref[...].reshape(per_tile, rows, d).astype(BF16)

    return _call(
        body, name=name, grid=(N_DEV // per_tile, ksteps),
        in_specs=[pl.BlockSpec((tk, tm), lambda i, k: (k, i)), pl.BlockSpec((tk, d), lambda i, k: (k, 0))]
                 + [ANY] * len(extra),
        out_specs=pl.BlockSpec((per_tile, rows, d), lambda i, k: (i, off // rows, 0)),
        out_shape=jax.ShapeDtypeStruct(shape, BF16),
        scratch_shapes=[pltpu.VMEM((tm, d), F32)],
        input_output_aliases={} if fresh else {2: 0},
        compiler_params=_params("parallel", "arbitrary"),
    )(lhs, rhs, *extra)


def _grad_pool_mixer(h1, dp, mixed, dm, pooled, draw, rows, name):
    t, d = h1.shape
    ngrp = len(B_WINDOWS)
    assert d == N_DEV * rows and d == ngrp * B_GROUP_DIM
    tk = 1024 if t % 1024 == 0 else 256
    ksteps = t // tk

    def body(h1_ref, dp_ref, mixed_ref, dm_ref, pooled_ref, draw_ref, o_ref, grp_ref, acc_in, acc_out):
        k = pl.program_id(0)

        @pl.when(k == 0)
        def _():
            acc_in[...] = jnp.zeros_like(acc_in)
            acc_out[...] = jnp.zeros_like(acc_out)
            grp_ref[...] = jnp.zeros_like(grp_ref)

        acc_in[...] += _tn(h1_ref[...], dp_ref[...])
        acc_out[...] += _tn(mixed_ref[...], dm_ref[...])
        for g in range(ngrp):
            cols = slice(g * B_GROUP_DIM, (g + 1) * B_GROUP_DIM)
            grp_ref[g] += _tn(pooled_ref[:, cols], draw_ref[:, cols])

        @pl.when(k == ksteps - 1)
        def _():
            o_ref[:, :rows, :] = acc_in[...].reshape(N_DEV, rows, d).astype(BF16)
            o_ref[:, rows:, :] = acc_out[...].reshape(N_DEV, rows, d).astype(BF16)

    return _call(
        body, name=name, grid=(ksteps,),
        in_specs=[_rows(tk, d)] * 6,
        out_specs=[pl.BlockSpec((N_DEV, 2 * rows, d), lambda k: (0, 0, 0)),
                   pl.BlockSpec((ngrp, B_GROUP_DIM, B_GROUP_DIM), lambda k: (0, 0, 0))],
        out_shape=[jax.ShapeDtypeStruct((N_DEV, 2 * rows, d), BF16),
                   jax.ShapeDtypeStruct((ngrp, B_GROUP_DIM, B_GROUP_DIM), F32)],
        scratch_shapes=[pltpu.VMEM((d, d), F32), pltpu.VMEM((d, d), F32)],
        compiler_params=_params("arbitrary"),
    )(h1, dp, mixed, dm, pooled, draw)


def _grad_ffn(acts, h2, df, rows, name):
    t, d = h2.shape
    per_tile = 4
    tm = per_tile * rows
    tiles = N_DEV // per_tile
    assert acts.shape[1] == 3 * N_DEV * rows and tm % 128 == 0 and rows % 16 == 0
    tk = 2048 if t % 2048 == 0 else 256
    ksteps = t // tk

    def body(l_ref, h2_ref, df_ref, o_ref, acc_ref):
        i, k = pl.program_id(0), pl.program_id(1)

        @pl.when(k == 0)
        def _():
            acc_ref[...] = jnp.zeros_like(acc_ref)

        @pl.when(i < 2 * tiles)
        def _():
            acc_ref[...] += _tn(l_ref[...], h2_ref[...])

        @pl.when(i >= 2 * tiles)
        def _():
            acc_ref[...] += _tn(l_ref[...], df_ref[...])

        @pl.when(k == ksteps - 1)
        def _():
            o_ref[...] = acc_ref[...].reshape(per_tile, rows, d).astype(BF16)

    return _call(
        body, name=name, grid=(3 * tiles, ksteps),
        in_specs=[pl.BlockSpec((tk, tm), lambda i, k: (k, i)),
                  pl.BlockSpec((tk, d), lambda i, k: (jnp.where(i < 2 * tiles, k, ksteps - 1), 0)),
                  pl.BlockSpec((tk, d), lambda i, k: (jnp.where(i >= 2 * tiles, k, 0), 0))],
        out_specs=pl.BlockSpec((per_tile, rows, d), lambda i, k: (i % tiles, i // tiles, 0)),
        out_shape=jax.ShapeDtypeStruct((N_DEV, 3 * rows, d), BF16),
        scratch_shapes=[pltpu.VMEM((tm, d), F32)],
        compiler_params=_params("arbitrary", "arbitrary"),
    )(acts, h2, df)


def _peers():
    x, y, c = lax.axis_index("x"), lax.axis_index("y"), lax.axis_index("c")
    flip = lambda v, f: 1 - v if f else v
    peers = []
    for r in range(1, N_DEV):
        px, py, pc = flip(x, r & 4), flip(y, r & 2), flip(c, r & 1)
        peers.append(((px, py, pc), 4 * px + 2 * py + pc))
    return 4 * x + 2 * y + c, peers


HBM = pl.BlockSpec(memory_space=pltpu.HBM)
SEM = pl.BlockSpec(memory_space=pltpu.SEMAPHORE)
EFFECT = pltpu.SideEffectType.DATAFLOW_SIDE_EFFECTING


def _peer_copies(scatter, srcs, lands, send_sems, recv_sems):
    me, peers = _peers()
    copies = []
    for a in range(len(srcs)):
        rows = srcs[a].shape[0]
        block = lambda k: lands[a].at[pl.ds(pl.multiple_of(k * rows, 8), rows)]
        for r, (peer, pidx) in enumerate(peers):
            src = srcs[a].at[pidx] if scatter else srcs[a]
            mine = lands[a].at[r] if scatter else block(pidx)
            theirs = lands[a].at[r] if scatter else block(me)
            send = pltpu.make_async_remote_copy(src_ref=src, dst_ref=theirs, send_sem=send_sems[a].at[r],
                                                recv_sem=recv_sems[a].at[r], device_id=peer, device_id_type=MESH)
            recv = pltpu.make_async_remote_copy(src_ref=src, dst_ref=mine, send_sem=send_sems[a].at[r],
                                                recv_sem=recv_sems[a].at[r], device_id=peer, device_id_type=MESH)
            copies.append((send, recv))
    return copies


def _own_copies(srcs, lands, send_sems):
    me, _ = _peers()
    copies = []
    for a in range(len(srcs)):
        rows = srcs[a].shape[0]
        copies.append(pltpu.make_async_copy(srcs[a], lands[a].at[pl.ds(pl.multiple_of(me * rows, 8), rows)],
                                            send_sems[a].at[N_DEV - 1]))
    return copies


def _exchange_start(scatter, srcs, lands, after, name):
    n = len(srcs)

    def body(*refs):
        src_refs, land_refs = refs[:n], refs[n:2 * n]
        outs = refs[2 * n + 1:]
        send_sems, recv_sems, token = outs[:n], outs[n:2 * n], outs[-1]
        for send, _ in _peer_copies(scatter, src_refs, land_refs, send_sems, recv_sems):
            send.start()
        if not scatter:
            for own in _own_copies(src_refs, land_refs, send_sems):
                own.start()
        token[...] = jnp.zeros_like(token)

    hbm = lambda a: pltpu.with_memory_space_constraint(a, pltpu.HBM)
    res = _call(
        body, name=name,
        in_specs=[HBM] * (2 * n) + [ANY],
        out_specs=[SEM] * (2 * n) + [HBM] * (2 * n) + [pl.BlockSpec(memory_space=pltpu.VMEM)],
        out_shape=[pltpu.SemaphoreType.DMA((N_DEV,))] * (2 * n)
                  + [pltpu.HBM(a.shape, a.dtype) for a in list(srcs) + list(lands)]
                  + [jax.ShapeDtypeStruct((8, 128), F32)],
        input_output_aliases={i: 2 * n + i for i in range(2 * n)},
        compiler_params=pltpu.CompilerParams(has_side_effects=EFFECT),
    )(*[hbm(a) for a in srcs], *[hbm(a) for a in lands], after)
    return res[:n], res[n:2 * n], res[2 * n:3 * n], res[3 * n:4 * n], res[-1]


def _chip_peers():
    x, y, c = lax.axis_index("x"), lax.axis_index("y"), lax.axis_index("c")
    far = []
    for px, py in ((1 - x, y), (x, 1 - y), (1 - x, 1 - y)):
        far.append(((px, py, c), 4 * px + 2 * py + c, 4 * px + 2 * py + 1 - c))
    return 4 * x + 2 * y + c, ((x, y, 1 - c), 4 * x + 2 * y + 1 - c), far


def _block(land, rows, k):
    return land.at[pl.ds(pl.multiple_of(k * rows, 8), rows)]


def _gather2_first(srcs, lands, after, name):
    n = len(srcs)

    def body(*refs):
        src_refs, land_refs = refs[:n], refs[n:2 * n]
        outs = refs[2 * n + 1:]
        send, recv_sib, recv_far, token = outs[:n], outs[n:2 * n], outs[2 * n:3 * n], outs[-1]
        me, (sib, _), far = _chip_peers()
        for a in range(n):
            rows = src_refs[a].shape[0]
            mine = _block(land_refs[a], rows, me)
            pltpu.make_async_copy(src_refs[a], mine, send[a].at[4]).start()
            pltpu.make_async_remote_copy(src_ref=src_refs[a], dst_ref=mine, send_sem=send[a].at[0],
                                         recv_sem=recv_sib[a].at[0], device_id=sib, device_id_type=MESH).start()
            for j, (peer, _, _) in enumerate(far):
                pltpu.make_async_remote_copy(src_ref=src_refs[a], dst_ref=mine, send_sem=send[a].at[1 + j],
                                             recv_sem=recv_far[a].at[j], device_id=peer, device_id_type=MESH).start()
        token[...] = jnp.zeros_like(token)

    hbm = lambda a: pltpu.with_memory_space_constraint(a, pltpu.HBM)
    res = _call(
        body, name=name,
        in_specs=[HBM] * (2 * n) + [ANY],
        out_specs=[SEM] * (3 * n) + [HBM] * (2 * n) + [pl.BlockSpec(memory_space=pltpu.VMEM)],
        out_shape=[pltpu.SemaphoreType.DMA((5,))] * n + [pltpu.SemaphoreType.DMA((1,))] * n
                  + [pltpu.SemaphoreType.DMA((3,))] * n
                  + [pltpu.HBM(a.shape, a.dtype) for a in list(srcs) + list(lands)]
                  + [jax.ShapeDtypeStruct((8, 128), F32)],
        input_output_aliases={i: 3 * n + i for i in range(2 * n)},
        compiler_params=pltpu.CompilerParams(has_side_effects=EFFECT),
    )(*[hbm(a) for a in srcs], *[hbm(a) for a in lands], after)
    return res[:n], res[n:2 * n], res[2 * n:3 * n], res[3 * n:4 * n], res[4 * n:5 * n], res[-1]


def _gather2_forward(recv_far, srcs, lands, after, name):
    n = len(lands)
    after = list(after) if isinstance(after, (list, tuple)) else [after]

    def body(*refs):
        src_refs, land_refs, far_sems = refs[:n], refs[n:2 * n], refs[2 * n:3 * n]
        outs = refs[3 * n + len(after):]
        send, recv, token = outs[:n], outs[n:2 * n], outs[-1]
        _, (sib, _), far = _chip_peers()
        for a in range(n):
            rows = src_refs[a].shape[0]
            for j, (peer, pidx, _) in enumerate(far):
                got = _block(land_refs[a], rows, pidx)
                pltpu.make_async_remote_copy(src_ref=src_refs[a], dst_ref=got, send_sem=send[a].at[j],
                                             recv_sem=far_sems[a].at[j], device_id=peer,
                                             device_id_type=MESH).wait_recv()
                pltpu.make_async_remote_copy(src_ref=got, dst_ref=got, send_sem=send[a].at[j], recv_sem=recv[a].at[j],
                                             device_id=sib, device_id_type=MESH).start()
        token[...] = jnp.zeros_like(token)

    res = _call(
        body, name=name,
        in_specs=[HBM] * (2 * n) + [SEM] * n + [ANY] * len(after),
        out_specs=[SEM] * (2 * n) + [HBM] * n + [pl.BlockSpec(memory_space=pltpu.VMEM)],
        out_shape=[pltpu.SemaphoreType.DMA((3,))] * (2 * n) + [pltpu.HBM(a.shape, a.dtype) for a in lands]
                  + [jax.ShapeDtypeStruct((8, 128), F32)],
        input_output_aliases={n + i: 2 * n + i for i in range(n)},
        compiler_params=pltpu.CompilerParams(has_side_effects=EFFECT),
    )(*srcs, *lands, *recv_far, *after)
    return res[:n], res[n:2 * n], res[2 * n:3 * n], res[-1]


def _gather2_wait(send, recv_sib, fwd_send, fwd_recv, srcs, lands, after, name):
    n = len(lands)

    def body(*refs):
        src_refs, land_refs = refs[:n], refs[n:2 * n]
        s_refs, rs_refs, fs_refs, fr_refs = (refs[(2 + q) * n:(3 + q) * n] for q in range(4))
        me, (sib, sib_idx), far = _chip_peers()
        for a in range(n):
            rows = src_refs[a].shape[0]
            mine = _block(land_refs[a], rows, me)
            pltpu.make_async_copy(src_refs[a], mine, s_refs[a].at[4]).wait()
            to_sib = pltpu.make_async_remote_copy(src_ref=src_refs[a], dst_ref=_block(land_refs[a], rows, sib_idx),
                                                  send_sem=s_refs[a].at[0], recv_sem=rs_refs[a].at[0], device_id=sib,
                                                  device_id_type=MESH)
            to_sib.wait_send()
            to_sib.wait_recv()
            for j, (peer, pidx, pair_idx) in enumerate(far):
                pltpu.make_async_remote_copy(src_ref=src_refs[a], dst_ref=mine, send_sem=s_refs[a].at[1 + j],
                                             recv_sem=fr_refs[a].at[j], device_id=peer,
                                             device_id_type=MESH).wait_send()
                fwd = pltpu.make_async_remote_copy(src_ref=_block(land_refs[a], rows, pidx),
                                                   dst_ref=_block(land_refs[a], rows, pair_idx),
                                                   send_sem=fs_refs[a].at[j], recv_sem=fr_refs[a].at[j], device_id=sib,
                                                   device_id_type=MESH)
                fwd.wait_send()
                fwd.wait_recv()

    res = _call(
        body, name=name,
        in_specs=[HBM] * (2 * n) + [SEM] * (4 * n) + [ANY],
        out_specs=[HBM] * (2 * n),
        out_shape=[pltpu.HBM(a.shape, a.dtype) for a in list(srcs) + list(lands)],
        input_output_aliases={i: i for i in range(2 * n)},
        compiler_params=pltpu.CompilerParams(has_side_effects=EFFECT),
    )(*srcs, *lands, *send, *recv_sib, *fwd_send, *fwd_recv, after)
    return res[n:]


def _exchange_wait(scatter, send_sems, recv_sems, srcs, lands, after, name):
    n = len(srcs)
    after = list(after) if isinstance(after, (list, tuple)) else [after]

    def body(*refs):
        src_refs, land_refs = refs[:n], refs[n:2 * n]
        send_refs, recv_refs = refs[2 * n:3 * n], refs[3 * n:4 * n]
        for send, recv in _peer_copies(scatter, src_refs, land_refs, send_refs, recv_refs):
            send.wait_send()
            recv.wait_recv()
        if not scatter:
            for own in _own_copies(src_refs, land_refs, send_refs):
                own.wait()

    res = _call(
        body, name=name,
        in_specs=[HBM] * (2 * n) + [SEM] * (2 * n) + [ANY] * len(after),
        out_specs=[HBM] * (2 * n),
        out_shape=[pltpu.HBM(a.shape, a.dtype) for a in list(srcs) + list(lands)],
        input_output_aliases={i: i for i in range(2 * n)},
        compiler_params=pltpu.CompilerParams(has_side_effects=EFFECT),
    )(*srcs, *lands, *send_sems, *recv_sems, *after)
    return res[:n], res[n:]


def _row_tile(rows):
    if rows <= 512:
        return rows
    return max([tr for tr in range(16, 513, 16) if rows % tr == 0] or [rows])


def _sum_parts(own, got, me, name):
    _, rows, w = own.shape
    tr = _row_tile(rows)

    def body(me_ref, a_ref, b_ref, o_ref):
        s = a_ref[...].astype(F32)
        for j in range(N_DEV - 1):
            s = s + b_ref[j].astype(F32)
        o_ref[...] = s

    return _call(
        body, name=name,
        grid_spec=pltpu.PrefetchScalarGridSpec(
            num_scalar_prefetch=1, grid=(rows // tr,),
            in_specs=[pl.BlockSpec((None, tr, w), lambda i, me_ref: (me_ref[0], i, 0)),
                      pl.BlockSpec((N_DEV - 1, tr, w), lambda i, me_ref: (0, i, 0))],
            out_specs=pl.BlockSpec((tr, w), lambda i, me_ref: (i, 0))),
        out_shape=jax.ShapeDtypeStruct((rows, w), F32),
        compiler_params=_params("parallel"),
    )(me, own, got)


def _sum_devices(stacked, name):
    k, rows, w = stacked.shape
    tr = _row_tile(rows)

    def body(a_ref, o_ref):
        s = a_ref[0]
        for j in range(1, k):
            s = s + a_ref[j]
        o_ref[...] = s

    return _call(
        body, name=name, grid=(rows // tr,),
        in_specs=[pl.BlockSpec((k, tr, w), lambda i: (0, i, 0))],
        out_specs=pl.BlockSpec((tr, w), lambda i: (i, 0)),
        out_shape=jax.ShapeDtypeStruct((rows, w), F32),
        compiler_params=_params("parallel"),
    )(stacked)


def _adamw_math(w, g, m, v):
    nm = ADAM_B1 * m + (1.0 - ADAM_B1) * g
    nv = ADAM_B2 * v + (1.0 - ADAM_B2) * (g * g)
    m_hat = nm / (1.0 - ADAM_B1 ** ADAM_STEP)
    v_hat = nv / (1.0 - ADAM_B2 ** ADAM_STEP)
    return -ADAM_LR * (m_hat / (jnp.sqrt(v_hat) + ADAM_EPS) + ADAM_WD * w), nm, nv


def _sum_adamw(owns, gots, me, off, rows, w, m, v, name):
    nl = len(owns)
    d = w.shape[-1]
    assert off % rows == 0 and w.shape == (nl, rows, d)

    def body(me_ref, *refs):
        own_refs, got_refs = refs[:nl], refs[nl:2 * nl]
        w_ref, m_ref, v_ref, g_ref, d_ref, nm_ref, nv_ref = refs[2 * nl:]
        for l in range(nl):
            @pl.when(pl.program_id(0) == l)
            def _(l=l):
                gv = own_refs[l][...].astype(F32)
                for r in range(N_DEV - 1):
                    gv = gv + got_refs[l][r].astype(F32)
                g_ref[...] = gv
                d_ref[...], nm_ref[...], nv_ref[...] = _adamw_math(w_ref[...], gv, m_ref[...], v_ref[...])

    layer = pl.BlockSpec((None, rows, d), lambda i, me_ref: (i, 0, 0))
    own = pl.BlockSpec((None, rows, d), lambda i, me_ref: (me_ref[0], off // rows, 0), pipeline_mode=pl.Buffered(1))
    got = pl.BlockSpec((N_DEV - 1, rows, d), lambda i, me_ref: (0, off // rows, 0), pipeline_mode=pl.Buffered(1))
    return _call(
        body, name=name,
        grid_spec=pltpu.PrefetchScalarGridSpec(
            num_scalar_prefetch=1, grid=(nl,),
            in_specs=[own] * nl + [got] * nl + [layer] * 3, out_specs=[layer] * 4),
        out_shape=[jax.ShapeDtypeStruct((nl, rows, d), F32)] * 4,
        compiler_params=_params("arbitrary"),
    )(me, *owns, *gots, w, m, v)


def _adamw(w, g, m, v, name):
    rows, cols = w.shape
    tr = _row_tile(rows)

    def body(w_ref, g_ref, m_ref, v_ref, d_ref, nm_ref, nv_ref):
        d_ref[...], nm_ref[...], nv_ref[...] = _adamw_math(w_ref[...], g_ref[...], m_ref[...], v_ref[...])

    spec = pl.BlockSpec((tr, cols), lambda i: (i, 0))
    return _call(
        body, name=name, grid=(rows // tr,),
        in_specs=[spec] * 4, out_specs=[spec] * 3,
        out_shape=[jax.ShapeDtypeStruct((rows, cols), F32)] * 3,
        compiler_params=_params("parallel"),
    )(w, g, m, v)


SMALL = ("a_ln_g", "a_ln_b", "a_w_s", "a_b_s", "mix_pre_g", "mix_post_g", "ffn_pre_g", "ffn_post_g")


def _pack_small(parts, d, last_row=None):
    rows = [parts[k].reshape(-1, d) for k in SMALL] + ([] if last_row is None else [last_row])
    flat = jnp.concatenate(rows, axis=0)
    return jnp.pad(flat, ((0, -flat.shape[0] % 8), (0, 0)))


def _unpack_small(flat, like):
    out, r = {}, 0
    for k in SMALL:
        n = like[k].size // flat.shape[1]
        out[k] = flat[r:r + n].reshape(like[k].shape)
        r += n
    return out


def kernel(x, a_w_in, a_ln_g, a_ln_b, a_w_s, a_b_s, a_w_out, b_w_in, b_w_grp, b_scale, b_w_out, mix_pre_g, mix_post_g, ffn_pre_g, ffn_post_g, ffn_w_gate, ffn_w_up, ffn_w_down, loss_target, m_a_w_in, m_a_ln_g, m_a_ln_b, m_a_w_s, m_a_b_s, m_a_w_out, m_b_w_in, m_b_w_grp, m_b_scale, m_b_w_out, m_mix_pre_g, m_mix_post_g, m_ffn_pre_g, m_ffn_post_g, m_ffn_w_gate, m_ffn_w_up, m_ffn_w_down, v_a_w_in, v_a_ln_g, v_a_ln_b, v_a_w_s, v_a_b_s, v_a_w_out, v_b_w_in, v_b_w_grp, v_b_scale, v_b_w_out, v_mix_pre_g, v_mix_post_g, v_ffn_pre_g, v_ffn_post_g, v_ffn_w_gate, v_ffn_w_up, v_ffn_w_down):
    args = dict(locals())
    names = ("a_w_in", "a_ln_g", "a_ln_b", "a_w_s", "a_b_s", "a_w_out", "b_w_in", "b_w_grp", "b_scale", "b_w_out",
             "mix_pre_g", "mix_post_g", "ffn_pre_g", "ffn_post_g", "ffn_w_gate", "ffn_w_up", "ffn_w_down")
    w = {k: args[k] for k in names}
    mom = {k: args["m_" + k] for k in names}
    var = {k: args["v_" + k] for k in names}

    t, d = x.shape[1], x.shape[2]
    ffn_local = ffn_w_gate.shape[2]
    lay = _Layout(d, ffn_local)
    me = 4 * lax.axis_index("x") + 2 * lax.axis_index("y") + lax.axis_index("c")
    me1 = jnp.reshape(me, (1,)).astype(jnp.int32)

    def landing(block):
        return lax.empty((N_DEV * block.shape[0],) + block.shape[1:], block.dtype)

    def shards(i, mixer, zero):
        j = i // 2
        if not mixer:
            parts = [ffn_w_gate[i].T, ffn_w_up[i].T, ffn_w_down[i]]
        elif i % 2 == 0:
            parts = [a_w_in[j].T, a_w_out[j]]
        else:
            parts = [b_w_in[j], b_w_out[j]]
        return [(p + zero).astype(BF16) for p in parts]

    nsub = 2 * DEPTH
    wg = [None] * nsub
    first = shards(0, True, 0.0)
    f_send, f_sib, f_far, first, f_zones, f_token = _gather2_first(
        first, [landing(b) for b in first], jnp.zeros((8, 128), F32), "gather_first_start")
    zero = f_token[0, 0]
    ngrp = len(B_WINDOWS)
    grp_local = b_w_grp.shape[2]
    sdev = b_scale.shape[1]
    side_rows = 2 * ngrp * grp_local
    side = jnp.concatenate(
        [b_w_grp.reshape(side_rows, B_GROUP_DIM),
         jnp.pad(b_scale, ((0, 6), (0, B_GROUP_DIM - sdev)))], axis=0) + zero
    later, where = [side], [slice(0, 1)]
    for k in range(1, nsub):
        new = shards(k // 2, k % 2 == 0, zero)
        where.append(slice(len(later), len(later) + len(new)))
        later += new
    send_sems, recv_sems, later, zones, token = _exchange_start(
        False, later, [landing(b) for b in later], f_token, "gather_start")
    turned = ("ffn_w_gate", "ffn_w_up")
    turn = lambda a: jnp.swapaxes(a, 1, 2)
    state = {k: tuple(turn(a[k]) for a in (w, mom, var)) for k in turned}
    state["small"] = tuple(_pack_small(a, d) for a in (w, mom, var))
    ready = [a for group in state.values() for a in group]
    fwd_send, fwd_recv, f_zones, fwd_token = _gather2_forward(f_far, first, f_zones, [token] + ready,
                                                              "gather_first_forward")
    wg[0] = _gather2_wait(f_send, f_sib, fwd_send, fwd_recv, first, f_zones, fwd_token, "gather_first_wait")

    def gathered(k, after):
        s = where[k]
        _, got = _exchange_wait(False, send_sems[s], recv_sems[s], later[s], zones[s], after, f"gather_wait_{k}")
        return got

    row = lambda a: a.reshape(1, -1)
    bst = jnp.transpose(a_b_s, (0, 2, 1))

    tm = 256 if t % 256 == 0 else CHUNK
    tm_abwd = tm
    tm_afwd = 512 if t % 512 == 0 else tm
    tm_b = 512 if t % 512 == 0 else tm
    tm_f = tm

    saved = []
    h = x[0]
    wgrp_full = scale_full = None
    for i in range(DEPTH):
        j = i // 2
        gpre = row(mix_pre_g[i])
        if i > 0:
            wg[2 * i] = gathered(2 * i, h)
        if i % 2 == 0:
            x1, h1, gp, u, vh, rs, gated, m = _a_fwd(h, gpre, wg[2 * i], lay, j, row(a_ln_g[j]), row(a_ln_b[j]),
                                                     a_w_s[j], bst[j], row(mix_post_g[i]), tm_afwd, f"a_fwd_{j}")
            mix = dict(h1=h1, gp=gp, u=u, vh=vh, rs=rs, gated=gated, m=m)
        else:
            if wgrp_full is None:
                side_g = gathered(0, h)[0].reshape(N_DEV, side_rows + 8, B_GROUP_DIM)
                wgrp_full = (side_g[:, :side_rows].reshape(N_DEV, 2, ngrp, grp_local, B_GROUP_DIM)
                             .transpose(1, 2, 0, 3, 4).reshape(2, ngrp, B_GROUP_DIM, B_GROUP_DIM).astype(BF16))
                scale_full = (side_g[:, side_rows:side_rows + 2, :sdev].transpose(1, 0, 2)
                              .reshape(2, 1, N_DEV * sdev))
            x1, h1, pooled, mixed, m = _b_fwd(h, gpre, wg[2 * i], lay, j, wgrp_full[j], scale_full[j],
                                              row(mix_post_g[i]), tm_b, f"b_fwd_{j}")
            mix = dict(h1=h1, pooled=pooled, mixed=mixed, m=m)
        wg[2 * i + 1] = gathered(2 * i + 1, x1)
        x2, h2, acts, f, *loss_acc = _f_fwd(x1, row(ffn_pre_g[i]), wg[2 * i + 1], lay, i, row(ffn_post_g[i]), tm_f,
                                               f"f_fwd_{i}", loss_target[0] if i == DEPTH - 1 else None)
        saved.append(dict(x=h, x1=x1, mix=mix, h2=h2, acts=acts, f=f))
        h = x2
    dy, (loss_acc,) = h, loss_acc

    small_g = {k: [None] * w[k].shape[0] for k in SMALL}
    dgrp, dscale = [None, None], [None, None]
    pending = [None] * nsub
    token = jnp.zeros((8, 128), F32)

    def scatter(ks, gbufs):
        gots = [pltpu.with_memory_space_constraint(lax.empty((N_DEV - 1,) + g.shape[1:], g.dtype), pltpu.HBM)
                for g in gbufs]
        ss, rs, src, zone, tok = _exchange_start(True, gbufs, gots, token, f"scatter_start_{ks[0]}")
        for n, k in enumerate(ks):
            pending[k] = (ss[n:n + 1], rs[n:n + 1], src[n:n + 1], zone[n:n + 1])
        return tok

    def small_exchanges():
        side_grad = jnp.concatenate(
            [jnp.stack(dgrp).reshape(2, ngrp, N_DEV, grp_local, B_GROUP_DIM).transpose(2, 0, 1, 3, 4)
             .reshape(N_DEV, side_rows, B_GROUP_DIM),
             jnp.pad(jnp.stack(dscale).reshape(2, N_DEV, sdev).transpose(1, 0, 2),
                     ((0, 0), (0, 6), (0, B_GROUP_DIM - sdev)))], axis=1)
        small_part = _pack_small({k: jnp.stack(small_g[k]) for k in SMALL}, d,
                                 jnp.broadcast_to(loss_acc[:1, :1], (1, d)))
        got = pltpu.with_memory_space_constraint(lax.empty((N_DEV - 1,) + side_grad.shape[1:], F32), pltpu.HBM)
        side_x = _exchange_start(True, [side_grad], [got], token, "side_scatter_start")
        small_x = _exchange_start(False, [small_part], [landing(small_part)], side_x[4], "small_gather_start")
        return side_x[:4], small_x[:4], small_x[4]

    for i in reversed(range(DEPTH)):
        sv = saved[i]
        j = i // 2
        wf, wm = wg[2 * i + 1], wg[2 * i]
        dx1, df, dacts, dgpost, dgpre = _f_bwd(dy, sv["f"], sv["x1"], sv["acts"], row(ffn_pre_g[i]), wf, lay, i,
                                               row(ffn_post_g[i]), token, tm_f, f"f_bwd_{i}")
        small_g["ffn_post_g"][i], small_g["ffn_pre_g"][i] = dgpost[0], dgpre[0]
        gbuf_f = _grad_ffn(dacts, sv["h2"], df, lay.ffn_rows, f"g_ffn_{i}")
        if i == 0:
            token = scatter([2 * i + 1], [gbuf_f])
        mix = sv["mix"]
        gpost = row(mix_post_g[i])
        if i % 2 == 0:
            dx, dm, dz, dgpost, dgpre, dlng, dlnb, dws, dbt = _a_bwd(
                dx1, mix["m"], sv["x"], mix["gp"], mix["u"], mix["vh"], mix["rs"], row(mix_pre_g[i]), wm, lay, j,
                row(a_ln_g[j]), row(a_ln_b[j]), a_w_s[j], bst[j], gpost, token, tm_abwd, f"a_bwd_{j}")
            small_g["a_ln_g"][j], small_g["a_ln_b"][j] = dlng[0], dlnb[0]
            small_g["a_w_s"][j], small_g["a_b_s"][j] = dws, dbt[:, :A_GROUPS].T
            small_g["mix_post_g"][i], small_g["mix_pre_g"][i] = dgpost[0], dgpre[0]
            order = None
            if i == 0:
                side_x, small_x, order = small_exchanges()
            gbuf = _grad_into(lay.a_total, dz, mix["h1"], lay.a_in[j], lay.a_in_rows, f"g_a_in_{j}", after=order)
            gbuf = _grad_into(gbuf, mix["gated"], dm, lay.a_out[j], lay.a_out_rows, f"g_a_out_{j}")
        else:
            dx, dm, draw, dp, dgpost, dgpre, dsc = _b_bwd(
                dx1, mix["m"], sv["x"], mix["pooled"], row(mix_pre_g[i]), wm, lay, j, wgrp_full[j], scale_full[j],
                gpost, token, tm_b, f"b_bwd_{j}")
            dscale[j] = dsc[0]
            gbuf, dgrp[j] = _grad_pool_mixer(mix["h1"], dp, mix["mixed"], dm, mix["pooled"], draw, lay.b_rows,
                                             f"g_b_{j}")
            small_g["mix_post_g"][i], small_g["mix_pre_g"][i] = dgpost[0], dgpre[0]
        token = scatter([2 * i], [gbuf]) if i == 0 else scatter([2 * i + 1, 2 * i], [gbuf_f, gbuf])
        dy = dx
    grad_x = dy[None]

    g_sub = [None] * nsub

    parts = [None] * nsub

    def arrived(k, after):
        ss, rs, src, zone = pending[k]
        (own,), (got,) = _exchange_wait(True, ss, rs, src, zone, after, f"scatter_wait_{k}")
        parts[k] = (own, got)
        if k % 4 == 0:
            g_sub[k] = _sum_parts(own, got, me1, f"sum_grads_{k}")

    def fused_update(k, subs, off, rows):
        back = turn if k in turned else (lambda a: a)
        wk, mk, vk = state[k] if k in turned else (w[k], mom[k], var[k])
        out = _sum_adamw([parts[s][0] for s in subs], [parts[s][1] for s in subs], me1, off, rows, wk, mk, vk,
                         f"update_{k}")
        grads[k], delta[k], new_m[k], new_v[k] = (back(a) for a in out)

    def rows_of(k, off, n):
        return g_sub[k][off:off + n]

    grads, delta, new_m, new_v = {}, {}, {}, {}

    def update(k):
        shape = w[k].shape
        two = lambda a: a.reshape(-1, shape[-1])
        dl, nm, nv = _adamw(two(w[k]), two(grads[k]), two(mom[k]), two(var[k]), f"adamw_{k}")
        delta[k], new_m[k], new_v[k] = dl.reshape(shape), nm.reshape(shape), nv.reshape(shape)

    for k in range(1, nsub):
        arrived(k, token)
    ffn_subs = [2 * l + 1 for l in range(DEPTH)]
    fused_update("ffn_w_gate", ffn_subs, lay.gate[0], ffn_local)
    fused_update("ffn_w_up", ffn_subs, lay.up[0], ffn_local)
    fused_update("ffn_w_down", ffn_subs, lay.down[0], ffn_local)
    fused_update("b_w_in", [2, 6], lay.b_in[0], lay.b_rows)
    fused_update("b_w_out", [2, 6], lay.b_out[0], lay.b_rows)
    early = ("ffn_w_gate", "ffn_w_up", "ffn_w_down", "b_w_in", "b_w_out")

    (side_own,), (side_got,) = _exchange_wait(True, *side_x, [delta[k] for k in early], "side_scatter_wait")
    g_side = _sum_parts(side_own, side_got, me1, "sum_side")
    grads["b_w_grp"] = g_side[:side_rows].reshape(b_w_grp.shape)
    grads["b_scale"] = g_side[side_rows:side_rows + 2, :sdev]
    update("b_w_grp")
    update("b_scale")
    _, (small_all,) = _exchange_wait(False, *small_x, [delta["b_w_grp"], delta["b_scale"]], "small_gather_wait")
    small_sum = _sum_devices(small_all.reshape(N_DEV, -1, d), "sum_small")
    g_small = _unpack_small(small_sum, w)
    loss = small_sum[sum(w[k].size for k in SMALL) // d, 0]
    grads.update(g_small)
    dl, nm, nv = _adamw(state["small"][0], _pack_small(g_small, d), state["small"][1], state["small"][2],
                        "adamw_small")
    delta.update(_unpack_small(dl, w))
    new_m.update(_unpack_small(nm, w))
    new_v.update(_unpack_small(nv, w))

    arrived(0, dl)
    grads["a_w_in"] = jnp.stack([rows_of(4 * j, lay.a_in[j], lay.a_in_rows).T for j in range(2)])
    update("a_w_in")
    fused_update("a_w_out", [0, 4], lay.a_out[0], lay.a_out_rows)

    return (loss, grad_x, *[grads[k] for k in names], *[delta[k] for k in names], *[new_m[k] for k in names],
            *[new_v[k] for k in names])
```
